```python
import math
import jax, jax.numpy as jnp
from jax import lax
import numpy as np

D_MODEL = 2048
BATCH = 2
SEQ = 4096
DEPTH = 1
DEC_BATCH = 128
DEC_SEQ = 1
PAST_LEN = 16384
PAGE_SIZE = 128

HEAD_DIM = 64
ATT_WIDTH = D_MODEL // 2
ATT_HEADS = ATT_WIDTH // HEAD_DIM
ATT_KV_HEADS = ATT_HEADS // 4
ATT_GROUP = ATT_HEADS // ATT_KV_HEADS
KV_WIDTH = ATT_KV_HEADS * HEAD_DIM
ATT_PROJ = ATT_WIDTH + 2 * KV_WIDTH
WINDOW = 128
ATT_BLOCK = WINDOW
ATT_SCALE = HEAD_DIM ** -0.5
ROPE_THETA = 500000.0
ROT_DIM = HEAD_DIM // 4

RWKV_WIDTH = D_MODEL - ATT_WIDTH
RWKV_HEAD_DIM = 64
RWKV_HEADS = RWKV_WIDTH // RWKV_HEAD_DIM
DECAY_LORA = max(32, int(round(1.8 * RWKV_WIDTH ** 0.5 / 32)) * 32)
AAA_LORA = max(32, int(round(1.8 * RWKV_WIDTH ** 0.5 / 32)) * 32)
GATE_LORA = max(32, int(round(0.6 * RWKV_WIDTH ** 0.8 / 32)) * 32)
RWKV_PROJ = 3 * RWKV_WIDTH + DECAY_LORA + AAA_LORA + GATE_LORA
IN_PROJ = ATT_PROJ + RWKV_PROJ
MIX_WIDTH = ATT_WIDTH + RWKV_WIDTH

N_MEM = 256
XATT_HEADS = 4
XATT_HEAD_DIM = 128
XATT_WIDTH = XATT_HEADS * XATT_HEAD_DIM

N_EXPERT_GROUPS = 8
EXPERTS_PER_GROUP = 8
N_EXPERTS = N_EXPERT_GROUPS * EXPERTS_PER_GROUP
TOP_K = 2
EXPERT_FF = D_MODEL // 4
MOE_BLOCK = 128

RMS_EPS = 1e-6
GN_EPS = 64e-5

kernel_name = 'hymba_swa_rwkv7_hmoe_memxattn_step'


def rmsnorm(x, w):
    xf = x.astype(jnp.float32)
    y = xf * lax.rsqrt(jnp.mean(xf * xf, axis=-1, keepdims=True) + RMS_EPS)
    return (y * w.astype(jnp.float32)).astype(x.dtype)


def rope_partial(x, pos):
    half = ROT_DIM // 2
    inv = ROPE_THETA ** (-jnp.arange(half, dtype=jnp.float32) * 2.0 / ROT_DIM)
    ang = pos.astype(jnp.float32)[:, None] * inv[None, :]
    cos = jnp.cos(ang)[:, None, :]
    sin = jnp.sin(ang)[:, None, :]
    xf = x[..., :ROT_DIM].astype(jnp.float32)
    x1, x2 = xf[..., :half], xf[..., half:]
    rot = jnp.concatenate([x1 * cos - x2 * sin, x2 * cos + x1 * sin], axis=-1).astype(x.dtype)
    return jnp.concatenate([rot, x[..., ROT_DIM:]], axis=-1)


def mixer_in(x, lw):
    p = rmsnorm(x, lw['ln1_w']) @ lw['w_in']
    return p[..., :ATT_PROJ], p[..., ATT_PROJ:]


def mixer_out(x, att_o, rw_o, lw):
    return x + jnp.concatenate([att_o, rw_o.astype(att_o.dtype)], axis=-1) @ lw['w_out']


def attn_qkv(pa, pos, lw):
    B, T, _ = pa.shape
    q = pa[..., :ATT_WIDTH].reshape(B, T, ATT_HEADS, HEAD_DIM)
    k = pa[..., ATT_WIDTH:ATT_WIDTH + KV_WIDTH].reshape(B, T, ATT_KV_HEADS, HEAD_DIM)
    v = pa[..., ATT_WIDTH + KV_WIDTH:].reshape(B, T, ATT_KV_HEADS, HEAD_DIM)
    q = rope_partial(rmsnorm(q, lw['q_norm_w']), pos)
    k = rope_partial(rmsnorm(k, lw['k_norm_w']), pos)
    return q, k, v


def sink_softmax(s, sinks):
    sk = sinks.astype(jnp.float32).reshape(ATT_KV_HEADS, ATT_GROUP)[:, :, None, None]
    m = jnp.maximum(jnp.max(s, axis=-1, keepdims=True), sk)
    e = jnp.exp(s - m)
    return e / (jnp.sum(e, axis=-1, keepdims=True) + jnp.exp(sk - m))


def swa_banded(q, k, v, sinks):
    B, T, _, _ = q.shape
    nb = T // ATT_BLOCK
    qb = q.reshape(B, nb, ATT_BLOCK, ATT_KV_HEADS, ATT_GROUP, HEAD_DIM)
    kb = k.reshape(B, nb, ATT_BLOCK, ATT_KV_HEADS, HEAD_DIM)
    vb = v.reshape(B, nb, ATT_BLOCK, ATT_KV_HEADS, HEAD_DIM)
    pad = ((0, 0), (1, 0), (0, 0), (0, 0), (0, 0))
    kc = jnp.concatenate([jnp.pad(kb, pad)[:, :-1], kb], axis=2)
    vc = jnp.concatenate([jnp.pad(vb, pad)[:, :-1], vb], axis=2)
    s = jnp.einsum('bnqkgd,bnskd->bnkgqs', qb, kc, preferred_element_type=jnp.float32) * ATT_SCALE
    qi = jnp.arange(ATT_BLOCK)[:, None] + ATT_BLOCK
    si = jnp.arange(2 * ATT_BLOCK)[None, :]
    rel = qi - si
    band = (rel >= 0) & (rel <= WINDOW)
    blk = jnp.arange(nb)[:, None, None]
    valid = band[None] & ((blk > 0) | (si[None] >= ATT_BLOCK))
    s = jnp.where(valid[None, :, None, None], s, -jnp.inf)
    p = sink_softmax(s, sinks)
    o = jnp.einsum('bnkgqs,bnskd->bnqkgd', p.astype(vc.dtype), vc)
    return o.reshape(B, T, ATT_WIDTH)


def swa_cached(q, k_new, v_new, buf_k, buf_v, sinks):
    B, T = q.shape[:2]
    nbuf = buf_k.shape[1]
    kc = jnp.concatenate([buf_k.astype(k_new.dtype), k_new], axis=1)
    vc = jnp.concatenate([buf_v.astype(v_new.dtype), v_new], axis=1)
    q_pos = PAST_LEN + jnp.arange(T)
    k_pos = jnp.concatenate([PAST_LEN - nbuf + jnp.arange(nbuf), q_pos])
    rel = q_pos[:, None] - k_pos[None, :]
    valid = (rel >= 0) & (rel <= WINDOW)
    qg = q.reshape(B, T, ATT_KV_HEADS, ATT_GROUP, HEAD_DIM)
    s = jnp.einsum('bqkgd,bskd->bkgqs', qg, kc, preferred_element_type=jnp.float32) * ATT_SCALE
    s = jnp.where(valid, s, -jnp.inf)
    p = sink_softmax(s, sinks)
    o = jnp.einsum('bkgqs,bskd->bqkgd', p.astype(vc.dtype), vc).reshape(B, T, ATT_WIDTH)
    return o, kc[:, -nbuf:], vc[:, -nbuf:]


def rwkv7_mix(pr, prev_row, s0, lw):
    B, T, _ = pr.shape
    H, N, C = RWKV_HEADS, RWKV_HEAD_DIM, RWKV_WIDTH
    f32 = jnp.float32
    prev = jnp.concatenate([prev_row[:, None, :].astype(pr.dtype), pr[:, :-1]], axis=1)
    xm = (pr + (prev - pr) * lw['rw_mu']).astype(f32)
    xr, xk, xv = xm[..., :C], xm[..., C:2 * C], xm[..., 2 * C:3 * C]
    o = 3 * C
    xw = xm[..., o:o + DECAY_LORA]
    o += DECAY_LORA
    xa = xm[..., o:o + AAA_LORA]
    o += AAA_LORA
    xg = xm[..., o:]
    w_log = -jax.nn.softplus(-(lw['rw_w0'].astype(f32) + jnp.tanh(xw) @ lw['rw_w2'].astype(f32))) - 0.5
    decay = jnp.exp(-jnp.exp(w_log))
    a = jax.nn.sigmoid(lw['rw_a0'].astype(f32) + xa @ lw['rw_a2'].astype(f32))
    g = jax.nn.sigmoid(xg) @ lw['rw_g2'].astype(f32)
    r, k, v, decay, a = [t.reshape(B, T, H, N) for t in (xr, xk, xv, decay, a)]
    kk = k * lw['rw_k_k'].astype(f32).reshape(H, N)
    kk = kk / jnp.maximum(jnp.sqrt(jnp.sum(kk * kk, axis=-1, keepdims=True)), 1e-12)
    k = k * (1.0 + (a - 1.0) * lw['rw_k_a'].astype(f32).reshape(H, N))

    def step(S, inp):
        r_t, w_t, k_t, v_t, kk_t, b_t = inp
        sa = jnp.einsum('bhvk,bhk->bhv', S, -kk_t)
        S = S * w_t[:, :, None, :] + sa[..., None] * b_t[:, :, None, :] + v_t[..., None] * k_t[:, :, None, :]
        return S, jnp.einsum('bhvk,bhk->bhv', S, r_t)

    xs = tuple(jnp.swapaxes(t, 0, 1) for t in (r, decay, k, v, kk, kk * a))
    S, y = lax.scan(step, s0.astype(f32), xs)
    y = jnp.swapaxes(y, 0, 1)
    mean = jnp.mean(y, axis=-1, keepdims=True)
    var = jnp.mean((y - mean) ** 2, axis=-1, keepdims=True)
    yn = (y - mean) * lax.rsqrt(var + GN_EPS) * lw['rw_ln_w'].astype(f32).reshape(H, N) + lw['rw_ln_b'].astype(f32).reshape(H, N)
    bonus = jnp.sum(r * k * lw['rw_r_k'].astype(f32).reshape(H, N), axis=-1, keepdims=True) * v
    out = ((yn + bonus).reshape(B, T, C) * g).astype(pr.dtype)
    return out, S.astype(s0.dtype), pr[:, -1]


def mem_kv(mem, lw):
    B, M, _ = mem.shape
    kv = rmsnorm(mem, lw['mem_norm_w']) @ lw['xkv_w']
    k = rmsnorm(kv[..., :XATT_WIDTH].reshape(B, M, XATT_HEADS, XATT_HEAD_DIM), lw['xk_norm_w'])
    v = kv[..., XATT_WIDTH:].reshape(B, M, XATT_HEADS, XATT_HEAD_DIM)
    return k, v


def cross_attend(x, mem_k, mem_v, lw):
    B, T, _ = x.shape
    h = rmsnorm(x, lw['ln2_w'])
    q = rmsnorm((h @ lw['xq_w']).reshape(B, T, XATT_HEADS, XATT_HEAD_DIM), lw['xq_norm_w'])
    s = jnp.einsum('bqhd,bmhd->bhqm', q, mem_k.astype(q.dtype), preferred_element_type=jnp.float32) / math.sqrt(XATT_HEAD_DIM)
    p = jax.nn.softmax(s, axis=-1)
    o = jnp.einsum('bhqm,bmhd->bqhd', p.astype(q.dtype), mem_v.astype(q.dtype)).reshape(B, T, XATT_WIDTH)
    return x + o @ lw['xo_w']


def expert_dispatch(u, e_idx, gates, w_gate, w_up, w_down):
    M, D = u.shape
    A = M * TOP_K
    e_flat = e_idx.reshape(A)
    w_flat = gates.reshape(A)
    tok_flat = jnp.arange(A, dtype=jnp.int32) // TOP_K
    order = jnp.argsort(e_flat)
    e_s, tok_s, w_s = e_flat[order], tok_flat[order], w_flat[order]
    counts = jnp.bincount(e_flat, length=N_EXPERTS)
    pad_counts = (counts + MOE_BLOCK - 1) // MOE_BLOCK * MOE_BLOCK
    starts = jnp.cumsum(counts) - counts
    pad_ends = jnp.cumsum(pad_counts)
    pad_starts = pad_ends - pad_counts
    dest = pad_starts[e_s] + jnp.arange(A, dtype=jnp.int32) - starts[e_s]
    n_blocks = -(-A // MOE_BLOCK) + N_EXPERTS
    P = n_blocks * MOE_BLOCK
    row_tok = jnp.full((P,), M, jnp.int32).at[dest].set(tok_s)
    row_w = jnp.zeros((P,), w_flat.dtype).at[dest].set(w_s)
    block_exp = jnp.minimum(jnp.searchsorted(pad_ends, jnp.arange(n_blocks) * MOE_BLOCK, side='right'), N_EXPERTS - 1)
    u_pad = jnp.concatenate([u, jnp.zeros((1, D), u.dtype)], axis=0)
    xb = u_pad[row_tok].reshape(n_blocks, MOE_BLOCK, D)

    def run_block(args):
        xblk, e = args
        return (jax.nn.silu(xblk @ w_gate[e]) * (xblk @ w_up[e])) @ w_down[e]

    yb = lax.map(run_block, (xb, block_exp)).reshape(P, D)
    out = jnp.zeros((M + 1, D), yb.dtype).at[row_tok].add(yb * row_w[:, None].astype(yb.dtype))
    return out[:M]


def hmoe_block(x, lw):
    B, T, D = x.shape
    M = B * T
    f32 = jnp.float32
    u = rmsnorm(x, lw['ln3_w']).reshape(M, D)
    g_logit = (u @ lw['router_group_w']).astype(f32) + lw['router_group_b'].astype(f32)
    g_idx = jnp.argmax(g_logit, axis=-1).astype(jnp.int32)
    g_gate = jnp.take_along_axis(jax.nn.softmax(g_logit, axis=-1), g_idx[:, None], axis=-1)
    e_logit = ((u @ lw['router_expert_w']).astype(f32) + lw['router_expert_b'].astype(f32)).reshape(M, N_EXPERT_GROUPS, EXPERTS_PER_GROUP)
    e_logit = jnp.take_along_axis(e_logit, g_idx[:, None, None], axis=1)[:, 0]
    top_v, top_i = lax.top_k(e_logit, TOP_K)
    gates = jax.nn.softmax(top_v, axis=-1) * g_gate
    e_idx = g_idx[:, None] * EXPERTS_PER_GROUP + top_i.astype(jnp.int32)
    y = expert_dispatch(u, e_idx, gates, lw['exp_w_gate'], lw['exp_w_up'], lw['exp_w_down'])
    return x + y.reshape(B, T, D).astype(x.dtype)


def setup_inputs(seed: int = 0) -> dict:
    key = jax.random.key(seed)
    ks = iter(list(jax.random.split(key, 64)))
    f32 = jnp.float32
    L = DEPTH
    win_buf = min(WINDOW, PAST_LEN)

    def nrm(shape, scale=1.0):
        return jax.random.normal(next(ks), shape, f32) * scale

    def gain(shape, center=1.0):
        return center + 0.02 * jax.random.normal(next(ks), shape, f32)

    return {
        'x_prompt': nrm((BATCH, SEQ, D_MODEL)),
        'x_sample': nrm((DEC_BATCH, DEC_SEQ, D_MODEL)),
        'cache_win_k': nrm((L, DEC_BATCH, win_buf, ATT_KV_HEADS, HEAD_DIM)),
        'cache_win_v': nrm((L, DEC_BATCH, win_buf, ATT_KV_HEADS, HEAD_DIM)),
        'state_wkv': nrm((L, DEC_BATCH, RWKV_HEADS, RWKV_HEAD_DIM, RWKV_HEAD_DIM), 0.5),
        'state_shift': nrm((L, DEC_BATCH, RWKV_PROJ)),
        'cache_mem_k': nrm((L, DEC_BATCH, N_MEM, XATT_HEADS, XATT_HEAD_DIM)),
        'cache_mem_v': nrm((L, DEC_BATCH, N_MEM, XATT_HEADS, XATT_HEAD_DIM)),
        'mem_prompt': nrm((BATCH, N_MEM, D_MODEL)),
        'ln1_w': gain((L, D_MODEL)),
        'w_in': nrm((L, D_MODEL, IN_PROJ), D_MODEL ** -0.5),
        'q_norm_w': gain((L, HEAD_DIM)),
        'k_norm_w': gain((L, HEAD_DIM)),
        'attn_sinks': nrm((L, ATT_HEADS), 0.5),
        'rw_mu': jax.random.uniform(next(ks), (L, RWKV_PROJ), f32),
        'rw_w0': jax.random.uniform(next(ks), (L, RWKV_WIDTH), f32, -6.0, -1.0),
        'rw_w2': nrm((L, DECAY_LORA, RWKV_WIDTH), 0.1),
        'rw_a0': nrm((L, RWKV_WIDTH), 0.1),
        'rw_a2': nrm((L, AAA_LORA, RWKV_WIDTH), 0.1),
        'rw_g2': nrm((L, GATE_LORA, RWKV_WIDTH), GATE_LORA ** -0.5),
        'rw_k_k': gain((L, RWKV_WIDTH), 0.85),
        'rw_k_a': gain((L, RWKV_WIDTH)),
        'rw_r_k': nrm((L, RWKV_WIDTH), 0.1),
        'rw_ln_w': gain((L, RWKV_WIDTH)),
        'rw_ln_b': nrm((L, RWKV_WIDTH), 0.01),
        'w_out': nrm((L, MIX_WIDTH, D_MODEL), MIX_WIDTH ** -0.5),
        'ln2_w': gain((L, D_MODEL)),
        'mem_norm_w': gain((L, D_MODEL)),
        'xq_w': nrm((L, D_MODEL, XATT_WIDTH), D_MODEL ** -0.5),
        'xkv_w': nrm((L, D_MODEL, 2 * XATT_WIDTH), D_MODEL ** -0.5),
        'xq_norm_w': gain((L, XATT_HEAD_DIM)),
        'xk_norm_w': gain((L, XATT_HEAD_DIM)),
        'xo_w': nrm((L, XATT_WIDTH, D_MODEL), XATT_WIDTH ** -0.5),
        'ln3_w': gain((L, D_MODEL)),
        'router_group_w': nrm((L, D_MODEL, N_EXPERT_GROUPS), D_MODEL ** -0.5),
        'router_group_b': nrm((L, N_EXPERT_GROUPS), 0.01),
        'router_expert_w': nrm((L, D_MODEL, N_EXPERTS), D_MODEL ** -0.5),
        'router_expert_b': nrm((L, N_EXPERTS), 0.01),
        'exp_w_gate': nrm((L, N_EXPERTS, D_MODEL, EXPERT_FF), D_MODEL ** -0.5),
        'exp_w_up': nrm((L, N_EXPERTS, D_MODEL, EXPERT_FF), D_MODEL ** -0.5),
        'exp_w_down': nrm((L, N_EXPERTS, EXPERT_FF, D_MODEL), EXPERT_FF ** -0.5),
    }


def reference(x_prompt, x_sample, cache_win_k, cache_win_v, state_wkv, state_shift, cache_mem_k, cache_mem_v,
              mem_prompt, ln1_w, w_in, q_norm_w, k_norm_w, attn_sinks, rw_mu, rw_w0, rw_w2, rw_a0, rw_a2, rw_g2,
              rw_k_k, rw_k_a, rw_r_k, rw_ln_w, rw_ln_b, w_out, ln2_w, mem_norm_w, xq_w, xkv_w, xq_norm_w,
              xk_norm_w, xo_w, ln3_w, router_group_w, router_group_b, router_expert_w, router_expert_b,
              exp_w_gate, exp_w_up, exp_w_down):
    pos_p = jnp.arange(SEQ, dtype=jnp.int32)
    pos_s = PAST_LEN + jnp.arange(DEC_SEQ, dtype=jnp.int32)
    win_p = min(WINDOW, SEQ)
    hp, hs = x_prompt, x_sample
    p_wk, p_wv, p_wkv, p_sh, p_mk, p_mv = [], [], [], [], [], []
    s_wk, s_wv, s_wkv, s_sh = [], [], [], []
    for l in range(DEPTH):
        lw = {
            'ln1_w': ln1_w[l], 'w_in': w_in[l], 'q_norm_w': q_norm_w[l], 'k_norm_w': k_norm_w[l],
            'attn_sinks': attn_sinks[l], 'rw_mu': rw_mu[l], 'rw_w0': rw_w0[l], 'rw_w2': rw_w2[l],
            'rw_a0': rw_a0[l], 'rw_a2': rw_a2[l], 'rw_g2': rw_g2[l], 'rw_k_k': rw_k_k[l], 'rw_k_a': rw_k_a[l],
            'rw_r_k': rw_r_k[l], 'rw_ln_w': rw_ln_w[l], 'rw_ln_b': rw_ln_b[l], 'w_out': w_out[l],
            'ln2_w': ln2_w[l], 'mem_norm_w': mem_norm_w[l], 'xq_w': xq_w[l], 'xkv_w': xkv_w[l],
            'xq_norm_w': xq_norm_w[l], 'xk_norm_w': xk_norm_w[l], 'xo_w': xo_w[l], 'ln3_w': ln3_w[l],
            'router_group_w': router_group_w[l], 'router_group_b': router_group_b[l],
            'router_expert_w': router_expert_w[l], 'router_expert_b': router_expert_b[l],
            'exp_w_gate': exp_w_gate[l], 'exp_w_up': exp_w_up[l], 'exp_w_down': exp_w_down[l],
        }
        pa, pr = mixer_in(hp, lw)
        q, k, v = attn_qkv(pa, pos_p, lw)
        att = swa_banded(q, k, v, lw['attn_sinks'])
        rw, wkv, shift = rwkv7_mix(pr, jnp.zeros((BATCH, RWKV_PROJ), pr.dtype),
                                   jnp.zeros((BATCH, RWKV_HEADS, RWKV_HEAD_DIM, RWKV_HEAD_DIM), jnp.float32), lw)
        hp = mixer_out(hp, att, rw, lw)
        mk, mv = mem_kv(mem_prompt, lw)
        hp = hmoe_block(cross_attend(hp, mk, mv, lw), lw)
        p_wk.append(k[:, SEQ - win_p:])
        p_wv.append(v[:, SEQ - win_p:])
        p_wkv.append(wkv)
        p_sh.append(shift)
        p_mk.append(mk)
        p_mv.append(mv)
        sa, sr = mixer_in(hs, lw)
        q, k, v = attn_qkv(sa, pos_s, lw)
        att, nk, nv = swa_cached(q, k, v, cache_win_k[l], cache_win_v[l], lw['attn_sinks'])
        rw, wkv, shift = rwkv7_mix(sr, state_shift[l], state_wkv[l], lw)
        hs = mixer_out(hs, att, rw, lw)
        hs = hmoe_block(cross_attend(hs, cache_mem_k[l], cache_mem_v[l], lw), lw)
        s_wk.append(nk)
        s_wv.append(nv)
        s_wkv.append(wkv)
        s_sh.append(shift)
    return (hp, hs, jnp.stack(p_wk), jnp.stack(p_wv), jnp.stack(p_wkv), jnp.stack(p_sh), jnp.stack(p_mk),
            jnp.stack(p_mv), jnp.stack(s_wk), jnp.stack(s_wv), jnp.stack(s_wkv), jnp.stack(s_sh))
```

```python
import functools
import math

import jax
import jax.numpy as jnp
from jax import lax
from jax.experimental import pallas as pl
from jax.experimental.pallas import tpu as pltpu

F32 = jnp.float32
BF16 = jnp.bfloat16

D_MODEL = 2048
HEAD_DIM = 64
ATT_HEADS = 16
ATT_KV_HEADS = 4
ATT_GROUP = ATT_HEADS // ATT_KV_HEADS
ATT_WIDTH = ATT_HEADS * HEAD_DIM
KV_WIDTH = ATT_KV_HEADS * HEAD_DIM
ATT_PROJ = ATT_WIDTH + 2 * KV_WIDTH
WINDOW = 128
ATT_SCALE = HEAD_DIM ** -0.5
ROPE_THETA = 500000.0
ROT_DIM = HEAD_DIM // 4
PAST_LEN = 16384

RWKV_WIDTH = 1024
RWKV_HEAD_DIM = 64
RWKV_HEADS = 16
DECAY_LORA = 64
AAA_LORA = 64
GATE_LORA = 160
RWKV_PROJ = 3 * RWKV_WIDTH + DECAY_LORA + AAA_LORA + GATE_LORA
RWKV_PROJ_PAD = 3456

N_MEM = 256
XATT_HEADS = 4
XATT_HEAD_DIM = 128
XATT_WIDTH = XATT_HEADS * XATT_HEAD_DIM

N_EXPERT_GROUPS = 8
EXPERTS_PER_GROUP = 8
N_EXPERTS = 64
TOP_K = 2
EXPERT_FF = D_MODEL // 4
MOE_BLOCK = 128

RMS_EPS = 1e-6
GN_EPS = 64e-5

LANES = 128
CHUNK = 64
VMEM_LIMIT = 56 * 1024 * 1024


def _cp(sem, vmem=VMEM_LIMIT):
    return pltpu.CompilerParams(dimension_semantics=sem, vmem_limit_bytes=vmem)


def _rms_rows(x, w):
    ms = jnp.mean(x * x, axis=-1, keepdims=True)
    return x * lax.rsqrt(ms + RMS_EPS) * w


def _split2(x):
    hi = x.astype(BF16)
    lo = (x - hi.astype(F32)).astype(BF16)
    return hi, lo


def _split3(x):
    h1 = x.astype(BF16)
    r1 = x - h1.astype(F32)
    h2 = r1.astype(BF16)
    h3 = (r1 - h2.astype(F32)).astype(BF16)
    return h1, h2, h3


def _dot(a, b):
    return jnp.dot(a, b, preferred_element_type=F32)


def _dot_nt(a, b):
    return lax.dot_general(a, b, (((1,), (1,)), ((), ())), preferred_element_type=F32)


def _group_sum(x, gmat):
    hi, lo = _split2(x)
    return _dot(hi, gmat) + _dot(lo, gmat)


def _head_indicator():
    r = lax.broadcasted_iota(jnp.int32, (LANES, LANES), 0) // HEAD_DIM
    c = lax.broadcasted_iota(jnp.int32, (LANES, LANES), 1) // HEAD_DIM
    return jnp.where(r == c, 1.0, 0.0).astype(BF16)


def _norm_mm_kernel(x_ref, lnw_ref, w_ref, o_ref, xn_ref):
    @pl.when(pl.program_id(1) == 0)
    def _():
        xn_ref[...] = _rms_rows(x_ref[...], lnw_ref[...]).astype(BF16)

    o_ref[...] = _dot(xn_ref[...], w_ref[...])


def norm_matmul(x, ln_w, w_bf16, *, tm, tn):
    m, k = x.shape
    n = w_bf16.shape[1]
    assert m % tm == 0 and n % tn == 0
    return pl.pallas_call(
        _norm_mm_kernel,
        grid=(m // tm, n // tn),
        in_specs=[
            pl.BlockSpec((tm, k), lambda i, j: (i, 0)),
            pl.BlockSpec((1, k), lambda i, j: (0, 0)),
            pl.BlockSpec((k, tn), lambda i, j: (0, j)),
        ],
        out_specs=pl.BlockSpec((tm, tn), lambda i, j: (i, j)),
        out_shape=jax.ShapeDtypeStruct((m, n), F32),
        scratch_shapes=[pltpu.VMEM((tm, k), BF16)],
        compiler_params=_cp(("parallel", "arbitrary")),
        name="norm_matmul",
    )(x, ln_w.reshape(1, k), w_bf16)


def _mm_res_kernel(*refs, n_lhs):
    a_refs = refs[:n_lhs]
    w_refs = refs[n_lhs:2 * n_lhs]
    res_ref = refs[2 * n_lhs]
    o_ref = refs[2 * n_lhs + 1]
    acc = res_ref[...]
    for a_ref, w_ref in zip(a_refs, w_refs):
        acc = acc + _dot(a_ref[...].astype(BF16), w_ref[...])
    o_ref[...] = acc


def matmul_residual(lhs_list, w_list, res, *, tm, tn):
    m, n = res.shape
    n_lhs = len(lhs_list)
    assert m % tm == 0 and n % tn == 0
    in_specs = [pl.BlockSpec((tm, a.shape[1]), lambda i, j: (i, 0)) for a in lhs_list]
    in_specs += [pl.BlockSpec((w.shape[0], tn), lambda i, j: (0, j)) for w in w_list]
    in_specs += [pl.BlockSpec((tm, tn), lambda i, j: (i, j))]
    return pl.pallas_call(
        functools.partial(_mm_res_kernel, n_lhs=n_lhs),
        grid=(m // tm, n // tn),
        in_specs=in_specs,
        out_specs=pl.BlockSpec((tm, tn), lambda i, j: (i, j)),
        out_shape=jax.ShapeDtypeStruct((m, n), F32),
        compiler_params=_cp(("parallel", "arbitrary")),
        name="matmul_residual",
    )(*lhs_list, *w_list, res)


def rope_tables(pos):
    half = ROT_DIM // 2
    inv = ROPE_THETA ** (-jnp.arange(half, dtype=F32) * 2.0 / ROT_DIM)
    ang = pos.astype(F32)[:, None] * inv[None, :]
    cos, sin = jnp.cos(ang), jnp.sin(ang)
    t = pos.shape[0]
    ones = jnp.ones((t, HEAD_DIM - ROT_DIM), F32)
    zeros = jnp.zeros((t, HEAD_DIM - ROT_DIM), F32)
    z8 = jnp.zeros((t, half), F32)
    cos_t = jnp.concatenate([cos, cos, ones], axis=1)
    sin_a = jnp.concatenate([z8, sin, zeros], axis=1)
    sin_b = jnp.concatenate([-sin, z8, zeros], axis=1)
    return tuple(jnp.concatenate([a, a], axis=1) for a in (cos_t, sin_a, sin_b))


def _norm_rope_chunk(x, w, cos_t, sin_a, sin_b, gmat):
    ms = _group_sum(x * x, gmat) * (1.0 / HEAD_DIM)
    xn = x * lax.rsqrt(ms + RMS_EPS) * w
    half = ROT_DIM // 2
    return (xn * cos_t + pltpu.roll(xn, half, axis=1) * sin_a
            + pltpu.roll(xn, LANES - half, axis=1) * sin_b)


def _norm_rope(x, w, tabs, gmat):
    chunks = [
        _norm_rope_chunk(x[:, c * LANES:(c + 1) * LANES], w, *tabs, gmat)
        for c in range(x.shape[1] // LANES)
    ]
    return chunks[0] if len(chunks) == 1 else jnp.concatenate(chunks, axis=1)


def _sink_softmax(s, sink):
    m = jnp.maximum(jnp.max(s, axis=-1, keepdims=True), sink)
    e = jnp.exp(s - m)
    return e / (jnp.sum(e, axis=-1, keepdims=True) + jnp.exp(sink - m))


def _swa_prompt_kernel(q_ref, kc_ref, vc_ref, kp_ref, vp_ref, cc_ref, sac_ref, sbc_ref,
                       cp_ref, sap_ref, sbp_ref, qw_ref, kw_ref, sink_ref, o_ref, kn_ref):
    n = pl.program_id(1)
    blk = q_ref.shape[0]
    gmat = _head_indicator()
    tabs_c = (cc_ref[...], sac_ref[...], sbc_ref[...])
    tabs_p = (cp_ref[...], sap_ref[...], sbp_ref[...])
    q = _norm_rope(q_ref[...], qw_ref[...], tabs_c, gmat)
    k_cur = _norm_rope(kc_ref[...], kw_ref[...], tabs_c, gmat)
    k_prev = _norm_rope(kp_ref[...], kw_ref[...], tabs_p, gmat)
    kn_ref[...] = k_cur
    k_all = jnp.concatenate([k_prev, k_cur], axis=0).astype(BF16)
    v_all = jnp.concatenate([vp_ref[...], vc_ref[...]], axis=0).astype(BF16)

    qi = lax.broadcasted_iota(jnp.int32, (blk, 2 * blk), 0) + blk
    si = lax.broadcasted_iota(jnp.int32, (blk, 2 * blk), 1)
    rel = qi - si
    valid = (rel >= 0) & (rel <= WINDOW) & ((n > 0) | (si >= blk))

    for kv in range(ATT_KV_HEADS):
        k_h = k_all[:, kv * HEAD_DIM:(kv + 1) * HEAD_DIM]
        v_h = v_all[:, kv * HEAD_DIM:(kv + 1) * HEAD_DIM]
        heads = [kv * ATT_GROUP + g for g in range(ATT_GROUP)]
        q_g = jnp.concatenate(
            [q[:, h * HEAD_DIM:(h + 1) * HEAD_DIM] for h in heads], axis=0).astype(BF16)
        s = _dot_nt(q_g, k_h) * ATT_SCALE
        probs = []
        for g, h in enumerate(heads):
            s_h = jnp.where(valid, s[g * blk:(g + 1) * blk], -jnp.inf)
            probs.append(_sink_softmax(s_h, sink_ref[h]))
        p = jnp.concatenate(probs, axis=0).astype(BF16)
        o = _dot(p, v_h)
        for g, h in enumerate(heads):
            o_ref[:, h * HEAD_DIM:(h + 1) * HEAD_DIM] = o[g * blk:(g + 1) * blk]


def swa_prompt(pa, tabs, q_norm_w, k_norm_w, sinks):
    b, t, _ = pa.shape
    blk = WINDOW
    nb = t // blk
    qb, kb, vb = 0, ATT_WIDTH // KV_WIDTH, ATT_WIDTH // KV_WIDTH + 1
    cur = lambda i, n, *_: (i, n, 0)
    tab_cur = pl.BlockSpec((blk, LANES), lambda i, n: (n, 0))
    tab_prev = pl.BlockSpec((blk, LANES), lambda i, n: (jnp.maximum(n - 1, 0), 0))
    qw = jnp.tile(q_norm_w.reshape(1, HEAD_DIM), (1, 2))
    kw = jnp.tile(k_norm_w.reshape(1, HEAD_DIM), (1, 2))
    return pl.pallas_call(
        _swa_prompt_kernel,
        grid=(b, nb),
        in_specs=[
            pl.BlockSpec((None, blk, ATT_WIDTH), lambda i, n: (i, n, qb)),
            pl.BlockSpec((None, blk, KV_WIDTH), lambda i, n: (i, n, kb)),
            pl.BlockSpec((None, blk, KV_WIDTH), lambda i, n: (i, n, vb)),
            pl.BlockSpec((None, blk, KV_WIDTH), lambda i, n: (i, jnp.maximum(n - 1, 0), kb)),
            pl.BlockSpec((None, blk, KV_WIDTH), lambda i, n: (i, jnp.maximum(n - 1, 0), vb)),
            tab_cur, tab_cur, tab_cur, tab_prev, tab_prev, tab_prev,
            pl.BlockSpec((1, LANES), lambda i, n: (0, 0)),
            pl.BlockSpec((1, LANES), lambda i, n: (0, 0)),
            pl.BlockSpec(memory_space=pltpu.SMEM),
        ],
        out_specs=[
            pl.BlockSpec((None, blk, ATT_WIDTH), cur),
            pl.BlockSpec((None, blk, KV_WIDTH), cur),
        ],
        out_shape=[
            jax.ShapeDtypeStruct((b, t, ATT_WIDTH), F32),
            jax.ShapeDtypeStruct((b, t, KV_WIDTH), F32),
        ],
        compiler_params=_cp(("parallel", "arbitrary")),
        name="swa_prompt",
    )(pa, pa, pa, pa, pa, *tabs, *tabs, qw, kw, sinks)


def _qk_norm_rope_kernel(x_ref, w_ref, c_ref, sa_ref, sb_ref, o_ref):
    gmat = _head_indicator()
    tabs = (c_ref[...], sa_ref[...], sb_ref[...])
    for c in range(x_ref.shape[1] // LANES):
        sl = slice(c * LANES, (c + 1) * LANES)
        o_ref[:, sl] = _norm_rope_chunk(x_ref[:, sl], w_ref[:, sl], *tabs, gmat)


def qk_norm_rope(x, w_row, tabs):
    m, w = x.shape
    full = lambda *shape: pl.BlockSpec(shape, lambda: (0,) * len(shape))
    return pl.pallas_call(
        _qk_norm_rope_kernel,
        in_specs=[full(m, w), full(1, w), full(1, LANES), full(1, LANES), full(1, LANES)],
        out_specs=full(m, w),
        out_shape=jax.ShapeDtypeStruct((m, w), F32),
        name="qk_norm_rope",
    )(x, w_row, *tabs)


def _swa_decode_kernel(q_ref, kn_ref, vn_ref, ck_ref, cv_ref, sink_ref, o_ref, kw_ref, vw_ref):
    bb = q_ref.shape[0]
    nbuf = ck_ref.shape[1]
    row_kv = lax.broadcasted_iota(jnp.int32, (ATT_HEADS, KV_WIDTH), 0) // ATT_GROUP
    lane_kv = lax.broadcasted_iota(jnp.int32, (ATT_HEADS, KV_WIDTH), 1) // HEAD_DIM
    own = row_kv == lane_kv
    sink = sink_ref[...]
    for b in range(bb):
        q2 = q_ref[b]
        q_exp = jnp.where(own, jnp.concatenate([q2] * ATT_KV_HEADS, axis=1), 0.0)
        k_new, v_new = kn_ref[b], vn_ref[b]
        k_buf, v_buf = ck_ref[b], cv_ref[b]
        s_buf = _dot_nt(q_exp.astype(BF16), k_buf.astype(BF16)) * ATT_SCALE
        s_new = jnp.sum(q_exp * k_new, axis=-1, keepdims=True) * ATT_SCALE
        m = jnp.maximum(jnp.maximum(jnp.max(s_buf, axis=-1, keepdims=True), s_new), sink)
        e_buf = jnp.exp(s_buf - m)
        e_new = jnp.exp(s_new - m)
        inv = 1.0 / (jnp.sum(e_buf, axis=-1, keepdims=True) + e_new + jnp.exp(sink - m))
        o = _dot((e_buf * inv).astype(BF16), v_buf.astype(BF16)) + (e_new * inv) * v_new
        o = jnp.where(own, o, 0.0)
        o_ref[b] = (o[:, 0:HEAD_DIM] + o[:, HEAD_DIM:2 * HEAD_DIM]
                    + o[:, 2 * HEAD_DIM:3 * HEAD_DIM] + o[:, 3 * HEAD_DIM:4 * HEAD_DIM])
        kw_ref[b, 0:nbuf - 1, :] = k_buf[1:nbuf]
        kw_ref[b, nbuf - 1:nbuf, :] = k_new
        vw_ref[b, 0:nbuf - 1, :] = v_buf[1:nbuf]
        vw_ref[b, nbuf - 1:nbuf, :] = v_new


def swa_decode(q, k_new, v_new, cache_k, cache_v, sinks, *, bb=8):
    b, nbuf, _ = cache_k.shape
    blk3 = lambda s1, s2: pl.BlockSpec((bb, s1, s2), lambda i: (i, 0, 0))
    return pl.pallas_call(
        _swa_decode_kernel,
        grid=(b // bb,),
        in_specs=[
            blk3(ATT_HEADS, HEAD_DIM), blk3(1, KV_WIDTH), blk3(1, KV_WIDTH),
            blk3(nbuf, KV_WIDTH), blk3(nbuf, KV_WIDTH),
            pl.BlockSpec((ATT_HEADS, 1), lambda i: (0, 0)),
        ],
        out_specs=[blk3(ATT_HEADS, HEAD_DIM), blk3(nbuf, KV_WIDTH), blk3(nbuf, KV_WIDTH)],
        out_shape=[
            jax.ShapeDtypeStruct((b, ATT_HEADS, HEAD_DIM), F32),
            jax.ShapeDtypeStruct((b, nbuf, KV_WIDTH), F32),
            jax.ShapeDtypeStruct((b, nbuf, KV_WIDTH), F32),
        ],
        compiler_params=_cp(("parallel",)),
        name="swa_decode",
    )(q, k_new, v_new, cache_k, cache_v, sinks.reshape(ATT_HEADS, 1))


def _head_rms_kernel(x_ref, w_ref, o_ref):
    for h in range(x_ref.shape[1] // XATT_HEAD_DIM):
        sl = slice(h * XATT_HEAD_DIM, (h + 1) * XATT_HEAD_DIM)
        o_ref[:, sl] = _rms_rows(x_ref[:, sl], w_ref[...])


def head_rms(x, w):
    m, wd = x.shape
    return pl.pallas_call(
        _head_rms_kernel,
        in_specs=[pl.BlockSpec((m, wd), lambda: (0, 0)),
                  pl.BlockSpec((1, XATT_HEAD_DIM), lambda: (0, 0))],
        out_specs=pl.BlockSpec((m, wd), lambda: (0, 0)),
        out_shape=jax.ShapeDtypeStruct((m, wd), F32),
        name="head_rms",
    )(x, w.reshape(1, XATT_HEAD_DIM))


def _xattn_prompt_kernel(q_ref, k_ref, v_ref, w_ref, o_ref):
    scale = 1.0 / math.sqrt(XATT_HEAD_DIM)
    for h in range(XATT_HEADS):
        sl = slice(h * XATT_HEAD_DIM, (h + 1) * XATT_HEAD_DIM)
        qn = _rms_rows(q_ref[:, sl], w_ref[...]).astype(BF16)
        s = _dot_nt(qn, k_ref[:, sl].astype(BF16)) * scale
        e = jnp.exp(s - jnp.max(s, axis=-1, keepdims=True))
        p = e / jnp.sum(e, axis=-1, keepdims=True)
        o_ref[:, sl] = _dot(p.astype(BF16), v_ref[:, sl].astype(BF16))


def xattn_prompt(q, mem_k, mem_v, xq_norm_w, *, tq=512):
    b, t, w = q.shape
    n_mem = mem_k.shape[1]
    return pl.pallas_call(
        _xattn_prompt_kernel,
        grid=(b, t // tq),
        in_specs=[
            pl.BlockSpec((None, tq, w), lambda i, j: (i, j, 0)),
            pl.BlockSpec((None, n_mem, w), lambda i, j: (i, 0, 0)),
            pl.BlockSpec((None, n_mem, w), lambda i, j: (i, 0, 0)),
            pl.BlockSpec((1, XATT_HEAD_DIM), lambda i, j: (0, 0)),
        ],
        out_specs=pl.BlockSpec((None, tq, w), lambda i, j: (i, j, 0)),
        out_shape=jax.ShapeDtypeStruct((b, t, w), F32),
        compiler_params=_cp(("parallel", "arbitrary")),
        name="xattn_prompt",
    )(q, mem_k, mem_v, xq_norm_w.reshape(1, XATT_HEAD_DIM))


def _xattn_decode_kernel(q_ref, k_ref, v_ref, w_ref, o_ref):
    bb, rows, _ = q_ref.shape
    scale = 1.0 / math.sqrt(XATT_HEAD_DIM)
    row_h = lax.broadcasted_iota(jnp.int32, (rows, XATT_WIDTH), 0)
    lane_h = lax.broadcasted_iota(jnp.int32, (rows, XATT_WIDTH), 1) // XATT_HEAD_DIM
    own = row_h == lane_h
    for b in range(bb):
        qn = _rms_rows(q_ref[b], w_ref[...])
        q_exp = jnp.where(own, jnp.concatenate([qn] * XATT_HEADS, axis=1), 0.0)
        s = _dot_nt(q_exp.astype(BF16), k_ref[b].astype(BF16)) * scale
        e = jnp.exp(s - jnp.max(s, axis=-1, keepdims=True))
        p = e / jnp.sum(e, axis=-1, keepdims=True)
        o = _dot(p.astype(BF16), v_ref[b].astype(BF16))
        o_ref[b] = jnp.sum(jnp.where(own, o, 0.0), axis=0, keepdims=True)


def xattn_decode(q_pad, mem_k, mem_v, xq_norm_w, *, bb=8):
    b, rows, _ = q_pad.shape
    n_mem = mem_k.shape[1]
    return pl.pallas_call(
        _xattn_decode_kernel,
        grid=(b // bb,),
        in_specs=[
            pl.BlockSpec((bb, rows, XATT_HEAD_DIM), lambda i: (i, 0, 0)),
            pl.BlockSpec((bb, n_mem, XATT_WIDTH), lambda i: (i, 0, 0)),
            pl.BlockSpec((bb, n_mem, XATT_WIDTH), lambda i: (i, 0, 0)),
            pl.BlockSpec((1, XATT_HEAD_DIM), lambda i: (0, 0)),
        ],
        out_specs=pl.BlockSpec((bb, 1, XATT_WIDTH), lambda i: (i, 0, 0)),
        out_shape=jax.ShapeDtypeStruct((b, 1, XATT_WIDTH), F32),
        compiler_params=_cp(("parallel",)),
        name="xattn_decode",
    )(q_pad, mem_k, mem_v, xq_norm_w.reshape(1, XATT_HEAD_DIM))


LORA_OFF = 3 * RWKV_WIDTH
GATE_OFF = LORA_OFF + DECAY_LORA + AAA_LORA
GATE_PAD = RWKV_PROJ_PAD - GATE_OFF


def _sigmoid(x):
    return 1.0 / (1.0 + jnp.exp(-x))


def _per_chunk(fn, *arrays):
    w = arrays[0].shape[1]
    outs = [fn(*(a[:, c * LANES:(c + 1) * LANES] for a in arrays)) for c in range(w // LANES)]
    return jnp.concatenate(outs, axis=1)


def _rwkv_prep_core(pr, prev, mu, w0, a0, kk_w, ka_w, rk_w, w_lora, w_gate):
    c = RWKV_WIDTH
    gmat = _head_indicator()
    xm = pr + (prev - pr) * mu
    r, k, v = xm[:, 0:c], xm[:, c:2 * c], xm[:, 2 * c:3 * c]
    lora = xm[:, LORA_OFF:LORA_OFF + LANES]
    lane = lax.broadcasted_iota(jnp.int32, lora.shape, 1)
    lora_in = jnp.where(lane < DECAY_LORA, jnp.tanh(lora), lora)
    wa = _dot(lora_in.astype(BF16), w_lora)
    z = -(w0 + wa[:, 0:c])
    softplus = jnp.maximum(z, 0.0) + jnp.log(1.0 + jnp.exp(-jnp.abs(z)))
    log_decay = -jnp.exp(-softplus - 0.5)
    a = _sigmoid(a0 + wa[:, c:2 * c])
    g = _dot(_sigmoid(xm[:, GATE_OFF:GATE_OFF + GATE_PAD]).astype(BF16), w_gate)
    kk = k * kk_w
    norm = jnp.sqrt(_per_chunk(lambda t: _group_sum(t * t, gmat), kk))
    kk = kk / jnp.maximum(norm, 1e-12)
    kp = k * (1.0 + (a - 1.0) * ka_w)
    bonus = _per_chunk(lambda t: _group_sum(t, gmat), r * kp * rk_w) * v
    return r, log_decay, kp, v, kk, kk * a, bonus, g


def _rwkv_prep_seq_kernel(pr_ref, prev0_ref, mu_ref, w0_ref, a0_ref, kkw_ref, kaw_ref, rkw_ref,
                          wl_ref, wg_ref, *refs):
    out_refs, last_ref = refs[:-1], refs[-1]

    @pl.when(pl.program_id(1) == 0)
    def _():
        last_ref[...] = prev0_ref[...]

    pr = pr_ref[...]
    rows = pr.shape[0]
    row = lax.broadcasted_iota(jnp.int32, (rows, 1), 0)
    prev = jnp.where(row == 0, last_ref[...], pltpu.roll(pr, 1, axis=0))
    last_ref[...] = pr[rows - 1:rows, :]
    outs = _rwkv_prep_core(pr, prev, mu_ref[...], w0_ref[...], a0_ref[...], kkw_ref[...],
                           kaw_ref[...], rkw_ref[...], wl_ref[...], wg_ref[...])
    for o_ref, o in zip(out_refs, outs):
        o_ref[...] = o


def _rwkv_prep_tok_kernel(pr_ref, prev_ref, mu_ref, w0_ref, a0_ref, kkw_ref, kaw_ref, rkw_ref,
                          wl_ref, wg_ref, *out_refs):
    outs = _rwkv_prep_core(pr_ref[...], prev_ref[...], mu_ref[...], w0_ref[...], a0_ref[...],
                           kkw_ref[...], kaw_ref[...], rkw_ref[...], wl_ref[...], wg_ref[...])
    for o_ref, o in zip(out_refs, outs):
        o_ref[...] = o


def _rwkv_param_specs(index_map):
    c = RWKV_WIDTH
    shapes = [(1, RWKV_PROJ_PAD)] + [(1, c)] * 5 + [(LANES, 2 * c), (GATE_PAD, c)]
    return [pl.BlockSpec(s, index_map) for s in shapes]


def rwkv_prep_seq(pr, prev0, params, *, tm=256):
    b, t, wd = pr.shape
    c = RWKV_WIDTH
    out = jax.ShapeDtypeStruct((b, t, c), F32)
    return pl.pallas_call(
        _rwkv_prep_seq_kernel,
        grid=(b, t // tm),
        in_specs=[pl.BlockSpec((None, tm, wd), lambda i, j: (i, j, 0)),
                  pl.BlockSpec((None, 1, wd), lambda i, j: (i, 0, 0))]
        + _rwkv_param_specs(lambda i, j: (0, 0)),
        out_specs=[pl.BlockSpec((None, tm, c), lambda i, j: (i, j, 0))] * 8,
        out_shape=[out] * 8,
        scratch_shapes=[pltpu.VMEM((1, wd), F32)],
        compiler_params=_cp(("parallel", "arbitrary")),
        name="rwkv_prep_seq",
    )(pr, prev0, *params)


def rwkv_prep_tok(pr, prev, params):
    m, wd = pr.shape
    c = RWKV_WIDTH
    out = jax.ShapeDtypeStruct((m, c), F32)
    return pl.pallas_call(
        _rwkv_prep_tok_kernel,
        grid=(1,),
        in_specs=[pl.BlockSpec((m, wd), lambda i: (0, 0))] * 2
        + _rwkv_param_specs(lambda i: (0, 0)),
        out_specs=[pl.BlockSpec((m, c), lambda i: (0, 0))] * 8,
        out_shape=[out] * 8,
        compiler_params=_cp(("arbitrary",)),
        name="rwkv_prep_tok",
    )(pr, prev, *params)


def _dot_tn(a, b):
    return lax.dot_general(a, b, (((0,), (0,)), ((), ())), preferred_element_type=F32)


def _rwkv_scan_kernel(r_ref, ld_ref, kp_ref, v_ref, kk_ref, b_ref, y_ref, s_out_ref, s_ref):
    @pl.when(pl.program_id(1) == 0)
    def _():
        s_ref[...] = jnp.zeros_like(s_ref)

    n = CHUNK
    ti = lax.broadcasted_iota(jnp.int32, (n, n), 0)
    si = lax.broadcasted_iota(jnp.int32, (n, n), 1)
    strict, incl = si < ti, si <= ti
    tri = jnp.where(incl, 1.0, 0.0).astype(BF16)

    ld = ld_ref[...]
    l1, l2, l3 = _split3(ld)
    lc = _dot(tri, l1) + _dot(tri, l2) + _dot(tri, l3)
    lc_end = lc[n - 1:n, :]
    e_neg = jnp.exp(-lc)
    kk, b, kp = kk_ref[...], b_ref[...], kp_ref[...]
    a_t = (-kk * jnp.exp(lc - ld)).astype(BF16)
    b_t = (b * e_neg).astype(BF16)
    k_t = (kp * e_neg).astype(BF16)
    r_t = (r_ref[...] * jnp.exp(lc)).astype(BF16)
    to_end = jnp.exp(lc_end - lc)
    b_e = (b * to_end).astype(BF16)
    k_e = (kp * to_end).astype(BF16)
    v_b = v_ref[...].astype(BF16)
    g_end = jnp.exp(lc_end)

    for h in range(RWKV_HEADS):
        sl = slice(h * n, (h + 1) * n)
        a_h, r_h, v_h = a_t[:, sl], r_t[:, sl], v_b[:, sl]
        gm = _dot_nt(jnp.concatenate([a_h, r_h], axis=0),
                     jnp.concatenate([b_t[:, sl], k_t[:, sl]], axis=0))
        n_ab = jnp.where(strict, gm[0:n, 0:n], 0.0).astype(BF16)
        l_ak = jnp.where(strict, gm[0:n, n:2 * n], 0.0).astype(BF16)
        p_rb = jnp.where(incl, gm[n:2 * n, 0:n], 0.0).astype(BF16)
        p_rk = jnp.where(incl, gm[n:2 * n, n:2 * n], 0.0).astype(BF16)
        s0 = s_ref[h]
        s0_b = s0.astype(BF16)
        u = _dot_nt(a_h, s0_b) + _dot(l_ak, v_h)
        pw = n_ab
        for step in range(6):
            u = u + _dot(pw, u.astype(BF16))
            if step < 5:
                pw = _dot(pw, pw).astype(BF16)
        u_b = u.astype(BF16)
        y_ref[:, sl] = _dot_nt(r_h, s0_b) + _dot(p_rb, u_b) + _dot(p_rk, v_h)
        s_new = s0 * g_end[:, sl] + _dot_tn(
            jnp.concatenate([u_b, v_h], axis=0),
            jnp.concatenate([b_e[:, sl], k_e[:, sl]], axis=0))
        s_ref[h] = s_new
        s_out_ref[h] = s_new


def rwkv_scan(r, ld, kp, v, kk, b):
    bsz, t, c = r.shape
    blk = pl.BlockSpec((None, CHUNK, c), lambda i, j: (i, j, 0))
    st = pl.BlockSpec((None, RWKV_HEADS, RWKV_HEAD_DIM, RWKV_HEAD_DIM), lambda i, j: (i, 0, 0, 0))
    return pl.pallas_call(
        _rwkv_scan_kernel,
        grid=(bsz, t // CHUNK),
        in_specs=[blk] * 6,
        out_specs=[blk, st],
        out_shape=[jax.ShapeDtypeStruct((bsz, t, c), F32),
                   jax.ShapeDtypeStruct((bsz, RWKV_HEADS, RWKV_HEAD_DIM, RWKV_HEAD_DIM), F32)],
        scratch_shapes=[pltpu.VMEM((RWKV_HEADS, RWKV_HEAD_DIM, RWKV_HEAD_DIM), F32)],
        compiler_params=_cp(("parallel", "arbitrary")),
        name="rwkv_scan",
    )(r, ld, kp, v, kk, b)


def _rwkv_step_kernel(r_ref, ld_ref, kp_ref, v_ref, kk_ref, b_ref, s_ref, y_ref, s_out_ref):
    n = RWKV_HEAD_DIM
    eye = lax.broadcasted_iota(jnp.int32, (n, n), 0) == lax.broadcasted_iota(jnp.int32, (n, n), 1)

    def body(g, carry):
        s = s_ref[g]
        sa = jnp.sum(s * (-kk_ref[g]), axis=1, keepdims=True)
        v_col = jnp.sum(jnp.where(eye, v_ref[g], 0.0), axis=1, keepdims=True)
        s_new = s * jnp.exp(ld_ref[g]) + sa * b_ref[g] + v_col * kp_ref[g]
        y_col = jnp.sum(s_new * r_ref[g], axis=1, keepdims=True)
        y_ref[g] = jnp.sum(jnp.where(eye, y_col, 0.0), axis=0, keepdims=True)
        s_out_ref[g] = s_new
        return carry

    lax.fori_loop(0, s_ref.shape[0], body, 0)


def rwkv_step(r, ld, kp, v, kk, b, state, *, gb=64):
    g = state.shape[0]
    n = RWKV_HEAD_DIM
    row = pl.BlockSpec((gb, 1, n), lambda i: (i, 0, 0))
    st = pl.BlockSpec((gb, n, n), lambda i: (i, 0, 0))
    return pl.pallas_call(
        _rwkv_step_kernel,
        grid=(g // gb,),
        in_specs=[row] * 6 + [st],
        out_specs=[row, st],
        out_shape=[jax.ShapeDtypeStruct((g, 1, n), F32), jax.ShapeDtypeStruct((g, n, n), F32)],
        compiler_params=_cp(("parallel",)),
        name="rwkv_step",
    )(r, ld, kp, v, kk, b, state)


def _rwkv_post_kernel(y_ref, bonus_ref, g_ref, lnw_ref, lnb_ref, o_ref):
    gmat = _head_indicator()
    inv = 1.0 / RWKV_HEAD_DIM
    for c in range(y_ref.shape[1] // LANES):
        sl = slice(c * LANES, (c + 1) * LANES)
        y = y_ref[:, sl]
        d = y - _group_sum(y, gmat) * inv
        var = _group_sum(d * d, gmat) * inv
        yn = d * lax.rsqrt(var + GN_EPS) * lnw_ref[:, sl] + lnb_ref[:, sl]
        o_ref[:, sl] = (yn + bonus_ref[:, sl]) * g_ref[:, sl]


def rwkv_post(y, bonus, g, ln_w, ln_b, *, tm):
    m, c = y.shape
    blk = pl.BlockSpec((tm, c), lambda i: (i, 0))
    vec = pl.BlockSpec((1, c), lambda i: (0, 0))
    return pl.pallas_call(
        _rwkv_post_kernel,
        grid=(m // tm,),
        in_specs=[blk, blk, blk, vec, vec],
        out_specs=blk,
        out_shape=jax.ShapeDtypeStruct((m, c), F32),
        compiler_params=_cp(("parallel",)),
        name="rwkv_post",
    )(y, bonus, g, ln_w.reshape(1, c), ln_b.reshape(1, c))


ROUTER_LANES = LANES


def _router_kernel(h_ref, lnw_ref, whi_ref, wlo_ref, bias_ref, u_ref, idx_ref, gate_ref):
    u = _rms_rows(h_ref[...], lnw_ref[...])
    u_ref[...] = u
    u_hi, u_lo = _split2(u)
    w_hi = whi_ref[...]
    logits = _dot(u_hi, w_hi) + _dot(u_lo, w_hi) + _dot(u_hi, wlo_ref[...]) + bias_ref[...]
    lane = lax.broadcasted_iota(jnp.int32, logits.shape, 1)
    neg = -jnp.inf

    def first_max(x):
        m = jnp.max(x, axis=1, keepdims=True)
        return m, jnp.min(jnp.where(x == m, lane, ROUTER_LANES), axis=1, keepdims=True)

    gl = jnp.where(lane < N_EXPERT_GROUPS, logits, neg)
    g_max, g_idx = first_max(gl)
    g_gate = 1.0 / jnp.sum(jnp.exp(gl - g_max), axis=1, keepdims=True)
    lo = N_EXPERT_GROUPS + g_idx * EXPERTS_PER_GROUP
    el = jnp.where((lane >= lo) & (lane < lo + EXPERTS_PER_GROUP), logits, neg)
    v1, i1 = first_max(el)
    v2, i2 = first_max(jnp.where(lane == i1, neg, el))
    e2 = jnp.exp(v2 - v1)
    w1 = g_gate / (1.0 + e2)
    w2 = g_gate * e2 / (1.0 + e2)
    idx_ref[...] = jnp.where(lane == 0, i1 - N_EXPERT_GROUPS,
                             jnp.where(lane == 1, i2 - N_EXPERT_GROUPS, 0))
    gate_ref[...] = jnp.where(lane == 0, w1, jnp.where(lane == 1, w2, 0.0))


def moe_router(h, ln_w, w_hi, w_lo, bias, *, tm):
    m, d = h.shape
    row = lambda w: pl.BlockSpec((tm, w), lambda i: (i, 0))
    const = lambda r, w: pl.BlockSpec((r, w), lambda i: (0, 0))
    return pl.pallas_call(
        _router_kernel,
        grid=(m // tm,),
        in_specs=[row(d), const(1, d), const(d, ROUTER_LANES), const(d, ROUTER_LANES),
                  const(1, ROUTER_LANES)],
        out_specs=[row(d), row(ROUTER_LANES), row(ROUTER_LANES)],
        out_shape=[jax.ShapeDtypeStruct((m, d), F32),
                   jax.ShapeDtypeStruct((m, ROUTER_LANES), jnp.int32),
                   jax.ShapeDtypeStruct((m, ROUTER_LANES), F32)],
        compiler_params=_cp(("parallel",)),
        name="moe_router",
    )(h, ln_w.reshape(1, d), w_hi, w_lo, bias)


def _moe_expert_kernel(bexp_ref, nused_ref, tok_ref, dst_ref, roww_ref, u_hbm, wg_ref, wu_ref,
                       wd_ref, y_hbm, xbuf, ybuf, wg_b, wu_b, wd_b, sem_in, sem_out):
    i = pl.program_id(0)
    n_assign = y_hbm.shape[0]

    def row_in(r):
        return pltpu.make_async_copy(u_hbm.at[pl.ds(tok_ref[0, 0, r], 1)],
                                     xbuf.at[pl.ds(r, 1)], sem_in)

    def row_out(r):
        return pltpu.make_async_copy(ybuf.at[pl.ds(r, 1)],
                                     y_hbm.at[pl.ds(dst_ref[0, 0, r], 1)], sem_out)

    def for_rows(fn):
        def body(r, carry):
            fn(r)
            return carry
        lax.fori_loop(0, MOE_BLOCK, body, 0)

    def for_valid_rows(fn):
        def body(r):
            @pl.when(dst_ref[0, 0, r] < n_assign)
            def _():
                fn(r)
        for_rows(body)

    @pl.when(i < nused_ref[0])
    def _():
        for_rows(lambda r: row_in(r).start())

        @pl.when((i == 0) | (bexp_ref[i] != bexp_ref[jnp.maximum(i - 1, 0)]))
        def _():
            wg_b[...] = wg_ref[...].astype(BF16)
            wu_b[...] = wu_ref[...].astype(BF16)
            wd_b[...] = wd_ref[...].astype(BF16)

        for_rows(lambda r: row_in(r).wait())
        x = xbuf[...].astype(BF16)
        hg = _dot(x, wg_b[...])
        hu = _dot(x, wu_b[...])
        act = (hg * _sigmoid(hg) * hu).astype(BF16)
        ybuf[...] = _dot(act, wd_b[...]) * roww_ref[...]
        for_valid_rows(lambda r: row_out(r).start())
        for_valid_rows(lambda r: row_out(r).wait())


def moe_experts(u_all, row_tok, row_dst, row_w, block_exp, n_used, w_gate, w_up, w_down,
                n_assign):
    d = u_all.shape[1]
    n_blocks = row_tok.shape[0]
    ff = w_gate.shape[2]
    smem_blk = pl.BlockSpec((1, 1, MOE_BLOCK), lambda i, be, nu: (i, 0, 0),
                            memory_space=pltpu.SMEM)
    grid_spec = pltpu.PrefetchScalarGridSpec(
        num_scalar_prefetch=2,
        grid=(n_blocks,),
        in_specs=[
            smem_blk, smem_blk,
            pl.BlockSpec((MOE_BLOCK, 1), lambda i, be, nu: (i, 0)),
            pl.BlockSpec(memory_space=pl.ANY),
            pl.BlockSpec((None, d, ff), lambda i, be, nu: (be[i], 0, 0)),
            pl.BlockSpec((None, d, ff), lambda i, be, nu: (be[i], 0, 0)),
            pl.BlockSpec((None, ff, d), lambda i, be, nu: (be[i], 0, 0)),
        ],
        out_specs=pl.BlockSpec(memory_space=pl.ANY),
        scratch_shapes=[
            pltpu.VMEM((MOE_BLOCK, d), F32), pltpu.VMEM((MOE_BLOCK, d), F32),
            pltpu.VMEM((d, ff), BF16), pltpu.VMEM((d, ff), BF16), pltpu.VMEM((ff, d), BF16),
            pltpu.SemaphoreType.DMA, pltpu.SemaphoreType.DMA,
        ],
    )
    return pl.pallas_call(
        _moe_expert_kernel,
        grid_spec=grid_spec,
        out_shape=jax.ShapeDtypeStruct((n_assign, d), F32),
        compiler_params=_cp(("arbitrary",)),
        name="moe_experts",
    )(block_exp, n_used, row_tok, row_dst, row_w, u_all, w_gate, w_up, w_down)


def _moe_combine_kernel(h_ref, y0_ref, y1_ref, o_ref):
    o_ref[...] = h_ref[...] + (y0_ref[...] + y1_ref[...])


def moe_combine(h, y_slots, row_off, slot_stride, *, tm):
    m, d = h.shape
    assert row_off % tm == 0 and slot_stride % tm == 0
    off0, off1 = row_off // tm, (row_off + slot_stride) // tm
    return pl.pallas_call(
        _moe_combine_kernel,
        grid=(m // tm,),
        in_specs=[pl.BlockSpec((tm, d), lambda i: (i, 0)),
                  pl.BlockSpec((tm, d), lambda i: (i + off0, 0)),
                  pl.BlockSpec((tm, d), lambda i: (i + off1, 0))],
        out_specs=pl.BlockSpec((tm, d), lambda i: (i, 0)),
        out_shape=jax.ShapeDtypeStruct((m, d), F32),
        compiler_params=_cp(("parallel",)),
        name="moe_combine",
    )(h, y_slots, y_slots)


def moe_dispatch(e_idx, gates, slot_stride):
    m = e_idx.shape[0]
    a = m * TOP_K
    e_flat = e_idx.reshape(a)
    order = jnp.argsort(e_flat, stable=True).astype(jnp.int32)
    e_s = e_flat[order]
    counts = jnp.bincount(e_flat, length=N_EXPERTS).astype(jnp.int32)
    pad_counts = (counts + MOE_BLOCK - 1) // MOE_BLOCK * MOE_BLOCK
    starts = jnp.cumsum(counts) - counts
    pad_ends = jnp.cumsum(pad_counts)
    pad_starts = pad_ends - pad_counts
    dest = pad_starts[e_s] + jnp.arange(a, dtype=jnp.int32) - starts[e_s]
    n_blocks = a // MOE_BLOCK + N_EXPERTS
    p = n_blocks * MOE_BLOCK
    row_tok = jnp.full((p,), m, jnp.int32).at[dest].set(order // TOP_K)
    row_dst = jnp.full((p,), TOP_K * slot_stride, jnp.int32).at[dest].set(
        (order % TOP_K) * slot_stride + order // TOP_K)
    row_w = jnp.zeros((p,), F32).at[dest].set(gates.reshape(a)[order])
    n_used = (pad_ends[-1] // MOE_BLOCK).astype(jnp.int32)
    blk_start = jnp.minimum(jnp.arange(n_blocks, dtype=jnp.int32), n_used - 1) * MOE_BLOCK
    block_exp = jnp.minimum(jnp.searchsorted(pad_ends, blk_start, side='right'),
                            N_EXPERTS - 1).astype(jnp.int32)
    return (row_tok.reshape(n_blocks, 1, MOE_BLOCK), row_dst.reshape(n_blocks, 1, MOE_BLOCK),
            row_w.reshape(p, 1), block_exp, n_used.reshape(1))


def rwkv_params(rw_mu, rw_w0, rw_w2, rw_a0, rw_a2, rw_g2, rw_k_k, rw_k_a, rw_r_k):
    c = RWKV_WIDTH
    mu = jnp.pad(rw_mu, (0, RWKV_PROJ_PAD - RWKV_PROJ)).reshape(1, RWKV_PROJ_PAD)
    w_lora = jnp.zeros((LANES, 2 * c), F32)
    w_lora = w_lora.at[0:DECAY_LORA, 0:c].set(rw_w2).at[DECAY_LORA:LANES, c:2 * c].set(rw_a2)
    w_gate = jnp.pad(rw_g2, ((0, GATE_PAD - GATE_LORA), (0, 0)))
    vec = lambda x: x.reshape(1, c)
    return (mu, vec(rw_w0), vec(rw_a0), vec(rw_k_k), vec(rw_k_a), vec(rw_r_k),
            w_lora.astype(BF16), w_gate.astype(BF16))


def _token_tiles(m):
    return (1024, 512) if m % 1024 == 0 else (m, m)


def _dense_front(x2d, wts, tm):
    pa = norm_matmul(x2d, wts['ln1_w'], wts['w_att'], tm=tm, tn=512)
    pr = norm_matmul(x2d, wts['ln1_w'], wts['w_rw'], tm=tm, tn=RWKV_PROJ_PAD // 3)
    return pa, pr


def _dense_back(x2d, att2d, rw2d, wts, xattn_fn, tm, te):
    h1 = matmul_residual([att2d, rw2d], [wts['w_out_a'], wts['w_out_r']], x2d, tm=tm, tn=512)
    qx = norm_matmul(h1, wts['ln2_w'], wts['xq_w'], tm=tm, tn=XATT_WIDTH)
    ox = xattn_fn(qx)
    h2 = matmul_residual([ox], [wts['xo_w']], h1, tm=tm, tn=512)
    u, idx, gate = moe_router(h2, wts['ln3_w'], wts['router_hi'], wts['router_lo'],
                              wts['router_b'], tm=te)
    return h2, u, idx[:, :TOP_K], gate[:, :TOP_K]


def kernel(x_prompt, x_sample, cache_win_k, cache_win_v, state_wkv, state_shift, cache_mem_k, cache_mem_v, mem_prompt, ln1_w, w_in, q_norm_w, k_norm_w, attn_sinks, rw_mu, rw_w0, rw_w2, rw_a0, rw_a2, rw_g2, rw_k_k, rw_k_a, rw_r_k, rw_ln_w, rw_ln_b, w_out, ln2_w, mem_norm_w, xq_w, xkv_w, xq_norm_w, xk_norm_w, xo_w, ln3_w, router_group_w, router_group_b, router_expert_w, router_expert_b, exp_w_gate, exp_w_up, exp_w_down):
    assert w_in.shape[0] == 1, "single-layer stack"
    bp, seq, d = x_prompt.shape
    bs = x_sample.shape[0]
    mp = bp * seq
    c = RWKV_WIDTH

    router_w = jnp.concatenate(
        [router_group_w[0], router_expert_w[0],
         jnp.zeros((d, ROUTER_LANES - N_EXPERT_GROUPS - N_EXPERTS), F32)], axis=1)
    router_hi = router_w.astype(BF16)
    wts = {
        'ln1_w': ln1_w[0], 'ln2_w': ln2_w[0], 'ln3_w': ln3_w[0],
        'w_att': w_in[0][:, :ATT_PROJ].astype(BF16),
        'w_rw': jnp.pad(w_in[0][:, ATT_PROJ:],
                        ((0, 0), (0, RWKV_PROJ_PAD - RWKV_PROJ))).astype(BF16),
        'w_out_a': w_out[0][:ATT_WIDTH].astype(BF16),
        'w_out_r': w_out[0][ATT_WIDTH:].astype(BF16),
        'xq_w': xq_w[0].astype(BF16), 'xo_w': xo_w[0].astype(BF16),
        'router_hi': router_hi,
        'router_lo': (router_w - router_hi.astype(F32)).astype(BF16),
        'router_b': jnp.pad(jnp.concatenate([router_group_b[0], router_expert_b[0]]),
                            (0, ROUTER_LANES - N_EXPERT_GROUPS - N_EXPERTS)).reshape(1, -1),
    }
    rw_par = rwkv_params(rw_mu[0], rw_w0[0], rw_w2[0], rw_a0[0], rw_a2[0], rw_g2[0],
                         rw_k_k[0], rw_k_a[0], rw_r_k[0])

    tm_p, te_p = _token_tiles(mp)
    xp = x_prompt.reshape(mp, d)
    pa, pr = _dense_front(xp, wts, tm_p)
    pa3 = pa.reshape(bp, seq, ATT_PROJ)
    pr3 = pr.reshape(bp, seq, RWKV_PROJ_PAD)
    tabs_p = rope_tables(jnp.arange(seq, dtype=jnp.int32))
    att_p, kn_p = swa_prompt(pa3, tabs_p, q_norm_w[0], k_norm_w[0], attn_sinks[0])
    prep = rwkv_prep_seq(pr3, jnp.zeros((bp, 1, RWKV_PROJ_PAD), F32), rw_par)
    r, ld, kp, v, kk, b, bonus, g = prep
    y_p, wkv_p = rwkv_scan(r, ld, kp, v, kk, b)
    rw_p = rwkv_post(y_p.reshape(mp, c), bonus.reshape(mp, c), g.reshape(mp, c),
                     rw_ln_w[0], rw_ln_b[0], tm=te_p)

    n_mem = mem_prompt.shape[1]
    kv_mem = norm_matmul(mem_prompt.reshape(bp * n_mem, d), mem_norm_w[0],
                         xkv_w[0].astype(BF16), tm=bp * n_mem, tn=512)
    mem_k = head_rms(kv_mem[:, :XATT_WIDTH], xk_norm_w[0])
    mem_v = kv_mem[:, XATT_WIDTH:]
    mem_k3 = mem_k.reshape(bp, n_mem, XATT_WIDTH)
    mem_v3 = mem_v.reshape(bp, n_mem, XATT_WIDTH)

    def xattn_p(qx):
        return xattn_prompt(qx.reshape(bp, seq, XATT_WIDTH), mem_k3, mem_v3,
                            xq_norm_w[0]).reshape(mp, XATT_WIDTH)

    h2_p, u_p, idx_p, gate_p = _dense_back(xp, att_p.reshape(mp, ATT_WIDTH), rw_p, wts,
                                           xattn_p, tm_p, te_p)

    tm_s, te_s = _token_tiles(bs)
    xs = x_sample.reshape(bs, d)
    sa, sr = _dense_front(xs, wts, tm_s)
    tabs_s = rope_tables(PAST_LEN + jnp.arange(1, dtype=jnp.int32))
    qk_w = jnp.concatenate([jnp.tile(q_norm_w[0], ATT_HEADS),
                            jnp.tile(k_norm_w[0], ATT_KV_HEADS)]).reshape(1, -1)
    qk = qk_norm_rope(sa[:, :ATT_WIDTH + KV_WIDTH], qk_w, tabs_s)
    nbuf = cache_win_k.shape[2]
    att_s, win_k, win_v = swa_decode(
        qk[:, :ATT_WIDTH].reshape(bs, ATT_HEADS, HEAD_DIM),
        qk[:, ATT_WIDTH:].reshape(bs, 1, KV_WIDTH),
        sa[:, ATT_WIDTH + KV_WIDTH:].reshape(bs, 1, KV_WIDTH),
        cache_win_k[0].reshape(bs, nbuf, KV_WIDTH), cache_win_v[0].reshape(bs, nbuf, KV_WIDTH),
        attn_sinks[0])
    shift_prev = jnp.pad(state_shift[0], ((0, 0), (0, RWKV_PROJ_PAD - RWKV_PROJ)))
    r, ld, kp, v, kk, b, bonus, g = rwkv_prep_tok(sr, shift_prev, rw_par)
    gh = bs * RWKV_HEADS
    rows = [t.reshape(gh, 1, RWKV_HEAD_DIM) for t in (r, ld, kp, v, kk, b)]
    y_s, wkv_s = rwkv_step(*rows, state_wkv[0].reshape(gh, RWKV_HEAD_DIM, RWKV_HEAD_DIM))
    rw_s = rwkv_post(y_s.reshape(bs, c), bonus, g, rw_ln_w[0], rw_ln_b[0], tm=te_s)
    cmk = cache_mem_k[0].reshape(bs, n_mem, XATT_WIDTH)
    cmv = cache_mem_v[0].reshape(bs, n_mem, XATT_WIDTH)

    def xattn_s(qx):
        q_pad = jnp.pad(qx.reshape(bs, XATT_HEADS, XATT_HEAD_DIM), ((0, 0), (0, 4), (0, 0)))
        return xattn_decode(q_pad, cmk, cmv, xq_norm_w[0]).reshape(bs, XATT_WIDTH)

    h2_s, u_s, idx_s, gate_s = _dense_back(xs, att_s.reshape(bs, ATT_WIDTH), rw_s, wts,
                                           xattn_s, tm_s, te_s)

    m_all = mp + bs
    slot_stride = m_all
    tc = math.gcd(mp, bs, 512)
    u_all = jnp.concatenate([u_p, u_s, jnp.zeros((1, d), F32)], axis=0)
    e_idx = jnp.concatenate([idx_p, idx_s], axis=0)
    gates = jnp.concatenate([gate_p, gate_s], axis=0)
    row_tok, row_dst, row_w, block_exp, n_used = moe_dispatch(e_idx, gates, slot_stride)
    y_slots = moe_experts(u_all, row_tok, row_dst, row_w, block_exp, n_used,
                          exp_w_gate[0], exp_w_up[0], exp_w_down[0], TOP_K * slot_stride)
    out_p = moe_combine(h2_p, y_slots, 0, slot_stride, tm=tc)
    out_s = moe_combine(h2_s, y_slots, mp, slot_stride, tm=tc)

    win = min(WINDOW, seq)
    kv_shape = (1, bp, win, ATT_KV_HEADS, HEAD_DIM)
    return (
        out_p.reshape(bp, seq, d),
        out_s.reshape(bs, 1, d),
        kn_p[:, seq - win:].reshape(kv_shape),
        pa3[:, seq - win:, ATT_WIDTH + KV_WIDTH:].reshape(kv_shape),
        wkv_p[None],
        pr3[:, seq - 1, :RWKV_PROJ][None],
        mem_k3.reshape(1, bp, n_mem, XATT_HEADS, XATT_HEAD_DIM),
        mem_v3.reshape(1, bp, n_mem, XATT_HEADS, XATT_HEAD_DIM),
        win_k.reshape(1, bs, nbuf, ATT_KV_HEADS, HEAD_DIM),
        win_v.reshape(1, bs, nbuf, ATT_KV_HEADS, HEAD_DIM),
        wkv_s.reshape(1, bs, RWKV_HEADS, RWKV_HEAD_DIM, RWKV_HEAD_DIM),
        sr[:, :RWKV_PROJ].reshape(1, bs, RWKV_PROJ),
    )
```

```python
import functools
import math

import jax
import jax.numpy as jnp
from jax import lax
from jax.experimental import pallas as pl
from jax.experimental.pallas import tpu as pltpu

F32 = jnp.float32
BF16 = jnp.bfloat16

D_MODEL = 2048
HEAD_DIM = 64
ATT_HEADS = 16
ATT_KV_HEADS = 4
ATT_GROUP = ATT_HEADS // ATT_KV_HEADS
ATT_WIDTH = ATT_HEADS * HEAD_DIM
KV_WIDTH = ATT_KV_HEADS * HEAD_DIM
ATT_PROJ = ATT_WIDTH + 2 * KV_WIDTH
WINDOW = 128
ATT_SCALE = HEAD_DIM ** -0.5
ROPE_THETA = 500000.0
ROT_DIM = HEAD_DIM // 4
PAST_LEN = 16384

RWKV_WIDTH = 1024
RWKV_HEAD_DIM = 64
RWKV_HEADS = 16
DECAY_LORA = 64
AAA_LORA = 64
GATE_LORA = 160
RWKV_PROJ = 3 * RWKV_WIDTH + DECAY_LORA + AAA_LORA + GATE_LORA
RWKV_PROJ_PAD = 3456

N_MEM = 256
XATT_HEADS = 4
XATT_HEAD_DIM = 128
XATT_WIDTH = XATT_HEADS * XATT_HEAD_DIM

N_EXPERT_GROUPS = 8
EXPERTS_PER_GROUP = 8
N_EXPERTS = 64
TOP_K = 2
EXPERT_FF = D_MODEL // 4
MOE_BLOCK = 128

RMS_EPS = 1e-6
GN_EPS = 64e-5

LANES = 128
CHUNK = 64
VMEM_LIMIT = 56 * 1024 * 1024


def _cp(sem, vmem=VMEM_LIMIT):
    return pltpu.CompilerParams(dimension_semantics=sem, vmem_limit_bytes=vmem)


def _rms_rows(x, w):
    ms = jnp.mean(x * x, axis=-1, keepdims=True)
    return x * lax.rsqrt(ms + RMS_EPS) * w


def _split2(x):
    hi = x.astype(BF16)
    lo = (x - hi.astype(F32)).astype(BF16)
    return hi, lo


def _split3(x):
    h1 = x.astype(BF16)
    r1 = x - h1.astype(F32)
    h2 = r1.astype(BF16)
    h3 = (r1 - h2.astype(F32)).astype(BF16)
    return h1, h2, h3


def _dot(a, b):
    return jnp.dot(a, b, preferred_element_type=F32)


def _dot_nt(a, b):
    return lax.dot_general(a, b, (((1,), (1,)), ((), ())), preferred_element_type=F32)


def _group_sum(x, gmat):
    hi, lo = _split2(x)
    return _dot(hi, gmat) + _dot(lo, gmat)


def _head_indicator():
    r = lax.broadcasted_iota(jnp.int32, (LANES, LANES), 0) // HEAD_DIM
    c = lax.broadcasted_iota(jnp.int32, (LANES, LANES), 1) // HEAD_DIM
    return jnp.where(r == c, 1.0, 0.0).astype(BF16)


def _norm_mm_kernel(x_ref, lnw_ref, w_ref, o_ref, xn_ref):
    @pl.when(pl.program_id(1) == 0)
    def _():
        xn_ref[...] = _rms_rows(x_ref[...], lnw_ref[...]).astype(BF16)

    o_ref[...] = _dot(xn_ref[...], w_ref[...])


def norm_matmul(x, ln_w, w_bf16, *, tm, tn):
    m, k = x.shape
    n = w_bf16.shape[1]
    assert m % tm == 0 and n % tn == 0
    return pl.pallas_call(
        _norm_mm_kernel,
        grid=(m // tm, n // tn),
        in_specs=[
            pl.BlockSpec((tm, k), lambda i, j: (i, 0)),
            pl.BlockSpec((1, k), lambda i, j: (0, 0)),
            pl.BlockSpec((k, tn), lambda i, j: (0, j)),
        ],
        out_specs=pl.BlockSpec((tm, tn), lambda i, j: (i, j)),
        out_shape=jax.ShapeDtypeStruct((m, n), F32),
        scratch_shapes=[pltpu.VMEM((tm, k), BF16)],
        compiler_params=_cp(("parallel", "arbitrary")),
        name="norm_matmul",
    )(x, ln_w.reshape(1, k), w_bf16)


def _mm_res_kernel(*refs, n_lhs):
    a_refs = refs[:n_lhs]
    w_refs = refs[n_lhs:2 * n_lhs]
    res_ref = refs[2 * n_lhs]
    o_ref = refs[2 * n_lhs + 1]
    acc = res_ref[...]
    for a_ref, w_ref in zip(a_refs, w_refs):
        acc = acc + _dot(a_ref[...].astype(BF16), w_ref[...])
    o_ref[...] = acc


def matmul_residual(lhs_list, w_list, res, *, tm, tn):
    m, n = res.shape
    n_lhs = len(lhs_list)
    assert m % tm == 0 and n % tn == 0
    in_specs = [pl.BlockSpec((tm, a.shape[1]), lambda i, j: (i, 0)) for a in lhs_list]
    in_specs += [pl.BlockSpec((w.shape[0], tn), lambda i, j: (0, j)) for w in w_list]
    in_specs += [pl.BlockSpec((tm, tn), lambda i, j: (i, j))]
    return pl.pallas_call(
        functools.partial(_mm_res_kernel, n_lhs=n_lhs),
        grid=(m // tm, n // tn),
        in_specs=in_specs,
        out_specs=pl.BlockSpec((tm, tn), lambda i, j: (i, j)),
        out_shape=jax.ShapeDtypeStruct((m, n), F32),
        compiler_params=_cp(("parallel", "arbitrary")),
        name="matmul_residual",
    )(*lhs_list, *w_list, res)


def rope_tables(pos):
    half = ROT_DIM // 2
    inv = ROPE_THETA ** (-jnp.arange(half, dtype=F32) * 2.0 / ROT_DIM)
    ang = pos.astype(F32)[:, None] * inv[None, :]
    cos, sin = jnp.cos(ang), jnp.sin(ang)
    t = pos.shape[0]
    ones = jnp.ones((t, HEAD_DIM - ROT_DIM), F32)
    zeros = jnp.zeros((t, HEAD_DIM - ROT_DIM), F32)
    z8 = jnp.zeros((t, half), F32)
    cos_t = jnp.concatenate([cos, cos, ones], axis=1)
    sin_a = jnp.concatenate([z8, sin, zeros], axis=1)
    sin_b = jnp.concatenate([-sin, z8, zeros], axis=1)
    return tuple(jnp.concatenate([a, a], axis=1) for a in (cos_t, sin_a, sin_b))


def _norm_rope_chunk(x, w, cos_t, sin_a, sin_b, gmat):
    ms = _group_sum(x * x, gmat) * (1.0 / HEAD_DIM)
    xn = x * lax.rsqrt(ms + RMS_EPS) * w
    half = ROT_DIM // 2
    return (xn * cos_t + pltpu.roll(xn, half, axis=1) * sin_a
            + pltpu.roll(xn, LANES - half, axis=1) * sin_b)


def _norm_rope(x, w, tabs, gmat):
    chunks = [
        _norm_rope_chunk(x[:, c * LANES:(c + 1) * LANES], w, *tabs, gmat)
        for c in range(x.shape[1] // LANES)
    ]
    return chunks[0] if len(chunks) == 1 else jnp.concatenate(chunks, axis=1)


def _sink_softmax(s, sink):
    m = jnp.maximum(jnp.max(s, axis=-1, keepdims=True), sink)
    e = jnp.exp(s - m)
    return e / (jnp.sum(e, axis=-1, keepdims=True) + jnp.exp(sink - m))


def _swa_prompt_kernel(q_ref, kc_ref, vc_ref, kp_ref, vp_ref, cc_ref, sac_ref, sbc_ref,
                       cp_ref, sap_ref, sbp_ref, qw_ref, kw_ref, sink_ref, o_ref, kn_ref):
    n = pl.program_id(1)
    blk = q_ref.shape[0]
    gmat = _head_indicator()
    tabs_c = (cc_ref[...], sac_ref[...], sbc_ref[...])
    tabs_p = (cp_ref[...], sap_ref[...], sbp_ref[...])
    q = _norm_rope(q_ref[...], qw_ref[...], tabs_c, gmat)
    k_cur = _norm_rope(kc_ref[...], kw_ref[...], tabs_c, gmat)
    k_prev = _norm_rope(kp_ref[...], kw_ref[...], tabs_p, gmat)
    kn_ref[...] = k_cur
    k_all = jnp.concatenate([k_prev, k_cur], axis=0).astype(BF16)
    v_all = jnp.concatenate([vp_ref[...], vc_ref[...]], axis=0).astype(BF16)

    qi = lax.broadcasted_iota(jnp.int32, (blk, 2 * blk), 0) + blk
    si = lax.broadcasted_iota(jnp.int32, (blk, 2 * blk), 1)
    rel = qi - si
    valid = (rel >= 0) & (rel <= WINDOW) & ((n > 0) | (si >= blk))

    for kv in range(ATT_KV_HEADS):
        k_h = k_all[:, kv * HEAD_DIM:(kv + 1) * HEAD_DIM]
        v_h = v_all[:, kv * HEAD_DIM:(kv + 1) * HEAD_DIM]
        heads = [kv * ATT_GROUP + g for g in range(ATT_GROUP)]
        q_g = jnp.concatenate(
            [q[:, h * HEAD_DIM:(h + 1) * HEAD_DIM] for h in heads], axis=0).astype(BF16)
        s = _dot_nt(q_g, k_h) * ATT_SCALE
        probs = []
        for g, h in enumerate(heads):
            s_h = jnp.where(valid, s[g * blk:(g + 1) * blk], -jnp.inf)
            probs.append(_sink_softmax(s_h, sink_ref[h]))
        p = jnp.concatenate(probs, axis=0).astype(BF16)
        o = _dot(p, v_h)
        for g, h in enumerate(heads):
            o_ref[:, h * HEAD_DIM:(h + 1) * HEAD_DIM] = o[g * blk:(g + 1) * blk]


def swa_prompt(pa, tabs, q_norm_w, k_norm_w, sinks):
    b, t, _ = pa.shape
    blk = WINDOW
    nb = t // blk
    qb, kb, vb = 0, ATT_WIDTH // KV_WIDTH, ATT_WIDTH // KV_WIDTH + 1
    cur = lambda i, n, *_: (i, n, 0)
    tab_cur = pl.BlockSpec((blk, LANES), lambda i, n: (n, 0))
    tab_prev = pl.BlockSpec((blk, LANES), lambda i, n: (jnp.maximum(n - 1, 0), 0))
    qw = jnp.tile(q_norm_w.reshape(1, HEAD_DIM), (1, 2))
    kw = jnp.tile(k_norm_w.reshape(1, HEAD_DIM), (1, 2))
    return pl.pallas_call(
        _swa_prompt_kernel,
        grid=(b, nb),
        in_specs=[
            pl.BlockSpec((None, blk, ATT_WIDTH), lambda i, n: (i, n, qb)),
            pl.BlockSpec((None, blk, KV_WIDTH), lambda i, n: (i, n, kb)),
            pl.BlockSpec((None, blk, KV_WIDTH), lambda i, n: (i, n, vb)),
            pl.BlockSpec((None, blk, KV_WIDTH), lambda i, n: (i, jnp.maximum(n - 1, 0), kb)),
            pl.BlockSpec((None, blk, KV_WIDTH), lambda i, n: (i, jnp.maximum(n - 1, 0), vb)),
            tab_cur, tab_cur, tab_cur, tab_prev, tab_prev, tab_prev,
            pl.BlockSpec((1, LANES), lambda i, n: (0, 0)),
            pl.BlockSpec((1, LANES), lambda i, n: (0, 0)),
            pl.BlockSpec(memory_space=pltpu.SMEM),
        ],
        out_specs=[
            pl.BlockSpec((None, blk, ATT_WIDTH), cur),
            pl.BlockSpec((None, blk, KV_WIDTH), cur),
        ],
        out_shape=[
            jax.ShapeDtypeStruct((b, t, ATT_WIDTH), F32),
            jax.ShapeDtypeStruct((b, t, KV_WIDTH), F32),
        ],
        compiler_params=_cp(("parallel", "arbitrary")),
        name="swa_prompt",
    )(pa, pa, pa, pa, pa, *tabs, *tabs, qw, kw, sinks)


def _qk_norm_rope_kernel(x_ref, w_ref, c_ref, sa_ref, sb_ref, o_ref):
    gmat = _head_indicator()
    tabs = (c_ref[...], sa_ref[...], sb_ref[...])
    for c in range(x_ref.shape[1] // LANES):
        sl = slice(c * LANES, (c + 1) * LANES)
        o_ref[:, sl] = _norm_rope_chunk(x_ref[:, sl], w_ref[:, sl], *tabs, gmat)


def qk_norm_rope(x, w_row, tabs):
    m, w = x.shape
    full = lambda *shape: pl.BlockSpec(shape, lambda: (0,) * len(shape))
    return pl.pallas_call(
        _qk_norm_rope_kernel,
        in_specs=[full(m, w), full(1, w), full(1, LANES), full(1, LANES), full(1, LANES)],
        out_specs=full(m, w),
        out_shape=jax.ShapeDtypeStruct((m, w), F32),
        name="qk_norm_rope",
    )(x, w_row, *tabs)


def _swa_decode_kernel(q_ref, kn_ref, vn_ref, ck_ref, cv_ref, sink_ref, o_ref, kw_ref, vw_ref):
    bb = q_ref.shape[0]
    nbuf = ck_ref.shape[1]
    row_kv = lax.broadcasted_iota(jnp.int32, (ATT_HEADS, KV_WIDTH), 0) // ATT_GROUP
    lane_kv = lax.broadcasted_iota(jnp.int32, (ATT_HEADS, KV_WIDTH), 1) // HEAD_DIM
    own = row_kv == lane_kv
    sink = sink_ref[...]
    for b in range(bb):
        q2 = q_ref[b]
        q_exp = jnp.where(own, jnp.concatenate([q2] * ATT_KV_HEADS, axis=1), 0.0)
        k_new, v_new = kn_ref[b], vn_ref[b]
        k_buf, v_buf = ck_ref[b], cv_ref[b]
        s_buf = _dot_nt(q_exp.astype(BF16), k_buf.astype(BF16)) * ATT_SCALE
        s_new = jnp.sum(q_exp * k_new, axis=-1, keepdims=True) * ATT_SCALE
        m = jnp.maximum(jnp.maximum(jnp.max(s_buf, axis=-1, keepdims=True), s_new), sink)
        e_buf = jnp.exp(s_buf - m)
        e_new = jnp.exp(s_new - m)
        inv = 1.0 / (jnp.sum(e_buf, axis=-1, keepdims=True) + e_new + jnp.exp(sink - m))
        o = _dot((e_buf * inv).astype(BF16), v_buf.astype(BF16)) + (e_new * inv) * v_new
        o = jnp.where(own, o, 0.0)
        o_ref[b] = (o[:, 0:HEAD_DIM] + o[:, HEAD_DIM:2 * HEAD_DIM]
                    + o[:, 2 * HEAD_DIM:3 * HEAD_DIM] + o[:, 3 * HEAD_DIM:4 * HEAD_DIM])
        kw_ref[b, 0:nbuf - 1, :] = k_buf[1:nbuf]
        kw_ref[b, nbuf - 1:nbuf, :] = k_new
        vw_ref[b, 0:nbuf - 1, :] = v_buf[1:nbuf]
        vw_ref[b, nbuf - 1:nbuf, :] = v_new


def swa_decode(q, k_new, v_new, cache_k, cache_v, sinks, *, bb=8):
    b, nbuf, _ = cache_k.shape
    blk3 = lambda s1, s2: pl.BlockSpec((bb, s1, s2), lambda i: (i, 0, 0))
    return pl.pallas_call(
        _swa_decode_kernel,
        grid=(b // bb,),
        in_specs=[
            blk3(ATT_HEADS, HEAD_DIM), blk3(1, KV_WIDTH), blk3(1, KV_WIDTH),
            blk3(nbuf, KV_WIDTH), blk3(nbuf, KV_WIDTH),
            pl.BlockSpec((ATT_HEADS, 1), lambda i: (0, 0)),
        ],
        out_specs=[blk3(ATT_HEADS, HEAD_DIM), blk3(nbuf, KV_WIDTH), blk3(nbuf, KV_WIDTH)],
        out_shape=[
            jax.ShapeDtypeStruct((b, ATT_HEADS, HEAD_DIM), F32),
            jax.ShapeDtypeStruct((b, nbuf, KV_WIDTH), F32),
            jax.ShapeDtypeStruct((b, nbuf, KV_WIDTH), F32),
        ],
        compiler_params=_cp(("parallel",)),
        name="swa_decode",
    )(q, k_new, v_new, cache_k, cache_v, sinks.reshape(ATT_HEADS, 1))


def _head_rms_kernel(x_ref, w_ref, o_ref):
    for h in range(x_ref.shape[1] // XATT_HEAD_DIM):
        sl = slice(h * XATT_HEAD_DIM, (h + 1) * XATT_HEAD_DIM)
        o_ref[:, sl] = _rms_rows(x_ref[:, sl], w_ref[...])


def head_rms(x, w):
    m, wd = x.shape
    return pl.pallas_call(
        _head_rms_kernel,
        in_specs=[pl.BlockSpec((m, wd), lambda: (0, 0)),
                  pl.BlockSpec((1, XATT_HEAD_DIM), lambda: (0, 0))],
        out_specs=pl.BlockSpec((m, wd), lambda: (0, 0)),
        out_shape=jax.ShapeDtypeStruct((m, wd), F32),
        name="head_rms",
    )(x, w.reshape(1, XATT_HEAD_DIM))


def _xattn_prompt_kernel(q_ref, k_ref, v_ref, w_ref, o_ref):
    scale = 1.0 / math.sqrt(XATT_HEAD_DIM)
    for h in range(XATT_HEADS):
        sl = slice(h * XATT_HEAD_DIM, (h + 1) * XATT_HEAD_DIM)
        qn = _rms_rows(q_ref[:, sl], w_ref[...]).astype(BF16)
        s = _dot_nt(qn, k_ref[:, sl].astype(BF16)) * scale
        e = jnp.exp(s - jnp.max(s, axis=-1, keepdims=True))
        p = e / jnp.sum(e, axis=-1, keepdims=True)
        o_ref[:, sl] = _dot(p.astype(BF16), v_ref[:, sl].astype(BF16))


def xattn_prompt(q, mem_k, mem_v, xq_norm_w, *, tq=512):
    b, t, w = q.shape
    n_mem = mem_k.shape[1]
    return pl.pallas_call(
        _xattn_prompt_kernel,
        grid=(b, t // tq),
        in_specs=[
            pl.BlockSpec((None, tq, w), lambda i, j: (i, j, 0)),
            pl.BlockSpec((None, n_mem, w), lambda i, j: (i, 0, 0)),
            pl.BlockSpec((None, n_mem, w), lambda i, j: (i, 0, 0)),
            pl.BlockSpec((1, XATT_HEAD_DIM), lambda i, j: (0, 0)),
        ],
        out_specs=pl.BlockSpec((None, tq, w), lambda i, j: (i, j, 0)),
        out_shape=jax.ShapeDtypeStruct((b, t, w), F32),
        compiler_params=_cp(("parallel", "arbitrary")),
        name="xattn_prompt",
    )(q, mem_k, mem_v, xq_norm_w.reshape(1, XATT_HEAD_DIM))


def _xattn_decode_kernel(q_ref, k_ref, v_ref, w_ref, o_ref):
    bb, rows, _ = q_ref.shape
    scale = 1.0 / math.sqrt(XATT_HEAD_DIM)
    row_h = lax.broadcasted_iota(jnp.int32, (rows, XATT_WIDTH), 0)
    lane_h = lax.broadcasted_iota(jnp.int32, (rows, XATT_WIDTH), 1) // XATT_HEAD_DIM
    own = row_h == lane_h
    for b in range(bb):
        qn = _rms_rows(q_ref[b], w_ref[...])
        q_exp = jnp.where(own, jnp.concatenate([qn] * XATT_HEADS, axis=1), 0.0)
        s = _dot_nt(q_exp.astype(BF16), k_ref[b].astype(BF16)) * scale
        e = jnp.exp(s - jnp.max(s, axis=-1, keepdims=True))
        p = e / jnp.sum(e, axis=-1, keepdims=True)
        o = _dot(p.astype(BF16), v_ref[b].astype(BF16))
        o_ref[b] = jnp.sum(jnp.where(own, o, 0.0), axis=0, keepdims=True)


def xattn_decode(q_pad, mem_k, mem_v, xq_norm_w, *, bb=8):
    b, rows, _ = q_pad.shape
    n_mem = mem_k.shape[1]
    return pl.pallas_call(
        _xattn_decode_kernel,
        grid=(b // bb,),
        in_specs=[
            pl.BlockSpec((bb, rows, XATT_HEAD_DIM), lambda i: (i, 0, 0)),
            pl.BlockSpec((bb, n_mem, XATT_WIDTH), lambda i: (i, 0, 0)),
            pl.BlockSpec((bb, n_mem, XATT_WIDTH), lambda i: (i, 0, 0)),
            pl.BlockSpec((1, XATT_HEAD_DIM), lambda i: (0, 0)),
        ],
        out_specs=pl.BlockSpec((bb, 1, XATT_WIDTH), lambda i: (i, 0, 0)),
        out_shape=jax.ShapeDtypeStruct((b, 1, XATT_WIDTH), F32),
        compiler_params=_cp(("parallel",)),
        name="xattn_decode",
    )(q_pad, mem_k, mem_v, xq_norm_w.reshape(1, XATT_HEAD_DIM))


LORA_OFF = 3 * RWKV_WIDTH
GATE_OFF = LORA_OFF + DECAY_LORA + AAA_LORA
GATE_PAD = RWKV_PROJ_PAD - GATE_OFF


def _sigmoid(x):
    return 1.0 / (1.0 + jnp.exp(-x))


def _per_chunk(fn, *arrays):
    w = arrays[0].shape[1]
    outs = [fn(*(a[:, c * LANES:(c + 1) * LANES] for a in arrays)) for c in range(w // LANES)]
    return jnp.concatenate(outs, axis=1)


def _rwkv_prep_core(pr, prev, mu, w0, a0, kk_w, ka_w, rk_w, w_lora, w_gate):
    c = RWKV_WIDTH
    gmat = _head_indicator()
    xm = pr + (prev - pr) * mu
    r, k, v = xm[:, 0:c], xm[:, c:2 * c], xm[:, 2 * c:3 * c]
    lora = xm[:, LORA_OFF:LORA_OFF + LANES]
    lane = lax.broadcasted_iota(jnp.int32, lora.shape, 1)
    lora_in = jnp.where(lane < DECAY_LORA, jnp.tanh(lora), lora)
    wa = _dot(lora_in.astype(BF16), w_lora)
    z = -(w0 + wa[:, 0:c])
    softplus = jnp.maximum(z, 0.0) + jnp.log(1.0 + jnp.exp(-jnp.abs(z)))
    log_decay = -jnp.exp(-softplus - 0.5)
    a = _sigmoid(a0 + wa[:, c:2 * c])
    g = _dot(_sigmoid(xm[:, GATE_OFF:GATE_OFF + GATE_PAD]).astype(BF16), w_gate)
    kk = k * kk_w
    norm = jnp.sqrt(_per_chunk(lambda t: _group_sum(t * t, gmat), kk))
    kk = kk / jnp.maximum(norm, 1e-12)
    kp = k * (1.0 + (a - 1.0) * ka_w)
    bonus = _per_chunk(lambda t: _group_sum(t, gmat), r * kp * rk_w) * v
    return r, log_decay, kp, v, kk, kk * a, bonus, g


def _rwkv_prep_seq_kernel(pr_ref, prev0_ref, mu_ref, w0_ref, a0_ref, kkw_ref, kaw_ref, rkw_ref,
                          wl_ref, wg_ref, *refs):
    out_refs, last_ref = refs[:-1], refs[-1]

    @pl.when(pl.program_id(1) == 0)
    def _():
        last_ref[...] = prev0_ref[...]

    pr = pr_ref[...]
    rows = pr.shape[0]
    row = lax.broadcasted_iota(jnp.int32, (rows, 1), 0)
    prev = jnp.where(row == 0, last_ref[...], pltpu.roll(pr, 1, axis=0))
    last_ref[...] = pr[rows - 1:rows, :]
    outs = _rwkv_prep_core(pr, prev, mu_ref[...], w0_ref[...], a0_ref[...], kkw_ref[...],
                           kaw_ref[...], rkw_ref[...], wl_ref[...], wg_ref[...])
    for o_ref, o in zip(out_refs, outs):
        o_ref[...] = o


def _rwkv_prep_tok_kernel(pr_ref, prev_ref, mu_ref, w0_ref, a0_ref, kkw_ref, kaw_ref, rkw_ref,
                          wl_ref, wg_ref, *out_refs):
    outs = _rwkv_prep_core(pr_ref[...], prev_ref[...], mu_ref[...], w0_ref[...], a0_ref[...],
                           kkw_ref[...], kaw_ref[...], rkw_ref[...], wl_ref[...], wg_ref[...])
    for o_ref, o in zip(out_refs, outs):
        o_ref[...] = o


def _rwkv_param_specs(index_map):
    c = RWKV_WIDTH
    shapes = [(1, RWKV_PROJ_PAD)] + [(1, c)] * 5 + [(LANES, 2 * c), (GATE_PAD, c)]
    return [pl.BlockSpec(s, index_map) for s in shapes]


def rwkv_prep_seq(pr, prev0, params, *, tm=256):
    b, t, wd = pr.shape
    c = RWKV_WIDTH
    out = jax.ShapeDtypeStruct((b, t, c), F32)
    return pl.pallas_call(
        _rwkv_prep_seq_kernel,
        grid=(b, t // tm),
        in_specs=[pl.BlockSpec((None, tm, wd), lambda i, j: (i, j, 0)),
                  pl.BlockSpec((None, 1, wd), lambda i, j: (i, 0, 0))]
        + _rwkv_param_specs(lambda i, j: (0, 0)),
        out_specs=[pl.BlockSpec((None, tm, c), lambda i, j: (i, j, 0))] * 8,
        out_shape=[out] * 8,
        scratch_shapes=[pltpu.VMEM((1, wd), F32)],
        compiler_params=_cp(("parallel", "arbitrary")),
        name="rwkv_prep_seq",
    )(pr, prev0, *params)


def rwkv_prep_tok(pr, prev, params):
    m, wd = pr.shape
    c = RWKV_WIDTH
    out = jax.ShapeDtypeStruct((m, c), F32)
    return pl.pallas_call(
        _rwkv_prep_tok_kernel,
        grid=(1,),
        in_specs=[pl.BlockSpec((m, wd), lambda i: (0, 0))] * 2
        + _rwkv_param_specs(lambda i: (0, 0)),
        out_specs=[pl.BlockSpec((m, c), lambda i: (0, 0))] * 8,
        out_shape=[out] * 8,
        compiler_params=_cp(("arbitrary",)),
        name="rwkv_prep_tok",
    )(pr, prev, *params)


def _dot_tn(a, b):
    return lax.dot_general(a, b, (((0,), (0,)), ((), ())), preferred_element_type=F32)


def _rwkv_scan_kernel(r_ref, ld_ref, kp_ref, v_ref, kk_ref, b_ref, y_ref, s_out_ref, s_ref):
    @pl.when(pl.program_id(1) == 0)
    def _():
        s_ref[...] = jnp.zeros_like(s_ref)

    n = CHUNK
    ti = lax.broadcasted_iota(jnp.int32, (n, n), 0)
    si = lax.broadcasted_iota(jnp.int32, (n, n), 1)
    strict, incl = si < ti, si <= ti
    tri = jnp.where(incl, 1.0, 0.0).astype(BF16)

    ld = ld_ref[...]
    l1, l2, l3 = _split3(ld)
    lc = _dot(tri, l1) + _dot(tri, l2) + _dot(tri, l3)
    lc_end = lc[n - 1:n, :]
    e_neg = jnp.exp(-lc)
    kk, b, kp = kk_ref[...], b_ref[...], kp_ref[...]
    a_t = (-kk * jnp.exp(lc - ld)).astype(BF16)
    b_t = (b * e_neg).astype(BF16)
    k_t = (kp * e_neg).astype(BF16)
    r_t = (r_ref[...] * jnp.exp(lc)).astype(BF16)
    to_end = jnp.exp(lc_end - lc)
    b_e = (b * to_end).astype(BF16)
    k_e = (kp * to_end).astype(BF16)
    v_b = v_ref[...].astype(BF16)
    g_end = jnp.exp(lc_end)

    heads = range(RWKV_HEADS)
    sl = [slice(h * n, (h + 1) * n) for h in heads]
    gm = [_dot_nt(jnp.concatenate([a_t[:, sl[h]], r_t[:, sl[h]]], axis=0),
                  jnp.concatenate([b_t[:, sl[h]], k_t[:, sl[h]]], axis=0)) for h in heads]
    n_ab = [jnp.where(strict, gm[h][0:n, 0:n], 0.0).astype(BF16) for h in heads]
    l_ak = [jnp.where(strict, gm[h][0:n, n:2 * n], 0.0).astype(BF16) for h in heads]
    p_rb = [jnp.where(incl, gm[h][n:2 * n, 0:n], 0.0).astype(BF16) for h in heads]
    p_rk = [jnp.where(incl, gm[h][n:2 * n, n:2 * n], 0.0).astype(BF16) for h in heads]
    s0 = [s_ref[h] for h in heads]
    s0_b = [s0[h].astype(BF16) for h in heads]
    u = [_dot_nt(a_t[:, sl[h]], s0_b[h]) + _dot(l_ak[h], v_b[:, sl[h]]) for h in heads]
    pw = n_ab
    for step in range(6):
        u = [u[h] + _dot(pw[h], u[h].astype(BF16)) for h in heads]
        if step < 5:
            pw = [_dot(pw[h], pw[h]).astype(BF16) for h in heads]
    u_b = [u[h].astype(BF16) for h in heads]
    y = [_dot_nt(r_t[:, sl[h]], s0_b[h]) + _dot(p_rb[h], u_b[h]) + _dot(p_rk[h], v_b[:, sl[h]])
         for h in heads]
    s_new = [s0[h] * g_end[:, sl[h]] + _dot_tn(
        jnp.concatenate([u_b[h], v_b[:, sl[h]]], axis=0),
        jnp.concatenate([b_e[:, sl[h]], k_e[:, sl[h]]], axis=0)) for h in heads]
    for h in heads:
        y_ref[:, sl[h]] = y[h]
        s_ref[h] = s_new[h]
        s_out_ref[h] = s_new[h]


def rwkv_scan(r, ld, kp, v, kk, b):
    bsz, t, c = r.shape
    blk = pl.BlockSpec((None, CHUNK, c), lambda i, j: (i, j, 0))
    st = pl.BlockSpec((None, RWKV_HEADS, RWKV_HEAD_DIM, RWKV_HEAD_DIM), lambda i, j: (i, 0, 0, 0))
    return pl.pallas_call(
        _rwkv_scan_kernel,
        grid=(bsz, t // CHUNK),
        in_specs=[blk] * 6,
        out_specs=[blk, st],
        out_shape=[jax.ShapeDtypeStruct((bsz, t, c), F32),
                   jax.ShapeDtypeStruct((bsz, RWKV_HEADS, RWKV_HEAD_DIM, RWKV_HEAD_DIM), F32)],
        scratch_shapes=[pltpu.VMEM((RWKV_HEADS, RWKV_HEAD_DIM, RWKV_HEAD_DIM), F32)],
        compiler_params=_cp(("parallel", "arbitrary")),
        name="rwkv_scan",
    )(r, ld, kp, v, kk, b)


STEP_UNROLL = 8


def _rwkv_step_kernel(r_ref, ld_ref, kp_ref, v_ref, kk_ref, b_ref, s_ref, y_ref, s_out_ref):
    n = RWKV_HEAD_DIM
    eye = lax.broadcasted_iota(jnp.int32, (n, n), 0) == lax.broadcasted_iota(jnp.int32, (n, n), 1)

    def body(i, carry):
        gs = [i * STEP_UNROLL + j for j in range(STEP_UNROLL)]
        s = [s_ref[g] for g in gs]
        sa = [jnp.sum(s[j] * (-kk_ref[g]), axis=1, keepdims=True) for j, g in enumerate(gs)]
        v_col = [jnp.sum(jnp.where(eye, v_ref[g], 0.0), axis=1, keepdims=True) for g in gs]
        s_new = [s[j] * jnp.exp(ld_ref[g]) + sa[j] * b_ref[g] + v_col[j] * kp_ref[g]
                 for j, g in enumerate(gs)]
        y_col = [jnp.sum(s_new[j] * r_ref[g], axis=1, keepdims=True) for j, g in enumerate(gs)]
        for j, g in enumerate(gs):
            y_ref[g] = jnp.sum(jnp.where(eye, y_col[j], 0.0), axis=0, keepdims=True)
            s_out_ref[g] = s_new[j]
        return carry

    lax.fori_loop(0, s_ref.shape[0] // STEP_UNROLL, body, 0)


def rwkv_step(r, ld, kp, v, kk, b, state, *, gb=64):
    g = state.shape[0]
    n = RWKV_HEAD_DIM
    row = pl.BlockSpec((gb, 1, n), lambda i: (i, 0, 0))
    st = pl.BlockSpec((gb, n, n), lambda i: (i, 0, 0))
    return pl.pallas_call(
        _rwkv_step_kernel,
        grid=(g // gb,),
        in_specs=[row] * 6 + [st],
        out_specs=[row, st],
        out_shape=[jax.ShapeDtypeStruct((g, 1, n), F32), jax.ShapeDtypeStruct((g, n, n), F32)],
        compiler_params=_cp(("parallel",)),
        name="rwkv_step",
    )(r, ld, kp, v, kk, b, state)


def _rwkv_post_kernel(y_ref, bonus_ref, g_ref, lnw_ref, lnb_ref, o_ref):
    gmat = _head_indicator()
    inv = 1.0 / RWKV_HEAD_DIM
    for c in range(y_ref.shape[1] // LANES):
        sl = slice(c * LANES, (c + 1) * LANES)
        y = y_ref[:, sl]
        d = y - _group_sum(y, gmat) * inv
        var = _group_sum(d * d, gmat) * inv
        yn = d * lax.rsqrt(var + GN_EPS) * lnw_ref[:, sl] + lnb_ref[:, sl]
        o_ref[:, sl] = (yn + bonus_ref[:, sl]) * g_ref[:, sl]


def rwkv_post(y, bonus, g, ln_w, ln_b, *, tm):
    m, c = y.shape
    blk = pl.BlockSpec((tm, c), lambda i: (i, 0))
    vec = pl.BlockSpec((1, c), lambda i: (0, 0))
    return pl.pallas_call(
        _rwkv_post_kernel,
        grid=(m // tm,),
        in_specs=[blk, blk, blk, vec, vec],
        out_specs=blk,
        out_shape=jax.ShapeDtypeStruct((m, c), F32),
        compiler_params=_cp(("parallel",)),
        name="rwkv_post",
    )(y, bonus, g, ln_w.reshape(1, c), ln_b.reshape(1, c))


ROUTER_LANES = LANES


def _router_kernel(h_ref, lnw_ref, whi_ref, wlo_ref, bias_ref, u_ref, idx_ref, gate_ref):
    u = _rms_rows(h_ref[...], lnw_ref[...])
    u_ref[...] = u
    u_hi, u_lo = _split2(u)
    w_hi = whi_ref[...]
    logits = _dot(u_hi, w_hi) + _dot(u_lo, w_hi) + _dot(u_hi, wlo_ref[...]) + bias_ref[...]
    lane = lax.broadcasted_iota(jnp.int32, logits.shape, 1)
    neg = -jnp.inf

    def first_max(x):
        m = jnp.max(x, axis=1, keepdims=True)
        return m, jnp.min(jnp.where(x == m, lane, ROUTER_LANES), axis=1, keepdims=True)

    gl = jnp.where(lane < N_EXPERT_GROUPS, logits, neg)
    g_max, g_idx = first_max(gl)
    g_gate = 1.0 / jnp.sum(jnp.exp(gl - g_max), axis=1, keepdims=True)
    lo = N_EXPERT_GROUPS + g_idx * EXPERTS_PER_GROUP
    el = jnp.where((lane >= lo) & (lane < lo + EXPERTS_PER_GROUP), logits, neg)
    v1, i1 = first_max(el)
    v2, i2 = first_max(jnp.where(lane == i1, neg, el))
    e2 = jnp.exp(v2 - v1)
    w1 = g_gate / (1.0 + e2)
    w2 = g_gate * e2 / (1.0 + e2)
    idx_ref[...] = jnp.where(lane == 0, i1 - N_EXPERT_GROUPS,
                             jnp.where(lane == 1, i2 - N_EXPERT_GROUPS, 0))
    gate_ref[...] = jnp.where(lane == 0, w1, jnp.where(lane == 1, w2, 0.0))


def moe_router(h, ln_w, w_hi, w_lo, bias, *, tm):
    m, d = h.shape
    row = lambda w: pl.BlockSpec((tm, w), lambda i: (i, 0))
    const = lambda r, w: pl.BlockSpec((r, w), lambda i: (0, 0))
    return pl.pallas_call(
        _router_kernel,
        grid=(m // tm,),
        in_specs=[row(d), const(1, d), const(d, ROUTER_LANES), const(d, ROUTER_LANES),
                  const(1, ROUTER_LANES)],
        out_specs=[row(d), row(ROUTER_LANES), row(ROUTER_LANES)],
        out_shape=[jax.ShapeDtypeStruct((m, d), F32),
                   jax.ShapeDtypeStruct((m, ROUTER_LANES), jnp.int32),
                   jax.ShapeDtypeStruct((m, ROUTER_LANES), F32)],
        compiler_params=_cp(("parallel",)),
        name="moe_router",
    )(h, ln_w.reshape(1, d), w_hi, w_lo, bias)


def _moe_expert_kernel(bexp_ref, nused_ref, cnt_ref, tok_ref, tokn_ref, dst_ref, roww_ref, u_hbm,
                       wg_ref, wu_ref, wd_ref, y_hbm, xbuf, ybuf, wg_b, wu_b, wd_b, sem_in,
                       sem_out):
    i = pl.program_id(0)
    n_used = nused_ref[0]
    slot = lax.rem(i, 2)

    def gather_rows(idx_ref, dst_slot):
        for r in range(MOE_BLOCK):
            pltpu.make_async_copy(u_hbm.at[pl.ds(idx_ref[0, 0, r], 1)],
                                  xbuf.at[dst_slot, pl.ds(r, 1)], sem_in.at[dst_slot]).start()

    def row_out(r, s):
        return pltpu.make_async_copy(ybuf.at[s, pl.ds(r, 1)],
                                     y_hbm.at[pl.ds(dst_ref[0, 0, r], 1)], sem_out.at[s])

    def wait_rows(n_rows, make_copy):
        def body(r, carry):
            make_copy(r).wait()
            return carry
        lax.fori_loop(0, n_rows, body, 0)

    def wait_out(n_rows, s):
        wait_rows(n_rows, lambda r: pltpu.make_async_copy(
            ybuf.at[s, pl.ds(0, 1)], y_hbm.at[pl.ds(0, 1)], sem_out.at[s]))

    @pl.when(i == 0)
    def _():
        gather_rows(tok_ref, 0)

    @pl.when(i + 1 < n_used)
    def _():
        gather_rows(tokn_ref, 1 - slot)

    @pl.when(i < n_used)
    def _():
        @pl.when((i == 0) | (bexp_ref[i] != bexp_ref[jnp.maximum(i - 1, 0)]))
        def _():
            wg_b[...] = wg_ref[...].astype(BF16)
            wu_b[...] = wu_ref[...].astype(BF16)
            wd_b[...] = wd_ref[...].astype(BF16)

        wait_rows(MOE_BLOCK, lambda r: pltpu.make_async_copy(
            u_hbm.at[pl.ds(0, 1)], xbuf.at[slot, pl.ds(0, 1)], sem_in.at[slot]))
        x = xbuf[slot].astype(BF16)
        hg = _dot(x, wg_b[...])
        hu = _dot(x, wu_b[...])
        act = (hg * _sigmoid(hg) * hu).astype(BF16)
        y = _dot(act, wd_b[...]) * roww_ref[...]

        @pl.when(i >= 2)
        def _():
            wait_out(cnt_ref[jnp.maximum(i - 2, 0)], slot)

        ybuf[slot] = y
        cnt = cnt_ref[i]
        base = 0
        for run in (128, 64, 32, 16, 8, 4, 2, 1):
            if run > MOE_BLOCK:
                continue
            take = (cnt & run) != 0 if run < MOE_BLOCK else cnt >= MOE_BLOCK
            start = base

            @pl.when(take)
            def _(start=start, run=run):
                for r in range(run):
                    row_out(start + r, slot).start()

            base = base + jnp.where(take, run, 0)

        @pl.when(i == n_used - 1)
        def _():
            wait_out(cnt, slot)

            @pl.when(i >= 1)
            def _():
                wait_out(cnt_ref[jnp.maximum(i - 1, 0)], 1 - slot)


def moe_experts(u_all, row_tok, row_dst, row_w, block_exp, n_used, row_cnt, w_gate, w_up,
                w_down, n_assign):
    d = u_all.shape[1]
    n_blocks = row_tok.shape[0]
    ff = w_gate.shape[2]
    smem_blk = lambda off: pl.BlockSpec(
        (1, 1, MOE_BLOCK), lambda i, be, nu, rc: (jnp.minimum(i + off, n_blocks - 1), 0, 0),
        memory_space=pltpu.SMEM)
    grid_spec = pltpu.PrefetchScalarGridSpec(
        num_scalar_prefetch=3,
        grid=(n_blocks,),
        in_specs=[
            smem_blk(0), smem_blk(1), smem_blk(0),
            pl.BlockSpec((MOE_BLOCK, 1), lambda i, be, nu, rc: (i, 0)),
            pl.BlockSpec(memory_space=pl.ANY),
            pl.BlockSpec((None, d, ff), lambda i, be, nu, rc: (be[i], 0, 0)),
            pl.BlockSpec((None, d, ff), lambda i, be, nu, rc: (be[i], 0, 0)),
            pl.BlockSpec((None, ff, d), lambda i, be, nu, rc: (be[i], 0, 0)),
        ],
        out_specs=pl.BlockSpec(memory_space=pl.ANY),
        scratch_shapes=[
            pltpu.VMEM((2, MOE_BLOCK, d), F32), pltpu.VMEM((2, MOE_BLOCK, d), F32),
            pltpu.VMEM((d, ff), BF16), pltpu.VMEM((d, ff), BF16), pltpu.VMEM((ff, d), BF16),
            pltpu.SemaphoreType.DMA((2,)), pltpu.SemaphoreType.DMA((2,)),
        ],
    )
    return pl.pallas_call(
        _moe_expert_kernel,
        grid_spec=grid_spec,
        out_shape=jax.ShapeDtypeStruct((n_assign, d), F32),
        compiler_params=_cp(("arbitrary",)),
        name="moe_experts",
    )(block_exp, n_used, row_cnt, row_tok, row_tok, row_dst, row_w, u_all, w_gate, w_up, w_down)


def _moe_combine_kernel(h_ref, y0_ref, y1_ref, o_ref):
    o_ref[...] = h_ref[...] + (y0_ref[...] + y1_ref[...])


def moe_combine(h, y_slots, row_off, slot_stride, *, tm):
    m, d = h.shape
    assert row_off % tm == 0 and slot_stride % tm == 0
    off0, off1 = row_off // tm, (row_off + slot_stride) // tm
    return pl.pallas_call(
        _moe_combine_kernel,
        grid=(m // tm,),
        in_specs=[pl.BlockSpec((tm, d), lambda i: (i, 0)),
                  pl.BlockSpec((tm, d), lambda i: (i + off0, 0)),
                  pl.BlockSpec((tm, d), lambda i: (i + off1, 0))],
        out_specs=pl.BlockSpec((tm, d), lambda i: (i, 0)),
        out_shape=jax.ShapeDtypeStruct((m, d), F32),
        compiler_params=_cp(("parallel",)),
        name="moe_combine",
    )(h, y_slots, y_slots)


def moe_dispatch(e_idx, gates, slot_stride):
    m = e_idx.shape[0]
    a = m * TOP_K
    e_flat = e_idx.reshape(a)
    order = jnp.argsort(e_flat, stable=True).astype(jnp.int32)
    counts = jnp.sum(e_flat[:, None] == jnp.arange(N_EXPERTS, dtype=jnp.int32)[None, :],
                     axis=0, dtype=jnp.int32)
    pad_counts = (counts + MOE_BLOCK - 1) // MOE_BLOCK * MOE_BLOCK
    starts = jnp.cumsum(counts) - counts
    pad_ends = jnp.cumsum(pad_counts)
    pad_starts = pad_ends - pad_counts
    n_blocks = a // MOE_BLOCK + N_EXPERTS
    p = n_blocks * MOE_BLOCK
    n_used = (pad_ends[-1] // MOE_BLOCK).astype(jnp.int32)
    blk = jnp.arange(n_blocks, dtype=jnp.int32)
    blk_start = jnp.minimum(blk, n_used - 1) * MOE_BLOCK
    block_exp = jnp.minimum(jnp.searchsorted(pad_ends, blk_start, side='right'),
                            N_EXPERTS - 1).astype(jnp.int32)
    in_exp = blk * MOE_BLOCK - pad_starts[block_exp]
    row_cnt = jnp.where(blk < n_used,
                        jnp.clip(counts[block_exp] - in_exp, 0, MOE_BLOCK), 0).astype(jnp.int32)
    lane = jnp.arange(MOE_BLOCK, dtype=jnp.int32)[None, :]
    valid = lane < row_cnt[:, None]
    src = jnp.clip((starts[block_exp] + in_exp)[:, None] + lane, 0, a - 1)
    assign = order[src]
    row_tok = jnp.where(valid, assign // TOP_K, 0)
    row_dst = jnp.where(valid, (assign % TOP_K) * slot_stride + assign // TOP_K, 0)
    row_w = jnp.where(valid, gates.reshape(a)[assign], 0.0)
    return (row_tok.reshape(n_blocks, 1, MOE_BLOCK), row_dst.reshape(n_blocks, 1, MOE_BLOCK),
            row_w.reshape(p, 1), block_exp, n_used.reshape(1), row_cnt)


def rwkv_params(rw_mu, rw_w0, rw_w2, rw_a0, rw_a2, rw_g2, rw_k_k, rw_k_a, rw_r_k):
    c = RWKV_WIDTH
    mu = jnp.pad(rw_mu, (0, RWKV_PROJ_PAD - RWKV_PROJ)).reshape(1, RWKV_PROJ_PAD)
    w_lora = jnp.zeros((LANES, 2 * c), F32)
    w_lora = w_lora.at[0:DECAY_LORA, 0:c].set(rw_w2).at[DECAY_LORA:LANES, c:2 * c].set(rw_a2)
    w_gate = jnp.pad(rw_g2, ((0, GATE_PAD - GATE_LORA), (0, 0)))
    vec = lambda x: x.reshape(1, c)
    return (mu, vec(rw_w0), vec(rw_a0), vec(rw_k_k), vec(rw_k_a), vec(rw_r_k),
            w_lora.astype(BF16), w_gate.astype(BF16))


def _token_tiles(m):
    return (1024, 512) if m % 1024 == 0 else (m, m)


def _dense_front(x2d, wts, tm):
    pa = norm_matmul(x2d, wts['ln1_w'], wts['w_att'], tm=tm, tn=512)
    pr = norm_matmul(x2d, wts['ln1_w'], wts['w_rw'], tm=tm, tn=RWKV_PROJ_PAD // 3)
    return pa, pr


def _dense_back(x2d, att2d, rw2d, wts, xattn_fn, tm, te):
    h1 = matmul_residual([att2d, rw2d], [wts['w_out_a'], wts['w_out_r']], x2d, tm=tm, tn=512)
    qx = norm_matmul(h1, wts['ln2_w'], wts['xq_w'], tm=tm, tn=XATT_WIDTH)
    ox = xattn_fn(qx)
    h2 = matmul_residual([ox], [wts['xo_w']], h1, tm=tm, tn=512)
    u, idx, gate = moe_router(h2, wts['ln3_w'], wts['router_hi'], wts['router_lo'],
                              wts['router_b'], tm=te)
    return h2, u, idx[:, :TOP_K], gate[:, :TOP_K]


def kernel(x_prompt, x_sample, cache_win_k, cache_win_v, state_wkv, state_shift, cache_mem_k, cache_mem_v, mem_prompt, ln1_w, w_in, q_norm_w, k_norm_w, attn_sinks, rw_mu, rw_w0, rw_w2, rw_a0, rw_a2, rw_g2, rw_k_k, rw_k_a, rw_r_k, rw_ln_w, rw_ln_b, w_out, ln2_w, mem_norm_w, xq_w, xkv_w, xq_norm_w, xk_norm_w, xo_w, ln3_w, router_group_w, router_group_b, router_expert_w, router_expert_b, exp_w_gate, exp_w_up, exp_w_down):
    assert w_in.shape[0] == 1, "single-layer stack"
    bp, seq, d = x_prompt.shape
    bs = x_sample.shape[0]
    mp = bp * seq
    c = RWKV_WIDTH

    router_w = jnp.concatenate(
        [router_group_w[0], router_expert_w[0],
         jnp.zeros((d, ROUTER_LANES - N_EXPERT_GROUPS - N_EXPERTS), F32)], axis=1)
    router_hi = router_w.astype(BF16)
    wts = {
        'ln1_w': ln1_w[0], 'ln2_w': ln2_w[0], 'ln3_w': ln3_w[0],
        'w_att': w_in[0][:, :ATT_PROJ].astype(BF16),
        'w_rw': jnp.pad(w_in[0][:, ATT_PROJ:],
                        ((0, 0), (0, RWKV_PROJ_PAD - RWKV_PROJ))).astype(BF16),
        'w_out_a': w_out[0][:ATT_WIDTH].astype(BF16),
        'w_out_r': w_out[0][ATT_WIDTH:].astype(BF16),
        'xq_w': xq_w[0].astype(BF16), 'xo_w': xo_w[0].astype(BF16),
        'router_hi': router_hi,
        'router_lo': (router_w - router_hi.astype(F32)).astype(BF16),
        'router_b': jnp.pad(jnp.concatenate([router_group_b[0], router_expert_b[0]]),
                            (0, ROUTER_LANES - N_EXPERT_GROUPS - N_EXPERTS)).reshape(1, -1),
    }
    rw_par = rwkv_params(rw_mu[0], rw_w0[0], rw_w2[0], rw_a0[0], rw_a2[0], rw_g2[0],
                         rw_k_k[0], rw_k_a[0], rw_r_k[0])

    tm_p, te_p = _token_tiles(mp)
    xp = x_prompt.reshape(mp, d)
    pa, pr = _dense_front(xp, wts, tm_p)
    pa3 = pa.reshape(bp, seq, ATT_PROJ)
    pr3 = pr.reshape(bp, seq, RWKV_PROJ_PAD)
    tabs_p = rope_tables(jnp.arange(seq, dtype=jnp.int32))
    att_p, kn_p = swa_prompt(pa3, tabs_p, q_norm_w[0], k_norm_w[0], attn_sinks[0])
    prep = rwkv_prep_seq(pr3, jnp.zeros((bp, 1, RWKV_PROJ_PAD), F32), rw_par)
    r, ld, kp, v, kk, b, bonus, g = prep
    y_p, wkv_p = rwkv_scan(r, ld, kp, v, kk, b)
    rw_p = rwkv_post(y_p.reshape(mp, c), bonus.reshape(mp, c), g.reshape(mp, c),
                     rw_ln_w[0], rw_ln_b[0], tm=te_p)

    n_mem = mem_prompt.shape[1]
    kv_mem = norm_matmul(mem_prompt.reshape(bp * n_mem, d), mem_norm_w[0],
                         xkv_w[0].astype(BF16), tm=bp * n_mem, tn=512)
    mem_k = head_rms(kv_mem[:, :XATT_WIDTH], xk_norm_w[0])
    mem_v = kv_mem[:, XATT_WIDTH:]
    mem_k3 = mem_k.reshape(bp, n_mem, XATT_WIDTH)
    mem_v3 = mem_v.reshape(bp, n_mem, XATT_WIDTH)

    def xattn_p(qx):
        return xattn_prompt(qx.reshape(bp, seq, XATT_WIDTH), mem_k3, mem_v3,
                            xq_norm_w[0]).reshape(mp, XATT_WIDTH)

    h2_p, u_p, idx_p, gate_p = _dense_back(xp, att_p.reshape(mp, ATT_WIDTH), rw_p, wts,
                                           xattn_p, tm_p, te_p)

    tm_s, te_s = _token_tiles(bs)
    xs = x_sample.reshape(bs, d)
    sa, sr = _dense_front(xs, wts, tm_s)
    tabs_s = rope_tables(PAST_LEN + jnp.arange(1, dtype=jnp.int32))
    qk_w = jnp.concatenate([jnp.tile(q_norm_w[0], ATT_HEADS),
                            jnp.tile(k_norm_w[0], ATT_KV_HEADS)]).reshape(1, -1)
    qk = qk_norm_rope(sa[:, :ATT_WIDTH + KV_WIDTH], qk_w, tabs_s)
    nbuf = cache_win_k.shape[2]
    att_s, win_k, win_v = swa_decode(
        qk[:, :ATT_WIDTH].reshape(bs, ATT_HEADS, HEAD_DIM),
        qk[:, ATT_WIDTH:].reshape(bs, 1, KV_WIDTH),
        sa[:, ATT_WIDTH + KV_WIDTH:].reshape(bs, 1, KV_WIDTH),
        cache_win_k[0].reshape(bs, nbuf, KV_WIDTH), cache_win_v[0].reshape(bs, nbuf, KV_WIDTH),
        attn_sinks[0])
    shift_prev = jnp.pad(state_shift[0], ((0, 0), (0, RWKV_PROJ_PAD - RWKV_PROJ)))
    r, ld, kp, v, kk, b, bonus, g = rwkv_prep_tok(sr, shift_prev, rw_par)
    gh = bs * RWKV_HEADS
    rows = [t.reshape(gh, 1, RWKV_HEAD_DIM) for t in (r, ld, kp, v, kk, b)]
    y_s, wkv_s = rwkv_step(*rows, state_wkv[0].reshape(gh, RWKV_HEAD_DIM, RWKV_HEAD_DIM))
    rw_s = rwkv_post(y_s.reshape(bs, c), bonus, g, rw_ln_w[0], rw_ln_b[0], tm=te_s)
    cmk = cache_mem_k[0].reshape(bs, n_mem, XATT_WIDTH)
    cmv = cache_mem_v[0].reshape(bs, n_mem, XATT_WIDTH)

    def xattn_s(qx):
        q_pad = jnp.pad(qx.reshape(bs, XATT_HEADS, XATT_HEAD_DIM), ((0, 0), (0, 4), (0, 0)))
        return xattn_decode(q_pad, cmk, cmv, xq_norm_w[0]).reshape(bs, XATT_WIDTH)

    h2_s, u_s, idx_s, gate_s = _dense_back(xs, att_s.reshape(bs, ATT_WIDTH), rw_s, wts,
                                           xattn_s, tm_s, te_s)

    m_all = mp + bs
    slot_stride = m_all
    tc = math.gcd(mp, bs, 512)
    u_all = jnp.concatenate([u_p, u_s], axis=0)
    e_idx = jnp.concatenate([idx_p, idx_s], axis=0)
    gates = jnp.concatenate([gate_p, gate_s], axis=0)
    row_tok, row_dst, row_w, block_exp, n_used, row_cnt = moe_dispatch(e_idx, gates, slot_stride)
    y_slots = moe_experts(u_all, row_tok, row_dst, row_w, block_exp, n_used, row_cnt,
                          exp_w_gate[0], exp_w_up[0], exp_w_down[0], TOP_K * slot_stride)
    out_p = moe_combine(h2_p, y_slots, 0, slot_stride, tm=tc)
    out_s = moe_combine(h2_s, y_slots, mp, slot_stride, tm=tc)

    win = min(WINDOW, seq)
    kv_shape = (1, bp, win, ATT_KV_HEADS, HEAD_DIM)
    return (
        out_p.reshape(bp, seq, d),
        out_s.reshape(bs, 1, d),
        kn_p[:, seq - win:].reshape(kv_shape),
        pa3[:, seq - win:, ATT_WIDTH + KV_WIDTH:].reshape(kv_shape),
        wkv_p[None],
        pr3[:, seq - 1, :RWKV_PROJ][None],
        mem_k3.reshape(1, bp, n_mem, XATT_HEADS, XATT_HEAD_DIM),
        mem_v3.reshape(1, bp, n_mem, XATT_HEADS, XATT_HEAD_DIM),
        win_k.reshape(1, bs, nbuf, ATT_KV_HEADS, HEAD_DIM),
        win_v.reshape(1, bs, nbuf, ATT_KV_HEADS, HEAD_DIM),
        wkv_s.reshape(1, bs, RWKV_HEADS, RWKV_HEAD_DIM, RWKV_HEAD_DIM),
        sr[:, :RWKV_PROJ].reshape(1, bs, RWKV_PROJ),
    )
```

```python
import functools
import math

import jax
import jax.numpy as jnp
from jax import lax
from jax.experimental import pallas as pl
from jax.experimental.pallas import tpu as pltpu

F32 = jnp.float32
BF16 = jnp.bfloat16

D_MODEL = 2048
HEAD_DIM = 64
ATT_HEADS = 16
ATT_KV_HEADS = 4
ATT_GROUP = ATT_HEADS // ATT_KV_HEADS
ATT_WIDTH = ATT_HEADS * HEAD_DIM
KV_WIDTH = ATT_KV_HEADS * HEAD_DIM
ATT_PROJ = ATT_WIDTH + 2 * KV_WIDTH
WINDOW = 128
ATT_SCALE = HEAD_DIM ** -0.5
ROPE_THETA = 500000.0
ROT_DIM = HEAD_DIM // 4
PAST_LEN = 16384

RWKV_WIDTH = 1024
RWKV_HEAD_DIM = 64
RWKV_HEADS = 16
DECAY_LORA = 64
AAA_LORA = 64
GATE_LORA = 160
RWKV_PROJ = 3 * RWKV_WIDTH + DECAY_LORA + AAA_LORA + GATE_LORA
RWKV_PROJ_PAD = 3456

N_MEM = 256
XATT_HEADS = 4
XATT_HEAD_DIM = 128
XATT_WIDTH = XATT_HEADS * XATT_HEAD_DIM

N_EXPERT_GROUPS = 8
EXPERTS_PER_GROUP = 8
N_EXPERTS = 64
TOP_K = 2
EXPERT_FF = D_MODEL // 4
MOE_BLOCK = 128

RMS_EPS = 1e-6
GN_EPS = 64e-5

LANES = 128
CHUNK = 64
VMEM_LIMIT = 56 * 1024 * 1024


def _cp(sem, vmem=VMEM_LIMIT):
    return pltpu.CompilerParams(dimension_semantics=sem, vmem_limit_bytes=vmem)


def _rms_rows(x, w):
    ms = jnp.mean(x * x, axis=-1, keepdims=True)
    return x * lax.rsqrt(ms + RMS_EPS) * w


def _split2(x):
    hi = x.astype(BF16)
    lo = (x - hi.astype(F32)).astype(BF16)
    return hi, lo


def _split3(x):
    h1 = x.astype(BF16)
    r1 = x - h1.astype(F32)
    h2 = r1.astype(BF16)
    h3 = (r1 - h2.astype(F32)).astype(BF16)
    return h1, h2, h3


def _dot(a, b):
    return jnp.dot(a, b, preferred_element_type=F32)


def _dot_nt(a, b):
    return lax.dot_general(a, b, (((1,), (1,)), ((), ())), preferred_element_type=F32)


def _group_sum(x, gmat):
    hi, lo = _split2(x)
    return _dot(hi, gmat) + _dot(lo, gmat)


def _head_indicator():
    r = lax.broadcasted_iota(jnp.int32, (LANES, LANES), 0) // HEAD_DIM
    c = lax.broadcasted_iota(jnp.int32, (LANES, LANES), 1) // HEAD_DIM
    return jnp.where(r == c, 1.0, 0.0).astype(BF16)


def _norm_mm_kernel(x_ref, lnw_ref, w_ref, o_ref, xn_ref):
    @pl.when(pl.program_id(1) == 0)
    def _():
        xn_ref[...] = _rms_rows(x_ref[...], lnw_ref[...]).astype(BF16)

    o_ref[...] = _dot(xn_ref[...], w_ref[...])


def norm_matmul(x, ln_w, w_bf16, *, tm, tn):
    m, k = x.shape
    n = w_bf16.shape[1]
    assert m % tm == 0 and n % tn == 0
    return pl.pallas_call(
        _norm_mm_kernel,
        grid=(m // tm, n // tn),
        in_specs=[
            pl.BlockSpec((tm, k), lambda i, j: (i, 0)),
            pl.BlockSpec((1, k), lambda i, j: (0, 0)),
            pl.BlockSpec((k, tn), lambda i, j: (0, j)),
        ],
        out_specs=pl.BlockSpec((tm, tn), lambda i, j: (i, j)),
        out_shape=jax.ShapeDtypeStruct((m, n), F32),
        scratch_shapes=[pltpu.VMEM((tm, k), BF16)],
        compiler_params=_cp(("parallel", "arbitrary")),
        name="norm_matmul",
    )(x, ln_w.reshape(1, k), w_bf16)


def _mm_res_kernel(*refs, n_lhs):
    a_refs = refs[:n_lhs]
    w_refs = refs[n_lhs:2 * n_lhs]
    res_ref = refs[2 * n_lhs]
    o_ref = refs[2 * n_lhs + 1]
    acc = res_ref[...]
    for a_ref, w_ref in zip(a_refs, w_refs):
        acc = acc + _dot(a_ref[...].astype(BF16), w_ref[...])
    o_ref[...] = acc


def matmul_residual(lhs_list, w_list, res, *, tm, tn):
    m, n = res.shape
    n_lhs = len(lhs_list)
    assert m % tm == 0 and n % tn == 0
    in_specs = [pl.BlockSpec((tm, a.shape[1]), lambda i, j: (i, 0)) for a in lhs_list]
    in_specs += [pl.BlockSpec((w.shape[0], tn), lambda i, j: (0, j)) for w in w_list]
    in_specs += [pl.BlockSpec((tm, tn), lambda i, j: (i, j))]
    return pl.pallas_call(
        functools.partial(_mm_res_kernel, n_lhs=n_lhs),
        grid=(m // tm, n // tn),
        in_specs=in_specs,
        out_specs=pl.BlockSpec((tm, tn), lambda i, j: (i, j)),
        out_shape=jax.ShapeDtypeStruct((m, n), F32),
        compiler_params=_cp(("parallel", "arbitrary")),
        name="matmul_residual",
    )(*lhs_list, *w_list, res)


def rope_tables(pos):
    half = ROT_DIM // 2
    inv = ROPE_THETA ** (-jnp.arange(half, dtype=F32) * 2.0 / ROT_DIM)
    ang = pos.astype(F32)[:, None] * inv[None, :]
    cos, sin = jnp.cos(ang), jnp.sin(ang)
    t = pos.shape[0]
    ones = jnp.ones((t, HEAD_DIM - ROT_DIM), F32)
    zeros = jnp.zeros((t, HEAD_DIM - ROT_DIM), F32)
    z8 = jnp.zeros((t, half), F32)
    cos_t = jnp.concatenate([cos, cos, ones], axis=1)
    sin_a = jnp.concatenate([z8, sin, zeros], axis=1)
    sin_b = jnp.concatenate([-sin, z8, zeros], axis=1)
    return tuple(jnp.concatenate([a, a], axis=1) for a in (cos_t, sin_a, sin_b))


def _norm_rope_chunk(x, w, cos_t, sin_a, sin_b, gmat):
    ms = _group_sum(x * x, gmat) * (1.0 / HEAD_DIM)
    xn = x * lax.rsqrt(ms + RMS_EPS) * w
    half = ROT_DIM // 2
    return (xn * cos_t + pltpu.roll(xn, half, axis=1) * sin_a
            + pltpu.roll(xn, LANES - half, axis=1) * sin_b)


def _norm_rope(x, w, tabs, gmat):
    chunks = [
        _norm_rope_chunk(x[:, c * LANES:(c + 1) * LANES], w, *tabs, gmat)
        for c in range(x.shape[1] // LANES)
    ]
    return chunks[0] if len(chunks) == 1 else jnp.concatenate(chunks, axis=1)


def _sink_softmax(s, sink):
    m = jnp.maximum(jnp.max(s, axis=-1, keepdims=True), sink)
    e = jnp.exp(s - m)
    return e / (jnp.sum(e, axis=-1, keepdims=True) + jnp.exp(sink - m))


def _swa_prompt_kernel(q_ref, kc_ref, vc_ref, kp_ref, vp_ref, cc_ref, sac_ref, sbc_ref,
                       cp_ref, sap_ref, sbp_ref, qw_ref, kw_ref, sink_ref, o_ref, kn_ref):
    n = pl.program_id(1)
    blk = q_ref.shape[0]
    gmat = _head_indicator()
    tabs_c = (cc_ref[...], sac_ref[...], sbc_ref[...])
    tabs_p = (cp_ref[...], sap_ref[...], sbp_ref[...])
    q = _norm_rope(q_ref[...], qw_ref[...], tabs_c, gmat)
    k_cur = _norm_rope(kc_ref[...], kw_ref[...], tabs_c, gmat)
    k_prev = _norm_rope(kp_ref[...], kw_ref[...], tabs_p, gmat)
    kn_ref[...] = k_cur
    k_all = jnp.concatenate([k_prev, k_cur], axis=0).astype(BF16)
    v_all = jnp.concatenate([vp_ref[...], vc_ref[...]], axis=0).astype(BF16)

    qi = lax.broadcasted_iota(jnp.int32, (blk, 2 * blk), 0) + blk
    si = lax.broadcasted_iota(jnp.int32, (blk, 2 * blk), 1)
    rel = qi - si
    valid = (rel >= 0) & (rel <= WINDOW) & ((n > 0) | (si >= blk))

    for kv in range(ATT_KV_HEADS):
        k_h = k_all[:, kv * HEAD_DIM:(kv + 1) * HEAD_DIM]
        v_h = v_all[:, kv * HEAD_DIM:(kv + 1) * HEAD_DIM]
        heads = [kv * ATT_GROUP + g for g in range(ATT_GROUP)]
        q_g = jnp.concatenate(
            [q[:, h * HEAD_DIM:(h + 1) * HEAD_DIM] for h in heads], axis=0).astype(BF16)
        s = _dot_nt(q_g, k_h) * ATT_SCALE
        probs = []
        for g, h in enumerate(heads):
            s_h = jnp.where(valid, s[g * blk:(g + 1) * blk], -jnp.inf)
            probs.append(_sink_softmax(s_h, sink_ref[h]))
        p = jnp.concatenate(probs, axis=0).astype(BF16)
        o = _dot(p, v_h)
        for g, h in enumerate(heads):
            o_ref[:, h * HEAD_DIM:(h + 1) * HEAD_DIM] = o[g * blk:(g + 1) * blk]


def swa_prompt(pa, tabs, q_norm_w, k_norm_w, sinks):
    b, t, _ = pa.shape
    blk = WINDOW
    nb = t // blk
    qb, kb, vb = 0, ATT_WIDTH // KV_WIDTH, ATT_WIDTH // KV_WIDTH + 1
    cur = lambda i, n, *_: (i, n, 0)
    tab_cur = pl.BlockSpec((blk, LANES), lambda i, n: (n, 0))
    tab_prev = pl.BlockSpec((blk, LANES), lambda i, n: (jnp.maximum(n - 1, 0), 0))
    qw = jnp.tile(q_norm_w.reshape(1, HEAD_DIM), (1, 2))
    kw = jnp.tile(k_norm_w.reshape(1, HEAD_DIM), (1, 2))
    return pl.pallas_call(
        _swa_prompt_kernel,
        grid=(b, nb),
        in_specs=[
            pl.BlockSpec((None, blk, ATT_WIDTH), lambda i, n: (i, n, qb)),
            pl.BlockSpec((None, blk, KV_WIDTH), lambda i, n: (i, n, kb)),
            pl.BlockSpec((None, blk, KV_WIDTH), lambda i, n: (i, n, vb)),
            pl.BlockSpec((None, blk, KV_WIDTH), lambda i, n: (i, jnp.maximum(n - 1, 0), kb)),
            pl.BlockSpec((None, blk, KV_WIDTH), lambda i, n: (i, jnp.maximum(n - 1, 0), vb)),
            tab_cur, tab_cur, tab_cur, tab_prev, tab_prev, tab_prev,
            pl.BlockSpec((1, LANES), lambda i, n: (0, 0)),
            pl.BlockSpec((1, LANES), lambda i, n: (0, 0)),
            pl.BlockSpec(memory_space=pltpu.SMEM),
        ],
        out_specs=[
            pl.BlockSpec((None, blk, ATT_WIDTH), cur),
            pl.BlockSpec((None, blk, KV_WIDTH), cur),
        ],
        out_shape=[
            jax.ShapeDtypeStruct((b, t, ATT_WIDTH), F32),
            jax.ShapeDtypeStruct((b, t, KV_WIDTH), F32),
        ],
        compiler_params=_cp(("parallel", "arbitrary")),
        name="swa_prompt",
    )(pa, pa, pa, pa, pa, *tabs, *tabs, qw, kw, sinks)


def _qk_norm_rope_kernel(x_ref, w_ref, c_ref, sa_ref, sb_ref, o_ref):
    gmat = _head_indicator()
    tabs = (c_ref[...], sa_ref[...], sb_ref[...])
    for c in range(x_ref.shape[1] // LANES):
        sl = slice(c * LANES, (c + 1) * LANES)
        o_ref[:, sl] = _norm_rope_chunk(x_ref[:, sl], w_ref[:, sl], *tabs, gmat)


def qk_norm_rope(x, w_row, tabs):
    m, w = x.shape
    full = lambda *shape: pl.BlockSpec(shape, lambda: (0,) * len(shape))
    return pl.pallas_call(
        _qk_norm_rope_kernel,
        in_specs=[full(m, w), full(1, w), full(1, LANES), full(1, LANES), full(1, LANES)],
        out_specs=full(m, w),
        out_shape=jax.ShapeDtypeStruct((m, w), F32),
        name="qk_norm_rope",
    )(x, w_row, *tabs)


def _swa_decode_kernel(q_ref, kn_ref, vn_ref, ck_ref, cv_ref, sink_ref, o_ref, kw_ref, vw_ref):
    bb = q_ref.shape[0]
    nbuf = ck_ref.shape[1]
    row_kv = lax.broadcasted_iota(jnp.int32, (ATT_HEADS, KV_WIDTH), 0) // ATT_GROUP
    lane_kv = lax.broadcasted_iota(jnp.int32, (ATT_HEADS, KV_WIDTH), 1) // HEAD_DIM
    own = row_kv == lane_kv
    sink = sink_ref[...]
    for b in range(bb):
        q2 = q_ref[b]
        q_exp = jnp.where(own, jnp.concatenate([q2] * ATT_KV_HEADS, axis=1), 0.0)
        k_new, v_new = kn_ref[b], vn_ref[b]
        k_buf, v_buf = ck_ref[b], cv_ref[b]
        s_buf = _dot_nt(q_exp.astype(BF16), k_buf.astype(BF16)) * ATT_SCALE
        s_new = jnp.sum(q_exp * k_new, axis=-1, keepdims=True) * ATT_SCALE
        m = jnp.maximum(jnp.maximum(jnp.max(s_buf, axis=-1, keepdims=True), s_new), sink)
        e_buf = jnp.exp(s_buf - m)
        e_new = jnp.exp(s_new - m)
        inv = 1.0 / (jnp.sum(e_buf, axis=-1, keepdims=True) + e_new + jnp.exp(sink - m))
        o = _dot((e_buf * inv).astype(BF16), v_buf.astype(BF16)) + (e_new * inv) * v_new
        o = jnp.where(own, o, 0.0)
        o_ref[b] = (o[:, 0:HEAD_DIM] + o[:, HEAD_DIM:2 * HEAD_DIM]
                    + o[:, 2 * HEAD_DIM:3 * HEAD_DIM] + o[:, 3 * HEAD_DIM:4 * HEAD_DIM])
        kw_ref[b, 0:nbuf - 1, :] = k_buf[1:nbuf]
        kw_ref[b, nbuf - 1:nbuf, :] = k_new
        vw_ref[b, 0:nbuf - 1, :] = v_buf[1:nbuf]
        vw_ref[b, nbuf - 1:nbuf, :] = v_new


def swa_decode(q, k_new, v_new, cache_k, cache_v, sinks, *, bb=8):
    b, nbuf, _ = cache_k.shape
    blk3 = lambda s1, s2: pl.BlockSpec((bb, s1, s2), lambda i: (i, 0, 0))
    return pl.pallas_call(
        _swa_decode_kernel,
        grid=(b // bb,),
        in_specs=[
            blk3(ATT_HEADS, HEAD_DIM), blk3(1, KV_WIDTH), blk3(1, KV_WIDTH),
            blk3(nbuf, KV_WIDTH), blk3(nbuf, KV_WIDTH),
            pl.BlockSpec((ATT_HEADS, 1), lambda i: (0, 0)),
        ],
        out_specs=[blk3(ATT_HEADS, HEAD_DIM), blk3(nbuf, KV_WIDTH), blk3(nbuf, KV_WIDTH)],
        out_shape=[
            jax.ShapeDtypeStruct((b, ATT_HEADS, HEAD_DIM), F32),
            jax.ShapeDtypeStruct((b, nbuf, KV_WIDTH), F32),
            jax.ShapeDtypeStruct((b, nbuf, KV_WIDTH), F32),
        ],
        compiler_params=_cp(("parallel",)),
        name="swa_decode",
    )(q, k_new, v_new, cache_k, cache_v, sinks.reshape(ATT_HEADS, 1))


def _head_rms_kernel(x_ref, w_ref, o_ref):
    for h in range(x_ref.shape[1] // XATT_HEAD_DIM):
        sl = slice(h * XATT_HEAD_DIM, (h + 1) * XATT_HEAD_DIM)
        o_ref[:, sl] = _rms_rows(x_ref[:, sl], w_ref[...])


def head_rms(x, w):
    m, wd = x.shape
    return pl.pallas_call(
        _head_rms_kernel,
        in_specs=[pl.BlockSpec((m, wd), lambda: (0, 0)),
                  pl.BlockSpec((1, XATT_HEAD_DIM), lambda: (0, 0))],
        out_specs=pl.BlockSpec((m, wd), lambda: (0, 0)),
        out_shape=jax.ShapeDtypeStruct((m, wd), F32),
        name="head_rms",
    )(x, w.reshape(1, XATT_HEAD_DIM))


def _xattn_prompt_kernel(q_ref, k_ref, v_ref, w_ref, o_ref):
    scale = 1.0 / math.sqrt(XATT_HEAD_DIM)
    for h in range(XATT_HEADS):
        sl = slice(h * XATT_HEAD_DIM, (h + 1) * XATT_HEAD_DIM)
        qn = _rms_rows(q_ref[:, sl], w_ref[...]).astype(BF16)
        s = _dot_nt(qn, k_ref[:, sl].astype(BF16)) * scale
        e = jnp.exp(s - jnp.max(s, axis=-1, keepdims=True))
        p = e / jnp.sum(e, axis=-1, keepdims=True)
        o_ref[:, sl] = _dot(p.astype(BF16), v_ref[:, sl].astype(BF16))


def xattn_prompt(q, mem_k, mem_v, xq_norm_w, *, tq=512):
    b, t, w = q.shape
    n_mem = mem_k.shape[1]
    return pl.pallas_call(
        _xattn_prompt_kernel,
        grid=(b, t // tq),
        in_specs=[
            pl.BlockSpec((None, tq, w), lambda i, j: (i, j, 0)),
            pl.BlockSpec((None, n_mem, w), lambda i, j: (i, 0, 0)),
            pl.BlockSpec((None, n_mem, w), lambda i, j: (i, 0, 0)),
            pl.BlockSpec((1, XATT_HEAD_DIM), lambda i, j: (0, 0)),
        ],
        out_specs=pl.BlockSpec((None, tq, w), lambda i, j: (i, j, 0)),
        out_shape=jax.ShapeDtypeStruct((b, t, w), F32),
        compiler_params=_cp(("parallel", "arbitrary")),
        name="xattn_prompt",
    )(q, mem_k, mem_v, xq_norm_w.reshape(1, XATT_HEAD_DIM))


def _xattn_decode_kernel(q_ref, k_ref, v_ref, w_ref, o_ref):
    bb, rows, _ = q_ref.shape
    scale = 1.0 / math.sqrt(XATT_HEAD_DIM)
    row = lax.broadcasted_iota(jnp.int32, (rows, 1), 0)
    for b in range(bb):
        qn = _rms_rows(q_ref[b], w_ref[...]).astype(BF16)
        s = sum(jnp.where(row == h, _dot_nt(qn, k_ref[b, :, h, :].astype(BF16)), 0.0)
                for h in range(XATT_HEADS)) * scale
        e = jnp.exp(s - jnp.max(s, axis=-1, keepdims=True))
        p = (e / jnp.sum(e, axis=-1, keepdims=True)).astype(BF16)
        o_ref[b] = sum(jnp.where(row == h, _dot(p, v_ref[b, :, h, :].astype(BF16)), 0.0)
                       for h in range(XATT_HEADS))


def xattn_decode(q_pad, mem_k, mem_v, xq_norm_w, *, bb=8):
    b, rows, _ = q_pad.shape
    n_mem = mem_k.shape[2]
    kv = pl.BlockSpec((None, bb, n_mem, XATT_HEADS, XATT_HEAD_DIM), lambda i: (0, i, 0, 0, 0))
    return pl.pallas_call(
        _xattn_decode_kernel,
        grid=(b // bb,),
        in_specs=[pl.BlockSpec((bb, rows, XATT_HEAD_DIM), lambda i: (i, 0, 0)), kv, kv,
                  pl.BlockSpec((1, XATT_HEAD_DIM), lambda i: (0, 0))],
        out_specs=pl.BlockSpec((bb, rows, XATT_HEAD_DIM), lambda i: (i, 0, 0)),
        out_shape=jax.ShapeDtypeStruct((b, rows, XATT_HEAD_DIM), F32),
        compiler_params=_cp(("parallel",)),
        name="xattn_decode",
    )(q_pad, mem_k, mem_v, xq_norm_w.reshape(1, XATT_HEAD_DIM))


LORA_OFF = 3 * RWKV_WIDTH
GATE_OFF = LORA_OFF + DECAY_LORA + AAA_LORA
GATE_PAD = RWKV_PROJ_PAD - GATE_OFF


def _sigmoid(x):
    return 1.0 / (1.0 + jnp.exp(-x))


def _per_chunk(fn, *arrays):
    w = arrays[0].shape[1]
    outs = [fn(*(a[:, c * LANES:(c + 1) * LANES] for a in arrays)) for c in range(w // LANES)]
    return jnp.concatenate(outs, axis=1)


def _rwkv_prep_core(pr, prev, mu, w0, a0, kk_w, ka_w, rk_w, w_lora, w_gate):
    c = RWKV_WIDTH
    gmat = _head_indicator()
    xm = pr + (prev - pr) * mu
    r, k, v = xm[:, 0:c], xm[:, c:2 * c], xm[:, 2 * c:3 * c]
    lora = xm[:, LORA_OFF:LORA_OFF + LANES]
    lane = lax.broadcasted_iota(jnp.int32, lora.shape, 1)
    lora_in = jnp.where(lane < DECAY_LORA, jnp.tanh(lora), lora)
    wa = _dot(lora_in.astype(BF16), w_lora)
    z = -(w0 + wa[:, 0:c])
    softplus = jnp.maximum(z, 0.0) + jnp.log(1.0 + jnp.exp(-jnp.abs(z)))
    log_decay = -jnp.exp(-softplus - 0.5)
    a = _sigmoid(a0 + wa[:, c:2 * c])
    g = _dot(_sigmoid(xm[:, GATE_OFF:GATE_OFF + GATE_PAD]).astype(BF16), w_gate)
    kk = k * kk_w
    norm = jnp.sqrt(_per_chunk(lambda t: _group_sum(t * t, gmat), kk))
    kk = kk / jnp.maximum(norm, 1e-12)
    kp = k * (1.0 + (a - 1.0) * ka_w)
    bonus = _per_chunk(lambda t: _group_sum(t, gmat), r * kp * rk_w) * v
    return r, log_decay, kp, v, kk, kk * a, bonus, g


def _rwkv_prep_seq_kernel(pr_ref, prev0_ref, mu_ref, w0_ref, a0_ref, kkw_ref, kaw_ref, rkw_ref,
                          wl_ref, wg_ref, *refs):
    out_refs, last_ref = refs[:-1], refs[-1]

    @pl.when(pl.program_id(1) == 0)
    def _():
        last_ref[...] = prev0_ref[...]

    pr = pr_ref[...]
    rows = pr.shape[0]
    row = lax.broadcasted_iota(jnp.int32, (rows, 1), 0)
    prev = jnp.where(row == 0, last_ref[...], pltpu.roll(pr, 1, axis=0))
    last_ref[...] = pr[rows - 1:rows, :]
    outs = _rwkv_prep_core(pr, prev, mu_ref[...], w0_ref[...], a0_ref[...], kkw_ref[...],
                           kaw_ref[...], rkw_ref[...], wl_ref[...], wg_ref[...])
    for o_ref, o in zip(out_refs, outs):
        o_ref[...] = o


def _rwkv_prep_tok_kernel(pr_ref, prev_ref, mu_ref, w0_ref, a0_ref, kkw_ref, kaw_ref, rkw_ref,
                          wl_ref, wg_ref, *out_refs):
    outs = _rwkv_prep_core(pr_ref[...], prev_ref[...], mu_ref[...], w0_ref[...], a0_ref[...],
                           kkw_ref[...], kaw_ref[...], rkw_ref[...], wl_ref[...], wg_ref[...])
    for o_ref, o in zip(out_refs, outs):
        o_ref[...] = o


def _rwkv_param_specs(index_map):
    c = RWKV_WIDTH
    shapes = [(1, RWKV_PROJ_PAD)] + [(1, c)] * 5 + [(LANES, 2 * c), (GATE_PAD, c)]
    return [pl.BlockSpec(s, index_map) for s in shapes]


def rwkv_prep_seq(pr, prev0, params, *, tm=256):
    b, t, wd = pr.shape
    c = RWKV_WIDTH
    out = jax.ShapeDtypeStruct((b, t, c), F32)
    return pl.pallas_call(
        _rwkv_prep_seq_kernel,
        grid=(b, t // tm),
        in_specs=[pl.BlockSpec((None, tm, wd), lambda i, j: (i, j, 0)),
                  pl.BlockSpec((None, 1, wd), lambda i, j: (i, 0, 0))]
        + _rwkv_param_specs(lambda i, j: (0, 0)),
        out_specs=[pl.BlockSpec((None, tm, c), lambda i, j: (i, j, 0))] * 8,
        out_shape=[out] * 8,
        scratch_shapes=[pltpu.VMEM((1, wd), F32)],
        compiler_params=_cp(("parallel", "arbitrary")),
        name="rwkv_prep_seq",
    )(pr, prev0, *params)


def rwkv_prep_tok(pr, prev, params):
    m, wd = pr.shape
    c = RWKV_WIDTH
    out = jax.ShapeDtypeStruct((m, c), F32)
    return pl.pallas_call(
        _rwkv_prep_tok_kernel,
        grid=(1,),
        in_specs=[pl.BlockSpec((m, wd), lambda i: (0, 0))] * 2
        + _rwkv_param_specs(lambda i: (0, 0)),
        out_specs=[pl.BlockSpec((m, c), lambda i: (0, 0))] * 8,
        out_shape=[out] * 8,
        compiler_params=_cp(("arbitrary",)),
        name="rwkv_prep_tok",
    )(pr, prev, *params)


def _dot_tn(a, b):
    return lax.dot_general(a, b, (((0,), (0,)), ((), ())), preferred_element_type=F32)


def _rwkv_scan_kernel(r_ref, ld_ref, kp_ref, v_ref, kk_ref, b_ref, y_ref, s_out_ref, s_ref):
    @pl.when(pl.program_id(1) == 0)
    def _():
        s_ref[...] = jnp.zeros_like(s_ref)

    n = CHUNK
    ti = lax.broadcasted_iota(jnp.int32, (n, n), 0)
    si = lax.broadcasted_iota(jnp.int32, (n, n), 1)
    strict, incl = si < ti, si <= ti
    tri = jnp.where(incl, 1.0, 0.0).astype(BF16)

    ld = ld_ref[...]
    l1, l2, l3 = _split3(ld)
    lc = _dot(tri, l1) + _dot(tri, l2) + _dot(tri, l3)
    lc_end = lc[n - 1:n, :]
    e_neg = jnp.exp(-lc)
    kk, b, kp = kk_ref[...], b_ref[...], kp_ref[...]
    a_t = (-kk * jnp.exp(lc - ld)).astype(BF16)
    b_t = (b * e_neg).astype(BF16)
    k_t = (kp * e_neg).astype(BF16)
    r_t = (r_ref[...] * jnp.exp(lc)).astype(BF16)
    to_end = jnp.exp(lc_end - lc)
    b_e = (b * to_end).astype(BF16)
    k_e = (kp * to_end).astype(BF16)
    v_b = v_ref[...].astype(BF16)
    g_end = jnp.exp(lc_end)

    heads = range(RWKV_HEADS)
    sl = [slice(h * n, (h + 1) * n) for h in heads]
    gm = [_dot_nt(jnp.concatenate([a_t[:, sl[h]], r_t[:, sl[h]]], axis=0),
                  jnp.concatenate([b_t[:, sl[h]], k_t[:, sl[h]]], axis=0)) for h in heads]
    n_ab = [jnp.where(strict, gm[h][0:n, 0:n], 0.0).astype(BF16) for h in heads]
    l_ak = [jnp.where(strict, gm[h][0:n, n:2 * n], 0.0).astype(BF16) for h in heads]
    p_rb = [jnp.where(incl, gm[h][n:2 * n, 0:n], 0.0).astype(BF16) for h in heads]
    p_rk = [jnp.where(incl, gm[h][n:2 * n, n:2 * n], 0.0).astype(BF16) for h in heads]
    s0 = [s_ref[h] for h in heads]
    s0_b = [s0[h].astype(BF16) for h in heads]
    u = [_dot_nt(a_t[:, sl[h]], s0_b[h]) + _dot(l_ak[h], v_b[:, sl[h]]) for h in heads]
    pw = n_ab
    for step in range(6):
        u = [u[h] + _dot(pw[h], u[h].astype(BF16)) for h in heads]
        if step < 5:
            pw = [_dot(pw[h], pw[h]).astype(BF16) for h in heads]
    u_b = [u[h].astype(BF16) for h in heads]
    y = [_dot_nt(r_t[:, sl[h]], s0_b[h]) + _dot(p_rb[h], u_b[h]) + _dot(p_rk[h], v_b[:, sl[h]])
         for h in heads]
    s_new = [s0[h] * g_end[:, sl[h]] + _dot_tn(
        jnp.concatenate([u_b[h], v_b[:, sl[h]]], axis=0),
        jnp.concatenate([b_e[:, sl[h]], k_e[:, sl[h]]], axis=0)) for h in heads]
    for h in heads:
        y_ref[:, sl[h]] = y[h]
        s_ref[h] = s_new[h]
        s_out_ref[h] = s_new[h]


def rwkv_scan(r, ld, kp, v, kk, b):
    bsz, t, c = r.shape
    blk = pl.BlockSpec((None, CHUNK, c), lambda i, j: (i, j, 0))
    st = pl.BlockSpec((None, RWKV_HEADS, RWKV_HEAD_DIM, RWKV_HEAD_DIM), lambda i, j: (i, 0, 0, 0))
    return pl.pallas_call(
        _rwkv_scan_kernel,
        grid=(bsz, t // CHUNK),
        in_specs=[blk] * 6,
        out_specs=[blk, st],
        out_shape=[jax.ShapeDtypeStruct((bsz, t, c), F32),
                   jax.ShapeDtypeStruct((bsz, RWKV_HEADS, RWKV_HEAD_DIM, RWKV_HEAD_DIM), F32)],
        scratch_shapes=[pltpu.VMEM((RWKV_HEADS, RWKV_HEAD_DIM, RWKV_HEAD_DIM), F32)],
        compiler_params=_cp(("parallel", "arbitrary")),
        name="rwkv_scan",
    )(r, ld, kp, v, kk, b)


STEP_UNROLL = 8


def _rwkv_step_kernel(r_ref, ld_ref, kp_ref, v_ref, kk_ref, b_ref, s_ref, y_ref, s_out_ref):
    n = RWKV_HEAD_DIM
    eye = lax.broadcasted_iota(jnp.int32, (n, n), 0) == lax.broadcasted_iota(jnp.int32, (n, n), 1)

    def body(i, carry):
        gs = [i * STEP_UNROLL + j for j in range(STEP_UNROLL)]
        s = [s_ref[g] for g in gs]
        sa = [jnp.sum(s[j] * (-kk_ref[g]), axis=1, keepdims=True) for j, g in enumerate(gs)]
        v_col = [jnp.sum(jnp.where(eye, v_ref[g], 0.0), axis=1, keepdims=True) for g in gs]
        s_new = [s[j] * jnp.exp(ld_ref[g]) + sa[j] * b_ref[g] + v_col[j] * kp_ref[g]
                 for j, g in enumerate(gs)]
        y_col = [jnp.sum(s_new[j] * r_ref[g], axis=1, keepdims=True) for j, g in enumerate(gs)]
        for j, g in enumerate(gs):
            y_ref[g] = jnp.sum(jnp.where(eye, y_col[j], 0.0), axis=0, keepdims=True)
            s_out_ref[g] = s_new[j]
        return carry

    lax.fori_loop(0, s_ref.shape[0] // STEP_UNROLL, body, 0)


def rwkv_step(r, ld, kp, v, kk, b, state, *, gb=64):
    g = state.shape[0]
    n = RWKV_HEAD_DIM
    row = pl.BlockSpec((gb, 1, n), lambda i: (i, 0, 0))
    st = pl.BlockSpec((gb, n, n), lambda i: (i, 0, 0))
    return pl.pallas_call(
        _rwkv_step_kernel,
        grid=(g // gb,),
        in_specs=[row] * 6 + [st],
        out_specs=[row, st],
        out_shape=[jax.ShapeDtypeStruct((g, 1, n), F32), jax.ShapeDtypeStruct((g, n, n), F32)],
        compiler_params=_cp(("parallel",)),
        name="rwkv_step",
    )(r, ld, kp, v, kk, b, state)


def _rwkv_post_kernel(y_ref, bonus_ref, g_ref, lnw_ref, lnb_ref, o_ref):
    gmat = _head_indicator()
    inv = 1.0 / RWKV_HEAD_DIM
    for c in range(y_ref.shape[1] // LANES):
        sl = slice(c * LANES, (c + 1) * LANES)
        y = y_ref[:, sl]
        d = y - _group_sum(y, gmat) * inv
        var = _group_sum(d * d, gmat) * inv
        yn = d * lax.rsqrt(var + GN_EPS) * lnw_ref[:, sl] + lnb_ref[:, sl]
        o_ref[:, sl] = (yn + bonus_ref[:, sl]) * g_ref[:, sl]


def rwkv_post(y, bonus, g, ln_w, ln_b, *, tm):
    m, c = y.shape
    blk = pl.BlockSpec((tm, c), lambda i: (i, 0))
    vec = pl.BlockSpec((1, c), lambda i: (0, 0))
    return pl.pallas_call(
        _rwkv_post_kernel,
        grid=(m // tm,),
        in_specs=[blk, blk, blk, vec, vec],
        out_specs=blk,
        out_shape=jax.ShapeDtypeStruct((m, c), F32),
        compiler_params=_cp(("parallel",)),
        name="rwkv_post",
    )(y, bonus, g, ln_w.reshape(1, c), ln_b.reshape(1, c))


ROUTER_LANES = LANES


def _router_kernel(ha_ref, hb_ref, lnw_ref, whi_ref, wlo_ref, bias_ref, u_ref, idx_ref, gate_ref,
                   *, steps_a):
    use_a = pl.program_id(0) < steps_a
    h = jnp.where(use_a, ha_ref[...], hb_ref[...])
    u = _rms_rows(h, lnw_ref[...])
    for j in range(u.shape[1] // LANES):
        u_ref[:, j, :] = u[:, j * LANES:(j + 1) * LANES]
    u_hi, u_lo = _split2(u)
    w_hi = whi_ref[...]
    logits = _dot(u_hi, w_hi) + _dot(u_lo, w_hi) + _dot(u_hi, wlo_ref[...]) + bias_ref[...]
    lane = lax.broadcasted_iota(jnp.int32, logits.shape, 1)
    neg = -jnp.inf

    def first_max(x):
        m = jnp.max(x, axis=1, keepdims=True)
        return m, jnp.min(jnp.where(x == m, lane, ROUTER_LANES), axis=1, keepdims=True)

    gl = jnp.where(lane < N_EXPERT_GROUPS, logits, neg)
    g_max, g_idx = first_max(gl)
    g_gate = 1.0 / jnp.sum(jnp.exp(gl - g_max), axis=1, keepdims=True)
    lo = N_EXPERT_GROUPS + g_idx * EXPERTS_PER_GROUP
    el = jnp.where((lane >= lo) & (lane < lo + EXPERTS_PER_GROUP), logits, neg)
    v1, i1 = first_max(el)
    v2, i2 = first_max(jnp.where(lane == i1, neg, el))
    e2 = jnp.exp(v2 - v1)
    w1 = g_gate / (1.0 + e2)
    w2 = g_gate * e2 / (1.0 + e2)
    idx_ref[...] = jnp.where(lane == 0, i1 - N_EXPERT_GROUPS,
                             jnp.where(lane == 1, i2 - N_EXPERT_GROUPS, 0))
    gate_ref[...] = jnp.where(lane == 0, w1, jnp.where(lane == 1, w2, 0.0))


def moe_router(h_a, h_b, ln_w, w_hi, w_lo, bias, *, tm):
    (ma, d), mb = h_a.shape, h_b.shape[0]
    assert ma % tm == 0 and mb % tm == 0
    steps_a, steps_b = ma // tm, mb // tm
    m = ma + mb
    const = lambda r, w: pl.BlockSpec((r, w), lambda i: (0, 0))
    row = lambda w: pl.BlockSpec((tm, w), lambda i: (i, 0))
    return pl.pallas_call(
        functools.partial(_router_kernel, steps_a=steps_a),
        grid=(steps_a + steps_b,),
        in_specs=[pl.BlockSpec((tm, d), lambda i: (jnp.minimum(i, steps_a - 1), 0)),
                  pl.BlockSpec((tm, d), lambda i: (jnp.maximum(i - steps_a, 0), 0)),
                  const(1, d), const(d, ROUTER_LANES), const(d, ROUTER_LANES),
                  const(1, ROUTER_LANES)],
        out_specs=[pl.BlockSpec((tm, d // LANES, LANES), lambda i: (i, 0, 0)),
                   row(ROUTER_LANES), row(ROUTER_LANES)],
        out_shape=[jax.ShapeDtypeStruct((m, d // LANES, LANES), F32),
                   jax.ShapeDtypeStruct((m, ROUTER_LANES), jnp.int32),
                   jax.ShapeDtypeStruct((m, ROUTER_LANES), F32)],
        compiler_params=_cp(("arbitrary",)),
        name="moe_router",
    )(h_a, h_b, ln_w.reshape(1, d), w_hi, w_lo, bias)


def _moe_expert_kernel(bexp_ref, nused_ref, cnt_ref, tok_ref, tokn_ref, dst_ref, roww_ref, u_hbm,
                       wg_ref, wu_ref, wd_ref, y_hbm, xbuf, ybuf, wg_b, wu_b, wd_b, sem_in,
                       sem_out):
    i = pl.program_id(0)
    n_used = nused_ref[0]
    slot = lax.rem(i, 2)

    def gather_rows(idx_ref, dst_slot):
        for r in range(MOE_BLOCK):
            pltpu.make_async_copy(u_hbm.at[idx_ref[0, 0, r]],
                                  xbuf.at[dst_slot, r], sem_in.at[dst_slot]).start()

    def row_out(r, s):
        return pltpu.make_async_copy(ybuf.at[s, r], y_hbm.at[dst_ref[0, 0, r]], sem_out.at[s])

    def wait_rows(n_rows, make_copy):
        def body(r, carry):
            make_copy(r).wait()
            return carry
        lax.fori_loop(0, n_rows, body, 0)

    def wait_out(n_rows, s):
        wait_rows(n_rows, lambda r: pltpu.make_async_copy(
            ybuf.at[s, 0], y_hbm.at[0], sem_out.at[s]))

    @pl.when(i == 0)
    def _():
        gather_rows(tok_ref, 0)

    @pl.when(i + 1 < n_used)
    def _():
        gather_rows(tokn_ref, 1 - slot)

    @pl.when(i < n_used)
    def _():
        @pl.when((i == 0) | (bexp_ref[i] != bexp_ref[jnp.maximum(i - 1, 0)]))
        def _():
            wg_b[...] = wg_ref[...].astype(BF16)
            wu_b[...] = wu_ref[...].astype(BF16)
            wd_b[...] = wd_ref[...].astype(BF16)

        wait_rows(MOE_BLOCK, lambda r: pltpu.make_async_copy(
            u_hbm.at[0], xbuf.at[slot, 0], sem_in.at[slot]))
        n_tiles = xbuf.shape[2]
        x = jnp.concatenate([xbuf[slot, :, j, :] for j in range(n_tiles)],
                            axis=1).astype(BF16)
        hg = _dot(x, wg_b[...])
        hu = _dot(x, wu_b[...])
        act = (hg * _sigmoid(hg) * hu).astype(BF16)
        y = _dot(act, wd_b[...]) * roww_ref[...]

        @pl.when(i >= 2)
        def _():
            wait_out(cnt_ref[jnp.maximum(i - 2, 0)], slot)

        for j in range(n_tiles):
            ybuf[slot, :, j, :] = y[:, j * LANES:(j + 1) * LANES]
        cnt = cnt_ref[i]
        base = 0
        for run in (128, 64, 32, 16, 8, 4, 2, 1):
            if run > MOE_BLOCK:
                continue
            take = (cnt & run) != 0 if run < MOE_BLOCK else cnt >= MOE_BLOCK
            start = base

            @pl.when(take)
            def _(start=start, run=run):
                for r in range(run):
                    row_out(start + r, slot).start()

            base = base + jnp.where(take, run, 0)

        @pl.when(i == n_used - 1)
        def _():
            wait_out(cnt, slot)

            @pl.when(i >= 1)
            def _():
                wait_out(cnt_ref[jnp.maximum(i - 1, 0)], 1 - slot)


def moe_experts(u_all, row_tok, row_dst, row_w, block_exp, n_used, row_cnt, w_gate, w_up,
                w_down, n_assign):
    n_tiles = u_all.shape[1]
    d = n_tiles * LANES
    n_blocks = row_tok.shape[0]
    ff = w_gate.shape[2]
    smem_blk = lambda off: pl.BlockSpec(
        (1, 1, MOE_BLOCK), lambda i, be, nu, rc: (jnp.minimum(i + off, n_blocks - 1), 0, 0),
        memory_space=pltpu.SMEM)
    grid_spec = pltpu.PrefetchScalarGridSpec(
        num_scalar_prefetch=3,
        grid=(n_blocks,),
        in_specs=[
            smem_blk(0), smem_blk(1), smem_blk(0),
            pl.BlockSpec((MOE_BLOCK, 1), lambda i, be, nu, rc: (i, 0)),
            pl.BlockSpec(memory_space=pl.ANY),
            pl.BlockSpec((None, d, ff), lambda i, be, nu, rc: (be[i], 0, 0)),
            pl.BlockSpec((None, d, ff), lambda i, be, nu, rc: (be[i], 0, 0)),
            pl.BlockSpec((None, ff, d), lambda i, be, nu, rc: (be[i], 0, 0)),
        ],
        out_specs=pl.BlockSpec(memory_space=pl.ANY),
        scratch_shapes=[
            pltpu.VMEM((2, MOE_BLOCK, n_tiles, LANES), F32),
            pltpu.VMEM((2, MOE_BLOCK, n_tiles, LANES), F32),
            pltpu.VMEM((d, ff), BF16), pltpu.VMEM((d, ff), BF16), pltpu.VMEM((ff, d), BF16),
            pltpu.SemaphoreType.DMA((2,)), pltpu.SemaphoreType.DMA((2,)),
        ],
    )
    return pl.pallas_call(
        _moe_expert_kernel,
        grid_spec=grid_spec,
        out_shape=jax.ShapeDtypeStruct((n_assign, n_tiles, LANES), F32),
        compiler_params=_cp(("arbitrary",)),
        name="moe_experts",
    )(block_exp, n_used, row_cnt, row_tok, row_tok, row_dst, row_w, u_all, w_gate, w_up, w_down)


def _moe_combine_kernel(h_ref, y0_ref, y1_ref, o_ref):
    for j in range(y0_ref.shape[1]):
        sl = slice(j * LANES, (j + 1) * LANES)
        o_ref[:, sl] = h_ref[:, sl] + (y0_ref[:, j, :] + y1_ref[:, j, :])


def moe_combine(h, y_slots, row_off, slot_stride, *, tm):
    m, d = h.shape
    n_tiles = y_slots.shape[1]
    assert row_off % tm == 0 and slot_stride % tm == 0
    off0, off1 = row_off // tm, (row_off + slot_stride) // tm
    return pl.pallas_call(
        _moe_combine_kernel,
        grid=(m // tm,),
        in_specs=[pl.BlockSpec((tm, d), lambda i: (i, 0)),
                  pl.BlockSpec((tm, n_tiles, LANES), lambda i: (i + off0, 0, 0)),
                  pl.BlockSpec((tm, n_tiles, LANES), lambda i: (i + off1, 0, 0))],
        out_specs=pl.BlockSpec((tm, d), lambda i: (i, 0)),
        out_shape=jax.ShapeDtypeStruct((m, d), F32),
        compiler_params=_cp(("parallel",)),
        name="moe_combine",
    )(h, y_slots, y_slots)


def moe_dispatch(e_idx, gates, slot_stride):
    m = e_idx.shape[0]
    a = m * TOP_K
    e_flat = e_idx.reshape(a)
    order = jnp.argsort(e_flat, stable=True).astype(jnp.int32)
    counts = jnp.sum(e_flat[:, None] == jnp.arange(N_EXPERTS, dtype=jnp.int32)[None, :],
                     axis=0, dtype=jnp.int32)
    pad_counts = (counts + MOE_BLOCK - 1) // MOE_BLOCK * MOE_BLOCK
    starts = jnp.cumsum(counts) - counts
    pad_ends = jnp.cumsum(pad_counts)
    pad_starts = pad_ends - pad_counts
    n_blocks = a // MOE_BLOCK + N_EXPERTS
    p = n_blocks * MOE_BLOCK
    n_used = (pad_ends[-1] // MOE_BLOCK).astype(jnp.int32)
    blk = jnp.arange(n_blocks, dtype=jnp.int32)
    blk_start = jnp.minimum(blk, n_used - 1) * MOE_BLOCK
    block_exp = jnp.minimum(jnp.searchsorted(pad_ends, blk_start, side='right'),
                            N_EXPERTS - 1).astype(jnp.int32)
    in_exp = blk * MOE_BLOCK - pad_starts[block_exp]
    row_cnt = jnp.where(blk < n_used,
                        jnp.clip(counts[block_exp] - in_exp, 0, MOE_BLOCK), 0).astype(jnp.int32)
    lane = jnp.arange(MOE_BLOCK, dtype=jnp.int32)[None, :]
    valid = lane < row_cnt[:, None]
    src = jnp.clip((starts[block_exp] + in_exp)[:, None] + lane, 0, a - 1)
    assign = order[src]
    row_tok = jnp.where(valid, assign // TOP_K, 0)
    row_dst = jnp.where(valid, (assign % TOP_K) * slot_stride + assign // TOP_K, 0)
    row_w = jnp.where(valid, gates.reshape(a)[assign], 0.0)
    return (row_tok.reshape(n_blocks, 1, MOE_BLOCK), row_dst.reshape(n_blocks, 1, MOE_BLOCK),
            row_w.reshape(p, 1), block_exp, n_used.reshape(1), row_cnt)


def rwkv_params(rw_mu, rw_w0, rw_w2, rw_a0, rw_a2, rw_g2, rw_k_k, rw_k_a, rw_r_k):
    c = RWKV_WIDTH
    mu = jnp.pad(rw_mu, (0, RWKV_PROJ_PAD - RWKV_PROJ)).reshape(1, RWKV_PROJ_PAD)
    w_lora = jnp.zeros((LANES, 2 * c), F32)
    w_lora = w_lora.at[0:DECAY_LORA, 0:c].set(rw_w2).at[DECAY_LORA:LANES, c:2 * c].set(rw_a2)
    w_gate = jnp.pad(rw_g2, ((0, GATE_PAD - GATE_LORA), (0, 0)))
    vec = lambda x: x.reshape(1, c)
    return (mu, vec(rw_w0), vec(rw_a0), vec(rw_k_k), vec(rw_k_a), vec(rw_r_k),
            w_lora.astype(BF16), w_gate.astype(BF16))


def _token_tiles(m):
    return (1024, 512) if m % 1024 == 0 else (m, m)


def _dense_front(x2d, wts, tm):
    pa = norm_matmul(x2d, wts['ln1_w'], wts['w_att'], tm=tm, tn=512)
    pr = norm_matmul(x2d, wts['ln1_w'], wts['w_rw'], tm=tm, tn=RWKV_PROJ_PAD // 3)
    return pa, pr


def _dense_back(x2d, att2d, rw2d, wts, xattn_fn, tm):
    h1 = matmul_residual([att2d, rw2d], [wts['w_out_a'], wts['w_out_r']], x2d, tm=tm, tn=512)
    qx = norm_matmul(h1, wts['ln2_w'], wts['xq_w'], tm=tm, tn=XATT_WIDTH)
    ox = xattn_fn(qx)
    return matmul_residual([ox], [wts['xo_w']], h1, tm=tm, tn=512)


def kernel(x_prompt, x_sample, cache_win_k, cache_win_v, state_wkv, state_shift, cache_mem_k, cache_mem_v, mem_prompt, ln1_w, w_in, q_norm_w, k_norm_w, attn_sinks, rw_mu, rw_w0, rw_w2, rw_a0, rw_a2, rw_g2, rw_k_k, rw_k_a, rw_r_k, rw_ln_w, rw_ln_b, w_out, ln2_w, mem_norm_w, xq_w, xkv_w, xq_norm_w, xk_norm_w, xo_w, ln3_w, router_group_w, router_group_b, router_expert_w, router_expert_b, exp_w_gate, exp_w_up, exp_w_down):
    assert w_in.shape[0] == 1, "single-layer stack"
    bp, seq, d = x_prompt.shape
    bs = x_sample.shape[0]
    mp = bp * seq
    c = RWKV_WIDTH

    router_w = jnp.concatenate(
        [router_group_w[0], router_expert_w[0],
         jnp.zeros((d, ROUTER_LANES - N_EXPERT_GROUPS - N_EXPERTS), F32)], axis=1)
    router_hi = router_w.astype(BF16)
    wts = {
        'ln1_w': ln1_w[0], 'ln2_w': ln2_w[0], 'ln3_w': ln3_w[0],
        'w_att': w_in[0][:, :ATT_PROJ].astype(BF16),
        'w_rw': jnp.pad(w_in[0][:, ATT_PROJ:],
                        ((0, 0), (0, RWKV_PROJ_PAD - RWKV_PROJ))).astype(BF16),
        'w_out_a': w_out[0][:ATT_WIDTH].astype(BF16),
        'w_out_r': w_out[0][ATT_WIDTH:].astype(BF16),
        'xq_w': xq_w[0].astype(BF16), 'xo_w': xo_w[0].astype(BF16),
        'router_hi': router_hi,
        'router_lo': (router_w - router_hi.astype(F32)).astype(BF16),
        'router_b': jnp.pad(jnp.concatenate([router_group_b[0], router_expert_b[0]]),
                            (0, ROUTER_LANES - N_EXPERT_GROUPS - N_EXPERTS)).reshape(1, -1),
    }
    rw_par = rwkv_params(rw_mu[0], rw_w0[0], rw_w2[0], rw_a0[0], rw_a2[0], rw_g2[0],
                         rw_k_k[0], rw_k_a[0], rw_r_k[0])

    tm_p, te_p = _token_tiles(mp)
    xp = x_prompt.reshape(mp, d)
    pa, pr = _dense_front(xp, wts, tm_p)
    pa3 = pa.reshape(bp, seq, ATT_PROJ)
    pr3 = pr.reshape(bp, seq, RWKV_PROJ_PAD)
    tabs_p = rope_tables(jnp.arange(seq, dtype=jnp.int32))
    att_p, kn_p = swa_prompt(pa3, tabs_p, q_norm_w[0], k_norm_w[0], attn_sinks[0])
    prep = rwkv_prep_seq(pr3, jnp.zeros((bp, 1, RWKV_PROJ_PAD), F32), rw_par)
    r, ld, kp, v, kk, b, bonus, g = prep
    y_p, wkv_p = rwkv_scan(r, ld, kp, v, kk, b)
    rw_p = rwkv_post(y_p.reshape(mp, c), bonus.reshape(mp, c), g.reshape(mp, c),
                     rw_ln_w[0], rw_ln_b[0], tm=te_p)

    n_mem = mem_prompt.shape[1]
    kv_mem = norm_matmul(mem_prompt.reshape(bp * n_mem, d), mem_norm_w[0],
                         xkv_w[0].astype(BF16), tm=bp * n_mem, tn=512)
    mem_k = head_rms(kv_mem[:, :XATT_WIDTH], xk_norm_w[0])
    mem_v = kv_mem[:, XATT_WIDTH:]
    mem_k3 = mem_k.reshape(bp, n_mem, XATT_WIDTH)
    mem_v3 = mem_v.reshape(bp, n_mem, XATT_WIDTH)

    def xattn_p(qx):
        return xattn_prompt(qx.reshape(bp, seq, XATT_WIDTH), mem_k3, mem_v3,
                            xq_norm_w[0]).reshape(mp, XATT_WIDTH)

    h2_p = _dense_back(xp, att_p.reshape(mp, ATT_WIDTH), rw_p, wts, xattn_p, tm_p)

    tm_s, te_s = _token_tiles(bs)
    xs = x_sample.reshape(bs, d)
    sa, sr = _dense_front(xs, wts, tm_s)
    tabs_s = rope_tables(PAST_LEN + jnp.arange(1, dtype=jnp.int32))
    qk_w = jnp.concatenate([jnp.tile(q_norm_w[0], ATT_HEADS),
                            jnp.tile(k_norm_w[0], ATT_KV_HEADS)]).reshape(1, -1)
    qk = qk_norm_rope(sa[:, :ATT_WIDTH + KV_WIDTH], qk_w, tabs_s)
    nbuf = cache_win_k.shape[2]
    att_s, win_k, win_v = swa_decode(
        qk[:, :ATT_WIDTH].reshape(bs, ATT_HEADS, HEAD_DIM),
        qk[:, ATT_WIDTH:].reshape(bs, 1, KV_WIDTH),
        sa[:, ATT_WIDTH + KV_WIDTH:].reshape(bs, 1, KV_WIDTH),
        cache_win_k[0].reshape(bs, nbuf, KV_WIDTH), cache_win_v[0].reshape(bs, nbuf, KV_WIDTH),
        attn_sinks[0])
    shift_prev = jnp.pad(state_shift[0], ((0, 0), (0, RWKV_PROJ_PAD - RWKV_PROJ)))
    r, ld, kp, v, kk, b, bonus, g = rwkv_prep_tok(sr, shift_prev, rw_par)
    gh = bs * RWKV_HEADS
    rows = [t.reshape(gh, 1, RWKV_HEAD_DIM) for t in (r, ld, kp, v, kk, b)]
    y_s, wkv_s = rwkv_step(*rows, state_wkv[0].reshape(gh, RWKV_HEAD_DIM, RWKV_HEAD_DIM))
    rw_s = rwkv_post(y_s.reshape(bs, c), bonus, g, rw_ln_w[0], rw_ln_b[0], tm=te_s)
    def xattn_s(qx):
        q_pad = jnp.pad(qx.reshape(bs, XATT_HEADS, XATT_HEAD_DIM), ((0, 0), (0, 4), (0, 0)))
        o = xattn_decode(q_pad, cache_mem_k, cache_mem_v, xq_norm_w[0])
        return o[:, :XATT_HEADS].reshape(bs, XATT_WIDTH)

    h2_s = _dense_back(xs, att_s.reshape(bs, ATT_WIDTH), rw_s, wts, xattn_s, tm_s)

    m_all = mp + bs
    slot_stride = m_all
    tc = math.gcd(mp, bs, 512)
    u_all, idx_all, gate_all = moe_router(h2_p, h2_s, wts['ln3_w'], wts['router_hi'],
                                          wts['router_lo'], wts['router_b'], tm=tc)
    row_tok, row_dst, row_w, block_exp, n_used, row_cnt = moe_dispatch(
        idx_all[:, :TOP_K], gate_all[:, :TOP_K], slot_stride)
    y_slots = moe_experts(u_all, row_tok, row_dst, row_w, block_exp, n_used, row_cnt,
                          exp_w_gate[0], exp_w_up[0], exp_w_down[0], TOP_K * slot_stride)
    out_p = moe_combine(h2_p, y_slots, 0, slot_stride, tm=tc)
    out_s = moe_combine(h2_s, y_slots, mp, slot_stride, tm=tc)

    win = min(WINDOW, seq)
    kv_shape = (1, bp, win, ATT_KV_HEADS, HEAD_DIM)
    return (
        out_p.reshape(bp, seq, d),
        out_s.reshape(bs, 1, d),
        kn_p[:, seq - win:].reshape(kv_shape),
        pa3[:, seq - win:, ATT_WIDTH + KV_WIDTH:].reshape(kv_shape),
        wkv_p[None],
        pr3[:, seq - 1, :RWKV_PROJ][None],
        mem_k3.reshape(1, bp, n_mem, XATT_HEADS, XATT_HEAD_DIM),
        mem_v3.reshape(1, bp, n_mem, XATT_HEADS, XATT_HEAD_DIM),
        win_k.reshape(1, bs, nbuf, ATT_KV_HEADS, HEAD_DIM),
        win_v.reshape(1, bs, nbuf, ATT_KV_HEADS, HEAD_DIM),
        wkv_s.reshape(1, bs, RWKV_HEADS, RWKV_HEAD_DIM, RWKV_HEAD_DIM),
        sr[:, :RWKV_PROJ].reshape(1, bs, RWKV_PROJ),
    )
```

```python
import functools
import math

import jax
import jax.numpy as jnp
from jax import lax
from jax.experimental import pallas as pl
from jax.experimental.pallas import tpu as pltpu

F32 = jnp.float32
BF16 = jnp.bfloat16

D_MODEL = 2048
HEAD_DIM = 64
ATT_HEADS = 16
ATT_KV_HEADS = 4
ATT_GROUP = ATT_HEADS // ATT_KV_HEADS
ATT_WIDTH = ATT_HEADS * HEAD_DIM
KV_WIDTH = ATT_KV_HEADS * HEAD_DIM
ATT_PROJ = ATT_WIDTH + 2 * KV_WIDTH
WINDOW = 128
ATT_SCALE = HEAD_DIM ** -0.5
ROPE_THETA = 500000.0
ROT_DIM = HEAD_DIM // 4
PAST_LEN = 16384

RWKV_WIDTH = 1024
RWKV_HEAD_DIM = 64
RWKV_HEADS = 16
DECAY_LORA = 64
AAA_LORA = 64
GATE_LORA = 160
RWKV_PROJ = 3 * RWKV_WIDTH + DECAY_LORA + AAA_LORA + GATE_LORA
RWKV_PROJ_PAD = 3456

N_MEM = 256
XATT_HEADS = 4
XATT_HEAD_DIM = 128
XATT_WIDTH = XATT_HEADS * XATT_HEAD_DIM

N_EXPERT_GROUPS = 8
EXPERTS_PER_GROUP = 8
N_EXPERTS = 64
TOP_K = 2
EXPERT_FF = D_MODEL // 4
MOE_BLOCK = 128

RMS_EPS = 1e-6
GN_EPS = 64e-5

LANES = 128
CHUNK = 64
VMEM_LIMIT = 56 * 1024 * 1024


def _cp(sem, vmem=VMEM_LIMIT):
    return pltpu.CompilerParams(dimension_semantics=sem, vmem_limit_bytes=vmem)


def _rms_rows(x, w):
    ms = jnp.mean(x * x, axis=-1, keepdims=True)
    return x * lax.rsqrt(ms + RMS_EPS) * w


def _split2(x):
    hi = x.astype(BF16)
    lo = (x - hi.astype(F32)).astype(BF16)
    return hi, lo


def _split3(x):
    h1 = x.astype(BF16)
    r1 = x - h1.astype(F32)
    h2 = r1.astype(BF16)
    h3 = (r1 - h2.astype(F32)).astype(BF16)
    return h1, h2, h3


def _dot(a, b):
    return jnp.dot(a, b, preferred_element_type=F32)


def _dot_nt(a, b):
    return lax.dot_general(a, b, (((1,), (1,)), ((), ())), preferred_element_type=F32)


def _group_sum(x, gmat):
    hi, lo = _split2(x)
    return _dot(hi, gmat) + _dot(lo, gmat)


def _head_indicator():
    r = lax.broadcasted_iota(jnp.int32, (LANES, LANES), 0) // HEAD_DIM
    c = lax.broadcasted_iota(jnp.int32, (LANES, LANES), 1) // HEAD_DIM
    return jnp.where(r == c, 1.0, 0.0).astype(BF16)


def _norm_mm_kernel(x_ref, lnw_ref, w_ref, o_ref, xn_ref):
    @pl.when(pl.program_id(1) == 0)
    def _():
        xn_ref[...] = _rms_rows(x_ref[...], lnw_ref[...]).astype(BF16)

    o_ref[...] = _dot(xn_ref[...], w_ref[...])


def norm_matmul(x, ln_w, w_bf16, *, tm, tn):
    m, k = x.shape
    n = w_bf16.shape[1]
    assert m % tm == 0 and n % tn == 0
    return pl.pallas_call(
        _norm_mm_kernel,
        grid=(m // tm, n // tn),
        in_specs=[
            pl.BlockSpec((tm, k), lambda i, j: (i, 0)),
            pl.BlockSpec((1, k), lambda i, j: (0, 0)),
            pl.BlockSpec((k, tn), lambda i, j: (0, j)),
        ],
        out_specs=pl.BlockSpec((tm, tn), lambda i, j: (i, j)),
        out_shape=jax.ShapeDtypeStruct((m, n), F32),
        scratch_shapes=[pltpu.VMEM((tm, k), BF16)],
        compiler_params=_cp(("parallel", "arbitrary")),
        name="norm_matmul",
    )(x, ln_w.reshape(1, k), w_bf16)


def _mm_res_kernel(*refs, n_lhs):
    a_refs = refs[:n_lhs]
    w_refs = refs[n_lhs:2 * n_lhs]
    res_ref = refs[2 * n_lhs]
    o_ref = refs[2 * n_lhs + 1]
    acc = res_ref[...]
    for a_ref, w_ref in zip(a_refs, w_refs):
        acc = acc + _dot(a_ref[...].astype(BF16), w_ref[...])
    o_ref[...] = acc


def matmul_residual(lhs_list, w_list, res, *, tm, tn):
    m, n = res.shape
    n_lhs = len(lhs_list)
    assert m % tm == 0 and n % tn == 0
    in_specs = [pl.BlockSpec((tm, a.shape[1]), lambda i, j: (i, 0)) for a in lhs_list]
    in_specs += [pl.BlockSpec((w.shape[0], tn), lambda i, j: (0, j)) for w in w_list]
    in_specs += [pl.BlockSpec((tm, tn), lambda i, j: (i, j))]
    return pl.pallas_call(
        functools.partial(_mm_res_kernel, n_lhs=n_lhs),
        grid=(m // tm, n // tn),
        in_specs=in_specs,
        out_specs=pl.BlockSpec((tm, tn), lambda i, j: (i, j)),
        out_shape=jax.ShapeDtypeStruct((m, n), F32),
        compiler_params=_cp(("parallel", "arbitrary")),
        name="matmul_residual",
    )(*lhs_list, *w_list, res)


def rope_tables(pos):
    half = ROT_DIM // 2
    inv = ROPE_THETA ** (-jnp.arange(half, dtype=F32) * 2.0 / ROT_DIM)
    ang = pos.astype(F32)[:, None] * inv[None, :]
    cos, sin = jnp.cos(ang), jnp.sin(ang)
    t = pos.shape[0]
    ones = jnp.ones((t, HEAD_DIM - ROT_DIM), F32)
    zeros = jnp.zeros((t, HEAD_DIM - ROT_DIM), F32)
    z8 = jnp.zeros((t, half), F32)
    cos_t = jnp.concatenate([cos, cos, ones], axis=1)
    sin_a = jnp.concatenate([z8, sin, zeros], axis=1)
    sin_b = jnp.concatenate([-sin, z8, zeros], axis=1)
    return tuple(jnp.concatenate([a, a], axis=1) for a in (cos_t, sin_a, sin_b))


def _norm_rope_chunk(x, w, cos_t, sin_a, sin_b, gmat):
    ms = _group_sum(x * x, gmat) * (1.0 / HEAD_DIM)
    xn = x * lax.rsqrt(ms + RMS_EPS) * w
    half = ROT_DIM // 2
    return (xn * cos_t + pltpu.roll(xn, half, axis=1) * sin_a
            + pltpu.roll(xn, LANES - half, axis=1) * sin_b)


def _norm_rope(x, w, tabs, gmat):
    chunks = [
        _norm_rope_chunk(x[:, c * LANES:(c + 1) * LANES], w, *tabs, gmat)
        for c in range(x.shape[1] // LANES)
    ]
    return chunks[0] if len(chunks) == 1 else jnp.concatenate(chunks, axis=1)


def _sink_softmax(s, sink):
    m = jnp.maximum(jnp.max(s, axis=-1, keepdims=True), sink)
    e = jnp.exp(s - m)
    return e / (jnp.sum(e, axis=-1, keepdims=True) + jnp.exp(sink - m))


def _swa_prompt_kernel(q_ref, kc_ref, vc_ref, kp_ref, vp_ref, cc_ref, sac_ref, sbc_ref,
                       cp_ref, sap_ref, sbp_ref, qw_ref, kw_ref, sink_ref, o_ref, kn_ref):
    n = pl.program_id(1)
    blk = q_ref.shape[0]
    gmat = _head_indicator()
    tabs_c = (cc_ref[...], sac_ref[...], sbc_ref[...])
    tabs_p = (cp_ref[...], sap_ref[...], sbp_ref[...])
    q = _norm_rope(q_ref[...], qw_ref[...], tabs_c, gmat)
    k_cur = _norm_rope(kc_ref[...], kw_ref[...], tabs_c, gmat)
    k_prev = _norm_rope(kp_ref[...], kw_ref[...], tabs_p, gmat)
    kn_ref[...] = k_cur
    k_all = jnp.concatenate([k_prev, k_cur], axis=0).astype(BF16)
    v_all = jnp.concatenate([vp_ref[...], vc_ref[...]], axis=0).astype(BF16)

    qi = lax.broadcasted_iota(jnp.int32, (blk, 2 * blk), 0) + blk
    si = lax.broadcasted_iota(jnp.int32, (blk, 2 * blk), 1)
    rel = qi - si
    valid = (rel >= 0) & (rel <= WINDOW) & ((n > 0) | (si >= blk))

    for kv in range(ATT_KV_HEADS):
        k_h = k_all[:, kv * HEAD_DIM:(kv + 1) * HEAD_DIM]
        v_h = v_all[:, kv * HEAD_DIM:(kv + 1) * HEAD_DIM]
        heads = [kv * ATT_GROUP + g for g in range(ATT_GROUP)]
        q_g = jnp.concatenate(
            [q[:, h * HEAD_DIM:(h + 1) * HEAD_DIM] for h in heads], axis=0).astype(BF16)
        s = _dot_nt(q_g, k_h) * ATT_SCALE
        probs = []
        for g, h in enumerate(heads):
            s_h = jnp.where(valid, s[g * blk:(g + 1) * blk], -jnp.inf)
            probs.append(_sink_softmax(s_h, sink_ref[h]))
        p = jnp.concatenate(probs, axis=0).astype(BF16)
        o = _dot(p, v_h)
        for g, h in enumerate(heads):
            o_ref[:, h * HEAD_DIM:(h + 1) * HEAD_DIM] = o[g * blk:(g + 1) * blk]


def swa_prompt(pa, tabs, q_norm_w, k_norm_w, sinks):
    b, t, _ = pa.shape
    blk = WINDOW
    nb = t // blk
    qb, kb, vb = 0, ATT_WIDTH // KV_WIDTH, ATT_WIDTH // KV_WIDTH + 1
    cur = lambda i, n, *_: (i, n, 0)
    tab_cur = pl.BlockSpec((blk, LANES), lambda i, n: (n, 0))
    tab_prev = pl.BlockSpec((blk, LANES), lambda i, n: (jnp.maximum(n - 1, 0), 0))
    qw = jnp.tile(q_norm_w.reshape(1, HEAD_DIM), (1, 2))
    kw = jnp.tile(k_norm_w.reshape(1, HEAD_DIM), (1, 2))
    return pl.pallas_call(
        _swa_prompt_kernel,
        grid=(b, nb),
        in_specs=[
            pl.BlockSpec((None, blk, ATT_WIDTH), lambda i, n: (i, n, qb)),
            pl.BlockSpec((None, blk, KV_WIDTH), lambda i, n: (i, n, kb)),
            pl.BlockSpec((None, blk, KV_WIDTH), lambda i, n: (i, n, vb)),
            pl.BlockSpec((None, blk, KV_WIDTH), lambda i, n: (i, jnp.maximum(n - 1, 0), kb)),
            pl.BlockSpec((None, blk, KV_WIDTH), lambda i, n: (i, jnp.maximum(n - 1, 0), vb)),
            tab_cur, tab_cur, tab_cur, tab_prev, tab_prev, tab_prev,
            pl.BlockSpec((1, LANES), lambda i, n: (0, 0)),
            pl.BlockSpec((1, LANES), lambda i, n: (0, 0)),
            pl.BlockSpec(memory_space=pltpu.SMEM),
        ],
        out_specs=[
            pl.BlockSpec((None, blk, ATT_WIDTH), cur),
            pl.BlockSpec((None, blk, KV_WIDTH), cur),
        ],
        out_shape=[
            jax.ShapeDtypeStruct((b, t, ATT_WIDTH), F32),
            jax.ShapeDtypeStruct((b, t, KV_WIDTH), F32),
        ],
        compiler_params=_cp(("parallel", "arbitrary")),
        name="swa_prompt",
    )(pa, pa, pa, pa, pa, *tabs, *tabs, qw, kw, sinks)


def _qk_norm_rope_kernel(x_ref, w_ref, c_ref, sa_ref, sb_ref, o_ref):
    gmat = _head_indicator()
    tabs = (c_ref[...], sa_ref[...], sb_ref[...])
    for c in range(x_ref.shape[1] // LANES):
        sl = slice(c * LANES, (c + 1) * LANES)
        o_ref[:, sl] = _norm_rope_chunk(x_ref[:, sl], w_ref[:, sl], *tabs, gmat)


def qk_norm_rope(x, w_row, tabs):
    m, w = x.shape
    full = lambda *shape: pl.BlockSpec(shape, lambda: (0,) * len(shape))
    return pl.pallas_call(
        _qk_norm_rope_kernel,
        in_specs=[full(m, w), full(1, w), full(1, LANES), full(1, LANES), full(1, LANES)],
        out_specs=full(m, w),
        out_shape=jax.ShapeDtypeStruct((m, w), F32),
        name="qk_norm_rope",
    )(x, w_row, *tabs)


def _swa_decode_kernel(q_ref, kn_ref, vn_ref, ck_ref, cv_ref, sink_ref, o_ref, kw_ref, vw_ref):
    bb = q_ref.shape[0]
    nbuf = ck_ref.shape[1]
    row_kv = lax.broadcasted_iota(jnp.int32, (ATT_HEADS, KV_WIDTH), 0) // ATT_GROUP
    lane_kv = lax.broadcasted_iota(jnp.int32, (ATT_HEADS, KV_WIDTH), 1) // HEAD_DIM
    own = row_kv == lane_kv
    sink = sink_ref[...]
    for b in range(bb):
        q2 = q_ref[b]
        q_exp = jnp.where(own, jnp.concatenate([q2] * ATT_KV_HEADS, axis=1), 0.0)
        k_new, v_new = kn_ref[b], vn_ref[b]
        k_buf, v_buf = ck_ref[b], cv_ref[b]
        s_buf = _dot_nt(q_exp.astype(BF16), k_buf.astype(BF16)) * ATT_SCALE
        s_new = jnp.sum(q_exp * k_new, axis=-1, keepdims=True) * ATT_SCALE
        m = jnp.maximum(jnp.maximum(jnp.max(s_buf, axis=-1, keepdims=True), s_new), sink)
        e_buf = jnp.exp(s_buf - m)
        e_new = jnp.exp(s_new - m)
        inv = 1.0 / (jnp.sum(e_buf, axis=-1, keepdims=True) + e_new + jnp.exp(sink - m))
        o = _dot((e_buf * inv).astype(BF16), v_buf.astype(BF16)) + (e_new * inv) * v_new
        o = jnp.where(own, o, 0.0)
        o_ref[b] = (o[:, 0:HEAD_DIM] + o[:, HEAD_DIM:2 * HEAD_DIM]
                    + o[:, 2 * HEAD_DIM:3 * HEAD_DIM] + o[:, 3 * HEAD_DIM:4 * HEAD_DIM])
        kw_ref[b, 0:nbuf - 1, :] = k_buf[1:nbuf]
        kw_ref[b, nbuf - 1:nbuf, :] = k_new
        vw_ref[b, 0:nbuf - 1, :] = v_buf[1:nbuf]
        vw_ref[b, nbuf - 1:nbuf, :] = v_new


def swa_decode(q, k_new, v_new, cache_k, cache_v, sinks, *, bb=8):
    b, nbuf, _ = cache_k.shape
    blk3 = lambda s1, s2: pl.BlockSpec((bb, s1, s2), lambda i: (i, 0, 0))
    return pl.pallas_call(
        _swa_decode_kernel,
        grid=(b // bb,),
        in_specs=[
            blk3(ATT_HEADS, HEAD_DIM), blk3(1, KV_WIDTH), blk3(1, KV_WIDTH),
            blk3(nbuf, KV_WIDTH), blk3(nbuf, KV_WIDTH),
            pl.BlockSpec((ATT_HEADS, 1), lambda i: (0, 0)),
        ],
        out_specs=[blk3(ATT_HEADS, HEAD_DIM), blk3(nbuf, KV_WIDTH), blk3(nbuf, KV_WIDTH)],
        out_shape=[
            jax.ShapeDtypeStruct((b, ATT_HEADS, HEAD_DIM), F32),
            jax.ShapeDtypeStruct((b, nbuf, KV_WIDTH), F32),
            jax.ShapeDtypeStruct((b, nbuf, KV_WIDTH), F32),
        ],
        compiler_params=_cp(("parallel",)),
        name="swa_decode",
    )(q, k_new, v_new, cache_k, cache_v, sinks.reshape(ATT_HEADS, 1))


def _head_rms_kernel(x_ref, w_ref, o_ref):
    for h in range(x_ref.shape[1] // XATT_HEAD_DIM):
        sl = slice(h * XATT_HEAD_DIM, (h + 1) * XATT_HEAD_DIM)
        o_ref[:, sl] = _rms_rows(x_ref[:, sl], w_ref[...])


def head_rms(x, w):
    m, wd = x.shape
    return pl.pallas_call(
        _head_rms_kernel,
        in_specs=[pl.BlockSpec((m, wd), lambda: (0, 0)),
                  pl.BlockSpec((1, XATT_HEAD_DIM), lambda: (0, 0))],
        out_specs=pl.BlockSpec((m, wd), lambda: (0, 0)),
        out_shape=jax.ShapeDtypeStruct((m, wd), F32),
        name="head_rms",
    )(x, w.reshape(1, XATT_HEAD_DIM))


def _xattn_prompt_kernel(q_ref, k_ref, v_ref, w_ref, o_ref):
    scale = 1.0 / math.sqrt(XATT_HEAD_DIM)
    for h in range(XATT_HEADS):
        sl = slice(h * XATT_HEAD_DIM, (h + 1) * XATT_HEAD_DIM)
        qn = _rms_rows(q_ref[:, sl], w_ref[...]).astype(BF16)
        s = _dot_nt(qn, k_ref[:, sl].astype(BF16)) * scale
        e = jnp.exp(s - jnp.max(s, axis=-1, keepdims=True))
        p = e / jnp.sum(e, axis=-1, keepdims=True)
        o_ref[:, sl] = _dot(p.astype(BF16), v_ref[:, sl].astype(BF16))


def xattn_prompt(q, mem_k, mem_v, xq_norm_w, *, tq=512):
    b, t, w = q.shape
    n_mem = mem_k.shape[1]
    return pl.pallas_call(
        _xattn_prompt_kernel,
        grid=(b, t // tq),
        in_specs=[
            pl.BlockSpec((None, tq, w), lambda i, j: (i, j, 0)),
            pl.BlockSpec((None, n_mem, w), lambda i, j: (i, 0, 0)),
            pl.BlockSpec((None, n_mem, w), lambda i, j: (i, 0, 0)),
            pl.BlockSpec((1, XATT_HEAD_DIM), lambda i, j: (0, 0)),
        ],
        out_specs=pl.BlockSpec((None, tq, w), lambda i, j: (i, j, 0)),
        out_shape=jax.ShapeDtypeStruct((b, t, w), F32),
        compiler_params=_cp(("parallel", "arbitrary")),
        name="xattn_prompt",
    )(q, mem_k, mem_v, xq_norm_w.reshape(1, XATT_HEAD_DIM))


def _xattn_decode_kernel(q_ref, k_ref, v_ref, w_ref, o_ref):
    bb, rows, _ = q_ref.shape
    scale = 1.0 / math.sqrt(XATT_HEAD_DIM)
    row = lax.broadcasted_iota(jnp.int32, (rows, 1), 0)
    for b in range(bb):
        qn = _rms_rows(q_ref[b], w_ref[...]).astype(BF16)
        s = sum(jnp.where(row == h, _dot_nt(qn, k_ref[b, :, h, :].astype(BF16)), 0.0)
                for h in range(XATT_HEADS)) * scale
        e = jnp.exp(s - jnp.max(s, axis=-1, keepdims=True))
        p = (e / jnp.sum(e, axis=-1, keepdims=True)).astype(BF16)
        o_ref[b] = sum(jnp.where(row == h, _dot(p, v_ref[b, :, h, :].astype(BF16)), 0.0)
                       for h in range(XATT_HEADS))


def xattn_decode(q_pad, mem_k, mem_v, xq_norm_w, *, bb=8):
    b, rows, _ = q_pad.shape
    n_mem = mem_k.shape[2]
    kv = pl.BlockSpec((None, bb, n_mem, XATT_HEADS, XATT_HEAD_DIM), lambda i: (0, i, 0, 0, 0))
    return pl.pallas_call(
        _xattn_decode_kernel,
        grid=(b // bb,),
        in_specs=[pl.BlockSpec((bb, rows, XATT_HEAD_DIM), lambda i: (i, 0, 0)), kv, kv,
                  pl.BlockSpec((1, XATT_HEAD_DIM), lambda i: (0, 0))],
        out_specs=pl.BlockSpec((bb, rows, XATT_HEAD_DIM), lambda i: (i, 0, 0)),
        out_shape=jax.ShapeDtypeStruct((b, rows, XATT_HEAD_DIM), F32),
        compiler_params=_cp(("parallel",)),
        name="xattn_decode",
    )(q_pad, mem_k, mem_v, xq_norm_w.reshape(1, XATT_HEAD_DIM))


LORA_OFF = 3 * RWKV_WIDTH
GATE_OFF = LORA_OFF + DECAY_LORA + AAA_LORA
GATE_PAD = RWKV_PROJ_PAD - GATE_OFF


def _sigmoid(x):
    return 1.0 / (1.0 + jnp.exp(-x))


def _per_chunk(fn, *arrays):
    w = arrays[0].shape[1]
    outs = [fn(*(a[:, c * LANES:(c + 1) * LANES] for a in arrays)) for c in range(w // LANES)]
    return jnp.concatenate(outs, axis=1)


def _rwkv_prep_core(pr, prev, mu, w0, a0, kk_w, ka_w, rk_w, w_lora, w_gate):
    c = RWKV_WIDTH
    gmat = _head_indicator()
    xm = pr + (prev - pr) * mu
    r, k, v = xm[:, 0:c], xm[:, c:2 * c], xm[:, 2 * c:3 * c]
    lora = xm[:, LORA_OFF:LORA_OFF + LANES]
    lane = lax.broadcasted_iota(jnp.int32, lora.shape, 1)
    lora_in = jnp.where(lane < DECAY_LORA, jnp.tanh(lora), lora)
    wa = _dot(lora_in.astype(BF16), w_lora)
    z = -(w0 + wa[:, 0:c])
    softplus = jnp.maximum(z, 0.0) + jnp.log(1.0 + jnp.exp(-jnp.abs(z)))
    log_decay = -jnp.exp(-softplus - 0.5)
    a = _sigmoid(a0 + wa[:, c:2 * c])
    g = _dot(_sigmoid(xm[:, GATE_OFF:GATE_OFF + GATE_PAD]).astype(BF16), w_gate)
    kk = k * kk_w
    norm = jnp.sqrt(_per_chunk(lambda t: _group_sum(t * t, gmat), kk))
    kk = kk / jnp.maximum(norm, 1e-12)
    kp = k * (1.0 + (a - 1.0) * ka_w)
    bonus = _per_chunk(lambda t: _group_sum(t, gmat), r * kp * rk_w) * v
    return r, log_decay, kp, v, kk, kk * a, bonus, g


def _rwkv_prep_seq_kernel(pr_ref, prev0_ref, mu_ref, w0_ref, a0_ref, kkw_ref, kaw_ref, rkw_ref,
                          wl_ref, wg_ref, *refs):
    out_refs, last_ref = refs[:-1], refs[-1]

    @pl.when(pl.program_id(1) == 0)
    def _():
        last_ref[...] = prev0_ref[...]

    pr = pr_ref[...]
    rows = pr.shape[0]
    row = lax.broadcasted_iota(jnp.int32, (rows, 1), 0)
    prev = jnp.where(row == 0, last_ref[...], pltpu.roll(pr, 1, axis=0))
    last_ref[...] = pr[rows - 1:rows, :]
    outs = _rwkv_prep_core(pr, prev, mu_ref[...], w0_ref[...], a0_ref[...], kkw_ref[...],
                           kaw_ref[...], rkw_ref[...], wl_ref[...], wg_ref[...])
    for o_ref, o in zip(out_refs, outs):
        o_ref[...] = o


def _rwkv_prep_tok_kernel(pr_ref, prev_ref, mu_ref, w0_ref, a0_ref, kkw_ref, kaw_ref, rkw_ref,
                          wl_ref, wg_ref, *out_refs):
    outs = _rwkv_prep_core(pr_ref[...], prev_ref[...], mu_ref[...], w0_ref[...], a0_ref[...],
                           kkw_ref[...], kaw_ref[...], rkw_ref[...], wl_ref[...], wg_ref[...])
    for o_ref, o in zip(out_refs, outs):
        o_ref[...] = o


def _rwkv_param_specs(index_map):
    c = RWKV_WIDTH
    shapes = [(1, RWKV_PROJ_PAD)] + [(1, c)] * 5 + [(LANES, 2 * c), (GATE_PAD, c)]
    return [pl.BlockSpec(s, index_map) for s in shapes]


def rwkv_prep_seq(pr, prev0, params, *, tm=256):
    b, t, wd = pr.shape
    c = RWKV_WIDTH
    out = jax.ShapeDtypeStruct((b, t, c), F32)
    return pl.pallas_call(
        _rwkv_prep_seq_kernel,
        grid=(b, t // tm),
        in_specs=[pl.BlockSpec((None, tm, wd), lambda i, j: (i, j, 0)),
                  pl.BlockSpec((None, 1, wd), lambda i, j: (i, 0, 0))]
        + _rwkv_param_specs(lambda i, j: (0, 0)),
        out_specs=[pl.BlockSpec((None, tm, c), lambda i, j: (i, j, 0))] * 8,
        out_shape=[out] * 8,
        scratch_shapes=[pltpu.VMEM((1, wd), F32)],
        compiler_params=_cp(("parallel", "arbitrary")),
        name="rwkv_prep_seq",
    )(pr, prev0, *params)


def rwkv_prep_tok(pr, prev, params):
    m, wd = pr.shape
    c = RWKV_WIDTH
    out = jax.ShapeDtypeStruct((m, c), F32)
    return pl.pallas_call(
        _rwkv_prep_tok_kernel,
        grid=(1,),
        in_specs=[pl.BlockSpec((m, wd), lambda i: (0, 0))] * 2
        + _rwkv_param_specs(lambda i: (0, 0)),
        out_specs=[pl.BlockSpec((m, c), lambda i: (0, 0))] * 8,
        out_shape=[out] * 8,
        compiler_params=_cp(("arbitrary",)),
        name="rwkv_prep_tok",
    )(pr, prev, *params)


def _dot_tn(a, b):
    return lax.dot_general(a, b, (((0,), (0,)), ((), ())), preferred_element_type=F32)


def _rwkv_scan_kernel(r_ref, ld_ref, kp_ref, v_ref, kk_ref, b_ref, y_ref, s_out_ref, s_ref):
    @pl.when(pl.program_id(1) == 0)
    def _():
        s_ref[...] = jnp.zeros_like(s_ref)

    n = CHUNK
    ti = lax.broadcasted_iota(jnp.int32, (n, n), 0)
    si = lax.broadcasted_iota(jnp.int32, (n, n), 1)
    strict, incl = si < ti, si <= ti
    tri = jnp.where(incl, 1.0, 0.0).astype(BF16)

    ld = ld_ref[...]
    l1, l2, l3 = _split3(ld)
    lc = _dot(tri, l1) + _dot(tri, l2) + _dot(tri, l3)
    lc_end = lc[n - 1:n, :]
    e_neg = jnp.exp(-lc)
    kk, b, kp = kk_ref[...], b_ref[...], kp_ref[...]
    a_t = (-kk * jnp.exp(lc - ld)).astype(BF16)
    b_t = (b * e_neg).astype(BF16)
    k_t = (kp * e_neg).astype(BF16)
    r_t = (r_ref[...] * jnp.exp(lc)).astype(BF16)
    to_end = jnp.exp(lc_end - lc)
    b_e = (b * to_end).astype(BF16)
    k_e = (kp * to_end).astype(BF16)
    v_b = v_ref[...].astype(BF16)
    g_end = jnp.exp(lc_end)

    heads = range(RWKV_HEADS)
    sl = [slice(h * n, (h + 1) * n) for h in heads]
    gm = [_dot_nt(jnp.concatenate([a_t[:, sl[h]], r_t[:, sl[h]]], axis=0),
                  jnp.concatenate([b_t[:, sl[h]], k_t[:, sl[h]]], axis=0)) for h in heads]
    n_ab = [jnp.where(strict, gm[h][0:n, 0:n], 0.0).astype(BF16) for h in heads]
    l_ak = [jnp.where(strict, gm[h][0:n, n:2 * n], 0.0).astype(BF16) for h in heads]
    p_rb = [jnp.where(incl, gm[h][n:2 * n, 0:n], 0.0).astype(BF16) for h in heads]
    p_rk = [jnp.where(incl, gm[h][n:2 * n, n:2 * n], 0.0).astype(BF16) for h in heads]
    s0 = [s_ref[h] for h in heads]
    s0_b = [s0[h].astype(BF16) for h in heads]
    u = [_dot_nt(a_t[:, sl[h]], s0_b[h]) + _dot(l_ak[h], v_b[:, sl[h]]) for h in heads]
    pw = n_ab
    for step in range(6):
        u = [u[h] + _dot(pw[h], u[h].astype(BF16)) for h in heads]
        if step < 5:
            pw = [_dot(pw[h], pw[h]).astype(BF16) for h in heads]
    u_b = [u[h].astype(BF16) for h in heads]
    y = [_dot_nt(r_t[:, sl[h]], s0_b[h]) + _dot(p_rb[h], u_b[h]) + _dot(p_rk[h], v_b[:, sl[h]])
         for h in heads]
    s_new = [s0[h] * g_end[:, sl[h]] + _dot_tn(
        jnp.concatenate([u_b[h], v_b[:, sl[h]]], axis=0),
        jnp.concatenate([b_e[:, sl[h]], k_e[:, sl[h]]], axis=0)) for h in heads]
    for h in heads:
        y_ref[:, sl[h]] = y[h]
        s_ref[h] = s_new[h]
        s_out_ref[h] = s_new[h]


def rwkv_scan(r, ld, kp, v, kk, b):
    bsz, t, c = r.shape
    blk = pl.BlockSpec((None, CHUNK, c), lambda i, j: (i, j, 0))
    st = pl.BlockSpec((None, RWKV_HEADS, RWKV_HEAD_DIM, RWKV_HEAD_DIM), lambda i, j: (i, 0, 0, 0))
    return pl.pallas_call(
        _rwkv_scan_kernel,
        grid=(bsz, t // CHUNK),
        in_specs=[blk] * 6,
        out_specs=[blk, st],
        out_shape=[jax.ShapeDtypeStruct((bsz, t, c), F32),
                   jax.ShapeDtypeStruct((bsz, RWKV_HEADS, RWKV_HEAD_DIM, RWKV_HEAD_DIM), F32)],
        scratch_shapes=[pltpu.VMEM((RWKV_HEADS, RWKV_HEAD_DIM, RWKV_HEAD_DIM), F32)],
        compiler_params=_cp(("parallel", "arbitrary")),
        name="rwkv_scan",
    )(r, ld, kp, v, kk, b)


STEP_UNROLL = 8


def _rwkv_step_kernel(r_ref, ld_ref, kp_ref, v_ref, kk_ref, b_ref, s_ref, y_ref, s_out_ref):
    n = RWKV_HEAD_DIM
    eye = lax.broadcasted_iota(jnp.int32, (n, n), 0) == lax.broadcasted_iota(jnp.int32, (n, n), 1)

    def body(i, carry):
        gs = [i * STEP_UNROLL + j for j in range(STEP_UNROLL)]
        s = [s_ref[g] for g in gs]
        sa = [jnp.sum(s[j] * (-kk_ref[g]), axis=1, keepdims=True) for j, g in enumerate(gs)]
        v_col = [jnp.sum(jnp.where(eye, v_ref[g], 0.0), axis=1, keepdims=True) for g in gs]
        s_new = [s[j] * jnp.exp(ld_ref[g]) + sa[j] * b_ref[g] + v_col[j] * kp_ref[g]
                 for j, g in enumerate(gs)]
        y_col = [jnp.sum(s_new[j] * r_ref[g], axis=1, keepdims=True) for j, g in enumerate(gs)]
        for j, g in enumerate(gs):
            y_ref[g] = jnp.sum(jnp.where(eye, y_col[j], 0.0), axis=0, keepdims=True)
            s_out_ref[g] = s_new[j]
        return carry

    lax.fori_loop(0, s_ref.shape[0] // STEP_UNROLL, body, 0)


def rwkv_step(r, ld, kp, v, kk, b, state, *, gb=64):
    g = state.shape[0]
    n = RWKV_HEAD_DIM
    row = pl.BlockSpec((gb, 1, n), lambda i: (i, 0, 0))
    st = pl.BlockSpec((gb, n, n), lambda i: (i, 0, 0))
    return pl.pallas_call(
        _rwkv_step_kernel,
        grid=(g // gb,),
        in_specs=[row] * 6 + [st],
        out_specs=[row, st],
        out_shape=[jax.ShapeDtypeStruct((g, 1, n), F32), jax.ShapeDtypeStruct((g, n, n), F32)],
        compiler_params=_cp(("parallel",)),
        name="rwkv_step",
    )(r, ld, kp, v, kk, b, state)


def _rwkv_post_kernel(y_ref, bonus_ref, g_ref, lnw_ref, lnb_ref, o_ref):
    gmat = _head_indicator()
    inv = 1.0 / RWKV_HEAD_DIM
    for c in range(y_ref.shape[1] // LANES):
        sl = slice(c * LANES, (c + 1) * LANES)
        y = y_ref[:, sl]
        d = y - _group_sum(y, gmat) * inv
        var = _group_sum(d * d, gmat) * inv
        yn = d * lax.rsqrt(var + GN_EPS) * lnw_ref[:, sl] + lnb_ref[:, sl]
        o_ref[:, sl] = (yn + bonus_ref[:, sl]) * g_ref[:, sl]


def rwkv_post(y, bonus, g, ln_w, ln_b, *, tm):
    m, c = y.shape
    blk = pl.BlockSpec((tm, c), lambda i: (i, 0))
    vec = pl.BlockSpec((1, c), lambda i: (0, 0))
    return pl.pallas_call(
        _rwkv_post_kernel,
        grid=(m // tm,),
        in_specs=[blk, blk, blk, vec, vec],
        out_specs=blk,
        out_shape=jax.ShapeDtypeStruct((m, c), F32),
        compiler_params=_cp(("parallel",)),
        name="rwkv_post",
    )(y, bonus, g, ln_w.reshape(1, c), ln_b.reshape(1, c))


ROUTER_LANES = LANES
ROW_TILES = D_MODEL // LANES


def _rows_to_tiles(ref, x):
    rows = x.shape[0]
    for j in range(ROW_TILES):
        ref[pl.ds(j, rows, stride=ROW_TILES), :] = x[:, j * LANES:(j + 1) * LANES]


def _tiles_to_rows(ref, rows):
    return jnp.concatenate(
        [ref[pl.ds(j, rows, stride=ROW_TILES), :] for j in range(ROW_TILES)], axis=1)


def _router_kernel(ha_ref, hb_ref, lnw_ref, whi_ref, wlo_ref, bias_ref, u_ref, idx_ref, gate_ref,
                   *, steps_a):
    use_a = pl.program_id(0) < steps_a
    h = jnp.where(use_a, ha_ref[...], hb_ref[...])
    u = _rms_rows(h, lnw_ref[...])
    _rows_to_tiles(u_ref, u)
    u_hi, u_lo = _split2(u)
    w_hi = whi_ref[...]
    logits = _dot(u_hi, w_hi) + _dot(u_lo, w_hi) + _dot(u_hi, wlo_ref[...]) + bias_ref[...]
    lane = lax.broadcasted_iota(jnp.int32, logits.shape, 1)
    neg = -jnp.inf

    def first_max(x):
        m = jnp.max(x, axis=1, keepdims=True)
        return m, jnp.min(jnp.where(x == m, lane, ROUTER_LANES), axis=1, keepdims=True)

    gl = jnp.where(lane < N_EXPERT_GROUPS, logits, neg)
    g_max, g_idx = first_max(gl)
    g_gate = 1.0 / jnp.sum(jnp.exp(gl - g_max), axis=1, keepdims=True)
    lo = N_EXPERT_GROUPS + g_idx * EXPERTS_PER_GROUP
    el = jnp.where((lane >= lo) & (lane < lo + EXPERTS_PER_GROUP), logits, neg)
    v1, i1 = first_max(el)
    v2, i2 = first_max(jnp.where(lane == i1, neg, el))
    e2 = jnp.exp(v2 - v1)
    w1 = g_gate / (1.0 + e2)
    w2 = g_gate * e2 / (1.0 + e2)
    idx_ref[...] = jnp.where(lane == 0, i1 - N_EXPERT_GROUPS,
                             jnp.where(lane == 1, i2 - N_EXPERT_GROUPS, 0))
    gate_ref[...] = jnp.where(lane == 0, w1, jnp.where(lane == 1, w2, 0.0))


def moe_router(h_a, h_b, ln_w, w_hi, w_lo, bias, *, tm):
    (ma, d), mb = h_a.shape, h_b.shape[0]
    assert ma % tm == 0 and mb % tm == 0
    steps_a, steps_b = ma // tm, mb // tm
    m = ma + mb
    const = lambda r, w: pl.BlockSpec((r, w), lambda i: (0, 0))
    row = lambda w: pl.BlockSpec((tm, w), lambda i: (i, 0))
    return pl.pallas_call(
        functools.partial(_router_kernel, steps_a=steps_a),
        grid=(steps_a + steps_b,),
        in_specs=[pl.BlockSpec((tm, d), lambda i: (jnp.minimum(i, steps_a - 1), 0)),
                  pl.BlockSpec((tm, d), lambda i: (jnp.maximum(i - steps_a, 0), 0)),
                  const(1, d), const(d, ROUTER_LANES), const(d, ROUTER_LANES),
                  const(1, ROUTER_LANES)],
        out_specs=[pl.BlockSpec((tm * (d // LANES), LANES), lambda i: (i, 0)),
                   row(ROUTER_LANES), row(ROUTER_LANES)],
        out_shape=[jax.ShapeDtypeStruct((m * (d // LANES), LANES), F32),
                   jax.ShapeDtypeStruct((m, ROUTER_LANES), jnp.int32),
                   jax.ShapeDtypeStruct((m, ROUTER_LANES), F32)],
        compiler_params=_cp(("arbitrary",)),
        name="moe_router",
    )(h_a, h_b, ln_w.reshape(1, d), w_hi, w_lo, bias)


X_SLOTS = 3
Y_SLOTS = 2


def _moe_expert_kernel(bexp_ref, nused_ref, tok0_ref, tok1_ref, tok2_ref, dst_ref, roww_ref,
                       u_hbm, wg_ref, wu_ref, wd_ref, y_hbm, xbuf, ybuf, wg_b, wu_b, wd_b,
                       sem_in, sem_out):
    i = pl.program_id(0)
    n_used = nused_ref[0]
    tile_rows = MOE_BLOCK * ROW_TILES
    pad_base = y_hbm.shape[0] - Y_SLOTS * tile_rows

    def gather_block(idx_ref, x_slot):
        for r in range(MOE_BLOCK):
            pltpu.make_async_copy(u_hbm.at[pl.ds(idx_ref[0, 0, r], ROW_TILES)],
                                  xbuf.at[x_slot, pl.ds(r * ROW_TILES, ROW_TILES)],
                                  sem_in.at[x_slot]).start()

    def gather_wait(x_slot):
        pltpu.make_async_copy(u_hbm.at[pl.ds(0, tile_rows)], xbuf.at[x_slot],
                              sem_in.at[x_slot]).wait()

    def scatter_wait(y_slot):
        pltpu.make_async_copy(ybuf.at[y_slot], y_hbm.at[pl.ds(0, tile_rows)],
                              sem_out.at[y_slot]).wait()

    @pl.when(i == 0)
    def _():
        ybuf[0] = jnp.zeros(ybuf.shape[1:], F32)
        for s in range(Y_SLOTS):
            pltpu.make_async_copy(ybuf.at[0], y_hbm.at[pl.ds(pad_base + s * tile_rows, tile_rows)],
                                  sem_out.at[s]).start()
        gather_block(tok0_ref, 0)
        gather_block(tok1_ref, 1)
        for s in range(Y_SLOTS):
            scatter_wait(s)

    @pl.when(i < n_used)
    def _():
        x_slot = lax.rem(i, X_SLOTS)
        y_slot = lax.rem(i, Y_SLOTS)

        @pl.when((i == 0) | (bexp_ref[i] != bexp_ref[jnp.maximum(i - 1, 0)]))
        def _():
            wg_b[...] = wg_ref[...].astype(BF16)
            wu_b[...] = wu_ref[...].astype(BF16)
            wd_b[...] = wd_ref[...].astype(BF16)

        gather_wait(x_slot)

        @pl.when(i >= Y_SLOTS)
        def _():
            scatter_wait(y_slot)

        x = _tiles_to_rows(xbuf.at[x_slot], MOE_BLOCK).astype(BF16)
        hg = _dot(x, wg_b[...])
        hu = _dot(x, wu_b[...])
        act = (hg * _sigmoid(hg) * hu).astype(BF16)
        y = _dot(act, wd_b[...]) * roww_ref[...]
        _rows_to_tiles(ybuf.at[y_slot], y)
        for r in range(MOE_BLOCK):
            pltpu.make_async_copy(ybuf.at[y_slot, pl.ds(r * ROW_TILES, ROW_TILES)],
                                  y_hbm.at[pl.ds(dst_ref[0, 0, r], ROW_TILES)],
                                  sem_out.at[y_slot]).start()
        gather_block(tok2_ref, lax.rem(i + 2, X_SLOTS))

        @pl.when(i == n_used - 1)
        def _():
            scatter_wait(y_slot)

            @pl.when(i >= 1)
            def _():
                scatter_wait(1 - y_slot)

            gather_wait(lax.rem(i + 1, X_SLOTS))
            gather_wait(lax.rem(i + 2, X_SLOTS))


def moe_experts(u_all, row_src, row_dst, row_w, block_exp, n_used, w_gate, w_up, w_down,
                n_assign):
    d, ff = w_gate.shape[1], w_gate.shape[2]
    n_blocks = row_src.shape[0]
    tile_rows = MOE_BLOCK * ROW_TILES
    smem_blk = lambda off: pl.BlockSpec(
        (1, 1, MOE_BLOCK), lambda i, be, nu: (jnp.minimum(i + off, n_blocks - 1), 0, 0),
        memory_space=pltpu.SMEM)
    grid_spec = pltpu.PrefetchScalarGridSpec(
        num_scalar_prefetch=2,
        grid=(n_blocks,),
        in_specs=[
            smem_blk(0), smem_blk(1), smem_blk(2), smem_blk(0),
            pl.BlockSpec((MOE_BLOCK, 1), lambda i, be, nu: (i, 0)),
            pl.BlockSpec(memory_space=pl.ANY),
            pl.BlockSpec((None, d, ff), lambda i, be, nu: (be[i], 0, 0)),
            pl.BlockSpec((None, d, ff), lambda i, be, nu: (be[i], 0, 0)),
            pl.BlockSpec((None, ff, d), lambda i, be, nu: (be[i], 0, 0)),
        ],
        out_specs=pl.BlockSpec(memory_space=pl.ANY),
        scratch_shapes=[
            pltpu.VMEM((X_SLOTS, tile_rows, LANES), F32),
            pltpu.VMEM((Y_SLOTS, tile_rows, LANES), F32),
            pltpu.VMEM((d, ff), BF16), pltpu.VMEM((d, ff), BF16), pltpu.VMEM((ff, d), BF16),
            pltpu.SemaphoreType.DMA((X_SLOTS,)), pltpu.SemaphoreType.DMA((Y_SLOTS,)),
        ],
    )
    y_rows = (n_assign + Y_SLOTS * MOE_BLOCK) * ROW_TILES
    return pl.pallas_call(
        _moe_expert_kernel,
        grid_spec=grid_spec,
        out_shape=jax.ShapeDtypeStruct((y_rows, LANES), F32),
        compiler_params=_cp(("arbitrary",)),
        name="moe_experts",
    )(block_exp, n_used, row_src, row_src, row_src, row_dst, row_w, u_all, w_gate, w_up, w_down)


def _moe_combine_kernel(h_ref, y0_ref, y1_ref, o_ref):
    rows = h_ref.shape[0]
    o_ref[...] = h_ref[...] + (_tiles_to_rows(y0_ref, rows) + _tiles_to_rows(y1_ref, rows))


def moe_combine(h, y_slots, row_off, slot_stride, *, tm):
    m, d = h.shape
    assert row_off % tm == 0 and slot_stride % tm == 0
    off0, off1 = row_off // tm, (row_off + slot_stride) // tm
    return pl.pallas_call(
        _moe_combine_kernel,
        grid=(m // tm,),
        in_specs=[pl.BlockSpec((tm, d), lambda i: (i, 0)),
                  pl.BlockSpec((tm * ROW_TILES, LANES), lambda i: (i + off0, 0)),
                  pl.BlockSpec((tm * ROW_TILES, LANES), lambda i: (i + off1, 0))],
        out_specs=pl.BlockSpec((tm, d), lambda i: (i, 0)),
        out_shape=jax.ShapeDtypeStruct((m, d), F32),
        compiler_params=_cp(("parallel",)),
        name="moe_combine",
    )(h, y_slots, y_slots)


def moe_dispatch(e_idx, gates, slot_stride):
    m = e_idx.shape[0]
    a = m * TOP_K
    e_flat = e_idx.reshape(a)
    order = jnp.argsort(e_flat, stable=True).astype(jnp.int32)
    counts = jnp.sum(e_flat[:, None] == jnp.arange(N_EXPERTS, dtype=jnp.int32)[None, :],
                     axis=0, dtype=jnp.int32)
    pad_counts = (counts + MOE_BLOCK - 1) // MOE_BLOCK * MOE_BLOCK
    starts = jnp.cumsum(counts) - counts
    pad_ends = jnp.cumsum(pad_counts)
    pad_starts = pad_ends - pad_counts
    n_blocks = a // MOE_BLOCK + N_EXPERTS
    p = n_blocks * MOE_BLOCK
    n_used = (pad_ends[-1] // MOE_BLOCK).astype(jnp.int32)
    blk = jnp.arange(n_blocks, dtype=jnp.int32)
    blk_start = jnp.minimum(blk, n_used - 1) * MOE_BLOCK
    block_exp = jnp.minimum(jnp.searchsorted(pad_ends, blk_start, side='right'),
                            N_EXPERTS - 1).astype(jnp.int32)
    in_exp = blk * MOE_BLOCK - pad_starts[block_exp]
    row_cnt = jnp.where(blk < n_used, jnp.clip(counts[block_exp] - in_exp, 0, MOE_BLOCK), 0)
    lane = jnp.arange(MOE_BLOCK, dtype=jnp.int32)[None, :]
    valid = lane < row_cnt[:, None]
    src = jnp.clip((starts[block_exp] + in_exp)[:, None] + lane, 0, a - 1)
    assign = order[src]
    row_tok = jnp.where(valid, assign // TOP_K, 0)
    pad_dst = TOP_K * slot_stride + (blk % Y_SLOTS)[:, None] * MOE_BLOCK + lane
    row_dst = jnp.where(valid, (assign % TOP_K) * slot_stride + assign // TOP_K, pad_dst)
    row_w = jnp.where(valid, gates.reshape(a)[assign], 0.0)
    as_blocks = lambda x: (x * ROW_TILES).astype(jnp.int32).reshape(n_blocks, 1, MOE_BLOCK)
    return as_blocks(row_tok), as_blocks(row_dst), row_w.reshape(p, 1), block_exp, n_used.reshape(1)


def rwkv_params(rw_mu, rw_w0, rw_w2, rw_a0, rw_a2, rw_g2, rw_k_k, rw_k_a, rw_r_k):
    c = RWKV_WIDTH
    mu = jnp.pad(rw_mu, (0, RWKV_PROJ_PAD - RWKV_PROJ)).reshape(1, RWKV_PROJ_PAD)
    w_lora = jnp.zeros((LANES, 2 * c), F32)
    w_lora = w_lora.at[0:DECAY_LORA, 0:c].set(rw_w2).at[DECAY_LORA:LANES, c:2 * c].set(rw_a2)
    w_gate = jnp.pad(rw_g2, ((0, GATE_PAD - GATE_LORA), (0, 0)))
    vec = lambda x: x.reshape(1, c)
    return (mu, vec(rw_w0), vec(rw_a0), vec(rw_k_k), vec(rw_k_a), vec(rw_r_k),
            w_lora.astype(BF16), w_gate.astype(BF16))


def _token_tiles(m):
    return (1024, 512) if m % 1024 == 0 else (m, m)


def _dense_front(x2d, wts, tm):
    pa = norm_matmul(x2d, wts['ln1_w'], wts['w_att'], tm=tm, tn=512)
    pr = norm_matmul(x2d, wts['ln1_w'], wts['w_rw'], tm=tm, tn=RWKV_PROJ_PAD // 3)
    return pa, pr


def _dense_back(x2d, att2d, rw2d, wts, xattn_fn, tm):
    h1 = matmul_residual([att2d, rw2d], [wts['w_out_a'], wts['w_out_r']], x2d, tm=tm, tn=512)
    qx = norm_matmul(h1, wts['ln2_w'], wts['xq_w'], tm=tm, tn=XATT_WIDTH)
    ox = xattn_fn(qx)
    return matmul_residual([ox], [wts['xo_w']], h1, tm=tm, tn=512)


def kernel(x_prompt, x_sample, cache_win_k, cache_win_v, state_wkv, state_shift, cache_mem_k, cache_mem_v, mem_prompt, ln1_w, w_in, q_norm_w, k_norm_w, attn_sinks, rw_mu, rw_w0, rw_w2, rw_a0, rw_a2, rw_g2, rw_k_k, rw_k_a, rw_r_k, rw_ln_w, rw_ln_b, w_out, ln2_w, mem_norm_w, xq_w, xkv_w, xq_norm_w, xk_norm_w, xo_w, ln3_w, router_group_w, router_group_b, router_expert_w, router_expert_b, exp_w_gate, exp_w_up, exp_w_down):
    assert w_in.shape[0] == 1, "single-layer stack"
    bp, seq, d = x_prompt.shape
    bs = x_sample.shape[0]
    mp = bp * seq
    c = RWKV_WIDTH

    router_w = jnp.concatenate(
        [router_group_w[0], router_expert_w[0],
         jnp.zeros((d, ROUTER_LANES - N_EXPERT_GROUPS - N_EXPERTS), F32)], axis=1)
    router_hi = router_w.astype(BF16)
    wts = {
        'ln1_w': ln1_w[0], 'ln2_w': ln2_w[0], 'ln3_w': ln3_w[0],
        'w_att': w_in[0][:, :ATT_PROJ].astype(BF16),
        'w_rw': jnp.pad(w_in[0][:, ATT_PROJ:],
                        ((0, 0), (0, RWKV_PROJ_PAD - RWKV_PROJ))).astype(BF16),
        'w_out_a': w_out[0][:ATT_WIDTH].astype(BF16),
        'w_out_r': w_out[0][ATT_WIDTH:].astype(BF16),
        'xq_w': xq_w[0].astype(BF16), 'xo_w': xo_w[0].astype(BF16),
        'router_hi': router_hi,
        'router_lo': (router_w - router_hi.astype(F32)).astype(BF16),
        'router_b': jnp.pad(jnp.concatenate([router_group_b[0], router_expert_b[0]]),
                            (0, ROUTER_LANES - N_EXPERT_GROUPS - N_EXPERTS)).reshape(1, -1),
    }
    rw_par = rwkv_params(rw_mu[0], rw_w0[0], rw_w2[0], rw_a0[0], rw_a2[0], rw_g2[0],
                         rw_k_k[0], rw_k_a[0], rw_r_k[0])

    tm_p, te_p = _token_tiles(mp)
    xp = x_prompt.reshape(mp, d)
    pa, pr = _dense_front(xp, wts, tm_p)
    pa3 = pa.reshape(bp, seq, ATT_PROJ)
    pr3 = pr.reshape(bp, seq, RWKV_PROJ_PAD)
    tabs_p = rope_tables(jnp.arange(seq, dtype=jnp.int32))
    att_p, kn_p = swa_prompt(pa3, tabs_p, q_norm_w[0], k_norm_w[0], attn_sinks[0])
    prep = rwkv_prep_seq(pr3, jnp.zeros((bp, 1, RWKV_PROJ_PAD), F32), rw_par)
    r, ld, kp, v, kk, b, bonus, g = prep
    y_p, wkv_p = rwkv_scan(r, ld, kp, v, kk, b)
    rw_p = rwkv_post(y_p.reshape(mp, c), bonus.reshape(mp, c), g.reshape(mp, c),
                     rw_ln_w[0], rw_ln_b[0], tm=te_p)

    n_mem = mem_prompt.shape[1]
    kv_mem = norm_matmul(mem_prompt.reshape(bp * n_mem, d), mem_norm_w[0],
                         xkv_w[0].astype(BF16), tm=bp * n_mem, tn=512)
    mem_k = head_rms(kv_mem[:, :XATT_WIDTH], xk_norm_w[0])
    mem_v = kv_mem[:, XATT_WIDTH:]
    mem_k3 = mem_k.reshape(bp, n_mem, XATT_WIDTH)
    mem_v3 = mem_v.reshape(bp, n_mem, XATT_WIDTH)

    def xattn_p(qx):
        return xattn_prompt(qx.reshape(bp, seq, XATT_WIDTH), mem_k3, mem_v3,
                            xq_norm_w[0]).reshape(mp, XATT_WIDTH)

    h2_p = _dense_back(xp, att_p.reshape(mp, ATT_WIDTH), rw_p, wts, xattn_p, tm_p)

    tm_s, te_s = _token_tiles(bs)
    xs = x_sample.reshape(bs, d)
    sa, sr = _dense_front(xs, wts, tm_s)
    tabs_s = rope_tables(PAST_LEN + jnp.arange(1, dtype=jnp.int32))
    qk_w = jnp.concatenate([jnp.tile(q_norm_w[0], ATT_HEADS),
                            jnp.tile(k_norm_w[0], ATT_KV_HEADS)]).reshape(1, -1)
    qk = qk_norm_rope(sa[:, :ATT_WIDTH + KV_WIDTH], qk_w, tabs_s)
    nbuf = cache_win_k.shape[2]
    att_s, win_k, win_v = swa_decode(
        qk[:, :ATT_WIDTH].reshape(bs, ATT_HEADS, HEAD_DIM),
        qk[:, ATT_WIDTH:].reshape(bs, 1, KV_WIDTH),
        sa[:, ATT_WIDTH + KV_WIDTH:].reshape(bs, 1, KV_WIDTH),
        cache_win_k[0].reshape(bs, nbuf, KV_WIDTH), cache_win_v[0].reshape(bs, nbuf, KV_WIDTH),
        attn_sinks[0])
    shift_prev = jnp.pad(state_shift[0], ((0, 0), (0, RWKV_PROJ_PAD - RWKV_PROJ)))
    r, ld, kp, v, kk, b, bonus, g = rwkv_prep_tok(sr, shift_prev, rw_par)
    gh = bs * RWKV_HEADS
    rows = [t.reshape(gh, 1, RWKV_HEAD_DIM) for t in (r, ld, kp, v, kk, b)]
    y_s, wkv_s = rwkv_step(*rows, state_wkv[0].reshape(gh, RWKV_HEAD_DIM, RWKV_HEAD_DIM))
    rw_s = rwkv_post(y_s.reshape(bs, c), bonus, g, rw_ln_w[0], rw_ln_b[0], tm=te_s)
    def xattn_s(qx):
        q_pad = jnp.pad(qx.reshape(bs, XATT_HEADS, XATT_HEAD_DIM), ((0, 0), (0, 4), (0, 0)))
        o = xattn_decode(q_pad, cache_mem_k, cache_mem_v, xq_norm_w[0])
        return o[:, :XATT_HEADS].reshape(bs, XATT_WIDTH)

    h2_s = _dense_back(xs, att_s.reshape(bs, ATT_WIDTH), rw_s, wts, xattn_s, tm_s)

    m_all = mp + bs
    slot_stride = m_all
    tc = math.gcd(mp, bs, 512)
    u_all, idx_all, gate_all = moe_router(h2_p, h2_s, wts['ln3_w'], wts['router_hi'],
                                          wts['router_lo'], wts['router_b'], tm=tc)
    row_src, row_dst, row_w, block_exp, n_used = moe_dispatch(
        idx_all[:, :TOP_K], gate_all[:, :TOP_K], slot_stride)
    y_slots = moe_experts(u_all, row_src, row_dst, row_w, block_exp, n_used,
                          exp_w_gate[0], exp_w_up[0], exp_w_down[0], TOP_K * slot_stride)
    out_p = moe_combine(h2_p, y_slots, 0, slot_stride, tm=tc)
    out_s = moe_combine(h2_s, y_slots, mp, slot_stride, tm=tc)

    win = min(WINDOW, seq)
    kv_shape = (1, bp, win, ATT_KV_HEADS, HEAD_DIM)
    return (
        out_p.reshape(bp, seq, d),
        out_s.reshape(bs, 1, d),
        kn_p[:, seq - win:].reshape(kv_shape),
        pa3[:, seq - win:, ATT_WIDTH + KV_WIDTH:].reshape(kv_shape),
        wkv_p[None],
        pr3[:, seq - 1, :RWKV_PROJ][None],
        mem_k3.reshape(1, bp, n_mem, XATT_HEADS, XATT_HEAD_DIM),
        mem_v3.reshape(1, bp, n_mem, XATT_HEADS, XATT_HEAD_DIM),
        win_k.reshape(1, bs, nbuf, ATT_KV_HEADS, HEAD_DIM),
        win_v.reshape(1, bs, nbuf, ATT_KV_HEADS, HEAD_DIM),
        wkv_s.reshape(1, bs, RWKV_HEADS, RWKV_HEAD_DIM, RWKV_HEAD_DIM),
        sr[:, :RWKV_PROJ].reshape(1, bs, RWKV_PROJ),
    )
```

```python
import functools
import math

import jax
import jax.numpy as jnp
from jax import lax
from jax.experimental import pallas as pl
from jax.experimental.pallas import tpu as pltpu

F32 = jnp.float32
BF16 = jnp.bfloat16

D_MODEL = 2048
HEAD_DIM = 64
ATT_HEADS = 16
ATT_KV_HEADS = 4
ATT_GROUP = ATT_HEADS // ATT_KV_HEADS
ATT_WIDTH = ATT_HEADS * HEAD_DIM
KV_WIDTH = ATT_KV_HEADS * HEAD_DIM
ATT_PROJ = ATT_WIDTH + 2 * KV_WIDTH
WINDOW = 128
ATT_SCALE = HEAD_DIM ** -0.5
ROPE_THETA = 500000.0
ROT_DIM = HEAD_DIM // 4
PAST_LEN = 16384

RWKV_WIDTH = 1024
RWKV_HEAD_DIM = 64
RWKV_HEADS = 16
DECAY_LORA = 64
AAA_LORA = 64
GATE_LORA = 160
RWKV_PROJ = 3 * RWKV_WIDTH + DECAY_LORA + AAA_LORA + GATE_LORA
RWKV_PROJ_PAD = 3456

N_MEM = 256
XATT_HEADS = 4
XATT_HEAD_DIM = 128
XATT_WIDTH = XATT_HEADS * XATT_HEAD_DIM

N_EXPERT_GROUPS = 8
EXPERTS_PER_GROUP = 8
N_EXPERTS = 64
TOP_K = 2
EXPERT_FF = D_MODEL // 4
MOE_BLOCK = 128

RMS_EPS = 1e-6
GN_EPS = 64e-5

LANES = 128
CHUNK = 64
VMEM_LIMIT = 56 * 1024 * 1024


def _cp(sem, vmem=VMEM_LIMIT):
    return pltpu.CompilerParams(dimension_semantics=sem, vmem_limit_bytes=vmem)


def _rms_rows(x, w):
    ms = jnp.mean(x * x, axis=-1, keepdims=True)
    return x * lax.rsqrt(ms + RMS_EPS) * w


def _split2(x):
    hi = x.astype(BF16)
    lo = (x - hi.astype(F32)).astype(BF16)
    return hi, lo


def _split3(x):
    h1 = x.astype(BF16)
    r1 = x - h1.astype(F32)
    h2 = r1.astype(BF16)
    h3 = (r1 - h2.astype(F32)).astype(BF16)
    return h1, h2, h3


def _dot(a, b):
    return jnp.dot(a, b, preferred_element_type=F32)


def _dot_nt(a, b):
    return lax.dot_general(a, b, (((1,), (1,)), ((), ())), preferred_element_type=F32)


def _group_sum(x, gmat):
    hi, lo = _split2(x)
    return _dot(hi, gmat) + _dot(lo, gmat)


def _head_indicator():
    r = lax.broadcasted_iota(jnp.int32, (LANES, LANES), 0) // HEAD_DIM
    c = lax.broadcasted_iota(jnp.int32, (LANES, LANES), 1) // HEAD_DIM
    return jnp.where(r == c, 1.0, 0.0).astype(BF16)


def _norm_mm_kernel(x_ref, lnw_ref, w_ref, o_ref, xn_ref):
    @pl.when(pl.program_id(1) == 0)
    def _():
        xn_ref[...] = _rms_rows(x_ref[...], lnw_ref[...]).astype(BF16)

    o_ref[...] = _dot(xn_ref[...], w_ref[...])


def norm_matmul(x, ln_w, w_bf16, *, tm, tn):
    m, k = x.shape
    n = w_bf16.shape[1]
    assert m % tm == 0 and n % tn == 0
    return pl.pallas_call(
        _norm_mm_kernel,
        grid=(m // tm, n // tn),
        in_specs=[
            pl.BlockSpec((tm, k), lambda i, j: (i, 0)),
            pl.BlockSpec((1, k), lambda i, j: (0, 0)),
            pl.BlockSpec((k, tn), lambda i, j: (0, j)),
        ],
        out_specs=pl.BlockSpec((tm, tn), lambda i, j: (i, j)),
        out_shape=jax.ShapeDtypeStruct((m, n), F32),
        scratch_shapes=[pltpu.VMEM((tm, k), BF16)],
        compiler_params=_cp(("parallel", "arbitrary")),
        name="norm_matmul",
    )(x, ln_w.reshape(1, k), w_bf16)


def _mm_res_kernel(*refs, n_lhs):
    a_refs = refs[:n_lhs]
    w_refs = refs[n_lhs:2 * n_lhs]
    res_ref = refs[2 * n_lhs]
    o_ref = refs[2 * n_lhs + 1]
    acc = res_ref[...]
    for a_ref, w_ref in zip(a_refs, w_refs):
        acc = acc + _dot(a_ref[...].astype(BF16), w_ref[...])
    o_ref[...] = acc


def matmul_residual(lhs_list, w_list, res, *, tm, tn):
    m, n = res.shape
    n_lhs = len(lhs_list)
    assert m % tm == 0 and n % tn == 0
    in_specs = [pl.BlockSpec((tm, a.shape[1]), lambda i, j: (i, 0)) for a in lhs_list]
    in_specs += [pl.BlockSpec((w.shape[0], tn), lambda i, j: (0, j)) for w in w_list]
    in_specs += [pl.BlockSpec((tm, tn), lambda i, j: (i, j))]
    return pl.pallas_call(
        functools.partial(_mm_res_kernel, n_lhs=n_lhs),
        grid=(m // tm, n // tn),
        in_specs=in_specs,
        out_specs=pl.BlockSpec((tm, tn), lambda i, j: (i, j)),
        out_shape=jax.ShapeDtypeStruct((m, n), F32),
        compiler_params=_cp(("parallel", "arbitrary")),
        name="matmul_residual",
    )(*lhs_list, *w_list, res)


def rope_tables(pos):
    half = ROT_DIM // 2
    inv = ROPE_THETA ** (-jnp.arange(half, dtype=F32) * 2.0 / ROT_DIM)
    ang = pos.astype(F32)[:, None] * inv[None, :]
    cos, sin = jnp.cos(ang), jnp.sin(ang)
    t = pos.shape[0]
    ones = jnp.ones((t, HEAD_DIM - ROT_DIM), F32)
    zeros = jnp.zeros((t, HEAD_DIM - ROT_DIM), F32)
    z8 = jnp.zeros((t, half), F32)
    cos_t = jnp.concatenate([cos, cos, ones], axis=1)
    sin_a = jnp.concatenate([z8, sin, zeros], axis=1)
    sin_b = jnp.concatenate([-sin, z8, zeros], axis=1)
    return tuple(jnp.concatenate([a, a], axis=1) for a in (cos_t, sin_a, sin_b))


def _norm_rope_chunk(x, w, cos_t, sin_a, sin_b, gmat):
    ms = _group_sum(x * x, gmat) * (1.0 / HEAD_DIM)
    xn = x * lax.rsqrt(ms + RMS_EPS) * w
    half = ROT_DIM // 2
    return (xn * cos_t + pltpu.roll(xn, half, axis=1) * sin_a
            + pltpu.roll(xn, LANES - half, axis=1) * sin_b)


def _norm_rope(x, w, tabs, gmat):
    chunks = [
        _norm_rope_chunk(x[:, c * LANES:(c + 1) * LANES], w, *tabs, gmat)
        for c in range(x.shape[1] // LANES)
    ]
    return chunks[0] if len(chunks) == 1 else jnp.concatenate(chunks, axis=1)


def _sink_softmax(s, sink):
    m = jnp.maximum(jnp.max(s, axis=-1, keepdims=True), sink)
    e = jnp.exp(s - m)
    return e / (jnp.sum(e, axis=-1, keepdims=True) + jnp.exp(sink - m))


def _swa_prompt_kernel(q_ref, kc_ref, vc_ref, kp_ref, vp_ref, cc_ref, sac_ref, sbc_ref,
                       cp_ref, sap_ref, sbp_ref, qw_ref, kw_ref, sink_ref, o_ref, kn_ref):
    n = pl.program_id(1)
    blk = q_ref.shape[0]
    gmat = _head_indicator()
    tabs_c = (cc_ref[...], sac_ref[...], sbc_ref[...])
    tabs_p = (cp_ref[...], sap_ref[...], sbp_ref[...])
    q = _norm_rope(q_ref[...], qw_ref[...], tabs_c, gmat)
    k_cur = _norm_rope(kc_ref[...], kw_ref[...], tabs_c, gmat)
    k_prev = _norm_rope(kp_ref[...], kw_ref[...], tabs_p, gmat)
    kn_ref[...] = k_cur
    k_all = jnp.concatenate([k_prev, k_cur], axis=0).astype(BF16)
    v_all = jnp.concatenate([vp_ref[...], vc_ref[...]], axis=0).astype(BF16)

    qi = lax.broadcasted_iota(jnp.int32, (blk, 2 * blk), 0) + blk
    si = lax.broadcasted_iota(jnp.int32, (blk, 2 * blk), 1)
    rel = qi - si
    valid = (rel >= 0) & (rel <= WINDOW) & ((n > 0) | (si >= blk))

    for kv in range(ATT_KV_HEADS):
        k_h = k_all[:, kv * HEAD_DIM:(kv + 1) * HEAD_DIM]
        v_h = v_all[:, kv * HEAD_DIM:(kv + 1) * HEAD_DIM]
        heads = [kv * ATT_GROUP + g for g in range(ATT_GROUP)]
        q_g = jnp.concatenate(
            [q[:, h * HEAD_DIM:(h + 1) * HEAD_DIM] for h in heads], axis=0).astype(BF16)
        s = _dot_nt(q_g, k_h) * ATT_SCALE
        probs = []
        for g, h in enumerate(heads):
            s_h = jnp.where(valid, s[g * blk:(g + 1) * blk], -jnp.inf)
            probs.append(_sink_softmax(s_h, sink_ref[h]))
        p = jnp.concatenate(probs, axis=0).astype(BF16)
        o = _dot(p, v_h)
        for g, h in enumerate(heads):
            o_ref[:, h * HEAD_DIM:(h + 1) * HEAD_DIM] = o[g * blk:(g + 1) * blk]


def swa_prompt(pa, tabs, q_norm_w, k_norm_w, sinks):
    b, t, _ = pa.shape
    blk = WINDOW
    nb = t // blk
    qb, kb, vb = 0, ATT_WIDTH // KV_WIDTH, ATT_WIDTH // KV_WIDTH + 1
    cur = lambda i, n, *_: (i, n, 0)
    tab_cur = pl.BlockSpec((blk, LANES), lambda i, n: (n, 0))
    tab_prev = pl.BlockSpec((blk, LANES), lambda i, n: (jnp.maximum(n - 1, 0), 0))
    qw = jnp.tile(q_norm_w.reshape(1, HEAD_DIM), (1, 2))
    kw = jnp.tile(k_norm_w.reshape(1, HEAD_DIM), (1, 2))
    return pl.pallas_call(
        _swa_prompt_kernel,
        grid=(b, nb),
        in_specs=[
            pl.BlockSpec((None, blk, ATT_WIDTH), lambda i, n: (i, n, qb)),
            pl.BlockSpec((None, blk, KV_WIDTH), lambda i, n: (i, n, kb)),
            pl.BlockSpec((None, blk, KV_WIDTH), lambda i, n: (i, n, vb)),
            pl.BlockSpec((None, blk, KV_WIDTH), lambda i, n: (i, jnp.maximum(n - 1, 0), kb)),
            pl.BlockSpec((None, blk, KV_WIDTH), lambda i, n: (i, jnp.maximum(n - 1, 0), vb)),
            tab_cur, tab_cur, tab_cur, tab_prev, tab_prev, tab_prev,
            pl.BlockSpec((1, LANES), lambda i, n: (0, 0)),
            pl.BlockSpec((1, LANES), lambda i, n: (0, 0)),
            pl.BlockSpec(memory_space=pltpu.SMEM),
        ],
        out_specs=[
            pl.BlockSpec((None, blk, ATT_WIDTH), cur),
            pl.BlockSpec((None, blk, KV_WIDTH), cur),
        ],
        out_shape=[
            jax.ShapeDtypeStruct((b, t, ATT_WIDTH), F32),
            jax.ShapeDtypeStruct((b, t, KV_WIDTH), F32),
        ],
        compiler_params=_cp(("parallel", "arbitrary")),
        name="swa_prompt",
    )(pa, pa, pa, pa, pa, *tabs, *tabs, qw, kw, sinks)


def _qk_norm_rope_kernel(x_ref, w_ref, c_ref, sa_ref, sb_ref, o_ref):
    gmat = _head_indicator()
    tabs = (c_ref[...], sa_ref[...], sb_ref[...])
    for c in range(x_ref.shape[1] // LANES):
        sl = slice(c * LANES, (c + 1) * LANES)
        o_ref[:, sl] = _norm_rope_chunk(x_ref[:, sl], w_ref[:, sl], *tabs, gmat)


def qk_norm_rope(x, w_row, tabs):
    m, w = x.shape
    full = lambda *shape: pl.BlockSpec(shape, lambda: (0,) * len(shape))
    return pl.pallas_call(
        _qk_norm_rope_kernel,
        in_specs=[full(m, w), full(1, w), full(1, LANES), full(1, LANES), full(1, LANES)],
        out_specs=full(m, w),
        out_shape=jax.ShapeDtypeStruct((m, w), F32),
        name="qk_norm_rope",
    )(x, w_row, *tabs)


def _swa_decode_kernel(q_ref, kn_ref, vn_ref, ck_ref, cv_ref, sink_ref, o_ref, kw_ref, vw_ref):
    bb = q_ref.shape[0]
    nbuf = ck_ref.shape[1]
    row_kv = lax.broadcasted_iota(jnp.int32, (ATT_HEADS, KV_WIDTH), 0) // ATT_GROUP
    lane_kv = lax.broadcasted_iota(jnp.int32, (ATT_HEADS, KV_WIDTH), 1) // HEAD_DIM
    own = row_kv == lane_kv
    sink = sink_ref[...]
    for b in range(bb):
        q2 = q_ref[b]
        q_exp = jnp.where(own, jnp.concatenate([q2] * ATT_KV_HEADS, axis=1), 0.0)
        k_new, v_new = kn_ref[b], vn_ref[b]
        k_buf, v_buf = ck_ref[b], cv_ref[b]
        s_buf = _dot_nt(q_exp.astype(BF16), k_buf.astype(BF16)) * ATT_SCALE
        s_new = jnp.sum(q_exp * k_new, axis=-1, keepdims=True) * ATT_SCALE
        m = jnp.maximum(jnp.maximum(jnp.max(s_buf, axis=-1, keepdims=True), s_new), sink)
        e_buf = jnp.exp(s_buf - m)
        e_new = jnp.exp(s_new - m)
        inv = 1.0 / (jnp.sum(e_buf, axis=-1, keepdims=True) + e_new + jnp.exp(sink - m))
        o = _dot((e_buf * inv).astype(BF16), v_buf.astype(BF16)) + (e_new * inv) * v_new
        o = jnp.where(own, o, 0.0)
        o_ref[b] = (o[:, 0:HEAD_DIM] + o[:, HEAD_DIM:2 * HEAD_DIM]
                    + o[:, 2 * HEAD_DIM:3 * HEAD_DIM] + o[:, 3 * HEAD_DIM:4 * HEAD_DIM])
        kw_ref[b, 0:nbuf - 1, :] = k_buf[1:nbuf]
        kw_ref[b, nbuf - 1:nbuf, :] = k_new
        vw_ref[b, 0:nbuf - 1, :] = v_buf[1:nbuf]
        vw_ref[b, nbuf - 1:nbuf, :] = v_new


def swa_decode(q, k_new, v_new, cache_k, cache_v, sinks, *, bb=8):
    b, nbuf, _ = cache_k.shape
    blk3 = lambda s1, s2: pl.BlockSpec((bb, s1, s2), lambda i: (i, 0, 0))
    return pl.pallas_call(
        _swa_decode_kernel,
        grid=(b // bb,),
        in_specs=[
            blk3(ATT_HEADS, HEAD_DIM), blk3(1, KV_WIDTH), blk3(1, KV_WIDTH),
            blk3(nbuf, KV_WIDTH), blk3(nbuf, KV_WIDTH),
            pl.BlockSpec((ATT_HEADS, 1), lambda i: (0, 0)),
        ],
        out_specs=[blk3(ATT_HEADS, HEAD_DIM), blk3(nbuf, KV_WIDTH), blk3(nbuf, KV_WIDTH)],
        out_shape=[
            jax.ShapeDtypeStruct((b, ATT_HEADS, HEAD_DIM), F32),
            jax.ShapeDtypeStruct((b, nbuf, KV_WIDTH), F32),
            jax.ShapeDtypeStruct((b, nbuf, KV_WIDTH), F32),
        ],
        compiler_params=_cp(("parallel",)),
        name="swa_decode",
    )(q, k_new, v_new, cache_k, cache_v, sinks.reshape(ATT_HEADS, 1))


def _head_rms_kernel(x_ref, w_ref, o_ref):
    for h in range(x_ref.shape[1] // XATT_HEAD_DIM):
        sl = slice(h * XATT_HEAD_DIM, (h + 1) * XATT_HEAD_DIM)
        o_ref[:, sl] = _rms_rows(x_ref[:, sl], w_ref[...])


def head_rms(x, w):
    m, wd = x.shape
    return pl.pallas_call(
        _head_rms_kernel,
        in_specs=[pl.BlockSpec((m, wd), lambda: (0, 0)),
                  pl.BlockSpec((1, XATT_HEAD_DIM), lambda: (0, 0))],
        out_specs=pl.BlockSpec((m, wd), lambda: (0, 0)),
        out_shape=jax.ShapeDtypeStruct((m, wd), F32),
        name="head_rms",
    )(x, w.reshape(1, XATT_HEAD_DIM))


def _xattn_prompt_kernel(q_ref, k_ref, v_ref, w_ref, o_ref):
    scale = 1.0 / math.sqrt(XATT_HEAD_DIM)
    for h in range(XATT_HEADS):
        sl = slice(h * XATT_HEAD_DIM, (h + 1) * XATT_HEAD_DIM)
        qn = _rms_rows(q_ref[:, sl], w_ref[...]).astype(BF16)
        s = _dot_nt(qn, k_ref[:, sl].astype(BF16)) * scale
        e = jnp.exp(s - jnp.max(s, axis=-1, keepdims=True))
        p = e / jnp.sum(e, axis=-1, keepdims=True)
        o_ref[:, sl] = _dot(p.astype(BF16), v_ref[:, sl].astype(BF16))


def xattn_prompt(q, mem_k, mem_v, xq_norm_w, *, tq=512):
    b, t, w = q.shape
    n_mem = mem_k.shape[1]
    return pl.pallas_call(
        _xattn_prompt_kernel,
        grid=(b, t // tq),
        in_specs=[
            pl.BlockSpec((None, tq, w), lambda i, j: (i, j, 0)),
            pl.BlockSpec((None, n_mem, w), lambda i, j: (i, 0, 0)),
            pl.BlockSpec((None, n_mem, w), lambda i, j: (i, 0, 0)),
            pl.BlockSpec((1, XATT_HEAD_DIM), lambda i, j: (0, 0)),
        ],
        out_specs=pl.BlockSpec((None, tq, w), lambda i, j: (i, j, 0)),
        out_shape=jax.ShapeDtypeStruct((b, t, w), F32),
        compiler_params=_cp(("parallel", "arbitrary")),
        name="xattn_prompt",
    )(q, mem_k, mem_v, xq_norm_w.reshape(1, XATT_HEAD_DIM))


def _xattn_decode_kernel(q_ref, k_ref, v_ref, w_ref, o_ref):
    bb, rows, _ = q_ref.shape
    scale = 1.0 / math.sqrt(XATT_HEAD_DIM)
    row = lax.broadcasted_iota(jnp.int32, (rows, 1), 0)
    for b in range(bb):
        qn = _rms_rows(q_ref[b], w_ref[...]).astype(BF16)
        s = sum(jnp.where(row == h, _dot_nt(qn, k_ref[b, :, h, :].astype(BF16)), 0.0)
                for h in range(XATT_HEADS)) * scale
        e = jnp.exp(s - jnp.max(s, axis=-1, keepdims=True))
        p = (e / jnp.sum(e, axis=-1, keepdims=True)).astype(BF16)
        o_ref[b] = sum(jnp.where(row == h, _dot(p, v_ref[b, :, h, :].astype(BF16)), 0.0)
                       for h in range(XATT_HEADS))


def xattn_decode(q_pad, mem_k, mem_v, xq_norm_w, *, bb=8):
    b, rows, _ = q_pad.shape
    n_mem = mem_k.shape[2]
    kv = pl.BlockSpec((None, bb, n_mem, XATT_HEADS, XATT_HEAD_DIM), lambda i: (0, i, 0, 0, 0))
    return pl.pallas_call(
        _xattn_decode_kernel,
        grid=(b // bb,),
        in_specs=[pl.BlockSpec((bb, rows, XATT_HEAD_DIM), lambda i: (i, 0, 0)), kv, kv,
                  pl.BlockSpec((1, XATT_HEAD_DIM), lambda i: (0, 0))],
        out_specs=pl.BlockSpec((bb, rows, XATT_HEAD_DIM), lambda i: (i, 0, 0)),
        out_shape=jax.ShapeDtypeStruct((b, rows, XATT_HEAD_DIM), F32),
        compiler_params=_cp(("parallel",)),
        name="xattn_decode",
    )(q_pad, mem_k, mem_v, xq_norm_w.reshape(1, XATT_HEAD_DIM))


LORA_OFF = 3 * RWKV_WIDTH
GATE_OFF = LORA_OFF + DECAY_LORA + AAA_LORA
GATE_PAD = RWKV_PROJ_PAD - GATE_OFF


def _sigmoid(x):
    return 1.0 / (1.0 + jnp.exp(-x))


def _per_chunk(fn, *arrays):
    w = arrays[0].shape[1]
    outs = [fn(*(a[:, c * LANES:(c + 1) * LANES] for a in arrays)) for c in range(w // LANES)]
    return jnp.concatenate(outs, axis=1)


def _rwkv_prep_core(pr, prev, mu, w0, a0, kk_w, ka_w, rk_w, w_lora, w_gate):
    c = RWKV_WIDTH
    gmat = _head_indicator()
    xm = pr + (prev - pr) * mu
    r, k, v = xm[:, 0:c], xm[:, c:2 * c], xm[:, 2 * c:3 * c]
    lora = xm[:, LORA_OFF:LORA_OFF + LANES]
    lane = lax.broadcasted_iota(jnp.int32, lora.shape, 1)
    lora_in = jnp.where(lane < DECAY_LORA, jnp.tanh(lora), lora)
    wa = _dot(lora_in.astype(BF16), w_lora)
    z = -(w0 + wa[:, 0:c])
    softplus = jnp.maximum(z, 0.0) + jnp.log(1.0 + jnp.exp(-jnp.abs(z)))
    log_decay = -jnp.exp(-softplus - 0.5)
    a = _sigmoid(a0 + wa[:, c:2 * c])
    g = _dot(_sigmoid(xm[:, GATE_OFF:GATE_OFF + GATE_PAD]).astype(BF16), w_gate)
    kk = k * kk_w
    norm = jnp.sqrt(_per_chunk(lambda t: _group_sum(t * t, gmat), kk))
    kk = kk / jnp.maximum(norm, 1e-12)
    kp = k * (1.0 + (a - 1.0) * ka_w)
    bonus = _per_chunk(lambda t: _group_sum(t, gmat), r * kp * rk_w) * v
    return r, log_decay, kp, v, kk, kk * a, bonus, g


def _rwkv_prep_seq_kernel(pr_ref, prev0_ref, mu_ref, w0_ref, a0_ref, kkw_ref, kaw_ref, rkw_ref,
                          wl_ref, wg_ref, *refs):
    out_refs, last_ref = refs[:-1], refs[-1]

    @pl.when(pl.program_id(1) == 0)
    def _():
        last_ref[...] = prev0_ref[...]

    pr = pr_ref[...]
    rows = pr.shape[0]
    row = lax.broadcasted_iota(jnp.int32, (rows, 1), 0)
    prev = jnp.where(row == 0, last_ref[...], pltpu.roll(pr, 1, axis=0))
    last_ref[...] = pr[rows - 1:rows, :]
    outs = _rwkv_prep_core(pr, prev, mu_ref[...], w0_ref[...], a0_ref[...], kkw_ref[...],
                           kaw_ref[...], rkw_ref[...], wl_ref[...], wg_ref[...])
    for o_ref, o in zip(out_refs, outs):
        o_ref[...] = o


def _rwkv_prep_tok_kernel(pr_ref, prev_ref, mu_ref, w0_ref, a0_ref, kkw_ref, kaw_ref, rkw_ref,
                          wl_ref, wg_ref, *out_refs):
    outs = _rwkv_prep_core(pr_ref[...], prev_ref[...], mu_ref[...], w0_ref[...], a0_ref[...],
                           kkw_ref[...], kaw_ref[...], rkw_ref[...], wl_ref[...], wg_ref[...])
    for o_ref, o in zip(out_refs, outs):
        o_ref[...] = o


def _rwkv_param_specs(index_map):
    c = RWKV_WIDTH
    shapes = [(1, RWKV_PROJ_PAD)] + [(1, c)] * 5 + [(LANES, 2 * c), (GATE_PAD, c)]
    return [pl.BlockSpec(s, index_map) for s in shapes]


def rwkv_prep_seq(pr, prev0, params, *, tm=256):
    b, t, wd = pr.shape
    c = RWKV_WIDTH
    out = jax.ShapeDtypeStruct((b, t, c), F32)
    return pl.pallas_call(
        _rwkv_prep_seq_kernel,
        grid=(b, t // tm),
        in_specs=[pl.BlockSpec((None, tm, wd), lambda i, j: (i, j, 0)),
                  pl.BlockSpec((None, 1, wd), lambda i, j: (i, 0, 0))]
        + _rwkv_param_specs(lambda i, j: (0, 0)),
        out_specs=[pl.BlockSpec((None, tm, c), lambda i, j: (i, j, 0))] * 8,
        out_shape=[out] * 8,
        scratch_shapes=[pltpu.VMEM((1, wd), F32)],
        compiler_params=_cp(("parallel", "arbitrary")),
        name="rwkv_prep_seq",
    )(pr, prev0, *params)


def rwkv_prep_tok(pr, prev, params):
    m, wd = pr.shape
    c = RWKV_WIDTH
    out = jax.ShapeDtypeStruct((m, c), F32)
    return pl.pallas_call(
        _rwkv_prep_tok_kernel,
        grid=(1,),
        in_specs=[pl.BlockSpec((m, wd), lambda i: (0, 0))] * 2
        + _rwkv_param_specs(lambda i: (0, 0)),
        out_specs=[pl.BlockSpec((m, c), lambda i: (0, 0))] * 8,
        out_shape=[out] * 8,
        compiler_params=_cp(("arbitrary",)),
        name="rwkv_prep_tok",
    )(pr, prev, *params)


def _dot_tn(a, b):
    return lax.dot_general(a, b, (((0,), (0,)), ((), ())), preferred_element_type=F32)


def _rwkv_scan_kernel(r_ref, ld_ref, kp_ref, v_ref, kk_ref, b_ref, y_ref, s_out_ref, s_ref):
    @pl.when(pl.program_id(1) == 0)
    def _():
        s_ref[...] = jnp.zeros_like(s_ref)

    n = CHUNK
    ti = lax.broadcasted_iota(jnp.int32, (n, n), 0)
    si = lax.broadcasted_iota(jnp.int32, (n, n), 1)
    strict, incl = si < ti, si <= ti
    tri = jnp.where(incl, 1.0, 0.0).astype(BF16)

    ld = ld_ref[...]
    l1, l2, l3 = _split3(ld)
    lc = _dot(tri, l1) + _dot(tri, l2) + _dot(tri, l3)
    lc_end = lc[n - 1:n, :]
    e_neg = jnp.exp(-lc)
    kk, b, kp = kk_ref[...], b_ref[...], kp_ref[...]
    a_t = (-kk * jnp.exp(lc - ld)).astype(BF16)
    b_t = (b * e_neg).astype(BF16)
    k_t = (kp * e_neg).astype(BF16)
    r_t = (r_ref[...] * jnp.exp(lc)).astype(BF16)
    to_end = jnp.exp(lc_end - lc)
    b_e = (b * to_end).astype(BF16)
    k_e = (kp * to_end).astype(BF16)
    v_b = v_ref[...].astype(BF16)
    g_end = jnp.exp(lc_end)

    heads = range(RWKV_HEADS)
    sl = [slice(h * n, (h + 1) * n) for h in heads]
    gm = [_dot_nt(jnp.concatenate([a_t[:, sl[h]], r_t[:, sl[h]]], axis=0),
                  jnp.concatenate([b_t[:, sl[h]], k_t[:, sl[h]]], axis=0)) for h in heads]
    n_ab = [jnp.where(strict, gm[h][0:n, 0:n], 0.0).astype(BF16) for h in heads]
    l_ak = [jnp.where(strict, gm[h][0:n, n:2 * n], 0.0).astype(BF16) for h in heads]
    p_rb = [jnp.where(incl, gm[h][n:2 * n, 0:n], 0.0).astype(BF16) for h in heads]
    p_rk = [jnp.where(incl, gm[h][n:2 * n, n:2 * n], 0.0).astype(BF16) for h in heads]
    s0 = [s_ref[h] for h in heads]
    s0_b = [s0[h].astype(BF16) for h in heads]
    u = [_dot_nt(a_t[:, sl[h]], s0_b[h]) + _dot(l_ak[h], v_b[:, sl[h]]) for h in heads]
    pw = n_ab
    for step in range(6):
        u = [u[h] + _dot(pw[h], u[h].astype(BF16)) for h in heads]
        if step < 5:
            pw = [_dot(pw[h], pw[h]).astype(BF16) for h in heads]
    u_b = [u[h].astype(BF16) for h in heads]
    y = [_dot_nt(r_t[:, sl[h]], s0_b[h]) + _dot(p_rb[h], u_b[h]) + _dot(p_rk[h], v_b[:, sl[h]])
         for h in heads]
    s_new = [s0[h] * g_end[:, sl[h]] + _dot_tn(
        jnp.concatenate([u_b[h], v_b[:, sl[h]]], axis=0),
        jnp.concatenate([b_e[:, sl[h]], k_e[:, sl[h]]], axis=0)) for h in heads]
    for h in heads:
        y_ref[:, sl[h]] = y[h]
        s_ref[h] = s_new[h]
        s_out_ref[h] = s_new[h]


def rwkv_scan(r, ld, kp, v, kk, b):
    bsz, t, c = r.shape
    blk = pl.BlockSpec((None, CHUNK, c), lambda i, j: (i, j, 0))
    st = pl.BlockSpec((None, RWKV_HEADS, RWKV_HEAD_DIM, RWKV_HEAD_DIM), lambda i, j: (i, 0, 0, 0))
    return pl.pallas_call(
        _rwkv_scan_kernel,
        grid=(bsz, t // CHUNK),
        in_specs=[blk] * 6,
        out_specs=[blk, st],
        out_shape=[jax.ShapeDtypeStruct((bsz, t, c), F32),
                   jax.ShapeDtypeStruct((bsz, RWKV_HEADS, RWKV_HEAD_DIM, RWKV_HEAD_DIM), F32)],
        scratch_shapes=[pltpu.VMEM((RWKV_HEADS, RWKV_HEAD_DIM, RWKV_HEAD_DIM), F32)],
        compiler_params=_cp(("parallel", "arbitrary")),
        name="rwkv_scan",
    )(r, ld, kp, v, kk, b)


STEP_UNROLL = 8


def _rwkv_step_kernel(r_ref, ld_ref, kp_ref, v_ref, kk_ref, b_ref, s_ref, y_ref, s_out_ref):
    n = RWKV_HEAD_DIM
    eye = lax.broadcasted_iota(jnp.int32, (n, n), 0) == lax.broadcasted_iota(jnp.int32, (n, n), 1)

    def body(i, carry):
        gs = [i * STEP_UNROLL + j for j in range(STEP_UNROLL)]
        s = [s_ref[g] for g in gs]
        sa = [jnp.sum(s[j] * (-kk_ref[g]), axis=1, keepdims=True) for j, g in enumerate(gs)]
        v_col = [jnp.sum(jnp.where(eye, v_ref[g], 0.0), axis=1, keepdims=True) for g in gs]
        s_new = [s[j] * jnp.exp(ld_ref[g]) + sa[j] * b_ref[g] + v_col[j] * kp_ref[g]
                 for j, g in enumerate(gs)]
        y_col = [jnp.sum(s_new[j] * r_ref[g], axis=1, keepdims=True) for j, g in enumerate(gs)]
        for j, g in enumerate(gs):
            y_ref[g] = jnp.sum(jnp.where(eye, y_col[j], 0.0), axis=0, keepdims=True)
            s_out_ref[g] = s_new[j]
        return carry

    lax.fori_loop(0, s_ref.shape[0] // STEP_UNROLL, body, 0)


def rwkv_step(r, ld, kp, v, kk, b, state, *, gb=64):
    g = state.shape[0]
    n = RWKV_HEAD_DIM
    row = pl.BlockSpec((gb, 1, n), lambda i: (i, 0, 0))
    st = pl.BlockSpec((gb, n, n), lambda i: (i, 0, 0))
    return pl.pallas_call(
        _rwkv_step_kernel,
        grid=(g // gb,),
        in_specs=[row] * 6 + [st],
        out_specs=[row, st],
        out_shape=[jax.ShapeDtypeStruct((g, 1, n), F32), jax.ShapeDtypeStruct((g, n, n), F32)],
        compiler_params=_cp(("parallel",)),
        name="rwkv_step",
    )(r, ld, kp, v, kk, b, state)


def _rwkv_post_kernel(y_ref, bonus_ref, g_ref, lnw_ref, lnb_ref, o_ref):
    gmat = _head_indicator()
    inv = 1.0 / RWKV_HEAD_DIM
    for c in range(y_ref.shape[1] // LANES):
        sl = slice(c * LANES, (c + 1) * LANES)
        y = y_ref[:, sl]
        d = y - _group_sum(y, gmat) * inv
        var = _group_sum(d * d, gmat) * inv
        yn = d * lax.rsqrt(var + GN_EPS) * lnw_ref[:, sl] + lnb_ref[:, sl]
        o_ref[:, sl] = (yn + bonus_ref[:, sl]) * g_ref[:, sl]


def rwkv_post(y, bonus, g, ln_w, ln_b, *, tm):
    m, c = y.shape
    blk = pl.BlockSpec((tm, c), lambda i: (i, 0))
    vec = pl.BlockSpec((1, c), lambda i: (0, 0))
    return pl.pallas_call(
        _rwkv_post_kernel,
        grid=(m // tm,),
        in_specs=[blk, blk, blk, vec, vec],
        out_specs=blk,
        out_shape=jax.ShapeDtypeStruct((m, c), F32),
        compiler_params=_cp(("parallel",)),
        name="rwkv_post",
    )(y, bonus, g, ln_w.reshape(1, c), ln_b.reshape(1, c))


ROUTER_LANES = LANES
ROW_TILES = D_MODEL // LANES


def _rows_to_tiles(ref, x):
    rows = x.shape[0]
    for j in range(ROW_TILES):
        ref[pl.ds(j, rows, stride=ROW_TILES), :] = x[:, j * LANES:(j + 1) * LANES]


def _tiles_to_rows(ref, rows):
    return jnp.concatenate(
        [ref[pl.ds(j, rows, stride=ROW_TILES), :] for j in range(ROW_TILES)], axis=1)


def _router_kernel(ha_ref, hb_ref, lnw_ref, whi_ref, wlo_ref, bias_ref, u_ref, idx_ref, gate_ref,
                   *, steps_a):
    use_a = pl.program_id(0) < steps_a
    h = jnp.where(use_a, ha_ref[...], hb_ref[...])
    u = _rms_rows(h, lnw_ref[...])
    _rows_to_tiles(u_ref, u)
    u_hi, u_lo = _split2(u)
    w_hi = whi_ref[...]
    logits = _dot(u_hi, w_hi) + _dot(u_lo, w_hi) + _dot(u_hi, wlo_ref[...]) + bias_ref[...]
    lane = lax.broadcasted_iota(jnp.int32, logits.shape, 1)
    neg = -jnp.inf

    def first_max(x):
        m = jnp.max(x, axis=1, keepdims=True)
        return m, jnp.min(jnp.where(x == m, lane, ROUTER_LANES), axis=1, keepdims=True)

    gl = jnp.where(lane < N_EXPERT_GROUPS, logits, neg)
    g_max, g_idx = first_max(gl)
    g_gate = 1.0 / jnp.sum(jnp.exp(gl - g_max), axis=1, keepdims=True)
    lo = N_EXPERT_GROUPS + g_idx * EXPERTS_PER_GROUP
    el = jnp.where((lane >= lo) & (lane < lo + EXPERTS_PER_GROUP), logits, neg)
    v1, i1 = first_max(el)
    v2, i2 = first_max(jnp.where(lane == i1, neg, el))
    e2 = jnp.exp(v2 - v1)
    w1 = g_gate / (1.0 + e2)
    w2 = g_gate * e2 / (1.0 + e2)
    idx_ref[...] = jnp.where(lane == 0, i1 - N_EXPERT_GROUPS,
                             jnp.where(lane == 1, i2 - N_EXPERT_GROUPS, 0))
    gate_ref[...] = jnp.where(lane == 0, w1, jnp.where(lane == 1, w2, 0.0))


def moe_router(h_a, h_b, ln_w, w_hi, w_lo, bias, *, tm):
    (ma, d), mb = h_a.shape, h_b.shape[0]
    assert ma % tm == 0 and mb % tm == 0
    steps_a, steps_b = ma // tm, mb // tm
    m = ma + mb
    const = lambda r, w: pl.BlockSpec((r, w), lambda i: (0, 0))
    row = lambda w: pl.BlockSpec((tm, w), lambda i: (i, 0))
    return pl.pallas_call(
        functools.partial(_router_kernel, steps_a=steps_a),
        grid=(steps_a + steps_b,),
        in_specs=[pl.BlockSpec((tm, d), lambda i: (jnp.minimum(i, steps_a - 1), 0)),
                  pl.BlockSpec((tm, d), lambda i: (jnp.maximum(i - steps_a, 0), 0)),
                  const(1, d), const(d, ROUTER_LANES), const(d, ROUTER_LANES),
                  const(1, ROUTER_LANES)],
        out_specs=[pl.BlockSpec((tm * (d // LANES), LANES), lambda i: (i, 0)),
                   row(ROUTER_LANES), row(ROUTER_LANES)],
        out_shape=[jax.ShapeDtypeStruct((m * (d // LANES), LANES), F32),
                   jax.ShapeDtypeStruct((m, ROUTER_LANES), jnp.int32),
                   jax.ShapeDtypeStruct((m, ROUTER_LANES), F32)],
        compiler_params=_cp(("arbitrary",)),
        name="moe_router",
    )(h_a, h_b, ln_w.reshape(1, d), w_hi, w_lo, bias)


X_SLOTS = 3
Y_SLOTS = 2


W_SLOTS = 3
ROW_DMA_PRIORITY = 1


def _moe_expert_kernel(run_ref, rexp_ref, nused_ref, tok0_ref, tok1_ref, tok2_ref, dst_ref,
                       roww_ref, u_hbm, wg_hbm, wu_hbm, wd_hbm, y_hbm, xbuf, ybuf, wg_f, wu_f,
                       wd_f, wg_b, wu_b, wd_b, sem_in, sem_out, sem_w):
    i = pl.program_id(0)
    n_used, n_runs = nused_ref[0], nused_ref[1]
    tile_rows = MOE_BLOCK * ROW_TILES
    pad_base = y_hbm.shape[0] - Y_SLOTS * tile_rows
    run = run_ref[i]

    def weight_copies(k):
        e, s = rexp_ref[jnp.minimum(k, n_runs - 1)], lax.rem(k, W_SLOTS)
        return [pltpu.make_async_copy(hbm.at[e], buf.at[s], sem_w.at[s])
                for hbm, buf in ((wg_hbm, wg_f), (wu_hbm, wu_f), (wd_hbm, wd_f))]

    def gather_block(idx_ref, x_slot):
        for r in range(MOE_BLOCK):
            pltpu.make_async_copy(u_hbm.at[pl.ds(idx_ref[0, 0, r], ROW_TILES)],
                                  xbuf.at[x_slot, pl.ds(r * ROW_TILES, ROW_TILES)],
                                  sem_in.at[x_slot]).start(priority=ROW_DMA_PRIORITY)

    def gather_wait(x_slot):
        pltpu.make_async_copy(u_hbm.at[pl.ds(0, tile_rows)], xbuf.at[x_slot],
                              sem_in.at[x_slot]).wait()

    def scatter_wait(y_slot):
        pltpu.make_async_copy(ybuf.at[y_slot], y_hbm.at[pl.ds(0, tile_rows)],
                              sem_out.at[y_slot]).wait()

    @pl.when(i == 0)
    def _():
        ybuf[0] = jnp.zeros(ybuf.shape[1:], F32)
        for s in range(Y_SLOTS):
            pltpu.make_async_copy(ybuf.at[0], y_hbm.at[pl.ds(pad_base + s * tile_rows, tile_rows)],
                                  sem_out.at[s]).start()
        for k in range(W_SLOTS - 1):
            for cp in weight_copies(k):
                cp.start()
        gather_block(tok0_ref, 0)
        gather_block(tok1_ref, 1)
        for s in range(Y_SLOTS):
            scatter_wait(s)

    @pl.when(i < n_used)
    def _():
        x_slot = lax.rem(i, X_SLOTS)
        y_slot = lax.rem(i, Y_SLOTS)

        @pl.when((i == 0) | (run != run_ref[jnp.maximum(i - 1, 0)]))
        def _():
            for cp in weight_copies(run):
                cp.wait()
            w_slot = lax.rem(run, W_SLOTS)
            wg_b[...] = wg_f[w_slot].astype(BF16)
            wu_b[...] = wu_f[w_slot].astype(BF16)
            wd_b[...] = wd_f[w_slot].astype(BF16)
            for cp in weight_copies(run + W_SLOTS - 1):
                cp.start()

        gather_wait(x_slot)

        @pl.when(i >= Y_SLOTS)
        def _():
            scatter_wait(y_slot)

        x = _tiles_to_rows(xbuf.at[x_slot], MOE_BLOCK).astype(BF16)
        hg = _dot(x, wg_b[...])
        hu = _dot(x, wu_b[...])
        act = (hg * _sigmoid(hg) * hu).astype(BF16)
        y = _dot(act, wd_b[...]) * roww_ref[...]
        _rows_to_tiles(ybuf.at[y_slot], y)
        for r in range(MOE_BLOCK):
            pltpu.make_async_copy(ybuf.at[y_slot, pl.ds(r * ROW_TILES, ROW_TILES)],
                                  y_hbm.at[pl.ds(dst_ref[0, 0, r], ROW_TILES)],
                                  sem_out.at[y_slot]).start(priority=ROW_DMA_PRIORITY)
        gather_block(tok2_ref, lax.rem(i + 2, X_SLOTS))

        @pl.when(i == n_used - 1)
        def _():
            scatter_wait(y_slot)

            @pl.when(i >= 1)
            def _():
                scatter_wait(1 - y_slot)

            gather_wait(lax.rem(i + 1, X_SLOTS))
            gather_wait(lax.rem(i + 2, X_SLOTS))
            for k in range(W_SLOTS - 1):
                for cp in weight_copies(n_runs + k):
                    cp.wait()


def moe_experts(u_all, row_src, row_dst, row_w, block_run, run_exp, n_used_runs, w_gate, w_up,
                w_down, n_assign):
    d, ff = w_gate.shape[1], w_gate.shape[2]
    n_blocks = row_src.shape[0]
    tile_rows = MOE_BLOCK * ROW_TILES
    smem_blk = lambda off: pl.BlockSpec(
        (1, 1, MOE_BLOCK), lambda i, *_: (jnp.minimum(i + off, n_blocks - 1), 0, 0),
        memory_space=pltpu.SMEM)
    hbm = pl.BlockSpec(memory_space=pl.ANY)
    grid_spec = pltpu.PrefetchScalarGridSpec(
        num_scalar_prefetch=3,
        grid=(n_blocks,),
        in_specs=[
            smem_blk(0), smem_blk(1), smem_blk(2), smem_blk(0),
            pl.BlockSpec((MOE_BLOCK, 1), lambda i, *_: (i, 0)),
            hbm, hbm, hbm, hbm,
        ],
        out_specs=hbm,
        scratch_shapes=[
            pltpu.VMEM((X_SLOTS, tile_rows, LANES), F32),
            pltpu.VMEM((Y_SLOTS, tile_rows, LANES), F32),
            pltpu.VMEM((W_SLOTS, d, ff), F32), pltpu.VMEM((W_SLOTS, d, ff), F32),
            pltpu.VMEM((W_SLOTS, ff, d), F32),
            pltpu.VMEM((d, ff), BF16), pltpu.VMEM((d, ff), BF16), pltpu.VMEM((ff, d), BF16),
            pltpu.SemaphoreType.DMA((X_SLOTS,)), pltpu.SemaphoreType.DMA((Y_SLOTS,)),
            pltpu.SemaphoreType.DMA((W_SLOTS,)),
        ],
    )
    y_rows = (n_assign + Y_SLOTS * MOE_BLOCK) * ROW_TILES
    return pl.pallas_call(
        _moe_expert_kernel,
        grid_spec=grid_spec,
        out_shape=jax.ShapeDtypeStruct((y_rows, LANES), F32),
        compiler_params=_cp(("arbitrary",)),
        name="moe_experts",
    )(block_run, run_exp, n_used_runs, row_src, row_src, row_src, row_dst, row_w, u_all,
      w_gate, w_up, w_down)


def _moe_combine_kernel(h_ref, y0_ref, y1_ref, o_ref):
    rows = h_ref.shape[0]
    o_ref[...] = h_ref[...] + (_tiles_to_rows(y0_ref, rows) + _tiles_to_rows(y1_ref, rows))


def moe_combine(h, y_slots, row_off, slot_stride, *, tm):
    m, d = h.shape
    assert row_off % tm == 0 and slot_stride % tm == 0
    off0, off1 = row_off // tm, (row_off + slot_stride) // tm
    return pl.pallas_call(
        _moe_combine_kernel,
        grid=(m // tm,),
        in_specs=[pl.BlockSpec((tm, d), lambda i: (i, 0)),
                  pl.BlockSpec((tm * ROW_TILES, LANES), lambda i: (i + off0, 0)),
                  pl.BlockSpec((tm * ROW_TILES, LANES), lambda i: (i + off1, 0))],
        out_specs=pl.BlockSpec((tm, d), lambda i: (i, 0)),
        out_shape=jax.ShapeDtypeStruct((m, d), F32),
        compiler_params=_cp(("parallel",)),
        name="moe_combine",
    )(h, y_slots, y_slots)


def moe_dispatch(e_idx, gates, slot_stride):
    m = e_idx.shape[0]
    a = m * TOP_K
    e_flat = e_idx.reshape(a)
    order = jnp.argsort(e_flat, stable=True).astype(jnp.int32)
    counts = jnp.sum(e_flat[:, None] == jnp.arange(N_EXPERTS, dtype=jnp.int32)[None, :],
                     axis=0, dtype=jnp.int32)
    pad_counts = (counts + MOE_BLOCK - 1) // MOE_BLOCK * MOE_BLOCK
    starts = jnp.cumsum(counts) - counts
    pad_ends = jnp.cumsum(pad_counts)
    pad_starts = pad_ends - pad_counts
    n_blocks = a // MOE_BLOCK + N_EXPERTS
    p = n_blocks * MOE_BLOCK
    n_used = (pad_ends[-1] // MOE_BLOCK).astype(jnp.int32)
    blk = jnp.arange(n_blocks, dtype=jnp.int32)
    blk_start = jnp.minimum(blk, n_used - 1) * MOE_BLOCK
    block_exp = jnp.minimum(jnp.searchsorted(pad_ends, blk_start, side='right'),
                            N_EXPERTS - 1).astype(jnp.int32)
    in_exp = blk * MOE_BLOCK - pad_starts[block_exp]
    row_cnt = jnp.where(blk < n_used, jnp.clip(counts[block_exp] - in_exp, 0, MOE_BLOCK), 0)
    lane = jnp.arange(MOE_BLOCK, dtype=jnp.int32)[None, :]
    valid = lane < row_cnt[:, None]
    src = jnp.clip((starts[block_exp] + in_exp)[:, None] + lane, 0, a - 1)
    assign = order[src]
    row_tok = jnp.where(valid, assign // TOP_K, 0)
    pad_dst = TOP_K * slot_stride + (blk % Y_SLOTS)[:, None] * MOE_BLOCK + lane
    row_dst = jnp.where(valid, (assign % TOP_K) * slot_stride + assign // TOP_K, pad_dst)
    row_w = jnp.where(valid, gates.reshape(a)[assign], 0.0)
    as_blocks = lambda x: (x * ROW_TILES).astype(jnp.int32).reshape(n_blocks, 1, MOE_BLOCK)
    has_rows = counts > 0
    run_exp = jnp.argsort(~has_rows, stable=True).astype(jnp.int32)
    block_run = (jnp.cumsum(has_rows) - 1)[block_exp].astype(jnp.int32)
    n_used_runs = jnp.stack([n_used, jnp.sum(has_rows, dtype=jnp.int32)])
    return (as_blocks(row_tok), as_blocks(row_dst), row_w.reshape(p, 1), block_run, run_exp,
            n_used_runs)


def rwkv_params(rw_mu, rw_w0, rw_w2, rw_a0, rw_a2, rw_g2, rw_k_k, rw_k_a, rw_r_k):
    c = RWKV_WIDTH
    mu = jnp.pad(rw_mu, (0, RWKV_PROJ_PAD - RWKV_PROJ)).reshape(1, RWKV_PROJ_PAD)
    w_lora = jnp.zeros((LANES, 2 * c), F32)
    w_lora = w_lora.at[0:DECAY_LORA, 0:c].set(rw_w2).at[DECAY_LORA:LANES, c:2 * c].set(rw_a2)
    w_gate = jnp.pad(rw_g2, ((0, GATE_PAD - GATE_LORA), (0, 0)))
    vec = lambda x: x.reshape(1, c)
    return (mu, vec(rw_w0), vec(rw_a0), vec(rw_k_k), vec(rw_k_a), vec(rw_r_k),
            w_lora.astype(BF16), w_gate.astype(BF16))


def _token_tiles(m):
    return (1024, 512) if m % 1024 == 0 else (m, m)


def _dense_front(x2d, wts, tm):
    pa = norm_matmul(x2d, wts['ln1_w'], wts['w_att'], tm=tm, tn=512)
    pr = norm_matmul(x2d, wts['ln1_w'], wts['w_rw'], tm=tm, tn=RWKV_PROJ_PAD // 3)
    return pa, pr


def _dense_back(x2d, att2d, rw2d, wts, xattn_fn, tm):
    h1 = matmul_residual([att2d, rw2d], [wts['w_out_a'], wts['w_out_r']], x2d, tm=tm, tn=512)
    qx = norm_matmul(h1, wts['ln2_w'], wts['xq_w'], tm=tm, tn=XATT_WIDTH)
    ox = xattn_fn(qx)
    return matmul_residual([ox], [wts['xo_w']], h1, tm=tm, tn=512)


def kernel(x_prompt, x_sample, cache_win_k, cache_win_v, state_wkv, state_shift, cache_mem_k, cache_mem_v, mem_prompt, ln1_w, w_in, q_norm_w, k_norm_w, attn_sinks, rw_mu, rw_w0, rw_w2, rw_a0, rw_a2, rw_g2, rw_k_k, rw_k_a, rw_r_k, rw_ln_w, rw_ln_b, w_out, ln2_w, mem_norm_w, xq_w, xkv_w, xq_norm_w, xk_norm_w, xo_w, ln3_w, router_group_w, router_group_b, router_expert_w, router_expert_b, exp_w_gate, exp_w_up, exp_w_down):
    assert w_in.shape[0] == 1, "single-layer stack"
    bp, seq, d = x_prompt.shape
    bs = x_sample.shape[0]
    mp = bp * seq
    c = RWKV_WIDTH

    router_w = jnp.concatenate(
        [router_group_w[0], router_expert_w[0],
         jnp.zeros((d, ROUTER_LANES - N_EXPERT_GROUPS - N_EXPERTS), F32)], axis=1)
    router_hi = router_w.astype(BF16)
    wts = {
        'ln1_w': ln1_w[0], 'ln2_w': ln2_w[0], 'ln3_w': ln3_w[0],
        'w_att': w_in[0][:, :ATT_PROJ].astype(BF16),
        'w_rw': jnp.pad(w_in[0][:, ATT_PROJ:],
                        ((0, 0), (0, RWKV_PROJ_PAD - RWKV_PROJ))).astype(BF16),
        'w_out_a': w_out[0][:ATT_WIDTH].astype(BF16),
        'w_out_r': w_out[0][ATT_WIDTH:].astype(BF16),
        'xq_w': xq_w[0].astype(BF16), 'xo_w': xo_w[0].astype(BF16),
        'router_hi': router_hi,
        'router_lo': (router_w - router_hi.astype(F32)).astype(BF16),
        'router_b': jnp.pad(jnp.concatenate([router_group_b[0], router_expert_b[0]]),
                            (0, ROUTER_LANES - N_EXPERT_GROUPS - N_EXPERTS)).reshape(1, -1),
    }
    rw_par = rwkv_params(rw_mu[0], rw_w0[0], rw_w2[0], rw_a0[0], rw_a2[0], rw_g2[0],
                         rw_k_k[0], rw_k_a[0], rw_r_k[0])

    tm_p, te_p = _token_tiles(mp)
    xp = x_prompt.reshape(mp, d)
    pa, pr = _dense_front(xp, wts, tm_p)
    pa3 = pa.reshape(bp, seq, ATT_PROJ)
    pr3 = pr.reshape(bp, seq, RWKV_PROJ_PAD)
    tabs_p = rope_tables(jnp.arange(seq, dtype=jnp.int32))
    att_p, kn_p = swa_prompt(pa3, tabs_p, q_norm_w[0], k_norm_w[0], attn_sinks[0])
    prep = rwkv_prep_seq(pr3, jnp.zeros((bp, 1, RWKV_PROJ_PAD), F32), rw_par)
    r, ld, kp, v, kk, b, bonus, g = prep
    y_p, wkv_p = rwkv_scan(r, ld, kp, v, kk, b)
    rw_p = rwkv_post(y_p.reshape(mp, c), bonus.reshape(mp, c), g.reshape(mp, c),
                     rw_ln_w[0], rw_ln_b[0], tm=te_p)

    n_mem = mem_prompt.shape[1]
    kv_mem = norm_matmul(mem_prompt.reshape(bp * n_mem, d), mem_norm_w[0],
                         xkv_w[0].astype(BF16), tm=bp * n_mem, tn=512)
    mem_k = head_rms(kv_mem[:, :XATT_WIDTH], xk_norm_w[0])
    mem_v = kv_mem[:, XATT_WIDTH:]
    mem_k3 = mem_k.reshape(bp, n_mem, XATT_WIDTH)
    mem_v3 = mem_v.reshape(bp, n_mem, XATT_WIDTH)

    def xattn_p(qx):
        return xattn_prompt(qx.reshape(bp, seq, XATT_WIDTH), mem_k3, mem_v3,
                            xq_norm_w[0]).reshape(mp, XATT_WIDTH)

    h2_p = _dense_back(xp, att_p.reshape(mp, ATT_WIDTH), rw_p, wts, xattn_p, tm_p)

    tm_s, te_s = _token_tiles(bs)
    xs = x_sample.reshape(bs, d)
    sa, sr = _dense_front(xs, wts, tm_s)
    tabs_s = rope_tables(PAST_LEN + jnp.arange(1, dtype=jnp.int32))
    qk_w = jnp.concatenate([jnp.tile(q_norm_w[0], ATT_HEADS),
                            jnp.tile(k_norm_w[0], ATT_KV_HEADS)]).reshape(1, -1)
    qk = qk_norm_rope(sa[:, :ATT_WIDTH + KV_WIDTH], qk_w, tabs_s)
    nbuf = cache_win_k.shape[2]
    att_s, win_k, win_v = swa_decode(
        qk[:, :ATT_WIDTH].reshape(bs, ATT_HEADS, HEAD_DIM),
        qk[:, ATT_WIDTH:].reshape(bs, 1, KV_WIDTH),
        sa[:, ATT_WIDTH + KV_WIDTH:].reshape(bs, 1, KV_WIDTH),
        cache_win_k[0].reshape(bs, nbuf, KV_WIDTH), cache_win_v[0].reshape(bs, nbuf, KV_WIDTH),
        attn_sinks[0])
    shift_prev = jnp.pad(state_shift[0], ((0, 0), (0, RWKV_PROJ_PAD - RWKV_PROJ)))
    r, ld, kp, v, kk, b, bonus, g = rwkv_prep_tok(sr, shift_prev, rw_par)
    gh = bs * RWKV_HEADS
    rows = [t.reshape(gh, 1, RWKV_HEAD_DIM) for t in (r, ld, kp, v, kk, b)]
    y_s, wkv_s = rwkv_step(*rows, state_wkv[0].reshape(gh, RWKV_HEAD_DIM, RWKV_HEAD_DIM))
    rw_s = rwkv_post(y_s.reshape(bs, c), bonus, g, rw_ln_w[0], rw_ln_b[0], tm=te_s)
    def xattn_s(qx):
        q_pad = jnp.pad(qx.reshape(bs, XATT_HEADS, XATT_HEAD_DIM), ((0, 0), (0, 4), (0, 0)))
        o = xattn_decode(q_pad, cache_mem_k, cache_mem_v, xq_norm_w[0])
        return o[:, :XATT_HEADS].reshape(bs, XATT_WIDTH)

    h2_s = _dense_back(xs, att_s.reshape(bs, ATT_WIDTH), rw_s, wts, xattn_s, tm_s)

    m_all = mp + bs
    slot_stride = m_all
    tc = math.gcd(mp, bs, 512)
    u_all, idx_all, gate_all = moe_router(h2_p, h2_s, wts['ln3_w'], wts['router_hi'],
                                          wts['router_lo'], wts['router_b'], tm=tc)
    row_src, row_dst, row_w, block_run, run_exp, n_used_runs = moe_dispatch(
        idx_all[:, :TOP_K], gate_all[:, :TOP_K], slot_stride)
    y_slots = moe_experts(u_all, row_src, row_dst, row_w, block_run, run_exp, n_used_runs,
                          exp_w_gate[0], exp_w_up[0], exp_w_down[0], TOP_K * slot_stride)
    out_p = moe_combine(h2_p, y_slots, 0, slot_stride, tm=tc)
    out_s = moe_combine(h2_s, y_slots, mp, slot_stride, tm=tc)

    win = min(WINDOW, seq)
    kv_shape = (1, bp, win, ATT_KV_HEADS, HEAD_DIM)
    return (
        out_p.reshape(bp, seq, d),
        out_s.reshape(bs, 1, d),
        kn_p[:, seq - win:].reshape(kv_shape),
        pa3[:, seq - win:, ATT_WIDTH + KV_WIDTH:].reshape(kv_shape),
        wkv_p[None],
        pr3[:, seq - 1, :RWKV_PROJ][None],
        mem_k3.reshape(1, bp, n_mem, XATT_HEADS, XATT_HEAD_DIM),
        mem_v3.reshape(1, bp, n_mem, XATT_HEADS, XATT_HEAD_DIM),
        win_k.reshape(1, bs, nbuf, ATT_KV_HEADS, HEAD_DIM),
        win_v.reshape(1, bs, nbuf, ATT_KV_HEADS, HEAD_DIM),
        wkv_s.reshape(1, bs, RWKV_HEADS, RWKV_HEAD_DIM, RWKV_HEAD_DIM),
        sr[:, :RWKV_PROJ].reshape(1, bs, RWKV_PROJ),
    )
```

```python
import functools
import math

import jax
import jax.numpy as jnp
from jax import lax
from jax.experimental import pallas as pl
from jax.experimental.pallas import tpu as pltpu

F32 = jnp.float32
BF16 = jnp.bfloat16

D_MODEL = 2048
HEAD_DIM = 64
ATT_HEADS = 16
ATT_KV_HEADS = 4
ATT_GROUP = ATT_HEADS // ATT_KV_HEADS
ATT_WIDTH = ATT_HEADS * HEAD_DIM
KV_WIDTH = ATT_KV_HEADS * HEAD_DIM
ATT_PROJ = ATT_WIDTH + 2 * KV_WIDTH
WINDOW = 128
ATT_SCALE = HEAD_DIM ** -0.5
ROPE_THETA = 500000.0
ROT_DIM = HEAD_DIM // 4
PAST_LEN = 16384

RWKV_WIDTH = 1024
RWKV_HEAD_DIM = 64
RWKV_HEADS = 16
DECAY_LORA = 64
AAA_LORA = 64
GATE_LORA = 160
RWKV_PROJ = 3 * RWKV_WIDTH + DECAY_LORA + AAA_LORA + GATE_LORA
RWKV_PROJ_PAD = 3456

N_MEM = 256
XATT_HEADS = 4
XATT_HEAD_DIM = 128
XATT_WIDTH = XATT_HEADS * XATT_HEAD_DIM

N_EXPERT_GROUPS = 8
EXPERTS_PER_GROUP = 8
N_EXPERTS = 64
TOP_K = 2
EXPERT_FF = D_MODEL // 4
MOE_BLOCK = 128

RMS_EPS = 1e-6
GN_EPS = 64e-5

LANES = 128
CHUNK = 64
VMEM_LIMIT = 56 * 1024 * 1024


def _cp(sem, vmem=VMEM_LIMIT):
    return pltpu.CompilerParams(dimension_semantics=sem, vmem_limit_bytes=vmem)


def _rms_rows(x, w):
    ms = jnp.mean(x * x, axis=-1, keepdims=True)
    return x * lax.rsqrt(ms + RMS_EPS) * w


def _split2(x):
    hi = x.astype(BF16)
    lo = (x - hi.astype(F32)).astype(BF16)
    return hi, lo


def _split3(x):
    h1 = x.astype(BF16)
    r1 = x - h1.astype(F32)
    h2 = r1.astype(BF16)
    h3 = (r1 - h2.astype(F32)).astype(BF16)
    return h1, h2, h3


def _dot(a, b):
    return jnp.dot(a, b, preferred_element_type=F32)


def _dot_nt(a, b):
    return lax.dot_general(a, b, (((1,), (1,)), ((), ())), preferred_element_type=F32)


def _group_sum(x, gmat):
    hi, lo = _split2(x)
    return _dot(hi, gmat) + _dot(lo, gmat)


def _head_indicator():
    r = lax.broadcasted_iota(jnp.int32, (LANES, LANES), 0) // HEAD_DIM
    c = lax.broadcasted_iota(jnp.int32, (LANES, LANES), 1) // HEAD_DIM
    return jnp.where(r == c, 1.0, 0.0).astype(BF16)


def _norm_mm_kernel(x_ref, lnw_ref, w_ref, o_ref, xn_ref):
    @pl.when(pl.program_id(1) == 0)
    def _():
        xn_ref[...] = _rms_rows(x_ref[...], lnw_ref[...]).astype(BF16)

    o_ref[...] = _dot(xn_ref[...], w_ref[...])


def norm_matmul(x, ln_w, w_bf16, *, tm, tn):
    m, k = x.shape
    n = w_bf16.shape[1]
    assert m % tm == 0 and n % tn == 0
    return pl.pallas_call(
        _norm_mm_kernel,
        grid=(m // tm, n // tn),
        in_specs=[
            pl.BlockSpec((tm, k), lambda i, j: (i, 0)),
            pl.BlockSpec((1, k), lambda i, j: (0, 0)),
            pl.BlockSpec((k, tn), lambda i, j: (0, j)),
        ],
        out_specs=pl.BlockSpec((tm, tn), lambda i, j: (i, j)),
        out_shape=jax.ShapeDtypeStruct((m, n), F32),
        scratch_shapes=[pltpu.VMEM((tm, k), BF16)],
        compiler_params=_cp(("parallel", "arbitrary")),
        name="norm_matmul",
    )(x, ln_w.reshape(1, k), w_bf16)


def _mm_res_kernel(*refs, n_lhs):
    a_refs = refs[:n_lhs]
    w_refs = refs[n_lhs:2 * n_lhs]
    res_ref = refs[2 * n_lhs]
    o_ref = refs[2 * n_lhs + 1]
    acc = res_ref[...]
    for a_ref, w_ref in zip(a_refs, w_refs):
        acc = acc + _dot(a_ref[...].astype(BF16), w_ref[...])
    o_ref[...] = acc


def matmul_residual(lhs_list, w_list, res, *, tm, tn):
    m, n = res.shape
    n_lhs = len(lhs_list)
    assert m % tm == 0 and n % tn == 0
    in_specs = [pl.BlockSpec((tm, a.shape[1]), lambda i, j: (i, 0)) for a in lhs_list]
    in_specs += [pl.BlockSpec((w.shape[0], tn), lambda i, j: (0, j)) for w in w_list]
    in_specs += [pl.BlockSpec((tm, tn), lambda i, j: (i, j))]
    return pl.pallas_call(
        functools.partial(_mm_res_kernel, n_lhs=n_lhs),
        grid=(m // tm, n // tn),
        in_specs=in_specs,
        out_specs=pl.BlockSpec((tm, tn), lambda i, j: (i, j)),
        out_shape=jax.ShapeDtypeStruct((m, n), F32),
        compiler_params=_cp(("parallel", "arbitrary")),
        name="matmul_residual",
    )(*lhs_list, *w_list, res)


def rope_tables(pos):
    half = ROT_DIM // 2
    inv = ROPE_THETA ** (-jnp.arange(half, dtype=F32) * 2.0 / ROT_DIM)
    ang = pos.astype(F32)[:, None] * inv[None, :]
    cos, sin = jnp.cos(ang), jnp.sin(ang)
    t = pos.shape[0]
    ones = jnp.ones((t, HEAD_DIM - ROT_DIM), F32)
    zeros = jnp.zeros((t, HEAD_DIM - ROT_DIM), F32)
    z8 = jnp.zeros((t, half), F32)
    cos_t = jnp.concatenate([cos, cos, ones], axis=1)
    sin_a = jnp.concatenate([z8, sin, zeros], axis=1)
    sin_b = jnp.concatenate([-sin, z8, zeros], axis=1)
    return tuple(jnp.concatenate([a, a], axis=1) for a in (cos_t, sin_a, sin_b))


def _norm_rope_chunk(x, w, cos_t, sin_a, sin_b, gmat):
    ms = _group_sum(x * x, gmat) * (1.0 / HEAD_DIM)
    xn = x * lax.rsqrt(ms + RMS_EPS) * w
    half = ROT_DIM // 2
    return (xn * cos_t + pltpu.roll(xn, half, axis=1) * sin_a
            + pltpu.roll(xn, LANES - half, axis=1) * sin_b)


def _norm_rope(x, w, tabs, gmat):
    chunks = [
        _norm_rope_chunk(x[:, c * LANES:(c + 1) * LANES], w, *tabs, gmat)
        for c in range(x.shape[1] // LANES)
    ]
    return chunks[0] if len(chunks) == 1 else jnp.concatenate(chunks, axis=1)


def _sink_softmax(s, sink):
    m = jnp.maximum(jnp.max(s, axis=-1, keepdims=True), sink)
    e = jnp.exp(s - m)
    return e / (jnp.sum(e, axis=-1, keepdims=True) + jnp.exp(sink - m))


def _swa_prompt_kernel(q_ref, kc_ref, vc_ref, kp_ref, vp_ref, cc_ref, sac_ref, sbc_ref,
                       cp_ref, sap_ref, sbp_ref, qw_ref, kw_ref, sink_ref, o_ref, kn_ref):
    n = pl.program_id(1)
    blk = q_ref.shape[0]
    gmat = _head_indicator()
    tabs_c = (cc_ref[...], sac_ref[...], sbc_ref[...])
    tabs_p = (cp_ref[...], sap_ref[...], sbp_ref[...])
    q = _norm_rope(q_ref[...], qw_ref[...], tabs_c, gmat)
    k_cur = _norm_rope(kc_ref[...], kw_ref[...], tabs_c, gmat)
    k_prev = _norm_rope(kp_ref[...], kw_ref[...], tabs_p, gmat)
    kn_ref[...] = k_cur
    k_all = jnp.concatenate([k_prev, k_cur], axis=0).astype(BF16)
    v_all = jnp.concatenate([vp_ref[...], vc_ref[...]], axis=0).astype(BF16)

    qi = lax.broadcasted_iota(jnp.int32, (blk, 2 * blk), 0) + blk
    si = lax.broadcasted_iota(jnp.int32, (blk, 2 * blk), 1)
    rel = qi - si
    valid = (rel >= 0) & (rel <= WINDOW) & ((n > 0) | (si >= blk))

    for kv in range(ATT_KV_HEADS):
        k_h = k_all[:, kv * HEAD_DIM:(kv + 1) * HEAD_DIM]
        v_h = v_all[:, kv * HEAD_DIM:(kv + 1) * HEAD_DIM]
        heads = [kv * ATT_GROUP + g for g in range(ATT_GROUP)]
        q_g = jnp.concatenate(
            [q[:, h * HEAD_DIM:(h + 1) * HEAD_DIM] for h in heads], axis=0).astype(BF16)
        s = _dot_nt(q_g, k_h) * ATT_SCALE
        probs = []
        for g, h in enumerate(heads):
            s_h = jnp.where(valid, s[g * blk:(g + 1) * blk], -jnp.inf)
            probs.append(_sink_softmax(s_h, sink_ref[h]))
        p = jnp.concatenate(probs, axis=0).astype(BF16)
        o = _dot(p, v_h)
        for g, h in enumerate(heads):
            o_ref[:, h * HEAD_DIM:(h + 1) * HEAD_DIM] = o[g * blk:(g + 1) * blk]


def swa_prompt(pa, tabs, q_norm_w, k_norm_w, sinks):
    b, t, _ = pa.shape
    blk = WINDOW
    nb = t // blk
    qb, kb, vb = 0, ATT_WIDTH // KV_WIDTH, ATT_WIDTH // KV_WIDTH + 1
    cur = lambda i, n, *_: (i, n, 0)
    tab_cur = pl.BlockSpec((blk, LANES), lambda i, n: (n, 0))
    tab_prev = pl.BlockSpec((blk, LANES), lambda i, n: (jnp.maximum(n - 1, 0), 0))
    qw = jnp.tile(q_norm_w.reshape(1, HEAD_DIM), (1, 2))
    kw = jnp.tile(k_norm_w.reshape(1, HEAD_DIM), (1, 2))
    return pl.pallas_call(
        _swa_prompt_kernel,
        grid=(b, nb),
        in_specs=[
            pl.BlockSpec((None, blk, ATT_WIDTH), lambda i, n: (i, n, qb)),
            pl.BlockSpec((None, blk, KV_WIDTH), lambda i, n: (i, n, kb)),
            pl.BlockSpec((None, blk, KV_WIDTH), lambda i, n: (i, n, vb)),
            pl.BlockSpec((None, blk, KV_WIDTH), lambda i, n: (i, jnp.maximum(n - 1, 0), kb)),
            pl.BlockSpec((None, blk, KV_WIDTH), lambda i, n: (i, jnp.maximum(n - 1, 0), vb)),
            tab_cur, tab_cur, tab_cur, tab_prev, tab_prev, tab_prev,
            pl.BlockSpec((1, LANES), lambda i, n: (0, 0)),
            pl.BlockSpec((1, LANES), lambda i, n: (0, 0)),
            pl.BlockSpec(memory_space=pltpu.SMEM),
        ],
        out_specs=[
            pl.BlockSpec((None, blk, ATT_WIDTH), cur),
            pl.BlockSpec((None, blk, KV_WIDTH), cur),
        ],
        out_shape=[
            jax.ShapeDtypeStruct((b, t, ATT_WIDTH), F32),
            jax.ShapeDtypeStruct((b, t, KV_WIDTH), F32),
        ],
        compiler_params=_cp(("parallel", "arbitrary")),
        name="swa_prompt",
    )(pa, pa, pa, pa, pa, *tabs, *tabs, qw, kw, sinks)


def _qk_norm_rope_kernel(x_ref, w_ref, c_ref, sa_ref, sb_ref, o_ref):
    gmat = _head_indicator()
    tabs = (c_ref[...], sa_ref[...], sb_ref[...])
    for c in range(x_ref.shape[1] // LANES):
        sl = slice(c * LANES, (c + 1) * LANES)
        o_ref[:, sl] = _norm_rope_chunk(x_ref[:, sl], w_ref[:, sl], *tabs, gmat)


def qk_norm_rope(x, w_row, tabs):
    m, w = x.shape
    full = lambda *shape: pl.BlockSpec(shape, lambda: (0,) * len(shape))
    return pl.pallas_call(
        _qk_norm_rope_kernel,
        in_specs=[full(m, w), full(1, w), full(1, LANES), full(1, LANES), full(1, LANES)],
        out_specs=full(m, w),
        out_shape=jax.ShapeDtypeStruct((m, w), F32),
        name="qk_norm_rope",
    )(x, w_row, *tabs)


def _swa_decode_kernel(q_ref, kn_ref, vn_ref, ck_ref, cv_ref, sink_ref, o_ref, kw_ref, vw_ref):
    bb = q_ref.shape[0]
    nbuf = ck_ref.shape[1]
    row_kv = lax.broadcasted_iota(jnp.int32, (ATT_HEADS, KV_WIDTH), 0) // ATT_GROUP
    lane_kv = lax.broadcasted_iota(jnp.int32, (ATT_HEADS, KV_WIDTH), 1) // HEAD_DIM
    own = row_kv == lane_kv
    sink = sink_ref[...]
    for b in range(bb):
        q2 = q_ref[b]
        q_exp = jnp.where(own, jnp.concatenate([q2] * ATT_KV_HEADS, axis=1), 0.0)
        k_new, v_new = kn_ref[b], vn_ref[b]
        k_buf, v_buf = ck_ref[b], cv_ref[b]
        s_buf = _dot_nt(q_exp.astype(BF16), k_buf.astype(BF16)) * ATT_SCALE
        s_new = jnp.sum(q_exp * k_new, axis=-1, keepdims=True) * ATT_SCALE
        m = jnp.maximum(jnp.maximum(jnp.max(s_buf, axis=-1, keepdims=True), s_new), sink)
        e_buf = jnp.exp(s_buf - m)
        e_new = jnp.exp(s_new - m)
        inv = 1.0 / (jnp.sum(e_buf, axis=-1, keepdims=True) + e_new + jnp.exp(sink - m))
        o = _dot((e_buf * inv).astype(BF16), v_buf.astype(BF16)) + (e_new * inv) * v_new
        o = jnp.where(own, o, 0.0)
        o_ref[b] = (o[:, 0:HEAD_DIM] + o[:, HEAD_DIM:2 * HEAD_DIM]
                    + o[:, 2 * HEAD_DIM:3 * HEAD_DIM] + o[:, 3 * HEAD_DIM:4 * HEAD_DIM])
        kw_ref[b, 0:nbuf - 1, :] = k_buf[1:nbuf]
        kw_ref[b, nbuf - 1:nbuf, :] = k_new
        vw_ref[b, 0:nbuf - 1, :] = v_buf[1:nbuf]
        vw_ref[b, nbuf - 1:nbuf, :] = v_new


def swa_decode(q, k_new, v_new, cache_k, cache_v, sinks, *, bb=8):
    b, nbuf, _ = cache_k.shape
    blk3 = lambda s1, s2: pl.BlockSpec((bb, s1, s2), lambda i: (i, 0, 0))
    return pl.pallas_call(
        _swa_decode_kernel,
        grid=(b // bb,),
        in_specs=[
            blk3(ATT_HEADS, HEAD_DIM), blk3(1, KV_WIDTH), blk3(1, KV_WIDTH),
            blk3(nbuf, KV_WIDTH), blk3(nbuf, KV_WIDTH),
            pl.BlockSpec((ATT_HEADS, 1), lambda i: (0, 0)),
        ],
        out_specs=[blk3(ATT_HEADS, HEAD_DIM), blk3(nbuf, KV_WIDTH), blk3(nbuf, KV_WIDTH)],
        out_shape=[
            jax.ShapeDtypeStruct((b, ATT_HEADS, HEAD_DIM), F32),
            jax.ShapeDtypeStruct((b, nbuf, KV_WIDTH), F32),
            jax.ShapeDtypeStruct((b, nbuf, KV_WIDTH), F32),
        ],
        compiler_params=_cp(("parallel",)),
        name="swa_decode",
    )(q, k_new, v_new, cache_k, cache_v, sinks.reshape(ATT_HEADS, 1))


def _head_rms_kernel(x_ref, w_ref, o_ref):
    for h in range(x_ref.shape[1] // XATT_HEAD_DIM):
        sl = slice(h * XATT_HEAD_DIM, (h + 1) * XATT_HEAD_DIM)
        o_ref[:, sl] = _rms_rows(x_ref[:, sl], w_ref[...])


def head_rms(x, w):
    m, wd = x.shape
    return pl.pallas_call(
        _head_rms_kernel,
        in_specs=[pl.BlockSpec((m, wd), lambda: (0, 0)),
                  pl.BlockSpec((1, XATT_HEAD_DIM), lambda: (0, 0))],
        out_specs=pl.BlockSpec((m, wd), lambda: (0, 0)),
        out_shape=jax.ShapeDtypeStruct((m, wd), F32),
        name="head_rms",
    )(x, w.reshape(1, XATT_HEAD_DIM))


def _xattn_prompt_kernel(q_ref, k_ref, v_ref, w_ref, o_ref):
    scale = 1.0 / math.sqrt(XATT_HEAD_DIM)
    for h in range(XATT_HEADS):
        sl = slice(h * XATT_HEAD_DIM, (h + 1) * XATT_HEAD_DIM)
        qn = _rms_rows(q_ref[:, sl], w_ref[...]).astype(BF16)
        s = _dot_nt(qn, k_ref[:, sl].astype(BF16)) * scale
        e = jnp.exp(s - jnp.max(s, axis=-1, keepdims=True))
        p = e / jnp.sum(e, axis=-1, keepdims=True)
        o_ref[:, sl] = _dot(p.astype(BF16), v_ref[:, sl].astype(BF16))


def xattn_prompt(q, mem_k, mem_v, xq_norm_w, *, tq=512):
    b, t, w = q.shape
    n_mem = mem_k.shape[1]
    return pl.pallas_call(
        _xattn_prompt_kernel,
        grid=(b, t // tq),
        in_specs=[
            pl.BlockSpec((None, tq, w), lambda i, j: (i, j, 0)),
            pl.BlockSpec((None, n_mem, w), lambda i, j: (i, 0, 0)),
            pl.BlockSpec((None, n_mem, w), lambda i, j: (i, 0, 0)),
            pl.BlockSpec((1, XATT_HEAD_DIM), lambda i, j: (0, 0)),
        ],
        out_specs=pl.BlockSpec((None, tq, w), lambda i, j: (i, j, 0)),
        out_shape=jax.ShapeDtypeStruct((b, t, w), F32),
        compiler_params=_cp(("parallel", "arbitrary")),
        name="xattn_prompt",
    )(q, mem_k, mem_v, xq_norm_w.reshape(1, XATT_HEAD_DIM))


def _xattn_decode_kernel(q_ref, k_ref, v_ref, w_ref, o_ref):
    bb, rows, _ = q_ref.shape
    scale = 1.0 / math.sqrt(XATT_HEAD_DIM)
    row = lax.broadcasted_iota(jnp.int32, (rows, 1), 0)
    for b in range(bb):
        qn = _rms_rows(q_ref[b], w_ref[...]).astype(BF16)
        s = sum(jnp.where(row == h, _dot_nt(qn, k_ref[b, :, h, :].astype(BF16)), 0.0)
                for h in range(XATT_HEADS)) * scale
        e = jnp.exp(s - jnp.max(s, axis=-1, keepdims=True))
        p = (e / jnp.sum(e, axis=-1, keepdims=True)).astype(BF16)
        o_ref[b] = sum(jnp.where(row == h, _dot(p, v_ref[b, :, h, :].astype(BF16)), 0.0)
                       for h in range(XATT_HEADS))


def xattn_decode(q_pad, mem_k, mem_v, xq_norm_w, *, bb=8):
    b, rows, _ = q_pad.shape
    n_mem = mem_k.shape[2]
    kv = pl.BlockSpec((None, bb, n_mem, XATT_HEADS, XATT_HEAD_DIM), lambda i: (0, i, 0, 0, 0))
    return pl.pallas_call(
        _xattn_decode_kernel,
        grid=(b // bb,),
        in_specs=[pl.BlockSpec((bb, rows, XATT_HEAD_DIM), lambda i: (i, 0, 0)), kv, kv,
                  pl.BlockSpec((1, XATT_HEAD_DIM), lambda i: (0, 0))],
        out_specs=pl.BlockSpec((bb, rows, XATT_HEAD_DIM), lambda i: (i, 0, 0)),
        out_shape=jax.ShapeDtypeStruct((b, rows, XATT_HEAD_DIM), F32),
        compiler_params=_cp(("parallel",)),
        name="xattn_decode",
    )(q_pad, mem_k, mem_v, xq_norm_w.reshape(1, XATT_HEAD_DIM))


LORA_OFF = 3 * RWKV_WIDTH
GATE_OFF = LORA_OFF + DECAY_LORA + AAA_LORA
GATE_PAD = RWKV_PROJ_PAD - GATE_OFF


def _sigmoid(x):
    return 1.0 / (1.0 + jnp.exp(-x))


def _per_chunk(fn, *arrays):
    w = arrays[0].shape[1]
    outs = [fn(*(a[:, c * LANES:(c + 1) * LANES] for a in arrays)) for c in range(w // LANES)]
    return jnp.concatenate(outs, axis=1)


def _rwkv_prep_core(pr, prev, mu, w0, a0, kk_w, ka_w, rk_w, w_lora, w_gate):
    c = RWKV_WIDTH
    gmat = _head_indicator()
    xm = pr + (prev - pr) * mu
    r, k, v = xm[:, 0:c], xm[:, c:2 * c], xm[:, 2 * c:3 * c]
    lora = xm[:, LORA_OFF:LORA_OFF + LANES]
    lane = lax.broadcasted_iota(jnp.int32, lora.shape, 1)
    lora_in = jnp.where(lane < DECAY_LORA, jnp.tanh(lora), lora)
    wa = _dot(lora_in.astype(BF16), w_lora)
    z = -(w0 + wa[:, 0:c])
    softplus = jnp.maximum(z, 0.0) + jnp.log(1.0 + jnp.exp(-jnp.abs(z)))
    log_decay = -jnp.exp(-softplus - 0.5)
    a = _sigmoid(a0 + wa[:, c:2 * c])
    g = _dot(_sigmoid(xm[:, GATE_OFF:GATE_OFF + GATE_PAD]).astype(BF16), w_gate)
    kk = k * kk_w
    norm = jnp.sqrt(_per_chunk(lambda t: _group_sum(t * t, gmat), kk))
    kk = kk / jnp.maximum(norm, 1e-12)
    kp = k * (1.0 + (a - 1.0) * ka_w)
    bonus = _per_chunk(lambda t: _group_sum(t, gmat), r * kp * rk_w) * v
    return r, log_decay, kp, v, kk, kk * a, bonus, g


def _rwkv_prep_seq_kernel(pr_ref, prev0_ref, mu_ref, w0_ref, a0_ref, kkw_ref, kaw_ref, rkw_ref,
                          wl_ref, wg_ref, *refs):
    out_refs, last_ref = refs[:-1], refs[-1]

    @pl.when(pl.program_id(1) == 0)
    def _():
        last_ref[...] = prev0_ref[...]

    pr = pr_ref[...]
    rows = pr.shape[0]
    row = lax.broadcasted_iota(jnp.int32, (rows, 1), 0)
    prev = jnp.where(row == 0, last_ref[...], pltpu.roll(pr, 1, axis=0))
    last_ref[...] = pr[rows - 1:rows, :]
    outs = _rwkv_prep_core(pr, prev, mu_ref[...], w0_ref[...], a0_ref[...], kkw_ref[...],
                           kaw_ref[...], rkw_ref[...], wl_ref[...], wg_ref[...])
    for o_ref, o in zip(out_refs, outs):
        o_ref[...] = o


def _rwkv_prep_tok_kernel(pr_ref, prev_ref, mu_ref, w0_ref, a0_ref, kkw_ref, kaw_ref, rkw_ref,
                          wl_ref, wg_ref, *out_refs):
    outs = _rwkv_prep_core(pr_ref[...], prev_ref[...], mu_ref[...], w0_ref[...], a0_ref[...],
                           kkw_ref[...], kaw_ref[...], rkw_ref[...], wl_ref[...], wg_ref[...])
    for o_ref, o in zip(out_refs, outs):
        o_ref[...] = o


def _rwkv_param_specs(index_map):
    c = RWKV_WIDTH
    shapes = [(1, RWKV_PROJ_PAD)] + [(1, c)] * 5 + [(LANES, 2 * c), (GATE_PAD, c)]
    return [pl.BlockSpec(s, index_map) for s in shapes]


def rwkv_prep_seq(pr, prev0, params, *, tm=256):
    b, t, wd = pr.shape
    c = RWKV_WIDTH
    out = jax.ShapeDtypeStruct((b, t, c), F32)
    return pl.pallas_call(
        _rwkv_prep_seq_kernel,
        grid=(b, t // tm),
        in_specs=[pl.BlockSpec((None, tm, wd), lambda i, j: (i, j, 0)),
                  pl.BlockSpec((None, 1, wd), lambda i, j: (i, 0, 0))]
        + _rwkv_param_specs(lambda i, j: (0, 0)),
        out_specs=[pl.BlockSpec((None, tm, c), lambda i, j: (i, j, 0))] * 8,
        out_shape=[out] * 8,
        scratch_shapes=[pltpu.VMEM((1, wd), F32)],
        compiler_params=_cp(("parallel", "arbitrary")),
        name="rwkv_prep_seq",
    )(pr, prev0, *params)


def rwkv_prep_tok(pr, prev, params):
    m, wd = pr.shape
    c = RWKV_WIDTH
    out = jax.ShapeDtypeStruct((m, c), F32)
    return pl.pallas_call(
        _rwkv_prep_tok_kernel,
        grid=(1,),
        in_specs=[pl.BlockSpec((m, wd), lambda i: (0, 0))] * 2
        + _rwkv_param_specs(lambda i: (0, 0)),
        out_specs=[pl.BlockSpec((m, c), lambda i: (0, 0))] * 8,
        out_shape=[out] * 8,
        compiler_params=_cp(("arbitrary",)),
        name="rwkv_prep_tok",
    )(pr, prev, *params)


def _dot_tn(a, b):
    return lax.dot_general(a, b, (((0,), (0,)), ((), ())), preferred_element_type=F32)


def _rwkv_scan_kernel(r_ref, ld_ref, kp_ref, v_ref, kk_ref, b_ref, y_ref, s_out_ref, s_ref):
    @pl.when(pl.program_id(1) == 0)
    def _():
        s_ref[...] = jnp.zeros_like(s_ref)

    n = CHUNK
    ti = lax.broadcasted_iota(jnp.int32, (n, n), 0)
    si = lax.broadcasted_iota(jnp.int32, (n, n), 1)
    strict, incl = si < ti, si <= ti
    tri = jnp.where(incl, 1.0, 0.0).astype(BF16)

    ld = ld_ref[...]
    l1, l2, l3 = _split3(ld)
    lc = _dot(tri, l1) + _dot(tri, l2) + _dot(tri, l3)
    lc_end = lc[n - 1:n, :]
    e_neg = jnp.exp(-lc)
    kk, b, kp = kk_ref[...], b_ref[...], kp_ref[...]
    a_t = (-kk * jnp.exp(lc - ld)).astype(BF16)
    b_t = (b * e_neg).astype(BF16)
    k_t = (kp * e_neg).astype(BF16)
    r_t = (r_ref[...] * jnp.exp(lc)).astype(BF16)
    to_end = jnp.exp(lc_end - lc)
    b_e = (b * to_end).astype(BF16)
    k_e = (kp * to_end).astype(BF16)
    v_b = v_ref[...].astype(BF16)
    g_end = jnp.exp(lc_end)

    heads = range(RWKV_HEADS)
    sl = [slice(h * n, (h + 1) * n) for h in heads]
    gm = [_dot_nt(jnp.concatenate([a_t[:, sl[h]], r_t[:, sl[h]]], axis=0),
                  jnp.concatenate([b_t[:, sl[h]], k_t[:, sl[h]]], axis=0)) for h in heads]
    n_ab = [jnp.where(strict, gm[h][0:n, 0:n], 0.0).astype(BF16) for h in heads]
    l_ak = [jnp.where(strict, gm[h][0:n, n:2 * n], 0.0).astype(BF16) for h in heads]
    p_rb = [jnp.where(incl, gm[h][n:2 * n, 0:n], 0.0).astype(BF16) for h in heads]
    p_rk = [jnp.where(incl, gm[h][n:2 * n, n:2 * n], 0.0).astype(BF16) for h in heads]
    s0 = [s_ref[h] for h in heads]
    s0_b = [s0[h].astype(BF16) for h in heads]
    u = [_dot_nt(a_t[:, sl[h]], s0_b[h]) + _dot(l_ak[h], v_b[:, sl[h]]) for h in heads]
    pw = n_ab
    for step in range(6):
        u = [u[h] + _dot(pw[h], u[h].astype(BF16)) for h in heads]
        if step < 5:
            pw = [_dot(pw[h], pw[h]).astype(BF16) for h in heads]
    u_b = [u[h].astype(BF16) for h in heads]
    y = [_dot_nt(r_t[:, sl[h]], s0_b[h]) + _dot(p_rb[h], u_b[h]) + _dot(p_rk[h], v_b[:, sl[h]])
         for h in heads]
    s_new = [s0[h] * g_end[:, sl[h]] + _dot_tn(
        jnp.concatenate([u_b[h], v_b[:, sl[h]]], axis=0),
        jnp.concatenate([b_e[:, sl[h]], k_e[:, sl[h]]], axis=0)) for h in heads]
    for h in heads:
        y_ref[:, sl[h]] = y[h]
        s_ref[h] = s_new[h]
        s_out_ref[h] = s_new[h]


def rwkv_scan(r, ld, kp, v, kk, b):
    bsz, t, c = r.shape
    blk = pl.BlockSpec((None, CHUNK, c), lambda i, j: (i, j, 0))
    st = pl.BlockSpec((None, RWKV_HEADS, RWKV_HEAD_DIM, RWKV_HEAD_DIM), lambda i, j: (i, 0, 0, 0))
    return pl.pallas_call(
        _rwkv_scan_kernel,
        grid=(bsz, t // CHUNK),
        in_specs=[blk] * 6,
        out_specs=[blk, st],
        out_shape=[jax.ShapeDtypeStruct((bsz, t, c), F32),
                   jax.ShapeDtypeStruct((bsz, RWKV_HEADS, RWKV_HEAD_DIM, RWKV_HEAD_DIM), F32)],
        scratch_shapes=[pltpu.VMEM((RWKV_HEADS, RWKV_HEAD_DIM, RWKV_HEAD_DIM), F32)],
        compiler_params=_cp(("parallel", "arbitrary")),
        name="rwkv_scan",
    )(r, ld, kp, v, kk, b)


STEP_UNROLL = 8


def _rwkv_step_kernel(r_ref, ld_ref, kp_ref, v_ref, kk_ref, b_ref, s_ref, y_ref, s_out_ref):
    n = RWKV_HEAD_DIM
    eye = lax.broadcasted_iota(jnp.int32, (n, n), 0) == lax.broadcasted_iota(jnp.int32, (n, n), 1)

    n_heads = s_ref.shape[1]
    assert n_heads % STEP_UNROLL == 0

    def body(i, carry):
        bi = i // (n_heads // STEP_UNROLL)
        h0 = (i % (n_heads // STEP_UNROLL)) * STEP_UNROLL
        hs = [h0 + j for j in range(STEP_UNROLL)]
        gs = [bi * n_heads + h for h in hs]
        s = [s_ref[bi, h] for h in hs]
        sa = [jnp.sum(s[j] * (-kk_ref[g]), axis=1, keepdims=True) for j, g in enumerate(gs)]
        v_col = [jnp.sum(jnp.where(eye, v_ref[g], 0.0), axis=1, keepdims=True) for g in gs]
        s_new = [s[j] * jnp.exp(ld_ref[g]) + sa[j] * b_ref[g] + v_col[j] * kp_ref[g]
                 for j, g in enumerate(gs)]
        y_col = [jnp.sum(s_new[j] * r_ref[g], axis=1, keepdims=True) for j, g in enumerate(gs)]
        for j, (g, h) in enumerate(zip(gs, hs)):
            y_ref[g] = jnp.sum(jnp.where(eye, y_col[j], 0.0), axis=0, keepdims=True)
            s_out_ref[bi, h] = s_new[j]
        return carry

    lax.fori_loop(0, s_ref.shape[0] * n_heads // STEP_UNROLL, body, 0)


def rwkv_step(r, ld, kp, v, kk, b, state, *, bb=4):
    _, bsz, nh, n, _ = state.shape
    row = pl.BlockSpec((bb * nh, 1, n), lambda i: (i, 0, 0))
    st = pl.BlockSpec((None, bb, nh, n, n), lambda i: (0, i, 0, 0, 0))
    return pl.pallas_call(
        _rwkv_step_kernel,
        grid=(bsz // bb,),
        in_specs=[row] * 6 + [st],
        out_specs=[row, st],
        out_shape=[jax.ShapeDtypeStruct((bsz * nh, 1, n), F32),
                   jax.ShapeDtypeStruct(state.shape, F32)],
        compiler_params=_cp(("parallel",)),
        name="rwkv_step",
    )(r, ld, kp, v, kk, b, state)


def _rwkv_post_kernel(y_ref, bonus_ref, g_ref, lnw_ref, lnb_ref, o_ref):
    gmat = _head_indicator()
    inv = 1.0 / RWKV_HEAD_DIM
    for c in range(y_ref.shape[1] // LANES):
        sl = slice(c * LANES, (c + 1) * LANES)
        y = y_ref[:, sl]
        d = y - _group_sum(y, gmat) * inv
        var = _group_sum(d * d, gmat) * inv
        yn = d * lax.rsqrt(var + GN_EPS) * lnw_ref[:, sl] + lnb_ref[:, sl]
        o_ref[:, sl] = (yn + bonus_ref[:, sl]) * g_ref[:, sl]


def rwkv_post(y, bonus, g, ln_w, ln_b, *, tm):
    m, c = y.shape
    blk = pl.BlockSpec((tm, c), lambda i: (i, 0))
    vec = pl.BlockSpec((1, c), lambda i: (0, 0))
    return pl.pallas_call(
        _rwkv_post_kernel,
        grid=(m // tm,),
        in_specs=[blk, blk, blk, vec, vec],
        out_specs=blk,
        out_shape=jax.ShapeDtypeStruct((m, c), F32),
        compiler_params=_cp(("parallel",)),
        name="rwkv_post",
    )(y, bonus, g, ln_w.reshape(1, c), ln_b.reshape(1, c))


ROUTER_LANES = LANES
ROW_TILES = D_MODEL // LANES


def _rows_to_tiles(ref, x):
    rows = x.shape[0]
    for j in range(ROW_TILES):
        ref[pl.ds(j, rows, stride=ROW_TILES), :] = x[:, j * LANES:(j + 1) * LANES]


def _tiles_to_rows(ref, rows):
    return jnp.concatenate(
        [ref[pl.ds(j, rows, stride=ROW_TILES), :] for j in range(ROW_TILES)], axis=1)


def _router_kernel(ha_ref, hb_ref, lnw_ref, whi_ref, wlo_ref, bias_ref, u_ref, idx_ref, gate_ref,
                   *, steps_a):
    use_a = pl.program_id(0) < steps_a
    h = jnp.where(use_a, ha_ref[...], hb_ref[...])
    u = _rms_rows(h, lnw_ref[...])
    _rows_to_tiles(u_ref, u)
    u_hi, u_lo = _split2(u)
    w_hi = whi_ref[...]
    logits = _dot(u_hi, w_hi) + _dot(u_lo, w_hi) + _dot(u_hi, wlo_ref[...]) + bias_ref[...]
    lane = lax.broadcasted_iota(jnp.int32, logits.shape, 1)
    neg = -jnp.inf

    def first_max(x):
        m = jnp.max(x, axis=1, keepdims=True)
        return m, jnp.min(jnp.where(x == m, lane, ROUTER_LANES), axis=1, keepdims=True)

    gl = jnp.where(lane < N_EXPERT_GROUPS, logits, neg)
    g_max, g_idx = first_max(gl)
    g_gate = 1.0 / jnp.sum(jnp.exp(gl - g_max), axis=1, keepdims=True)
    lo = N_EXPERT_GROUPS + g_idx * EXPERTS_PER_GROUP
    el = jnp.where((lane >= lo) & (lane < lo + EXPERTS_PER_GROUP), logits, neg)
    v1, i1 = first_max(el)
    v2, i2 = first_max(jnp.where(lane == i1, neg, el))
    e2 = jnp.exp(v2 - v1)
    w1 = g_gate / (1.0 + e2)
    w2 = g_gate * e2 / (1.0 + e2)
    idx_ref[...] = jnp.where(lane == 0, i1 - N_EXPERT_GROUPS,
                             jnp.where(lane == 1, i2 - N_EXPERT_GROUPS, 0))
    gate_ref[...] = jnp.where(lane == 0, w1, jnp.where(lane == 1, w2, 0.0))


def moe_router(h_a, h_b, ln_w, w_hi, w_lo, bias, *, tm):
    (ma, d), mb = h_a.shape, h_b.shape[0]
    assert ma % tm == 0 and mb % tm == 0
    steps_a, steps_b = ma // tm, mb // tm
    m = ma + mb
    const = lambda r, w: pl.BlockSpec((r, w), lambda i: (0, 0))
    row = lambda w: pl.BlockSpec((tm, w), lambda i: (i, 0))
    return pl.pallas_call(
        functools.partial(_router_kernel, steps_a=steps_a),
        grid=(steps_a + steps_b,),
        in_specs=[pl.BlockSpec((tm, d), lambda i: (jnp.minimum(i, steps_a - 1), 0)),
                  pl.BlockSpec((tm, d), lambda i: (jnp.maximum(i - steps_a, 0), 0)),
                  const(1, d), const(d, ROUTER_LANES), const(d, ROUTER_LANES),
                  const(1, ROUTER_LANES)],
        out_specs=[pl.BlockSpec((tm * (d // LANES), LANES), lambda i: (i, 0)),
                   row(ROUTER_LANES), row(ROUTER_LANES)],
        out_shape=[jax.ShapeDtypeStruct((m * (d // LANES), LANES), F32),
                   jax.ShapeDtypeStruct((m, ROUTER_LANES), jnp.int32),
                   jax.ShapeDtypeStruct((m, ROUTER_LANES), F32)],
        compiler_params=_cp(("arbitrary",)),
        name="moe_router",
    )(h_a, h_b, ln_w.reshape(1, d), w_hi, w_lo, bias)


X_SLOTS = 3
Y_SLOTS = 2


W_SLOTS = 3


def _moe_expert_kernel(run_ref, rexp_ref, nused_ref, tok0_ref, tok1_ref, tok2_ref, dst_ref,
                       dstp_ref, roww_ref, u_hbm, wg_hbm, wu_hbm, wd_hbm, y_hbm, xbuf, ybuf,
                       wg_f, wu_f, wd_f, wg_b, wu_b, wd_b, sem_in, sem_out, sem_w):
    i = pl.program_id(0)
    n_used, n_runs = nused_ref[0], nused_ref[1]
    tile_rows = MOE_BLOCK * ROW_TILES
    pad_base = y_hbm.shape[0] - Y_SLOTS * tile_rows
    run = run_ref[i]

    def weight_copies(k):
        e, s = rexp_ref[jnp.minimum(k, n_runs - 1)], lax.rem(k, W_SLOTS)
        return [pltpu.make_async_copy(hbm.at[e], buf.at[s], sem_w.at[s])
                for hbm, buf in ((wg_hbm, wg_f), (wu_hbm, wu_f), (wd_hbm, wd_f))]

    half = MOE_BLOCK // 2

    def gather_rows(idx_ref, x_slot, rows):
        for r in rows:
            pltpu.make_async_copy(u_hbm.at[pl.ds(idx_ref[0, 0, r], ROW_TILES)],
                                  xbuf.at[x_slot, pl.ds(r * ROW_TILES, ROW_TILES)],
                                  sem_in.at[x_slot]).start(priority=r % 2)

    def gather_block(idx_ref, x_slot):
        gather_rows(idx_ref, x_slot, range(MOE_BLOCK))

    def scatter_rows(idx_ref, y_slot, rows):
        for r in rows:
            pltpu.make_async_copy(ybuf.at[y_slot, pl.ds(r * ROW_TILES, ROW_TILES)],
                                  y_hbm.at[pl.ds(idx_ref[0, 0, r], ROW_TILES)],
                                  sem_out.at[y_slot]).start(priority=r % 2)

    def gather_wait(x_slot):
        pltpu.make_async_copy(u_hbm.at[pl.ds(0, tile_rows)], xbuf.at[x_slot],
                              sem_in.at[x_slot]).wait()

    def scatter_wait(y_slot):
        pltpu.make_async_copy(ybuf.at[y_slot], y_hbm.at[pl.ds(0, tile_rows)],
                              sem_out.at[y_slot]).wait()

    @pl.when(i == 0)
    def _():
        ybuf[0] = jnp.zeros(ybuf.shape[1:], F32)
        for s in range(Y_SLOTS):
            pltpu.make_async_copy(ybuf.at[0], y_hbm.at[pl.ds(pad_base + s * tile_rows, tile_rows)],
                                  sem_out.at[s]).start()
        for k in range(W_SLOTS - 1):
            for cp in weight_copies(k):
                cp.start()
        gather_block(tok0_ref, 0)
        gather_block(tok1_ref, 1)
        for s in range(Y_SLOTS):
            scatter_wait(s)

    @pl.when(i < n_used)
    def _():
        x_slot = lax.rem(i, X_SLOTS)
        y_slot = lax.rem(i, Y_SLOTS)

        @pl.when((i == 0) | (run != run_ref[jnp.maximum(i - 1, 0)]))
        def _():
            for cp in weight_copies(run):
                cp.wait()
            w_slot = lax.rem(run, W_SLOTS)
            wg_b[...] = wg_f[w_slot].astype(BF16)
            wu_b[...] = wu_f[w_slot].astype(BF16)
            wd_b[...] = wd_f[w_slot].astype(BF16)
            for cp in weight_copies(run + W_SLOTS - 1):
                cp.start()

        def prev_scatter(rows):
            @pl.when(i >= 1)
            def _():
                scatter_rows(dstp_ref, 1 - y_slot, rows)

        gather_wait(x_slot)
        x = _tiles_to_rows(xbuf.at[x_slot], MOE_BLOCK).astype(BF16)
        prev_scatter(range(half))
        hg = _dot(x, wg_b[...])
        prev_scatter(range(half, MOE_BLOCK))
        hu = _dot(x, wu_b[...])
        gather_rows(tok2_ref, lax.rem(i + 2, X_SLOTS), range(half))
        act = (hg * _sigmoid(hg) * hu).astype(BF16)
        y = _dot(act, wd_b[...]) * roww_ref[...]
        gather_rows(tok2_ref, lax.rem(i + 2, X_SLOTS), range(half, MOE_BLOCK))

        @pl.when(i >= Y_SLOTS)
        def _():
            scatter_wait(y_slot)

        _rows_to_tiles(ybuf.at[y_slot], y)

        @pl.when(i == n_used - 1)
        def _():
            scatter_rows(dst_ref, y_slot, range(MOE_BLOCK))
            scatter_wait(y_slot)

            @pl.when(i >= 1)
            def _():
                scatter_wait(1 - y_slot)

            gather_wait(lax.rem(i + 1, X_SLOTS))
            gather_wait(lax.rem(i + 2, X_SLOTS))
            for k in range(W_SLOTS - 1):
                for cp in weight_copies(n_runs + k):
                    cp.wait()


def moe_experts(u_all, row_src, row_dst, row_w, block_run, run_exp, n_used_runs, w_gate, w_up,
                w_down, n_assign):
    d, ff = w_gate.shape[1], w_gate.shape[2]
    n_blocks = row_src.shape[0]
    tile_rows = MOE_BLOCK * ROW_TILES
    smem_blk = lambda off: pl.BlockSpec(
        (1, 1, MOE_BLOCK), lambda i, *_: (jnp.clip(i + off, 0, n_blocks - 1), 0, 0),
        memory_space=pltpu.SMEM)
    hbm = pl.BlockSpec(memory_space=pl.ANY)
    grid_spec = pltpu.PrefetchScalarGridSpec(
        num_scalar_prefetch=3,
        grid=(n_blocks,),
        in_specs=[
            smem_blk(0), smem_blk(1), smem_blk(2), smem_blk(0), smem_blk(-1),
            pl.BlockSpec((MOE_BLOCK, 1), lambda i, *_: (i, 0)),
            hbm, hbm, hbm, hbm,
        ],
        out_specs=hbm,
        scratch_shapes=[
            pltpu.VMEM((X_SLOTS, tile_rows, LANES), F32),
            pltpu.VMEM((Y_SLOTS, tile_rows, LANES), F32),
            pltpu.VMEM((W_SLOTS, d, ff), F32), pltpu.VMEM((W_SLOTS, d, ff), F32),
            pltpu.VMEM((W_SLOTS, ff, d), F32),
            pltpu.VMEM((d, ff), BF16), pltpu.VMEM((d, ff), BF16), pltpu.VMEM((ff, d), BF16),
            pltpu.SemaphoreType.DMA((X_SLOTS,)), pltpu.SemaphoreType.DMA((Y_SLOTS,)),
            pltpu.SemaphoreType.DMA((W_SLOTS,)),
        ],
    )
    y_rows = (n_assign + Y_SLOTS * MOE_BLOCK) * ROW_TILES
    return pl.pallas_call(
        _moe_expert_kernel,
        grid_spec=grid_spec,
        out_shape=jax.ShapeDtypeStruct((y_rows, LANES), F32),
        compiler_params=_cp(("arbitrary",)),
        name="moe_experts",
    )(block_run, run_exp, n_used_runs, row_src, row_src, row_src, row_dst, row_dst, row_w, u_all,
      w_gate, w_up, w_down)


def _moe_combine_kernel(h_ref, y0_ref, y1_ref, o_ref):
    rows = h_ref.shape[0]
    o_ref[...] = h_ref[...] + (_tiles_to_rows(y0_ref, rows) + _tiles_to_rows(y1_ref, rows))


def moe_combine(h, y_slots, row_off, slot_stride, *, tm):
    m, d = h.shape
    assert row_off % tm == 0 and slot_stride % tm == 0
    off0, off1 = row_off // tm, (row_off + slot_stride) // tm
    return pl.pallas_call(
        _moe_combine_kernel,
        grid=(m // tm,),
        in_specs=[pl.BlockSpec((tm, d), lambda i: (i, 0)),
                  pl.BlockSpec((tm * ROW_TILES, LANES), lambda i: (i + off0, 0)),
                  pl.BlockSpec((tm * ROW_TILES, LANES), lambda i: (i + off1, 0))],
        out_specs=pl.BlockSpec((tm, d), lambda i: (i, 0)),
        out_shape=jax.ShapeDtypeStruct((m, d), F32),
        compiler_params=_cp(("parallel",)),
        name="moe_combine",
    )(h, y_slots, y_slots)


def moe_dispatch(e_idx, gates, slot_stride):
    m = e_idx.shape[0]
    a = m * TOP_K
    e_flat = e_idx.reshape(a)
    order = jnp.argsort(e_flat, stable=True).astype(jnp.int32)
    counts = jnp.sum(e_flat[:, None] == jnp.arange(N_EXPERTS, dtype=jnp.int32)[None, :],
                     axis=0, dtype=jnp.int32)
    pad_counts = (counts + MOE_BLOCK - 1) // MOE_BLOCK * MOE_BLOCK
    starts = jnp.cumsum(counts) - counts
    pad_ends = jnp.cumsum(pad_counts)
    pad_starts = pad_ends - pad_counts
    n_blocks = a // MOE_BLOCK + N_EXPERTS
    p = n_blocks * MOE_BLOCK
    n_used = (pad_ends[-1] // MOE_BLOCK).astype(jnp.int32)
    blk = jnp.arange(n_blocks, dtype=jnp.int32)
    blk_start = jnp.minimum(blk, n_used - 1) * MOE_BLOCK
    block_exp = jnp.minimum(jnp.searchsorted(pad_ends, blk_start, side='right'),
                            N_EXPERTS - 1).astype(jnp.int32)
    in_exp = blk * MOE_BLOCK - pad_starts[block_exp]
    row_cnt = jnp.where(blk < n_used, jnp.clip(counts[block_exp] - in_exp, 0, MOE_BLOCK), 0)
    lane = jnp.arange(MOE_BLOCK, dtype=jnp.int32)[None, :]
    valid = lane < row_cnt[:, None]
    src = jnp.clip((starts[block_exp] + in_exp)[:, None] + lane, 0, a - 1)
    assign = order[src]
    row_tok = jnp.where(valid, assign // TOP_K, 0)
    pad_dst = TOP_K * slot_stride + (blk % Y_SLOTS)[:, None] * MOE_BLOCK + lane
    row_dst = jnp.where(valid, (assign % TOP_K) * slot_stride + assign // TOP_K, pad_dst)
    row_w = jnp.where(valid, gates.reshape(a)[assign], 0.0)
    as_blocks = lambda x: (x * ROW_TILES).astype(jnp.int32).reshape(n_blocks, 1, MOE_BLOCK)
    has_rows = counts > 0
    run_exp = jnp.argsort(~has_rows, stable=True).astype(jnp.int32)
    block_run = (jnp.cumsum(has_rows) - 1)[block_exp].astype(jnp.int32)
    n_used_runs = jnp.stack([n_used, jnp.sum(has_rows, dtype=jnp.int32)])
    return (as_blocks(row_tok), as_blocks(row_dst), row_w.reshape(p, 1), block_run, run_exp,
            n_used_runs)


def rwkv_params(rw_mu, rw_w0, rw_w2, rw_a0, rw_a2, rw_g2, rw_k_k, rw_k_a, rw_r_k):
    c = RWKV_WIDTH
    mu = jnp.pad(rw_mu, (0, RWKV_PROJ_PAD - RWKV_PROJ)).reshape(1, RWKV_PROJ_PAD)
    w_lora = jnp.zeros((LANES, 2 * c), F32)
    w_lora = w_lora.at[0:DECAY_LORA, 0:c].set(rw_w2).at[DECAY_LORA:LANES, c:2 * c].set(rw_a2)
    w_gate = jnp.pad(rw_g2, ((0, GATE_PAD - GATE_LORA), (0, 0)))
    vec = lambda x: x.reshape(1, c)
    return (mu, vec(rw_w0), vec(rw_a0), vec(rw_k_k), vec(rw_k_a), vec(rw_r_k),
            w_lora.astype(BF16), w_gate.astype(BF16))


def _token_tiles(m):
    return (1024, 512) if m % 1024 == 0 else (m, m)


def _dense_front(x2d, wts, tm):
    pa = norm_matmul(x2d, wts['ln1_w'], wts['w_att'], tm=tm, tn=512)
    pr = norm_matmul(x2d, wts['ln1_w'], wts['w_rw'], tm=tm, tn=RWKV_PROJ_PAD // 3)
    return pa, pr


def _dense_back(x2d, att2d, rw2d, wts, xattn_fn, tm):
    h1 = matmul_residual([att2d, rw2d], [wts['w_out_a'], wts['w_out_r']], x2d, tm=tm, tn=512)
    qx = norm_matmul(h1, wts['ln2_w'], wts['xq_w'], tm=tm, tn=XATT_WIDTH)
    ox = xattn_fn(qx)
    return matmul_residual([ox], [wts['xo_w']], h1, tm=tm, tn=512)


def kernel(x_prompt, x_sample, cache_win_k, cache_win_v, state_wkv, state_shift, cache_mem_k, cache_mem_v, mem_prompt, ln1_w, w_in, q_norm_w, k_norm_w, attn_sinks, rw_mu, rw_w0, rw_w2, rw_a0, rw_a2, rw_g2, rw_k_k, rw_k_a, rw_r_k, rw_ln_w, rw_ln_b, w_out, ln2_w, mem_norm_w, xq_w, xkv_w, xq_norm_w, xk_norm_w, xo_w, ln3_w, router_group_w, router_group_b, router_expert_w, router_expert_b, exp_w_gate, exp_w_up, exp_w_down):
    assert w_in.shape[0] == 1, "single-layer stack"
    bp, seq, d = x_prompt.shape
    bs = x_sample.shape[0]
    mp = bp * seq
    c = RWKV_WIDTH

    router_w = jnp.concatenate(
        [router_group_w[0], router_expert_w[0],
         jnp.zeros((d, ROUTER_LANES - N_EXPERT_GROUPS - N_EXPERTS), F32)], axis=1)
    router_hi = router_w.astype(BF16)
    wts = {
        'ln1_w': ln1_w[0], 'ln2_w': ln2_w[0], 'ln3_w': ln3_w[0],
        'w_att': w_in[0][:, :ATT_PROJ].astype(BF16),
        'w_rw': jnp.pad(w_in[0][:, ATT_PROJ:],
                        ((0, 0), (0, RWKV_PROJ_PAD - RWKV_PROJ))).astype(BF16),
        'w_out_a': w_out[0][:ATT_WIDTH].astype(BF16),
        'w_out_r': w_out[0][ATT_WIDTH:].astype(BF16),
        'xq_w': xq_w[0].astype(BF16), 'xo_w': xo_w[0].astype(BF16),
        'router_hi': router_hi,
        'router_lo': (router_w - router_hi.astype(F32)).astype(BF16),
        'router_b': jnp.pad(jnp.concatenate([router_group_b[0], router_expert_b[0]]),
                            (0, ROUTER_LANES - N_EXPERT_GROUPS - N_EXPERTS)).reshape(1, -1),
    }
    rw_par = rwkv_params(rw_mu[0], rw_w0[0], rw_w2[0], rw_a0[0], rw_a2[0], rw_g2[0],
                         rw_k_k[0], rw_k_a[0], rw_r_k[0])

    tm_p, te_p = _token_tiles(mp)
    xp = x_prompt.reshape(mp, d)
    pa, pr = _dense_front(xp, wts, tm_p)
    pa3 = pa.reshape(bp, seq, ATT_PROJ)
    pr3 = pr.reshape(bp, seq, RWKV_PROJ_PAD)
    tabs_p = rope_tables(jnp.arange(seq, dtype=jnp.int32))
    att_p, kn_p = swa_prompt(pa3, tabs_p, q_norm_w[0], k_norm_w[0], attn_sinks[0])
    prep = rwkv_prep_seq(pr3, jnp.zeros((bp, 1, RWKV_PROJ_PAD), F32), rw_par)
    r, ld, kp, v, kk, b, bonus, g = prep
    y_p, wkv_p = rwkv_scan(r, ld, kp, v, kk, b)
    rw_p = rwkv_post(y_p.reshape(mp, c), bonus.reshape(mp, c), g.reshape(mp, c),
                     rw_ln_w[0], rw_ln_b[0], tm=te_p)

    n_mem = mem_prompt.shape[1]
    kv_mem = norm_matmul(mem_prompt.reshape(bp * n_mem, d), mem_norm_w[0],
                         xkv_w[0].astype(BF16), tm=bp * n_mem, tn=512)
    mem_k = head_rms(kv_mem[:, :XATT_WIDTH], xk_norm_w[0])
    mem_v = kv_mem[:, XATT_WIDTH:]
    mem_k3 = mem_k.reshape(bp, n_mem, XATT_WIDTH)
    mem_v3 = mem_v.reshape(bp, n_mem, XATT_WIDTH)

    def xattn_p(qx):
        return xattn_prompt(qx.reshape(bp, seq, XATT_WIDTH), mem_k3, mem_v3,
                            xq_norm_w[0]).reshape(mp, XATT_WIDTH)

    h2_p = _dense_back(xp, att_p.reshape(mp, ATT_WIDTH), rw_p, wts, xattn_p, tm_p)

    tm_s, te_s = _token_tiles(bs)
    xs = x_sample.reshape(bs, d)
    sa, sr = _dense_front(xs, wts, tm_s)
    tabs_s = rope_tables(PAST_LEN + jnp.arange(1, dtype=jnp.int32))
    qk_w = jnp.concatenate([jnp.tile(q_norm_w[0], ATT_HEADS),
                            jnp.tile(k_norm_w[0], ATT_KV_HEADS)]).reshape(1, -1)
    qk = qk_norm_rope(sa[:, :ATT_WIDTH + KV_WIDTH], qk_w, tabs_s)
    nbuf = cache_win_k.shape[2]
    att_s, win_k, win_v = swa_decode(
        qk[:, :ATT_WIDTH].reshape(bs, ATT_HEADS, HEAD_DIM),
        qk[:, ATT_WIDTH:].reshape(bs, 1, KV_WIDTH),
        sa[:, ATT_WIDTH + KV_WIDTH:].reshape(bs, 1, KV_WIDTH),
        cache_win_k[0].reshape(bs, nbuf, KV_WIDTH), cache_win_v[0].reshape(bs, nbuf, KV_WIDTH),
        attn_sinks[0])
    shift_prev = jnp.pad(state_shift[0], ((0, 0), (0, RWKV_PROJ_PAD - RWKV_PROJ)))
    r, ld, kp, v, kk, b, bonus, g = rwkv_prep_tok(sr, shift_prev, rw_par)
    gh = bs * RWKV_HEADS
    rows = [t.reshape(gh, 1, RWKV_HEAD_DIM) for t in (r, ld, kp, v, kk, b)]
    y_s, wkv_s = rwkv_step(*rows, state_wkv)
    rw_s = rwkv_post(y_s.reshape(bs, c), bonus, g, rw_ln_w[0], rw_ln_b[0], tm=te_s)
    def xattn_s(qx):
        q_pad = jnp.pad(qx.reshape(bs, XATT_HEADS, XATT_HEAD_DIM), ((0, 0), (0, 4), (0, 0)))
        o = xattn_decode(q_pad, cache_mem_k, cache_mem_v, xq_norm_w[0])
        return o[:, :XATT_HEADS].reshape(bs, XATT_WIDTH)

    h2_s = _dense_back(xs, att_s.reshape(bs, ATT_WIDTH), rw_s, wts, xattn_s, tm_s)

    m_all = mp + bs
    slot_stride = m_all
    tc = math.gcd(mp, bs, 512)
    u_all, idx_all, gate_all = moe_router(h2_p, h2_s, wts['ln3_w'], wts['router_hi'],
                                          wts['router_lo'], wts['router_b'], tm=tc)
    row_src, row_dst, row_w, block_run, run_exp, n_used_runs = moe_dispatch(
        idx_all[:, :TOP_K], gate_all[:, :TOP_K], slot_stride)
    y_slots = moe_experts(u_all, row_src, row_dst, row_w, block_run, run_exp, n_used_runs,
                          exp_w_gate[0], exp_w_up[0], exp_w_down[0], TOP_K * slot_stride)
    out_p = moe_combine(h2_p, y_slots, 0, slot_stride, tm=tc)
    out_s = moe_combine(h2_s, y_slots, mp, slot_stride, tm=tc)

    win = min(WINDOW, seq)
    kv_shape = (1, bp, win, ATT_KV_HEADS, HEAD_DIM)
    return (
        out_p.reshape(bp, seq, d),
        out_s.reshape(bs, 1, d),
        kn_p[:, seq - win:].reshape(kv_shape),
        pa3[:, seq - win:, ATT_WIDTH + KV_WIDTH:].reshape(kv_shape),
        wkv_p[None],
        pr3[:, seq - 1, :RWKV_PROJ][None],
        mem_k3.reshape(1, bp, n_mem, XATT_HEADS, XATT_HEAD_DIM),
        mem_v3.reshape(1, bp, n_mem, XATT_HEADS, XATT_HEAD_DIM),
        win_k.reshape(1, bs, nbuf, ATT_KV_HEADS, HEAD_DIM),
        win_v.reshape(1, bs, nbuf, ATT_KV_HEADS, HEAD_DIM),
        wkv_s,
        sr[:, :RWKV_PROJ].reshape(1, bs, RWKV_PROJ),
    )
```

```python
import functools
import math

import jax
import jax.numpy as jnp
from jax import lax
from jax.experimental import pallas as pl
from jax.experimental.pallas import tpu as pltpu

F32 = jnp.float32
BF16 = jnp.bfloat16

D_MODEL = 2048
HEAD_DIM = 64
ATT_HEADS = 16
ATT_KV_HEADS = 4
ATT_GROUP = ATT_HEADS // ATT_KV_HEADS
ATT_WIDTH = ATT_HEADS * HEAD_DIM
KV_WIDTH = ATT_KV_HEADS * HEAD_DIM
ATT_PROJ = ATT_WIDTH + 2 * KV_WIDTH
WINDOW = 128
ATT_SCALE = HEAD_DIM ** -0.5
ROPE_THETA = 500000.0
ROT_DIM = HEAD_DIM // 4
PAST_LEN = 16384

RWKV_WIDTH = 1024
RWKV_HEAD_DIM = 64
RWKV_HEADS = 16
DECAY_LORA = 64
AAA_LORA = 64
GATE_LORA = 160
RWKV_PROJ = 3 * RWKV_WIDTH + DECAY_LORA + AAA_LORA + GATE_LORA
RWKV_PROJ_PAD = 3456

N_MEM = 256
XATT_HEADS = 4
XATT_HEAD_DIM = 128
XATT_WIDTH = XATT_HEADS * XATT_HEAD_DIM

N_EXPERT_GROUPS = 8
EXPERTS_PER_GROUP = 8
N_EXPERTS = 64
TOP_K = 2
EXPERT_FF = D_MODEL // 4
MOE_BLOCK = 128

RMS_EPS = 1e-6
GN_EPS = 64e-5

LANES = 128
CHUNK = 64
VMEM_LIMIT = 56 * 1024 * 1024


def _cp(sem, vmem=VMEM_LIMIT):
    return pltpu.CompilerParams(dimension_semantics=sem, vmem_limit_bytes=vmem)


def _rms_rows(x, w):
    ms = jnp.mean(x * x, axis=-1, keepdims=True)
    return x * lax.rsqrt(ms + RMS_EPS) * w


def _split2(x):
    hi = x.astype(BF16)
    lo = (x - hi.astype(F32)).astype(BF16)
    return hi, lo


def _split3(x):
    h1 = x.astype(BF16)
    r1 = x - h1.astype(F32)
    h2 = r1.astype(BF16)
    h3 = (r1 - h2.astype(F32)).astype(BF16)
    return h1, h2, h3


def _dot(a, b):
    return jnp.dot(a, b, preferred_element_type=F32)


def _dot_nt(a, b):
    return lax.dot_general(a, b, (((1,), (1,)), ((), ())), preferred_element_type=F32)


def _group_sum(x, gmat):
    hi, lo = _split2(x)
    return _dot(hi, gmat) + _dot(lo, gmat)


def _head_indicator():
    r = lax.broadcasted_iota(jnp.int32, (LANES, LANES), 0) // HEAD_DIM
    c = lax.broadcasted_iota(jnp.int32, (LANES, LANES), 1) // HEAD_DIM
    return jnp.where(r == c, 1.0, 0.0).astype(BF16)


def _norm_mm_kernel(x_ref, lnw_ref, w_ref, o_ref, xn_ref):
    @pl.when(pl.program_id(1) == 0)
    def _():
        xn_ref[...] = _rms_rows(x_ref[...], lnw_ref[...]).astype(BF16)

    o_ref[...] = _dot(xn_ref[...], w_ref[...])


def norm_matmul(x, ln_w, w_bf16, *, tm, tn):
    m, k = x.shape
    n = w_bf16.shape[1]
    assert m % tm == 0 and n % tn == 0
    return pl.pallas_call(
        _norm_mm_kernel,
        grid=(m // tm, n // tn),
        in_specs=[
            pl.BlockSpec((tm, k), lambda i, j: (i, 0)),
            pl.BlockSpec((1, k), lambda i, j: (0, 0)),
            pl.BlockSpec((k, tn), lambda i, j: (0, j)),
        ],
        out_specs=pl.BlockSpec((tm, tn), lambda i, j: (i, j)),
        out_shape=jax.ShapeDtypeStruct((m, n), F32),
        scratch_shapes=[pltpu.VMEM((tm, k), BF16)],
        compiler_params=_cp(("parallel", "arbitrary")),
        name="norm_matmul",
    )(x, ln_w.reshape(1, k), w_bf16)


def _mm_res_kernel(*refs, n_lhs):
    a_refs = refs[:n_lhs]
    w_refs = refs[n_lhs:2 * n_lhs]
    res_ref = refs[2 * n_lhs]
    o_ref = refs[2 * n_lhs + 1]
    acc = res_ref[...]
    for a_ref, w_ref in zip(a_refs, w_refs):
        acc = acc + _dot(a_ref[...].astype(BF16), w_ref[...])
    o_ref[...] = acc


def matmul_residual(lhs_list, w_list, res, *, tm, tn):
    m, n = res.shape
    n_lhs = len(lhs_list)
    assert m % tm == 0 and n % tn == 0
    in_specs = [pl.BlockSpec((tm, a.shape[1]), lambda i, j: (i, 0)) for a in lhs_list]
    in_specs += [pl.BlockSpec((w.shape[0], tn), lambda i, j: (0, j)) for w in w_list]
    in_specs += [pl.BlockSpec((tm, tn), lambda i, j: (i, j))]
    return pl.pallas_call(
        functools.partial(_mm_res_kernel, n_lhs=n_lhs),
        grid=(m // tm, n // tn),
        in_specs=in_specs,
        out_specs=pl.BlockSpec((tm, tn), lambda i, j: (i, j)),
        out_shape=jax.ShapeDtypeStruct((m, n), F32),
        compiler_params=_cp(("parallel", "arbitrary")),
        name="matmul_residual",
    )(*lhs_list, *w_list, res)


def rope_tables(pos):
    half = ROT_DIM // 2
    inv = ROPE_THETA ** (-jnp.arange(half, dtype=F32) * 2.0 / ROT_DIM)
    ang = pos.astype(F32)[:, None] * inv[None, :]
    cos, sin = jnp.cos(ang), jnp.sin(ang)
    t = pos.shape[0]
    ones = jnp.ones((t, HEAD_DIM - ROT_DIM), F32)
    zeros = jnp.zeros((t, HEAD_DIM - ROT_DIM), F32)
    z8 = jnp.zeros((t, half), F32)
    cos_t = jnp.concatenate([cos, cos, ones], axis=1)
    sin_a = jnp.concatenate([z8, sin, zeros], axis=1)
    sin_b = jnp.concatenate([-sin, z8, zeros], axis=1)
    return tuple(jnp.concatenate([a, a], axis=1) for a in (cos_t, sin_a, sin_b))


def _norm_rope_chunk(x, w, cos_t, sin_a, sin_b, gmat):
    ms = _group_sum(x * x, gmat) * (1.0 / HEAD_DIM)
    xn = x * lax.rsqrt(ms + RMS_EPS) * w
    half = ROT_DIM // 2
    return (xn * cos_t + pltpu.roll(xn, half, axis=1) * sin_a
            + pltpu.roll(xn, LANES - half, axis=1) * sin_b)


def _norm_rope(x, w, tabs, gmat):
    chunks = [
        _norm_rope_chunk(x[:, c * LANES:(c + 1) * LANES], w, *tabs, gmat)
        for c in range(x.shape[1] // LANES)
    ]
    return chunks[0] if len(chunks) == 1 else jnp.concatenate(chunks, axis=1)


def _sink_softmax(s, sink):
    m = jnp.maximum(jnp.max(s, axis=-1, keepdims=True), sink)
    e = jnp.exp(s - m)
    return e / (jnp.sum(e, axis=-1, keepdims=True) + jnp.exp(sink - m))


def _swa_prompt_kernel(q_ref, kc_ref, vc_ref, kp_ref, vp_ref, cc_ref, sac_ref, sbc_ref,
                       cp_ref, sap_ref, sbp_ref, qw_ref, kw_ref, sink_ref, o_ref, kn_ref):
    n = pl.program_id(1)
    blk = q_ref.shape[0]
    gmat = _head_indicator()
    tabs_c = (cc_ref[...], sac_ref[...], sbc_ref[...])
    tabs_p = (cp_ref[...], sap_ref[...], sbp_ref[...])
    q = _norm_rope(q_ref[...], qw_ref[...], tabs_c, gmat)
    k_cur = _norm_rope(kc_ref[...], kw_ref[...], tabs_c, gmat)
    k_prev = _norm_rope(kp_ref[...], kw_ref[...], tabs_p, gmat)
    kn_ref[...] = k_cur
    k_all = jnp.concatenate([k_prev, k_cur], axis=0).astype(BF16)
    v_all = jnp.concatenate([vp_ref[...], vc_ref[...]], axis=0).astype(BF16)

    qi = lax.broadcasted_iota(jnp.int32, (blk, 2 * blk), 0) + blk
    si = lax.broadcasted_iota(jnp.int32, (blk, 2 * blk), 1)
    rel = qi - si
    valid = (rel >= 0) & (rel <= WINDOW) & ((n > 0) | (si >= blk))

    for kv in range(ATT_KV_HEADS):
        k_h = k_all[:, kv * HEAD_DIM:(kv + 1) * HEAD_DIM]
        v_h = v_all[:, kv * HEAD_DIM:(kv + 1) * HEAD_DIM]
        heads = [kv * ATT_GROUP + g for g in range(ATT_GROUP)]
        q_g = jnp.concatenate(
            [q[:, h * HEAD_DIM:(h + 1) * HEAD_DIM] for h in heads], axis=0).astype(BF16)
        s = _dot_nt(q_g, k_h) * ATT_SCALE
        probs = []
        for g, h in enumerate(heads):
            s_h = jnp.where(valid, s[g * blk:(g + 1) * blk], -jnp.inf)
            probs.append(_sink_softmax(s_h, sink_ref[h]))
        p = jnp.concatenate(probs, axis=0).astype(BF16)
        o = _dot(p, v_h)
        for g, h in enumerate(heads):
            o_ref[:, h * HEAD_DIM:(h + 1) * HEAD_DIM] = o[g * blk:(g + 1) * blk]


def swa_prompt(pa, tabs, q_norm_w, k_norm_w, sinks):
    b, t, _ = pa.shape
    blk = WINDOW
    nb = t // blk
    qb, kb, vb = 0, ATT_WIDTH // KV_WIDTH, ATT_WIDTH // KV_WIDTH + 1
    cur = lambda i, n, *_: (i, n, 0)
    tab_cur = pl.BlockSpec((blk, LANES), lambda i, n: (n, 0))
    tab_prev = pl.BlockSpec((blk, LANES), lambda i, n: (jnp.maximum(n - 1, 0), 0))
    qw = jnp.tile(q_norm_w.reshape(1, HEAD_DIM), (1, 2))
    kw = jnp.tile(k_norm_w.reshape(1, HEAD_DIM), (1, 2))
    return pl.pallas_call(
        _swa_prompt_kernel,
        grid=(b, nb),
        in_specs=[
            pl.BlockSpec((None, blk, ATT_WIDTH), lambda i, n: (i, n, qb)),
            pl.BlockSpec((None, blk, KV_WIDTH), lambda i, n: (i, n, kb)),
            pl.BlockSpec((None, blk, KV_WIDTH), lambda i, n: (i, n, vb)),
            pl.BlockSpec((None, blk, KV_WIDTH), lambda i, n: (i, jnp.maximum(n - 1, 0), kb)),
            pl.BlockSpec((None, blk, KV_WIDTH), lambda i, n: (i, jnp.maximum(n - 1, 0), vb)),
            tab_cur, tab_cur, tab_cur, tab_prev, tab_prev, tab_prev,
            pl.BlockSpec((1, LANES), lambda i, n: (0, 0)),
            pl.BlockSpec((1, LANES), lambda i, n: (0, 0)),
            pl.BlockSpec(memory_space=pltpu.SMEM),
        ],
        out_specs=[
            pl.BlockSpec((None, blk, ATT_WIDTH), cur),
            pl.BlockSpec((None, blk, KV_WIDTH), cur),
        ],
        out_shape=[
            jax.ShapeDtypeStruct((b, t, ATT_WIDTH), F32),
            jax.ShapeDtypeStruct((b, t, KV_WIDTH), F32),
        ],
        compiler_params=_cp(("parallel", "arbitrary")),
        name="swa_prompt",
    )(pa, pa, pa, pa, pa, *tabs, *tabs, qw, kw, sinks)


def _qk_norm_rope_kernel(x_ref, w_ref, c_ref, sa_ref, sb_ref, o_ref):
    gmat = _head_indicator()
    tabs = (c_ref[...], sa_ref[...], sb_ref[...])
    for c in range(x_ref.shape[1] // LANES):
        sl = slice(c * LANES, (c + 1) * LANES)
        o_ref[:, sl] = _norm_rope_chunk(x_ref[:, sl], w_ref[:, sl], *tabs, gmat)


def qk_norm_rope(x, w_row, tabs):
    m, w = x.shape
    full = lambda *shape: pl.BlockSpec(shape, lambda: (0,) * len(shape))
    return pl.pallas_call(
        _qk_norm_rope_kernel,
        in_specs=[full(m, w), full(1, w), full(1, LANES), full(1, LANES), full(1, LANES)],
        out_specs=full(m, w),
        out_shape=jax.ShapeDtypeStruct((m, w), F32),
        name="qk_norm_rope",
    )(x, w_row, *tabs)


def _swa_decode_kernel(q_ref, kn_ref, vn_ref, knt_ref, vnt_ref, ck_ref, cv_ref, sink_ref, o_ref,
                       kw_ref, vw_ref):
    bb = q_ref.shape[0]
    nbuf = ck_ref.shape[2]
    row_kv = lax.broadcasted_iota(jnp.int32, (ATT_HEADS, KV_WIDTH), 0) // ATT_GROUP
    lane_kv = lax.broadcasted_iota(jnp.int32, (ATT_HEADS, KV_WIDTH), 1) // HEAD_DIM
    own = row_kv == lane_kv
    sink = sink_ref[...]
    seq = lax.broadcasted_iota(jnp.int32, knt_ref.shape, 1)
    pos = lax.broadcasted_iota(jnp.int32, (KV_WIDTH, nbuf), 1)
    b0 = pl.program_id(0) * bb
    for b in range(bb):
        q2 = q_ref[b]
        q_exp = jnp.where(own, jnp.concatenate([q2] * ATT_KV_HEADS, axis=1), 0.0)
        k_new, v_new = kn_ref[b], vn_ref[b]
        k_buf, v_buf = ck_ref[b], cv_ref[b]
        s_buf = _dot(q_exp.astype(BF16), k_buf.astype(BF16)) * ATT_SCALE
        s_new = jnp.sum(q_exp * k_new, axis=-1, keepdims=True) * ATT_SCALE
        m = jnp.maximum(jnp.maximum(jnp.max(s_buf, axis=-1, keepdims=True), s_new), sink)
        e_buf = jnp.exp(s_buf - m)
        e_new = jnp.exp(s_new - m)
        inv = 1.0 / (jnp.sum(e_buf, axis=-1, keepdims=True) + e_new + jnp.exp(sink - m))
        o = _dot_nt((e_buf * inv).astype(BF16), v_buf.astype(BF16)) + (e_new * inv) * v_new
        o = jnp.where(own, o, 0.0)
        o_ref[b] = (o[:, 0:HEAD_DIM] + o[:, HEAD_DIM:2 * HEAD_DIM]
                    + o[:, 2 * HEAD_DIM:3 * HEAD_DIM] + o[:, 3 * HEAD_DIM:4 * HEAD_DIM])
        k_col = jnp.sum(jnp.where(seq == b0 + b, knt_ref[...], 0.0), axis=1, keepdims=True)
        v_col = jnp.sum(jnp.where(seq == b0 + b, vnt_ref[...], 0.0), axis=1, keepdims=True)
        kw_ref[b] = jnp.where(pos == nbuf - 1, k_col, pltpu.roll(k_buf, nbuf - 1, axis=1))
        vw_ref[b] = jnp.where(pos == nbuf - 1, v_col, pltpu.roll(v_buf, nbuf - 1, axis=1))


def swa_decode(q, k_new, v_new, cache_kt, cache_vt, sinks, *, bb=8):
    b, _, nbuf = cache_kt.shape
    blk3 = lambda s1, s2: pl.BlockSpec((bb, s1, s2), lambda i: (i, 0, 0))
    whole = pl.BlockSpec((KV_WIDTH, b), lambda i: (0, 0))
    return pl.pallas_call(
        _swa_decode_kernel,
        grid=(b // bb,),
        in_specs=[
            blk3(ATT_HEADS, HEAD_DIM), blk3(1, KV_WIDTH), blk3(1, KV_WIDTH), whole, whole,
            blk3(KV_WIDTH, nbuf), blk3(KV_WIDTH, nbuf),
            pl.BlockSpec((ATT_HEADS, 1), lambda i: (0, 0)),
        ],
        out_specs=[blk3(ATT_HEADS, HEAD_DIM), blk3(KV_WIDTH, nbuf), blk3(KV_WIDTH, nbuf)],
        out_shape=[
            jax.ShapeDtypeStruct((b, ATT_HEADS, HEAD_DIM), F32),
            jax.ShapeDtypeStruct((b, KV_WIDTH, nbuf), F32),
            jax.ShapeDtypeStruct((b, KV_WIDTH, nbuf), F32),
        ],
        compiler_params=_cp(("parallel",)),
        name="swa_decode",
    )(q, k_new, v_new, k_new.reshape(b, KV_WIDTH).T, v_new.reshape(b, KV_WIDTH).T,
      cache_kt, cache_vt, sinks.reshape(ATT_HEADS, 1))


def _head_rms_kernel(x_ref, w_ref, o_ref):
    for h in range(x_ref.shape[1] // XATT_HEAD_DIM):
        sl = slice(h * XATT_HEAD_DIM, (h + 1) * XATT_HEAD_DIM)
        o_ref[:, sl] = _rms_rows(x_ref[:, sl], w_ref[...])


def head_rms(x, w):
    m, wd = x.shape
    return pl.pallas_call(
        _head_rms_kernel,
        in_specs=[pl.BlockSpec((m, wd), lambda: (0, 0)),
                  pl.BlockSpec((1, XATT_HEAD_DIM), lambda: (0, 0))],
        out_specs=pl.BlockSpec((m, wd), lambda: (0, 0)),
        out_shape=jax.ShapeDtypeStruct((m, wd), F32),
        name="head_rms",
    )(x, w.reshape(1, XATT_HEAD_DIM))


def _xattn_prompt_kernel(q_ref, k_ref, v_ref, w_ref, o_ref):
    scale = 1.0 / math.sqrt(XATT_HEAD_DIM)
    for h in range(XATT_HEADS):
        sl = slice(h * XATT_HEAD_DIM, (h + 1) * XATT_HEAD_DIM)
        qn = _rms_rows(q_ref[:, sl], w_ref[...]).astype(BF16)
        s = _dot_nt(qn, k_ref[:, sl].astype(BF16)) * scale
        e = jnp.exp(s - jnp.max(s, axis=-1, keepdims=True))
        p = e / jnp.sum(e, axis=-1, keepdims=True)
        o_ref[:, sl] = _dot(p.astype(BF16), v_ref[:, sl].astype(BF16))


def xattn_prompt(q, mem_k, mem_v, xq_norm_w, *, tq=512):
    b, t, w = q.shape
    n_mem = mem_k.shape[1]
    return pl.pallas_call(
        _xattn_prompt_kernel,
        grid=(b, t // tq),
        in_specs=[
            pl.BlockSpec((None, tq, w), lambda i, j: (i, j, 0)),
            pl.BlockSpec((None, n_mem, w), lambda i, j: (i, 0, 0)),
            pl.BlockSpec((None, n_mem, w), lambda i, j: (i, 0, 0)),
            pl.BlockSpec((1, XATT_HEAD_DIM), lambda i, j: (0, 0)),
        ],
        out_specs=pl.BlockSpec((None, tq, w), lambda i, j: (i, j, 0)),
        out_shape=jax.ShapeDtypeStruct((b, t, w), F32),
        compiler_params=_cp(("parallel", "arbitrary")),
        name="xattn_prompt",
    )(q, mem_k, mem_v, xq_norm_w.reshape(1, XATT_HEAD_DIM))


def _xattn_decode_kernel(q_ref, k_ref, v_ref, w_ref, o_ref):
    bb, rows, _ = q_ref.shape
    scale = 1.0 / math.sqrt(XATT_HEAD_DIM)
    row = lax.broadcasted_iota(jnp.int32, (rows, 1), 0)
    for b in range(bb):
        qn = _rms_rows(q_ref[b], w_ref[...]).astype(BF16)
        s = sum(jnp.where(row == h, _dot_nt(qn, k_ref[b, :, h, :].astype(BF16)), 0.0)
                for h in range(XATT_HEADS)) * scale
        e = jnp.exp(s - jnp.max(s, axis=-1, keepdims=True))
        p = (e / jnp.sum(e, axis=-1, keepdims=True)).astype(BF16)
        o_ref[b] = sum(jnp.where(row == h, _dot(p, v_ref[b, :, h, :].astype(BF16)), 0.0)
                       for h in range(XATT_HEADS))


def xattn_decode(q_pad, mem_k, mem_v, xq_norm_w, *, bb=8):
    b, rows, _ = q_pad.shape
    n_mem = mem_k.shape[2]
    kv = pl.BlockSpec((None, bb, n_mem, XATT_HEADS, XATT_HEAD_DIM), lambda i: (0, i, 0, 0, 0))
    return pl.pallas_call(
        _xattn_decode_kernel,
        grid=(b // bb,),
        in_specs=[pl.BlockSpec((bb, rows, XATT_HEAD_DIM), lambda i: (i, 0, 0)), kv, kv,
                  pl.BlockSpec((1, XATT_HEAD_DIM), lambda i: (0, 0))],
        out_specs=pl.BlockSpec((bb, rows, XATT_HEAD_DIM), lambda i: (i, 0, 0)),
        out_shape=jax.ShapeDtypeStruct((b, rows, XATT_HEAD_DIM), F32),
        compiler_params=_cp(("parallel",)),
        name="xattn_decode",
    )(q_pad, mem_k, mem_v, xq_norm_w.reshape(1, XATT_HEAD_DIM))


LORA_OFF = 3 * RWKV_WIDTH
GATE_OFF = LORA_OFF + DECAY_LORA + AAA_LORA
GATE_PAD = RWKV_PROJ_PAD - GATE_OFF


def _sigmoid(x):
    return 1.0 / (1.0 + jnp.exp(-x))


def _per_chunk(fn, *arrays):
    w = arrays[0].shape[1]
    outs = [fn(*(a[:, c * LANES:(c + 1) * LANES] for a in arrays)) for c in range(w // LANES)]
    return jnp.concatenate(outs, axis=1)


def _rwkv_prep_core(pr, prev, mu, w0, a0, kk_w, ka_w, rk_w, w_lora, w_gate):
    c = RWKV_WIDTH
    gmat = _head_indicator()
    xm = pr + (prev - pr) * mu
    r, k, v = xm[:, 0:c], xm[:, c:2 * c], xm[:, 2 * c:3 * c]
    lora = xm[:, LORA_OFF:LORA_OFF + LANES]
    lane = lax.broadcasted_iota(jnp.int32, lora.shape, 1)
    lora_in = jnp.where(lane < DECAY_LORA, jnp.tanh(lora), lora)
    wa = _dot(lora_in.astype(BF16), w_lora)
    z = -(w0 + wa[:, 0:c])
    softplus = jnp.maximum(z, 0.0) + jnp.log(1.0 + jnp.exp(-jnp.abs(z)))
    log_decay = -jnp.exp(-softplus - 0.5)
    a = _sigmoid(a0 + wa[:, c:2 * c])
    g = _dot(_sigmoid(xm[:, GATE_OFF:GATE_OFF + GATE_PAD]).astype(BF16), w_gate)
    kk = k * kk_w
    norm = jnp.sqrt(_per_chunk(lambda t: _group_sum(t * t, gmat), kk))
    kk = kk / jnp.maximum(norm, 1e-12)
    kp = k * (1.0 + (a - 1.0) * ka_w)
    bonus = _per_chunk(lambda t: _group_sum(t, gmat), r * kp * rk_w) * v
    return r, log_decay, kp, v, kk, kk * a, bonus, g


def _rwkv_prep_seq_kernel(pr_ref, prev0_ref, mu_ref, w0_ref, a0_ref, kkw_ref, kaw_ref, rkw_ref,
                          wl_ref, wg_ref, *refs):
    out_refs, last_ref = refs[:-1], refs[-1]

    @pl.when(pl.program_id(1) == 0)
    def _():
        last_ref[...] = prev0_ref[...]

    pr = pr_ref[...]
    rows = pr.shape[0]
    row = lax.broadcasted_iota(jnp.int32, (rows, 1), 0)
    prev = jnp.where(row == 0, last_ref[...], pltpu.roll(pr, 1, axis=0))
    last_ref[...] = pr[rows - 1:rows, :]
    outs = _rwkv_prep_core(pr, prev, mu_ref[...], w0_ref[...], a0_ref[...], kkw_ref[...],
                           kaw_ref[...], rkw_ref[...], wl_ref[...], wg_ref[...])
    for o_ref, o in zip(out_refs, outs):
        o_ref[...] = o


def _rwkv_prep_tok_kernel(pr_ref, prev_ref, mu_ref, w0_ref, a0_ref, kkw_ref, kaw_ref, rkw_ref,
                          wl_ref, wg_ref, *out_refs):
    outs = _rwkv_prep_core(pr_ref[...], prev_ref[...], mu_ref[...], w0_ref[...], a0_ref[...],
                           kkw_ref[...], kaw_ref[...], rkw_ref[...], wl_ref[...], wg_ref[...])
    for k, (o_ref, o) in enumerate(zip(out_refs, outs)):
        o_ref[...] = o.T if k < N_STEP_VECS else o


def _rwkv_param_specs(index_map):
    c = RWKV_WIDTH
    shapes = [(1, RWKV_PROJ_PAD)] + [(1, c)] * 5 + [(LANES, 2 * c), (GATE_PAD, c)]
    return [pl.BlockSpec(s, index_map) for s in shapes]


def rwkv_prep_seq(pr, prev0, params, *, tm=256):
    b, t, wd = pr.shape
    c = RWKV_WIDTH
    out = jax.ShapeDtypeStruct((b, t, c), F32)
    return pl.pallas_call(
        _rwkv_prep_seq_kernel,
        grid=(b, t // tm),
        in_specs=[pl.BlockSpec((None, tm, wd), lambda i, j: (i, j, 0)),
                  pl.BlockSpec((None, 1, wd), lambda i, j: (i, 0, 0))]
        + _rwkv_param_specs(lambda i, j: (0, 0)),
        out_specs=[pl.BlockSpec((None, tm, c), lambda i, j: (i, j, 0))] * 8,
        out_shape=[out] * 8,
        scratch_shapes=[pltpu.VMEM((1, wd), F32)],
        compiler_params=_cp(("parallel", "arbitrary")),
        name="rwkv_prep_seq",
    )(pr, prev0, *params)


N_STEP_VECS = 6


def rwkv_prep_tok(pr, prev, params):
    m, wd = pr.shape
    c = RWKV_WIDTH
    shapes = [(c, m)] * N_STEP_VECS + [(m, c)] * 2
    return pl.pallas_call(
        _rwkv_prep_tok_kernel,
        grid=(1,),
        in_specs=[pl.BlockSpec((m, wd), lambda i: (0, 0))] * 2
        + _rwkv_param_specs(lambda i: (0, 0)),
        out_specs=[pl.BlockSpec(s, lambda i: (0, 0)) for s in shapes],
        out_shape=[jax.ShapeDtypeStruct(s, F32) for s in shapes],
        compiler_params=_cp(("arbitrary",)),
        name="rwkv_prep_tok",
    )(pr, prev, *params)


def _dot_tn(a, b):
    return lax.dot_general(a, b, (((0,), (0,)), ((), ())), preferred_element_type=F32)


def _rwkv_scan_kernel(r_ref, ld_ref, kp_ref, v_ref, kk_ref, b_ref, y_ref, s_out_ref, s_ref):
    @pl.when(pl.program_id(1) == 0)
    def _():
        s_ref[...] = jnp.zeros_like(s_ref)

    n = CHUNK
    ti = lax.broadcasted_iota(jnp.int32, (n, n), 0)
    si = lax.broadcasted_iota(jnp.int32, (n, n), 1)
    strict, incl = si < ti, si <= ti
    tri = jnp.where(incl, 1.0, 0.0).astype(BF16)

    ld = ld_ref[...]
    l1, l2, l3 = _split3(ld)
    lc = _dot(tri, l1) + _dot(tri, l2) + _dot(tri, l3)
    lc_end = lc[n - 1:n, :]
    e_neg = jnp.exp(-lc)
    kk, b, kp = kk_ref[...], b_ref[...], kp_ref[...]
    a_t = (-kk * jnp.exp(lc - ld)).astype(BF16)
    b_t = (b * e_neg).astype(BF16)
    k_t = (kp * e_neg).astype(BF16)
    r_t = (r_ref[...] * jnp.exp(lc)).astype(BF16)
    to_end = jnp.exp(lc_end - lc)
    b_e = (b * to_end).astype(BF16)
    k_e = (kp * to_end).astype(BF16)
    v_b = v_ref[...].astype(BF16)
    g_end = jnp.exp(lc_end)

    heads = range(RWKV_HEADS)
    sl = [slice(h * n, (h + 1) * n) for h in heads]
    gm = [_dot_nt(jnp.concatenate([a_t[:, sl[h]], r_t[:, sl[h]]], axis=0),
                  jnp.concatenate([b_t[:, sl[h]], k_t[:, sl[h]]], axis=0)) for h in heads]
    n_ab = [jnp.where(strict, gm[h][0:n, 0:n], 0.0).astype(BF16) for h in heads]
    l_ak = [jnp.where(strict, gm[h][0:n, n:2 * n], 0.0).astype(BF16) for h in heads]
    p_rb = [jnp.where(incl, gm[h][n:2 * n, 0:n], 0.0).astype(BF16) for h in heads]
    p_rk = [jnp.where(incl, gm[h][n:2 * n, n:2 * n], 0.0).astype(BF16) for h in heads]
    s0 = [s_ref[h] for h in heads]
    s0_b = [s0[h].astype(BF16) for h in heads]
    u = [_dot_nt(a_t[:, sl[h]], s0_b[h]) + _dot(l_ak[h], v_b[:, sl[h]]) for h in heads]
    pw = n_ab
    for step in range(6):
        u = [u[h] + _dot(pw[h], u[h].astype(BF16)) for h in heads]
        if step < 5:
            pw = [_dot(pw[h], pw[h]).astype(BF16) for h in heads]
    u_b = [u[h].astype(BF16) for h in heads]
    y = [_dot_nt(r_t[:, sl[h]], s0_b[h]) + _dot(p_rb[h], u_b[h]) + _dot(p_rk[h], v_b[:, sl[h]])
         for h in heads]
    s_new = [s0[h] * g_end[:, sl[h]] + _dot_tn(
        jnp.concatenate([u_b[h], v_b[:, sl[h]]], axis=0),
        jnp.concatenate([b_e[:, sl[h]], k_e[:, sl[h]]], axis=0)) for h in heads]
    for h in heads:
        y_ref[:, sl[h]] = y[h]
        s_ref[h] = s_new[h]
        s_out_ref[h] = s_new[h]


def rwkv_scan(r, ld, kp, v, kk, b):
    bsz, t, c = r.shape
    blk = pl.BlockSpec((None, CHUNK, c), lambda i, j: (i, j, 0))
    st = pl.BlockSpec((None, RWKV_HEADS, RWKV_HEAD_DIM, RWKV_HEAD_DIM), lambda i, j: (i, 0, 0, 0))
    return pl.pallas_call(
        _rwkv_scan_kernel,
        grid=(bsz, t // CHUNK),
        in_specs=[blk] * 6,
        out_specs=[blk, st],
        out_shape=[jax.ShapeDtypeStruct((bsz, t, c), F32),
                   jax.ShapeDtypeStruct((bsz, RWKV_HEADS, RWKV_HEAD_DIM, RWKV_HEAD_DIM), F32)],
        scratch_shapes=[pltpu.VMEM((RWKV_HEADS, RWKV_HEAD_DIM, RWKV_HEAD_DIM), F32)],
        compiler_params=_cp(("parallel", "arbitrary")),
        name="rwkv_scan",
    )(r, ld, kp, v, kk, b)


STEP_UNROLL = 8


def _rwkv_step_kernel(r_ref, ld_ref, kp_ref, v_ref, kk_ref, b_ref, s_ref, y_ref, s_out_ref):
    n = RWKV_HEAD_DIM
    neg_kk, decay = -kk_ref[...], jnp.exp(ld_ref[...])
    b_mat, kp_mat, r_mat = b_ref[...], kp_ref[...], r_ref[...]

    def body(i, carry):
        v0 = pl.multiple_of(i * STEP_UNROLL, STEP_UNROLL)
        v_rows = v_ref[pl.ds(v0, STEP_UNROLL), :]
        rows = range(STEP_UNROLL)
        s = [s_ref[v0 + j] for j in rows]
        sa = [jnp.sum(s[j] * neg_kk, axis=0, keepdims=True) for j in rows]
        s_new = [s[j] * decay + sa[j] * b_mat + v_rows[j:j + 1, :] * kp_mat for j in rows]
        y = [jnp.sum(s_new[j] * r_mat, axis=0, keepdims=True) for j in rows]
        for j in rows:
            s_out_ref[v0 + j] = s_new[j]
        y_ref[pl.ds(v0, STEP_UNROLL), :] = jnp.concatenate(y, axis=0)
        return carry

    lax.fori_loop(0, n // STEP_UNROLL, body, 0)


def rwkv_step(r, ld, kp, v, kk, b, state_t):
    _, nh, n, _, bsz = state_t.shape
    vec = pl.BlockSpec((n, bsz), lambda h: (h, 0))
    st = pl.BlockSpec((None, None, n, n, bsz), lambda h: (0, h, 0, 0, 0))
    return pl.pallas_call(
        _rwkv_step_kernel,
        grid=(nh,),
        in_specs=[vec] * 6 + [st],
        out_specs=[vec, st],
        out_shape=[jax.ShapeDtypeStruct((nh * n, bsz), F32),
                   jax.ShapeDtypeStruct(state_t.shape, F32)],
        compiler_params=_cp(("parallel",)),
        name="rwkv_step",
    )(r, ld, kp, v, kk, b, state_t)


def _rwkv_post_kernel(y_ref, bonus_ref, g_ref, lnw_ref, lnb_ref, o_ref, *, y_channel_major):
    gmat = _head_indicator()
    inv = 1.0 / RWKV_HEAD_DIM
    for c in range(o_ref.shape[1] // LANES):
        sl = slice(c * LANES, (c + 1) * LANES)
        y = y_ref[sl, :].T if y_channel_major else y_ref[:, sl]
        d = y - _group_sum(y, gmat) * inv
        var = _group_sum(d * d, gmat) * inv
        yn = d * lax.rsqrt(var + GN_EPS) * lnw_ref[:, sl] + lnb_ref[:, sl]
        o_ref[:, sl] = (yn + bonus_ref[:, sl]) * g_ref[:, sl]


def rwkv_post(y, bonus, g, ln_w, ln_b, *, tm, y_channel_major=False):
    m, c = bonus.shape
    blk = pl.BlockSpec((tm, c), lambda i: (i, 0))
    y_blk = pl.BlockSpec((c, tm), lambda i: (0, i)) if y_channel_major else blk
    vec = pl.BlockSpec((1, c), lambda i: (0, 0))
    return pl.pallas_call(
        functools.partial(_rwkv_post_kernel, y_channel_major=y_channel_major),
        grid=(m // tm,),
        in_specs=[y_blk, blk, blk, vec, vec],
        out_specs=blk,
        out_shape=jax.ShapeDtypeStruct((m, c), F32),
        compiler_params=_cp(("parallel",)),
        name="rwkv_post",
    )(y, bonus, g, ln_w.reshape(1, c), ln_b.reshape(1, c))


ROUTER_LANES = LANES
ROW_TILES = D_MODEL // LANES


def _rows_to_tiles(ref, x):
    rows = x.shape[0]
    for j in range(ROW_TILES):
        ref[pl.ds(j, rows, stride=ROW_TILES), :] = x[:, j * LANES:(j + 1) * LANES]


def _tiles_to_rows(ref, rows):
    return jnp.concatenate(
        [ref[pl.ds(j, rows, stride=ROW_TILES), :] for j in range(ROW_TILES)], axis=1)


def _router_kernel(ha_ref, hb_ref, lnw_ref, whi_ref, wlo_ref, bias_ref, u_ref, idx_ref, gate_ref,
                   *, steps_a):
    use_a = pl.program_id(0) < steps_a
    h = jnp.where(use_a, ha_ref[...], hb_ref[...])
    u = _rms_rows(h, lnw_ref[...])
    _rows_to_tiles(u_ref, u)
    u_hi, u_lo = _split2(u)
    w_hi = whi_ref[...]
    logits = _dot(u_hi, w_hi) + _dot(u_lo, w_hi) + _dot(u_hi, wlo_ref[...]) + bias_ref[...]
    lane = lax.broadcasted_iota(jnp.int32, logits.shape, 1)
    neg = -jnp.inf

    def first_max(x):
        m = jnp.max(x, axis=1, keepdims=True)
        return m, jnp.min(jnp.where(x == m, lane, ROUTER_LANES), axis=1, keepdims=True)

    gl = jnp.where(lane < N_EXPERT_GROUPS, logits, neg)
    g_max, g_idx = first_max(gl)
    g_gate = 1.0 / jnp.sum(jnp.exp(gl - g_max), axis=1, keepdims=True)
    lo = N_EXPERT_GROUPS + g_idx * EXPERTS_PER_GROUP
    el = jnp.where((lane >= lo) & (lane < lo + EXPERTS_PER_GROUP), logits, neg)
    v1, i1 = first_max(el)
    v2, i2 = first_max(jnp.where(lane == i1, neg, el))
    e2 = jnp.exp(v2 - v1)
    w1 = g_gate / (1.0 + e2)
    w2 = g_gate * e2 / (1.0 + e2)
    idx_ref[...] = jnp.where(lane == 0, i1 - N_EXPERT_GROUPS,
                             jnp.where(lane == 1, i2 - N_EXPERT_GROUPS, 0))
    gate_ref[...] = jnp.where(lane == 0, w1, jnp.where(lane == 1, w2, 0.0))


def moe_router(h_a, h_b, ln_w, w_hi, w_lo, bias, *, tm):
    (ma, d), mb = h_a.shape, h_b.shape[0]
    assert ma % tm == 0 and mb % tm == 0
    steps_a, steps_b = ma // tm, mb // tm
    m = ma + mb
    const = lambda r, w: pl.BlockSpec((r, w), lambda i: (0, 0))
    row = lambda w: pl.BlockSpec((tm, w), lambda i: (i, 0))
    return pl.pallas_call(
        functools.partial(_router_kernel, steps_a=steps_a),
        grid=(steps_a + steps_b,),
        in_specs=[pl.BlockSpec((tm, d), lambda i: (jnp.minimum(i, steps_a - 1), 0)),
                  pl.BlockSpec((tm, d), lambda i: (jnp.maximum(i - steps_a, 0), 0)),
                  const(1, d), const(d, ROUTER_LANES), const(d, ROUTER_LANES),
                  const(1, ROUTER_LANES)],
        out_specs=[pl.BlockSpec((tm * (d // LANES), LANES), lambda i: (i, 0)),
                   row(ROUTER_LANES), row(ROUTER_LANES)],
        out_shape=[jax.ShapeDtypeStruct((m * (d // LANES), LANES), F32),
                   jax.ShapeDtypeStruct((m, ROUTER_LANES), jnp.int32),
                   jax.ShapeDtypeStruct((m, ROUTER_LANES), F32)],
        compiler_params=_cp(("arbitrary",)),
        name="moe_router",
    )(h_a, h_b, ln_w.reshape(1, d), w_hi, w_lo, bias)


X_SLOTS = 3
Y_SLOTS = 2


W_SLOTS = 3
ROW_DMA_PRIORITY = 1


def _moe_expert_kernel(run_ref, rexp_ref, nused_ref, tok0_ref, tok1_ref, tok2_ref, dst_ref,
                       roww_ref, u_hbm, wg_hbm, wu_hbm, wd_hbm, y_hbm, xbuf, ybuf, wg_f, wu_f,
                       wd_f, wg_b, wu_b, wd_b, sem_in, sem_out, sem_w):
    i = pl.program_id(0)
    n_used, n_runs = nused_ref[0], nused_ref[1]
    tile_rows = MOE_BLOCK * ROW_TILES
    pad_base = y_hbm.shape[0] - Y_SLOTS * tile_rows
    run = run_ref[i]

    def weight_copies(k):
        e, s = rexp_ref[jnp.minimum(k, n_runs - 1)], lax.rem(k, W_SLOTS)
        return [pltpu.make_async_copy(hbm.at[e], buf.at[s], sem_w.at[s])
                for hbm, buf in ((wg_hbm, wg_f), (wu_hbm, wu_f), (wd_hbm, wd_f))]

    def gather_block(idx_ref, x_slot):
        for r in range(MOE_BLOCK):
            pltpu.make_async_copy(u_hbm.at[pl.ds(idx_ref[0, 0, r], ROW_TILES)],
                                  xbuf.at[x_slot, pl.ds(r * ROW_TILES, ROW_TILES)],
                                  sem_in.at[x_slot]).start(priority=ROW_DMA_PRIORITY)

    def scatter_rows(idx_ref, y_slot, rows):
        for r in rows:
            pltpu.make_async_copy(ybuf.at[y_slot, pl.ds(r * ROW_TILES, ROW_TILES)],
                                  y_hbm.at[pl.ds(idx_ref[0, 0, r], ROW_TILES)],
                                  sem_out.at[y_slot]).start(priority=ROW_DMA_PRIORITY)

    def gather_wait(x_slot):
        pltpu.make_async_copy(u_hbm.at[pl.ds(0, tile_rows)], xbuf.at[x_slot],
                              sem_in.at[x_slot]).wait()

    def scatter_wait(y_slot):
        pltpu.make_async_copy(ybuf.at[y_slot], y_hbm.at[pl.ds(0, tile_rows)],
                              sem_out.at[y_slot]).wait()

    @pl.when(i == 0)
    def _():
        ybuf[0] = jnp.zeros(ybuf.shape[1:], F32)
        for s in range(Y_SLOTS):
            pltpu.make_async_copy(ybuf.at[0], y_hbm.at[pl.ds(pad_base + s * tile_rows, tile_rows)],
                                  sem_out.at[s]).start()
        for k in range(W_SLOTS - 1):
            for cp in weight_copies(k):
                cp.start()
        gather_block(tok0_ref, 0)
        gather_block(tok1_ref, 1)
        for s in range(Y_SLOTS):
            scatter_wait(s)

    @pl.when(i < n_used)
    def _():
        x_slot = lax.rem(i, X_SLOTS)
        y_slot = lax.rem(i, Y_SLOTS)

        @pl.when((i == 0) | (run != run_ref[jnp.maximum(i - 1, 0)]))
        def _():
            for cp in weight_copies(run):
                cp.wait()
            w_slot = lax.rem(run, W_SLOTS)
            wg_b[...] = wg_f[w_slot].astype(BF16)
            wu_b[...] = wu_f[w_slot].astype(BF16)
            wd_b[...] = wd_f[w_slot].astype(BF16)
            for cp in weight_copies(run + W_SLOTS - 1):
                cp.start()

        gather_wait(x_slot)

        @pl.when(i >= Y_SLOTS)
        def _():
            scatter_wait(y_slot)

        x = _tiles_to_rows(xbuf.at[x_slot], MOE_BLOCK).astype(BF16)
        hg = _dot(x, wg_b[...])
        hu = _dot(x, wu_b[...])
        act = (hg * _sigmoid(hg) * hu).astype(BF16)
        y = _dot(act, wd_b[...]) * roww_ref[...]
        _rows_to_tiles(ybuf.at[y_slot], y)
        scatter_rows(dst_ref, y_slot, range(MOE_BLOCK))
        gather_block(tok2_ref, lax.rem(i + 2, X_SLOTS))

        @pl.when(i == n_used - 1)
        def _():
            scatter_wait(y_slot)

            @pl.when(i >= 1)
            def _():
                scatter_wait(1 - y_slot)

            gather_wait(lax.rem(i + 1, X_SLOTS))
            gather_wait(lax.rem(i + 2, X_SLOTS))
            for k in range(W_SLOTS - 1):
                for cp in weight_copies(n_runs + k):
                    cp.wait()


def moe_experts(u_all, row_src, row_dst, row_w, block_run, run_exp, n_used_runs, w_gate, w_up,
                w_down, n_assign):
    d, ff = w_gate.shape[1], w_gate.shape[2]
    n_blocks = row_src.shape[0]
    tile_rows = MOE_BLOCK * ROW_TILES
    smem_blk = lambda off: pl.BlockSpec(
        (1, 1, MOE_BLOCK), lambda i, *_: (jnp.clip(i + off, 0, n_blocks - 1), 0, 0),
        memory_space=pltpu.SMEM)
    hbm = pl.BlockSpec(memory_space=pl.ANY)
    grid_spec = pltpu.PrefetchScalarGridSpec(
        num_scalar_prefetch=3,
        grid=(n_blocks,),
        in_specs=[
            smem_blk(0), smem_blk(1), smem_blk(2), smem_blk(0),
            pl.BlockSpec((MOE_BLOCK, 1), lambda i, *_: (i, 0)),
            hbm, hbm, hbm, hbm,
        ],
        out_specs=hbm,
        scratch_shapes=[
            pltpu.VMEM((X_SLOTS, tile_rows, LANES), F32),
            pltpu.VMEM((Y_SLOTS, tile_rows, LANES), F32),
            pltpu.VMEM((W_SLOTS, d, ff), F32), pltpu.VMEM((W_SLOTS, d, ff), F32),
            pltpu.VMEM((W_SLOTS, ff, d), F32),
            pltpu.VMEM((d, ff), BF16), pltpu.VMEM((d, ff), BF16), pltpu.VMEM((ff, d), BF16),
            pltpu.SemaphoreType.DMA((X_SLOTS,)), pltpu.SemaphoreType.DMA((Y_SLOTS,)),
            pltpu.SemaphoreType.DMA((W_SLOTS,)),
        ],
    )
    y_rows = (n_assign + Y_SLOTS * MOE_BLOCK) * ROW_TILES
    return pl.pallas_call(
        _moe_expert_kernel,
        grid_spec=grid_spec,
        out_shape=jax.ShapeDtypeStruct((y_rows, LANES), F32),
        compiler_params=_cp(("arbitrary",)),
        name="moe_experts",
    )(block_run, run_exp, n_used_runs, row_src, row_src, row_src, row_dst, row_w, u_all,
      w_gate, w_up, w_down)


def _moe_combine_kernel(h_ref, y0_ref, y1_ref, o_ref):
    rows = h_ref.shape[0]
    o_ref[...] = h_ref[...] + (_tiles_to_rows(y0_ref, rows) + _tiles_to_rows(y1_ref, rows))


def moe_combine(h, y_slots, row_off, slot_stride, *, tm):
    m, d = h.shape
    assert row_off % tm == 0 and slot_stride % tm == 0
    off0, off1 = row_off // tm, (row_off + slot_stride) // tm
    return pl.pallas_call(
        _moe_combine_kernel,
        grid=(m // tm,),
        in_specs=[pl.BlockSpec((tm, d), lambda i: (i, 0)),
                  pl.BlockSpec((tm * ROW_TILES, LANES), lambda i: (i + off0, 0)),
                  pl.BlockSpec((tm * ROW_TILES, LANES), lambda i: (i + off1, 0))],
        out_specs=pl.BlockSpec((tm, d), lambda i: (i, 0)),
        out_shape=jax.ShapeDtypeStruct((m, d), F32),
        compiler_params=_cp(("parallel",)),
        name="moe_combine",
    )(h, y_slots, y_slots)


def moe_dispatch(e_idx, gates, slot_stride):
    m = e_idx.shape[0]
    a = m * TOP_K
    e_flat = e_idx.reshape(a)
    order = jnp.argsort(e_flat, stable=True).astype(jnp.int32)
    counts = jnp.sum(e_flat[:, None] == jnp.arange(N_EXPERTS, dtype=jnp.int32)[None, :],
                     axis=0, dtype=jnp.int32)
    pad_counts = (counts + MOE_BLOCK - 1) // MOE_BLOCK * MOE_BLOCK
    starts = jnp.cumsum(counts) - counts
    pad_ends = jnp.cumsum(pad_counts)
    pad_starts = pad_ends - pad_counts
    n_blocks = a // MOE_BLOCK + N_EXPERTS
    p = n_blocks * MOE_BLOCK
    n_used = (pad_ends[-1] // MOE_BLOCK).astype(jnp.int32)
    blk = jnp.arange(n_blocks, dtype=jnp.int32)
    blk_start = jnp.minimum(blk, n_used - 1) * MOE_BLOCK
    block_exp = jnp.minimum(jnp.searchsorted(pad_ends, blk_start, side='right'),
                            N_EXPERTS - 1).astype(jnp.int32)
    in_exp = blk * MOE_BLOCK - pad_starts[block_exp]
    row_cnt = jnp.where(blk < n_used, jnp.clip(counts[block_exp] - in_exp, 0, MOE_BLOCK), 0)
    lane = jnp.arange(MOE_BLOCK, dtype=jnp.int32)[None, :]
    valid = lane < row_cnt[:, None]
    src = jnp.clip((starts[block_exp] + in_exp)[:, None] + lane, 0, a - 1)
    assign = order[src]
    row_tok = jnp.where(valid, assign // TOP_K, 0)
    pad_dst = TOP_K * slot_stride + (blk % Y_SLOTS)[:, None] * MOE_BLOCK + lane
    row_dst = jnp.where(valid, (assign % TOP_K) * slot_stride + assign // TOP_K, pad_dst)
    row_w = jnp.where(valid, gates.reshape(a)[assign], 0.0)
    as_blocks = lambda x: (x * ROW_TILES).astype(jnp.int32).reshape(n_blocks, 1, MOE_BLOCK)
    has_rows = counts > 0
    run_exp = jnp.argsort(~has_rows, stable=True).astype(jnp.int32)
    block_run = (jnp.cumsum(has_rows) - 1)[block_exp].astype(jnp.int32)
    n_used_runs = jnp.stack([n_used, jnp.sum(has_rows, dtype=jnp.int32)])
    return (as_blocks(row_tok), as_blocks(row_dst), row_w.reshape(p, 1), block_run, run_exp,
            n_used_runs)


def rwkv_params(rw_mu, rw_w0, rw_w2, rw_a0, rw_a2, rw_g2, rw_k_k, rw_k_a, rw_r_k):
    c = RWKV_WIDTH
    mu = jnp.pad(rw_mu, (0, RWKV_PROJ_PAD - RWKV_PROJ)).reshape(1, RWKV_PROJ_PAD)
    w_lora = jnp.zeros((LANES, 2 * c), F32)
    w_lora = w_lora.at[0:DECAY_LORA, 0:c].set(rw_w2).at[DECAY_LORA:LANES, c:2 * c].set(rw_a2)
    w_gate = jnp.pad(rw_g2, ((0, GATE_PAD - GATE_LORA), (0, 0)))
    vec = lambda x: x.reshape(1, c)
    return (mu, vec(rw_w0), vec(rw_a0), vec(rw_k_k), vec(rw_k_a), vec(rw_r_k),
            w_lora.astype(BF16), w_gate.astype(BF16))


def _token_tiles(m):
    return (1024, 512) if m % 1024 == 0 else (m, m)


def _dense_front(x2d, wts, tm):
    pa = norm_matmul(x2d, wts['ln1_w'], wts['w_att'], tm=tm, tn=512)
    pr = norm_matmul(x2d, wts['ln1_w'], wts['w_rw'], tm=tm, tn=RWKV_PROJ_PAD // 3)
    return pa, pr


def _dense_back(x2d, att2d, rw2d, wts, xattn_fn, tm):
    h1 = matmul_residual([att2d, rw2d], [wts['w_out_a'], wts['w_out_r']], x2d, tm=tm, tn=512)
    qx = norm_matmul(h1, wts['ln2_w'], wts['xq_w'], tm=tm, tn=XATT_WIDTH)
    ox = xattn_fn(qx)
    return matmul_residual([ox], [wts['xo_w']], h1, tm=tm, tn=512)


def kernel(x_prompt, x_sample, cache_win_k, cache_win_v, state_wkv, state_shift, cache_mem_k, cache_mem_v, mem_prompt, ln1_w, w_in, q_norm_w, k_norm_w, attn_sinks, rw_mu, rw_w0, rw_w2, rw_a0, rw_a2, rw_g2, rw_k_k, rw_k_a, rw_r_k, rw_ln_w, rw_ln_b, w_out, ln2_w, mem_norm_w, xq_w, xkv_w, xq_norm_w, xk_norm_w, xo_w, ln3_w, router_group_w, router_group_b, router_expert_w, router_expert_b, exp_w_gate, exp_w_up, exp_w_down):
    assert w_in.shape[0] == 1, "single-layer stack"
    bp, seq, d = x_prompt.shape
    bs = x_sample.shape[0]
    mp = bp * seq
    c = RWKV_WIDTH

    router_w = jnp.concatenate(
        [router_group_w[0], router_expert_w[0],
         jnp.zeros((d, ROUTER_LANES - N_EXPERT_GROUPS - N_EXPERTS), F32)], axis=1)
    router_hi = router_w.astype(BF16)
    wts = {
        'ln1_w': ln1_w[0], 'ln2_w': ln2_w[0], 'ln3_w': ln3_w[0],
        'w_att': w_in[0][:, :ATT_PROJ].astype(BF16),
        'w_rw': jnp.pad(w_in[0][:, ATT_PROJ:],
                        ((0, 0), (0, RWKV_PROJ_PAD - RWKV_PROJ))).astype(BF16),
        'w_out_a': w_out[0][:ATT_WIDTH].astype(BF16),
        'w_out_r': w_out[0][ATT_WIDTH:].astype(BF16),
        'xq_w': xq_w[0].astype(BF16), 'xo_w': xo_w[0].astype(BF16),
        'router_hi': router_hi,
        'router_lo': (router_w - router_hi.astype(F32)).astype(BF16),
        'router_b': jnp.pad(jnp.concatenate([router_group_b[0], router_expert_b[0]]),
                            (0, ROUTER_LANES - N_EXPERT_GROUPS - N_EXPERTS)).reshape(1, -1),
    }
    rw_par = rwkv_params(rw_mu[0], rw_w0[0], rw_w2[0], rw_a0[0], rw_a2[0], rw_g2[0],
                         rw_k_k[0], rw_k_a[0], rw_r_k[0])

    tm_p, te_p = _token_tiles(mp)
    xp = x_prompt.reshape(mp, d)
    pa, pr = _dense_front(xp, wts, tm_p)
    pa3 = pa.reshape(bp, seq, ATT_PROJ)
    pr3 = pr.reshape(bp, seq, RWKV_PROJ_PAD)
    tabs_p = rope_tables(jnp.arange(seq, dtype=jnp.int32))
    att_p, kn_p = swa_prompt(pa3, tabs_p, q_norm_w[0], k_norm_w[0], attn_sinks[0])
    prep = rwkv_prep_seq(pr3, jnp.zeros((bp, 1, RWKV_PROJ_PAD), F32), rw_par)
    r, ld, kp, v, kk, b, bonus, g = prep
    y_p, wkv_p = rwkv_scan(r, ld, kp, v, kk, b)
    rw_p = rwkv_post(y_p.reshape(mp, c), bonus.reshape(mp, c), g.reshape(mp, c),
                     rw_ln_w[0], rw_ln_b[0], tm=te_p)

    n_mem = mem_prompt.shape[1]
    kv_mem = norm_matmul(mem_prompt.reshape(bp * n_mem, d), mem_norm_w[0],
                         xkv_w[0].astype(BF16), tm=bp * n_mem, tn=512)
    mem_k = head_rms(kv_mem[:, :XATT_WIDTH], xk_norm_w[0])
    mem_v = kv_mem[:, XATT_WIDTH:]
    mem_k3 = mem_k.reshape(bp, n_mem, XATT_WIDTH)
    mem_v3 = mem_v.reshape(bp, n_mem, XATT_WIDTH)

    def xattn_p(qx):
        return xattn_prompt(qx.reshape(bp, seq, XATT_WIDTH), mem_k3, mem_v3,
                            xq_norm_w[0]).reshape(mp, XATT_WIDTH)

    h2_p = _dense_back(xp, att_p.reshape(mp, ATT_WIDTH), rw_p, wts, xattn_p, tm_p)

    tm_s, te_s = _token_tiles(bs)
    xs = x_sample.reshape(bs, d)
    sa, sr = _dense_front(xs, wts, tm_s)
    tabs_s = rope_tables(PAST_LEN + jnp.arange(1, dtype=jnp.int32))
    qk_w = jnp.concatenate([jnp.tile(q_norm_w[0], ATT_HEADS),
                            jnp.tile(k_norm_w[0], ATT_KV_HEADS)]).reshape(1, -1)
    qk = qk_norm_rope(sa[:, :ATT_WIDTH + KV_WIDTH], qk_w, tabs_s)
    nbuf = cache_win_k.shape[2]

    def feature_major(cache):
        return jnp.transpose(cache, (0, 1, 3, 4, 2)).reshape(bs, KV_WIDTH, nbuf)

    def position_major(win):
        return jnp.transpose(win.reshape(1, bs, ATT_KV_HEADS, HEAD_DIM, nbuf), (0, 1, 4, 2, 3))

    att_s, win_k, win_v = swa_decode(
        qk[:, :ATT_WIDTH].reshape(bs, ATT_HEADS, HEAD_DIM),
        qk[:, ATT_WIDTH:].reshape(bs, 1, KV_WIDTH),
        sa[:, ATT_WIDTH + KV_WIDTH:].reshape(bs, 1, KV_WIDTH),
        feature_major(cache_win_k), feature_major(cache_win_v), attn_sinks[0])
    shift_prev = jnp.pad(state_shift[0], ((0, 0), (0, RWKV_PROJ_PAD - RWKV_PROJ)))
    r, ld, kp, v, kk, b, bonus, g = rwkv_prep_tok(sr, shift_prev, rw_par)
    y_s, wkv_s = rwkv_step(r, ld, kp, v, kk, b, jnp.transpose(state_wkv, (0, 2, 3, 4, 1)))
    wkv_s = jnp.transpose(wkv_s, (0, 4, 1, 2, 3))
    rw_s = rwkv_post(y_s, bonus, g, rw_ln_w[0], rw_ln_b[0], tm=te_s, y_channel_major=True)
    def xattn_s(qx):
        q_pad = jnp.pad(qx.reshape(bs, XATT_HEADS, XATT_HEAD_DIM), ((0, 0), (0, 4), (0, 0)))
        o = xattn_decode(q_pad, cache_mem_k, cache_mem_v, xq_norm_w[0])
        return o[:, :XATT_HEADS].reshape(bs, XATT_WIDTH)

    h2_s = _dense_back(xs, att_s.reshape(bs, ATT_WIDTH), rw_s, wts, xattn_s, tm_s)

    m_all = mp + bs
    slot_stride = m_all
    tc = math.gcd(mp, bs, 512)
    u_all, idx_all, gate_all = moe_router(h2_p, h2_s, wts['ln3_w'], wts['router_hi'],
                                          wts['router_lo'], wts['router_b'], tm=tc)
    row_src, row_dst, row_w, block_run, run_exp, n_used_runs = moe_dispatch(
        idx_all[:, :TOP_K], gate_all[:, :TOP_K], slot_stride)
    y_slots = moe_experts(u_all, row_src, row_dst, row_w, block_run, run_exp, n_used_runs,
                          exp_w_gate[0], exp_w_up[0], exp_w_down[0], TOP_K * slot_stride)
    out_p = moe_combine(h2_p, y_slots, 0, slot_stride, tm=tc)
    out_s = moe_combine(h2_s, y_slots, mp, slot_stride, tm=tc)

    win = min(WINDOW, seq)
    kv_shape = (1, bp, win, ATT_KV_HEADS, HEAD_DIM)
    return (
        out_p.reshape(bp, seq, d),
        out_s.reshape(bs, 1, d),
        kn_p[:, seq - win:].reshape(kv_shape),
        pa3[:, seq - win:, ATT_WIDTH + KV_WIDTH:].reshape(kv_shape),
        wkv_p[None],
        pr3[:, seq - 1, :RWKV_PROJ][None],
        mem_k3.reshape(1, bp, n_mem, XATT_HEADS, XATT_HEAD_DIM),
        mem_v3.reshape(1, bp, n_mem, XATT_HEADS, XATT_HEAD_DIM),
        position_major(win_k),
        position_major(win_v),
        wkv_s,
        sr[:, :RWKV_PROJ].reshape(1, bs, RWKV_PROJ),
    )
```

```python
import functools
import math

import jax
import jax.numpy as jnp
from jax import lax
from jax.experimental import pallas as pl
from jax.experimental.pallas import tpu as pltpu

F32 = jnp.float32
BF16 = jnp.bfloat16

D_MODEL = 2048
HEAD_DIM = 64
ATT_HEADS = 16
ATT_KV_HEADS = 4
ATT_GROUP = ATT_HEADS // ATT_KV_HEADS
ATT_WIDTH = ATT_HEADS * HEAD_DIM
KV_WIDTH = ATT_KV_HEADS * HEAD_DIM
ATT_PROJ = ATT_WIDTH + 2 * KV_WIDTH
WINDOW = 128
ATT_SCALE = HEAD_DIM ** -0.5
ROPE_THETA = 500000.0
ROT_DIM = HEAD_DIM // 4
PAST_LEN = 16384

RWKV_WIDTH = 1024
RWKV_HEAD_DIM = 64
RWKV_HEADS = 16
DECAY_LORA = 64
AAA_LORA = 64
GATE_LORA = 160
RWKV_PROJ = 3 * RWKV_WIDTH + DECAY_LORA + AAA_LORA + GATE_LORA
RWKV_PROJ_PAD = 3456

N_MEM = 256
XATT_HEADS = 4
XATT_HEAD_DIM = 128
XATT_WIDTH = XATT_HEADS * XATT_HEAD_DIM

N_EXPERT_GROUPS = 8
EXPERTS_PER_GROUP = 8
N_EXPERTS = 64
TOP_K = 2
EXPERT_FF = D_MODEL // 4
MOE_BLOCK = 256

RMS_EPS = 1e-6
GN_EPS = 64e-5

LANES = 128
CHUNK = 64
VMEM_LIMIT = 56 * 1024 * 1024
MOE_VMEM_LIMIT = 60 * 1024 * 1024


def _cp(sem, vmem=VMEM_LIMIT):
    return pltpu.CompilerParams(dimension_semantics=sem, vmem_limit_bytes=vmem)


def _rms_rows(x, w):
    ms = jnp.mean(x * x, axis=-1, keepdims=True)
    return x * lax.rsqrt(ms + RMS_EPS) * w


def _split2(x):
    hi = x.astype(BF16)
    lo = (x - hi.astype(F32)).astype(BF16)
    return hi, lo


def _split3(x):
    h1 = x.astype(BF16)
    r1 = x - h1.astype(F32)
    h2 = r1.astype(BF16)
    h3 = (r1 - h2.astype(F32)).astype(BF16)
    return h1, h2, h3


def _dot(a, b):
    return jnp.dot(a, b, preferred_element_type=F32)


def _dot_nt(a, b):
    return lax.dot_general(a, b, (((1,), (1,)), ((), ())), preferred_element_type=F32)


def _group_sum(x, gmat):
    hi, lo = _split2(x)
    return _dot(hi, gmat) + _dot(lo, gmat)


def _head_indicator():
    r = lax.broadcasted_iota(jnp.int32, (LANES, LANES), 0) // HEAD_DIM
    c = lax.broadcasted_iota(jnp.int32, (LANES, LANES), 1) // HEAD_DIM
    return jnp.where(r == c, 1.0, 0.0).astype(BF16)


def _norm_mm_kernel(x_ref, lnw_ref, w_ref, o_ref, xn_ref):
    @pl.when(pl.program_id(1) == 0)
    def _():
        xn_ref[...] = _rms_rows(x_ref[...], lnw_ref[...]).astype(BF16)

    o_ref[...] = _dot(xn_ref[...], w_ref[...])


def norm_matmul(x, ln_w, w_bf16, *, tm, tn):
    m, k = x.shape
    n = w_bf16.shape[1]
    assert m % tm == 0 and n % tn == 0
    return pl.pallas_call(
        _norm_mm_kernel,
        grid=(m // tm, n // tn),
        in_specs=[
            pl.BlockSpec((tm, k), lambda i, j: (i, 0)),
            pl.BlockSpec((1, k), lambda i, j: (0, 0)),
            pl.BlockSpec((k, tn), lambda i, j: (0, j)),
        ],
        out_specs=pl.BlockSpec((tm, tn), lambda i, j: (i, j)),
        out_shape=jax.ShapeDtypeStruct((m, n), F32),
        scratch_shapes=[pltpu.VMEM((tm, k), BF16)],
        compiler_params=_cp(("parallel", "arbitrary")),
        name="norm_matmul",
    )(x, ln_w.reshape(1, k), w_bf16)


def _mm_res_kernel(*refs, n_lhs):
    a_refs = refs[:n_lhs]
    w_refs = refs[n_lhs:2 * n_lhs]
    res_ref = refs[2 * n_lhs]
    o_ref = refs[2 * n_lhs + 1]
    acc = res_ref[...]
    for a_ref, w_ref in zip(a_refs, w_refs):
        acc = acc + _dot(a_ref[...].astype(BF16), w_ref[...])
    o_ref[...] = acc


def matmul_residual(lhs_list, w_list, res, *, tm, tn):
    m, n = res.shape
    n_lhs = len(lhs_list)
    assert m % tm == 0 and n % tn == 0
    in_specs = [pl.BlockSpec((tm, a.shape[1]), lambda i, j: (i, 0)) for a in lhs_list]
    in_specs += [pl.BlockSpec((w.shape[0], tn), lambda i, j: (0, j)) for w in w_list]
    in_specs += [pl.BlockSpec((tm, tn), lambda i, j: (i, j))]
    return pl.pallas_call(
        functools.partial(_mm_res_kernel, n_lhs=n_lhs),
        grid=(m // tm, n // tn),
        in_specs=in_specs,
        out_specs=pl.BlockSpec((tm, tn), lambda i, j: (i, j)),
        out_shape=jax.ShapeDtypeStruct((m, n), F32),
        compiler_params=_cp(("parallel", "arbitrary")),
        name="matmul_residual",
    )(*lhs_list, *w_list, res)


def rope_tables(pos):
    half = ROT_DIM // 2
    inv = ROPE_THETA ** (-jnp.arange(half, dtype=F32) * 2.0 / ROT_DIM)
    ang = pos.astype(F32)[:, None] * inv[None, :]
    cos, sin = jnp.cos(ang), jnp.sin(ang)
    t = pos.shape[0]
    ones = jnp.ones((t, HEAD_DIM - ROT_DIM), F32)
    zeros = jnp.zeros((t, HEAD_DIM - ROT_DIM), F32)
    z8 = jnp.zeros((t, half), F32)
    cos_t = jnp.concatenate([cos, cos, ones], axis=1)
    sin_a = jnp.concatenate([z8, sin, zeros], axis=1)
    sin_b = jnp.concatenate([-sin, z8, zeros], axis=1)
    return tuple(jnp.concatenate([a, a], axis=1) for a in (cos_t, sin_a, sin_b))


def _norm_rope_chunk(x, w, cos_t, sin_a, sin_b, gmat):
    ms = _group_sum(x * x, gmat) * (1.0 / HEAD_DIM)
    xn = x * lax.rsqrt(ms + RMS_EPS) * w
    half = ROT_DIM // 2
    return (xn * cos_t + pltpu.roll(xn, half, axis=1) * sin_a
            + pltpu.roll(xn, LANES - half, axis=1) * sin_b)


def _norm_rope(x, w, tabs, gmat):
    chunks = [
        _norm_rope_chunk(x[:, c * LANES:(c + 1) * LANES], w, *tabs, gmat)
        for c in range(x.shape[1] // LANES)
    ]
    return chunks[0] if len(chunks) == 1 else jnp.concatenate(chunks, axis=1)


def _sink_softmax(s, sink):
    m = jnp.maximum(jnp.max(s, axis=-1, keepdims=True), sink)
    e = jnp.exp(s - m)
    return e / (jnp.sum(e, axis=-1, keepdims=True) + jnp.exp(sink - m))


def _swa_prompt_kernel(q_ref, kc_ref, vc_ref, kp_ref, vp_ref, cc_ref, sac_ref, sbc_ref,
                       cp_ref, sap_ref, sbp_ref, qw_ref, kw_ref, sink_ref, o_ref, kn_ref):
    n = pl.program_id(1)
    blk = q_ref.shape[0]
    gmat = _head_indicator()
    tabs_c = (cc_ref[...], sac_ref[...], sbc_ref[...])
    tabs_p = (cp_ref[...], sap_ref[...], sbp_ref[...])
    q = _norm_rope(q_ref[...], qw_ref[...], tabs_c, gmat)
    k_cur = _norm_rope(kc_ref[...], kw_ref[...], tabs_c, gmat)
    k_prev = _norm_rope(kp_ref[...], kw_ref[...], tabs_p, gmat)
    kn_ref[...] = k_cur
    k_all = jnp.concatenate([k_prev, k_cur], axis=0).astype(BF16)
    v_all = jnp.concatenate([vp_ref[...], vc_ref[...]], axis=0).astype(BF16)

    qi = lax.broadcasted_iota(jnp.int32, (blk, 2 * blk), 0) + blk
    si = lax.broadcasted_iota(jnp.int32, (blk, 2 * blk), 1)
    rel = qi - si
    valid = (rel >= 0) & (rel <= WINDOW) & ((n > 0) | (si >= blk))

    groups = range(ATT_KV_HEADS)
    lanes = [slice(kv * HEAD_DIM, (kv + 1) * HEAD_DIM) for kv in groups]
    heads = [[kv * ATT_GROUP + g for g in range(ATT_GROUP)] for kv in groups]
    q_g = [jnp.concatenate([q[:, h * HEAD_DIM:(h + 1) * HEAD_DIM] for h in heads[kv]],
                           axis=0).astype(BF16) for kv in groups]
    s = [_dot_nt(q_g[kv], k_all[:, lanes[kv]]) * ATT_SCALE for kv in groups]
    p = [jnp.concatenate(
        [_sink_softmax(jnp.where(valid, s[kv][g * blk:(g + 1) * blk], -jnp.inf), sink_ref[h])
         for g, h in enumerate(heads[kv])], axis=0).astype(BF16) for kv in groups]
    o = [_dot(p[kv], v_all[:, lanes[kv]]) for kv in groups]
    for kv in groups:
        for g, h in enumerate(heads[kv]):
            o_ref[:, h * HEAD_DIM:(h + 1) * HEAD_DIM] = o[kv][g * blk:(g + 1) * blk]


def swa_prompt(pa, tabs, q_norm_w, k_norm_w, sinks):
    b, t, _ = pa.shape
    blk = WINDOW
    nb = t // blk
    qb, kb, vb = 0, ATT_WIDTH // KV_WIDTH, ATT_WIDTH // KV_WIDTH + 1
    cur = lambda i, n, *_: (i, n, 0)
    tab_cur = pl.BlockSpec((blk, LANES), lambda i, n: (n, 0))
    tab_prev = pl.BlockSpec((blk, LANES), lambda i, n: (jnp.maximum(n - 1, 0), 0))
    qw = jnp.tile(q_norm_w.reshape(1, HEAD_DIM), (1, 2))
    kw = jnp.tile(k_norm_w.reshape(1, HEAD_DIM), (1, 2))
    return pl.pallas_call(
        _swa_prompt_kernel,
        grid=(b, nb),
        in_specs=[
            pl.BlockSpec((None, blk, ATT_WIDTH), lambda i, n: (i, n, qb)),
            pl.BlockSpec((None, blk, KV_WIDTH), lambda i, n: (i, n, kb)),
            pl.BlockSpec((None, blk, KV_WIDTH), lambda i, n: (i, n, vb)),
            pl.BlockSpec((None, blk, KV_WIDTH), lambda i, n: (i, jnp.maximum(n - 1, 0), kb)),
            pl.BlockSpec((None, blk, KV_WIDTH), lambda i, n: (i, jnp.maximum(n - 1, 0), vb)),
            tab_cur, tab_cur, tab_cur, tab_prev, tab_prev, tab_prev,
            pl.BlockSpec((1, LANES), lambda i, n: (0, 0)),
            pl.BlockSpec((1, LANES), lambda i, n: (0, 0)),
            pl.BlockSpec(memory_space=pltpu.SMEM),
        ],
        out_specs=[
            pl.BlockSpec((None, blk, ATT_WIDTH), cur),
            pl.BlockSpec((None, blk, KV_WIDTH), cur),
        ],
        out_shape=[
            jax.ShapeDtypeStruct((b, t, ATT_WIDTH), F32),
            jax.ShapeDtypeStruct((b, t, KV_WIDTH), F32),
        ],
        compiler_params=_cp(("parallel", "arbitrary")),
        name="swa_prompt",
    )(pa, pa, pa, pa, pa, *tabs, *tabs, qw, kw, sinks)


def _qk_norm_rope_kernel(x_ref, w_ref, c_ref, sa_ref, sb_ref, o_ref):
    gmat = _head_indicator()
    tabs = (c_ref[...], sa_ref[...], sb_ref[...])
    for c in range(x_ref.shape[1] // LANES):
        sl = slice(c * LANES, (c + 1) * LANES)
        o_ref[:, sl] = _norm_rope_chunk(x_ref[:, sl], w_ref[:, sl], *tabs, gmat)


def qk_norm_rope(x, w_row, tabs):
    m, w = x.shape
    full = lambda *shape: pl.BlockSpec(shape, lambda: (0,) * len(shape))
    return pl.pallas_call(
        _qk_norm_rope_kernel,
        in_specs=[full(m, w), full(1, w), full(1, LANES), full(1, LANES), full(1, LANES)],
        out_specs=full(m, w),
        out_shape=jax.ShapeDtypeStruct((m, w), F32),
        name="qk_norm_rope",
    )(x, w_row, *tabs)


def _swa_decode_kernel(q_ref, kn_ref, vn_ref, knt_ref, vnt_ref, ck_ref, cv_ref, sink_ref, o_ref,
                       kw_ref, vw_ref):
    bb = q_ref.shape[0]
    nbuf = ck_ref.shape[2]
    row_kv = lax.broadcasted_iota(jnp.int32, (ATT_HEADS, KV_WIDTH), 0) // ATT_GROUP
    lane_kv = lax.broadcasted_iota(jnp.int32, (ATT_HEADS, KV_WIDTH), 1) // HEAD_DIM
    own = row_kv == lane_kv
    sink = sink_ref[...]
    seq = lax.broadcasted_iota(jnp.int32, knt_ref.shape, 1)
    pos = lax.broadcasted_iota(jnp.int32, (KV_WIDTH, nbuf), 1)
    b0 = pl.program_id(0) * bb
    for b in range(bb):
        q2 = q_ref[b]
        q_exp = jnp.where(own, jnp.concatenate([q2] * ATT_KV_HEADS, axis=1), 0.0)
        k_new, v_new = kn_ref[b], vn_ref[b]
        k_buf, v_buf = ck_ref[b], cv_ref[b]
        s_buf = _dot(q_exp.astype(BF16), k_buf.astype(BF16)) * ATT_SCALE
        s_new = jnp.sum(q_exp * k_new, axis=-1, keepdims=True) * ATT_SCALE
        m = jnp.maximum(jnp.maximum(jnp.max(s_buf, axis=-1, keepdims=True), s_new), sink)
        e_buf = jnp.exp(s_buf - m)
        e_new = jnp.exp(s_new - m)
        inv = 1.0 / (jnp.sum(e_buf, axis=-1, keepdims=True) + e_new + jnp.exp(sink - m))
        o = _dot_nt((e_buf * inv).astype(BF16), v_buf.astype(BF16)) + (e_new * inv) * v_new
        o = jnp.where(own, o, 0.0)
        o_ref[b] = (o[:, 0:HEAD_DIM] + o[:, HEAD_DIM:2 * HEAD_DIM]
                    + o[:, 2 * HEAD_DIM:3 * HEAD_DIM] + o[:, 3 * HEAD_DIM:4 * HEAD_DIM])
        k_col = jnp.sum(jnp.where(seq == b0 + b, knt_ref[...], 0.0), axis=1, keepdims=True)
        v_col = jnp.sum(jnp.where(seq == b0 + b, vnt_ref[...], 0.0), axis=1, keepdims=True)
        kw_ref[b] = jnp.where(pos == nbuf - 1, k_col, pltpu.roll(k_buf, nbuf - 1, axis=1))
        vw_ref[b] = jnp.where(pos == nbuf - 1, v_col, pltpu.roll(v_buf, nbuf - 1, axis=1))


def swa_decode(q, k_new, v_new, cache_kt, cache_vt, sinks, *, bb=8):
    b, _, nbuf = cache_kt.shape
    blk3 = lambda s1, s2: pl.BlockSpec((bb, s1, s2), lambda i: (i, 0, 0))
    whole = pl.BlockSpec((KV_WIDTH, b), lambda i: (0, 0))
    return pl.pallas_call(
        _swa_decode_kernel,
        grid=(b // bb,),
        in_specs=[
            blk3(ATT_HEADS, HEAD_DIM), blk3(1, KV_WIDTH), blk3(1, KV_WIDTH), whole, whole,
            blk3(KV_WIDTH, nbuf), blk3(KV_WIDTH, nbuf),
            pl.BlockSpec((ATT_HEADS, 1), lambda i: (0, 0)),
        ],
        out_specs=[blk3(ATT_HEADS, HEAD_DIM), blk3(KV_WIDTH, nbuf), blk3(KV_WIDTH, nbuf)],
        out_shape=[
            jax.ShapeDtypeStruct((b, ATT_HEADS, HEAD_DIM), F32),
            jax.ShapeDtypeStruct((b, KV_WIDTH, nbuf), F32),
            jax.ShapeDtypeStruct((b, KV_WIDTH, nbuf), F32),
        ],
        compiler_params=_cp(("parallel",)),
        name="swa_decode",
    )(q, k_new, v_new, k_new.reshape(b, KV_WIDTH).T, v_new.reshape(b, KV_WIDTH).T,
      cache_kt, cache_vt, sinks.reshape(ATT_HEADS, 1))


def _head_rms_kernel(x_ref, w_ref, o_ref):
    for h in range(x_ref.shape[1] // XATT_HEAD_DIM):
        sl = slice(h * XATT_HEAD_DIM, (h + 1) * XATT_HEAD_DIM)
        o_ref[:, sl] = _rms_rows(x_ref[:, sl], w_ref[...])


def head_rms(x, w):
    m, wd = x.shape
    return pl.pallas_call(
        _head_rms_kernel,
        in_specs=[pl.BlockSpec((m, wd), lambda: (0, 0)),
                  pl.BlockSpec((1, XATT_HEAD_DIM), lambda: (0, 0))],
        out_specs=pl.BlockSpec((m, wd), lambda: (0, 0)),
        out_shape=jax.ShapeDtypeStruct((m, wd), F32),
        name="head_rms",
    )(x, w.reshape(1, XATT_HEAD_DIM))


def _xattn_prompt_kernel(q_ref, k_ref, v_ref, w_ref, o_ref):
    scale = 1.0 / math.sqrt(XATT_HEAD_DIM)
    for h in range(XATT_HEADS):
        sl = slice(h * XATT_HEAD_DIM, (h + 1) * XATT_HEAD_DIM)
        qn = _rms_rows(q_ref[:, sl], w_ref[...]).astype(BF16)
        s = _dot_nt(qn, k_ref[:, sl].astype(BF16)) * scale
        e = jnp.exp(s - jnp.max(s, axis=-1, keepdims=True))
        p = e / jnp.sum(e, axis=-1, keepdims=True)
        o_ref[:, sl] = _dot(p.astype(BF16), v_ref[:, sl].astype(BF16))


def xattn_prompt(q, mem_k, mem_v, xq_norm_w, *, tq=512):
    b, t, w = q.shape
    n_mem = mem_k.shape[1]
    return pl.pallas_call(
        _xattn_prompt_kernel,
        grid=(b, t // tq),
        in_specs=[
            pl.BlockSpec((None, tq, w), lambda i, j: (i, j, 0)),
            pl.BlockSpec((None, n_mem, w), lambda i, j: (i, 0, 0)),
            pl.BlockSpec((None, n_mem, w), lambda i, j: (i, 0, 0)),
            pl.BlockSpec((1, XATT_HEAD_DIM), lambda i, j: (0, 0)),
        ],
        out_specs=pl.BlockSpec((None, tq, w), lambda i, j: (i, j, 0)),
        out_shape=jax.ShapeDtypeStruct((b, t, w), F32),
        compiler_params=_cp(("parallel", "arbitrary")),
        name="xattn_prompt",
    )(q, mem_k, mem_v, xq_norm_w.reshape(1, XATT_HEAD_DIM))


def _xattn_decode_kernel(q_ref, k_ref, v_ref, w_ref, o_ref):
    bb, rows, _ = q_ref.shape
    n_keys = k_ref.shape[1]
    scale = 1.0 / math.sqrt(XATT_HEAD_DIM)
    own = (lax.broadcasted_iota(jnp.int32, (rows, n_keys), 1) % XATT_HEADS
           == lax.broadcasted_iota(jnp.int32, (rows, n_keys), 0) % XATT_HEADS)
    seqs = range(bb)
    qn = [_rms_rows(q_ref[b], w_ref[...]).astype(BF16) for b in seqs]
    s = [jnp.where(own, _dot_nt(qn[b], k_ref[b].astype(BF16)) * scale, -jnp.inf) for b in seqs]
    e = [jnp.exp(s[b] - jnp.max(s[b], axis=-1, keepdims=True)) for b in seqs]
    p = [(e[b] / jnp.sum(e[b], axis=-1, keepdims=True)).astype(BF16) for b in seqs]
    for b in seqs:
        o_ref[b] = _dot(p[b], v_ref[b].astype(BF16))


def xattn_decode(q_pad, mem_k, mem_v, xq_norm_w, *, bb=8):
    b, rows, _ = q_pad.shape
    n_keys = mem_k.shape[1]
    kv = pl.BlockSpec((bb, n_keys, XATT_HEAD_DIM), lambda i: (i, 0, 0))
    return pl.pallas_call(
        _xattn_decode_kernel,
        grid=(b // bb,),
        in_specs=[pl.BlockSpec((bb, rows, XATT_HEAD_DIM), lambda i: (i, 0, 0)), kv, kv,
                  pl.BlockSpec((1, XATT_HEAD_DIM), lambda i: (0, 0))],
        out_specs=pl.BlockSpec((bb, rows, XATT_HEAD_DIM), lambda i: (i, 0, 0)),
        out_shape=jax.ShapeDtypeStruct((b, rows, XATT_HEAD_DIM), F32),
        compiler_params=_cp(("parallel",)),
        name="xattn_decode",
    )(q_pad, mem_k, mem_v, xq_norm_w.reshape(1, XATT_HEAD_DIM))


LORA_OFF = 3 * RWKV_WIDTH
GATE_OFF = LORA_OFF + DECAY_LORA + AAA_LORA
GATE_PAD = RWKV_PROJ_PAD - GATE_OFF


def _sigmoid(x):
    return 1.0 / (1.0 + jnp.exp(-x))


def _per_chunk(fn, *arrays):
    w = arrays[0].shape[1]
    outs = [fn(*(a[:, c * LANES:(c + 1) * LANES] for a in arrays)) for c in range(w // LANES)]
    return jnp.concatenate(outs, axis=1)


def _rwkv_prep_core(pr, prev, mu, w0, a0, kk_w, ka_w, rk_w, w_lora, w_gate):
    c = RWKV_WIDTH
    gmat = _head_indicator()
    xm = pr + (prev - pr) * mu
    r, k, v = xm[:, 0:c], xm[:, c:2 * c], xm[:, 2 * c:3 * c]
    lora = xm[:, LORA_OFF:LORA_OFF + LANES]
    lane = lax.broadcasted_iota(jnp.int32, lora.shape, 1)
    lora_in = jnp.where(lane < DECAY_LORA, jnp.tanh(lora), lora)
    wa = _dot(lora_in.astype(BF16), w_lora)
    z = -(w0 + wa[:, 0:c])
    softplus = jnp.maximum(z, 0.0) + jnp.log(1.0 + jnp.exp(-jnp.abs(z)))
    log_decay = -jnp.exp(-softplus - 0.5)
    a = _sigmoid(a0 + wa[:, c:2 * c])
    g = _dot(_sigmoid(xm[:, GATE_OFF:GATE_OFF + GATE_PAD]).astype(BF16), w_gate)
    kk = k * kk_w
    norm = jnp.sqrt(_per_chunk(lambda t: _group_sum(t * t, gmat), kk))
    kk = kk / jnp.maximum(norm, 1e-12)
    kp = k * (1.0 + (a - 1.0) * ka_w)
    bonus = _per_chunk(lambda t: _group_sum(t, gmat), r * kp * rk_w) * v
    return r, log_decay, kp, v, kk, kk * a, bonus, g


def _rwkv_prep_seq_kernel(pr_ref, prev0_ref, mu_ref, w0_ref, a0_ref, kkw_ref, kaw_ref, rkw_ref,
                          wl_ref, wg_ref, *refs):
    out_refs, last_ref = refs[:-1], refs[-1]

    @pl.when(pl.program_id(1) == 0)
    def _():
        last_ref[...] = prev0_ref[...]

    pr = pr_ref[...]
    rows = pr.shape[0]
    row = lax.broadcasted_iota(jnp.int32, (rows, 1), 0)
    prev = jnp.where(row == 0, last_ref[...], pltpu.roll(pr, 1, axis=0))
    last_ref[...] = pr[rows - 1:rows, :]
    outs = _rwkv_prep_core(pr, prev, mu_ref[...], w0_ref[...], a0_ref[...], kkw_ref[...],
                           kaw_ref[...], rkw_ref[...], wl_ref[...], wg_ref[...])
    for o_ref, o in zip(out_refs, outs):
        o_ref[...] = o


def _rwkv_prep_tok_kernel(pr_ref, prev_ref, mu_ref, w0_ref, a0_ref, kkw_ref, kaw_ref, rkw_ref,
                          wl_ref, wg_ref, *out_refs):
    outs = _rwkv_prep_core(pr_ref[...], prev_ref[...], mu_ref[...], w0_ref[...], a0_ref[...],
                           kkw_ref[...], kaw_ref[...], rkw_ref[...], wl_ref[...], wg_ref[...])
    for k, (o_ref, o) in enumerate(zip(out_refs, outs)):
        o_ref[...] = o.T if k < N_STEP_VECS else o


def _rwkv_param_specs(index_map):
    c = RWKV_WIDTH
    shapes = [(1, RWKV_PROJ_PAD)] + [(1, c)] * 5 + [(LANES, 2 * c), (GATE_PAD, c)]
    return [pl.BlockSpec(s, index_map) for s in shapes]


def rwkv_prep_seq(pr, prev0, params, *, tm=256):
    b, t, wd = pr.shape
    c = RWKV_WIDTH
    out = jax.ShapeDtypeStruct((b, t, c), F32)
    return pl.pallas_call(
        _rwkv_prep_seq_kernel,
        grid=(b, t // tm),
        in_specs=[pl.BlockSpec((None, tm, wd), lambda i, j: (i, j, 0)),
                  pl.BlockSpec((None, 1, wd), lambda i, j: (i, 0, 0))]
        + _rwkv_param_specs(lambda i, j: (0, 0)),
        out_specs=[pl.BlockSpec((None, tm, c), lambda i, j: (i, j, 0))] * 8,
        out_shape=[out] * 8,
        scratch_shapes=[pltpu.VMEM((1, wd), F32)],
        compiler_params=_cp(("parallel", "arbitrary")),
        name="rwkv_prep_seq",
    )(pr, prev0, *params)


N_STEP_VECS = 6


def rwkv_prep_tok(pr, prev, params):
    m, wd = pr.shape
    c = RWKV_WIDTH
    shapes = [(c, m)] * N_STEP_VECS + [(m, c)] * 2
    return pl.pallas_call(
        _rwkv_prep_tok_kernel,
        grid=(1,),
        in_specs=[pl.BlockSpec((m, wd), lambda i: (0, 0))] * 2
        + _rwkv_param_specs(lambda i: (0, 0)),
        out_specs=[pl.BlockSpec(s, lambda i: (0, 0)) for s in shapes],
        out_shape=[jax.ShapeDtypeStruct(s, F32) for s in shapes],
        compiler_params=_cp(("arbitrary",)),
        name="rwkv_prep_tok",
    )(pr, prev, *params)


def _dot_tn(a, b):
    return lax.dot_general(a, b, (((0,), (0,)), ((), ())), preferred_element_type=F32)


def _rwkv_scan_kernel(r_ref, ld_ref, kp_ref, v_ref, kk_ref, b_ref, y_ref, s_out_ref, s_ref):
    @pl.when(pl.program_id(1) == 0)
    def _():
        s_ref[...] = jnp.zeros_like(s_ref)

    n = CHUNK
    ti = lax.broadcasted_iota(jnp.int32, (n, n), 0)
    si = lax.broadcasted_iota(jnp.int32, (n, n), 1)
    strict, incl = si < ti, si <= ti
    tri = jnp.where(incl, 1.0, 0.0).astype(BF16)

    ld = ld_ref[...]
    l1, l2, l3 = _split3(ld)
    lc = _dot(tri, l1) + _dot(tri, l2) + _dot(tri, l3)
    lc_end = lc[n - 1:n, :]
    e_neg = jnp.exp(-lc)
    kk, b, kp = kk_ref[...], b_ref[...], kp_ref[...]
    a_t = (-kk * jnp.exp(lc - ld)).astype(BF16)
    b_t = (b * e_neg).astype(BF16)
    k_t = (kp * e_neg).astype(BF16)
    r_t = (r_ref[...] * jnp.exp(lc)).astype(BF16)
    to_end = jnp.exp(lc_end - lc)
    b_e = (b * to_end).astype(BF16)
    k_e = (kp * to_end).astype(BF16)
    v_b = v_ref[...].astype(BF16)
    g_end = jnp.exp(lc_end)

    heads = range(RWKV_HEADS)
    sl = [slice(h * n, (h + 1) * n) for h in heads]
    gm = [_dot_nt(jnp.concatenate([a_t[:, sl[h]], r_t[:, sl[h]]], axis=0),
                  jnp.concatenate([b_t[:, sl[h]], k_t[:, sl[h]]], axis=0)) for h in heads]
    n_ab = [jnp.where(strict, gm[h][0:n, 0:n], 0.0).astype(BF16) for h in heads]
    l_ak = [jnp.where(strict, gm[h][0:n, n:2 * n], 0.0).astype(BF16) for h in heads]
    p_rb = [jnp.where(incl, gm[h][n:2 * n, 0:n], 0.0).astype(BF16) for h in heads]
    p_rk = [jnp.where(incl, gm[h][n:2 * n, n:2 * n], 0.0).astype(BF16) for h in heads]
    s0 = [s_ref[h] for h in heads]
    s0_b = [s0[h].astype(BF16) for h in heads]
    u = [_dot_nt(a_t[:, sl[h]], s0_b[h]) + _dot(l_ak[h], v_b[:, sl[h]]) for h in heads]
    pw = n_ab
    for step in range(6):
        u = [u[h] + _dot(pw[h], u[h].astype(BF16)) for h in heads]
        if step < 5:
            pw = [_dot(pw[h], pw[h]).astype(BF16) for h in heads]
    u_b = [u[h].astype(BF16) for h in heads]
    y = [_dot_nt(r_t[:, sl[h]], s0_b[h]) + _dot(p_rb[h], u_b[h]) + _dot(p_rk[h], v_b[:, sl[h]])
         for h in heads]
    s_new = [s0[h] * g_end[:, sl[h]] + _dot_tn(
        jnp.concatenate([u_b[h], v_b[:, sl[h]]], axis=0),
        jnp.concatenate([b_e[:, sl[h]], k_e[:, sl[h]]], axis=0)) for h in heads]
    for h in heads:
        y_ref[:, sl[h]] = y[h]
        s_ref[h] = s_new[h]
        s_out_ref[h] = s_new[h]


def rwkv_scan(r, ld, kp, v, kk, b):
    bsz, t, c = r.shape
    blk = pl.BlockSpec((None, CHUNK, c), lambda i, j: (i, j, 0))
    st = pl.BlockSpec((None, RWKV_HEADS, RWKV_HEAD_DIM, RWKV_HEAD_DIM), lambda i, j: (i, 0, 0, 0))
    return pl.pallas_call(
        _rwkv_scan_kernel,
        grid=(bsz, t // CHUNK),
        in_specs=[blk] * 6,
        out_specs=[blk, st],
        out_shape=[jax.ShapeDtypeStruct((bsz, t, c), F32),
                   jax.ShapeDtypeStruct((bsz, RWKV_HEADS, RWKV_HEAD_DIM, RWKV_HEAD_DIM), F32)],
        scratch_shapes=[pltpu.VMEM((RWKV_HEADS, RWKV_HEAD_DIM, RWKV_HEAD_DIM), F32)],
        compiler_params=_cp(("parallel", "arbitrary")),
        name="rwkv_scan",
    )(r, ld, kp, v, kk, b)


STEP_UNROLL = 8


def _rwkv_step_kernel(r_ref, ld_ref, kp_ref, v_ref, kk_ref, b_ref, s_ref, y_ref, s_out_ref):
    n = RWKV_HEAD_DIM
    neg_kk, decay = -kk_ref[...], jnp.exp(ld_ref[...])
    b_mat, kp_mat, r_mat = b_ref[...], kp_ref[...], r_ref[...]

    def body(i, carry):
        v0 = pl.multiple_of(i * STEP_UNROLL, STEP_UNROLL)
        v_rows = v_ref[pl.ds(v0, STEP_UNROLL), :]
        rows = range(STEP_UNROLL)
        s = [s_ref[v0 + j] for j in rows]
        sa = [jnp.sum(s[j] * neg_kk, axis=0, keepdims=True) for j in rows]
        s_new = [s[j] * decay + sa[j] * b_mat + v_rows[j:j + 1, :] * kp_mat for j in rows]
        y = [jnp.sum(s_new[j] * r_mat, axis=0, keepdims=True) for j in rows]
        for j in rows:
            s_out_ref[v0 + j] = s_new[j]
        y_ref[pl.ds(v0, STEP_UNROLL), :] = jnp.concatenate(y, axis=0)
        return carry

    lax.fori_loop(0, n // STEP_UNROLL, body, 0)


def rwkv_step(r, ld, kp, v, kk, b, state_t):
    _, nh, n, _, bsz = state_t.shape
    vec = pl.BlockSpec((n, bsz), lambda h: (h, 0))
    st = pl.BlockSpec((None, None, n, n, bsz), lambda h: (0, h, 0, 0, 0))
    return pl.pallas_call(
        _rwkv_step_kernel,
        grid=(nh,),
        in_specs=[vec] * 6 + [st],
        out_specs=[vec, st],
        out_shape=[jax.ShapeDtypeStruct((nh * n, bsz), F32),
                   jax.ShapeDtypeStruct(state_t.shape, F32)],
        compiler_params=_cp(("parallel",)),
        name="rwkv_step",
    )(r, ld, kp, v, kk, b, state_t)


def _rwkv_post_kernel(y_ref, bonus_ref, g_ref, lnw_ref, lnb_ref, o_ref, *, y_channel_major):
    gmat = _head_indicator()
    inv = 1.0 / RWKV_HEAD_DIM
    for c in range(o_ref.shape[1] // LANES):
        sl = slice(c * LANES, (c + 1) * LANES)
        y = y_ref[sl, :].T if y_channel_major else y_ref[:, sl]
        d = y - _group_sum(y, gmat) * inv
        var = _group_sum(d * d, gmat) * inv
        yn = d * lax.rsqrt(var + GN_EPS) * lnw_ref[:, sl] + lnb_ref[:, sl]
        o_ref[:, sl] = (yn + bonus_ref[:, sl]) * g_ref[:, sl]


def rwkv_post(y, bonus, g, ln_w, ln_b, *, tm, y_channel_major=False):
    m, c = bonus.shape
    blk = pl.BlockSpec((tm, c), lambda i: (i, 0))
    y_blk = pl.BlockSpec((c, tm), lambda i: (0, i)) if y_channel_major else blk
    vec = pl.BlockSpec((1, c), lambda i: (0, 0))
    return pl.pallas_call(
        functools.partial(_rwkv_post_kernel, y_channel_major=y_channel_major),
        grid=(m // tm,),
        in_specs=[y_blk, blk, blk, vec, vec],
        out_specs=blk,
        out_shape=jax.ShapeDtypeStruct((m, c), F32),
        compiler_params=_cp(("parallel",)),
        name="rwkv_post",
    )(y, bonus, g, ln_w.reshape(1, c), ln_b.reshape(1, c))


ROUTER_LANES = LANES
ROW_TILES = D_MODEL // LANES


def _rows_to_tiles(ref, x):
    rows = x.shape[0]
    for j in range(ROW_TILES):
        ref[pl.ds(j, rows, stride=ROW_TILES), :] = x[:, j * LANES:(j + 1) * LANES]


def _tiles_to_rows(ref, rows):
    return jnp.concatenate(
        [ref[pl.ds(j, rows, stride=ROW_TILES), :] for j in range(ROW_TILES)], axis=1)


def _router_kernel(ha_ref, hb_ref, lnw_ref, whi_ref, wlo_ref, bias_ref, u_ref, idx_ref, gate_ref,
                   *, steps_a):
    use_a = pl.program_id(0) < steps_a
    h = jnp.where(use_a, ha_ref[...], hb_ref[...])
    u = _rms_rows(h, lnw_ref[...])
    _rows_to_tiles(u_ref, u)
    u_hi, u_lo = _split2(u)
    w_hi = whi_ref[...]
    logits = _dot(u_hi, w_hi) + _dot(u_lo, w_hi) + _dot(u_hi, wlo_ref[...]) + bias_ref[...]
    lane = lax.broadcasted_iota(jnp.int32, logits.shape, 1)
    neg = -jnp.inf

    def first_max(x):
        m = jnp.max(x, axis=1, keepdims=True)
        return m, jnp.min(jnp.where(x == m, lane, ROUTER_LANES), axis=1, keepdims=True)

    gl = jnp.where(lane < N_EXPERT_GROUPS, logits, neg)
    g_max, g_idx = first_max(gl)
    g_gate = 1.0 / jnp.sum(jnp.exp(gl - g_max), axis=1, keepdims=True)
    lo = N_EXPERT_GROUPS + g_idx * EXPERTS_PER_GROUP
    el = jnp.where((lane >= lo) & (lane < lo + EXPERTS_PER_GROUP), logits, neg)
    v1, i1 = first_max(el)
    v2, i2 = first_max(jnp.where(lane == i1, neg, el))
    e2 = jnp.exp(v2 - v1)
    w1 = g_gate / (1.0 + e2)
    w2 = g_gate * e2 / (1.0 + e2)
    idx_ref[...] = jnp.where(lane == 0, i1 - N_EXPERT_GROUPS,
                             jnp.where(lane == 1, i2 - N_EXPERT_GROUPS, 0))
    gate_ref[...] = jnp.where(lane == 0, w1, jnp.where(lane == 1, w2, 0.0))


def moe_router(h_a, h_b, ln_w, w_hi, w_lo, bias, *, tm):
    (ma, d), mb = h_a.shape, h_b.shape[0]
    assert ma % tm == 0 and mb % tm == 0
    steps_a, steps_b = ma // tm, mb // tm
    m = ma + mb
    const = lambda r, w: pl.BlockSpec((r, w), lambda i: (0, 0))
    row = lambda w: pl.BlockSpec((tm, w), lambda i: (i, 0))
    return pl.pallas_call(
        functools.partial(_router_kernel, steps_a=steps_a),
        grid=(steps_a + steps_b,),
        in_specs=[pl.BlockSpec((tm, d), lambda i: (jnp.minimum(i, steps_a - 1), 0)),
                  pl.BlockSpec((tm, d), lambda i: (jnp.maximum(i - steps_a, 0), 0)),
                  const(1, d), const(d, ROUTER_LANES), const(d, ROUTER_LANES),
                  const(1, ROUTER_LANES)],
        out_specs=[pl.BlockSpec((tm * (d // LANES), LANES), lambda i: (i, 0)),
                   row(ROUTER_LANES), row(ROUTER_LANES)],
        out_shape=[jax.ShapeDtypeStruct((m * (d // LANES), LANES), F32),
                   jax.ShapeDtypeStruct((m, ROUTER_LANES), jnp.int32),
                   jax.ShapeDtypeStruct((m, ROUTER_LANES), F32)],
        compiler_params=_cp(("arbitrary",)),
        name="moe_router",
    )(h_a, h_b, ln_w.reshape(1, d), w_hi, w_lo, bias)


X_SLOTS = 3
Y_SLOTS = 2


W_SLOTS = 3
ROW_DMA_PRIORITY = 1


def _moe_expert_kernel(run_ref, rexp_ref, nused_ref, tok0_ref, tok1_ref, tok2_ref, dst_ref,
                       roww_ref, u_hbm, wg_hbm, wu_hbm, wd_hbm, y_hbm, xbuf, ybuf, wg_f, wu_f,
                       wd_f, wg_b, wu_b, wd_b, sem_in, sem_out, sem_w):
    i = pl.program_id(0)
    n_used, n_runs = nused_ref[0], nused_ref[1]
    tile_rows = MOE_BLOCK * ROW_TILES
    pad_base = y_hbm.shape[0] - Y_SLOTS * tile_rows
    run = run_ref[i]

    def weight_copies(k):
        e, s = rexp_ref[jnp.minimum(k, n_runs - 1)], lax.rem(k, W_SLOTS)
        return [pltpu.make_async_copy(hbm.at[e], buf.at[s], sem_w.at[s])
                for hbm, buf in ((wg_hbm, wg_f), (wu_hbm, wu_f), (wd_hbm, wd_f))]

    def gather_block(idx_ref, x_slot):
        for r in range(MOE_BLOCK):
            pltpu.make_async_copy(u_hbm.at[pl.ds(idx_ref[0, 0, r], ROW_TILES)],
                                  xbuf.at[x_slot, pl.ds(r * ROW_TILES, ROW_TILES)],
                                  sem_in.at[x_slot]).start(priority=ROW_DMA_PRIORITY)

    def scatter_rows(idx_ref, y_slot, rows):
        for r in rows:
            pltpu.make_async_copy(ybuf.at[y_slot, pl.ds(r * ROW_TILES, ROW_TILES)],
                                  y_hbm.at[pl.ds(idx_ref[0, 0, r], ROW_TILES)],
                                  sem_out.at[y_slot]).start(priority=ROW_DMA_PRIORITY)

    def gather_wait(x_slot):
        pltpu.make_async_copy(u_hbm.at[pl.ds(0, tile_rows)], xbuf.at[x_slot],
                              sem_in.at[x_slot]).wait()

    def scatter_wait(y_slot):
        pltpu.make_async_copy(ybuf.at[y_slot], y_hbm.at[pl.ds(0, tile_rows)],
                              sem_out.at[y_slot]).wait()

    @pl.when(i == 0)
    def _():
        ybuf[0] = jnp.zeros(ybuf.shape[1:], F32)
        for s in range(Y_SLOTS):
            pltpu.make_async_copy(ybuf.at[0], y_hbm.at[pl.ds(pad_base + s * tile_rows, tile_rows)],
                                  sem_out.at[s]).start()
        for k in range(W_SLOTS - 1):
            for cp in weight_copies(k):
                cp.start()
        gather_block(tok0_ref, 0)
        gather_block(tok1_ref, 1)
        for s in range(Y_SLOTS):
            scatter_wait(s)

    @pl.when(i < n_used)
    def _():
        x_slot = lax.rem(i, X_SLOTS)
        y_slot = lax.rem(i, Y_SLOTS)

        @pl.when((i == 0) | (run != run_ref[jnp.maximum(i - 1, 0)]))
        def _():
            for cp in weight_copies(run):
                cp.wait()
            w_slot = lax.rem(run, W_SLOTS)
            wg_b[...] = wg_f[w_slot].astype(BF16)
            wu_b[...] = wu_f[w_slot].astype(BF16)
            wd_b[...] = wd_f[w_slot].astype(BF16)
            for cp in weight_copies(run + W_SLOTS - 1):
                cp.start()

        gather_wait(x_slot)

        @pl.when(i >= Y_SLOTS)
        def _():
            scatter_wait(y_slot)

        x = _tiles_to_rows(xbuf.at[x_slot], MOE_BLOCK).astype(BF16)
        hg = _dot(x, wg_b[...])
        hu = _dot(x, wu_b[...])
        act = (hg * _sigmoid(hg) * hu).astype(BF16)
        y = _dot(act, wd_b[...]) * roww_ref[...]
        _rows_to_tiles(ybuf.at[y_slot], y)
        scatter_rows(dst_ref, y_slot, range(MOE_BLOCK))
        gather_block(tok2_ref, lax.rem(i + 2, X_SLOTS))

        @pl.when(i == n_used - 1)
        def _():
            scatter_wait(y_slot)

            @pl.when(i >= 1)
            def _():
                scatter_wait(1 - y_slot)

            gather_wait(lax.rem(i + 1, X_SLOTS))
            gather_wait(lax.rem(i + 2, X_SLOTS))
            for k in range(W_SLOTS - 1):
                for cp in weight_copies(n_runs + k):
                    cp.wait()


def moe_experts(u_all, row_src, row_dst, row_w, block_run, run_exp, n_used_runs, w_gate, w_up,
                w_down, n_assign):
    d, ff = w_gate.shape[1], w_gate.shape[2]
    n_blocks = row_src.shape[0]
    tile_rows = MOE_BLOCK * ROW_TILES
    smem_blk = lambda off: pl.BlockSpec(
        (1, 1, MOE_BLOCK), lambda i, *_: (jnp.clip(i + off, 0, n_blocks - 1), 0, 0),
        memory_space=pltpu.SMEM)
    hbm = pl.BlockSpec(memory_space=pl.ANY)
    grid_spec = pltpu.PrefetchScalarGridSpec(
        num_scalar_prefetch=3,
        grid=(n_blocks,),
        in_specs=[
            smem_blk(0), smem_blk(1), smem_blk(2), smem_blk(0),
            pl.BlockSpec((MOE_BLOCK, 1), lambda i, *_: (i, 0)),
            hbm, hbm, hbm, hbm,
        ],
        out_specs=hbm,
        scratch_shapes=[
            pltpu.VMEM((X_SLOTS, tile_rows, LANES), F32),
            pltpu.VMEM((Y_SLOTS, tile_rows, LANES), F32),
            pltpu.VMEM((W_SLOTS, d, ff), F32), pltpu.VMEM((W_SLOTS, d, ff), F32),
            pltpu.VMEM((W_SLOTS, ff, d), F32),
            pltpu.VMEM((d, ff), BF16), pltpu.VMEM((d, ff), BF16), pltpu.VMEM((ff, d), BF16),
            pltpu.SemaphoreType.DMA((X_SLOTS,)), pltpu.SemaphoreType.DMA((Y_SLOTS,)),
            pltpu.SemaphoreType.DMA((W_SLOTS,)),
        ],
    )
    y_rows = (n_assign + Y_SLOTS * MOE_BLOCK) * ROW_TILES
    return pl.pallas_call(
        _moe_expert_kernel,
        grid_spec=grid_spec,
        out_shape=jax.ShapeDtypeStruct((y_rows, LANES), F32),
        compiler_params=_cp(("arbitrary",), vmem=MOE_VMEM_LIMIT),
        name="moe_experts",
    )(block_run, run_exp, n_used_runs, row_src, row_src, row_src, row_dst, row_w, u_all,
      w_gate, w_up, w_down)


def _moe_combine_kernel(h_ref, y0_ref, y1_ref, o_ref):
    rows = h_ref.shape[0]
    o_ref[...] = h_ref[...] + (_tiles_to_rows(y0_ref, rows) + _tiles_to_rows(y1_ref, rows))


def moe_combine(h, y_slots, row_off, slot_stride, *, tm):
    m, d = h.shape
    assert row_off % tm == 0 and slot_stride % tm == 0
    off0, off1 = row_off // tm, (row_off + slot_stride) // tm
    return pl.pallas_call(
        _moe_combine_kernel,
        grid=(m // tm,),
        in_specs=[pl.BlockSpec((tm, d), lambda i: (i, 0)),
                  pl.BlockSpec((tm * ROW_TILES, LANES), lambda i: (i + off0, 0)),
                  pl.BlockSpec((tm * ROW_TILES, LANES), lambda i: (i + off1, 0))],
        out_specs=pl.BlockSpec((tm, d), lambda i: (i, 0)),
        out_shape=jax.ShapeDtypeStruct((m, d), F32),
        compiler_params=_cp(("parallel",)),
        name="moe_combine",
    )(h, y_slots, y_slots)


def moe_dispatch(e_idx, gates, slot_stride):
    m = e_idx.shape[0]
    a = m * TOP_K
    e_flat = e_idx.reshape(a)
    order = jnp.argsort(e_flat, stable=True).astype(jnp.int32)
    counts = jnp.sum(e_flat[:, None] == jnp.arange(N_EXPERTS, dtype=jnp.int32)[None, :],
                     axis=0, dtype=jnp.int32)
    pad_counts = (counts + MOE_BLOCK - 1) // MOE_BLOCK * MOE_BLOCK
    starts = jnp.cumsum(counts) - counts
    pad_ends = jnp.cumsum(pad_counts)
    pad_starts = pad_ends - pad_counts
    n_blocks = a // MOE_BLOCK + N_EXPERTS
    p = n_blocks * MOE_BLOCK
    n_used = (pad_ends[-1] // MOE_BLOCK).astype(jnp.int32)
    blk = jnp.arange(n_blocks, dtype=jnp.int32)
    blk_start = jnp.minimum(blk, n_used - 1) * MOE_BLOCK
    block_exp = jnp.minimum(jnp.searchsorted(pad_ends, blk_start, side='right'),
                            N_EXPERTS - 1).astype(jnp.int32)
    in_exp = blk * MOE_BLOCK - pad_starts[block_exp]
    row_cnt = jnp.where(blk < n_used, jnp.clip(counts[block_exp] - in_exp, 0, MOE_BLOCK), 0)
    lane = jnp.arange(MOE_BLOCK, dtype=jnp.int32)[None, :]
    valid = lane < row_cnt[:, None]
    src = jnp.clip((starts[block_exp] + in_exp)[:, None] + lane, 0, a - 1)
    assign = order[src]
    row_tok = jnp.where(valid, assign // TOP_K, 0)
    pad_dst = TOP_K * slot_stride + (blk % Y_SLOTS)[:, None] * MOE_BLOCK + lane
    row_dst = jnp.where(valid, (assign % TOP_K) * slot_stride + assign // TOP_K, pad_dst)
    row_w = jnp.where(valid, gates.reshape(a)[assign], 0.0)
    as_blocks = lambda x: (x * ROW_TILES).astype(jnp.int32).reshape(n_blocks, 1, MOE_BLOCK)
    has_rows = counts > 0
    run_exp = jnp.argsort(~has_rows, stable=True).astype(jnp.int32)
    block_run = (jnp.cumsum(has_rows) - 1)[block_exp].astype(jnp.int32)
    n_used_runs = jnp.stack([n_used, jnp.sum(has_rows, dtype=jnp.int32)])
    return (as_blocks(row_tok), as_blocks(row_dst), row_w.reshape(p, 1), block_run, run_exp,
            n_used_runs)


def rwkv_params(rw_mu, rw_w0, rw_w2, rw_a0, rw_a2, rw_g2, rw_k_k, rw_k_a, rw_r_k):
    c = RWKV_WIDTH
    mu = jnp.pad(rw_mu, (0, RWKV_PROJ_PAD - RWKV_PROJ)).reshape(1, RWKV_PROJ_PAD)
    w_lora = jnp.zeros((LANES, 2 * c), F32)
    w_lora = w_lora.at[0:DECAY_LORA, 0:c].set(rw_w2).at[DECAY_LORA:LANES, c:2 * c].set(rw_a2)
    w_gate = jnp.pad(rw_g2, ((0, GATE_PAD - GATE_LORA), (0, 0)))
    vec = lambda x: x.reshape(1, c)
    return (mu, vec(rw_w0), vec(rw_a0), vec(rw_k_k), vec(rw_k_a), vec(rw_r_k),
            w_lora.astype(BF16), w_gate.astype(BF16))


def _token_tiles(m):
    return (1024, 512) if m % 1024 == 0 else (m, m)


def _dense_front(x2d, wts, tm):
    pa = norm_matmul(x2d, wts['ln1_w'], wts['w_att'], tm=tm, tn=512)
    pr = norm_matmul(x2d, wts['ln1_w'], wts['w_rw'], tm=tm, tn=RWKV_PROJ_PAD // 3)
    return pa, pr


def _dense_back(x2d, att2d, rw2d, wts, xattn_fn, tm):
    h1 = matmul_residual([att2d, rw2d], [wts['w_out_a'], wts['w_out_r']], x2d, tm=tm, tn=512)
    qx = norm_matmul(h1, wts['ln2_w'], wts['xq_w'], tm=tm, tn=XATT_WIDTH)
    ox = xattn_fn(qx)
    return matmul_residual([ox], [wts['xo_w']], h1, tm=tm, tn=512)


def kernel(x_prompt, x_sample, cache_win_k, cache_win_v, state_wkv, state_shift, cache_mem_k, cache_mem_v, mem_prompt, ln1_w, w_in, q_norm_w, k_norm_w, attn_sinks, rw_mu, rw_w0, rw_w2, rw_a0, rw_a2, rw_g2, rw_k_k, rw_k_a, rw_r_k, rw_ln_w, rw_ln_b, w_out, ln2_w, mem_norm_w, xq_w, xkv_w, xq_norm_w, xk_norm_w, xo_w, ln3_w, router_group_w, router_group_b, router_expert_w, router_expert_b, exp_w_gate, exp_w_up, exp_w_down):
    assert w_in.shape[0] == 1, "single-layer stack"
    bp, seq, d = x_prompt.shape
    bs = x_sample.shape[0]
    mp = bp * seq
    c = RWKV_WIDTH

    router_w = jnp.concatenate(
        [router_group_w[0], router_expert_w[0],
         jnp.zeros((d, ROUTER_LANES - N_EXPERT_GROUPS - N_EXPERTS), F32)], axis=1)
    router_hi = router_w.astype(BF16)
    wts = {
        'ln1_w': ln1_w[0], 'ln2_w': ln2_w[0], 'ln3_w': ln3_w[0],
        'w_att': w_in[0][:, :ATT_PROJ].astype(BF16),
        'w_rw': jnp.pad(w_in[0][:, ATT_PROJ:],
                        ((0, 0), (0, RWKV_PROJ_PAD - RWKV_PROJ))).astype(BF16),
        'w_out_a': w_out[0][:ATT_WIDTH].astype(BF16),
        'w_out_r': w_out[0][ATT_WIDTH:].astype(BF16),
        'xq_w': xq_w[0].astype(BF16), 'xo_w': xo_w[0].astype(BF16),
        'router_hi': router_hi,
        'router_lo': (router_w - router_hi.astype(F32)).astype(BF16),
        'router_b': jnp.pad(jnp.concatenate([router_group_b[0], router_expert_b[0]]),
                            (0, ROUTER_LANES - N_EXPERT_GROUPS - N_EXPERTS)).reshape(1, -1),
    }
    rw_par = rwkv_params(rw_mu[0], rw_w0[0], rw_w2[0], rw_a0[0], rw_a2[0], rw_g2[0],
                         rw_k_k[0], rw_k_a[0], rw_r_k[0])

    tm_p, te_p = _token_tiles(mp)
    xp = x_prompt.reshape(mp, d)
    pa, pr = _dense_front(xp, wts, tm_p)
    pa3 = pa.reshape(bp, seq, ATT_PROJ)
    pr3 = pr.reshape(bp, seq, RWKV_PROJ_PAD)
    tabs_p = rope_tables(jnp.arange(seq, dtype=jnp.int32))
    att_p, kn_p = swa_prompt(pa3, tabs_p, q_norm_w[0], k_norm_w[0], attn_sinks[0])
    prep = rwkv_prep_seq(pr3, jnp.zeros((bp, 1, RWKV_PROJ_PAD), F32), rw_par)
    r, ld, kp, v, kk, b, bonus, g = prep
    y_p, wkv_p = rwkv_scan(r, ld, kp, v, kk, b)
    rw_p = rwkv_post(y_p.reshape(mp, c), bonus.reshape(mp, c), g.reshape(mp, c),
                     rw_ln_w[0], rw_ln_b[0], tm=te_p)

    n_mem = mem_prompt.shape[1]
    kv_mem = norm_matmul(mem_prompt.reshape(bp * n_mem, d), mem_norm_w[0],
                         xkv_w[0].astype(BF16), tm=bp * n_mem, tn=512)
    mem_k = head_rms(kv_mem[:, :XATT_WIDTH], xk_norm_w[0])
    mem_v = kv_mem[:, XATT_WIDTH:]
    mem_k3 = mem_k.reshape(bp, n_mem, XATT_WIDTH)
    mem_v3 = mem_v.reshape(bp, n_mem, XATT_WIDTH)

    def xattn_p(qx):
        return xattn_prompt(qx.reshape(bp, seq, XATT_WIDTH), mem_k3, mem_v3,
                            xq_norm_w[0]).reshape(mp, XATT_WIDTH)

    h2_p = _dense_back(xp, att_p.reshape(mp, ATT_WIDTH), rw_p, wts, xattn_p, tm_p)

    tm_s, te_s = _token_tiles(bs)
    xs = x_sample.reshape(bs, d)
    sa, sr = _dense_front(xs, wts, tm_s)
    tabs_s = rope_tables(PAST_LEN + jnp.arange(1, dtype=jnp.int32))
    qk_w = jnp.concatenate([jnp.tile(q_norm_w[0], ATT_HEADS),
                            jnp.tile(k_norm_w[0], ATT_KV_HEADS)]).reshape(1, -1)
    qk = qk_norm_rope(sa[:, :ATT_WIDTH + KV_WIDTH], qk_w, tabs_s)
    nbuf = cache_win_k.shape[2]

    def feature_major(cache):
        return jnp.transpose(cache, (0, 1, 3, 4, 2)).reshape(bs, KV_WIDTH, nbuf)

    def position_major(win):
        return jnp.transpose(win.reshape(1, bs, ATT_KV_HEADS, HEAD_DIM, nbuf), (0, 1, 4, 2, 3))

    att_s, win_k, win_v = swa_decode(
        qk[:, :ATT_WIDTH].reshape(bs, ATT_HEADS, HEAD_DIM),
        qk[:, ATT_WIDTH:].reshape(bs, 1, KV_WIDTH),
        sa[:, ATT_WIDTH + KV_WIDTH:].reshape(bs, 1, KV_WIDTH),
        feature_major(cache_win_k), feature_major(cache_win_v), attn_sinks[0])
    shift_prev = jnp.pad(state_shift[0], ((0, 0), (0, RWKV_PROJ_PAD - RWKV_PROJ)))
    r, ld, kp, v, kk, b, bonus, g = rwkv_prep_tok(sr, shift_prev, rw_par)
    y_s, wkv_s = rwkv_step(r, ld, kp, v, kk, b, jnp.transpose(state_wkv, (0, 2, 3, 4, 1)))
    wkv_s = jnp.transpose(wkv_s, (0, 4, 1, 2, 3))
    rw_s = rwkv_post(y_s, bonus, g, rw_ln_w[0], rw_ln_b[0], tm=te_s, y_channel_major=True)
    def xattn_s(qx):
        q_pad = jnp.pad(qx.reshape(bs, XATT_HEADS, XATT_HEAD_DIM), ((0, 0), (0, 4), (0, 0)))
        rows_of = lambda c: c.reshape(bs, n_mem * XATT_HEADS, XATT_HEAD_DIM)
        o = xattn_decode(q_pad, rows_of(cache_mem_k), rows_of(cache_mem_v), xq_norm_w[0])
        return o[:, :XATT_HEADS].reshape(bs, XATT_WIDTH)

    h2_s = _dense_back(xs, att_s.reshape(bs, ATT_WIDTH), rw_s, wts, xattn_s, tm_s)

    m_all = mp + bs
    slot_stride = m_all
    tc = math.gcd(mp, bs, 512)
    u_all, idx_all, gate_all = moe_router(h2_p, h2_s, wts['ln3_w'], wts['router_hi'],
                                          wts['router_lo'], wts['router_b'], tm=tc)
    row_src, row_dst, row_w, block_run, run_exp, n_used_runs = moe_dispatch(
        idx_all[:, :TOP_K], gate_all[:, :TOP_K], slot_stride)
    y_slots = moe_experts(u_all, row_src, row_dst, row_w, block_run, run_exp, n_used_runs,
                          exp_w_gate[0], exp_w_up[0], exp_w_down[0], TOP_K * slot_stride)
    out_p = moe_combine(h2_p, y_slots, 0, slot_stride, tm=tc)
    out_s = moe_combine(h2_s, y_slots, mp, slot_stride, tm=tc)

    win = min(WINDOW, seq)
    kv_shape = (1, bp, win, ATT_KV_HEADS, HEAD_DIM)
    return (
        out_p.reshape(bp, seq, d),
        out_s.reshape(bs, 1, d),
        kn_p[:, seq - win:].reshape(kv_shape),
        pa3[:, seq - win:, ATT_WIDTH + KV_WIDTH:].reshape(kv_shape),
        wkv_p[None],
        pr3[:, seq - 1, :RWKV_PROJ][None],
        mem_k3.reshape(1, bp, n_mem, XATT_HEADS, XATT_HEAD_DIM),
        mem_v3.reshape(1, bp, n_mem, XATT_HEADS, XATT_HEAD_DIM),
        position_major(win_k),
        position_major(win_v),
        wkv_s,
        sr[:, :RWKV_PROJ].reshape(1, bs, RWKV_PROJ),
    )
```

```python
import functools
import math

import jax
import jax.numpy as jnp
from jax import lax
from jax.experimental import pallas as pl
from jax.experimental.pallas import tpu as pltpu

F32 = jnp.float32
BF16 = jnp.bfloat16

D_MODEL = 2048
HEAD_DIM = 64
ATT_HEADS = 16
ATT_KV_HEADS = 4
ATT_GROUP = ATT_HEADS // ATT_KV_HEADS
ATT_WIDTH = ATT_HEADS * HEAD_DIM
KV_WIDTH = ATT_KV_HEADS * HEAD_DIM
ATT_PROJ = ATT_WIDTH + 2 * KV_WIDTH
WINDOW = 128
ATT_SCALE = HEAD_DIM ** -0.5
ROPE_THETA = 500000.0
ROT_DIM = HEAD_DIM // 4
PAST_LEN = 16384

RWKV_WIDTH = 1024
RWKV_HEAD_DIM = 64
RWKV_HEADS = 16
DECAY_LORA = 64
AAA_LORA = 64
GATE_LORA = 160
RWKV_PROJ = 3 * RWKV_WIDTH + DECAY_LORA + AAA_LORA + GATE_LORA
RWKV_PROJ_PAD = 3456

N_MEM = 256
XATT_HEADS = 4
XATT_HEAD_DIM = 128
XATT_WIDTH = XATT_HEADS * XATT_HEAD_DIM

N_EXPERT_GROUPS = 8
EXPERTS_PER_GROUP = 8
N_EXPERTS = 64
TOP_K = 2
EXPERT_FF = D_MODEL // 4
MOE_BLOCK = 128

RMS_EPS = 1e-6
GN_EPS = 64e-5

LANES = 128
CHUNK = 64
VMEM_LIMIT = 56 * 1024 * 1024
MOE_VMEM_LIMIT = 60 * 1024 * 1024


def _cp(sem, vmem=VMEM_LIMIT):
    return pltpu.CompilerParams(dimension_semantics=sem, vmem_limit_bytes=vmem)


def _rms_rows(x, w):
    ms = jnp.mean(x * x, axis=-1, keepdims=True)
    return x * lax.rsqrt(ms + RMS_EPS) * w


def _split2(x):
    hi = x.astype(BF16)
    lo = (x - hi.astype(F32)).astype(BF16)
    return hi, lo


def _split3(x):
    h1 = x.astype(BF16)
    r1 = x - h1.astype(F32)
    h2 = r1.astype(BF16)
    h3 = (r1 - h2.astype(F32)).astype(BF16)
    return h1, h2, h3


def _dot(a, b):
    return jnp.dot(a, b, preferred_element_type=F32)


def _dot_nt(a, b):
    return lax.dot_general(a, b, (((1,), (1,)), ((), ())), preferred_element_type=F32)


def _group_sum(x, gmat):
    hi, lo = _split2(x)
    return _dot(hi, gmat) + _dot(lo, gmat)


def _head_indicator():
    r = lax.broadcasted_iota(jnp.int32, (LANES, LANES), 0) // HEAD_DIM
    c = lax.broadcasted_iota(jnp.int32, (LANES, LANES), 1) // HEAD_DIM
    return jnp.where(r == c, 1.0, 0.0).astype(BF16)


def _norm_mm_kernel(x_ref, lnw_ref, w_ref, o_ref, xn_ref):
    @pl.when(pl.program_id(1) == 0)
    def _():
        xn_ref[...] = _rms_rows(x_ref[...], lnw_ref[...]).astype(BF16)

    o_ref[...] = _dot(xn_ref[...], w_ref[...])


def norm_matmul(x, ln_w, w_bf16, *, tm, tn):
    m, k = x.shape
    n = w_bf16.shape[1]
    assert m % tm == 0 and n % tn == 0
    return pl.pallas_call(
        _norm_mm_kernel,
        grid=(m // tm, n // tn),
        in_specs=[
            pl.BlockSpec((tm, k), lambda i, j: (i, 0)),
            pl.BlockSpec((1, k), lambda i, j: (0, 0)),
            pl.BlockSpec((k, tn), lambda i, j: (0, j)),
        ],
        out_specs=pl.BlockSpec((tm, tn), lambda i, j: (i, j)),
        out_shape=jax.ShapeDtypeStruct((m, n), F32),
        scratch_shapes=[pltpu.VMEM((tm, k), BF16)],
        compiler_params=_cp(("parallel", "arbitrary")),
        name="norm_matmul",
    )(x, ln_w.reshape(1, k), w_bf16)


def _mm_res_kernel(*refs, n_lhs):
    a_refs = refs[:n_lhs]
    w_refs = refs[n_lhs:2 * n_lhs]
    res_ref = refs[2 * n_lhs]
    o_ref = refs[2 * n_lhs + 1]
    acc = res_ref[...]
    for a_ref, w_ref in zip(a_refs, w_refs):
        acc = acc + _dot(a_ref[...].astype(BF16), w_ref[...])
    o_ref[...] = acc


def matmul_residual(lhs_list, w_list, res, *, tm, tn):
    m, n = res.shape
    n_lhs = len(lhs_list)
    assert m % tm == 0 and n % tn == 0
    in_specs = [pl.BlockSpec((tm, a.shape[1]), lambda i, j: (i, 0)) for a in lhs_list]
    in_specs += [pl.BlockSpec((w.shape[0], tn), lambda i, j: (0, j)) for w in w_list]
    in_specs += [pl.BlockSpec((tm, tn), lambda i, j: (i, j))]
    return pl.pallas_call(
        functools.partial(_mm_res_kernel, n_lhs=n_lhs),
        grid=(m // tm, n // tn),
        in_specs=in_specs,
        out_specs=pl.BlockSpec((tm, tn), lambda i, j: (i, j)),
        out_shape=jax.ShapeDtypeStruct((m, n), F32),
        compiler_params=_cp(("parallel", "arbitrary")),
        name="matmul_residual",
    )(*lhs_list, *w_list, res)


def rope_tables(pos):
    half = ROT_DIM // 2
    inv = ROPE_THETA ** (-jnp.arange(half, dtype=F32) * 2.0 / ROT_DIM)
    ang = pos.astype(F32)[:, None] * inv[None, :]
    cos, sin = jnp.cos(ang), jnp.sin(ang)
    t = pos.shape[0]
    ones = jnp.ones((t, HEAD_DIM - ROT_DIM), F32)
    zeros = jnp.zeros((t, HEAD_DIM - ROT_DIM), F32)
    z8 = jnp.zeros((t, half), F32)
    cos_t = jnp.concatenate([cos, cos, ones], axis=1)
    sin_a = jnp.concatenate([z8, sin, zeros], axis=1)
    sin_b = jnp.concatenate([-sin, z8, zeros], axis=1)
    return tuple(jnp.concatenate([a, a], axis=1) for a in (cos_t, sin_a, sin_b))


def _norm_rope_chunk(x, w, cos_t, sin_a, sin_b, gmat):
    ms = _group_sum(x * x, gmat) * (1.0 / HEAD_DIM)
    xn = x * lax.rsqrt(ms + RMS_EPS) * w
    half = ROT_DIM // 2
    return (xn * cos_t + pltpu.roll(xn, half, axis=1) * sin_a
            + pltpu.roll(xn, LANES - half, axis=1) * sin_b)


def _norm_rope(x, w, tabs, gmat):
    chunks = [
        _norm_rope_chunk(x[:, c * LANES:(c + 1) * LANES], w, *tabs, gmat)
        for c in range(x.shape[1] // LANES)
    ]
    return chunks[0] if len(chunks) == 1 else jnp.concatenate(chunks, axis=1)


def _sink_softmax(s, sink):
    m = jnp.maximum(jnp.max(s, axis=-1, keepdims=True), sink)
    e = jnp.exp(s - m)
    return e / (jnp.sum(e, axis=-1, keepdims=True) + jnp.exp(sink - m))


def _swa_prompt_kernel(q_ref, kc_ref, vc_ref, kp_ref, vp_ref, cc_ref, sac_ref, sbc_ref,
                       cp_ref, sap_ref, sbp_ref, qw_ref, kw_ref, sink_ref, o_ref, kn_ref):
    n = pl.program_id(1)
    blk = q_ref.shape[0]
    gmat = _head_indicator()
    tabs_c = (cc_ref[...], sac_ref[...], sbc_ref[...])
    tabs_p = (cp_ref[...], sap_ref[...], sbp_ref[...])
    q = _norm_rope(q_ref[...], qw_ref[...], tabs_c, gmat)
    k_cur = _norm_rope(kc_ref[...], kw_ref[...], tabs_c, gmat)
    k_prev = _norm_rope(kp_ref[...], kw_ref[...], tabs_p, gmat)
    kn_ref[...] = k_cur
    k_all = jnp.concatenate([k_prev, k_cur], axis=0).astype(BF16)
    v_all = jnp.concatenate([vp_ref[...], vc_ref[...]], axis=0).astype(BF16)

    qi = lax.broadcasted_iota(jnp.int32, (blk, 2 * blk), 0) + blk
    si = lax.broadcasted_iota(jnp.int32, (blk, 2 * blk), 1)
    rel = qi - si
    valid = (rel >= 0) & (rel <= WINDOW) & ((n > 0) | (si >= blk))

    groups = range(ATT_KV_HEADS)
    lanes = [slice(kv * HEAD_DIM, (kv + 1) * HEAD_DIM) for kv in groups]
    heads = [[kv * ATT_GROUP + g for g in range(ATT_GROUP)] for kv in groups]
    q_g = [jnp.concatenate([q[:, h * HEAD_DIM:(h + 1) * HEAD_DIM] for h in heads[kv]],
                           axis=0).astype(BF16) for kv in groups]
    s = [_dot_nt(q_g[kv], k_all[:, lanes[kv]]) * ATT_SCALE for kv in groups]
    p = [jnp.concatenate(
        [_sink_softmax(jnp.where(valid, s[kv][g * blk:(g + 1) * blk], -jnp.inf), sink_ref[h])
         for g, h in enumerate(heads[kv])], axis=0).astype(BF16) for kv in groups]
    o = [_dot(p[kv], v_all[:, lanes[kv]]) for kv in groups]
    for kv in groups:
        for g, h in enumerate(heads[kv]):
            o_ref[:, h * HEAD_DIM:(h + 1) * HEAD_DIM] = o[kv][g * blk:(g + 1) * blk]


def swa_prompt(pa, tabs, q_norm_w, k_norm_w, sinks):
    b, t, _ = pa.shape
    blk = WINDOW
    nb = t // blk
    qb, kb, vb = 0, ATT_WIDTH // KV_WIDTH, ATT_WIDTH // KV_WIDTH + 1
    cur = lambda i, n, *_: (i, n, 0)
    tab_cur = pl.BlockSpec((blk, LANES), lambda i, n: (n, 0))
    tab_prev = pl.BlockSpec((blk, LANES), lambda i, n: (jnp.maximum(n - 1, 0), 0))
    qw = jnp.tile(q_norm_w.reshape(1, HEAD_DIM), (1, 2))
    kw = jnp.tile(k_norm_w.reshape(1, HEAD_DIM), (1, 2))
    return pl.pallas_call(
        _swa_prompt_kernel,
        grid=(b, nb),
        in_specs=[
            pl.BlockSpec((None, blk, ATT_WIDTH), lambda i, n: (i, n, qb)),
            pl.BlockSpec((None, blk, KV_WIDTH), lambda i, n: (i, n, kb)),
            pl.BlockSpec((None, blk, KV_WIDTH), lambda i, n: (i, n, vb)),
            pl.BlockSpec((None, blk, KV_WIDTH), lambda i, n: (i, jnp.maximum(n - 1, 0), kb)),
            pl.BlockSpec((None, blk, KV_WIDTH), lambda i, n: (i, jnp.maximum(n - 1, 0), vb)),
            tab_cur, tab_cur, tab_cur, tab_prev, tab_prev, tab_prev,
            pl.BlockSpec((1, LANES), lambda i, n: (0, 0)),
            pl.BlockSpec((1, LANES), lambda i, n: (0, 0)),
            pl.BlockSpec(memory_space=pltpu.SMEM),
        ],
        out_specs=[
            pl.BlockSpec((None, blk, ATT_WIDTH), cur),
            pl.BlockSpec((None, blk, KV_WIDTH), cur),
        ],
        out_shape=[
            jax.ShapeDtypeStruct((b, t, ATT_WIDTH), F32),
            jax.ShapeDtypeStruct((b, t, KV_WIDTH), F32),
        ],
        compiler_params=_cp(("parallel", "arbitrary")),
        name="swa_prompt",
    )(pa, pa, pa, pa, pa, *tabs, *tabs, qw, kw, sinks)


def _qk_norm_rope_kernel(x_ref, w_ref, c_ref, sa_ref, sb_ref, o_ref):
    gmat = _head_indicator()
    tabs = (c_ref[...], sa_ref[...], sb_ref[...])
    for c in range(x_ref.shape[1] // LANES):
        sl = slice(c * LANES, (c + 1) * LANES)
        o_ref[:, sl] = _norm_rope_chunk(x_ref[:, sl], w_ref[:, sl], *tabs, gmat)


def qk_norm_rope(x, w_row, tabs):
    m, w = x.shape
    full = lambda *shape: pl.BlockSpec(shape, lambda: (0,) * len(shape))
    return pl.pallas_call(
        _qk_norm_rope_kernel,
        in_specs=[full(m, w), full(1, w), full(1, LANES), full(1, LANES), full(1, LANES)],
        out_specs=full(m, w),
        out_shape=jax.ShapeDtypeStruct((m, w), F32),
        name="qk_norm_rope",
    )(x, w_row, *tabs)


def _swa_decode_kernel(q_ref, kn_ref, vn_ref, knt_ref, vnt_ref, ck_ref, cv_ref, sink_ref, o_ref,
                       kw_ref, vw_ref):
    bb = q_ref.shape[0]
    nbuf = ck_ref.shape[2]
    row_kv = lax.broadcasted_iota(jnp.int32, (ATT_HEADS, KV_WIDTH), 0) // ATT_GROUP
    lane_kv = lax.broadcasted_iota(jnp.int32, (ATT_HEADS, KV_WIDTH), 1) // HEAD_DIM
    own = row_kv == lane_kv
    sink = sink_ref[...]
    seq = lax.broadcasted_iota(jnp.int32, knt_ref.shape, 1)
    pos = lax.broadcasted_iota(jnp.int32, (KV_WIDTH, nbuf), 1)
    b0 = pl.program_id(0) * bb
    for b in range(bb):
        q2 = q_ref[b]
        q_exp = jnp.where(own, jnp.concatenate([q2] * ATT_KV_HEADS, axis=1), 0.0)
        k_new, v_new = kn_ref[b], vn_ref[b]
        k_buf, v_buf = ck_ref[b], cv_ref[b]
        s_buf = _dot(q_exp.astype(BF16), k_buf.astype(BF16)) * ATT_SCALE
        s_new = jnp.sum(q_exp * k_new, axis=-1, keepdims=True) * ATT_SCALE
        m = jnp.maximum(jnp.maximum(jnp.max(s_buf, axis=-1, keepdims=True), s_new), sink)
        e_buf = jnp.exp(s_buf - m)
        e_new = jnp.exp(s_new - m)
        inv = 1.0 / (jnp.sum(e_buf, axis=-1, keepdims=True) + e_new + jnp.exp(sink - m))
        o = _dot_nt((e_buf * inv).astype(BF16), v_buf.astype(BF16)) + (e_new * inv) * v_new
        o = jnp.where(own, o, 0.0)
        o_ref[b] = (o[:, 0:HEAD_DIM] + o[:, HEAD_DIM:2 * HEAD_DIM]
                    + o[:, 2 * HEAD_DIM:3 * HEAD_DIM] + o[:, 3 * HEAD_DIM:4 * HEAD_DIM])
        k_col = jnp.sum(jnp.where(seq == b0 + b, knt_ref[...], 0.0), axis=1, keepdims=True)
        v_col = jnp.sum(jnp.where(seq == b0 + b, vnt_ref[...], 0.0), axis=1, keepdims=True)
        kw_ref[b] = jnp.where(pos == nbuf - 1, k_col, pltpu.roll(k_buf, nbuf - 1, axis=1))
        vw_ref[b] = jnp.where(pos == nbuf - 1, v_col, pltpu.roll(v_buf, nbuf - 1, axis=1))


def swa_decode(q, k_new, v_new, cache_kt, cache_vt, sinks, *, bb=8):
    b, _, nbuf = cache_kt.shape
    blk3 = lambda s1, s2: pl.BlockSpec((bb, s1, s2), lambda i: (i, 0, 0))
    whole = pl.BlockSpec((KV_WIDTH, b), lambda i: (0, 0))
    return pl.pallas_call(
        _swa_decode_kernel,
        grid=(b // bb,),
        in_specs=[
            blk3(ATT_HEADS, HEAD_DIM), blk3(1, KV_WIDTH), blk3(1, KV_WIDTH), whole, whole,
            blk3(KV_WIDTH, nbuf), blk3(KV_WIDTH, nbuf),
            pl.BlockSpec((ATT_HEADS, 1), lambda i: (0, 0)),
        ],
        out_specs=[blk3(ATT_HEADS, HEAD_DIM), blk3(KV_WIDTH, nbuf), blk3(KV_WIDTH, nbuf)],
        out_shape=[
            jax.ShapeDtypeStruct((b, ATT_HEADS, HEAD_DIM), F32),
            jax.ShapeDtypeStruct((b, KV_WIDTH, nbuf), F32),
            jax.ShapeDtypeStruct((b, KV_WIDTH, nbuf), F32),
        ],
        compiler_params=_cp(("parallel",)),
        name="swa_decode",
    )(q, k_new, v_new, k_new.reshape(b, KV_WIDTH).T, v_new.reshape(b, KV_WIDTH).T,
      cache_kt, cache_vt, sinks.reshape(ATT_HEADS, 1))


def _head_rms_kernel(x_ref, w_ref, o_ref):
    for h in range(x_ref.shape[1] // XATT_HEAD_DIM):
        sl = slice(h * XATT_HEAD_DIM, (h + 1) * XATT_HEAD_DIM)
        o_ref[:, sl] = _rms_rows(x_ref[:, sl], w_ref[...])


def head_rms(x, w):
    m, wd = x.shape
    return pl.pallas_call(
        _head_rms_kernel,
        in_specs=[pl.BlockSpec((m, wd), lambda: (0, 0)),
                  pl.BlockSpec((1, XATT_HEAD_DIM), lambda: (0, 0))],
        out_specs=pl.BlockSpec((m, wd), lambda: (0, 0)),
        out_shape=jax.ShapeDtypeStruct((m, wd), F32),
        name="head_rms",
    )(x, w.reshape(1, XATT_HEAD_DIM))


def _xattn_prompt_kernel(q_ref, k_ref, v_ref, w_ref, o_ref):
    scale = 1.0 / math.sqrt(XATT_HEAD_DIM)
    for h in range(XATT_HEADS):
        sl = slice(h * XATT_HEAD_DIM, (h + 1) * XATT_HEAD_DIM)
        qn = _rms_rows(q_ref[:, sl], w_ref[...]).astype(BF16)
        s = _dot_nt(qn, k_ref[:, sl].astype(BF16)) * scale
        e = jnp.exp(s - jnp.max(s, axis=-1, keepdims=True))
        p = e / jnp.sum(e, axis=-1, keepdims=True)
        o_ref[:, sl] = _dot(p.astype(BF16), v_ref[:, sl].astype(BF16))


def xattn_prompt(q, mem_k, mem_v, xq_norm_w, *, tq=512):
    b, t, w = q.shape
    n_mem = mem_k.shape[1]
    return pl.pallas_call(
        _xattn_prompt_kernel,
        grid=(b, t // tq),
        in_specs=[
            pl.BlockSpec((None, tq, w), lambda i, j: (i, j, 0)),
            pl.BlockSpec((None, n_mem, w), lambda i, j: (i, 0, 0)),
            pl.BlockSpec((None, n_mem, w), lambda i, j: (i, 0, 0)),
            pl.BlockSpec((1, XATT_HEAD_DIM), lambda i, j: (0, 0)),
        ],
        out_specs=pl.BlockSpec((None, tq, w), lambda i, j: (i, j, 0)),
        out_shape=jax.ShapeDtypeStruct((b, t, w), F32),
        compiler_params=_cp(("parallel", "arbitrary")),
        name="xattn_prompt",
    )(q, mem_k, mem_v, xq_norm_w.reshape(1, XATT_HEAD_DIM))


def _xattn_decode_kernel(q_ref, k_ref, v_ref, w_ref, o_ref):
    bb, rows, _ = q_ref.shape
    n_keys = k_ref.shape[1]
    scale = 1.0 / math.sqrt(XATT_HEAD_DIM)
    own = (lax.broadcasted_iota(jnp.int32, (rows, n_keys), 1) % XATT_HEADS
           == lax.broadcasted_iota(jnp.int32, (rows, n_keys), 0) % XATT_HEADS)
    seqs = range(bb)
    qn = [_rms_rows(q_ref[b], w_ref[...]).astype(BF16) for b in seqs]
    s = [jnp.where(own, _dot_nt(qn[b], k_ref[b].astype(BF16)) * scale, -jnp.inf) for b in seqs]
    e = [jnp.exp(s[b] - jnp.max(s[b], axis=-1, keepdims=True)) for b in seqs]
    p = [(e[b] / jnp.sum(e[b], axis=-1, keepdims=True)).astype(BF16) for b in seqs]
    for b in seqs:
        o_ref[b] = _dot(p[b], v_ref[b].astype(BF16))


def xattn_decode(q_pad, mem_k, mem_v, xq_norm_w, *, bb=8):
    b, rows, _ = q_pad.shape
    n_keys = mem_k.shape[1]
    kv = pl.BlockSpec((bb, n_keys, XATT_HEAD_DIM), lambda i: (i, 0, 0))
    return pl.pallas_call(
        _xattn_decode_kernel,
        grid=(b // bb,),
        in_specs=[pl.BlockSpec((bb, rows, XATT_HEAD_DIM), lambda i: (i, 0, 0)), kv, kv,
                  pl.BlockSpec((1, XATT_HEAD_DIM), lambda i: (0, 0))],
        out_specs=pl.BlockSpec((bb, rows, XATT_HEAD_DIM), lambda i: (i, 0, 0)),
        out_shape=jax.ShapeDtypeStruct((b, rows, XATT_HEAD_DIM), F32),
        compiler_params=_cp(("parallel",)),
        name="xattn_decode",
    )(q_pad, mem_k, mem_v, xq_norm_w.reshape(1, XATT_HEAD_DIM))


LORA_OFF = 3 * RWKV_WIDTH
GATE_OFF = LORA_OFF + DECAY_LORA + AAA_LORA
GATE_PAD = RWKV_PROJ_PAD - GATE_OFF


def _sigmoid(x):
    return 1.0 / (1.0 + jnp.exp(-x))


def _per_chunk(fn, *arrays):
    w = arrays[0].shape[1]
    outs = [fn(*(a[:, c * LANES:(c + 1) * LANES] for a in arrays)) for c in range(w // LANES)]
    return jnp.concatenate(outs, axis=1)


def _rwkv_prep_core(pr, prev, mu, w0, a0, kk_w, ka_w, rk_w, w_lora, w_gate):
    c = RWKV_WIDTH
    gmat = _head_indicator()
    xm = pr + (prev - pr) * mu
    r, k, v = xm[:, 0:c], xm[:, c:2 * c], xm[:, 2 * c:3 * c]
    lora = xm[:, LORA_OFF:LORA_OFF + LANES]
    lane = lax.broadcasted_iota(jnp.int32, lora.shape, 1)
    lora_in = jnp.where(lane < DECAY_LORA, jnp.tanh(lora), lora)
    wa = _dot(lora_in.astype(BF16), w_lora)
    z = -(w0 + wa[:, 0:c])
    softplus = jnp.maximum(z, 0.0) + jnp.log(1.0 + jnp.exp(-jnp.abs(z)))
    log_decay = -jnp.exp(-softplus - 0.5)
    a = _sigmoid(a0 + wa[:, c:2 * c])
    g = _dot(_sigmoid(xm[:, GATE_OFF:GATE_OFF + GATE_PAD]).astype(BF16), w_gate)
    kk = k * kk_w
    norm = jnp.sqrt(_per_chunk(lambda t: _group_sum(t * t, gmat), kk))
    kk = kk / jnp.maximum(norm, 1e-12)
    kp = k * (1.0 + (a - 1.0) * ka_w)
    bonus = _per_chunk(lambda t: _group_sum(t, gmat), r * kp * rk_w) * v
    return r, log_decay, kp, v, kk, kk * a, bonus, g


def _rwkv_prep_seq_kernel(pr_ref, prev0_ref, mu_ref, w0_ref, a0_ref, kkw_ref, kaw_ref, rkw_ref,
                          wl_ref, wg_ref, *refs):
    out_refs, last_ref = refs[:-1], refs[-1]

    @pl.when(pl.program_id(1) == 0)
    def _():
        last_ref[...] = prev0_ref[...]

    pr = pr_ref[...]
    rows = pr.shape[0]
    row = lax.broadcasted_iota(jnp.int32, (rows, 1), 0)
    prev = jnp.where(row == 0, last_ref[...], pltpu.roll(pr, 1, axis=0))
    last_ref[...] = pr[rows - 1:rows, :]
    outs = _rwkv_prep_core(pr, prev, mu_ref[...], w0_ref[...], a0_ref[...], kkw_ref[...],
                           kaw_ref[...], rkw_ref[...], wl_ref[...], wg_ref[...])
    for o_ref, o in zip(out_refs, outs):
        o_ref[...] = o


def _rwkv_prep_tok_kernel(pr_ref, prev_ref, mu_ref, w0_ref, a0_ref, kkw_ref, kaw_ref, rkw_ref,
                          wl_ref, wg_ref, *out_refs):
    outs = _rwkv_prep_core(pr_ref[...], prev_ref[...], mu_ref[...], w0_ref[...], a0_ref[...],
                           kkw_ref[...], kaw_ref[...], rkw_ref[...], wl_ref[...], wg_ref[...])
    for k, (o_ref, o) in enumerate(zip(out_refs, outs)):
        o_ref[...] = o.T if k < N_STEP_VECS else o


def _rwkv_param_specs(index_map):
    c = RWKV_WIDTH
    shapes = [(1, RWKV_PROJ_PAD)] + [(1, c)] * 5 + [(LANES, 2 * c), (GATE_PAD, c)]
    return [pl.BlockSpec(s, index_map) for s in shapes]


def rwkv_prep_seq(pr, prev0, params, *, tm=256):
    b, t, wd = pr.shape
    c = RWKV_WIDTH
    out = jax.ShapeDtypeStruct((b, t, c), F32)
    return pl.pallas_call(
        _rwkv_prep_seq_kernel,
        grid=(b, t // tm),
        in_specs=[pl.BlockSpec((None, tm, wd), lambda i, j: (i, j, 0)),
                  pl.BlockSpec((None, 1, wd), lambda i, j: (i, 0, 0))]
        + _rwkv_param_specs(lambda i, j: (0, 0)),
        out_specs=[pl.BlockSpec((None, tm, c), lambda i, j: (i, j, 0))] * 8,
        out_shape=[out] * 8,
        scratch_shapes=[pltpu.VMEM((1, wd), F32)],
        compiler_params=_cp(("parallel", "arbitrary")),
        name="rwkv_prep_seq",
    )(pr, prev0, *params)


N_STEP_VECS = 6


def rwkv_prep_tok(pr, prev, params):
    m, wd = pr.shape
    c = RWKV_WIDTH
    shapes = [(c, m)] * N_STEP_VECS + [(m, c)] * 2
    return pl.pallas_call(
        _rwkv_prep_tok_kernel,
        grid=(1,),
        in_specs=[pl.BlockSpec((m, wd), lambda i: (0, 0))] * 2
        + _rwkv_param_specs(lambda i: (0, 0)),
        out_specs=[pl.BlockSpec(s, lambda i: (0, 0)) for s in shapes],
        out_shape=[jax.ShapeDtypeStruct(s, F32) for s in shapes],
        compiler_params=_cp(("arbitrary",)),
        name="rwkv_prep_tok",
    )(pr, prev, *params)


def _dot_tn(a, b):
    return lax.dot_general(a, b, (((0,), (0,)), ((), ())), preferred_element_type=F32)


def _rwkv_scan_kernel(r_ref, ld_ref, kp_ref, v_ref, kk_ref, b_ref, y_ref, s_out_ref, s_ref):
    @pl.when(pl.program_id(1) == 0)
    def _():
        s_ref[...] = jnp.zeros_like(s_ref)

    n = CHUNK
    ti = lax.broadcasted_iota(jnp.int32, (n, n), 0)
    si = lax.broadcasted_iota(jnp.int32, (n, n), 1)
    strict, incl = si < ti, si <= ti
    tri = jnp.where(incl, 1.0, 0.0).astype(BF16)

    ld = ld_ref[...]
    l1, l2, l3 = _split3(ld)
    lc = _dot(tri, l1) + _dot(tri, l2) + _dot(tri, l3)
    lc_end = lc[n - 1:n, :]
    e_neg = jnp.exp(-lc)
    kk, b, kp = kk_ref[...], b_ref[...], kp_ref[...]
    a_t = (-kk * jnp.exp(lc - ld)).astype(BF16)
    b_t = (b * e_neg).astype(BF16)
    k_t = (kp * e_neg).astype(BF16)
    r_t = (r_ref[...] * jnp.exp(lc)).astype(BF16)
    to_end = jnp.exp(lc_end - lc)
    b_e = (b * to_end).astype(BF16)
    k_e = (kp * to_end).astype(BF16)
    v_b = v_ref[...].astype(BF16)
    g_end = jnp.exp(lc_end)

    heads = range(RWKV_HEADS)
    sl = [slice(h * n, (h + 1) * n) for h in heads]
    gm = [_dot_nt(jnp.concatenate([a_t[:, sl[h]], r_t[:, sl[h]]], axis=0),
                  jnp.concatenate([b_t[:, sl[h]], k_t[:, sl[h]]], axis=0)) for h in heads]
    n_ab = [jnp.where(strict, gm[h][0:n, 0:n], 0.0).astype(BF16) for h in heads]
    l_ak = [jnp.where(strict, gm[h][0:n, n:2 * n], 0.0).astype(BF16) for h in heads]
    p_rb = [jnp.where(incl, gm[h][n:2 * n, 0:n], 0.0).astype(BF16) for h in heads]
    p_rk = [jnp.where(incl, gm[h][n:2 * n, n:2 * n], 0.0).astype(BF16) for h in heads]
    s0 = [s_ref[h] for h in heads]
    s0_b = [s0[h].astype(BF16) for h in heads]
    u = [_dot_nt(a_t[:, sl[h]], s0_b[h]) + _dot(l_ak[h], v_b[:, sl[h]]) for h in heads]
    pw = n_ab
    for step in range(6):
        u = [u[h] + _dot(pw[h], u[h].astype(BF16)) for h in heads]
        if step < 5:
            pw = [_dot(pw[h], pw[h]).astype(BF16) for h in heads]
    u_b = [u[h].astype(BF16) for h in heads]
    y = [_dot_nt(r_t[:, sl[h]], s0_b[h]) + _dot(p_rb[h], u_b[h]) + _dot(p_rk[h], v_b[:, sl[h]])
         for h in heads]
    s_new = [s0[h] * g_end[:, sl[h]] + _dot_tn(
        jnp.concatenate([u_b[h], v_b[:, sl[h]]], axis=0),
        jnp.concatenate([b_e[:, sl[h]], k_e[:, sl[h]]], axis=0)) for h in heads]
    for h in heads:
        y_ref[:, sl[h]] = y[h]
        s_ref[h] = s_new[h]
        s_out_ref[h] = s_new[h]


def rwkv_scan(r, ld, kp, v, kk, b):
    bsz, t, c = r.shape
    blk = pl.BlockSpec((None, CHUNK, c), lambda i, j: (i, j, 0))
    st = pl.BlockSpec((None, RWKV_HEADS, RWKV_HEAD_DIM, RWKV_HEAD_DIM), lambda i, j: (i, 0, 0, 0))
    return pl.pallas_call(
        _rwkv_scan_kernel,
        grid=(bsz, t // CHUNK),
        in_specs=[blk] * 6,
        out_specs=[blk, st],
        out_shape=[jax.ShapeDtypeStruct((bsz, t, c), F32),
                   jax.ShapeDtypeStruct((bsz, RWKV_HEADS, RWKV_HEAD_DIM, RWKV_HEAD_DIM), F32)],
        scratch_shapes=[pltpu.VMEM((RWKV_HEADS, RWKV_HEAD_DIM, RWKV_HEAD_DIM), F32)],
        compiler_params=_cp(("parallel", "arbitrary")),
        name="rwkv_scan",
    )(r, ld, kp, v, kk, b)


STEP_UNROLL = 8


def _rwkv_step_kernel(r_ref, ld_ref, kp_ref, v_ref, kk_ref, b_ref, s_ref, y_ref, s_out_ref):
    n = RWKV_HEAD_DIM
    neg_kk, decay = -kk_ref[...], jnp.exp(ld_ref[...])
    b_mat, kp_mat, r_mat = b_ref[...], kp_ref[...], r_ref[...]

    def body(i, carry):
        v0 = pl.multiple_of(i * STEP_UNROLL, STEP_UNROLL)
        v_rows = v_ref[pl.ds(v0, STEP_UNROLL), :]
        rows = range(STEP_UNROLL)
        s = [s_ref[v0 + j] for j in rows]
        sa = [jnp.sum(s[j] * neg_kk, axis=0, keepdims=True) for j in rows]
        s_new = [s[j] * decay + sa[j] * b_mat + v_rows[j:j + 1, :] * kp_mat for j in rows]
        y = [jnp.sum(s_new[j] * r_mat, axis=0, keepdims=True) for j in rows]
        for j in rows:
            s_out_ref[v0 + j] = s_new[j]
        y_ref[pl.ds(v0, STEP_UNROLL), :] = jnp.concatenate(y, axis=0)
        return carry

    lax.fori_loop(0, n // STEP_UNROLL, body, 0)


def rwkv_step(r, ld, kp, v, kk, b, state_t):
    _, nh, n, _, bsz = state_t.shape
    vec = pl.BlockSpec((n, bsz), lambda h: (h, 0))
    st = pl.BlockSpec((None, None, n, n, bsz), lambda h: (0, h, 0, 0, 0))
    return pl.pallas_call(
        _rwkv_step_kernel,
        grid=(nh,),
        in_specs=[vec] * 6 + [st],
        out_specs=[vec, st],
        out_shape=[jax.ShapeDtypeStruct((nh * n, bsz), F32),
                   jax.ShapeDtypeStruct(state_t.shape, F32)],
        compiler_params=_cp(("parallel",)),
        name="rwkv_step",
    )(r, ld, kp, v, kk, b, state_t)


def _rwkv_post_kernel(y_ref, bonus_ref, g_ref, lnw_ref, lnb_ref, o_ref, *, y_channel_major):
    gmat = _head_indicator()
    inv = 1.0 / RWKV_HEAD_DIM
    for c in range(o_ref.shape[1] // LANES):
        sl = slice(c * LANES, (c + 1) * LANES)
        y = y_ref[sl, :].T if y_channel_major else y_ref[:, sl]
        d = y - _group_sum(y, gmat) * inv
        var = _group_sum(d * d, gmat) * inv
        yn = d * lax.rsqrt(var + GN_EPS) * lnw_ref[:, sl] + lnb_ref[:, sl]
        o_ref[:, sl] = (yn + bonus_ref[:, sl]) * g_ref[:, sl]


def rwkv_post(y, bonus, g, ln_w, ln_b, *, tm, y_channel_major=False):
    m, c = bonus.shape
    blk = pl.BlockSpec((tm, c), lambda i: (i, 0))
    y_blk = pl.BlockSpec((c, tm), lambda i: (0, i)) if y_channel_major else blk
    vec = pl.BlockSpec((1, c), lambda i: (0, 0))
    return pl.pallas_call(
        functools.partial(_rwkv_post_kernel, y_channel_major=y_channel_major),
        grid=(m // tm,),
        in_specs=[y_blk, blk, blk, vec, vec],
        out_specs=blk,
        out_shape=jax.ShapeDtypeStruct((m, c), F32),
        compiler_params=_cp(("parallel",)),
        name="rwkv_post",
    )(y, bonus, g, ln_w.reshape(1, c), ln_b.reshape(1, c))


ROUTER_LANES = LANES
ROW_TILES = D_MODEL // LANES


def _rows_to_tiles(ref, x):
    rows = x.shape[0]
    for j in range(ROW_TILES):
        ref[pl.ds(j, rows, stride=ROW_TILES), :] = x[:, j * LANES:(j + 1) * LANES]


def _tiles_to_rows(ref, rows):
    return jnp.concatenate(
        [ref[pl.ds(j, rows, stride=ROW_TILES), :] for j in range(ROW_TILES)], axis=1)


def _router_kernel(ha_ref, hb_ref, lnw_ref, whi_ref, wlo_ref, bias_ref, u_ref, idx_ref, gate_ref,
                   *, steps_a):
    use_a = pl.program_id(0) < steps_a
    h = jnp.where(use_a, ha_ref[...], hb_ref[...])
    u = _rms_rows(h, lnw_ref[...])
    _rows_to_tiles(u_ref, u)
    u_hi, u_lo = _split2(u)
    w_hi = whi_ref[...]
    logits = _dot(u_hi, w_hi) + _dot(u_lo, w_hi) + _dot(u_hi, wlo_ref[...]) + bias_ref[...]
    lane = lax.broadcasted_iota(jnp.int32, logits.shape, 1)
    neg = -jnp.inf

    def first_max(x):
        m = jnp.max(x, axis=1, keepdims=True)
        return m, jnp.min(jnp.where(x == m, lane, ROUTER_LANES), axis=1, keepdims=True)

    gl = jnp.where(lane < N_EXPERT_GROUPS, logits, neg)
    g_max, g_idx = first_max(gl)
    g_gate = 1.0 / jnp.sum(jnp.exp(gl - g_max), axis=1, keepdims=True)
    lo = N_EXPERT_GROUPS + g_idx * EXPERTS_PER_GROUP
    el = jnp.where((lane >= lo) & (lane < lo + EXPERTS_PER_GROUP), logits, neg)
    v1, i1 = first_max(el)
    v2, i2 = first_max(jnp.where(lane == i1, neg, el))
    e2 = jnp.exp(v2 - v1)
    w1 = g_gate / (1.0 + e2)
    w2 = g_gate * e2 / (1.0 + e2)
    idx_ref[...] = jnp.where(lane == 0, i1 - N_EXPERT_GROUPS,
                             jnp.where(lane == 1, i2 - N_EXPERT_GROUPS, 0))
    gate_ref[...] = jnp.where(lane == 0, w1, jnp.where(lane == 1, w2, 0.0))


def moe_router(h_a, h_b, ln_w, w_hi, w_lo, bias, *, tm):
    (ma, d), mb = h_a.shape, h_b.shape[0]
    assert ma % tm == 0 and mb % tm == 0
    steps_a, steps_b = ma // tm, mb // tm
    m = ma + mb
    const = lambda r, w: pl.BlockSpec((r, w), lambda i: (0, 0))
    row = lambda w: pl.BlockSpec((tm, w), lambda i: (i, 0))
    return pl.pallas_call(
        functools.partial(_router_kernel, steps_a=steps_a),
        grid=(steps_a + steps_b,),
        in_specs=[pl.BlockSpec((tm, d), lambda i: (jnp.minimum(i, steps_a - 1), 0)),
                  pl.BlockSpec((tm, d), lambda i: (jnp.maximum(i - steps_a, 0), 0)),
                  const(1, d), const(d, ROUTER_LANES), const(d, ROUTER_LANES),
                  const(1, ROUTER_LANES)],
        out_specs=[pl.BlockSpec((tm * (d // LANES), LANES), lambda i: (i, 0)),
                   row(ROUTER_LANES), row(ROUTER_LANES)],
        out_shape=[jax.ShapeDtypeStruct((m * (d // LANES), LANES), F32),
                   jax.ShapeDtypeStruct((m, ROUTER_LANES), jnp.int32),
                   jax.ShapeDtypeStruct((m, ROUTER_LANES), F32)],
        compiler_params=_cp(("arbitrary",)),
        name="moe_router",
    )(h_a, h_b, ln_w.reshape(1, d), w_hi, w_lo, bias)


X_SLOTS = 3
Y_SLOTS = 2


W_SLOTS = 3
DMA_QUEUES = 2


def _moe_expert_kernel(run_ref, rexp_ref, nused_ref, tok0_ref, tok1_ref, tok2_ref, dst_ref,
                       roww_ref, u_hbm, wg_hbm, wu_hbm, wd_hbm, y_hbm, xbuf, ybuf, wg_f, wu_f,
                       wd_f, wg_b, wu_b, wd_b, sem_in, sem_out, sem_w):
    i = pl.program_id(0)
    n_used, n_runs = nused_ref[0], nused_ref[1]
    tile_rows = MOE_BLOCK * ROW_TILES
    pad_base = y_hbm.shape[0] - Y_SLOTS * tile_rows
    run = run_ref[i]

    def weight_copies(k):
        e, s = rexp_ref[jnp.minimum(k, n_runs - 1)], lax.rem(k, W_SLOTS)
        copies = []
        for hbm, buf in ((wg_hbm, wg_f), (wu_hbm, wu_f), (wd_hbm, wd_f)):
            rows = buf.shape[1] // 2
            for part in range(2):
                sl = pl.ds(part * rows, rows)
                copies.append((pltpu.make_async_copy(hbm.at[e, sl], buf.at[s, sl], sem_w.at[s]),
                               part))
        return copies

    def gather_block(idx_ref, x_slot):
        for r in range(MOE_BLOCK):
            pltpu.make_async_copy(u_hbm.at[pl.ds(idx_ref[0, 0, r], ROW_TILES)],
                                  xbuf.at[x_slot, pl.ds(r * ROW_TILES, ROW_TILES)],
                                  sem_in.at[x_slot]).start(priority=r % DMA_QUEUES)

    def scatter_rows(idx_ref, y_slot, rows):
        for r in rows:
            pltpu.make_async_copy(ybuf.at[y_slot, pl.ds(r * ROW_TILES, ROW_TILES)],
                                  y_hbm.at[pl.ds(idx_ref[0, 0, r], ROW_TILES)],
                                  sem_out.at[y_slot]).start(priority=r % DMA_QUEUES)

    def gather_wait(x_slot):
        pltpu.make_async_copy(u_hbm.at[pl.ds(0, tile_rows)], xbuf.at[x_slot],
                              sem_in.at[x_slot]).wait()

    def scatter_wait(y_slot):
        pltpu.make_async_copy(ybuf.at[y_slot], y_hbm.at[pl.ds(0, tile_rows)],
                              sem_out.at[y_slot]).wait()

    @pl.when(i == 0)
    def _():
        ybuf[0] = jnp.zeros(ybuf.shape[1:], F32)
        for s in range(Y_SLOTS):
            pltpu.make_async_copy(ybuf.at[0], y_hbm.at[pl.ds(pad_base + s * tile_rows, tile_rows)],
                                  sem_out.at[s]).start()
        for k in range(W_SLOTS - 1):
            for cp, queue in weight_copies(k):
                cp.start(priority=queue)
        gather_block(tok0_ref, 0)
        gather_block(tok1_ref, 1)
        for s in range(Y_SLOTS):
            scatter_wait(s)

    @pl.when(i < n_used)
    def _():
        x_slot = lax.rem(i, X_SLOTS)
        y_slot = lax.rem(i, Y_SLOTS)

        @pl.when((i == 0) | (run != run_ref[jnp.maximum(i - 1, 0)]))
        def _():
            for cp, _ in weight_copies(run):
                cp.wait()
            w_slot = lax.rem(run, W_SLOTS)
            wg_b[...] = wg_f[w_slot].astype(BF16)
            wu_b[...] = wu_f[w_slot].astype(BF16)
            wd_b[...] = wd_f[w_slot].astype(BF16)
            for cp, queue in weight_copies(run + W_SLOTS - 1):
                cp.start(priority=queue)

        gather_wait(x_slot)

        @pl.when(i >= Y_SLOTS)
        def _():
            scatter_wait(y_slot)

        x = _tiles_to_rows(xbuf.at[x_slot], MOE_BLOCK).astype(BF16)
        hg = _dot(x, wg_b[...])
        hu = _dot(x, wu_b[...])
        act = (hg * _sigmoid(hg) * hu).astype(BF16)
        y = _dot(act, wd_b[...]) * roww_ref[...]
        _rows_to_tiles(ybuf.at[y_slot], y)
        scatter_rows(dst_ref, y_slot, range(MOE_BLOCK))
        gather_block(tok2_ref, lax.rem(i + 2, X_SLOTS))

        @pl.when(i == n_used - 1)
        def _():
            scatter_wait(y_slot)

            @pl.when(i >= 1)
            def _():
                scatter_wait(1 - y_slot)

            gather_wait(lax.rem(i + 1, X_SLOTS))
            gather_wait(lax.rem(i + 2, X_SLOTS))
            for k in range(W_SLOTS - 1):
                for cp, _ in weight_copies(n_runs + k):
                    cp.wait()


def moe_experts(u_all, row_src, row_dst, row_w, block_run, run_exp, n_used_runs, w_gate, w_up,
                w_down, n_assign):
    d, ff = w_gate.shape[1], w_gate.shape[2]
    n_blocks = row_src.shape[0]
    tile_rows = MOE_BLOCK * ROW_TILES
    smem_blk = lambda off: pl.BlockSpec(
        (1, 1, MOE_BLOCK), lambda i, *_: (jnp.clip(i + off, 0, n_blocks - 1), 0, 0),
        memory_space=pltpu.SMEM)
    hbm = pl.BlockSpec(memory_space=pl.ANY)
    grid_spec = pltpu.PrefetchScalarGridSpec(
        num_scalar_prefetch=3,
        grid=(n_blocks,),
        in_specs=[
            smem_blk(0), smem_blk(1), smem_blk(2), smem_blk(0),
            pl.BlockSpec((MOE_BLOCK, 1), lambda i, *_: (i, 0)),
            hbm, hbm, hbm, hbm,
        ],
        out_specs=hbm,
        scratch_shapes=[
            pltpu.VMEM((X_SLOTS, tile_rows, LANES), F32),
            pltpu.VMEM((Y_SLOTS, tile_rows, LANES), F32),
            pltpu.VMEM((W_SLOTS, d, ff), F32), pltpu.VMEM((W_SLOTS, d, ff), F32),
            pltpu.VMEM((W_SLOTS, ff, d), F32),
            pltpu.VMEM((d, ff), BF16), pltpu.VMEM((d, ff), BF16), pltpu.VMEM((ff, d), BF16),
            pltpu.SemaphoreType.DMA((X_SLOTS,)), pltpu.SemaphoreType.DMA((Y_SLOTS,)),
            pltpu.SemaphoreType.DMA((W_SLOTS,)),
        ],
    )
    y_rows = (n_assign + Y_SLOTS * MOE_BLOCK) * ROW_TILES
    return pl.pallas_call(
        _moe_expert_kernel,
        grid_spec=grid_spec,
        out_shape=jax.ShapeDtypeStruct((y_rows, LANES), F32),
        compiler_params=_cp(("arbitrary",), vmem=MOE_VMEM_LIMIT),
        name="moe_experts",
    )(block_run, run_exp, n_used_runs, row_src, row_src, row_src, row_dst, row_w, u_all,
      w_gate, w_up, w_down)


def _moe_combine_kernel(h_ref, y0_ref, y1_ref, o_ref):
    rows = h_ref.shape[0]
    o_ref[...] = h_ref[...] + (_tiles_to_rows(y0_ref, rows) + _tiles_to_rows(y1_ref, rows))


def moe_combine(h, y_slots, row_off, slot_stride, *, tm):
    m, d = h.shape
    assert row_off % tm == 0 and slot_stride % tm == 0
    off0, off1 = row_off // tm, (row_off + slot_stride) // tm
    return pl.pallas_call(
        _moe_combine_kernel,
        grid=(m // tm,),
        in_specs=[pl.BlockSpec((tm, d), lambda i: (i, 0)),
                  pl.BlockSpec((tm * ROW_TILES, LANES), lambda i: (i + off0, 0)),
                  pl.BlockSpec((tm * ROW_TILES, LANES), lambda i: (i + off1, 0))],
        out_specs=pl.BlockSpec((tm, d), lambda i: (i, 0)),
        out_shape=jax.ShapeDtypeStruct((m, d), F32),
        compiler_params=_cp(("parallel",)),
        name="moe_combine",
    )(h, y_slots, y_slots)


def moe_dispatch(e_idx, gates, slot_stride):
    m = e_idx.shape[0]
    a = m * TOP_K
    e_flat = e_idx.reshape(a)
    order = jnp.argsort(e_flat, stable=True).astype(jnp.int32)
    counts = jnp.sum(e_flat[:, None] == jnp.arange(N_EXPERTS, dtype=jnp.int32)[None, :],
                     axis=0, dtype=jnp.int32)
    pad_counts = (counts + MOE_BLOCK - 1) // MOE_BLOCK * MOE_BLOCK
    starts = jnp.cumsum(counts) - counts
    pad_ends = jnp.cumsum(pad_counts)
    pad_starts = pad_ends - pad_counts
    n_blocks = a // MOE_BLOCK + N_EXPERTS
    p = n_blocks * MOE_BLOCK
    n_used = (pad_ends[-1] // MOE_BLOCK).astype(jnp.int32)
    blk = jnp.arange(n_blocks, dtype=jnp.int32)
    blk_start = jnp.minimum(blk, n_used - 1) * MOE_BLOCK
    block_exp = jnp.minimum(jnp.searchsorted(pad_ends, blk_start, side='right'),
                            N_EXPERTS - 1).astype(jnp.int32)
    in_exp = blk * MOE_BLOCK - pad_starts[block_exp]
    row_cnt = jnp.where(blk < n_used, jnp.clip(counts[block_exp] - in_exp, 0, MOE_BLOCK), 0)
    lane = jnp.arange(MOE_BLOCK, dtype=jnp.int32)[None, :]
    valid = lane < row_cnt[:, None]
    src = jnp.clip((starts[block_exp] + in_exp)[:, None] + lane, 0, a - 1)
    assign = order[src]
    row_tok = jnp.where(valid, assign // TOP_K, 0)
    pad_dst = TOP_K * slot_stride + (blk % Y_SLOTS)[:, None] * MOE_BLOCK + lane
    row_dst = jnp.where(valid, (assign % TOP_K) * slot_stride + assign // TOP_K, pad_dst)
    row_w = jnp.where(valid, gates.reshape(a)[assign], 0.0)
    as_blocks = lambda x: (x * ROW_TILES).astype(jnp.int32).reshape(n_blocks, 1, MOE_BLOCK)
    has_rows = counts > 0
    run_exp = jnp.argsort(~has_rows, stable=True).astype(jnp.int32)
    block_run = (jnp.cumsum(has_rows) - 1)[block_exp].astype(jnp.int32)
    n_used_runs = jnp.stack([n_used, jnp.sum(has_rows, dtype=jnp.int32)])
    return (as_blocks(row_tok), as_blocks(row_dst), row_w.reshape(p, 1), block_run, run_exp,
            n_used_runs)


def rwkv_params(rw_mu, rw_w0, rw_w2, rw_a0, rw_a2, rw_g2, rw_k_k, rw_k_a, rw_r_k):
    c = RWKV_WIDTH
    mu = jnp.pad(rw_mu, (0, RWKV_PROJ_PAD - RWKV_PROJ)).reshape(1, RWKV_PROJ_PAD)
    w_lora = jnp.zeros((LANES, 2 * c), F32)
    w_lora = w_lora.at[0:DECAY_LORA, 0:c].set(rw_w2).at[DECAY_LORA:LANES, c:2 * c].set(rw_a2)
    w_gate = jnp.pad(rw_g2, ((0, GATE_PAD - GATE_LORA), (0, 0)))
    vec = lambda x: x.reshape(1, c)
    return (mu, vec(rw_w0), vec(rw_a0), vec(rw_k_k), vec(rw_k_a), vec(rw_r_k),
            w_lora.astype(BF16), w_gate.astype(BF16))


def _token_tiles(m):
    return (1024, 512) if m % 1024 == 0 else (m, m)


def _dense_front(x2d, wts, tm):
    pa = norm_matmul(x2d, wts['ln1_w'], wts['w_att'], tm=tm, tn=512)
    pr = norm_matmul(x2d, wts['ln1_w'], wts['w_rw'], tm=tm, tn=RWKV_PROJ_PAD // 3)
    return pa, pr


def _dense_back(x2d, att2d, rw2d, wts, xattn_fn, tm):
    h1 = matmul_residual([att2d, rw2d], [wts['w_out_a'], wts['w_out_r']], x2d, tm=tm, tn=512)
    qx = norm_matmul(h1, wts['ln2_w'], wts['xq_w'], tm=tm, tn=XATT_WIDTH)
    ox = xattn_fn(qx)
    return matmul_residual([ox], [wts['xo_w']], h1, tm=tm, tn=512)


def kernel(x_prompt, x_sample, cache_win_k, cache_win_v, state_wkv, state_shift, cache_mem_k, cache_mem_v, mem_prompt, ln1_w, w_in, q_norm_w, k_norm_w, attn_sinks, rw_mu, rw_w0, rw_w2, rw_a0, rw_a2, rw_g2, rw_k_k, rw_k_a, rw_r_k, rw_ln_w, rw_ln_b, w_out, ln2_w, mem_norm_w, xq_w, xkv_w, xq_norm_w, xk_norm_w, xo_w, ln3_w, router_group_w, router_group_b, router_expert_w, router_expert_b, exp_w_gate, exp_w_up, exp_w_down):
    assert w_in.shape[0] == 1, "single-layer stack"
    bp, seq, d = x_prompt.shape
    bs = x_sample.shape[0]
    mp = bp * seq
    c = RWKV_WIDTH

    router_w = jnp.concatenate(
        [router_group_w[0], router_expert_w[0],
         jnp.zeros((d, ROUTER_LANES - N_EXPERT_GROUPS - N_EXPERTS), F32)], axis=1)
    router_hi = router_w.astype(BF16)
    wts = {
        'ln1_w': ln1_w[0], 'ln2_w': ln2_w[0], 'ln3_w': ln3_w[0],
        'w_att': w_in[0][:, :ATT_PROJ].astype(BF16),
        'w_rw': jnp.pad(w_in[0][:, ATT_PROJ:],
                        ((0, 0), (0, RWKV_PROJ_PAD - RWKV_PROJ))).astype(BF16),
        'w_out_a': w_out[0][:ATT_WIDTH].astype(BF16),
        'w_out_r': w_out[0][ATT_WIDTH:].astype(BF16),
        'xq_w': xq_w[0].astype(BF16), 'xo_w': xo_w[0].astype(BF16),
        'router_hi': router_hi,
        'router_lo': (router_w - router_hi.astype(F32)).astype(BF16),
        'router_b': jnp.pad(jnp.concatenate([router_group_b[0], router_expert_b[0]]),
                            (0, ROUTER_LANES - N_EXPERT_GROUPS - N_EXPERTS)).reshape(1, -1),
    }
    rw_par = rwkv_params(rw_mu[0], rw_w0[0], rw_w2[0], rw_a0[0], rw_a2[0], rw_g2[0],
                         rw_k_k[0], rw_k_a[0], rw_r_k[0])

    tm_p, te_p = _token_tiles(mp)
    xp = x_prompt.reshape(mp, d)
    pa, pr = _dense_front(xp, wts, tm_p)
    pa3 = pa.reshape(bp, seq, ATT_PROJ)
    pr3 = pr.reshape(bp, seq, RWKV_PROJ_PAD)
    tabs_p = rope_tables(jnp.arange(seq, dtype=jnp.int32))
    att_p, kn_p = swa_prompt(pa3, tabs_p, q_norm_w[0], k_norm_w[0], attn_sinks[0])
    prep = rwkv_prep_seq(pr3, jnp.zeros((bp, 1, RWKV_PROJ_PAD), F32), rw_par)
    r, ld, kp, v, kk, b, bonus, g = prep
    y_p, wkv_p = rwkv_scan(r, ld, kp, v, kk, b)
    rw_p = rwkv_post(y_p.reshape(mp, c), bonus.reshape(mp, c), g.reshape(mp, c),
                     rw_ln_w[0], rw_ln_b[0], tm=te_p)

    n_mem = mem_prompt.shape[1]
    kv_mem = norm_matmul(mem_prompt.reshape(bp * n_mem, d), mem_norm_w[0],
                         xkv_w[0].astype(BF16), tm=bp * n_mem, tn=512)
    mem_k = head_rms(kv_mem[:, :XATT_WIDTH], xk_norm_w[0])
    mem_v = kv_mem[:, XATT_WIDTH:]
    mem_k3 = mem_k.reshape(bp, n_mem, XATT_WIDTH)
    mem_v3 = mem_v.reshape(bp, n_mem, XATT_WIDTH)

    def xattn_p(qx):
        return xattn_prompt(qx.reshape(bp, seq, XATT_WIDTH), mem_k3, mem_v3,
                            xq_norm_w[0]).reshape(mp, XATT_WIDTH)

    h2_p = _dense_back(xp, att_p.reshape(mp, ATT_WIDTH), rw_p, wts, xattn_p, tm_p)

    tm_s, te_s = _token_tiles(bs)
    xs = x_sample.reshape(bs, d)
    sa, sr = _dense_front(xs, wts, tm_s)
    tabs_s = rope_tables(PAST_LEN + jnp.arange(1, dtype=jnp.int32))
    qk_w = jnp.concatenate([jnp.tile(q_norm_w[0], ATT_HEADS),
                            jnp.tile(k_norm_w[0], ATT_KV_HEADS)]).reshape(1, -1)
    qk = qk_norm_rope(sa[:, :ATT_WIDTH + KV_WIDTH], qk_w, tabs_s)
    nbuf = cache_win_k.shape[2]

    def feature_major(cache):
        return jnp.transpose(cache, (0, 1, 3, 4, 2)).reshape(bs, KV_WIDTH, nbuf)

    def position_major(win):
        return jnp.transpose(win.reshape(1, bs, ATT_KV_HEADS, HEAD_DIM, nbuf), (0, 1, 4, 2, 3))

    att_s, win_k, win_v = swa_decode(
        qk[:, :ATT_WIDTH].reshape(bs, ATT_HEADS, HEAD_DIM),
        qk[:, ATT_WIDTH:].reshape(bs, 1, KV_WIDTH),
        sa[:, ATT_WIDTH + KV_WIDTH:].reshape(bs, 1, KV_WIDTH),
        feature_major(cache_win_k), feature_major(cache_win_v), attn_sinks[0])
    shift_prev = jnp.pad(state_shift[0], ((0, 0), (0, RWKV_PROJ_PAD - RWKV_PROJ)))
    r, ld, kp, v, kk, b, bonus, g = rwkv_prep_tok(sr, shift_prev, rw_par)
    y_s, wkv_s = rwkv_step(r, ld, kp, v, kk, b, jnp.transpose(state_wkv, (0, 2, 3, 4, 1)))
    wkv_s = jnp.transpose(wkv_s, (0, 4, 1, 2, 3))
    rw_s = rwkv_post(y_s, bonus, g, rw_ln_w[0], rw_ln_b[0], tm=te_s, y_channel_major=True)
    def xattn_s(qx):
        q_pad = jnp.pad(qx.reshape(bs, XATT_HEADS, XATT_HEAD_DIM), ((0, 0), (0, 4), (0, 0)))
        rows_of = lambda c: c.reshape(bs, n_mem * XATT_HEADS, XATT_HEAD_DIM)
        o = xattn_decode(q_pad, rows_of(cache_mem_k), rows_of(cache_mem_v), xq_norm_w[0])
        return o[:, :XATT_HEADS].reshape(bs, XATT_WIDTH)

    h2_s = _dense_back(xs, att_s.reshape(bs, ATT_WIDTH), rw_s, wts, xattn_s, tm_s)

    m_all = mp + bs
    slot_stride = m_all
    tc = math.gcd(mp, bs, 512)
    u_all, idx_all, gate_all = moe_router(h2_p, h2_s, wts['ln3_w'], wts['router_hi'],
                                          wts['router_lo'], wts['router_b'], tm=tc)
    row_src, row_dst, row_w, block_run, run_exp, n_used_runs = moe_dispatch(
        idx_all[:, :TOP_K], gate_all[:, :TOP_K], slot_stride)
    y_slots = moe_experts(u_all, row_src, row_dst, row_w, block_run, run_exp, n_used_runs,
                          exp_w_gate[0], exp_w_up[0], exp_w_down[0], TOP_K * slot_stride)
    out_p = moe_combine(h2_p, y_slots, 0, slot_stride, tm=tc)
    out_s = moe_combine(h2_s, y_slots, mp, slot_stride, tm=tc)

    win = min(WINDOW, seq)
    kv_shape = (1, bp, win, ATT_KV_HEADS, HEAD_DIM)
    return (
        out_p.reshape(bp, seq, d),
        out_s.reshape(bs, 1, d),
        kn_p[:, seq - win:].reshape(kv_shape),
        pa3[:, seq - win:, ATT_WIDTH + KV_WIDTH:].reshape(kv_shape),
        wkv_p[None],
        pr3[:, seq - 1, :RWKV_PROJ][None],
        mem_k3.reshape(1, bp, n_mem, XATT_HEADS, XATT_HEAD_DIM),
        mem_v3.reshape(1, bp, n_mem, XATT_HEADS, XATT_HEAD_DIM),
        position_major(win_k),
        position_major(win_v),
        wkv_s,
        sr[:, :RWKV_PROJ].reshape(1, bs, RWKV_PROJ),
    )
```

```python
import functools
import math

import jax
import jax.numpy as jnp
from jax import lax
from jax.experimental import pallas as pl
from jax.experimental.pallas import tpu as pltpu

F32 = jnp.float32
BF16 = jnp.bfloat16

D_MODEL = 2048
HEAD_DIM = 64
ATT_HEADS = 16
ATT_KV_HEADS = 4
ATT_GROUP = ATT_HEADS // ATT_KV_HEADS
ATT_WIDTH = ATT_HEADS * HEAD_DIM
KV_WIDTH = ATT_KV_HEADS * HEAD_DIM
ATT_PROJ = ATT_WIDTH + 2 * KV_WIDTH
WINDOW = 128
ATT_SCALE = HEAD_DIM ** -0.5
ROPE_THETA = 500000.0
ROT_DIM = HEAD_DIM // 4
PAST_LEN = 16384

RWKV_WIDTH = 1024
RWKV_HEAD_DIM = 64
RWKV_HEADS = 16
DECAY_LORA = 64
AAA_LORA = 64
GATE_LORA = 160
RWKV_PROJ = 3 * RWKV_WIDTH + DECAY_LORA + AAA_LORA + GATE_LORA
RWKV_PROJ_PAD = 3456

N_MEM = 256
XATT_HEADS = 4
XATT_HEAD_DIM = 128
XATT_WIDTH = XATT_HEADS * XATT_HEAD_DIM

N_EXPERT_GROUPS = 8
EXPERTS_PER_GROUP = 8
N_EXPERTS = 64
TOP_K = 2
EXPERT_FF = D_MODEL // 4
MOE_BLOCK = 128

RMS_EPS = 1e-6
GN_EPS = 64e-5

LANES = 128
CHUNK = 64
VMEM_LIMIT = 56 * 1024 * 1024
MOE_VMEM_LIMIT = 60 * 1024 * 1024


def _cp(sem, vmem=VMEM_LIMIT):
    return pltpu.CompilerParams(dimension_semantics=sem, vmem_limit_bytes=vmem)


def _rms_rows(x, w):
    ms = jnp.mean(x * x, axis=-1, keepdims=True)
    return x * lax.rsqrt(ms + RMS_EPS) * w


def _split2(x):
    hi = x.astype(BF16)
    lo = (x - hi.astype(F32)).astype(BF16)
    return hi, lo


def _split3(x):
    h1 = x.astype(BF16)
    r1 = x - h1.astype(F32)
    h2 = r1.astype(BF16)
    h3 = (r1 - h2.astype(F32)).astype(BF16)
    return h1, h2, h3


def _dot(a, b):
    return jnp.dot(a, b, preferred_element_type=F32)


def _dot_nt(a, b):
    return lax.dot_general(a, b, (((1,), (1,)), ((), ())), preferred_element_type=F32)


def _group_sum(x, gmat):
    hi, lo = _split2(x)
    return _dot(hi, gmat) + _dot(lo, gmat)


def _head_indicator():
    r = lax.broadcasted_iota(jnp.int32, (LANES, LANES), 0) // HEAD_DIM
    c = lax.broadcasted_iota(jnp.int32, (LANES, LANES), 1) // HEAD_DIM
    return jnp.where(r == c, 1.0, 0.0).astype(BF16)


def _norm_mm_kernel(x_ref, lnw_ref, w_ref, o_ref, xn_ref):
    @pl.when(pl.program_id(1) == 0)
    def _():
        xn_ref[...] = _rms_rows(x_ref[...], lnw_ref[...]).astype(BF16)

    o_ref[...] = _dot(xn_ref[...], w_ref[...])


def norm_matmul(x, ln_w, w_bf16, *, tm, tn):
    m, k = x.shape
    n = w_bf16.shape[1]
    assert m % tm == 0 and n % tn == 0
    return pl.pallas_call(
        _norm_mm_kernel,
        grid=(m // tm, n // tn),
        in_specs=[
            pl.BlockSpec((tm, k), lambda i, j: (i, 0)),
            pl.BlockSpec((1, k), lambda i, j: (0, 0)),
            pl.BlockSpec((k, tn), lambda i, j: (0, j)),
        ],
        out_specs=pl.BlockSpec((tm, tn), lambda i, j: (i, j)),
        out_shape=jax.ShapeDtypeStruct((m, n), F32),
        scratch_shapes=[pltpu.VMEM((tm, k), BF16)],
        compiler_params=_cp(("parallel", "arbitrary")),
        name="norm_matmul",
    )(x, ln_w.reshape(1, k), w_bf16)


def _mm_res_kernel(*refs, n_lhs):
    a_refs = refs[:n_lhs]
    w_refs = refs[n_lhs:2 * n_lhs]
    res_ref = refs[2 * n_lhs]
    o_ref = refs[2 * n_lhs + 1]
    acc = res_ref[...]
    for a_ref, w_ref in zip(a_refs, w_refs):
        acc = acc + _dot(a_ref[...].astype(BF16), w_ref[...])
    o_ref[...] = acc


def matmul_residual(lhs_list, w_list, res, *, tm, tn):
    m, n = res.shape
    n_lhs = len(lhs_list)
    assert m % tm == 0 and n % tn == 0
    in_specs = [pl.BlockSpec((tm, a.shape[1]), lambda i, j: (i, 0)) for a in lhs_list]
    in_specs += [pl.BlockSpec((w.shape[0], tn), lambda i, j: (0, j)) for w in w_list]
    in_specs += [pl.BlockSpec((tm, tn), lambda i, j: (i, j))]
    return pl.pallas_call(
        functools.partial(_mm_res_kernel, n_lhs=n_lhs),
        grid=(m // tm, n // tn),
        in_specs=in_specs,
        out_specs=pl.BlockSpec((tm, tn), lambda i, j: (i, j)),
        out_shape=jax.ShapeDtypeStruct((m, n), F32),
        compiler_params=_cp(("parallel", "arbitrary")),
        name="matmul_residual",
    )(*lhs_list, *w_list, res)


def rope_tables(pos):
    half = ROT_DIM // 2
    inv = ROPE_THETA ** (-jnp.arange(half, dtype=F32) * 2.0 / ROT_DIM)
    ang = pos.astype(F32)[:, None] * inv[None, :]
    cos, sin = jnp.cos(ang), jnp.sin(ang)
    t = pos.shape[0]
    ones = jnp.ones((t, HEAD_DIM - ROT_DIM), F32)
    zeros = jnp.zeros((t, HEAD_DIM - ROT_DIM), F32)
    z8 = jnp.zeros((t, half), F32)
    cos_t = jnp.concatenate([cos, cos, ones], axis=1)
    sin_a = jnp.concatenate([z8, sin, zeros], axis=1)
    sin_b = jnp.concatenate([-sin, z8, zeros], axis=1)
    return tuple(jnp.concatenate([a, a], axis=1) for a in (cos_t, sin_a, sin_b))


def _norm_rope_chunk(x, w, cos_t, sin_a, sin_b, gmat):
    ms = _group_sum(x * x, gmat) * (1.0 / HEAD_DIM)
    xn = x * lax.rsqrt(ms + RMS_EPS) * w
    half = ROT_DIM // 2
    return (xn * cos_t + pltpu.roll(xn, half, axis=1) * sin_a
            + pltpu.roll(xn, LANES - half, axis=1) * sin_b)


def _norm_rope(x, w, tabs, gmat):
    chunks = [
        _norm_rope_chunk(x[:, c * LANES:(c + 1) * LANES], w, *tabs, gmat)
        for c in range(x.shape[1] // LANES)
    ]
    return chunks[0] if len(chunks) == 1 else jnp.concatenate(chunks, axis=1)


def _sink_softmax(s, sink):
    m = jnp.maximum(jnp.max(s, axis=-1, keepdims=True), sink)
    e = jnp.exp(s - m)
    return e / (jnp.sum(e, axis=-1, keepdims=True) + jnp.exp(sink - m))


def _swa_prompt_kernel(q_ref, kc_ref, vc_ref, kp_ref, vp_ref, cc_ref, sac_ref, sbc_ref,
                       cp_ref, sap_ref, sbp_ref, qw_ref, kw_ref, sink_ref, o_ref, kn_ref):
    n = pl.program_id(1)
    blk = q_ref.shape[0]
    gmat = _head_indicator()
    tabs_c = (cc_ref[...], sac_ref[...], sbc_ref[...])
    tabs_p = (cp_ref[...], sap_ref[...], sbp_ref[...])
    q = _norm_rope(q_ref[...], qw_ref[...], tabs_c, gmat)
    k_cur = _norm_rope(kc_ref[...], kw_ref[...], tabs_c, gmat)
    k_prev = _norm_rope(kp_ref[...], kw_ref[...], tabs_p, gmat)
    kn_ref[...] = k_cur
    k_all = jnp.concatenate([k_prev, k_cur], axis=0).astype(BF16)
    v_all = jnp.concatenate([vp_ref[...], vc_ref[...]], axis=0).astype(BF16)

    qi = lax.broadcasted_iota(jnp.int32, (blk, 2 * blk), 0) + blk
    si = lax.broadcasted_iota(jnp.int32, (blk, 2 * blk), 1)
    rel = qi - si
    valid = (rel >= 0) & (rel <= WINDOW) & ((n > 0) | (si >= blk))

    groups = range(ATT_KV_HEADS)
    lanes = [slice(kv * HEAD_DIM, (kv + 1) * HEAD_DIM) for kv in groups]
    heads = [[kv * ATT_GROUP + g for g in range(ATT_GROUP)] for kv in groups]
    q_g = [jnp.concatenate([q[:, h * HEAD_DIM:(h + 1) * HEAD_DIM] for h in heads[kv]],
                           axis=0).astype(BF16) for kv in groups]
    s = [_dot_nt(q_g[kv], k_all[:, lanes[kv]]) * ATT_SCALE for kv in groups]
    p = [jnp.concatenate(
        [_sink_softmax(jnp.where(valid, s[kv][g * blk:(g + 1) * blk], -jnp.inf), sink_ref[h])
         for g, h in enumerate(heads[kv])], axis=0).astype(BF16) for kv in groups]
    o = [_dot(p[kv], v_all[:, lanes[kv]]) for kv in groups]
    for kv in groups:
        for g, h in enumerate(heads[kv]):
            o_ref[:, h * HEAD_DIM:(h + 1) * HEAD_DIM] = o[kv][g * blk:(g + 1) * blk]


def swa_prompt(pa, tabs, q_norm_w, k_norm_w, sinks):
    b, t, _ = pa.shape
    blk = WINDOW
    nb = t // blk
    qb, kb, vb = 0, ATT_WIDTH // KV_WIDTH, ATT_WIDTH // KV_WIDTH + 1
    cur = lambda i, n, *_: (i, n, 0)
    tab_cur = pl.BlockSpec((blk, LANES), lambda i, n: (n, 0))
    tab_prev = pl.BlockSpec((blk, LANES), lambda i, n: (jnp.maximum(n - 1, 0), 0))
    qw = jnp.tile(q_norm_w.reshape(1, HEAD_DIM), (1, 2))
    kw = jnp.tile(k_norm_w.reshape(1, HEAD_DIM), (1, 2))
    return pl.pallas_call(
        _swa_prompt_kernel,
        grid=(b, nb),
        in_specs=[
            pl.BlockSpec((None, blk, ATT_WIDTH), lambda i, n: (i, n, qb)),
            pl.BlockSpec((None, blk, KV_WIDTH), lambda i, n: (i, n, kb)),
            pl.BlockSpec((None, blk, KV_WIDTH), lambda i, n: (i, n, vb)),
            pl.BlockSpec((None, blk, KV_WIDTH), lambda i, n: (i, jnp.maximum(n - 1, 0), kb)),
            pl.BlockSpec((None, blk, KV_WIDTH), lambda i, n: (i, jnp.maximum(n - 1, 0), vb)),
            tab_cur, tab_cur, tab_cur, tab_prev, tab_prev, tab_prev,
            pl.BlockSpec((1, LANES), lambda i, n: (0, 0)),
            pl.BlockSpec((1, LANES), lambda i, n: (0, 0)),
            pl.BlockSpec(memory_space=pltpu.SMEM),
        ],
        out_specs=[
            pl.BlockSpec((None, blk, ATT_WIDTH), cur),
            pl.BlockSpec((None, blk, KV_WIDTH), cur),
        ],
        out_shape=[
            jax.ShapeDtypeStruct((b, t, ATT_WIDTH), F32),
            jax.ShapeDtypeStruct((b, t, KV_WIDTH), F32),
        ],
        compiler_params=_cp(("parallel", "arbitrary")),
        name="swa_prompt",
    )(pa, pa, pa, pa, pa, *tabs, *tabs, qw, kw, sinks)


def _qk_norm_rope_kernel(x_ref, w_ref, c_ref, sa_ref, sb_ref, o_ref):
    gmat = _head_indicator()
    tabs = (c_ref[...], sa_ref[...], sb_ref[...])
    for c in range(x_ref.shape[1] // LANES):
        sl = slice(c * LANES, (c + 1) * LANES)
        o_ref[:, sl] = _norm_rope_chunk(x_ref[:, sl], w_ref[:, sl], *tabs, gmat)


def qk_norm_rope(x, w_row, tabs):
    m, w = x.shape
    full = lambda *shape: pl.BlockSpec(shape, lambda: (0,) * len(shape))
    return pl.pallas_call(
        _qk_norm_rope_kernel,
        in_specs=[full(m, w), full(1, w), full(1, LANES), full(1, LANES), full(1, LANES)],
        out_specs=full(m, w),
        out_shape=jax.ShapeDtypeStruct((m, w), F32),
        name="qk_norm_rope",
    )(x, w_row, *tabs)


def _swa_decode_kernel(q_ref, kn_ref, vn_ref, knt_ref, vnt_ref, ck_ref, cv_ref, sink_ref, o_ref,
                       kw_ref, vw_ref):
    bb = q_ref.shape[0]
    nbuf = ck_ref.shape[2]
    row_kv = lax.broadcasted_iota(jnp.int32, (ATT_HEADS, KV_WIDTH), 0) // ATT_GROUP
    lane_kv = lax.broadcasted_iota(jnp.int32, (ATT_HEADS, KV_WIDTH), 1) // HEAD_DIM
    own = row_kv == lane_kv
    sink = sink_ref[...]
    seq = lax.broadcasted_iota(jnp.int32, knt_ref.shape, 1)
    pos = lax.broadcasted_iota(jnp.int32, (KV_WIDTH, nbuf), 1)
    b0 = pl.program_id(0) * bb
    for b in range(bb):
        q2 = q_ref[b]
        q_exp = jnp.where(own, jnp.concatenate([q2] * ATT_KV_HEADS, axis=1), 0.0)
        k_new, v_new = kn_ref[b], vn_ref[b]
        k_buf, v_buf = ck_ref[b], cv_ref[b]
        s_buf = _dot(q_exp.astype(BF16), k_buf.astype(BF16)) * ATT_SCALE
        s_new = jnp.sum(q_exp * k_new, axis=-1, keepdims=True) * ATT_SCALE
        m = jnp.maximum(jnp.maximum(jnp.max(s_buf, axis=-1, keepdims=True), s_new), sink)
        e_buf = jnp.exp(s_buf - m)
        e_new = jnp.exp(s_new - m)
        inv = 1.0 / (jnp.sum(e_buf, axis=-1, keepdims=True) + e_new + jnp.exp(sink - m))
        o = _dot_nt((e_buf * inv).astype(BF16), v_buf.astype(BF16)) + (e_new * inv) * v_new
        o = jnp.where(own, o, 0.0)
        o_ref[b] = (o[:, 0:HEAD_DIM] + o[:, HEAD_DIM:2 * HEAD_DIM]
                    + o[:, 2 * HEAD_DIM:3 * HEAD_DIM] + o[:, 3 * HEAD_DIM:4 * HEAD_DIM])
        k_col = jnp.sum(jnp.where(seq == b0 + b, knt_ref[...], 0.0), axis=1, keepdims=True)
        v_col = jnp.sum(jnp.where(seq == b0 + b, vnt_ref[...], 0.0), axis=1, keepdims=True)
        kw_ref[b] = jnp.where(pos == nbuf - 1, k_col, pltpu.roll(k_buf, nbuf - 1, axis=1))
        vw_ref[b] = jnp.where(pos == nbuf - 1, v_col, pltpu.roll(v_buf, nbuf - 1, axis=1))


def swa_decode(q, k_new, v_new, cache_kt, cache_vt, sinks, *, bb=8):
    b, _, nbuf = cache_kt.shape
    blk3 = lambda s1, s2: pl.BlockSpec((bb, s1, s2), lambda i: (i, 0, 0))
    whole = pl.BlockSpec((KV_WIDTH, b), lambda i: (0, 0))
    return pl.pallas_call(
        _swa_decode_kernel,
        grid=(b // bb,),
        in_specs=[
            blk3(ATT_HEADS, HEAD_DIM), blk3(1, KV_WIDTH), blk3(1, KV_WIDTH), whole, whole,
            blk3(KV_WIDTH, nbuf), blk3(KV_WIDTH, nbuf),
            pl.BlockSpec((ATT_HEADS, 1), lambda i: (0, 0)),
        ],
        out_specs=[blk3(ATT_HEADS, HEAD_DIM), blk3(KV_WIDTH, nbuf), blk3(KV_WIDTH, nbuf)],
        out_shape=[
            jax.ShapeDtypeStruct((b, ATT_HEADS, HEAD_DIM), F32),
            jax.ShapeDtypeStruct((b, KV_WIDTH, nbuf), F32),
            jax.ShapeDtypeStruct((b, KV_WIDTH, nbuf), F32),
        ],
        compiler_params=_cp(("parallel",)),
        name="swa_decode",
    )(q, k_new, v_new, k_new.reshape(b, KV_WIDTH).T, v_new.reshape(b, KV_WIDTH).T,
      cache_kt, cache_vt, sinks.reshape(ATT_HEADS, 1))


def _head_rms_kernel(x_ref, w_ref, o_ref):
    for h in range(x_ref.shape[1] // XATT_HEAD_DIM):
        sl = slice(h * XATT_HEAD_DIM, (h + 1) * XATT_HEAD_DIM)
        o_ref[:, sl] = _rms_rows(x_ref[:, sl], w_ref[...])


def head_rms(x, w):
    m, wd = x.shape
    return pl.pallas_call(
        _head_rms_kernel,
        in_specs=[pl.BlockSpec((m, wd), lambda: (0, 0)),
                  pl.BlockSpec((1, XATT_HEAD_DIM), lambda: (0, 0))],
        out_specs=pl.BlockSpec((m, wd), lambda: (0, 0)),
        out_shape=jax.ShapeDtypeStruct((m, wd), F32),
        name="head_rms",
    )(x, w.reshape(1, XATT_HEAD_DIM))


def _xattn_prompt_kernel(q_ref, k_ref, v_ref, w_ref, o_ref):
    scale = 1.0 / math.sqrt(XATT_HEAD_DIM)
    for h in range(XATT_HEADS):
        sl = slice(h * XATT_HEAD_DIM, (h + 1) * XATT_HEAD_DIM)
        qn = _rms_rows(q_ref[:, sl], w_ref[...]).astype(BF16)
        s = _dot_nt(qn, k_ref[:, sl].astype(BF16)) * scale
        e = jnp.exp(s - jnp.max(s, axis=-1, keepdims=True))
        p = e / jnp.sum(e, axis=-1, keepdims=True)
        o_ref[:, sl] = _dot(p.astype(BF16), v_ref[:, sl].astype(BF16))


def xattn_prompt(q, mem_k, mem_v, xq_norm_w, *, tq=512):
    b, t, w = q.shape
    n_mem = mem_k.shape[1]
    return pl.pallas_call(
        _xattn_prompt_kernel,
        grid=(b, t // tq),
        in_specs=[
            pl.BlockSpec((None, tq, w), lambda i, j: (i, j, 0)),
            pl.BlockSpec((None, n_mem, w), lambda i, j: (i, 0, 0)),
            pl.BlockSpec((None, n_mem, w), lambda i, j: (i, 0, 0)),
            pl.BlockSpec((1, XATT_HEAD_DIM), lambda i, j: (0, 0)),
        ],
        out_specs=pl.BlockSpec((None, tq, w), lambda i, j: (i, j, 0)),
        out_shape=jax.ShapeDtypeStruct((b, t, w), F32),
        compiler_params=_cp(("parallel", "arbitrary")),
        name="xattn_prompt",
    )(q, mem_k, mem_v, xq_norm_w.reshape(1, XATT_HEAD_DIM))


def _xattn_decode_kernel(q_ref, k_ref, v_ref, w_ref, o_ref):
    bb, rows, _ = q_ref.shape
    n_keys = k_ref.shape[1]
    scale = 1.0 / math.sqrt(XATT_HEAD_DIM)
    own = (lax.broadcasted_iota(jnp.int32, (rows, n_keys), 1) % XATT_HEADS
           == lax.broadcasted_iota(jnp.int32, (rows, n_keys), 0) % XATT_HEADS)
    seqs = range(bb)
    qn = [_rms_rows(q_ref[b], w_ref[...]).astype(BF16) for b in seqs]
    s = [jnp.where(own, _dot_nt(qn[b], k_ref[b].astype(BF16)) * scale, -jnp.inf) for b in seqs]
    e = [jnp.exp(s[b] - jnp.max(s[b], axis=-1, keepdims=True)) for b in seqs]
    p = [(e[b] / jnp.sum(e[b], axis=-1, keepdims=True)).astype(BF16) for b in seqs]
    for b in seqs:
        o_ref[b] = _dot(p[b], v_ref[b].astype(BF16))


def xattn_decode(q_pad, mem_k, mem_v, xq_norm_w, *, bb=8):
    b, rows, _ = q_pad.shape
    n_keys = mem_k.shape[1]
    kv = pl.BlockSpec((bb, n_keys, XATT_HEAD_DIM), lambda i: (i, 0, 0))
    return pl.pallas_call(
        _xattn_decode_kernel,
        grid=(b // bb,),
        in_specs=[pl.BlockSpec((bb, rows, XATT_HEAD_DIM), lambda i: (i, 0, 0)), kv, kv,
                  pl.BlockSpec((1, XATT_HEAD_DIM), lambda i: (0, 0))],
        out_specs=pl.BlockSpec((bb, rows, XATT_HEAD_DIM), lambda i: (i, 0, 0)),
        out_shape=jax.ShapeDtypeStruct((b, rows, XATT_HEAD_DIM), F32),
        compiler_params=_cp(("parallel",)),
        name="xattn_decode",
    )(q_pad, mem_k, mem_v, xq_norm_w.reshape(1, XATT_HEAD_DIM))


LORA_OFF = 3 * RWKV_WIDTH
GATE_OFF = LORA_OFF + DECAY_LORA + AAA_LORA
GATE_PAD = RWKV_PROJ_PAD - GATE_OFF


def _sigmoid(x):
    return 1.0 / (1.0 + jnp.exp(-x))


def _per_chunk(fn, *arrays):
    w = arrays[0].shape[1]
    outs = [fn(*(a[:, c * LANES:(c + 1) * LANES] for a in arrays)) for c in range(w // LANES)]
    return jnp.concatenate(outs, axis=1)


def _rwkv_prep_core(pr, prev, mu, w0, a0, kk_w, ka_w, rk_w, w_lora, w_gate):
    c = RWKV_WIDTH
    gmat = _head_indicator()
    xm = pr + (prev - pr) * mu
    r, k, v = xm[:, 0:c], xm[:, c:2 * c], xm[:, 2 * c:3 * c]
    lora = xm[:, LORA_OFF:LORA_OFF + LANES]
    lane = lax.broadcasted_iota(jnp.int32, lora.shape, 1)
    lora_in = jnp.where(lane < DECAY_LORA, jnp.tanh(lora), lora)
    wa = _dot(lora_in.astype(BF16), w_lora)
    z = -(w0 + wa[:, 0:c])
    softplus = jnp.maximum(z, 0.0) + jnp.log(1.0 + jnp.exp(-jnp.abs(z)))
    log_decay = -jnp.exp(-softplus - 0.5)
    a = _sigmoid(a0 + wa[:, c:2 * c])
    g = _dot(_sigmoid(xm[:, GATE_OFF:GATE_OFF + GATE_PAD]).astype(BF16), w_gate)
    kk = k * kk_w
    norm = jnp.sqrt(_per_chunk(lambda t: _group_sum(t * t, gmat), kk))
    kk = kk / jnp.maximum(norm, 1e-12)
    kp = k * (1.0 + (a - 1.0) * ka_w)
    bonus = _per_chunk(lambda t: _group_sum(t, gmat), r * kp * rk_w) * v
    return r, log_decay, kp, v, kk, kk * a, bonus, g


def _rwkv_prep_seq_kernel(pr_ref, prev0_ref, mu_ref, w0_ref, a0_ref, kkw_ref, kaw_ref, rkw_ref,
                          wl_ref, wg_ref, *refs):
    out_refs, last_ref = refs[:-1], refs[-1]

    @pl.when(pl.program_id(1) == 0)
    def _():
        last_ref[...] = prev0_ref[...]

    pr = pr_ref[...]
    rows = pr.shape[0]
    row = lax.broadcasted_iota(jnp.int32, (rows, 1), 0)
    prev = jnp.where(row == 0, last_ref[...], pltpu.roll(pr, 1, axis=0))
    last_ref[...] = pr[rows - 1:rows, :]
    outs = _rwkv_prep_core(pr, prev, mu_ref[...], w0_ref[...], a0_ref[...], kkw_ref[...],
                           kaw_ref[...], rkw_ref[...], wl_ref[...], wg_ref[...])
    for o_ref, o in zip(out_refs, outs):
        o_ref[...] = o


def _rwkv_prep_tok_kernel(pr_ref, prev_ref, mu_ref, w0_ref, a0_ref, kkw_ref, kaw_ref, rkw_ref,
                          wl_ref, wg_ref, *out_refs):
    outs = _rwkv_prep_core(pr_ref[...], prev_ref[...], mu_ref[...], w0_ref[...], a0_ref[...],
                           kkw_ref[...], kaw_ref[...], rkw_ref[...], wl_ref[...], wg_ref[...])
    for k, (o_ref, o) in enumerate(zip(out_refs, outs)):
        o_ref[...] = o.T if k < N_STEP_VECS else o


def _rwkv_param_specs(index_map):
    c = RWKV_WIDTH
    shapes = [(1, RWKV_PROJ_PAD)] + [(1, c)] * 5 + [(LANES, 2 * c), (GATE_PAD, c)]
    return [pl.BlockSpec(s, index_map) for s in shapes]


def rwkv_prep_seq(pr, prev0, params, *, tm=256):
    b, t, wd = pr.shape
    c = RWKV_WIDTH
    out = jax.ShapeDtypeStruct((b, t, c), F32)
    return pl.pallas_call(
        _rwkv_prep_seq_kernel,
        grid=(b, t // tm),
        in_specs=[pl.BlockSpec((None, tm, wd), lambda i, j: (i, j, 0)),
                  pl.BlockSpec((None, 1, wd), lambda i, j: (i, 0, 0))]
        + _rwkv_param_specs(lambda i, j: (0, 0)),
        out_specs=[pl.BlockSpec((None, tm, c), lambda i, j: (i, j, 0))] * 8,
        out_shape=[out] * 8,
        scratch_shapes=[pltpu.VMEM((1, wd), F32)],
        compiler_params=_cp(("parallel", "arbitrary")),
        name="rwkv_prep_seq",
    )(pr, prev0, *params)


N_STEP_VECS = 6


def rwkv_prep_tok(pr, prev, params):
    m, wd = pr.shape
    c = RWKV_WIDTH
    shapes = [(c, m)] * N_STEP_VECS + [(m, c)] * 2
    return pl.pallas_call(
        _rwkv_prep_tok_kernel,
        grid=(1,),
        in_specs=[pl.BlockSpec((m, wd), lambda i: (0, 0))] * 2
        + _rwkv_param_specs(lambda i: (0, 0)),
        out_specs=[pl.BlockSpec(s, lambda i: (0, 0)) for s in shapes],
        out_shape=[jax.ShapeDtypeStruct(s, F32) for s in shapes],
        compiler_params=_cp(("arbitrary",)),
        name="rwkv_prep_tok",
    )(pr, prev, *params)


def _dot_tn(a, b):
    return lax.dot_general(a, b, (((0,), (0,)), ((), ())), preferred_element_type=F32)


def _rwkv_scan_kernel(r_ref, ld_ref, kp_ref, v_ref, kk_ref, b_ref, y_ref, s_out_ref, s_ref):
    @pl.when(pl.program_id(1) == 0)
    def _():
        s_ref[...] = jnp.zeros_like(s_ref)

    n = CHUNK
    nc = r_ref.shape[0] // n
    ti = lax.broadcasted_iota(jnp.int32, (n, n), 0)
    si = lax.broadcasted_iota(jnp.int32, (n, n), 1)
    tri = jnp.where(si <= ti, 1.0, 0.0).astype(BF16)
    t2 = lax.broadcasted_iota(jnp.int32, (n, LANES), 0)
    lane2 = lax.broadcasted_iota(jnp.int32, (n, LANES), 1)
    s2 = lane2 % n
    low = lane2 < n
    strict2, incl2, eye2 = s2 < t2, s2 <= t2, s2 == t2
    low4 = lax.broadcasted_iota(jnp.int32, (2 * n, LANES), 1) < n
    top4 = lax.broadcasted_iota(jnp.int32, (2 * n, LANES), 0) < n
    diag_blk = top4 == low4
    own4 = {0: low4, 1: ~low4}
    pairs = range(RWKV_HEADS // 2)
    cs = [slice(c * LANES, (c + 1) * LANES) for c in pairs]

    pre = []
    for j in range(nc):
        rows = slice(j * n, (j + 1) * n)
        ld = ld_ref[rows, :]
        l1, l2, l3 = _split3(ld)
        lc = _dot(tri, l1) + _dot(tri, l2) + _dot(tri, l3)
        lc_end = lc[n - 1:n, :]
        e_neg = jnp.exp(-lc)
        kk, b, kp = kk_ref[rows, :], b_ref[rows, :], kp_ref[rows, :]
        to_end = jnp.exp(lc_end - lc)
        pre.append(dict(
            a_t=(-kk * jnp.exp(lc - ld)).astype(BF16), b_t=(b * e_neg).astype(BF16),
            k_t=(kp * e_neg).astype(BF16), r_t=(r_ref[rows, :] * jnp.exp(lc)).astype(BF16),
            b_e=(b * to_end).astype(BF16), k_e=(kp * to_end).astype(BF16),
            v_b=v_ref[rows, :].astype(BF16), g_end=jnp.exp(lc_end)))
    items = [(j, c, par) for j in range(nc) for c in pairs for par in (0, 1)]
    at = {it: i for i, it in enumerate(items)}
    ar = {(j, c): jnp.concatenate([pre[j]['a_t'][:, cs[c]], pre[j]['r_t'][:, cs[c]]], axis=0)
          for j in range(nc) for c in pairs}
    kb = {(j, c): jnp.concatenate([pre[j]['k_t'][:, cs[c]], pre[j]['b_t'][:, cs[c]]], axis=0)
          for j in range(nc) for c in pairs}
    gm = [_dot_nt(jnp.where(own4[par], ar[j, c], 0).astype(BF16), kb[j, c]) for j, c, par in items]
    top = [g[0:n] for g in gm]
    pm = [jnp.where(incl2, g[n:2 * n], 0.0).astype(BF16) for g in gm]
    lak = [jnp.where(strict2 & low, t, 0.0).astype(BF16) for t in top]
    lv = [_dot(lak[i][:, 0:n], pre[j]['v_b'][:, cs[c]]) for i, (j, c, par) in enumerate(items)]
    z = [jnp.where(low, jnp.where(eye2, 1.0, 0.0), jnp.where(strict2, t, 0.0)) for t in top]
    for _ in range(6):
        zb = [zz.astype(BF16) for zz in z]
        res = [_dot(jnp.where(low, 0, zb[i]).astype(BF16), jnp.concatenate([zb[i], zb[i]], axis=0))
               for i in range(len(items))]
        z = [res[i] + jnp.where(low, z[i], 0.0) for i in range(len(items))]
    tmat = [zz[:, 0:n].astype(BF16) for zz in z]

    s_cur = [s_ref[c] for c in pairs]
    for j in range(nc):
        p = pre[j]
        sw = [_dot_nt(ar[j, c], s_cur[c].astype(BF16)) for c in pairs]
        w0 = [(sw[c][0:n] + jnp.where(low, lv[at[j, c, 0]], lv[at[j, c, 1]])).astype(BF16)
              for c in pairs]
        u = [jnp.where(low, _dot(tmat[at[j, c, 0]], w0[c]),
                       _dot(tmat[at[j, c, 1]], w0[c])).astype(BF16) for c in pairs]
        vu = [jnp.concatenate([p['v_b'][:, cs[c]], u[c]], axis=0) for c in pairs]
        yb = [jnp.where(low, _dot(pm[at[j, c, 0]], vu[c]), _dot(pm[at[j, c, 1]], vu[c]))
              for c in pairs]
        for c in pairs:
            y_ref[j * n:(j + 1) * n, cs[c]] = sw[c][n:2 * n] + yb[c]
        upd = [_dot_tn(jnp.concatenate([u[c], p['v_b'][:, cs[c]]], axis=0),
                       jnp.concatenate([p['b_e'][:, cs[c]], p['k_e'][:, cs[c]]], axis=0))
               for c in pairs]
        s_cur = [s_cur[c] * p['g_end'][:, cs[c]] + jnp.where(diag_blk, upd[c], 0.0) for c in pairs]
    for c in pairs:
        s_ref[c] = s_cur[c]
        s_out_ref[2 * c] = s_cur[c][0:n, 0:n]
        s_out_ref[2 * c + 1] = s_cur[c][n:2 * n, n:2 * n]


SCAN_CHUNKS = 4


def rwkv_scan(r, ld, kp, v, kk, b):
    bsz, t, c = r.shape
    rows = SCAN_CHUNKS * CHUNK
    assert t % rows == 0
    blk = pl.BlockSpec((None, rows, c), lambda i, j: (i, j, 0))
    st = pl.BlockSpec((None, RWKV_HEADS, RWKV_HEAD_DIM, RWKV_HEAD_DIM), lambda i, j: (i, 0, 0, 0))
    return pl.pallas_call(
        _rwkv_scan_kernel,
        grid=(bsz, t // rows),
        in_specs=[blk] * 6,
        out_specs=[blk, st],
        out_shape=[jax.ShapeDtypeStruct((bsz, t, c), F32),
                   jax.ShapeDtypeStruct((bsz, RWKV_HEADS, RWKV_HEAD_DIM, RWKV_HEAD_DIM), F32)],
        scratch_shapes=[pltpu.VMEM((RWKV_HEADS // 2, LANES, LANES), F32)],
        compiler_params=_cp(("parallel", "arbitrary")),
        name="rwkv_scan",
    )(r, ld, kp, v, kk, b)


STEP_UNROLL = 8


def _rwkv_step_kernel(r_ref, ld_ref, kp_ref, v_ref, kk_ref, b_ref, s_ref, y_ref, s_out_ref):
    n = RWKV_HEAD_DIM
    neg_kk, decay = -kk_ref[...], jnp.exp(ld_ref[...])
    b_mat, kp_mat, r_mat = b_ref[...], kp_ref[...], r_ref[...]

    def body(i, carry):
        v0 = pl.multiple_of(i * STEP_UNROLL, STEP_UNROLL)
        v_rows = v_ref[pl.ds(v0, STEP_UNROLL), :]
        rows = range(STEP_UNROLL)
        s = [s_ref[v0 + j] for j in rows]
        sa = [jnp.sum(s[j] * neg_kk, axis=0, keepdims=True) for j in rows]
        s_new = [s[j] * decay + sa[j] * b_mat + v_rows[j:j + 1, :] * kp_mat for j in rows]
        y = [jnp.sum(s_new[j] * r_mat, axis=0, keepdims=True) for j in rows]
        for j in rows:
            s_out_ref[v0 + j] = s_new[j]
        y_ref[pl.ds(v0, STEP_UNROLL), :] = jnp.concatenate(y, axis=0)
        return carry

    lax.fori_loop(0, n // STEP_UNROLL, body, 0)


def rwkv_step(r, ld, kp, v, kk, b, state_t):
    _, nh, n, _, bsz = state_t.shape
    vec = pl.BlockSpec((n, bsz), lambda h: (h, 0))
    st = pl.BlockSpec((None, None, n, n, bsz), lambda h: (0, h, 0, 0, 0))
    return pl.pallas_call(
        _rwkv_step_kernel,
        grid=(nh,),
        in_specs=[vec] * 6 + [st],
        out_specs=[vec, st],
        out_shape=[jax.ShapeDtypeStruct((nh * n, bsz), F32),
                   jax.ShapeDtypeStruct(state_t.shape, F32)],
        compiler_params=_cp(("parallel",)),
        name="rwkv_step",
    )(r, ld, kp, v, kk, b, state_t)


def _rwkv_post_kernel(y_ref, bonus_ref, g_ref, lnw_ref, lnb_ref, o_ref, *, y_channel_major):
    gmat = _head_indicator()
    inv = 1.0 / RWKV_HEAD_DIM
    for c in range(o_ref.shape[1] // LANES):
        sl = slice(c * LANES, (c + 1) * LANES)
        y = y_ref[sl, :].T if y_channel_major else y_ref[:, sl]
        d = y - _group_sum(y, gmat) * inv
        var = _group_sum(d * d, gmat) * inv
        yn = d * lax.rsqrt(var + GN_EPS) * lnw_ref[:, sl] + lnb_ref[:, sl]
        o_ref[:, sl] = (yn + bonus_ref[:, sl]) * g_ref[:, sl]


def rwkv_post(y, bonus, g, ln_w, ln_b, *, tm, y_channel_major=False):
    m, c = bonus.shape
    blk = pl.BlockSpec((tm, c), lambda i: (i, 0))
    y_blk = pl.BlockSpec((c, tm), lambda i: (0, i)) if y_channel_major else blk
    vec = pl.BlockSpec((1, c), lambda i: (0, 0))
    return pl.pallas_call(
        functools.partial(_rwkv_post_kernel, y_channel_major=y_channel_major),
        grid=(m // tm,),
        in_specs=[y_blk, blk, blk, vec, vec],
        out_specs=blk,
        out_shape=jax.ShapeDtypeStruct((m, c), F32),
        compiler_params=_cp(("parallel",)),
        name="rwkv_post",
    )(y, bonus, g, ln_w.reshape(1, c), ln_b.reshape(1, c))


ROUTER_LANES = LANES
ROW_TILES = D_MODEL // LANES


def _rows_to_tiles(ref, x):
    rows = x.shape[0]
    for j in range(ROW_TILES):
        ref[pl.ds(j, rows, stride=ROW_TILES), :] = x[:, j * LANES:(j + 1) * LANES]


def _tiles_to_rows(ref, rows):
    return jnp.concatenate(
        [ref[pl.ds(j, rows, stride=ROW_TILES), :] for j in range(ROW_TILES)], axis=1)


def _router_kernel(ha_ref, hb_ref, lnw_ref, whi_ref, wlo_ref, bias_ref, u_ref, idx_ref, gate_ref,
                   *, steps_a):
    use_a = pl.program_id(0) < steps_a
    h = jnp.where(use_a, ha_ref[...], hb_ref[...])
    u = _rms_rows(h, lnw_ref[...])
    _rows_to_tiles(u_ref, u)
    u_hi, u_lo = _split2(u)
    w_hi = whi_ref[...]
    logits = _dot(u_hi, w_hi) + _dot(u_lo, w_hi) + _dot(u_hi, wlo_ref[...]) + bias_ref[...]
    lane = lax.broadcasted_iota(jnp.int32, logits.shape, 1)
    neg = -jnp.inf

    def first_max(x):
        m = jnp.max(x, axis=1, keepdims=True)
        return m, jnp.min(jnp.where(x == m, lane, ROUTER_LANES), axis=1, keepdims=True)

    gl = jnp.where(lane < N_EXPERT_GROUPS, logits, neg)
    g_max, g_idx = first_max(gl)
    g_gate = 1.0 / jnp.sum(jnp.exp(gl - g_max), axis=1, keepdims=True)
    lo = N_EXPERT_GROUPS + g_idx * EXPERTS_PER_GROUP
    el = jnp.where((lane >= lo) & (lane < lo + EXPERTS_PER_GROUP), logits, neg)
    v1, i1 = first_max(el)
    v2, i2 = first_max(jnp.where(lane == i1, neg, el))
    e2 = jnp.exp(v2 - v1)
    w1 = g_gate / (1.0 + e2)
    w2 = g_gate * e2 / (1.0 + e2)
    idx_ref[...] = jnp.where(lane == 0, i1 - N_EXPERT_GROUPS,
                             jnp.where(lane == 1, i2 - N_EXPERT_GROUPS, 0))
    gate_ref[...] = jnp.where(lane == 0, w1, jnp.where(lane == 1, w2, 0.0))


def moe_router(h_a, h_b, ln_w, w_hi, w_lo, bias, *, tm):
    (ma, d), mb = h_a.shape, h_b.shape[0]
    assert ma % tm == 0 and mb % tm == 0
    steps_a, steps_b = ma // tm, mb // tm
    m = ma + mb
    const = lambda r, w: pl.BlockSpec((r, w), lambda i: (0, 0))
    row = lambda w: pl.BlockSpec((tm, w), lambda i: (i, 0))
    return pl.pallas_call(
        functools.partial(_router_kernel, steps_a=steps_a),
        grid=(steps_a + steps_b,),
        in_specs=[pl.BlockSpec((tm, d), lambda i: (jnp.minimum(i, steps_a - 1), 0)),
                  pl.BlockSpec((tm, d), lambda i: (jnp.maximum(i - steps_a, 0), 0)),
                  const(1, d), const(d, ROUTER_LANES), const(d, ROUTER_LANES),
                  const(1, ROUTER_LANES)],
        out_specs=[pl.BlockSpec((tm * (d // LANES), LANES), lambda i: (i, 0)),
                   row(ROUTER_LANES), row(ROUTER_LANES)],
        out_shape=[jax.ShapeDtypeStruct((m * (d // LANES), LANES), F32),
                   jax.ShapeDtypeStruct((m, ROUTER_LANES), jnp.int32),
                   jax.ShapeDtypeStruct((m, ROUTER_LANES), F32)],
        compiler_params=_cp(("arbitrary",)),
        name="moe_router",
    )(h_a, h_b, ln_w.reshape(1, d), w_hi, w_lo, bias)


X_SLOTS = 3
Y_SLOTS = 2


W_SLOTS = 3
DMA_QUEUES = 2


def _moe_expert_kernel(run_ref, rexp_ref, nused_ref, tok0_ref, tok1_ref, tok2_ref, dst_ref,
                       roww_ref, u_hbm, wg_hbm, wu_hbm, wd_hbm, y_hbm, xbuf, ybuf, wg_f, wu_f,
                       wd_f, wg_b, wu_b, wd_b, sem_in, sem_out, sem_w):
    i = pl.program_id(0)
    n_used, n_runs = nused_ref[0], nused_ref[1]
    tile_rows = MOE_BLOCK * ROW_TILES
    pad_base = y_hbm.shape[0] - Y_SLOTS * tile_rows
    run = run_ref[i]

    def weight_copies(k):
        e, s = rexp_ref[jnp.minimum(k, n_runs - 1)], lax.rem(k, W_SLOTS)
        copies = []
        for hbm, buf in ((wg_hbm, wg_f), (wu_hbm, wu_f), (wd_hbm, wd_f)):
            rows = buf.shape[1] // 2
            for part in range(2):
                sl = pl.ds(part * rows, rows)
                copies.append((pltpu.make_async_copy(hbm.at[e, sl], buf.at[s, sl], sem_w.at[s]),
                               part))
        return copies

    def gather_block(idx_ref, x_slot):
        for r in range(MOE_BLOCK):
            pltpu.make_async_copy(u_hbm.at[pl.ds(idx_ref[0, 0, r], ROW_TILES)],
                                  xbuf.at[x_slot, pl.ds(r * ROW_TILES, ROW_TILES)],
                                  sem_in.at[x_slot]).start(priority=r % DMA_QUEUES)

    def scatter_rows(idx_ref, y_slot, rows):
        for r in rows:
            pltpu.make_async_copy(ybuf.at[y_slot, pl.ds(r * ROW_TILES, ROW_TILES)],
                                  y_hbm.at[pl.ds(idx_ref[0, 0, r], ROW_TILES)],
                                  sem_out.at[y_slot]).start(priority=r % DMA_QUEUES)

    def gather_wait(x_slot):
        pltpu.make_async_copy(u_hbm.at[pl.ds(0, tile_rows)], xbuf.at[x_slot],
                              sem_in.at[x_slot]).wait()

    def scatter_wait(y_slot):
        pltpu.make_async_copy(ybuf.at[y_slot], y_hbm.at[pl.ds(0, tile_rows)],
                              sem_out.at[y_slot]).wait()

    @pl.when(i == 0)
    def _():
        ybuf[0] = jnp.zeros(ybuf.shape[1:], F32)
        for s in range(Y_SLOTS):
            pltpu.make_async_copy(ybuf.at[0], y_hbm.at[pl.ds(pad_base + s * tile_rows, tile_rows)],
                                  sem_out.at[s]).start()
        for k in range(W_SLOTS - 1):
            for cp, queue in weight_copies(k):
                cp.start(priority=queue)
        gather_block(tok0_ref, 0)
        gather_block(tok1_ref, 1)
        for s in range(Y_SLOTS):
            scatter_wait(s)

    @pl.when(i < n_used)
    def _():
        x_slot = lax.rem(i, X_SLOTS)
        y_slot = lax.rem(i, Y_SLOTS)

        @pl.when((i == 0) | (run != run_ref[jnp.maximum(i - 1, 0)]))
        def _():
            for cp, _ in weight_copies(run):
                cp.wait()
            w_slot = lax.rem(run, W_SLOTS)
            wg_b[...] = wg_f[w_slot].astype(BF16)
            wu_b[...] = wu_f[w_slot].astype(BF16)
            wd_b[...] = wd_f[w_slot].astype(BF16)
            for cp, queue in weight_copies(run + W_SLOTS - 1):
                cp.start(priority=queue)

        gather_wait(x_slot)

        @pl.when(i >= Y_SLOTS)
        def _():
            scatter_wait(y_slot)

        x = _tiles_to_rows(xbuf.at[x_slot], MOE_BLOCK).astype(BF16)
        hg = _dot(x, wg_b[...])
        hu = _dot(x, wu_b[...])
        act = (hg * _sigmoid(hg) * hu).astype(BF16)
        y = _dot(act, wd_b[...]) * roww_ref[...]
        _rows_to_tiles(ybuf.at[y_slot], y)
        scatter_rows(dst_ref, y_slot, range(MOE_BLOCK))
        gather_block(tok2_ref, lax.rem(i + 2, X_SLOTS))

        @pl.when(i == n_used - 1)
        def _():
            scatter_wait(y_slot)

            @pl.when(i >= 1)
            def _():
                scatter_wait(1 - y_slot)

            gather_wait(lax.rem(i + 1, X_SLOTS))
            gather_wait(lax.rem(i + 2, X_SLOTS))
            for k in range(W_SLOTS - 1):
                for cp, _ in weight_copies(n_runs + k):
                    cp.wait()


def moe_experts(u_all, row_src, row_dst, row_w, block_run, run_exp, n_used_runs, w_gate, w_up,
                w_down, n_assign):
    d, ff = w_gate.shape[1], w_gate.shape[2]
    n_blocks = row_src.shape[0]
    tile_rows = MOE_BLOCK * ROW_TILES
    smem_blk = lambda off: pl.BlockSpec(
        (1, 1, MOE_BLOCK), lambda i, *_: (jnp.clip(i + off, 0, n_blocks - 1), 0, 0),
        memory_space=pltpu.SMEM)
    hbm = pl.BlockSpec(memory_space=pl.ANY)
    grid_spec = pltpu.PrefetchScalarGridSpec(
        num_scalar_prefetch=3,
        grid=(n_blocks,),
        in_specs=[
            smem_blk(0), smem_blk(1), smem_blk(2), smem_blk(0),
            pl.BlockSpec((MOE_BLOCK, 1), lambda i, *_: (i, 0)),
            hbm, hbm, hbm, hbm,
        ],
        out_specs=hbm,
        scratch_shapes=[
            pltpu.VMEM((X_SLOTS, tile_rows, LANES), F32),
            pltpu.VMEM((Y_SLOTS, tile_rows, LANES), F32),
            pltpu.VMEM((W_SLOTS, d, ff), F32), pltpu.VMEM((W_SLOTS, d, ff), F32),
            pltpu.VMEM((W_SLOTS, ff, d), F32),
            pltpu.VMEM((d, ff), BF16), pltpu.VMEM((d, ff), BF16), pltpu.VMEM((ff, d), BF16),
            pltpu.SemaphoreType.DMA((X_SLOTS,)), pltpu.SemaphoreType.DMA((Y_SLOTS,)),
            pltpu.SemaphoreType.DMA((W_SLOTS,)),
        ],
    )
    y_rows = (n_assign + Y_SLOTS * MOE_BLOCK) * ROW_TILES
    return pl.pallas_call(
        _moe_expert_kernel,
        grid_spec=grid_spec,
        out_shape=jax.ShapeDtypeStruct((y_rows, LANES), F32),
        compiler_params=_cp(("arbitrary",), vmem=MOE_VMEM_LIMIT),
        name="moe_experts",
    )(block_run, run_exp, n_used_runs, row_src, row_src, row_src, row_dst, row_w, u_all,
      w_gate, w_up, w_down)


def _moe_combine_kernel(h_ref, y0_ref, y1_ref, o_ref):
    rows = h_ref.shape[0]
    o_ref[...] = h_ref[...] + (_tiles_to_rows(y0_ref, rows) + _tiles_to_rows(y1_ref, rows))


def moe_combine(h, y_slots, row_off, slot_stride, *, tm):
    m, d = h.shape
    assert row_off % tm == 0 and slot_stride % tm == 0
    off0, off1 = row_off // tm, (row_off + slot_stride) // tm
    return pl.pallas_call(
        _moe_combine_kernel,
        grid=(m // tm,),
        in_specs=[pl.BlockSpec((tm, d), lambda i: (i, 0)),
                  pl.BlockSpec((tm * ROW_TILES, LANES), lambda i: (i + off0, 0)),
                  pl.BlockSpec((tm * ROW_TILES, LANES), lambda i: (i + off1, 0))],
        out_specs=pl.BlockSpec((tm, d), lambda i: (i, 0)),
        out_shape=jax.ShapeDtypeStruct((m, d), F32),
        compiler_params=_cp(("parallel",)),
        name="moe_combine",
    )(h, y_slots, y_slots)


def moe_dispatch(e_idx, gates, slot_stride):
    m = e_idx.shape[0]
    a = m * TOP_K
    e_flat = e_idx.reshape(a)
    _, order, gate_bits = lax.sort(
        (e_flat, jnp.arange(a, dtype=jnp.int32),
         lax.bitcast_convert_type(gates.reshape(a), jnp.int32)), num_keys=1, is_stable=True)
    sorted_rows = jnp.stack([order, gate_bits], axis=1)
    counts =jnp.sum(e_flat[:, None] == jnp.arange(N_EXPERTS, dtype=jnp.int32)[None, :],
                     axis=0, dtype=jnp.int32)
    pad_counts = (counts + MOE_BLOCK - 1) // MOE_BLOCK * MOE_BLOCK
    starts = jnp.cumsum(counts) - counts
    pad_ends = jnp.cumsum(pad_counts)
    pad_starts = pad_ends - pad_counts
    n_blocks = a // MOE_BLOCK + N_EXPERTS
    p = n_blocks * MOE_BLOCK
    n_used = (pad_ends[-1] // MOE_BLOCK).astype(jnp.int32)
    blk = jnp.arange(n_blocks, dtype=jnp.int32)
    blk_start = jnp.minimum(blk, n_used - 1) * MOE_BLOCK
    block_exp = jnp.minimum(jnp.sum(blk_start[:, None] >= pad_ends[None, :], axis=1),
                            N_EXPERTS - 1).astype(jnp.int32)
    in_exp = blk * MOE_BLOCK - pad_starts[block_exp]
    row_cnt = jnp.where(blk < n_used, jnp.clip(counts[block_exp] - in_exp, 0, MOE_BLOCK), 0)
    lane = jnp.arange(MOE_BLOCK, dtype=jnp.int32)[None, :]
    valid = lane < row_cnt[:, None]
    src = jnp.clip((starts[block_exp] + in_exp)[:, None] + lane, 0, a - 1)
    picked = sorted_rows[src]
    assign = picked[..., 0]
    row_tok = jnp.where(valid, assign // TOP_K, 0)
    pad_dst = TOP_K * slot_stride + (blk % Y_SLOTS)[:, None] * MOE_BLOCK + lane
    row_dst = jnp.where(valid, (assign % TOP_K) * slot_stride + assign // TOP_K, pad_dst)
    row_w = jnp.where(valid, lax.bitcast_convert_type(picked[..., 1], F32), 0.0)
    as_blocks = lambda x: (x * ROW_TILES).astype(jnp.int32).reshape(n_blocks, 1, MOE_BLOCK)
    has_rows = counts > 0
    run_exp = jnp.argsort(~has_rows, stable=True).astype(jnp.int32)
    block_run = (jnp.cumsum(has_rows) - 1)[block_exp].astype(jnp.int32)
    n_used_runs = jnp.stack([n_used, jnp.sum(has_rows, dtype=jnp.int32)])
    return (as_blocks(row_tok), as_blocks(row_dst), row_w.reshape(p, 1), block_run, run_exp,
            n_used_runs)


def rwkv_params(rw_mu, rw_w0, rw_w2, rw_a0, rw_a2, rw_g2, rw_k_k, rw_k_a, rw_r_k):
    c = RWKV_WIDTH
    mu = jnp.pad(rw_mu, (0, RWKV_PROJ_PAD - RWKV_PROJ)).reshape(1, RWKV_PROJ_PAD)
    w_lora = jnp.zeros((LANES, 2 * c), F32)
    w_lora = w_lora.at[0:DECAY_LORA, 0:c].set(rw_w2).at[DECAY_LORA:LANES, c:2 * c].set(rw_a2)
    w_gate = jnp.pad(rw_g2, ((0, GATE_PAD - GATE_LORA), (0, 0)))
    vec = lambda x: x.reshape(1, c)
    return (mu, vec(rw_w0), vec(rw_a0), vec(rw_k_k), vec(rw_k_a), vec(rw_r_k),
            w_lora.astype(BF16), w_gate.astype(BF16))


def _token_tiles(m):
    return (1024, 512) if m % 1024 == 0 else (m, m)


def _dense_front(x2d, wts, tm):
    pa = norm_matmul(x2d, wts['ln1_w'], wts['w_att'], tm=tm, tn=512)
    pr = norm_matmul(x2d, wts['ln1_w'], wts['w_rw'], tm=tm, tn=RWKV_PROJ_PAD // 3)
    return pa, pr


def _dense_back(x2d, att2d, rw2d, wts, xattn_fn, tm):
    h1 = matmul_residual([att2d, rw2d], [wts['w_out_a'], wts['w_out_r']], x2d, tm=tm, tn=512)
    qx = norm_matmul(h1, wts['ln2_w'], wts['xq_w'], tm=tm, tn=XATT_WIDTH)
    ox = xattn_fn(qx)
    return matmul_residual([ox], [wts['xo_w']], h1, tm=tm, tn=512)


def kernel(x_prompt, x_sample, cache_win_k, cache_win_v, state_wkv, state_shift, cache_mem_k, cache_mem_v, mem_prompt, ln1_w, w_in, q_norm_w, k_norm_w, attn_sinks, rw_mu, rw_w0, rw_w2, rw_a0, rw_a2, rw_g2, rw_k_k, rw_k_a, rw_r_k, rw_ln_w, rw_ln_b, w_out, ln2_w, mem_norm_w, xq_w, xkv_w, xq_norm_w, xk_norm_w, xo_w, ln3_w, router_group_w, router_group_b, router_expert_w, router_expert_b, exp_w_gate, exp_w_up, exp_w_down):
    assert w_in.shape[0] == 1, "single-layer stack"
    bp, seq, d = x_prompt.shape
    bs = x_sample.shape[0]
    mp = bp * seq
    c = RWKV_WIDTH

    router_w = jnp.concatenate(
        [router_group_w[0], router_expert_w[0],
         jnp.zeros((d, ROUTER_LANES - N_EXPERT_GROUPS - N_EXPERTS), F32)], axis=1)
    router_hi = router_w.astype(BF16)
    wts = {
        'ln1_w': ln1_w[0], 'ln2_w': ln2_w[0], 'ln3_w': ln3_w[0],
        'w_att': w_in[0][:, :ATT_PROJ].astype(BF16),
        'w_rw': jnp.pad(w_in[0][:, ATT_PROJ:],
                        ((0, 0), (0, RWKV_PROJ_PAD - RWKV_PROJ))).astype(BF16),
        'w_out_a': w_out[0][:ATT_WIDTH].astype(BF16),
        'w_out_r': w_out[0][ATT_WIDTH:].astype(BF16),
        'xq_w': xq_w[0].astype(BF16), 'xo_w': xo_w[0].astype(BF16),
        'router_hi': router_hi,
        'router_lo': (router_w - router_hi.astype(F32)).astype(BF16),
        'router_b': jnp.pad(jnp.concatenate([router_group_b[0], router_expert_b[0]]),
                            (0, ROUTER_LANES - N_EXPERT_GROUPS - N_EXPERTS)).reshape(1, -1),
    }
    rw_par = rwkv_params(rw_mu[0], rw_w0[0], rw_w2[0], rw_a0[0], rw_a2[0], rw_g2[0],
                         rw_k_k[0], rw_k_a[0], rw_r_k[0])

    tm_p, te_p = _token_tiles(mp)
    xp = x_prompt.reshape(mp, d)
    pa, pr = _dense_front(xp, wts, tm_p)
    pa3 = pa.reshape(bp, seq, ATT_PROJ)
    pr3 = pr.reshape(bp, seq, RWKV_PROJ_PAD)
    tabs_p = rope_tables(jnp.arange(seq, dtype=jnp.int32))
    att_p, kn_p = swa_prompt(pa3, tabs_p, q_norm_w[0], k_norm_w[0], attn_sinks[0])
    prep = rwkv_prep_seq(pr3, jnp.zeros((bp, 1, RWKV_PROJ_PAD), F32), rw_par)
    r, ld, kp, v, kk, b, bonus, g = prep
    y_p, wkv_p = rwkv_scan(r, ld, kp, v, kk, b)
    rw_p = rwkv_post(y_p.reshape(mp, c), bonus.reshape(mp, c), g.reshape(mp, c),
                     rw_ln_w[0], rw_ln_b[0], tm=te_p)

    n_mem = mem_prompt.shape[1]
    kv_mem = norm_matmul(mem_prompt.reshape(bp * n_mem, d), mem_norm_w[0],
                         xkv_w[0].astype(BF16), tm=bp * n_mem, tn=512)
    mem_k = head_rms(kv_mem[:, :XATT_WIDTH], xk_norm_w[0])
    mem_v = kv_mem[:, XATT_WIDTH:]
    mem_k3 = mem_k.reshape(bp, n_mem, XATT_WIDTH)
    mem_v3 = mem_v.reshape(bp, n_mem, XATT_WIDTH)

    def xattn_p(qx):
        return xattn_prompt(qx.reshape(bp, seq, XATT_WIDTH), mem_k3, mem_v3,
                            xq_norm_w[0]).reshape(mp, XATT_WIDTH)

    h2_p = _dense_back(xp, att_p.reshape(mp, ATT_WIDTH), rw_p, wts, xattn_p, tm_p)

    tm_s, te_s = _token_tiles(bs)
    xs = x_sample.reshape(bs, d)
    sa, sr = _dense_front(xs, wts, tm_s)
    tabs_s = rope_tables(PAST_LEN + jnp.arange(1, dtype=jnp.int32))
    qk_w = jnp.concatenate([jnp.tile(q_norm_w[0], ATT_HEADS),
                            jnp.tile(k_norm_w[0], ATT_KV_HEADS)]).reshape(1, -1)
    qk = qk_norm_rope(sa[:, :ATT_WIDTH + KV_WIDTH], qk_w, tabs_s)
    nbuf = cache_win_k.shape[2]

    def feature_major(cache):
        return jnp.transpose(cache, (0, 1, 3, 4, 2)).reshape(bs, KV_WIDTH, nbuf)

    def position_major(win):
        return jnp.transpose(win.reshape(1, bs, ATT_KV_HEADS, HEAD_DIM, nbuf), (0, 1, 4, 2, 3))

    att_s, win_k, win_v = swa_decode(
        qk[:, :ATT_WIDTH].reshape(bs, ATT_HEADS, HEAD_DIM),
        qk[:, ATT_WIDTH:].reshape(bs, 1, KV_WIDTH),
        sa[:, ATT_WIDTH + KV_WIDTH:].reshape(bs, 1, KV_WIDTH),
        feature_major(cache_win_k), feature_major(cache_win_v), attn_sinks[0])
    shift_prev = jnp.pad(state_shift[0], ((0, 0), (0, RWKV_PROJ_PAD - RWKV_PROJ)))
    r, ld, kp, v, kk, b, bonus, g = rwkv_prep_tok(sr, shift_prev, rw_par)
    y_s, wkv_s = rwkv_step(r, ld, kp, v, kk, b, jnp.transpose(state_wkv, (0, 2, 3, 4, 1)))
    wkv_s = jnp.transpose(wkv_s, (0, 4, 1, 2, 3))
    rw_s = rwkv_post(y_s, bonus, g, rw_ln_w[0], rw_ln_b[0], tm=te_s, y_channel_major=True)
    def xattn_s(qx):
        q_pad = jnp.pad(qx.reshape(bs, XATT_HEADS, XATT_HEAD_DIM), ((0, 0), (0, 4), (0, 0)))
        rows_of = lambda c: c.reshape(bs, n_mem * XATT_HEADS, XATT_HEAD_DIM)
        o = xattn_decode(q_pad, rows_of(cache_mem_k), rows_of(cache_mem_v), xq_norm_w[0])
        return o[:, :XATT_HEADS].reshape(bs, XATT_WIDTH)

    h2_s = _dense_back(xs, att_s.reshape(bs, ATT_WIDTH), rw_s, wts, xattn_s, tm_s)

    m_all = mp + bs
    slot_stride = m_all
    tc = math.gcd(mp, bs, 512)
    u_all, idx_all, gate_all = moe_router(h2_p, h2_s, wts['ln3_w'], wts['router_hi'],
                                          wts['router_lo'], wts['router_b'], tm=tc)
    row_src, row_dst, row_w, block_run, run_exp, n_used_runs = moe_dispatch(
        idx_all[:, :TOP_K], gate_all[:, :TOP_K], slot_stride)
    y_slots = moe_experts(u_all, row_src, row_dst, row_w, block_run, run_exp, n_used_runs,
                          exp_w_gate[0], exp_w_up[0], exp_w_down[0], TOP_K * slot_stride)
    out_p = moe_combine(h2_p, y_slots, 0, slot_stride, tm=tc)
    out_s = moe_combine(h2_s, y_slots, mp, slot_stride, tm=tc)

    win = min(WINDOW, seq)
    kv_shape = (1, bp, win, ATT_KV_HEADS, HEAD_DIM)
    return (
        out_p.reshape(bp, seq, d),
        out_s.reshape(bs, 1, d),
        kn_p[:, seq - win:].reshape(kv_shape),
        pa3[:, seq - win:, ATT_WIDTH + KV_WIDTH:].reshape(kv_shape),
        wkv_p[None],
        pr3[:, seq - 1, :RWKV_PROJ][None],
        mem_k3.reshape(1, bp, n_mem, XATT_HEADS, XATT_HEAD_DIM),
        mem_v3.reshape(1, bp, n_mem, XATT_HEADS, XATT_HEAD_DIM),
        position_major(win_k),
        position_major(win_v),
        wkv_s,
        sr[:, :RWKV_PROJ].reshape(1, bs, RWKV_PROJ),
    )
```

```python
import functools
import math

import jax
import jax.numpy as jnp
from jax import lax
from jax.experimental import pallas as pl
from jax.experimental.pallas import tpu as pltpu

F32 = jnp.float32
BF16 = jnp.bfloat16

D_MODEL = 2048
HEAD_DIM = 64
ATT_HEADS = 16
ATT_KV_HEADS = 4
ATT_GROUP = ATT_HEADS // ATT_KV_HEADS
ATT_WIDTH = ATT_HEADS * HEAD_DIM
KV_WIDTH = ATT_KV_HEADS * HEAD_DIM
ATT_PROJ = ATT_WIDTH + 2 * KV_WIDTH
WINDOW = 128
ATT_SCALE = HEAD_DIM ** -0.5
ROPE_THETA = 500000.0
ROT_DIM = HEAD_DIM // 4
PAST_LEN = 16384

RWKV_WIDTH = 1024
RWKV_HEAD_DIM = 64
RWKV_HEADS = 16
DECAY_LORA = 64
AAA_LORA = 64
GATE_LORA = 160
RWKV_PROJ = 3 * RWKV_WIDTH + DECAY_LORA + AAA_LORA + GATE_LORA
RWKV_PROJ_PAD = 3456

N_MEM = 256
XATT_HEADS = 4
XATT_HEAD_DIM = 128
XATT_WIDTH = XATT_HEADS * XATT_HEAD_DIM

N_EXPERT_GROUPS = 8
EXPERTS_PER_GROUP = 8
N_EXPERTS = 64
TOP_K = 2
EXPERT_FF = D_MODEL // 4
MOE_BLOCK = 128

RMS_EPS = 1e-6
GN_EPS = 64e-5

LANES = 128
CHUNK = 64
VMEM_LIMIT = 56 * 1024 * 1024
MOE_VMEM_LIMIT = 60 * 1024 * 1024


def _cp(sem, vmem=VMEM_LIMIT):
    return pltpu.CompilerParams(dimension_semantics=sem, vmem_limit_bytes=vmem)


def _rms_rows(x, w):
    ms = jnp.mean(x * x, axis=-1, keepdims=True)
    return x * lax.rsqrt(ms + RMS_EPS) * w


def _split2(x):
    hi = x.astype(BF16)
    lo = (x - hi.astype(F32)).astype(BF16)
    return hi, lo


def _split3(x):
    h1 = x.astype(BF16)
    r1 = x - h1.astype(F32)
    h2 = r1.astype(BF16)
    h3 = (r1 - h2.astype(F32)).astype(BF16)
    return h1, h2, h3


def _dot(a, b):
    return jnp.dot(a, b, preferred_element_type=F32)


def _dot_nt(a, b):
    return lax.dot_general(a, b, (((1,), (1,)), ((), ())), preferred_element_type=F32)


def _group_sum(x, gmat):
    hi, lo = _split2(x)
    return _dot(hi, gmat) + _dot(lo, gmat)


def _head_indicator():
    r = lax.broadcasted_iota(jnp.int32, (LANES, LANES), 0) // HEAD_DIM
    c = lax.broadcasted_iota(jnp.int32, (LANES, LANES), 1) // HEAD_DIM
    return jnp.where(r == c, 1.0, 0.0).astype(BF16)


def _norm_mm_kernel(x_ref, lnw_ref, w_ref, o_ref, xn_ref):
    @pl.when(pl.program_id(1) == 0)
    def _():
        xn_ref[...] = _rms_rows(x_ref[...], lnw_ref[...]).astype(BF16)

    o_ref[...] = _dot(xn_ref[...], w_ref[...])


def norm_matmul(x, ln_w, w_bf16, *, tm, tn):
    m, k = x.shape
    n = w_bf16.shape[1]
    assert m % tm == 0 and n % tn == 0
    return pl.pallas_call(
        _norm_mm_kernel,
        grid=(m // tm, n // tn),
        in_specs=[
            pl.BlockSpec((tm, k), lambda i, j: (i, 0)),
            pl.BlockSpec((1, k), lambda i, j: (0, 0)),
            pl.BlockSpec((k, tn), lambda i, j: (0, j)),
        ],
        out_specs=pl.BlockSpec((tm, tn), lambda i, j: (i, j)),
        out_shape=jax.ShapeDtypeStruct((m, n), F32),
        scratch_shapes=[pltpu.VMEM((tm, k), BF16)],
        compiler_params=_cp(("parallel", "arbitrary")),
        name="norm_matmul",
    )(x, ln_w.reshape(1, k), w_bf16)


def _mm_res_kernel(*refs, n_lhs):
    a_refs = refs[:n_lhs]
    w_refs = refs[n_lhs:2 * n_lhs]
    res_ref = refs[2 * n_lhs]
    o_ref = refs[2 * n_lhs + 1]
    acc = res_ref[...]
    for a_ref, w_ref in zip(a_refs, w_refs):
        acc = acc + _dot(a_ref[...].astype(BF16), w_ref[...])
    o_ref[...] = acc


def matmul_residual(lhs_list, w_list, res, *, tm, tn):
    m, n = res.shape
    n_lhs = len(lhs_list)
    assert m % tm == 0 and n % tn == 0
    in_specs = [pl.BlockSpec((tm, a.shape[1]), lambda i, j: (i, 0)) for a in lhs_list]
    in_specs += [pl.BlockSpec((w.shape[0], tn), lambda i, j: (0, j)) for w in w_list]
    in_specs += [pl.BlockSpec((tm, tn), lambda i, j: (i, j))]
    return pl.pallas_call(
        functools.partial(_mm_res_kernel, n_lhs=n_lhs),
        grid=(m // tm, n // tn),
        in_specs=in_specs,
        out_specs=pl.BlockSpec((tm, tn), lambda i, j: (i, j)),
        out_shape=jax.ShapeDtypeStruct((m, n), F32),
        compiler_params=_cp(("parallel", "arbitrary")),
        name="matmul_residual",
    )(*lhs_list, *w_list, res)


def rope_tables(pos):
    half = ROT_DIM // 2
    inv = ROPE_THETA ** (-jnp.arange(half, dtype=F32) * 2.0 / ROT_DIM)
    ang = pos.astype(F32)[:, None] * inv[None, :]
    cos, sin = jnp.cos(ang), jnp.sin(ang)
    t = pos.shape[0]
    ones = jnp.ones((t, HEAD_DIM - ROT_DIM), F32)
    zeros = jnp.zeros((t, HEAD_DIM - ROT_DIM), F32)
    z8 = jnp.zeros((t, half), F32)
    cos_t = jnp.concatenate([cos, cos, ones], axis=1)
    sin_a = jnp.concatenate([z8, sin, zeros], axis=1)
    sin_b = jnp.concatenate([-sin, z8, zeros], axis=1)
    return tuple(jnp.concatenate([a, a], axis=1) for a in (cos_t, sin_a, sin_b))


def _norm_rope_chunk(x, w, cos_t, sin_a, sin_b, gmat):
    ms = _group_sum(x * x, gmat) * (1.0 / HEAD_DIM)
    xn = x * lax.rsqrt(ms + RMS_EPS) * w
    half = ROT_DIM // 2
    return (xn * cos_t + pltpu.roll(xn, half, axis=1) * sin_a
            + pltpu.roll(xn, LANES - half, axis=1) * sin_b)


def _norm_rope(x, w, tabs, gmat):
    chunks = [
        _norm_rope_chunk(x[:, c * LANES:(c + 1) * LANES], w, *tabs, gmat)
        for c in range(x.shape[1] // LANES)
    ]
    return chunks[0] if len(chunks) == 1 else jnp.concatenate(chunks, axis=1)


def _sink_softmax(s, sink):
    m = jnp.maximum(jnp.max(s, axis=-1, keepdims=True), sink)
    e = jnp.exp(s - m)
    return e / (jnp.sum(e, axis=-1, keepdims=True) + jnp.exp(sink - m))


def _swa_prompt_kernel(q_ref, kc_ref, vc_ref, kp_ref, vp_ref, cc_ref, sac_ref, sbc_ref,
                       cp_ref, sap_ref, sbp_ref, qw_ref, kw_ref, sink_ref, o_ref, kn_ref):
    n = pl.program_id(1)
    blk = q_ref.shape[0]
    gmat = _head_indicator()
    tabs_c = (cc_ref[...], sac_ref[...], sbc_ref[...])
    tabs_p = (cp_ref[...], sap_ref[...], sbp_ref[...])
    q = _norm_rope(q_ref[...], qw_ref[...], tabs_c, gmat)
    k_cur = _norm_rope(kc_ref[...], kw_ref[...], tabs_c, gmat)
    k_prev = _norm_rope(kp_ref[...], kw_ref[...], tabs_p, gmat)
    kn_ref[...] = k_cur
    k_all = jnp.concatenate([k_prev, k_cur], axis=0).astype(BF16)
    v_all = jnp.concatenate([vp_ref[...], vc_ref[...]], axis=0).astype(BF16)

    qi = lax.broadcasted_iota(jnp.int32, (blk, 2 * blk), 0) + blk
    si = lax.broadcasted_iota(jnp.int32, (blk, 2 * blk), 1)
    rel = qi - si
    valid = (rel >= 0) & (rel <= WINDOW) & ((n > 0) | (si >= blk))

    groups = range(ATT_KV_HEADS)
    lanes = [slice(kv * HEAD_DIM, (kv + 1) * HEAD_DIM) for kv in groups]
    heads = [[kv * ATT_GROUP + g for g in range(ATT_GROUP)] for kv in groups]
    q_g = [jnp.concatenate([q[:, h * HEAD_DIM:(h + 1) * HEAD_DIM] for h in heads[kv]],
                           axis=0).astype(BF16) for kv in groups]
    s = [_dot_nt(q_g[kv], k_all[:, lanes[kv]]) * ATT_SCALE for kv in groups]
    p = [jnp.concatenate(
        [_sink_softmax(jnp.where(valid, s[kv][g * blk:(g + 1) * blk], -jnp.inf), sink_ref[h])
         for g, h in enumerate(heads[kv])], axis=0).astype(BF16) for kv in groups]
    o = [_dot(p[kv], v_all[:, lanes[kv]]) for kv in groups]
    for kv in groups:
        for g, h in enumerate(heads[kv]):
            o_ref[:, h * HEAD_DIM:(h + 1) * HEAD_DIM] = o[kv][g * blk:(g + 1) * blk]


def swa_prompt(pa, tabs, q_norm_w, k_norm_w, sinks):
    b, t, _ = pa.shape
    blk = WINDOW
    nb = t // blk
    qb, kb, vb = 0, ATT_WIDTH // KV_WIDTH, ATT_WIDTH // KV_WIDTH + 1
    cur = lambda i, n, *_: (i, n, 0)
    tab_cur = pl.BlockSpec((blk, LANES), lambda i, n: (n, 0))
    tab_prev = pl.BlockSpec((blk, LANES), lambda i, n: (jnp.maximum(n - 1, 0), 0))
    qw = jnp.tile(q_norm_w.reshape(1, HEAD_DIM), (1, 2))
    kw = jnp.tile(k_norm_w.reshape(1, HEAD_DIM), (1, 2))
    return pl.pallas_call(
        _swa_prompt_kernel,
        grid=(b, nb),
        in_specs=[
            pl.BlockSpec((None, blk, ATT_WIDTH), lambda i, n: (i, n, qb)),
            pl.BlockSpec((None, blk, KV_WIDTH), lambda i, n: (i, n, kb)),
            pl.BlockSpec((None, blk, KV_WIDTH), lambda i, n: (i, n, vb)),
            pl.BlockSpec((None, blk, KV_WIDTH), lambda i, n: (i, jnp.maximum(n - 1, 0), kb)),
            pl.BlockSpec((None, blk, KV_WIDTH), lambda i, n: (i, jnp.maximum(n - 1, 0), vb)),
            tab_cur, tab_cur, tab_cur, tab_prev, tab_prev, tab_prev,
            pl.BlockSpec((1, LANES), lambda i, n: (0, 0)),
            pl.BlockSpec((1, LANES), lambda i, n: (0, 0)),
            pl.BlockSpec(memory_space=pltpu.SMEM),
        ],
        out_specs=[
            pl.BlockSpec((None, blk, ATT_WIDTH), cur),
            pl.BlockSpec((None, blk, KV_WIDTH), cur),
        ],
        out_shape=[
            jax.ShapeDtypeStruct((b, t, ATT_WIDTH), F32),
            jax.ShapeDtypeStruct((b, t, KV_WIDTH), F32),
        ],
        compiler_params=_cp(("parallel", "arbitrary")),
        name="swa_prompt",
    )(pa, pa, pa, pa, pa, *tabs, *tabs, qw, kw, sinks)


def _qk_norm_rope_kernel(x_ref, w_ref, c_ref, sa_ref, sb_ref, o_ref):
    gmat = _head_indicator()
    tabs = (c_ref[...], sa_ref[...], sb_ref[...])
    for c in range(x_ref.shape[1] // LANES):
        sl = slice(c * LANES, (c + 1) * LANES)
        o_ref[:, sl] = _norm_rope_chunk(x_ref[:, sl], w_ref[:, sl], *tabs, gmat)


def qk_norm_rope(x, w_row, tabs):
    m, w = x.shape
    full = lambda *shape: pl.BlockSpec(shape, lambda: (0,) * len(shape))
    return pl.pallas_call(
        _qk_norm_rope_kernel,
        in_specs=[full(m, w), full(1, w), full(1, LANES), full(1, LANES), full(1, LANES)],
        out_specs=full(m, w),
        out_shape=jax.ShapeDtypeStruct((m, w), F32),
        name="qk_norm_rope",
    )(x, w_row, *tabs)


def _swa_decode_kernel(q_ref, kn_ref, vn_ref, knt_ref, vnt_ref, ck_ref, cv_ref, sink_ref, o_ref,
                       kw_ref, vw_ref):
    bb = q_ref.shape[0]
    nbuf = ck_ref.shape[2]
    row_kv = lax.broadcasted_iota(jnp.int32, (ATT_HEADS, KV_WIDTH), 0) // ATT_GROUP
    lane_kv = lax.broadcasted_iota(jnp.int32, (ATT_HEADS, KV_WIDTH), 1) // HEAD_DIM
    own = row_kv == lane_kv
    sink = sink_ref[...]
    seq = lax.broadcasted_iota(jnp.int32, knt_ref.shape, 1)
    pos = lax.broadcasted_iota(jnp.int32, (KV_WIDTH, nbuf), 1)
    b0 = pl.program_id(0) * bb
    for b in range(bb):
        q2 = q_ref[b]
        q_exp = jnp.where(own, jnp.concatenate([q2] * ATT_KV_HEADS, axis=1), 0.0)
        k_new, v_new = kn_ref[b], vn_ref[b]
        k_buf, v_buf = ck_ref[b], cv_ref[b]
        s_buf = _dot(q_exp.astype(BF16), k_buf.astype(BF16)) * ATT_SCALE
        s_new = jnp.sum(q_exp * k_new, axis=-1, keepdims=True) * ATT_SCALE
        m = jnp.maximum(jnp.maximum(jnp.max(s_buf, axis=-1, keepdims=True), s_new), sink)
        e_buf = jnp.exp(s_buf - m)
        e_new = jnp.exp(s_new - m)
        inv = 1.0 / (jnp.sum(e_buf, axis=-1, keepdims=True) + e_new + jnp.exp(sink - m))
        o = _dot_nt((e_buf * inv).astype(BF16), v_buf.astype(BF16)) + (e_new * inv) * v_new
        o = jnp.where(own, o, 0.0)
        o_ref[b] = (o[:, 0:HEAD_DIM] + o[:, HEAD_DIM:2 * HEAD_DIM]
                    + o[:, 2 * HEAD_DIM:3 * HEAD_DIM] + o[:, 3 * HEAD_DIM:4 * HEAD_DIM])
        k_col = jnp.sum(jnp.where(seq == b0 + b, knt_ref[...], 0.0), axis=1, keepdims=True)
        v_col = jnp.sum(jnp.where(seq == b0 + b, vnt_ref[...], 0.0), axis=1, keepdims=True)
        kw_ref[b] = jnp.where(pos == nbuf - 1, k_col, pltpu.roll(k_buf, nbuf - 1, axis=1))
        vw_ref[b] = jnp.where(pos == nbuf - 1, v_col, pltpu.roll(v_buf, nbuf - 1, axis=1))


def swa_decode(q, k_new, v_new, cache_kt, cache_vt, sinks, *, bb=8):
    b, _, nbuf = cache_kt.shape
    blk3 = lambda s1, s2: pl.BlockSpec((bb, s1, s2), lambda i: (i, 0, 0))
    whole = pl.BlockSpec((KV_WIDTH, b), lambda i: (0, 0))
    return pl.pallas_call(
        _swa_decode_kernel,
        grid=(b // bb,),
        in_specs=[
            blk3(ATT_HEADS, HEAD_DIM), blk3(1, KV_WIDTH), blk3(1, KV_WIDTH), whole, whole,
            blk3(KV_WIDTH, nbuf), blk3(KV_WIDTH, nbuf),
            pl.BlockSpec((ATT_HEADS, 1), lambda i: (0, 0)),
        ],
        out_specs=[blk3(ATT_HEADS, HEAD_DIM), blk3(KV_WIDTH, nbuf), blk3(KV_WIDTH, nbuf)],
        out_shape=[
            jax.ShapeDtypeStruct((b, ATT_HEADS, HEAD_DIM), F32),
            jax.ShapeDtypeStruct((b, KV_WIDTH, nbuf), F32),
            jax.ShapeDtypeStruct((b, KV_WIDTH, nbuf), F32),
        ],
        compiler_params=_cp(("parallel",)),
        name="swa_decode",
    )(q, k_new, v_new, k_new.reshape(b, KV_WIDTH).T, v_new.reshape(b, KV_WIDTH).T,
      cache_kt, cache_vt, sinks.reshape(ATT_HEADS, 1))


def _head_rms_kernel(x_ref, w_ref, o_ref):
    for h in range(x_ref.shape[1] // XATT_HEAD_DIM):
        sl = slice(h * XATT_HEAD_DIM, (h + 1) * XATT_HEAD_DIM)
        o_ref[:, sl] = _rms_rows(x_ref[:, sl], w_ref[...])


def head_rms(x, w):
    m, wd = x.shape
    return pl.pallas_call(
        _head_rms_kernel,
        in_specs=[pl.BlockSpec((m, wd), lambda: (0, 0)),
                  pl.BlockSpec((1, XATT_HEAD_DIM), lambda: (0, 0))],
        out_specs=pl.BlockSpec((m, wd), lambda: (0, 0)),
        out_shape=jax.ShapeDtypeStruct((m, wd), F32),
        name="head_rms",
    )(x, w.reshape(1, XATT_HEAD_DIM))


def _xattn_prompt_kernel(q_ref, k_ref, v_ref, w_ref, o_ref):
    scale = 1.0 / math.sqrt(XATT_HEAD_DIM)
    for h in range(XATT_HEADS):
        sl = slice(h * XATT_HEAD_DIM, (h + 1) * XATT_HEAD_DIM)
        qn = _rms_rows(q_ref[:, sl], w_ref[...]).astype(BF16)
        s = _dot_nt(qn, k_ref[:, sl].astype(BF16)) * scale
        e = jnp.exp(s - jnp.max(s, axis=-1, keepdims=True))
        p = e / jnp.sum(e, axis=-1, keepdims=True)
        o_ref[:, sl] = _dot(p.astype(BF16), v_ref[:, sl].astype(BF16))


def xattn_prompt(q, mem_k, mem_v, xq_norm_w, *, tq=512):
    b, t, w = q.shape
    n_mem = mem_k.shape[1]
    return pl.pallas_call(
        _xattn_prompt_kernel,
        grid=(b, t // tq),
        in_specs=[
            pl.BlockSpec((None, tq, w), lambda i, j: (i, j, 0)),
            pl.BlockSpec((None, n_mem, w), lambda i, j: (i, 0, 0)),
            pl.BlockSpec((None, n_mem, w), lambda i, j: (i, 0, 0)),
            pl.BlockSpec((1, XATT_HEAD_DIM), lambda i, j: (0, 0)),
        ],
        out_specs=pl.BlockSpec((None, tq, w), lambda i, j: (i, j, 0)),
        out_shape=jax.ShapeDtypeStruct((b, t, w), F32),
        compiler_params=_cp(("parallel", "arbitrary")),
        name="xattn_prompt",
    )(q, mem_k, mem_v, xq_norm_w.reshape(1, XATT_HEAD_DIM))


def _xattn_decode_kernel(q_ref, k_ref, v_ref, w_ref, o_ref):
    bb, rows, _ = q_ref.shape
    n_keys = k_ref.shape[1]
    scale = 1.0 / math.sqrt(XATT_HEAD_DIM)
    own = (lax.broadcasted_iota(jnp.int32, (rows, n_keys), 1) % XATT_HEADS
           == lax.broadcasted_iota(jnp.int32, (rows, n_keys), 0) % XATT_HEADS)
    seqs = range(bb)
    qn = [_rms_rows(q_ref[b], w_ref[...]).astype(BF16) for b in seqs]
    s = [jnp.where(own, _dot_nt(qn[b], k_ref[b].astype(BF16)) * scale, -jnp.inf) for b in seqs]
    e = [jnp.exp(s[b] - jnp.max(s[b], axis=-1, keepdims=True)) for b in seqs]
    p = [(e[b] / jnp.sum(e[b], axis=-1, keepdims=True)).astype(BF16) for b in seqs]
    for b in seqs:
        o_ref[b] = _dot(p[b], v_ref[b].astype(BF16))


def xattn_decode(q_pad, mem_k, mem_v, xq_norm_w, *, bb=8):
    b, rows, _ = q_pad.shape
    n_keys = mem_k.shape[1]
    kv = pl.BlockSpec((bb, n_keys, XATT_HEAD_DIM), lambda i: (i, 0, 0))
    return pl.pallas_call(
        _xattn_decode_kernel,
        grid=(b // bb,),
        in_specs=[pl.BlockSpec((bb, rows, XATT_HEAD_DIM), lambda i: (i, 0, 0)), kv, kv,
                  pl.BlockSpec((1, XATT_HEAD_DIM), lambda i: (0, 0))],
        out_specs=pl.BlockSpec((bb, rows, XATT_HEAD_DIM), lambda i: (i, 0, 0)),
        out_shape=jax.ShapeDtypeStruct((b, rows, XATT_HEAD_DIM), F32),
        compiler_params=_cp(("parallel",)),
        name="xattn_decode",
    )(q_pad, mem_k, mem_v, xq_norm_w.reshape(1, XATT_HEAD_DIM))


LORA_OFF = 3 * RWKV_WIDTH
GATE_OFF = LORA_OFF + DECAY_LORA + AAA_LORA
GATE_PAD = RWKV_PROJ_PAD - GATE_OFF


def _sigmoid(x):
    return 1.0 / (1.0 + jnp.exp(-x))


def _per_chunk(fn, *arrays):
    w = arrays[0].shape[1]
    outs = [fn(*(a[:, c * LANES:(c + 1) * LANES] for a in arrays)) for c in range(w // LANES)]
    return jnp.concatenate(outs, axis=1)


def _rwkv_prep_core(pr, prev, mu, w0, a0, kk_w, ka_w, rk_w, w_lora, w_gate):
    c = RWKV_WIDTH
    gmat = _head_indicator()
    xm = pr + (prev - pr) * mu
    r, k, v = xm[:, 0:c], xm[:, c:2 * c], xm[:, 2 * c:3 * c]
    lora = xm[:, LORA_OFF:LORA_OFF + LANES]
    lane = lax.broadcasted_iota(jnp.int32, lora.shape, 1)
    lora_in = jnp.where(lane < DECAY_LORA, jnp.tanh(lora), lora)
    wa = _dot(lora_in.astype(BF16), w_lora)
    z = -(w0 + wa[:, 0:c])
    softplus = jnp.maximum(z, 0.0) + jnp.log(1.0 + jnp.exp(-jnp.abs(z)))
    log_decay = -jnp.exp(-softplus - 0.5)
    a = _sigmoid(a0 + wa[:, c:2 * c])
    g = _dot(_sigmoid(xm[:, GATE_OFF:GATE_OFF + GATE_PAD]).astype(BF16), w_gate)
    kk = k * kk_w
    norm = jnp.sqrt(_per_chunk(lambda t: _group_sum(t * t, gmat), kk))
    kk = kk / jnp.maximum(norm, 1e-12)
    kp = k * (1.0 + (a - 1.0) * ka_w)
    bonus = _per_chunk(lambda t: _group_sum(t, gmat), r * kp * rk_w) * v
    return r, log_decay, kp, v, kk, kk * a, bonus, g


def _rwkv_prep_tok_kernel(pr_ref, prev_ref, mu_ref, w0_ref, a0_ref, kkw_ref, kaw_ref, rkw_ref,
                          wl_ref, wg_ref, *out_refs):
    outs = _rwkv_prep_core(pr_ref[...], prev_ref[...], mu_ref[...], w0_ref[...], a0_ref[...],
                           kkw_ref[...], kaw_ref[...], rkw_ref[...], wl_ref[...], wg_ref[...])
    for k, (o_ref, o) in enumerate(zip(out_refs, outs)):
        o_ref[...] = o.T if k < N_STEP_VECS else o


def _rwkv_param_specs(index_map):
    c = RWKV_WIDTH
    shapes = [(1, RWKV_PROJ_PAD)] + [(1, c)] * 5 + [(LANES, 2 * c), (GATE_PAD, c)]
    return [pl.BlockSpec(s, index_map) for s in shapes]


N_STEP_VECS = 6


def rwkv_prep_tok(pr, prev, params):
    m, wd = pr.shape
    c = RWKV_WIDTH
    shapes = [(c, m)] * N_STEP_VECS + [(m, c)] * 2
    return pl.pallas_call(
        _rwkv_prep_tok_kernel,
        grid=(1,),
        in_specs=[pl.BlockSpec((m, wd), lambda i: (0, 0))] * 2
        + _rwkv_param_specs(lambda i: (0, 0)),
        out_specs=[pl.BlockSpec(s, lambda i: (0, 0)) for s in shapes],
        out_shape=[jax.ShapeDtypeStruct(s, F32) for s in shapes],
        compiler_params=_cp(("arbitrary",)),
        name="rwkv_prep_tok",
    )(pr, prev, *params)


def _dot_tn(a, b):
    return lax.dot_general(a, b, (((0,), (0,)), ((), ())), preferred_element_type=F32)


def _rwkv_scan_tile(r_all, ld_all, kp_all, v_all, kk_all, b_all, s_cur, y_ref):
    n = CHUNK
    nc = r_all.shape[0] // n
    ti = lax.broadcasted_iota(jnp.int32, (n, n), 0)
    si = lax.broadcasted_iota(jnp.int32, (n, n), 1)
    tri = jnp.where(si <= ti, 1.0, 0.0).astype(BF16)
    t2 = lax.broadcasted_iota(jnp.int32, (n, LANES), 0)
    lane2 = lax.broadcasted_iota(jnp.int32, (n, LANES), 1)
    s2 = lane2 % n
    low = lane2 < n
    strict2, incl2, eye2 = s2 < t2, s2 <= t2, s2 == t2
    low4 = lax.broadcasted_iota(jnp.int32, (2 * n, LANES), 1) < n
    top4 = lax.broadcasted_iota(jnp.int32, (2 * n, LANES), 0) < n
    diag_blk = top4 == low4
    own4 = {0: low4, 1: ~low4}
    pairs = range(RWKV_HEADS // 2)
    cs = [slice(c * LANES, (c + 1) * LANES) for c in pairs]

    pre = []
    for j in range(nc):
        rows = slice(j * n, (j + 1) * n)
        ld = ld_all[rows, :]
        l1, l2, l3 = _split3(ld)
        lc = _dot(tri, l1) + _dot(tri, l2) + _dot(tri, l3)
        lc_end = lc[n - 1:n, :]
        e_neg = jnp.exp(-lc)
        kk, b, kp = kk_all[rows, :], b_all[rows, :], kp_all[rows, :]
        to_end = jnp.exp(lc_end - lc)
        pre.append(dict(
            a_t=(-kk * jnp.exp(lc - ld)).astype(BF16), b_t=(b * e_neg).astype(BF16),
            k_t=(kp * e_neg).astype(BF16), r_t=(r_all[rows, :] * jnp.exp(lc)).astype(BF16),
            b_e=(b * to_end).astype(BF16), k_e=(kp * to_end).astype(BF16),
            v_b=v_all[rows, :].astype(BF16), g_end=jnp.exp(lc_end)))
    items = [(j, c, par) for j in range(nc) for c in pairs for par in (0, 1)]
    at = {it: i for i, it in enumerate(items)}
    ar = {(j, c): jnp.concatenate([pre[j]['a_t'][:, cs[c]], pre[j]['r_t'][:, cs[c]]], axis=0)
          for j in range(nc) for c in pairs}
    kb = {(j, c): jnp.concatenate([pre[j]['k_t'][:, cs[c]], pre[j]['b_t'][:, cs[c]]], axis=0)
          for j in range(nc) for c in pairs}
    gm = [_dot_nt(jnp.where(own4[par], ar[j, c], 0).astype(BF16), kb[j, c]) for j, c, par in items]
    top = [g[0:n] for g in gm]
    pm = [jnp.where(incl2, g[n:2 * n], 0.0).astype(BF16) for g in gm]
    lak = [jnp.where(strict2 & low, t, 0.0).astype(BF16) for t in top]
    lv = [_dot(lak[i][:, 0:n], pre[j]['v_b'][:, cs[c]]) for i, (j, c, par) in enumerate(items)]
    z = [jnp.where(low, jnp.where(eye2, 1.0, 0.0), jnp.where(strict2, t, 0.0)) for t in top]
    for _ in range(6):
        zb = [zz.astype(BF16) for zz in z]
        res = [_dot(jnp.where(low, 0, zb[i]).astype(BF16), jnp.concatenate([zb[i], zb[i]], axis=0))
               for i in range(len(items))]
        z = [res[i] + jnp.where(low, z[i], 0.0) for i in range(len(items))]
    tmat = [zz[:, 0:n].astype(BF16) for zz in z]

    for j in range(nc):
        p = pre[j]
        sw = [_dot_nt(ar[j, c], s_cur[c].astype(BF16)) for c in pairs]
        w0 = [(sw[c][0:n] + jnp.where(low, lv[at[j, c, 0]], lv[at[j, c, 1]])).astype(BF16)
              for c in pairs]
        u = [jnp.where(low, _dot(tmat[at[j, c, 0]], w0[c]),
                       _dot(tmat[at[j, c, 1]], w0[c])).astype(BF16) for c in pairs]
        vu = [jnp.concatenate([p['v_b'][:, cs[c]], u[c]], axis=0) for c in pairs]
        yb = [jnp.where(low, _dot(pm[at[j, c, 0]], vu[c]), _dot(pm[at[j, c, 1]], vu[c]))
              for c in pairs]
        for c in pairs:
            y_ref[j * n:(j + 1) * n, cs[c]] = sw[c][n:2 * n] + yb[c]
        upd = [_dot_tn(jnp.concatenate([u[c], p['v_b'][:, cs[c]]], axis=0),
                       jnp.concatenate([p['b_e'][:, cs[c]], p['k_e'][:, cs[c]]], axis=0))
               for c in pairs]
        s_cur = [s_cur[c] * p['g_end'][:, cs[c]] + jnp.where(diag_blk, upd[c], 0.0) for c in pairs]
    return s_cur


def _group_norm_gate(y, bonus, g, ln_w, ln_b, gmat):
    inv = 1.0 / RWKV_HEAD_DIM
    d = y - _group_sum(y, gmat) * inv
    var = _group_sum(d * d, gmat) * inv
    return (d * lax.rsqrt(var + GN_EPS) * ln_w + ln_b + bonus) * g


def _rwkv_seq_kernel(pr_ref, prev0_ref, mu_ref, w0_ref, a0_ref, kkw_ref, kaw_ref, rkw_ref,
                     wl_ref, wg_ref, lnw_ref, lnb_ref, o_ref, s_out_ref, s_ref, last_ref, y_ref):
    @pl.when(pl.program_id(1) == 0)
    def _():
        s_ref[...] = jnp.zeros_like(s_ref)
        last_ref[...] = prev0_ref[...]

    pr = pr_ref[...]
    rows = pr.shape[0]
    row = lax.broadcasted_iota(jnp.int32, (rows, 1), 0)
    prev = jnp.where(row == 0, last_ref[...], pltpu.roll(pr, 1, axis=0))
    last_ref[...] = pr[rows - 1:rows, :]
    r, ld, kp, v, kk, b, bonus, g = _rwkv_prep_core(
        pr, prev, mu_ref[...], w0_ref[...], a0_ref[...], kkw_ref[...], kaw_ref[...],
        rkw_ref[...], wl_ref[...], wg_ref[...])
    pairs = range(RWKV_HEADS // 2)
    s_new = _rwkv_scan_tile(r, ld, kp, v, kk, b, [s_ref[c] for c in pairs], y_ref)
    n = RWKV_HEAD_DIM
    for c in pairs:
        s_ref[c] = s_new[c]
        s_out_ref[2 * c] = s_new[c][0:n, 0:n]
        s_out_ref[2 * c + 1] = s_new[c][n:2 * n, n:2 * n]
    gmat = _head_indicator()
    for c in pairs:
        sl = slice(c * LANES, (c + 1) * LANES)
        o_ref[:, sl] = _group_norm_gate(y_ref[:, sl], bonus[:, sl], g[:, sl], lnw_ref[:, sl],
                                        lnb_ref[:, sl], gmat)


SCAN_CHUNKS = 4


def rwkv_seq(pr, prev0, params, ln_w, ln_b):
    bsz, t, wd = pr.shape
    c = RWKV_WIDTH
    rows = SCAN_CHUNKS * CHUNK
    assert t % rows == 0
    vec = pl.BlockSpec((1, c), lambda i, j: (0, 0))
    st = pl.BlockSpec((None, RWKV_HEADS, RWKV_HEAD_DIM, RWKV_HEAD_DIM), lambda i, j: (i, 0, 0, 0))
    return pl.pallas_call(
        _rwkv_seq_kernel,
        grid=(bsz, t // rows),
        in_specs=[pl.BlockSpec((None, rows, wd), lambda i, j: (i, j, 0)),
                  pl.BlockSpec((None, 1, wd), lambda i, j: (i, 0, 0))]
        + _rwkv_param_specs(lambda i, j: (0, 0)) + [vec, vec],
        out_specs=[pl.BlockSpec((None, rows, c), lambda i, j: (i, j, 0)), st],
        out_shape=[jax.ShapeDtypeStruct((bsz, t, c), F32),
                   jax.ShapeDtypeStruct((bsz, RWKV_HEADS, RWKV_HEAD_DIM, RWKV_HEAD_DIM), F32)],
        scratch_shapes=[pltpu.VMEM((RWKV_HEADS // 2, LANES, LANES), F32),
                        pltpu.VMEM((1, wd), F32), pltpu.VMEM((rows, c), F32)],
        compiler_params=_cp(("parallel", "arbitrary")),
        name="rwkv_seq",
    )(pr, prev0, *params, ln_w.reshape(1, c), ln_b.reshape(1, c))


STEP_UNROLL = 8


def _rwkv_step_kernel(r_ref, ld_ref, kp_ref, v_ref, kk_ref, b_ref, s_ref, y_ref, s_out_ref):
    n = RWKV_HEAD_DIM
    neg_kk, decay = -kk_ref[...], jnp.exp(ld_ref[...])
    b_mat, kp_mat, r_mat = b_ref[...], kp_ref[...], r_ref[...]

    def body(i, carry):
        v0 = pl.multiple_of(i * STEP_UNROLL, STEP_UNROLL)
        v_rows = v_ref[pl.ds(v0, STEP_UNROLL), :]
        rows = range(STEP_UNROLL)
        s = [s_ref[v0 + j] for j in rows]
        sa = [jnp.sum(s[j] * neg_kk, axis=0, keepdims=True) for j in rows]
        s_new = [s[j] * decay + sa[j] * b_mat + v_rows[j:j + 1, :] * kp_mat for j in rows]
        y = [jnp.sum(s_new[j] * r_mat, axis=0, keepdims=True) for j in rows]
        for j in rows:
            s_out_ref[v0 + j] = s_new[j]
        y_ref[pl.ds(v0, STEP_UNROLL), :] = jnp.concatenate(y, axis=0)
        return carry

    lax.fori_loop(0, n // STEP_UNROLL, body, 0)


def rwkv_step(r, ld, kp, v, kk, b, state_t):
    _, nh, n, _, bsz = state_t.shape
    vec = pl.BlockSpec((n, bsz), lambda h: (h, 0))
    st = pl.BlockSpec((None, None, n, n, bsz), lambda h: (0, h, 0, 0, 0))
    return pl.pallas_call(
        _rwkv_step_kernel,
        grid=(nh,),
        in_specs=[vec] * 6 + [st],
        out_specs=[vec, st],
        out_shape=[jax.ShapeDtypeStruct((nh * n, bsz), F32),
                   jax.ShapeDtypeStruct(state_t.shape, F32)],
        compiler_params=_cp(("parallel",)),
        name="rwkv_step",
    )(r, ld, kp, v, kk, b, state_t)


def _rwkv_post_kernel(y_ref, bonus_ref, g_ref, lnw_ref, lnb_ref, o_ref, *, y_channel_major):
    gmat = _head_indicator()
    for c in range(o_ref.shape[1] // LANES):
        sl = slice(c * LANES, (c + 1) * LANES)
        y = y_ref[sl, :].T if y_channel_major else y_ref[:, sl]
        o_ref[:, sl] = _group_norm_gate(y, bonus_ref[:, sl], g_ref[:, sl], lnw_ref[:, sl],
                                        lnb_ref[:, sl], gmat)


def rwkv_post(y, bonus, g, ln_w, ln_b, *, tm, y_channel_major=False):
    m, c = bonus.shape
    blk = pl.BlockSpec((tm, c), lambda i: (i, 0))
    y_blk = pl.BlockSpec((c, tm), lambda i: (0, i)) if y_channel_major else blk
    vec = pl.BlockSpec((1, c), lambda i: (0, 0))
    return pl.pallas_call(
        functools.partial(_rwkv_post_kernel, y_channel_major=y_channel_major),
        grid=(m // tm,),
        in_specs=[y_blk, blk, blk, vec, vec],
        out_specs=blk,
        out_shape=jax.ShapeDtypeStruct((m, c), F32),
        compiler_params=_cp(("parallel",)),
        name="rwkv_post",
    )(y, bonus, g, ln_w.reshape(1, c), ln_b.reshape(1, c))


ROUTER_LANES = LANES
ROW_TILES = D_MODEL // LANES


def _rows_to_tiles(ref, x):
    rows = x.shape[0]
    for j in range(ROW_TILES):
        ref[pl.ds(j, rows, stride=ROW_TILES), :] = x[:, j * LANES:(j + 1) * LANES]


def _tiles_to_rows(ref, rows):
    return jnp.concatenate(
        [ref[pl.ds(j, rows, stride=ROW_TILES), :] for j in range(ROW_TILES)], axis=1)


def _router_kernel(ha_ref, hb_ref, lnw_ref, whi_ref, wlo_ref, bias_ref, u_ref, idx_ref, gate_ref,
                   *, steps_a):
    use_a = pl.program_id(0) < steps_a
    h = jnp.where(use_a, ha_ref[...], hb_ref[...])
    u = _rms_rows(h, lnw_ref[...])
    _rows_to_tiles(u_ref, u)
    u_hi, u_lo = _split2(u)
    w_hi = whi_ref[...]
    logits = _dot(u_hi, w_hi) + _dot(u_lo, w_hi) + _dot(u_hi, wlo_ref[...]) + bias_ref[...]
    lane = lax.broadcasted_iota(jnp.int32, logits.shape, 1)
    neg = -jnp.inf

    def first_max(x):
        m = jnp.max(x, axis=1, keepdims=True)
        return m, jnp.min(jnp.where(x == m, lane, ROUTER_LANES), axis=1, keepdims=True)

    gl = jnp.where(lane < N_EXPERT_GROUPS, logits, neg)
    g_max, g_idx = first_max(gl)
    g_gate = 1.0 / jnp.sum(jnp.exp(gl - g_max), axis=1, keepdims=True)
    lo = N_EXPERT_GROUPS + g_idx * EXPERTS_PER_GROUP
    el = jnp.where((lane >= lo) & (lane < lo + EXPERTS_PER_GROUP), logits, neg)
    v1, i1 = first_max(el)
    v2, i2 = first_max(jnp.where(lane == i1, neg, el))
    e2 = jnp.exp(v2 - v1)
    w1 = g_gate / (1.0 + e2)
    w2 = g_gate * e2 / (1.0 + e2)
    idx_ref[...] = jnp.where(lane == 0, i1 - N_EXPERT_GROUPS,
                             jnp.where(lane == 1, i2 - N_EXPERT_GROUPS, 0))
    gate_ref[...] = jnp.where(lane == 0, w1, jnp.where(lane == 1, w2, 0.0))


def moe_router(h_a, h_b, ln_w, w_hi, w_lo, bias, *, tm):
    (ma, d), mb = h_a.shape, h_b.shape[0]
    assert ma % tm == 0 and mb % tm == 0
    steps_a, steps_b = ma // tm, mb // tm
    m = ma + mb
    const = lambda r, w: pl.BlockSpec((r, w), lambda i: (0, 0))
    row = lambda w: pl.BlockSpec((tm, w), lambda i: (i, 0))
    return pl.pallas_call(
        functools.partial(_router_kernel, steps_a=steps_a),
        grid=(steps_a + steps_b,),
        in_specs=[pl.BlockSpec((tm, d), lambda i: (jnp.minimum(i, steps_a - 1), 0)),
                  pl.BlockSpec((tm, d), lambda i: (jnp.maximum(i - steps_a, 0), 0)),
                  const(1, d), const(d, ROUTER_LANES), const(d, ROUTER_LANES),
                  const(1, ROUTER_LANES)],
        out_specs=[pl.BlockSpec((tm * (d // LANES), LANES), lambda i: (i, 0)),
                   row(ROUTER_LANES), row(ROUTER_LANES)],
        out_shape=[jax.ShapeDtypeStruct((m * (d // LANES), LANES), F32),
                   jax.ShapeDtypeStruct((m, ROUTER_LANES), jnp.int32),
                   jax.ShapeDtypeStruct((m, ROUTER_LANES), F32)],
        compiler_params=_cp(("arbitrary",)),
        name="moe_router",
    )(h_a, h_b, ln_w.reshape(1, d), w_hi, w_lo, bias)


X_SLOTS = 3
Y_SLOTS = 2


W_SLOTS = 3
DMA_QUEUES = 2


def _moe_expert_kernel(run_ref, rexp_ref, nused_ref, tok0_ref, tok1_ref, tok2_ref, dst_ref,
                       roww_ref, u_hbm, wg_hbm, wu_hbm, wd_hbm, y_hbm, xbuf, ybuf, wg_f, wu_f,
                       wd_f, wg_b, wu_b, wd_b, sem_in, sem_out, sem_w):
    i = pl.program_id(0)
    n_used, n_runs = nused_ref[0], nused_ref[1]
    tile_rows = MOE_BLOCK * ROW_TILES
    pad_base = y_hbm.shape[0] - Y_SLOTS * tile_rows
    run = run_ref[i]

    def weight_copies(k):
        e, s = rexp_ref[jnp.minimum(k, n_runs - 1)], lax.rem(k, W_SLOTS)
        copies = []
        for hbm, buf in ((wg_hbm, wg_f), (wu_hbm, wu_f), (wd_hbm, wd_f)):
            rows = buf.shape[1] // 2
            for part in range(2):
                sl = pl.ds(part * rows, rows)
                copies.append((pltpu.make_async_copy(hbm.at[e, sl], buf.at[s, sl], sem_w.at[s]),
                               part))
        return copies

    def gather_block(idx_ref, x_slot):
        for r in range(MOE_BLOCK):
            pltpu.make_async_copy(u_hbm.at[pl.ds(idx_ref[0, 0, r], ROW_TILES)],
                                  xbuf.at[x_slot, pl.ds(r * ROW_TILES, ROW_TILES)],
                                  sem_in.at[x_slot]).start(priority=r % DMA_QUEUES)

    def scatter_rows(idx_ref, y_slot, rows):
        for r in rows:
            pltpu.make_async_copy(ybuf.at[y_slot, pl.ds(r * ROW_TILES, ROW_TILES)],
                                  y_hbm.at[pl.ds(idx_ref[0, 0, r], ROW_TILES)],
                                  sem_out.at[y_slot]).start(priority=r % DMA_QUEUES)

    def gather_wait(x_slot):
        pltpu.make_async_copy(u_hbm.at[pl.ds(0, tile_rows)], xbuf.at[x_slot],
                              sem_in.at[x_slot]).wait()

    def scatter_wait(y_slot):
        pltpu.make_async_copy(ybuf.at[y_slot], y_hbm.at[pl.ds(0, tile_rows)],
                              sem_out.at[y_slot]).wait()

    @pl.when(i == 0)
    def _():
        ybuf[0] = jnp.zeros(ybuf.shape[1:], F32)
        for s in range(Y_SLOTS):
            pltpu.make_async_copy(ybuf.at[0], y_hbm.at[pl.ds(pad_base + s * tile_rows, tile_rows)],
                                  sem_out.at[s]).start()
        for k in range(W_SLOTS - 1):
            for cp, queue in weight_copies(k):
                cp.start(priority=queue)
        gather_block(tok0_ref, 0)
        gather_block(tok1_ref, 1)
        for s in range(Y_SLOTS):
            scatter_wait(s)

    @pl.when(i < n_used)
    def _():
        x_slot = lax.rem(i, X_SLOTS)
        y_slot = lax.rem(i, Y_SLOTS)

        @pl.when((i == 0) | (run != run_ref[jnp.maximum(i - 1, 0)]))
        def _():
            for cp, _ in weight_copies(run):
                cp.wait()
            w_slot = lax.rem(run, W_SLOTS)
            wg_b[...] = wg_f[w_slot].astype(BF16)
            wu_b[...] = wu_f[w_slot].astype(BF16)
            wd_b[...] = wd_f[w_slot].astype(BF16)
            for cp, queue in weight_copies(run + W_SLOTS - 1):
                cp.start(priority=queue)

        gather_wait(x_slot)

        @pl.when(i >= Y_SLOTS)
        def _():
            scatter_wait(y_slot)

        x = _tiles_to_rows(xbuf.at[x_slot], MOE_BLOCK).astype(BF16)
        hg = _dot(x, wg_b[...])
        hu = _dot(x, wu_b[...])
        act = (hg * _sigmoid(hg) * hu).astype(BF16)
        y = _dot(act, wd_b[...]) * roww_ref[...]
        _rows_to_tiles(ybuf.at[y_slot], y)
        scatter_rows(dst_ref, y_slot, range(MOE_BLOCK))
        gather_block(tok2_ref, lax.rem(i + 2, X_SLOTS))

        @pl.when(i == n_used - 1)
        def _():
            scatter_wait(y_slot)

            @pl.when(i >= 1)
            def _():
                scatter_wait(1 - y_slot)

            gather_wait(lax.rem(i + 1, X_SLOTS))
            gather_wait(lax.rem(i + 2, X_SLOTS))
            for k in range(W_SLOTS - 1):
                for cp, _ in weight_copies(n_runs + k):
                    cp.wait()


def moe_experts(u_all, row_src, row_dst, row_w, block_run, run_exp, n_used_runs, w_gate, w_up,
                w_down, n_assign):
    d, ff = w_gate.shape[1], w_gate.shape[2]
    n_blocks = row_src.shape[0]
    tile_rows = MOE_BLOCK * ROW_TILES
    smem_blk = lambda off: pl.BlockSpec(
        (1, 1, MOE_BLOCK), lambda i, *_: (jnp.clip(i + off, 0, n_blocks - 1), 0, 0),
        memory_space=pltpu.SMEM)
    hbm = pl.BlockSpec(memory_space=pl.ANY)
    grid_spec = pltpu.PrefetchScalarGridSpec(
        num_scalar_prefetch=3,
        grid=(n_blocks,),
        in_specs=[
            smem_blk(0), smem_blk(1), smem_blk(2), smem_blk(0),
            pl.BlockSpec((MOE_BLOCK, 1), lambda i, *_: (i, 0)),
            hbm, hbm, hbm, hbm,
        ],
        out_specs=hbm,
        scratch_shapes=[
            pltpu.VMEM((X_SLOTS, tile_rows, LANES), F32),
            pltpu.VMEM((Y_SLOTS, tile_rows, LANES), F32),
            pltpu.VMEM((W_SLOTS, d, ff), F32), pltpu.VMEM((W_SLOTS, d, ff), F32),
            pltpu.VMEM((W_SLOTS, ff, d), F32),
            pltpu.VMEM((d, ff), BF16), pltpu.VMEM((d, ff), BF16), pltpu.VMEM((ff, d), BF16),
            pltpu.SemaphoreType.DMA((X_SLOTS,)), pltpu.SemaphoreType.DMA((Y_SLOTS,)),
            pltpu.SemaphoreType.DMA((W_SLOTS,)),
        ],
    )
    y_rows = (n_assign + Y_SLOTS * MOE_BLOCK) * ROW_TILES
    return pl.pallas_call(
        _moe_expert_kernel,
        grid_spec=grid_spec,
        out_shape=jax.ShapeDtypeStruct((y_rows, LANES), F32),
        compiler_params=_cp(("arbitrary",), vmem=MOE_VMEM_LIMIT),
        name="moe_experts",
    )(block_run, run_exp, n_used_runs, row_src, row_src, row_src, row_dst, row_w, u_all,
      w_gate, w_up, w_down)


def _moe_combine_kernel(h_ref, y0_ref, y1_ref, o_ref):
    rows = h_ref.shape[0]
    o_ref[...] = h_ref[...] + (_tiles_to_rows(y0_ref, rows) + _tiles_to_rows(y1_ref, rows))


def moe_combine(h, y_slots, row_off, slot_stride, *, tm):
    m, d = h.shape
    assert row_off % tm == 0 and slot_stride % tm == 0
    off0, off1 = row_off // tm, (row_off + slot_stride) // tm
    return pl.pallas_call(
        _moe_combine_kernel,
        grid=(m // tm,),
        in_specs=[pl.BlockSpec((tm, d), lambda i: (i, 0)),
                  pl.BlockSpec((tm * ROW_TILES, LANES), lambda i: (i + off0, 0)),
                  pl.BlockSpec((tm * ROW_TILES, LANES), lambda i: (i + off1, 0))],
        out_specs=pl.BlockSpec((tm, d), lambda i: (i, 0)),
        out_shape=jax.ShapeDtypeStruct((m, d), F32),
        compiler_params=_cp(("parallel",)),
        name="moe_combine",
    )(h, y_slots, y_slots)


def moe_dispatch(e_idx, gates, slot_stride):
    m = e_idx.shape[0]
    a = m * TOP_K
    e_flat = e_idx.reshape(a)
    _, order, gate_bits = lax.sort(
        (e_flat, jnp.arange(a, dtype=jnp.int32),
         lax.bitcast_convert_type(gates.reshape(a), jnp.int32)), num_keys=1, is_stable=True)
    gate_sorted = lax.bitcast_convert_type(gate_bits, F32)
    counts =jnp.sum(e_flat[:, None] == jnp.arange(N_EXPERTS, dtype=jnp.int32)[None, :],
                     axis=0, dtype=jnp.int32)
    pad_counts = (counts + MOE_BLOCK - 1) // MOE_BLOCK * MOE_BLOCK
    starts = jnp.cumsum(counts) - counts
    pad_ends = jnp.cumsum(pad_counts)
    pad_starts = pad_ends - pad_counts
    n_blocks = a // MOE_BLOCK + N_EXPERTS
    p = n_blocks * MOE_BLOCK
    n_used = (pad_ends[-1] // MOE_BLOCK).astype(jnp.int32)
    blk = jnp.arange(n_blocks, dtype=jnp.int32)
    blk_start = jnp.minimum(blk, n_used - 1) * MOE_BLOCK
    block_exp = jnp.minimum(jnp.sum(blk_start[:, None] >= pad_ends[None, :], axis=1),
                            N_EXPERTS - 1).astype(jnp.int32)
    in_exp = blk * MOE_BLOCK - pad_starts[block_exp]
    row_cnt = jnp.where(blk < n_used, jnp.clip(counts[block_exp] - in_exp, 0, MOE_BLOCK), 0)
    lane = jnp.arange(MOE_BLOCK, dtype=jnp.int32)[None, :]
    valid = lane < row_cnt[:, None]
    src = jnp.clip((starts[block_exp] + in_exp)[:, None] + lane, 0, a - 1)
    assign = order[src]
    row_tok = jnp.where(valid, assign // TOP_K, 0)
    pad_dst = TOP_K * slot_stride + (blk % Y_SLOTS)[:, None] * MOE_BLOCK + lane
    row_dst = jnp.where(valid, (assign % TOP_K) * slot_stride + assign // TOP_K, pad_dst)
    row_w = jnp.where(valid, gate_sorted[src], 0.0)
    as_blocks = lambda x: (x * ROW_TILES).astype(jnp.int32).reshape(n_blocks, 1, MOE_BLOCK)
    has_rows = counts > 0
    run_exp = jnp.argsort(~has_rows, stable=True).astype(jnp.int32)
    block_run = (jnp.cumsum(has_rows) - 1)[block_exp].astype(jnp.int32)
    n_used_runs = jnp.stack([n_used, jnp.sum(has_rows, dtype=jnp.int32)])
    return (as_blocks(row_tok), as_blocks(row_dst), row_w.reshape(p, 1), block_run, run_exp,
            n_used_runs)


def rwkv_params(rw_mu, rw_w0, rw_w2, rw_a0, rw_a2, rw_g2, rw_k_k, rw_k_a, rw_r_k):
    c = RWKV_WIDTH
    mu = jnp.pad(rw_mu, (0, RWKV_PROJ_PAD - RWKV_PROJ)).reshape(1, RWKV_PROJ_PAD)
    w_lora = jnp.zeros((LANES, 2 * c), F32)
    w_lora = w_lora.at[0:DECAY_LORA, 0:c].set(rw_w2).at[DECAY_LORA:LANES, c:2 * c].set(rw_a2)
    w_gate = jnp.pad(rw_g2, ((0, GATE_PAD - GATE_LORA), (0, 0)))
    vec = lambda x: x.reshape(1, c)
    return (mu, vec(rw_w0), vec(rw_a0), vec(rw_k_k), vec(rw_k_a), vec(rw_r_k),
            w_lora.astype(BF16), w_gate.astype(BF16))


def _token_tiles(m):
    return (1024, 512) if m % 1024 == 0 else (m, m)


def _dense_front(x2d, wts, tm):
    pa = norm_matmul(x2d, wts['ln1_w'], wts['w_att'], tm=tm, tn=512)
    pr = norm_matmul(x2d, wts['ln1_w'], wts['w_rw'], tm=tm, tn=RWKV_PROJ_PAD // 3)
    return pa, pr


def _dense_back(x2d, att2d, rw2d, wts, xattn_fn, tm):
    h1 = matmul_residual([att2d, rw2d], [wts['w_out_a'], wts['w_out_r']], x2d, tm=tm, tn=512)
    qx = norm_matmul(h1, wts['ln2_w'], wts['xq_w'], tm=tm, tn=XATT_WIDTH)
    ox = xattn_fn(qx)
    return matmul_residual([ox], [wts['xo_w']], h1, tm=tm, tn=512)


def kernel(x_prompt, x_sample, cache_win_k, cache_win_v, state_wkv, state_shift, cache_mem_k, cache_mem_v, mem_prompt, ln1_w, w_in, q_norm_w, k_norm_w, attn_sinks, rw_mu, rw_w0, rw_w2, rw_a0, rw_a2, rw_g2, rw_k_k, rw_k_a, rw_r_k, rw_ln_w, rw_ln_b, w_out, ln2_w, mem_norm_w, xq_w, xkv_w, xq_norm_w, xk_norm_w, xo_w, ln3_w, router_group_w, router_group_b, router_expert_w, router_expert_b, exp_w_gate, exp_w_up, exp_w_down):
    assert w_in.shape[0] == 1, "single-layer stack"
    bp, seq, d = x_prompt.shape
    bs = x_sample.shape[0]
    mp = bp * seq
    c = RWKV_WIDTH

    router_w = jnp.concatenate(
        [router_group_w[0], router_expert_w[0],
         jnp.zeros((d, ROUTER_LANES - N_EXPERT_GROUPS - N_EXPERTS), F32)], axis=1)
    router_hi = router_w.astype(BF16)
    wts = {
        'ln1_w': ln1_w[0], 'ln2_w': ln2_w[0], 'ln3_w': ln3_w[0],
        'w_att': w_in[0][:, :ATT_PROJ].astype(BF16),
        'w_rw': jnp.pad(w_in[0][:, ATT_PROJ:],
                        ((0, 0), (0, RWKV_PROJ_PAD - RWKV_PROJ))).astype(BF16),
        'w_out_a': w_out[0][:ATT_WIDTH].astype(BF16),
        'w_out_r': w_out[0][ATT_WIDTH:].astype(BF16),
        'xq_w': xq_w[0].astype(BF16), 'xo_w': xo_w[0].astype(BF16),
        'router_hi': router_hi,
        'router_lo': (router_w - router_hi.astype(F32)).astype(BF16),
        'router_b': jnp.pad(jnp.concatenate([router_group_b[0], router_expert_b[0]]),
                            (0, ROUTER_LANES - N_EXPERT_GROUPS - N_EXPERTS)).reshape(1, -1),
    }
    rw_par = rwkv_params(rw_mu[0], rw_w0[0], rw_w2[0], rw_a0[0], rw_a2[0], rw_g2[0],
                         rw_k_k[0], rw_k_a[0], rw_r_k[0])

    tm_p, te_p = _token_tiles(mp)
    xp = x_prompt.reshape(mp, d)
    pa, pr = _dense_front(xp, wts, tm_p)
    pa3 = pa.reshape(bp, seq, ATT_PROJ)
    pr3 = pr.reshape(bp, seq, RWKV_PROJ_PAD)
    tabs_p = rope_tables(jnp.arange(seq, dtype=jnp.int32))
    att_p, kn_p = swa_prompt(pa3, tabs_p, q_norm_w[0], k_norm_w[0], attn_sinks[0])
    rw_p, wkv_p = rwkv_seq(pr3, jnp.zeros((bp, 1, RWKV_PROJ_PAD), F32), rw_par,
                           rw_ln_w[0], rw_ln_b[0])
    rw_p = rw_p.reshape(mp, c)

    n_mem = mem_prompt.shape[1]
    kv_mem = norm_matmul(mem_prompt.reshape(bp * n_mem, d), mem_norm_w[0],
                         xkv_w[0].astype(BF16), tm=bp * n_mem, tn=512)
    mem_k = head_rms(kv_mem[:, :XATT_WIDTH], xk_norm_w[0])
    mem_v = kv_mem[:, XATT_WIDTH:]
    mem_k3 = mem_k.reshape(bp, n_mem, XATT_WIDTH)
    mem_v3 = mem_v.reshape(bp, n_mem, XATT_WIDTH)

    def xattn_p(qx):
        return xattn_prompt(qx.reshape(bp, seq, XATT_WIDTH), mem_k3, mem_v3,
                            xq_norm_w[0]).reshape(mp, XATT_WIDTH)

    h2_p = _dense_back(xp, att_p.reshape(mp, ATT_WIDTH), rw_p, wts, xattn_p, tm_p)

    tm_s, te_s = _token_tiles(bs)
    xs = x_sample.reshape(bs, d)
    sa, sr = _dense_front(xs, wts, tm_s)
    tabs_s = rope_tables(PAST_LEN + jnp.arange(1, dtype=jnp.int32))
    qk_w = jnp.concatenate([jnp.tile(q_norm_w[0], ATT_HEADS),
                            jnp.tile(k_norm_w[0], ATT_KV_HEADS)]).reshape(1, -1)
    qk = qk_norm_rope(sa[:, :ATT_WIDTH + KV_WIDTH], qk_w, tabs_s)
    nbuf = cache_win_k.shape[2]

    def feature_major(cache):
        return jnp.transpose(cache, (0, 1, 3, 4, 2)).reshape(bs, KV_WIDTH, nbuf)

    def position_major(win):
        return jnp.transpose(win.reshape(1, bs, ATT_KV_HEADS, HEAD_DIM, nbuf), (0, 1, 4, 2, 3))

    att_s, win_k, win_v = swa_decode(
        qk[:, :ATT_WIDTH].reshape(bs, ATT_HEADS, HEAD_DIM),
        qk[:, ATT_WIDTH:].reshape(bs, 1, KV_WIDTH),
        sa[:, ATT_WIDTH + KV_WIDTH:].reshape(bs, 1, KV_WIDTH),
        feature_major(cache_win_k), feature_major(cache_win_v), attn_sinks[0])
    shift_prev = jnp.pad(state_shift[0], ((0, 0), (0, RWKV_PROJ_PAD - RWKV_PROJ)))
    r, ld, kp, v, kk, b, bonus, g = rwkv_prep_tok(sr, shift_prev, rw_par)
    y_s, wkv_s = rwkv_step(r, ld, kp, v, kk, b, jnp.transpose(state_wkv, (0, 2, 3, 4, 1)))
    wkv_s = jnp.transpose(wkv_s, (0, 4, 1, 2, 3))
    rw_s = rwkv_post(y_s, bonus, g, rw_ln_w[0], rw_ln_b[0], tm=te_s, y_channel_major=True)
    def xattn_s(qx):
        q_pad = jnp.pad(qx.reshape(bs, XATT_HEADS, XATT_HEAD_DIM), ((0, 0), (0, 4), (0, 0)))
        rows_of = lambda c: c.reshape(bs, n_mem * XATT_HEADS, XATT_HEAD_DIM)
        o = xattn_decode(q_pad, rows_of(cache_mem_k), rows_of(cache_mem_v), xq_norm_w[0])
        return o[:, :XATT_HEADS].reshape(bs, XATT_WIDTH)

    h2_s = _dense_back(xs, att_s.reshape(bs, ATT_WIDTH), rw_s, wts, xattn_s, tm_s)

    m_all = mp + bs
    slot_stride = m_all
    tc = math.gcd(mp, bs, 512)
    u_all, idx_all, gate_all = moe_router(h2_p, h2_s, wts['ln3_w'], wts['router_hi'],
                                          wts['router_lo'], wts['router_b'], tm=tc)
    row_src, row_dst, row_w, block_run, run_exp, n_used_runs = moe_dispatch(
        idx_all[:, :TOP_K], gate_all[:, :TOP_K], slot_stride)
    y_slots = moe_experts(u_all, row_src, row_dst, row_w, block_run, run_exp, n_used_runs,
                          exp_w_gate[0], exp_w_up[0], exp_w_down[0], TOP_K * slot_stride)
    out_p = moe_combine(h2_p, y_slots, 0, slot_stride, tm=tc)
    out_s = moe_combine(h2_s, y_slots, mp, slot_stride, tm=tc)

    win = min(WINDOW, seq)
    kv_shape = (1, bp, win, ATT_KV_HEADS, HEAD_DIM)
    return (
        out_p.reshape(bp, seq, d),
        out_s.reshape(bs, 1, d),
        kn_p[:, seq - win:].reshape(kv_shape),
        pa3[:, seq - win:, ATT_WIDTH + KV_WIDTH:].reshape(kv_shape),
        wkv_p[None],
        pr3[:, seq - 1, :RWKV_PROJ][None],
        mem_k3.reshape(1, bp, n_mem, XATT_HEADS, XATT_HEAD_DIM),
        mem_v3.reshape(1, bp, n_mem, XATT_HEADS, XATT_HEAD_DIM),
        position_major(win_k),
        position_major(win_v),
        wkv_s,
        sr[:, :RWKV_PROJ].reshape(1, bs, RWKV_PROJ),
    )
```

```python
import functools
import math

import numpy as np
import jax
import jax.numpy as jnp
from jax import lax
from jax.experimental import pallas as pl
from jax.experimental.pallas import tpu as pltpu

F32 = jnp.float32
BF16 = jnp.bfloat16

D_MODEL = 2048
HEAD_DIM = 64
ATT_HEADS = 16
ATT_KV_HEADS = 4
ATT_GROUP = ATT_HEADS // ATT_KV_HEADS
ATT_WIDTH = ATT_HEADS * HEAD_DIM
KV_WIDTH = ATT_KV_HEADS * HEAD_DIM
ATT_PROJ = ATT_WIDTH + 2 * KV_WIDTH
WINDOW = 128
ATT_SCALE = HEAD_DIM ** -0.5
ROPE_THETA = 500000.0
ROT_DIM = HEAD_DIM // 4
PAST_LEN = 16384

RWKV_WIDTH = 1024
RWKV_HEAD_DIM = 64
RWKV_HEADS = 16
DECAY_LORA = 64
AAA_LORA = 64
GATE_LORA = 160
RWKV_PROJ = 3 * RWKV_WIDTH + DECAY_LORA + AAA_LORA + GATE_LORA
RWKV_PROJ_PAD = 3456

N_MEM = 256
XATT_HEADS = 4
XATT_HEAD_DIM = 128
XATT_WIDTH = XATT_HEADS * XATT_HEAD_DIM

N_EXPERT_GROUPS = 8
EXPERTS_PER_GROUP = 8
N_EXPERTS = 64
TOP_K = 2
EXPERT_FF = D_MODEL // 4
MOE_BLOCK = 128

RMS_EPS = 1e-6
GN_EPS = 64e-5

LANES = 128
CHUNK = 64
VMEM_LIMIT = 56 * 1024 * 1024
MOE_VMEM_LIMIT = 60 * 1024 * 1024


def _cp(sem, vmem=VMEM_LIMIT):
    return pltpu.CompilerParams(dimension_semantics=sem, vmem_limit_bytes=vmem)


def _rms_rows(x, w):
    ms = jnp.mean(x * x, axis=-1, keepdims=True)
    return x * lax.rsqrt(ms + RMS_EPS) * w


def _split2(x):
    hi = x.astype(BF16)
    lo = (x - hi.astype(F32)).astype(BF16)
    return hi, lo


def _split3(x):
    h1 = x.astype(BF16)
    r1 = x - h1.astype(F32)
    h2 = r1.astype(BF16)
    h3 = (r1 - h2.astype(F32)).astype(BF16)
    return h1, h2, h3


def _dot(a, b):
    return jnp.dot(a, b, preferred_element_type=F32)


def _dot_nt(a, b):
    return lax.dot_general(a, b, (((1,), (1,)), ((), ())), preferred_element_type=F32)


def _group_sum(x, gmat):
    hi, lo = _split2(x)
    return _dot(hi, gmat) + _dot(lo, gmat)


def _head_indicator():
    r = lax.broadcasted_iota(jnp.int32, (LANES, LANES), 0) // HEAD_DIM
    c = lax.broadcasted_iota(jnp.int32, (LANES, LANES), 1) // HEAD_DIM
    return jnp.where(r == c, 1.0, 0.0).astype(BF16)


def _norm_mm_kernel(x_ref, lnw_ref, w_ref, o_ref, xn_ref, *, w_transposed):
    @pl.when(pl.program_id(1) == 0)
    def _():
        xn_ref[...] = _rms_rows(x_ref[...], lnw_ref[...]).astype(BF16)

    o_ref[...] = (_dot_nt if w_transposed else _dot)(xn_ref[...], w_ref[...])


def norm_matmul(x, ln_w, w_bf16, *, tm, tn, w_transposed=False):
    m, k = x.shape
    n = w_bf16.shape[0 if w_transposed else 1]
    assert m % tm == 0 and n % tn == 0
    w_spec = (pl.BlockSpec((tn, k), lambda i, j: (j, 0)) if w_transposed
              else pl.BlockSpec((k, tn), lambda i, j: (0, j)))
    return pl.pallas_call(
        functools.partial(_norm_mm_kernel, w_transposed=w_transposed),
        grid=(m // tm, n // tn),
        in_specs=[
            pl.BlockSpec((tm, k), lambda i, j: (i, 0)),
            pl.BlockSpec((1, k), lambda i, j: (0, 0)),
            w_spec,
        ],
        out_specs=pl.BlockSpec((tm, tn), lambda i, j: (i, j)),
        out_shape=jax.ShapeDtypeStruct((m, n), F32),
        scratch_shapes=[pltpu.VMEM((tm, k), BF16)],
        compiler_params=_cp(("parallel", "arbitrary")),
        name="norm_matmul",
    )(x, ln_w.reshape(1, k), w_bf16)


def _cast_rows_kernel(w_ref, o_ref, *, n_in):
    @pl.when(pl.program_id(0) < n_in)
    def _():
        o_ref[...] = w_ref[...].astype(BF16)

    @pl.when(pl.program_id(0) >= n_in)
    def _():
        o_ref[...] = jnp.zeros_like(o_ref)


def cast_rows_bf16(w, first_row, n_rows, n_rows_out, *, rows=96):
    k = w.shape[1]
    assert first_row % rows == 0 and n_rows % rows == 0 and n_rows_out % rows == 0
    first, n_in = first_row // rows, n_rows // rows
    return pl.pallas_call(
        functools.partial(_cast_rows_kernel, n_in=n_in),
        grid=(n_rows_out // rows,),
        in_specs=[pl.BlockSpec((rows, k), lambda i: (first + jnp.minimum(i, n_in - 1), 0))],
        out_specs=pl.BlockSpec((rows, k), lambda i: (i, 0)),
        out_shape=jax.ShapeDtypeStruct((n_rows_out, k), BF16),
        compiler_params=_cp(("parallel",)),
        name="cast_rows_bf16",
    )(w)


def _mm_res_kernel(*refs, n_lhs):
    a_refs = refs[:n_lhs]
    w_refs = refs[n_lhs:2 * n_lhs]
    res_ref = refs[2 * n_lhs]
    o_ref = refs[2 * n_lhs + 1]
    acc = res_ref[...]
    for a_ref, w_ref in zip(a_refs, w_refs):
        acc = acc + _dot(a_ref[...].astype(BF16), w_ref[...])
    o_ref[...] = acc


def matmul_residual(lhs_list, w_list, res, *, tm, tn):
    m, n = res.shape
    n_lhs = len(lhs_list)
    assert m % tm == 0 and n % tn == 0
    in_specs = [pl.BlockSpec((tm, a.shape[1]), lambda i, j: (i, 0)) for a in lhs_list]
    in_specs += [pl.BlockSpec((w.shape[0], tn), lambda i, j: (0, j)) for w in w_list]
    in_specs += [pl.BlockSpec((tm, tn), lambda i, j: (i, j))]
    return pl.pallas_call(
        functools.partial(_mm_res_kernel, n_lhs=n_lhs),
        grid=(m // tm, n // tn),
        in_specs=in_specs,
        out_specs=pl.BlockSpec((tm, tn), lambda i, j: (i, j)),
        out_shape=jax.ShapeDtypeStruct((m, n), F32),
        compiler_params=_cp(("parallel", "arbitrary")),
        name="matmul_residual",
    )(*lhs_list, *w_list, res)


def rope_tables(pos):
    half = ROT_DIM // 2
    f32 = np.float32
    inv = f32(ROPE_THETA) ** (-np.arange(half, dtype=f32) * f32(2.0) / f32(ROT_DIM))
    ang = (np.asarray(pos, f32)[:, None] * inv[None, :].astype(f32)).astype(f32)
    cos, sin = np.cos(ang.astype(np.float64)).astype(f32), np.sin(ang.astype(np.float64)).astype(f32)
    t = ang.shape[0]
    ones = np.ones((t, HEAD_DIM - ROT_DIM), f32)
    zeros = np.zeros((t, HEAD_DIM - ROT_DIM), f32)
    z8 = np.zeros((t, half), f32)
    cos_t = np.concatenate([cos, cos, ones], axis=1)
    sin_a = np.concatenate([z8, sin, zeros], axis=1)
    sin_b = np.concatenate([-sin, z8, zeros], axis=1)
    return tuple(jnp.asarray(np.concatenate([a, a], axis=1)) for a in (cos_t, sin_a, sin_b))


def _norm_rope_chunk(x, w, cos_t, sin_a, sin_b, gmat):
    ms = _group_sum(x * x, gmat) * (1.0 / HEAD_DIM)
    xn = x * lax.rsqrt(ms + RMS_EPS) * w
    half = ROT_DIM // 2
    return (xn * cos_t + pltpu.roll(xn, half, axis=1) * sin_a
            + pltpu.roll(xn, LANES - half, axis=1) * sin_b)


def _norm_rope(x, w, tabs, gmat):
    chunks = [
        _norm_rope_chunk(x[:, c * LANES:(c + 1) * LANES], w, *tabs, gmat)
        for c in range(x.shape[1] // LANES)
    ]
    return chunks[0] if len(chunks) == 1 else jnp.concatenate(chunks, axis=1)


def _sink_softmax(s, sink):
    m = jnp.maximum(jnp.max(s, axis=-1, keepdims=True), sink)
    e = jnp.exp(s - m)
    return e / (jnp.sum(e, axis=-1, keepdims=True) + jnp.exp(sink - m))


def _swa_prompt_kernel(q_ref, kc_ref, vc_ref, kp_ref, vp_ref, cc_ref, sac_ref, sbc_ref,
                       cp_ref, sap_ref, sbp_ref, qw_ref, kw_ref, sink_ref, o_ref, kn_ref):
    n = pl.program_id(1)
    blk = q_ref.shape[0]
    gmat = _head_indicator()
    tabs_c = (cc_ref[...], sac_ref[...], sbc_ref[...])
    tabs_p = (cp_ref[...], sap_ref[...], sbp_ref[...])
    q = _norm_rope(q_ref[...], qw_ref[...], tabs_c, gmat)
    k_cur = _norm_rope(kc_ref[...], kw_ref[...], tabs_c, gmat)
    k_prev = _norm_rope(kp_ref[...], kw_ref[...], tabs_p, gmat)
    kn_ref[...] = k_cur
    k_all = jnp.concatenate([k_prev, k_cur], axis=0).astype(BF16)
    v_all = jnp.concatenate([vp_ref[...], vc_ref[...]], axis=0).astype(BF16)

    qi = lax.broadcasted_iota(jnp.int32, (blk, 2 * blk), 0) + blk
    si = lax.broadcasted_iota(jnp.int32, (blk, 2 * blk), 1)
    rel = qi - si
    valid = (rel >= 0) & (rel <= WINDOW) & ((n > 0) | (si >= blk))

    groups = range(ATT_KV_HEADS)
    lanes = [slice(kv * HEAD_DIM, (kv + 1) * HEAD_DIM) for kv in groups]
    heads = [[kv * ATT_GROUP + g for g in range(ATT_GROUP)] for kv in groups]
    q_g = [jnp.concatenate([q[:, h * HEAD_DIM:(h + 1) * HEAD_DIM] for h in heads[kv]],
                           axis=0).astype(BF16) for kv in groups]
    s = [_dot_nt(q_g[kv], k_all[:, lanes[kv]]) * ATT_SCALE for kv in groups]
    p = [jnp.concatenate(
        [_sink_softmax(jnp.where(valid, s[kv][g * blk:(g + 1) * blk], -jnp.inf), sink_ref[h])
         for g, h in enumerate(heads[kv])], axis=0).astype(BF16) for kv in groups]
    o = [_dot(p[kv], v_all[:, lanes[kv]]) for kv in groups]
    for kv in groups:
        for g, h in enumerate(heads[kv]):
            o_ref[:, h * HEAD_DIM:(h + 1) * HEAD_DIM] = o[kv][g * blk:(g + 1) * blk]


def swa_prompt(pa, tabs, q_norm_w, k_norm_w, sinks):
    b, t, _ = pa.shape
    blk = WINDOW
    nb = t // blk
    qb, kb, vb = 0, ATT_WIDTH // KV_WIDTH, ATT_WIDTH // KV_WIDTH + 1
    cur = lambda i, n, *_: (i, n, 0)
    tab_cur = pl.BlockSpec((blk, LANES), lambda i, n: (n, 0))
    tab_prev = pl.BlockSpec((blk, LANES), lambda i, n: (jnp.maximum(n - 1, 0), 0))
    qw = jnp.tile(q_norm_w.reshape(1, HEAD_DIM), (1, 2))
    kw = jnp.tile(k_norm_w.reshape(1, HEAD_DIM), (1, 2))
    return pl.pallas_call(
        _swa_prompt_kernel,
        grid=(b, nb),
        in_specs=[
            pl.BlockSpec((None, blk, ATT_WIDTH), lambda i, n: (i, n, qb)),
            pl.BlockSpec((None, blk, KV_WIDTH), lambda i, n: (i, n, kb)),
            pl.BlockSpec((None, blk, KV_WIDTH), lambda i, n: (i, n, vb)),
            pl.BlockSpec((None, blk, KV_WIDTH), lambda i, n: (i, jnp.maximum(n - 1, 0), kb)),
            pl.BlockSpec((None, blk, KV_WIDTH), lambda i, n: (i, jnp.maximum(n - 1, 0), vb)),
            tab_cur, tab_cur, tab_cur, tab_prev, tab_prev, tab_prev,
            pl.BlockSpec((1, LANES), lambda i, n: (0, 0)),
            pl.BlockSpec((1, LANES), lambda i, n: (0, 0)),
            pl.BlockSpec(memory_space=pltpu.SMEM),
        ],
        out_specs=[
            pl.BlockSpec((None, blk, ATT_WIDTH), cur),
            pl.BlockSpec((None, blk, KV_WIDTH), cur),
        ],
        out_shape=[
            jax.ShapeDtypeStruct((b, t, ATT_WIDTH), F32),
            jax.ShapeDtypeStruct((b, t, KV_WIDTH), F32),
        ],
        compiler_params=_cp(("parallel", "arbitrary")),
        name="swa_prompt",
    )(pa, pa, pa, pa, pa, *tabs, *tabs, qw, kw, sinks)


def _qk_norm_rope_kernel(x_ref, w_ref, c_ref, sa_ref, sb_ref, o_ref):
    gmat = _head_indicator()
    tabs = (c_ref[...], sa_ref[...], sb_ref[...])
    for c in range(x_ref.shape[1] // LANES):
        sl = slice(c * LANES, (c + 1) * LANES)
        o_ref[:, sl] = _norm_rope_chunk(x_ref[:, sl], w_ref[:, sl], *tabs, gmat)


def qk_norm_rope(x, w_row, tabs):
    m, w = x.shape
    full = lambda *shape: pl.BlockSpec(shape, lambda: (0,) * len(shape))
    return pl.pallas_call(
        _qk_norm_rope_kernel,
        in_specs=[full(m, w), full(1, w), full(1, LANES), full(1, LANES), full(1, LANES)],
        out_specs=full(m, w),
        out_shape=jax.ShapeDtypeStruct((m, w), F32),
        name="qk_norm_rope",
    )(x, w_row, *tabs)


def _swa_decode_kernel(q_ref, kn_ref, vn_ref, knt_ref, vnt_ref, ck_ref, cv_ref, sink_ref, o_ref,
                       kw_ref, vw_ref):
    bb = q_ref.shape[0]
    nbuf = ck_ref.shape[2]
    row_kv = lax.broadcasted_iota(jnp.int32, (ATT_HEADS, KV_WIDTH), 0) // ATT_GROUP
    lane_kv = lax.broadcasted_iota(jnp.int32, (ATT_HEADS, KV_WIDTH), 1) // HEAD_DIM
    own = row_kv == lane_kv
    sink = sink_ref[...]
    seq = lax.broadcasted_iota(jnp.int32, knt_ref.shape, 1)
    pos = lax.broadcasted_iota(jnp.int32, (KV_WIDTH, nbuf), 1)
    b0 = pl.program_id(0) * bb
    for b in range(bb):
        q2 = q_ref[b]
        q_exp = jnp.where(own, jnp.concatenate([q2] * ATT_KV_HEADS, axis=1), 0.0)
        k_new, v_new = kn_ref[b], vn_ref[b]
        k_buf, v_buf = ck_ref[b], cv_ref[b]
        s_buf = _dot(q_exp.astype(BF16), k_buf.astype(BF16)) * ATT_SCALE
        s_new = jnp.sum(q_exp * k_new, axis=-1, keepdims=True) * ATT_SCALE
        m = jnp.maximum(jnp.maximum(jnp.max(s_buf, axis=-1, keepdims=True), s_new), sink)
        e_buf = jnp.exp(s_buf - m)
        e_new = jnp.exp(s_new - m)
        inv = 1.0 / (jnp.sum(e_buf, axis=-1, keepdims=True) + e_new + jnp.exp(sink - m))
        o = _dot_nt((e_buf * inv).astype(BF16), v_buf.astype(BF16)) + (e_new * inv) * v_new
        o = jnp.where(own, o, 0.0)
        o_ref[b] = (o[:, 0:HEAD_DIM] + o[:, HEAD_DIM:2 * HEAD_DIM]
                    + o[:, 2 * HEAD_DIM:3 * HEAD_DIM] + o[:, 3 * HEAD_DIM:4 * HEAD_DIM])
        k_col = jnp.sum(jnp.where(seq == b0 + b, knt_ref[...], 0.0), axis=1, keepdims=True)
        v_col = jnp.sum(jnp.where(seq == b0 + b, vnt_ref[...], 0.0), axis=1, keepdims=True)
        kw_ref[b] = jnp.where(pos == nbuf - 1, k_col, pltpu.roll(k_buf, nbuf - 1, axis=1))
        vw_ref[b] = jnp.where(pos == nbuf - 1, v_col, pltpu.roll(v_buf, nbuf - 1, axis=1))


def swa_decode(q, k_new, v_new, cache_kt, cache_vt, sinks, *, bb=8):
    b, _, nbuf = cache_kt.shape
    blk3 = lambda s1, s2: pl.BlockSpec((bb, s1, s2), lambda i: (i, 0, 0))
    whole = pl.BlockSpec((KV_WIDTH, b), lambda i: (0, 0))
    return pl.pallas_call(
        _swa_decode_kernel,
        grid=(b // bb,),
        in_specs=[
            blk3(ATT_HEADS, HEAD_DIM), blk3(1, KV_WIDTH), blk3(1, KV_WIDTH), whole, whole,
            blk3(KV_WIDTH, nbuf), blk3(KV_WIDTH, nbuf),
            pl.BlockSpec((ATT_HEADS, 1), lambda i: (0, 0)),
        ],
        out_specs=[blk3(ATT_HEADS, HEAD_DIM), blk3(KV_WIDTH, nbuf), blk3(KV_WIDTH, nbuf)],
        out_shape=[
            jax.ShapeDtypeStruct((b, ATT_HEADS, HEAD_DIM), F32),
            jax.ShapeDtypeStruct((b, KV_WIDTH, nbuf), F32),
            jax.ShapeDtypeStruct((b, KV_WIDTH, nbuf), F32),
        ],
        compiler_params=_cp(("parallel",)),
        name="swa_decode",
    )(q, k_new, v_new, k_new.reshape(b, KV_WIDTH).T, v_new.reshape(b, KV_WIDTH).T,
      cache_kt, cache_vt, sinks.reshape(ATT_HEADS, 1))


def _head_rms_kernel(x_ref, w_ref, o_ref):
    for h in range(x_ref.shape[1] // XATT_HEAD_DIM):
        sl = slice(h * XATT_HEAD_DIM, (h + 1) * XATT_HEAD_DIM)
        o_ref[:, sl] = _rms_rows(x_ref[:, sl], w_ref[...])


def head_rms(x, w):
    m, wd = x.shape
    return pl.pallas_call(
        _head_rms_kernel,
        in_specs=[pl.BlockSpec((m, wd), lambda: (0, 0)),
                  pl.BlockSpec((1, XATT_HEAD_DIM), lambda: (0, 0))],
        out_specs=pl.BlockSpec((m, wd), lambda: (0, 0)),
        out_shape=jax.ShapeDtypeStruct((m, wd), F32),
        name="head_rms",
    )(x, w.reshape(1, XATT_HEAD_DIM))


def _xattn_prompt_kernel(q_ref, k_ref, v_ref, w_ref, o_ref):
    scale = 1.0 / math.sqrt(XATT_HEAD_DIM)
    for h in range(XATT_HEADS):
        sl = slice(h * XATT_HEAD_DIM, (h + 1) * XATT_HEAD_DIM)
        qn = _rms_rows(q_ref[:, sl], w_ref[...]).astype(BF16)
        s = _dot_nt(qn, k_ref[:, sl].astype(BF16)) * scale
        e = jnp.exp(s - jnp.max(s, axis=-1, keepdims=True))
        p = e / jnp.sum(e, axis=-1, keepdims=True)
        o_ref[:, sl] = _dot(p.astype(BF16), v_ref[:, sl].astype(BF16))


def xattn_prompt(q, mem_k, mem_v, xq_norm_w, *, tq=512):
    b, t, w = q.shape
    n_mem = mem_k.shape[1]
    return pl.pallas_call(
        _xattn_prompt_kernel,
        grid=(b, t // tq),
        in_specs=[
            pl.BlockSpec((None, tq, w), lambda i, j: (i, j, 0)),
            pl.BlockSpec((None, n_mem, w), lambda i, j: (i, 0, 0)),
            pl.BlockSpec((None, n_mem, w), lambda i, j: (i, 0, 0)),
            pl.BlockSpec((1, XATT_HEAD_DIM), lambda i, j: (0, 0)),
        ],
        out_specs=pl.BlockSpec((None, tq, w), lambda i, j: (i, j, 0)),
        out_shape=jax.ShapeDtypeStruct((b, t, w), F32),
        compiler_params=_cp(("parallel", "arbitrary")),
        name="xattn_prompt",
    )(q, mem_k, mem_v, xq_norm_w.reshape(1, XATT_HEAD_DIM))


def _xattn_decode_kernel(q_ref, k_ref, v_ref, w_ref, o_ref):
    bb, rows, _ = q_ref.shape
    n_keys = k_ref.shape[1]
    scale = 1.0 / math.sqrt(XATT_HEAD_DIM)
    own = (lax.broadcasted_iota(jnp.int32, (rows, n_keys), 1) % XATT_HEADS
           == lax.broadcasted_iota(jnp.int32, (rows, n_keys), 0) % XATT_HEADS)
    seqs = range(bb)
    qn = [_rms_rows(q_ref[b], w_ref[...]).astype(BF16) for b in seqs]
    s = [jnp.where(own, _dot_nt(qn[b], k_ref[b].astype(BF16)) * scale, -jnp.inf) for b in seqs]
    e = [jnp.exp(s[b] - jnp.max(s[b], axis=-1, keepdims=True)) for b in seqs]
    p = [(e[b] / jnp.sum(e[b], axis=-1, keepdims=True)).astype(BF16) for b in seqs]
    for b in seqs:
        o_ref[b] = _dot(p[b], v_ref[b].astype(BF16))


def xattn_decode(q_pad, mem_k, mem_v, xq_norm_w, *, bb=8):
    b, rows, _ = q_pad.shape
    n_keys = mem_k.shape[1]
    kv = pl.BlockSpec((bb, n_keys, XATT_HEAD_DIM), lambda i: (i, 0, 0))
    return pl.pallas_call(
        _xattn_decode_kernel,
        grid=(b // bb,),
        in_specs=[pl.BlockSpec((bb, rows, XATT_HEAD_DIM), lambda i: (i, 0, 0)), kv, kv,
                  pl.BlockSpec((1, XATT_HEAD_DIM), lambda i: (0, 0))],
        out_specs=pl.BlockSpec((bb, rows, XATT_HEAD_DIM), lambda i: (i, 0, 0)),
        out_shape=jax.ShapeDtypeStruct((b, rows, XATT_HEAD_DIM), F32),
        compiler_params=_cp(("parallel",)),
        name="xattn_decode",
    )(q_pad, mem_k, mem_v, xq_norm_w.reshape(1, XATT_HEAD_DIM))


LORA_OFF = 3 * RWKV_WIDTH
GATE_OFF = LORA_OFF + DECAY_LORA + AAA_LORA
GATE_PAD = RWKV_PROJ_PAD - GATE_OFF


def _sigmoid(x):
    return 1.0 / (1.0 + jnp.exp(-x))


def _per_chunk(fn, *arrays):
    w = arrays[0].shape[1]
    outs = [fn(*(a[:, c * LANES:(c + 1) * LANES] for a in arrays)) for c in range(w // LANES)]
    return jnp.concatenate(outs, axis=1)


def _rwkv_prep_core(pr, prev, mu, w0, a0, kk_w, ka_w, rk_w, w_lora, w_gate):
    c = RWKV_WIDTH
    gmat = _head_indicator()
    xm = pr + (prev - pr) * mu
    r, k, v = xm[:, 0:c], xm[:, c:2 * c], xm[:, 2 * c:3 * c]
    lora = xm[:, LORA_OFF:LORA_OFF + LANES]
    lane = lax.broadcasted_iota(jnp.int32, lora.shape, 1)
    lora_in = jnp.where(lane < DECAY_LORA, jnp.tanh(lora), lora)
    wa = _dot(lora_in.astype(BF16), w_lora)
    z = -(w0 + wa[:, 0:c])
    softplus = jnp.maximum(z, 0.0) + jnp.log(1.0 + jnp.exp(-jnp.abs(z)))
    log_decay = -jnp.exp(-softplus - 0.5)
    a = _sigmoid(a0 + wa[:, c:2 * c])
    g = _dot(_sigmoid(xm[:, GATE_OFF:GATE_OFF + GATE_PAD]).astype(BF16), w_gate)
    kk = k * kk_w
    norm = jnp.sqrt(_per_chunk(lambda t: _group_sum(t * t, gmat), kk))
    kk = kk / jnp.maximum(norm, 1e-12)
    kp = k * (1.0 + (a - 1.0) * ka_w)
    bonus = _per_chunk(lambda t: _group_sum(t, gmat), r * kp * rk_w) * v
    return r, log_decay, kp, v, kk, kk * a, bonus, g


def _rwkv_prep_tok_kernel(pr_ref, prev_ref, mu_ref, w0_ref, a0_ref, kkw_ref, kaw_ref, rkw_ref,
                          wl_ref, wg_ref, *out_refs):
    outs = _rwkv_prep_core(pr_ref[...], prev_ref[...], mu_ref[...], w0_ref[...], a0_ref[...],
                           kkw_ref[...], kaw_ref[...], rkw_ref[...], wl_ref[...], wg_ref[...])
    for k, (o_ref, o) in enumerate(zip(out_refs, outs)):
        o_ref[...] = o.T if k < N_STEP_VECS else o


def _rwkv_param_specs(index_map):
    c = RWKV_WIDTH
    shapes = [(1, RWKV_PROJ_PAD)] + [(1, c)] * 5 + [(LANES, 2 * c), (GATE_PAD, c)]
    return [pl.BlockSpec(s, index_map) for s in shapes]


N_STEP_VECS = 6


def rwkv_prep_tok(pr, prev, params):
    m, wd = pr.shape
    c = RWKV_WIDTH
    shapes = [(c, m)] * N_STEP_VECS + [(m, c)] * 2
    return pl.pallas_call(
        _rwkv_prep_tok_kernel,
        grid=(1,),
        in_specs=[pl.BlockSpec((m, wd), lambda i: (0, 0))] * 2
        + _rwkv_param_specs(lambda i: (0, 0)),
        out_specs=[pl.BlockSpec(s, lambda i: (0, 0)) for s in shapes],
        out_shape=[jax.ShapeDtypeStruct(s, F32) for s in shapes],
        compiler_params=_cp(("arbitrary",)),
        name="rwkv_prep_tok",
    )(pr, prev, *params)


def _dot_tn(a, b):
    return lax.dot_general(a, b, (((0,), (0,)), ((), ())), preferred_element_type=F32)


def _rwkv_scan_tile(r_all, ld_all, kp_all, v_all, kk_all, b_all, s_cur, y_ref):
    n = CHUNK
    nc = r_all.shape[0] // n
    ti = lax.broadcasted_iota(jnp.int32, (n, n), 0)
    si = lax.broadcasted_iota(jnp.int32, (n, n), 1)
    tri = jnp.where(si <= ti, 1.0, 0.0).astype(BF16)
    t2 = lax.broadcasted_iota(jnp.int32, (n, LANES), 0)
    lane2 = lax.broadcasted_iota(jnp.int32, (n, LANES), 1)
    s2 = lane2 % n
    low = lane2 < n
    strict2, incl2, eye2 = s2 < t2, s2 <= t2, s2 == t2
    low4 = lax.broadcasted_iota(jnp.int32, (2 * n, LANES), 1) < n
    top4 = lax.broadcasted_iota(jnp.int32, (2 * n, LANES), 0) < n
    diag_blk = top4 == low4
    own4 = {0: low4, 1: ~low4}
    pairs = range(RWKV_HEADS // 2)
    cs = [slice(c * LANES, (c + 1) * LANES) for c in pairs]

    pre = []
    for j in range(nc):
        rows = slice(j * n, (j + 1) * n)
        ld = ld_all[rows, :]
        l1, l2, l3 = _split3(ld)
        lc = _dot(tri, l1) + _dot(tri, l2) + _dot(tri, l3)
        lc_end = lc[n - 1:n, :]
        e_neg = jnp.exp(-lc)
        kk, b, kp = kk_all[rows, :], b_all[rows, :], kp_all[rows, :]
        to_end = jnp.exp(lc_end - lc)
        pre.append(dict(
            a_t=(-kk * jnp.exp(lc - ld)).astype(BF16), b_t=(b * e_neg).astype(BF16),
            k_t=(kp * e_neg).astype(BF16), r_t=(r_all[rows, :] * jnp.exp(lc)).astype(BF16),
            b_e=(b * to_end).astype(BF16), k_e=(kp * to_end).astype(BF16),
            v_b=v_all[rows, :].astype(BF16), g_end=jnp.exp(lc_end)))
    items = [(j, c, par) for j in range(nc) for c in pairs for par in (0, 1)]
    at = {it: i for i, it in enumerate(items)}
    ar = {(j, c): jnp.concatenate([pre[j]['a_t'][:, cs[c]], pre[j]['r_t'][:, cs[c]]], axis=0)
          for j in range(nc) for c in pairs}
    kb = {(j, c): jnp.concatenate([pre[j]['k_t'][:, cs[c]], pre[j]['b_t'][:, cs[c]]], axis=0)
          for j in range(nc) for c in pairs}
    gm = [_dot_nt(jnp.where(own4[par], ar[j, c], 0).astype(BF16), kb[j, c]) for j, c, par in items]
    top = [g[0:n] for g in gm]
    pm = [jnp.where(incl2, g[n:2 * n], 0.0).astype(BF16) for g in gm]
    lak = [jnp.where(strict2 & low, t, 0.0).astype(BF16) for t in top]
    lv = [_dot(lak[i][:, 0:n], pre[j]['v_b'][:, cs[c]]) for i, (j, c, par) in enumerate(items)]
    z = [jnp.where(low, jnp.where(eye2, 1.0, 0.0), jnp.where(strict2, t, 0.0)) for t in top]
    for _ in range(6):
        zb = [zz.astype(BF16) for zz in z]
        res = [_dot(jnp.where(low, 0, zb[i]).astype(BF16), jnp.concatenate([zb[i], zb[i]], axis=0))
               for i in range(len(items))]
        z = [res[i] + jnp.where(low, z[i], 0.0) for i in range(len(items))]
    tmat = [zz[:, 0:n].astype(BF16) for zz in z]

    for j in range(nc):
        p = pre[j]
        sw = [_dot_nt(ar[j, c], s_cur[c].astype(BF16)) for c in pairs]
        w0 = [(sw[c][0:n] + jnp.where(low, lv[at[j, c, 0]], lv[at[j, c, 1]])).astype(BF16)
              for c in pairs]
        u = [jnp.where(low, _dot(tmat[at[j, c, 0]], w0[c]),
                       _dot(tmat[at[j, c, 1]], w0[c])).astype(BF16) for c in pairs]
        vu = [jnp.concatenate([p['v_b'][:, cs[c]], u[c]], axis=0) for c in pairs]
        yb = [jnp.where(low, _dot(pm[at[j, c, 0]], vu[c]), _dot(pm[at[j, c, 1]], vu[c]))
              for c in pairs]
        for c in pairs:
            y_ref[j * n:(j + 1) * n, cs[c]] = sw[c][n:2 * n] + yb[c]
        upd = [_dot_tn(jnp.concatenate([u[c], p['v_b'][:, cs[c]]], axis=0),
                       jnp.concatenate([p['b_e'][:, cs[c]], p['k_e'][:, cs[c]]], axis=0))
               for c in pairs]
        s_cur = [s_cur[c] * p['g_end'][:, cs[c]] + jnp.where(diag_blk, upd[c], 0.0) for c in pairs]
    return s_cur


def _group_norm_gate(y, bonus, g, ln_w, ln_b, gmat):
    inv = 1.0 / RWKV_HEAD_DIM
    d = y - _group_sum(y, gmat) * inv
    var = _group_sum(d * d, gmat) * inv
    return (d * lax.rsqrt(var + GN_EPS) * ln_w + ln_b + bonus) * g


def _rwkv_seq_kernel(pr_ref, prev0_ref, mu_ref, w0_ref, a0_ref, kkw_ref, kaw_ref, rkw_ref,
                     wl_ref, wg_ref, lnw_ref, lnb_ref, o_ref, s_out_ref, s_ref, last_ref, y_ref):
    @pl.when(pl.program_id(1) == 0)
    def _():
        s_ref[...] = jnp.zeros_like(s_ref)
        last_ref[...] = prev0_ref[...]

    pr = pr_ref[...]
    rows = pr.shape[0]
    row = lax.broadcasted_iota(jnp.int32, (rows, 1), 0)
    prev = jnp.where(row == 0, last_ref[...], pltpu.roll(pr, 1, axis=0))
    last_ref[...] = pr[rows - 1:rows, :]
    r, ld, kp, v, kk, b, bonus, g = _rwkv_prep_core(
        pr, prev, mu_ref[...], w0_ref[...], a0_ref[...], kkw_ref[...], kaw_ref[...],
        rkw_ref[...], wl_ref[...], wg_ref[...])
    pairs = range(RWKV_HEADS // 2)
    s_new = _rwkv_scan_tile(r, ld, kp, v, kk, b, [s_ref[c] for c in pairs], y_ref)
    n = RWKV_HEAD_DIM
    for c in pairs:
        s_ref[c] = s_new[c]
        s_out_ref[2 * c] = s_new[c][0:n, 0:n]
        s_out_ref[2 * c + 1] = s_new[c][n:2 * n, n:2 * n]
    gmat = _head_indicator()
    for c in pairs:
        sl = slice(c * LANES, (c + 1) * LANES)
        o_ref[:, sl] = _group_norm_gate(y_ref[:, sl], bonus[:, sl], g[:, sl], lnw_ref[:, sl],
                                        lnb_ref[:, sl], gmat)


SCAN_CHUNKS = 4


def rwkv_seq(pr, prev0, params, ln_w, ln_b):
    bsz, t, wd = pr.shape
    c = RWKV_WIDTH
    rows = SCAN_CHUNKS * CHUNK
    assert t % rows == 0
    vec = pl.BlockSpec((1, c), lambda i, j: (0, 0))
    st = pl.BlockSpec((None, RWKV_HEADS, RWKV_HEAD_DIM, RWKV_HEAD_DIM), lambda i, j: (i, 0, 0, 0))
    return pl.pallas_call(
        _rwkv_seq_kernel,
        grid=(bsz, t // rows),
        in_specs=[pl.BlockSpec((None, rows, wd), lambda i, j: (i, j, 0)),
                  pl.BlockSpec((None, 1, wd), lambda i, j: (i, 0, 0))]
        + _rwkv_param_specs(lambda i, j: (0, 0)) + [vec, vec],
        out_specs=[pl.BlockSpec((None, rows, c), lambda i, j: (i, j, 0)), st],
        out_shape=[jax.ShapeDtypeStruct((bsz, t, c), F32),
                   jax.ShapeDtypeStruct((bsz, RWKV_HEADS, RWKV_HEAD_DIM, RWKV_HEAD_DIM), F32)],
        scratch_shapes=[pltpu.VMEM((RWKV_HEADS // 2, LANES, LANES), F32),
                        pltpu.VMEM((1, wd), F32), pltpu.VMEM((rows, c), F32)],
        compiler_params=_cp(("parallel", "arbitrary")),
        name="rwkv_seq",
    )(pr, prev0, *params, ln_w.reshape(1, c), ln_b.reshape(1, c))


STEP_UNROLL = 8


def _rwkv_step_kernel(r_ref, ld_ref, kp_ref, v_ref, kk_ref, b_ref, s_ref, y_ref, s_out_ref):
    n = RWKV_HEAD_DIM
    neg_kk, decay = -kk_ref[...], jnp.exp(ld_ref[...])
    b_mat, kp_mat, r_mat = b_ref[...], kp_ref[...], r_ref[...]

    def body(i, carry):
        v0 = pl.multiple_of(i * STEP_UNROLL, STEP_UNROLL)
        v_rows = v_ref[pl.ds(v0, STEP_UNROLL), :]
        rows = range(STEP_UNROLL)
        s = [s_ref[v0 + j] for j in rows]
        sa = [jnp.sum(s[j] * neg_kk, axis=0, keepdims=True) for j in rows]
        s_new = [s[j] * decay + sa[j] * b_mat + v_rows[j:j + 1, :] * kp_mat for j in rows]
        y = [jnp.sum(s_new[j] * r_mat, axis=0, keepdims=True) for j in rows]
        for j in rows:
            s_out_ref[v0 + j] = s_new[j]
        y_ref[pl.ds(v0, STEP_UNROLL), :] = jnp.concatenate(y, axis=0)
        return carry

    lax.fori_loop(0, n // STEP_UNROLL, body, 0)


def rwkv_step(r, ld, kp, v, kk, b, state_t):
    _, nh, n, _, bsz = state_t.shape
    vec = pl.BlockSpec((n, bsz), lambda h: (h, 0))
    st = pl.BlockSpec((None, None, n, n, bsz), lambda h: (0, h, 0, 0, 0))
    return pl.pallas_call(
        _rwkv_step_kernel,
        grid=(nh,),
        in_specs=[vec] * 6 + [st],
        out_specs=[vec, st],
        out_shape=[jax.ShapeDtypeStruct((nh * n, bsz), F32),
                   jax.ShapeDtypeStruct(state_t.shape, F32)],
        compiler_params=_cp(("parallel",)),
        name="rwkv_step",
    )(r, ld, kp, v, kk, b, state_t)


def _rwkv_post_kernel(y_ref, bonus_ref, g_ref, lnw_ref, lnb_ref, o_ref, *, y_channel_major):
    gmat = _head_indicator()
    for c in range(o_ref.shape[1] // LANES):
        sl = slice(c * LANES, (c + 1) * LANES)
        y = y_ref[sl, :].T if y_channel_major else y_ref[:, sl]
        o_ref[:, sl] = _group_norm_gate(y, bonus_ref[:, sl], g_ref[:, sl], lnw_ref[:, sl],
                                        lnb_ref[:, sl], gmat)


def rwkv_post(y, bonus, g, ln_w, ln_b, *, tm, y_channel_major=False):
    m, c = bonus.shape
    blk = pl.BlockSpec((tm, c), lambda i: (i, 0))
    y_blk = pl.BlockSpec((c, tm), lambda i: (0, i)) if y_channel_major else blk
    vec = pl.BlockSpec((1, c), lambda i: (0, 0))
    return pl.pallas_call(
        functools.partial(_rwkv_post_kernel, y_channel_major=y_channel_major),
        grid=(m // tm,),
        in_specs=[y_blk, blk, blk, vec, vec],
        out_specs=blk,
        out_shape=jax.ShapeDtypeStruct((m, c), F32),
        compiler_params=_cp(("parallel",)),
        name="rwkv_post",
    )(y, bonus, g, ln_w.reshape(1, c), ln_b.reshape(1, c))


ROUTER_LANES = LANES
ROW_TILES = D_MODEL // LANES


def _rows_to_tiles(ref, x):
    rows = x.shape[0]
    for j in range(ROW_TILES):
        ref[pl.ds(j, rows, stride=ROW_TILES), :] = x[:, j * LANES:(j + 1) * LANES]


def _tiles_to_rows(ref, rows):
    return jnp.concatenate(
        [ref[pl.ds(j, rows, stride=ROW_TILES), :] for j in range(ROW_TILES)], axis=1)


def _router_kernel(ha_ref, hb_ref, lnw_ref, whi_ref, wlo_ref, bias_ref, u_ref, idx_ref, gate_ref,
                   *, steps_a):
    use_a = pl.program_id(0) < steps_a
    h = jnp.where(use_a, ha_ref[...], hb_ref[...])
    u = _rms_rows(h, lnw_ref[...])
    _rows_to_tiles(u_ref, u)
    u_hi, u_lo = _split2(u)
    w_hi = whi_ref[...]
    logits = _dot(u_hi, w_hi) + _dot(u_lo, w_hi) + _dot(u_hi, wlo_ref[...]) + bias_ref[...]
    lane = lax.broadcasted_iota(jnp.int32, logits.shape, 1)
    neg = -jnp.inf

    def first_max(x):
        m = jnp.max(x, axis=1, keepdims=True)
        return m, jnp.min(jnp.where(x == m, lane, ROUTER_LANES), axis=1, keepdims=True)

    gl = jnp.where(lane < N_EXPERT_GROUPS, logits, neg)
    g_max, g_idx = first_max(gl)
    g_gate = 1.0 / jnp.sum(jnp.exp(gl - g_max), axis=1, keepdims=True)
    lo = N_EXPERT_GROUPS + g_idx * EXPERTS_PER_GROUP
    el = jnp.where((lane >= lo) & (lane < lo + EXPERTS_PER_GROUP), logits, neg)
    v1, i1 = first_max(el)
    v2, i2 = first_max(jnp.where(lane == i1, neg, el))
    e2 = jnp.exp(v2 - v1)
    w1 = g_gate / (1.0 + e2)
    w2 = g_gate * e2 / (1.0 + e2)
    idx_ref[...] = jnp.where(lane == 0, i1 - N_EXPERT_GROUPS,
                             jnp.where(lane == 1, i2 - N_EXPERT_GROUPS, 0))
    gate_ref[...] = jnp.where(lane == 0, w1, jnp.where(lane == 1, w2, 0.0))


def moe_router(h_a, h_b, ln_w, w_hi, w_lo, bias, *, tm):
    (ma, d), mb = h_a.shape, h_b.shape[0]
    assert ma % tm == 0 and mb % tm == 0
    steps_a, steps_b = ma // tm, mb // tm
    m = ma + mb
    const = lambda r, w: pl.BlockSpec((r, w), lambda i: (0, 0))
    row = lambda w: pl.BlockSpec((tm, w), lambda i: (i, 0))
    return pl.pallas_call(
        functools.partial(_router_kernel, steps_a=steps_a),
        grid=(steps_a + steps_b,),
        in_specs=[pl.BlockSpec((tm, d), lambda i: (jnp.minimum(i, steps_a - 1), 0)),
                  pl.BlockSpec((tm, d), lambda i: (jnp.maximum(i - steps_a, 0), 0)),
                  const(1, d), const(d, ROUTER_LANES), const(d, ROUTER_LANES),
                  const(1, ROUTER_LANES)],
        out_specs=[pl.BlockSpec((tm * (d // LANES), LANES), lambda i: (i, 0)),
                   row(ROUTER_LANES), row(ROUTER_LANES)],
        out_shape=[jax.ShapeDtypeStruct((m * (d // LANES), LANES), F32),
                   jax.ShapeDtypeStruct((m, ROUTER_LANES), jnp.int32),
                   jax.ShapeDtypeStruct((m, ROUTER_LANES), F32)],
        compiler_params=_cp(("arbitrary",)),
        name="moe_router",
    )(h_a, h_b, ln_w.reshape(1, d), w_hi, w_lo, bias)


X_SLOTS = 3
Y_SLOTS = 2


W_SLOTS = 3
DMA_QUEUES = 2


def _moe_expert_kernel(run_ref, rexp_ref, nused_ref, tok0_ref, tok1_ref, tok2_ref, dst_ref,
                       roww_ref, u_hbm, wg_hbm, wu_hbm, wd_hbm, y_hbm, xbuf, ybuf, wg_f, wu_f,
                       wd_f, wg_b, wu_b, wd_b, sem_in, sem_out, sem_w):
    i = pl.program_id(0)
    n_used, n_runs = nused_ref[0], nused_ref[1]
    tile_rows = MOE_BLOCK * ROW_TILES
    pad_base = y_hbm.shape[0] - Y_SLOTS * tile_rows
    run = run_ref[i]

    def weight_copies(k):
        e, s = rexp_ref[jnp.minimum(k, n_runs - 1)], lax.rem(k, W_SLOTS)
        copies = []
        for hbm, buf in ((wg_hbm, wg_f), (wu_hbm, wu_f), (wd_hbm, wd_f)):
            rows = buf.shape[1] // 2
            for part in range(2):
                sl = pl.ds(part * rows, rows)
                copies.append((pltpu.make_async_copy(hbm.at[e, sl], buf.at[s, sl], sem_w.at[s]),
                               part))
        return copies

    def gather_block(idx_ref, x_slot):
        for r in range(MOE_BLOCK):
            pltpu.make_async_copy(u_hbm.at[pl.ds(idx_ref[0, 0, r], ROW_TILES)],
                                  xbuf.at[x_slot, pl.ds(r * ROW_TILES, ROW_TILES)],
                                  sem_in.at[x_slot]).start(priority=r % DMA_QUEUES)

    def scatter_rows(idx_ref, y_slot, rows):
        for r in rows:
            pltpu.make_async_copy(ybuf.at[y_slot, pl.ds(r * ROW_TILES, ROW_TILES)],
                                  y_hbm.at[pl.ds(idx_ref[0, 0, r], ROW_TILES)],
                                  sem_out.at[y_slot]).start(priority=r % DMA_QUEUES)

    def gather_wait(x_slot):
        pltpu.make_async_copy(u_hbm.at[pl.ds(0, tile_rows)], xbuf.at[x_slot],
                              sem_in.at[x_slot]).wait()

    def scatter_wait(y_slot):
        pltpu.make_async_copy(ybuf.at[y_slot], y_hbm.at[pl.ds(0, tile_rows)],
                              sem_out.at[y_slot]).wait()

    @pl.when(i == 0)
    def _():
        ybuf[0] = jnp.zeros(ybuf.shape[1:], F32)
        for s in range(Y_SLOTS):
            pltpu.make_async_copy(ybuf.at[0], y_hbm.at[pl.ds(pad_base + s * tile_rows, tile_rows)],
                                  sem_out.at[s]).start()
        for k in range(W_SLOTS - 1):
            for cp, queue in weight_copies(k):
                cp.start(priority=queue)
        gather_block(tok0_ref, 0)
        gather_block(tok1_ref, 1)
        for s in range(Y_SLOTS):
            scatter_wait(s)

    @pl.when(i < n_used)
    def _():
        x_slot = lax.rem(i, X_SLOTS)
        y_slot = lax.rem(i, Y_SLOTS)

        @pl.when((i == 0) | (run != run_ref[jnp.maximum(i - 1, 0)]))
        def _():
            for cp, _ in weight_copies(run):
                cp.wait()
            w_slot = lax.rem(run, W_SLOTS)
            wg_b[...] = wg_f[w_slot].astype(BF16)
            wu_b[...] = wu_f[w_slot].astype(BF16)
            wd_b[...] = wd_f[w_slot].astype(BF16)
            for cp, queue in weight_copies(run + W_SLOTS - 1):
                cp.start(priority=queue)

        gather_wait(x_slot)

        @pl.when(i >= Y_SLOTS)
        def _():
            scatter_wait(y_slot)

        x = _tiles_to_rows(xbuf.at[x_slot], MOE_BLOCK).astype(BF16)
        hg = _dot(x, wg_b[...])
        hu = _dot(x, wu_b[...])
        act = (hg * _sigmoid(hg) * hu).astype(BF16)
        y = _dot(act, wd_b[...]) * roww_ref[...]
        _rows_to_tiles(ybuf.at[y_slot], y)
        scatter_rows(dst_ref, y_slot, range(MOE_BLOCK))
        gather_block(tok2_ref, lax.rem(i + 2, X_SLOTS))

        @pl.when(i == n_used - 1)
        def _():
            scatter_wait(y_slot)

            @pl.when(i >= 1)
            def _():
                scatter_wait(1 - y_slot)

            gather_wait(lax.rem(i + 1, X_SLOTS))
            gather_wait(lax.rem(i + 2, X_SLOTS))
            for k in range(W_SLOTS - 1):
                for cp, _ in weight_copies(n_runs + k):
                    cp.wait()


def moe_experts(u_all, row_src, row_dst, row_w, block_run, run_exp, n_used_runs, w_gate, w_up,
                w_down, n_assign):
    d, ff = w_gate.shape[1], w_gate.shape[2]
    n_blocks = row_src.shape[0]
    tile_rows = MOE_BLOCK * ROW_TILES
    smem_blk = lambda off: pl.BlockSpec(
        (1, 1, MOE_BLOCK), lambda i, *_: (jnp.clip(i + off, 0, n_blocks - 1), 0, 0),
        memory_space=pltpu.SMEM)
    hbm = pl.BlockSpec(memory_space=pl.ANY)
    grid_spec = pltpu.PrefetchScalarGridSpec(
        num_scalar_prefetch=3,
        grid=(n_blocks,),
        in_specs=[
            smem_blk(0), smem_blk(1), smem_blk(2), smem_blk(0),
            pl.BlockSpec((MOE_BLOCK, 1), lambda i, *_: (i, 0)),
            hbm, hbm, hbm, hbm,
        ],
        out_specs=hbm,
        scratch_shapes=[
            pltpu.VMEM((X_SLOTS, tile_rows, LANES), F32),
            pltpu.VMEM((Y_SLOTS, tile_rows, LANES), F32),
            pltpu.VMEM((W_SLOTS, d, ff), F32), pltpu.VMEM((W_SLOTS, d, ff), F32),
            pltpu.VMEM((W_SLOTS, ff, d), F32),
            pltpu.VMEM((d, ff), BF16), pltpu.VMEM((d, ff), BF16), pltpu.VMEM((ff, d), BF16),
            pltpu.SemaphoreType.DMA((X_SLOTS,)), pltpu.SemaphoreType.DMA((Y_SLOTS,)),
            pltpu.SemaphoreType.DMA((W_SLOTS,)),
        ],
    )
    y_rows = (n_assign + Y_SLOTS * MOE_BLOCK) * ROW_TILES
    return pl.pallas_call(
        _moe_expert_kernel,
        grid_spec=grid_spec,
        out_shape=jax.ShapeDtypeStruct((y_rows, LANES), F32),
        compiler_params=_cp(("arbitrary",), vmem=MOE_VMEM_LIMIT),
        name="moe_experts",
    )(block_run, run_exp, n_used_runs, row_src, row_src, row_src, row_dst, row_w, u_all,
      w_gate, w_up, w_down)


def _moe_combine_kernel(h_ref, y0_ref, y1_ref, o_ref):
    rows = h_ref.shape[0]
    o_ref[...] = h_ref[...] + (_tiles_to_rows(y0_ref, rows) + _tiles_to_rows(y1_ref, rows))


def moe_combine(h, y_slots, row_off, slot_stride, *, tm):
    m, d = h.shape
    assert row_off % tm == 0 and slot_stride % tm == 0
    off0, off1 = row_off // tm, (row_off + slot_stride) // tm
    return pl.pallas_call(
        _moe_combine_kernel,
        grid=(m // tm,),
        in_specs=[pl.BlockSpec((tm, d), lambda i: (i, 0)),
                  pl.BlockSpec((tm * ROW_TILES, LANES), lambda i: (i + off0, 0)),
                  pl.BlockSpec((tm * ROW_TILES, LANES), lambda i: (i + off1, 0))],
        out_specs=pl.BlockSpec((tm, d), lambda i: (i, 0)),
        out_shape=jax.ShapeDtypeStruct((m, d), F32),
        compiler_params=_cp(("parallel",)),
        name="moe_combine",
    )(h, y_slots, y_slots)


def moe_dispatch(e_idx, gates, slot_stride):
    m = e_idx.shape[0]
    a = m * TOP_K
    e_flat = e_idx.reshape(a)
    _, order, gate_bits = lax.sort(
        (e_flat, jnp.arange(a, dtype=jnp.int32),
         lax.bitcast_convert_type(gates.reshape(a), jnp.int32)), num_keys=1, is_stable=True)
    gate_sorted = lax.bitcast_convert_type(gate_bits, F32)
    counts =jnp.sum(e_flat[:, None] == jnp.arange(N_EXPERTS, dtype=jnp.int32)[None, :],
                     axis=0, dtype=jnp.int32)
    pad_counts = (counts + MOE_BLOCK - 1) // MOE_BLOCK * MOE_BLOCK
    starts = jnp.cumsum(counts) - counts
    pad_ends = jnp.cumsum(pad_counts)
    pad_starts = pad_ends - pad_counts
    n_blocks = a // MOE_BLOCK + N_EXPERTS
    p = n_blocks * MOE_BLOCK
    n_used = (pad_ends[-1] // MOE_BLOCK).astype(jnp.int32)
    blk = jnp.arange(n_blocks, dtype=jnp.int32)
    blk_start = jnp.minimum(blk, n_used - 1) * MOE_BLOCK
    block_exp = jnp.minimum(jnp.sum(blk_start[:, None] >= pad_ends[None, :], axis=1),
                            N_EXPERTS - 1).astype(jnp.int32)
    in_exp = blk * MOE_BLOCK - pad_starts[block_exp]
    row_cnt = jnp.where(blk < n_used, jnp.clip(counts[block_exp] - in_exp, 0, MOE_BLOCK), 0)
    lane = jnp.arange(MOE_BLOCK, dtype=jnp.int32)[None, :]
    valid = lane < row_cnt[:, None]
    src = jnp.clip((starts[block_exp] + in_exp)[:, None] + lane, 0, a - 1)
    assign = order[src]
    row_tok = jnp.where(valid, assign // TOP_K, 0)
    pad_dst = TOP_K * slot_stride + (blk % Y_SLOTS)[:, None] * MOE_BLOCK + lane
    row_dst = jnp.where(valid, (assign % TOP_K) * slot_stride + assign // TOP_K, pad_dst)
    row_w = jnp.where(valid, gate_sorted[src], 0.0)
    as_blocks = lambda x: (x * ROW_TILES).astype(jnp.int32).reshape(n_blocks, 1, MOE_BLOCK)
    has_rows = counts > 0
    run_exp = jnp.argsort(~has_rows, stable=True).astype(jnp.int32)
    block_run = (jnp.cumsum(has_rows) - 1)[block_exp].astype(jnp.int32)
    n_used_runs = jnp.stack([n_used, jnp.sum(has_rows, dtype=jnp.int32)])
    return (as_blocks(row_tok), as_blocks(row_dst), row_w.reshape(p, 1), block_run, run_exp,
            n_used_runs)


def rwkv_params(rw_mu, rw_w0, rw_w2, rw_a0, rw_a2, rw_g2, rw_k_k, rw_k_a, rw_r_k):
    c = RWKV_WIDTH
    mu = jnp.pad(rw_mu, (0, RWKV_PROJ_PAD - RWKV_PROJ)).reshape(1, RWKV_PROJ_PAD)
    w_lora = jnp.zeros((LANES, 2 * c), F32)
    w_lora = w_lora.at[0:DECAY_LORA, 0:c].set(rw_w2).at[DECAY_LORA:LANES, c:2 * c].set(rw_a2)
    w_gate = jnp.pad(rw_g2, ((0, GATE_PAD - GATE_LORA), (0, 0)))
    vec = lambda x: x.reshape(1, c)
    return (mu, vec(rw_w0), vec(rw_a0), vec(rw_k_k), vec(rw_k_a), vec(rw_r_k),
            w_lora.astype(BF16), w_gate.astype(BF16))


def _token_tiles(m):
    return (1024, 512) if m % 1024 == 0 else (m, m)


def _dense_front(x2d, wts, tm):
    pa = norm_matmul(x2d, wts['ln1_w'], wts['wt_att'], tm=tm, tn=512, w_transposed=True)
    pr = norm_matmul(x2d, wts['ln1_w'], wts['wt_rw'], tm=tm, tn=RWKV_PROJ_PAD // 3,
                     w_transposed=True)
    return pa, pr


def _dense_back(x2d, att2d, rw2d, wts, xattn_fn, tm):
    h1 = matmul_residual([att2d, rw2d], [wts['w_out_a'], wts['w_out_r']], x2d, tm=tm, tn=512)
    qx = norm_matmul(h1, wts['ln2_w'], wts['xq_w'], tm=tm, tn=XATT_WIDTH)
    ox = xattn_fn(qx)
    return matmul_residual([ox], [wts['xo_w']], h1, tm=tm, tn=512)


def kernel(x_prompt, x_sample, cache_win_k, cache_win_v, state_wkv, state_shift, cache_mem_k, cache_mem_v, mem_prompt, ln1_w, w_in, q_norm_w, k_norm_w, attn_sinks, rw_mu, rw_w0, rw_w2, rw_a0, rw_a2, rw_g2, rw_k_k, rw_k_a, rw_r_k, rw_ln_w, rw_ln_b, w_out, ln2_w, mem_norm_w, xq_w, xkv_w, xq_norm_w, xk_norm_w, xo_w, ln3_w, router_group_w, router_group_b, router_expert_w, router_expert_b, exp_w_gate, exp_w_up, exp_w_down):
    assert w_in.shape[0] == 1, "single-layer stack"
    bp, seq, d = x_prompt.shape
    bs = x_sample.shape[0]
    mp = bp * seq
    c = RWKV_WIDTH

    router_w = jnp.concatenate(
        [router_group_w[0], router_expert_w[0],
         jnp.zeros((d, ROUTER_LANES - N_EXPERT_GROUPS - N_EXPERTS), F32)], axis=1)
    router_hi = router_w.astype(BF16)
    wts = {
        'ln1_w': ln1_w[0], 'ln2_w': ln2_w[0], 'ln3_w': ln3_w[0],
        'wt_att': cast_rows_bf16(w_in[0].T, 0, ATT_PROJ, ATT_PROJ),
        'wt_rw': cast_rows_bf16(w_in[0].T, ATT_PROJ, RWKV_PROJ, RWKV_PROJ_PAD),
        'w_out_a': w_out[0][:ATT_WIDTH].astype(BF16),
        'w_out_r': w_out[0][ATT_WIDTH:].astype(BF16),
        'xq_w': xq_w[0].astype(BF16), 'xo_w': xo_w[0].astype(BF16),
        'router_hi': router_hi,
        'router_lo': (router_w - router_hi.astype(F32)).astype(BF16),
        'router_b': jnp.pad(jnp.concatenate([router_group_b[0], router_expert_b[0]]),
                            (0, ROUTER_LANES - N_EXPERT_GROUPS - N_EXPERTS)).reshape(1, -1),
    }
    rw_par = rwkv_params(rw_mu[0], rw_w0[0], rw_w2[0], rw_a0[0], rw_a2[0], rw_g2[0],
                         rw_k_k[0], rw_k_a[0], rw_r_k[0])

    tm_p, te_p = _token_tiles(mp)
    xp = x_prompt.reshape(mp, d)
    pa, pr = _dense_front(xp, wts, tm_p)
    pa3 = pa.reshape(bp, seq, ATT_PROJ)
    pr3 = pr.reshape(bp, seq, RWKV_PROJ_PAD)
    tabs_p = rope_tables(np.arange(seq))
    att_p, kn_p = swa_prompt(pa3, tabs_p, q_norm_w[0], k_norm_w[0], attn_sinks[0])
    rw_p, wkv_p = rwkv_seq(pr3, jnp.zeros((bp, 1, RWKV_PROJ_PAD), F32), rw_par,
                           rw_ln_w[0], rw_ln_b[0])
    rw_p = rw_p.reshape(mp, c)

    n_mem = mem_prompt.shape[1]
    kv_mem = norm_matmul(mem_prompt.reshape(bp * n_mem, d), mem_norm_w[0],
                         xkv_w[0].astype(BF16), tm=bp * n_mem, tn=512)
    mem_k = head_rms(kv_mem[:, :XATT_WIDTH], xk_norm_w[0])
    mem_v = kv_mem[:, XATT_WIDTH:]
    mem_k3 = mem_k.reshape(bp, n_mem, XATT_WIDTH)
    mem_v3 = mem_v.reshape(bp, n_mem, XATT_WIDTH)

    def xattn_p(qx):
        return xattn_prompt(qx.reshape(bp, seq, XATT_WIDTH), mem_k3, mem_v3,
                            xq_norm_w[0]).reshape(mp, XATT_WIDTH)

    h2_p = _dense_back(xp, att_p.reshape(mp, ATT_WIDTH), rw_p, wts, xattn_p, tm_p)

    tm_s, te_s = _token_tiles(bs)
    xs = x_sample.reshape(bs, d)
    sa, sr = _dense_front(xs, wts, tm_s)
    tabs_s = rope_tables(PAST_LEN + np.arange(1))
    qk_w = jnp.concatenate([jnp.tile(q_norm_w[0], ATT_HEADS),
                            jnp.tile(k_norm_w[0], ATT_KV_HEADS)]).reshape(1, -1)
    qk = qk_norm_rope(sa[:, :ATT_WIDTH + KV_WIDTH], qk_w, tabs_s)
    nbuf = cache_win_k.shape[2]

    def feature_major(cache):
        return jnp.transpose(cache, (0, 1, 3, 4, 2)).reshape(bs, KV_WIDTH, nbuf)

    def position_major(win):
        return jnp.transpose(win.reshape(1, bs, ATT_KV_HEADS, HEAD_DIM, nbuf), (0, 1, 4, 2, 3))

    att_s, win_k, win_v = swa_decode(
        qk[:, :ATT_WIDTH].reshape(bs, ATT_HEADS, HEAD_DIM),
        qk[:, ATT_WIDTH:].reshape(bs, 1, KV_WIDTH),
        sa[:, ATT_WIDTH + KV_WIDTH:].reshape(bs, 1, KV_WIDTH),
        feature_major(cache_win_k), feature_major(cache_win_v), attn_sinks[0])
    shift_prev = jnp.pad(state_shift[0], ((0, 0), (0, RWKV_PROJ_PAD - RWKV_PROJ)))
    r, ld, kp, v, kk, b, bonus, g = rwkv_prep_tok(sr, shift_prev, rw_par)
    y_s, wkv_s = rwkv_step(r, ld, kp, v, kk, b, jnp.transpose(state_wkv, (0, 2, 3, 4, 1)))
    wkv_s = jnp.transpose(wkv_s, (0, 4, 1, 2, 3))
    rw_s = rwkv_post(y_s, bonus, g, rw_ln_w[0], rw_ln_b[0], tm=te_s, y_channel_major=True)
    def xattn_s(qx):
        q_pad = jnp.pad(qx.reshape(bs, XATT_HEADS, XATT_HEAD_DIM), ((0, 0), (0, 4), (0, 0)))
        rows_of = lambda c: c.reshape(bs, n_mem * XATT_HEADS, XATT_HEAD_DIM)
        o = xattn_decode(q_pad, rows_of(cache_mem_k), rows_of(cache_mem_v), xq_norm_w[0])
        return o[:, :XATT_HEADS].reshape(bs, XATT_WIDTH)

    h2_s = _dense_back(xs, att_s.reshape(bs, ATT_WIDTH), rw_s, wts, xattn_s, tm_s)

    m_all = mp + bs
    slot_stride = m_all
    tc = math.gcd(mp, bs, 512)
    u_all, idx_all, gate_all = moe_router(h2_p, h2_s, wts['ln3_w'], wts['router_hi'],
                                          wts['router_lo'], wts['router_b'], tm=tc)
    row_src, row_dst, row_w, block_run, run_exp, n_used_runs = moe_dispatch(
        idx_all[:, :TOP_K], gate_all[:, :TOP_K], slot_stride)
    y_slots = moe_experts(u_all, row_src, row_dst, row_w, block_run, run_exp, n_used_runs,
                          exp_w_gate[0], exp_w_up[0], exp_w_down[0], TOP_K * slot_stride)
    out_p = moe_combine(h2_p, y_slots, 0, slot_stride, tm=tc)
    out_s = moe_combine(h2_s, y_slots, mp, slot_stride, tm=tc)

    win = min(WINDOW, seq)
    kv_shape = (1, bp, win, ATT_KV_HEADS, HEAD_DIM)
    return (
        out_p.reshape(bp, seq, d),
        out_s.reshape(bs, 1, d),
        kn_p[:, seq - win:].reshape(kv_shape),
        pa3[:, seq - win:, ATT_WIDTH + KV_WIDTH:].reshape(kv_shape),
        wkv_p[None],
        pr3[:, seq - 1, :RWKV_PROJ][None],
        mem_k3.reshape(1, bp, n_mem, XATT_HEADS, XATT_HEAD_DIM),
        mem_v3.reshape(1, bp, n_mem, XATT_HEADS, XATT_HEAD_DIM),
        position_major(win_k),
        position_major(win_v),
        wkv_s,
        sr[:, :RWKV_PROJ].reshape(1, bs, RWKV_PROJ),
    )
```

```python
import functools
import math

import numpy as np
import jax
import jax.numpy as jnp
from jax import lax
from jax.experimental import pallas as pl
from jax.experimental.pallas import tpu as pltpu

F32 = jnp.float32
BF16 = jnp.bfloat16

D_MODEL = 2048
HEAD_DIM = 64
ATT_HEADS = 16
ATT_KV_HEADS = 4
ATT_GROUP = ATT_HEADS // ATT_KV_HEADS
ATT_WIDTH = ATT_HEADS * HEAD_DIM
KV_WIDTH = ATT_KV_HEADS * HEAD_DIM
ATT_PROJ = ATT_WIDTH + 2 * KV_WIDTH
WINDOW = 128
ATT_SCALE = HEAD_DIM ** -0.5
ROPE_THETA = 500000.0
ROT_DIM = HEAD_DIM // 4
PAST_LEN = 16384

RWKV_WIDTH = 1024
RWKV_HEAD_DIM = 64
RWKV_HEADS = 16
DECAY_LORA = 64
AAA_LORA = 64
GATE_LORA = 160
RWKV_PROJ = 3 * RWKV_WIDTH + DECAY_LORA + AAA_LORA + GATE_LORA
RWKV_PROJ_PAD = 3456

N_MEM = 256
XATT_HEADS = 4
XATT_HEAD_DIM = 128
XATT_WIDTH = XATT_HEADS * XATT_HEAD_DIM

N_EXPERT_GROUPS = 8
EXPERTS_PER_GROUP = 8
N_EXPERTS = 64
TOP_K = 2
EXPERT_FF = D_MODEL // 4
MOE_BLOCK = 128

RMS_EPS = 1e-6
GN_EPS = 64e-5

LANES = 128
CHUNK = 64
VMEM_LIMIT = 56 * 1024 * 1024
MOE_VMEM_LIMIT = 60 * 1024 * 1024


def _cp(sem, vmem=VMEM_LIMIT):
    return pltpu.CompilerParams(dimension_semantics=sem, vmem_limit_bytes=vmem)


def _rms_rows(x, w):
    ms = jnp.mean(x * x, axis=-1, keepdims=True)
    return x * lax.rsqrt(ms + RMS_EPS) * w


def _split2(x):
    hi = x.astype(BF16)
    lo = (x - hi.astype(F32)).astype(BF16)
    return hi, lo


def _split3(x):
    h1 = x.astype(BF16)
    r1 = x - h1.astype(F32)
    h2 = r1.astype(BF16)
    h3 = (r1 - h2.astype(F32)).astype(BF16)
    return h1, h2, h3


def _dot(a, b):
    return jnp.dot(a, b, preferred_element_type=F32)


def _dot_nt(a, b):
    return lax.dot_general(a, b, (((1,), (1,)), ((), ())), preferred_element_type=F32)


def _group_sum(x, gmat):
    hi, lo = _split2(x)
    return _dot(hi, gmat) + _dot(lo, gmat)


def _head_indicator():
    r = lax.broadcasted_iota(jnp.int32, (LANES, LANES), 0) // HEAD_DIM
    c = lax.broadcasted_iota(jnp.int32, (LANES, LANES), 1) // HEAD_DIM
    return jnp.where(r == c, 1.0, 0.0).astype(BF16)


def _norm_mm_kernel(x_ref, lnw_ref, w_ref, o_ref, xn_ref, *, w_transposed):
    @pl.when(pl.program_id(1) == 0)
    def _():
        xn_ref[...] = _rms_rows(x_ref[...], lnw_ref[...]).astype(BF16)

    o_ref[...] = (_dot_nt if w_transposed else _dot)(xn_ref[...], w_ref[...])


def norm_matmul(x, ln_w, w_bf16, *, tm, tn, w_transposed=False):
    m, k = x.shape
    n = w_bf16.shape[0 if w_transposed else 1]
    assert m % tm == 0 and n % tn == 0
    w_spec = (pl.BlockSpec((tn, k), lambda i, j: (j, 0)) if w_transposed
              else pl.BlockSpec((k, tn), lambda i, j: (0, j)))
    return pl.pallas_call(
        functools.partial(_norm_mm_kernel, w_transposed=w_transposed),
        grid=(m // tm, n // tn),
        in_specs=[
            pl.BlockSpec((tm, k), lambda i, j: (i, 0)),
            pl.BlockSpec((1, k), lambda i, j: (0, 0)),
            w_spec,
        ],
        out_specs=pl.BlockSpec((tm, tn), lambda i, j: (i, j)),
        out_shape=jax.ShapeDtypeStruct((m, n), F32),
        scratch_shapes=[pltpu.VMEM((tm, k), BF16)],
        compiler_params=_cp(("parallel", "arbitrary")),
        name="norm_matmul",
    )(x, ln_w.reshape(1, k), w_bf16)


def _cast_rows_kernel(w_ref, o_ref, *, n_in):
    @pl.when(pl.program_id(0) < n_in)
    def _():
        o_ref[...] = w_ref[...].astype(BF16)

    @pl.when(pl.program_id(0) >= n_in)
    def _():
        o_ref[...] = jnp.zeros_like(o_ref)


def cast_rows_bf16(w, first_row, n_rows, n_rows_out, *, rows=96):
    k = w.shape[1]
    assert first_row % rows == 0 and n_rows % rows == 0 and n_rows_out % rows == 0
    first, n_in = first_row // rows, n_rows // rows
    return pl.pallas_call(
        functools.partial(_cast_rows_kernel, n_in=n_in),
        grid=(n_rows_out // rows,),
        in_specs=[pl.BlockSpec((rows, k), lambda i: (first + jnp.minimum(i, n_in - 1), 0))],
        out_specs=pl.BlockSpec((rows, k), lambda i: (i, 0)),
        out_shape=jax.ShapeDtypeStruct((n_rows_out, k), BF16),
        compiler_params=_cp(("parallel",)),
        name="cast_rows_bf16",
    )(w)


def _mm_res_kernel(*refs, n_lhs):
    a_refs = refs[:n_lhs]
    w_refs = refs[n_lhs:2 * n_lhs]
    res_ref = refs[2 * n_lhs]
    o_ref = refs[2 * n_lhs + 1]
    acc = res_ref[...]
    for a_ref, w_ref in zip(a_refs, w_refs):
        acc = acc + _dot(a_ref[...].astype(BF16), w_ref[...])
    o_ref[...] = acc


def matmul_residual(lhs_list, w_list, res, *, tm, tn):
    m, n = res.shape
    n_lhs = len(lhs_list)
    assert m % tm == 0 and n % tn == 0
    in_specs = [pl.BlockSpec((tm, a.shape[1]), lambda i, j: (i, 0)) for a in lhs_list]
    in_specs += [pl.BlockSpec((w.shape[0], tn), lambda i, j: (0, j)) for w in w_list]
    in_specs += [pl.BlockSpec((tm, tn), lambda i, j: (i, j))]
    return pl.pallas_call(
        functools.partial(_mm_res_kernel, n_lhs=n_lhs),
        grid=(m // tm, n // tn),
        in_specs=in_specs,
        out_specs=pl.BlockSpec((tm, tn), lambda i, j: (i, j)),
        out_shape=jax.ShapeDtypeStruct((m, n), F32),
        compiler_params=_cp(("parallel", "arbitrary")),
        name="matmul_residual",
    )(*lhs_list, *w_list, res)


def rope_tables(pos):
    half = ROT_DIM // 2
    f32 = np.float32
    inv = f32(ROPE_THETA) ** (-np.arange(half, dtype=f32) * f32(2.0) / f32(ROT_DIM))
    ang = (np.asarray(pos, f32)[:, None] * inv[None, :].astype(f32)).astype(f32)
    cos, sin = np.cos(ang.astype(np.float64)).astype(f32), np.sin(ang.astype(np.float64)).astype(f32)
    t = ang.shape[0]
    ones = np.ones((t, HEAD_DIM - ROT_DIM), f32)
    zeros = np.zeros((t, HEAD_DIM - ROT_DIM), f32)
    z8 = np.zeros((t, half), f32)
    cos_t = np.concatenate([cos, cos, ones], axis=1)
    sin_a = np.concatenate([z8, sin, zeros], axis=1)
    sin_b = np.concatenate([-sin, z8, zeros], axis=1)
    return tuple(jnp.asarray(np.concatenate([a, a], axis=1)) for a in (cos_t, sin_a, sin_b))


def _norm_rope_chunk(x, w, cos_t, sin_a, sin_b, gmat):
    ms = _group_sum(x * x, gmat) * (1.0 / HEAD_DIM)
    xn = x * lax.rsqrt(ms + RMS_EPS) * w
    half = ROT_DIM // 2
    return (xn * cos_t + pltpu.roll(xn, half, axis=1) * sin_a
            + pltpu.roll(xn, LANES - half, axis=1) * sin_b)


def _norm_rope(x, w, tabs, gmat):
    chunks = [
        _norm_rope_chunk(x[:, c * LANES:(c + 1) * LANES], w, *tabs, gmat)
        for c in range(x.shape[1] // LANES)
    ]
    return chunks[0] if len(chunks) == 1 else jnp.concatenate(chunks, axis=1)


def _sink_softmax(s, sink):
    m = jnp.maximum(jnp.max(s, axis=-1, keepdims=True), sink)
    e = jnp.exp(s - m)
    return e / (jnp.sum(e, axis=-1, keepdims=True) + jnp.exp(sink - m))


def _swa_prompt_kernel(q_ref, kc_ref, vc_ref, kp_ref, vp_ref, cc_ref, sac_ref, sbc_ref,
                       cp_ref, sap_ref, sbp_ref, qw_ref, kw_ref, sink_ref, o_ref, kn_ref):
    n = pl.program_id(1)
    blk = q_ref.shape[0]
    gmat = _head_indicator()
    tabs_c = (cc_ref[...], sac_ref[...], sbc_ref[...])
    tabs_p = (cp_ref[...], sap_ref[...], sbp_ref[...])
    q = _norm_rope(q_ref[...], qw_ref[...], tabs_c, gmat)
    k_cur = _norm_rope(kc_ref[...], kw_ref[...], tabs_c, gmat)
    k_prev = _norm_rope(kp_ref[...], kw_ref[...], tabs_p, gmat)
    kn_ref[...] = k_cur
    k_all = jnp.concatenate([k_prev, k_cur], axis=0).astype(BF16)
    v_all = jnp.concatenate([vp_ref[...], vc_ref[...]], axis=0).astype(BF16)

    qi = lax.broadcasted_iota(jnp.int32, (blk, 2 * blk), 0) + blk
    si = lax.broadcasted_iota(jnp.int32, (blk, 2 * blk), 1)
    rel = qi - si
    valid = (rel >= 0) & (rel <= WINDOW) & ((n > 0) | (si >= blk))

    groups = range(ATT_KV_HEADS)
    lanes = [slice(kv * HEAD_DIM, (kv + 1) * HEAD_DIM) for kv in groups]
    heads = [[kv * ATT_GROUP + g for g in range(ATT_GROUP)] for kv in groups]
    q_g = [jnp.concatenate([q[:, h * HEAD_DIM:(h + 1) * HEAD_DIM] for h in heads[kv]],
                           axis=0).astype(BF16) for kv in groups]
    s = [_dot_nt(q_g[kv], k_all[:, lanes[kv]]) * ATT_SCALE for kv in groups]
    p = [jnp.concatenate(
        [_sink_softmax(jnp.where(valid, s[kv][g * blk:(g + 1) * blk], -jnp.inf), sink_ref[h])
         for g, h in enumerate(heads[kv])], axis=0).astype(BF16) for kv in groups]
    o = [_dot(p[kv], v_all[:, lanes[kv]]) for kv in groups]
    for kv in groups:
        for g, h in enumerate(heads[kv]):
            o_ref[:, h * HEAD_DIM:(h + 1) * HEAD_DIM] = o[kv][g * blk:(g + 1) * blk]


def swa_prompt(pa, tabs, q_norm_w, k_norm_w, sinks):
    b, t, _ = pa.shape
    blk = WINDOW
    nb = t // blk
    qb, kb, vb = 0, ATT_WIDTH // KV_WIDTH, ATT_WIDTH // KV_WIDTH + 1
    cur = lambda i, n, *_: (i, n, 0)
    tab_cur = pl.BlockSpec((blk, LANES), lambda i, n: (n, 0))
    tab_prev = pl.BlockSpec((blk, LANES), lambda i, n: (jnp.maximum(n - 1, 0), 0))
    qw = jnp.tile(q_norm_w.reshape(1, HEAD_DIM), (1, 2))
    kw = jnp.tile(k_norm_w.reshape(1, HEAD_DIM), (1, 2))
    return pl.pallas_call(
        _swa_prompt_kernel,
        grid=(b, nb),
        in_specs=[
            pl.BlockSpec((None, blk, ATT_WIDTH), lambda i, n: (i, n, qb)),
            pl.BlockSpec((None, blk, KV_WIDTH), lambda i, n: (i, n, kb)),
            pl.BlockSpec((None, blk, KV_WIDTH), lambda i, n: (i, n, vb)),
            pl.BlockSpec((None, blk, KV_WIDTH), lambda i, n: (i, jnp.maximum(n - 1, 0), kb)),
            pl.BlockSpec((None, blk, KV_WIDTH), lambda i, n: (i, jnp.maximum(n - 1, 0), vb)),
            tab_cur, tab_cur, tab_cur, tab_prev, tab_prev, tab_prev,
            pl.BlockSpec((1, LANES), lambda i, n: (0, 0)),
            pl.BlockSpec((1, LANES), lambda i, n: (0, 0)),
            pl.BlockSpec(memory_space=pltpu.SMEM),
        ],
        out_specs=[
            pl.BlockSpec((None, blk, ATT_WIDTH), cur),
            pl.BlockSpec((None, blk, KV_WIDTH), cur),
        ],
        out_shape=[
            jax.ShapeDtypeStruct((b, t, ATT_WIDTH), F32),
            jax.ShapeDtypeStruct((b, t, KV_WIDTH), F32),
        ],
        compiler_params=_cp(("parallel", "arbitrary")),
        name="swa_prompt",
    )(pa, pa, pa, pa, pa, *tabs, *tabs, qw, kw, sinks)


def _qk_norm_rope_kernel(x_ref, w_ref, c_ref, sa_ref, sb_ref, o_ref):
    gmat = _head_indicator()
    tabs = (c_ref[...], sa_ref[...], sb_ref[...])
    for c in range(x_ref.shape[1] // LANES):
        sl = slice(c * LANES, (c + 1) * LANES)
        o_ref[:, sl] = _norm_rope_chunk(x_ref[:, sl], w_ref[:, sl], *tabs, gmat)


def qk_norm_rope(x, w_row, tabs):
    m, w = x.shape
    full = lambda *shape: pl.BlockSpec(shape, lambda: (0,) * len(shape))
    return pl.pallas_call(
        _qk_norm_rope_kernel,
        in_specs=[full(m, w), full(1, w), full(1, LANES), full(1, LANES), full(1, LANES)],
        out_specs=full(m, w),
        out_shape=jax.ShapeDtypeStruct((m, w), F32),
        name="qk_norm_rope",
    )(x, w_row, *tabs)


def _swa_decode_kernel(q_ref, kn_ref, vn_ref, knt_ref, vnt_ref, ck_ref, cv_ref, sink_ref, o_ref,
                       kw_ref, vw_ref):
    bb = q_ref.shape[0]
    nbuf = ck_ref.shape[2]
    row_kv = lax.broadcasted_iota(jnp.int32, (ATT_HEADS, KV_WIDTH), 0) // ATT_GROUP
    lane_kv = lax.broadcasted_iota(jnp.int32, (ATT_HEADS, KV_WIDTH), 1) // HEAD_DIM
    own = row_kv == lane_kv
    sink = sink_ref[...]
    seq = lax.broadcasted_iota(jnp.int32, knt_ref.shape, 1)
    pos = lax.broadcasted_iota(jnp.int32, (KV_WIDTH, nbuf), 1)
    b0 = pl.program_id(0) * bb
    for b in range(bb):
        q2 = q_ref[b]
        q_exp = jnp.where(own, jnp.concatenate([q2] * ATT_KV_HEADS, axis=1), 0.0)
        k_new, v_new = kn_ref[b], vn_ref[b]
        k_buf, v_buf = ck_ref[b], cv_ref[b]
        s_buf = _dot(q_exp.astype(BF16), k_buf.astype(BF16)) * ATT_SCALE
        s_new = jnp.sum(q_exp * k_new, axis=-1, keepdims=True) * ATT_SCALE
        m = jnp.maximum(jnp.maximum(jnp.max(s_buf, axis=-1, keepdims=True), s_new), sink)
        e_buf = jnp.exp(s_buf - m)
        e_new = jnp.exp(s_new - m)
        inv = 1.0 / (jnp.sum(e_buf, axis=-1, keepdims=True) + e_new + jnp.exp(sink - m))
        o = _dot_nt((e_buf * inv).astype(BF16), v_buf.astype(BF16)) + (e_new * inv) * v_new
        o = jnp.where(own, o, 0.0)
        o_ref[b] = (o[:, 0:HEAD_DIM] + o[:, HEAD_DIM:2 * HEAD_DIM]
                    + o[:, 2 * HEAD_DIM:3 * HEAD_DIM] + o[:, 3 * HEAD_DIM:4 * HEAD_DIM])
        k_col = jnp.sum(jnp.where(seq == b0 + b, knt_ref[...], 0.0), axis=1, keepdims=True)
        v_col = jnp.sum(jnp.where(seq == b0 + b, vnt_ref[...], 0.0), axis=1, keepdims=True)
        kw_ref[b] = jnp.where(pos == nbuf - 1, k_col, pltpu.roll(k_buf, nbuf - 1, axis=1))
        vw_ref[b] = jnp.where(pos == nbuf - 1, v_col, pltpu.roll(v_buf, nbuf - 1, axis=1))


def swa_decode(q, k_new, v_new, cache_kt, cache_vt, sinks, *, bb=8):
    b, _, nbuf = cache_kt.shape
    blk3 = lambda s1, s2: pl.BlockSpec((bb, s1, s2), lambda i: (i, 0, 0))
    whole = pl.BlockSpec((KV_WIDTH, b), lambda i: (0, 0))
    return pl.pallas_call(
        _swa_decode_kernel,
        grid=(b // bb,),
        in_specs=[
            blk3(ATT_HEADS, HEAD_DIM), blk3(1, KV_WIDTH), blk3(1, KV_WIDTH), whole, whole,
            blk3(KV_WIDTH, nbuf), blk3(KV_WIDTH, nbuf),
            pl.BlockSpec((ATT_HEADS, 1), lambda i: (0, 0)),
        ],
        out_specs=[blk3(ATT_HEADS, HEAD_DIM), blk3(KV_WIDTH, nbuf), blk3(KV_WIDTH, nbuf)],
        out_shape=[
            jax.ShapeDtypeStruct((b, ATT_HEADS, HEAD_DIM), F32),
            jax.ShapeDtypeStruct((b, KV_WIDTH, nbuf), F32),
            jax.ShapeDtypeStruct((b, KV_WIDTH, nbuf), F32),
        ],
        compiler_params=_cp(("parallel",)),
        name="swa_decode",
    )(q, k_new, v_new, k_new.reshape(b, KV_WIDTH).T, v_new.reshape(b, KV_WIDTH).T,
      cache_kt, cache_vt, sinks.reshape(ATT_HEADS, 1))


def _head_rms_kernel(x_ref, w_ref, o_ref):
    for h in range(x_ref.shape[1] // XATT_HEAD_DIM):
        sl = slice(h * XATT_HEAD_DIM, (h + 1) * XATT_HEAD_DIM)
        o_ref[:, sl] = _rms_rows(x_ref[:, sl], w_ref[...])


def head_rms(x, w):
    m, wd = x.shape
    return pl.pallas_call(
        _head_rms_kernel,
        in_specs=[pl.BlockSpec((m, wd), lambda: (0, 0)),
                  pl.BlockSpec((1, XATT_HEAD_DIM), lambda: (0, 0))],
        out_specs=pl.BlockSpec((m, wd), lambda: (0, 0)),
        out_shape=jax.ShapeDtypeStruct((m, wd), F32),
        name="head_rms",
    )(x, w.reshape(1, XATT_HEAD_DIM))


def _xattn_prompt_kernel(q_ref, k_ref, v_ref, w_ref, o_ref):
    scale = 1.0 / math.sqrt(XATT_HEAD_DIM)
    for h in range(XATT_HEADS):
        sl = slice(h * XATT_HEAD_DIM, (h + 1) * XATT_HEAD_DIM)
        qn = _rms_rows(q_ref[:, sl], w_ref[...]).astype(BF16)
        s = _dot_nt(qn, k_ref[:, sl].astype(BF16)) * scale
        e = jnp.exp(s - jnp.max(s, axis=-1, keepdims=True))
        p = e / jnp.sum(e, axis=-1, keepdims=True)
        o_ref[:, sl] = _dot(p.astype(BF16), v_ref[:, sl].astype(BF16))


def xattn_prompt(q, mem_k, mem_v, xq_norm_w, *, tq=512):
    b, t, w = q.shape
    n_mem = mem_k.shape[1]
    return pl.pallas_call(
        _xattn_prompt_kernel,
        grid=(b, t // tq),
        in_specs=[
            pl.BlockSpec((None, tq, w), lambda i, j: (i, j, 0)),
            pl.BlockSpec((None, n_mem, w), lambda i, j: (i, 0, 0)),
            pl.BlockSpec((None, n_mem, w), lambda i, j: (i, 0, 0)),
            pl.BlockSpec((1, XATT_HEAD_DIM), lambda i, j: (0, 0)),
        ],
        out_specs=pl.BlockSpec((None, tq, w), lambda i, j: (i, j, 0)),
        out_shape=jax.ShapeDtypeStruct((b, t, w), F32),
        compiler_params=_cp(("parallel", "arbitrary")),
        name="xattn_prompt",
    )(q, mem_k, mem_v, xq_norm_w.reshape(1, XATT_HEAD_DIM))


def _xattn_decode_kernel(q_ref, k_ref, v_ref, w_ref, o_ref):
    bb, rows, _ = q_ref.shape
    n_keys = k_ref.shape[1]
    scale = 1.0 / math.sqrt(XATT_HEAD_DIM)
    own = (lax.broadcasted_iota(jnp.int32, (rows, n_keys), 1) % XATT_HEADS
           == lax.broadcasted_iota(jnp.int32, (rows, n_keys), 0) % XATT_HEADS)
    seqs = range(bb)
    qn = [_rms_rows(q_ref[b], w_ref[...]).astype(BF16) for b in seqs]
    s = [jnp.where(own, _dot_nt(qn[b], k_ref[b].astype(BF16)) * scale, -jnp.inf) for b in seqs]
    e = [jnp.exp(s[b] - jnp.max(s[b], axis=-1, keepdims=True)) for b in seqs]
    p = [(e[b] / jnp.sum(e[b], axis=-1, keepdims=True)).astype(BF16) for b in seqs]
    for b in seqs:
        o_ref[b] = _dot(p[b], v_ref[b].astype(BF16))


def xattn_decode(q_pad, mem_k, mem_v, xq_norm_w, *, bb=8):
    b, rows, _ = q_pad.shape
    n_keys = mem_k.shape[1]
    kv = pl.BlockSpec((bb, n_keys, XATT_HEAD_DIM), lambda i: (i, 0, 0))
    return pl.pallas_call(
        _xattn_decode_kernel,
        grid=(b // bb,),
        in_specs=[pl.BlockSpec((bb, rows, XATT_HEAD_DIM), lambda i: (i, 0, 0)), kv, kv,
                  pl.BlockSpec((1, XATT_HEAD_DIM), lambda i: (0, 0))],
        out_specs=pl.BlockSpec((bb, rows, XATT_HEAD_DIM), lambda i: (i, 0, 0)),
        out_shape=jax.ShapeDtypeStruct((b, rows, XATT_HEAD_DIM), F32),
        compiler_params=_cp(("parallel",)),
        name="xattn_decode",
    )(q_pad, mem_k, mem_v, xq_norm_w.reshape(1, XATT_HEAD_DIM))


LORA_OFF = 3 * RWKV_WIDTH
GATE_OFF = LORA_OFF + DECAY_LORA + AAA_LORA
GATE_PAD = RWKV_PROJ_PAD - GATE_OFF


def _sigmoid(x):
    return 1.0 / (1.0 + jnp.exp(-x))


def _per_chunk(fn, *arrays):
    w = arrays[0].shape[1]
    outs = [fn(*(a[:, c * LANES:(c + 1) * LANES] for a in arrays)) for c in range(w // LANES)]
    return jnp.concatenate(outs, axis=1)


def _rwkv_prep_core(pr, prev, mu, w0, a0, kk_w, ka_w, rk_w, w_lora, w_gate):
    c = RWKV_WIDTH
    gmat = _head_indicator()
    xm = pr + (prev - pr) * mu
    r, k, v = xm[:, 0:c], xm[:, c:2 * c], xm[:, 2 * c:3 * c]
    lora = xm[:, LORA_OFF:LORA_OFF + LANES]
    lane = lax.broadcasted_iota(jnp.int32, lora.shape, 1)
    lora_in = jnp.where(lane < DECAY_LORA, jnp.tanh(lora), lora)
    wa = _dot(lora_in.astype(BF16), w_lora)
    z = -(w0 + wa[:, 0:c])
    softplus = jnp.maximum(z, 0.0) + jnp.log(1.0 + jnp.exp(-jnp.abs(z)))
    log_decay = -jnp.exp(-softplus - 0.5)
    a = _sigmoid(a0 + wa[:, c:2 * c])
    g = _dot(_sigmoid(xm[:, GATE_OFF:GATE_OFF + GATE_PAD]).astype(BF16), w_gate)
    kk = k * kk_w
    norm = jnp.sqrt(_per_chunk(lambda t: _group_sum(t * t, gmat), kk))
    kk = kk / jnp.maximum(norm, 1e-12)
    kp = k * (1.0 + (a - 1.0) * ka_w)
    bonus = _per_chunk(lambda t: _group_sum(t, gmat), r * kp * rk_w) * v
    return r, log_decay, kp, v, kk, kk * a, bonus, g


def _rwkv_prep_tok_kernel(pr_ref, prev_ref, mu_ref, w0_ref, a0_ref, kkw_ref, kaw_ref, rkw_ref,
                          wl_ref, wg_ref, *out_refs):
    outs = _rwkv_prep_core(pr_ref[...], prev_ref[...], mu_ref[...], w0_ref[...], a0_ref[...],
                           kkw_ref[...], kaw_ref[...], rkw_ref[...], wl_ref[...], wg_ref[...])
    for k, (o_ref, o) in enumerate(zip(out_refs, outs)):
        o_ref[...] = o.T if k < N_STEP_VECS else o


def _rwkv_param_specs(index_map):
    c = RWKV_WIDTH
    shapes = [(1, RWKV_PROJ_PAD)] + [(1, c)] * 5 + [(LANES, 2 * c), (GATE_PAD, c)]
    return [pl.BlockSpec(s, index_map) for s in shapes]


N_STEP_VECS = 6


def rwkv_prep_tok(pr, prev, params):
    m, wd = pr.shape
    c = RWKV_WIDTH
    shapes = [(c, m)] * N_STEP_VECS + [(m, c)] * 2
    return pl.pallas_call(
        _rwkv_prep_tok_kernel,
        grid=(1,),
        in_specs=[pl.BlockSpec((m, wd), lambda i: (0, 0))] * 2
        + _rwkv_param_specs(lambda i: (0, 0)),
        out_specs=[pl.BlockSpec(s, lambda i: (0, 0)) for s in shapes],
        out_shape=[jax.ShapeDtypeStruct(s, F32) for s in shapes],
        compiler_params=_cp(("arbitrary",)),
        name="rwkv_prep_tok",
    )(pr, prev, *params)


def _dot_tn(a, b):
    return lax.dot_general(a, b, (((0,), (0,)), ((), ())), preferred_element_type=F32)


def _rwkv_scan_tile(r_all, ld_all, kp_all, v_all, kk_all, b_all, s_cur, y_ref):
    n = CHUNK
    nc = r_all.shape[0] // n
    ti = lax.broadcasted_iota(jnp.int32, (n, n), 0)
    si = lax.broadcasted_iota(jnp.int32, (n, n), 1)
    tri = jnp.where(si <= ti, 1.0, 0.0).astype(BF16)
    t2 = lax.broadcasted_iota(jnp.int32, (n, LANES), 0)
    lane2 = lax.broadcasted_iota(jnp.int32, (n, LANES), 1)
    s2 = lane2 % n
    low = lane2 < n
    strict2, incl2, eye2 = s2 < t2, s2 <= t2, s2 == t2
    low4 = lax.broadcasted_iota(jnp.int32, (2 * n, LANES), 1) < n
    top4 = lax.broadcasted_iota(jnp.int32, (2 * n, LANES), 0) < n
    diag_blk = top4 == low4
    own4 = {0: low4, 1: ~low4}
    pairs = range(RWKV_HEADS // 2)
    cs = [slice(c * LANES, (c + 1) * LANES) for c in pairs]

    pre = []
    for j in range(nc):
        rows = slice(j * n, (j + 1) * n)
        ld = ld_all[rows, :]
        l1, l2, l3 = _split3(ld)
        lc = _dot(tri, l1) + _dot(tri, l2) + _dot(tri, l3)
        lc_end = lc[n - 1:n, :]
        e_neg = jnp.exp(-lc)
        kk, b, kp = kk_all[rows, :], b_all[rows, :], kp_all[rows, :]
        to_end = jnp.exp(lc_end - lc)
        pre.append(dict(
            a_t=(-kk * jnp.exp(lc - ld)).astype(BF16), b_t=(b * e_neg).astype(BF16),
            k_t=(kp * e_neg).astype(BF16), r_t=(r_all[rows, :] * jnp.exp(lc)).astype(BF16),
            b_e=(b * to_end).astype(BF16), k_e=(kp * to_end).astype(BF16),
            v_b=v_all[rows, :].astype(BF16), g_end=jnp.exp(lc_end)))
    items = [(j, c, par) for j in range(nc) for c in pairs for par in (0, 1)]
    at = {it: i for i, it in enumerate(items)}
    ar = {(j, c): jnp.concatenate([pre[j]['a_t'][:, cs[c]], pre[j]['r_t'][:, cs[c]]], axis=0)
          for j in range(nc) for c in pairs}
    kb = {(j, c): jnp.concatenate([pre[j]['k_t'][:, cs[c]], pre[j]['b_t'][:, cs[c]]], axis=0)
          for j in range(nc) for c in pairs}
    gm = [_dot_nt(jnp.where(own4[par], ar[j, c], 0).astype(BF16), kb[j, c]) for j, c, par in items]
    top = [g[0:n] for g in gm]
    pm = [jnp.where(incl2, g[n:2 * n], 0.0).astype(BF16) for g in gm]
    lak = [jnp.where(strict2 & low, t, 0.0).astype(BF16) for t in top]
    lv = [_dot(lak[i][:, 0:n], pre[j]['v_b'][:, cs[c]]) for i, (j, c, par) in enumerate(items)]
    z = [jnp.where(low, jnp.where(eye2, 1.0, 0.0), jnp.where(strict2, t, 0.0)) for t in top]
    for _ in range(6):
        zb = [zz.astype(BF16) for zz in z]
        res = [_dot(jnp.where(low, 0, zb[i]).astype(BF16), jnp.concatenate([zb[i], zb[i]], axis=0))
               for i in range(len(items))]
        z = [res[i] + jnp.where(low, z[i], 0.0) for i in range(len(items))]
    tmat = [zz[:, 0:n].astype(BF16) for zz in z]

    for j in range(nc):
        p = pre[j]
        sw = [_dot_nt(ar[j, c], s_cur[c].astype(BF16)) for c in pairs]
        w0 = [(sw[c][0:n] + jnp.where(low, lv[at[j, c, 0]], lv[at[j, c, 1]])).astype(BF16)
              for c in pairs]
        u = [jnp.where(low, _dot(tmat[at[j, c, 0]], w0[c]),
                       _dot(tmat[at[j, c, 1]], w0[c])).astype(BF16) for c in pairs]
        vu = [jnp.concatenate([p['v_b'][:, cs[c]], u[c]], axis=0) for c in pairs]
        yb = [jnp.where(low, _dot(pm[at[j, c, 0]], vu[c]), _dot(pm[at[j, c, 1]], vu[c]))
              for c in pairs]
        for c in pairs:
            y_ref[j * n:(j + 1) * n, cs[c]] = sw[c][n:2 * n] + yb[c]
        upd = [_dot_tn(jnp.concatenate([u[c], p['v_b'][:, cs[c]]], axis=0),
                       jnp.concatenate([p['b_e'][:, cs[c]], p['k_e'][:, cs[c]]], axis=0))
               for c in pairs]
        s_cur = [s_cur[c] * p['g_end'][:, cs[c]] + jnp.where(diag_blk, upd[c], 0.0) for c in pairs]
    return s_cur


def _group_norm_gate(y, bonus, g, ln_w, ln_b, gmat):
    inv = 1.0 / RWKV_HEAD_DIM
    d = y - _group_sum(y, gmat) * inv
    var = _group_sum(d * d, gmat) * inv
    return (d * lax.rsqrt(var + GN_EPS) * ln_w + ln_b + bonus) * g


def _rwkv_seq_kernel(pr_ref, prev0_ref, mu_ref, w0_ref, a0_ref, kkw_ref, kaw_ref, rkw_ref,
                     wl_ref, wg_ref, lnw_ref, lnb_ref, o_ref, s_out_ref, s_ref, last_ref, y_ref):
    @pl.when(pl.program_id(1) == 0)
    def _():
        s_ref[...] = jnp.zeros_like(s_ref)
        last_ref[...] = prev0_ref[...]

    pr = pr_ref[...]
    rows = pr.shape[0]
    row = lax.broadcasted_iota(jnp.int32, (rows, 1), 0)
    prev = jnp.where(row == 0, last_ref[...], pltpu.roll(pr, 1, axis=0))
    last_ref[...] = pr[rows - 1:rows, :]
    r, ld, kp, v, kk, b, bonus, g = _rwkv_prep_core(
        pr, prev, mu_ref[...], w0_ref[...], a0_ref[...], kkw_ref[...], kaw_ref[...],
        rkw_ref[...], wl_ref[...], wg_ref[...])
    pairs = range(RWKV_HEADS // 2)
    s_new = _rwkv_scan_tile(r, ld, kp, v, kk, b, [s_ref[c] for c in pairs], y_ref)
    n = RWKV_HEAD_DIM
    for c in pairs:
        s_ref[c] = s_new[c]
        s_out_ref[2 * c] = s_new[c][0:n, 0:n]
        s_out_ref[2 * c + 1] = s_new[c][n:2 * n, n:2 * n]
    gmat = _head_indicator()
    for c in pairs:
        sl = slice(c * LANES, (c + 1) * LANES)
        o_ref[:, sl] = _group_norm_gate(y_ref[:, sl], bonus[:, sl], g[:, sl], lnw_ref[:, sl],
                                        lnb_ref[:, sl], gmat)


SCAN_CHUNKS = 4


def rwkv_seq(pr, prev0, params, ln_w, ln_b):
    bsz, t, wd = pr.shape
    c = RWKV_WIDTH
    rows = SCAN_CHUNKS * CHUNK
    assert t % rows == 0
    vec = pl.BlockSpec((1, c), lambda i, j: (0, 0))
    st = pl.BlockSpec((None, RWKV_HEADS, RWKV_HEAD_DIM, RWKV_HEAD_DIM), lambda i, j: (i, 0, 0, 0))
    return pl.pallas_call(
        _rwkv_seq_kernel,
        grid=(bsz, t // rows),
        in_specs=[pl.BlockSpec((None, rows, wd), lambda i, j: (i, j, 0)),
                  pl.BlockSpec((None, 1, wd), lambda i, j: (i, 0, 0))]
        + _rwkv_param_specs(lambda i, j: (0, 0)) + [vec, vec],
        out_specs=[pl.BlockSpec((None, rows, c), lambda i, j: (i, j, 0)), st],
        out_shape=[jax.ShapeDtypeStruct((bsz, t, c), F32),
                   jax.ShapeDtypeStruct((bsz, RWKV_HEADS, RWKV_HEAD_DIM, RWKV_HEAD_DIM), F32)],
        scratch_shapes=[pltpu.VMEM((RWKV_HEADS // 2, LANES, LANES), F32),
                        pltpu.VMEM((1, wd), F32), pltpu.VMEM((rows, c), F32)],
        compiler_params=_cp(("parallel", "arbitrary")),
        name="rwkv_seq",
    )(pr, prev0, *params, ln_w.reshape(1, c), ln_b.reshape(1, c))


STEP_UNROLL = 8


def _rwkv_step_kernel(r_ref, ld_ref, kp_ref, v_ref, kk_ref, b_ref, s_ref, y_ref, s_out_ref):
    n = RWKV_HEAD_DIM
    neg_kk, decay = -kk_ref[...], jnp.exp(ld_ref[...])
    b_mat, kp_mat, r_mat = b_ref[...], kp_ref[...], r_ref[...]

    def body(i, carry):
        v0 = pl.multiple_of(i * STEP_UNROLL, STEP_UNROLL)
        v_rows = v_ref[pl.ds(v0, STEP_UNROLL), :]
        rows = range(STEP_UNROLL)
        s = [s_ref[v0 + j] for j in rows]
        sa = [jnp.sum(s[j] * neg_kk, axis=0, keepdims=True) for j in rows]
        s_new = [s[j] * decay + sa[j] * b_mat + v_rows[j:j + 1, :] * kp_mat for j in rows]
        y = [jnp.sum(s_new[j] * r_mat, axis=0, keepdims=True) for j in rows]
        for j in rows:
            s_out_ref[v0 + j] = s_new[j]
        y_ref[pl.ds(v0, STEP_UNROLL), :] = jnp.concatenate(y, axis=0)
        return carry

    lax.fori_loop(0, n // STEP_UNROLL, body, 0)


def rwkv_step(r, ld, kp, v, kk, b, state_t):
    _, nh, n, _, bsz = state_t.shape
    vec = pl.BlockSpec((n, bsz), lambda h: (h, 0))
    st = pl.BlockSpec((None, None, n, n, bsz), lambda h: (0, h, 0, 0, 0))
    return pl.pallas_call(
        _rwkv_step_kernel,
        grid=(nh,),
        in_specs=[vec] * 6 + [st],
        out_specs=[vec, st],
        out_shape=[jax.ShapeDtypeStruct((nh * n, bsz), F32),
                   jax.ShapeDtypeStruct(state_t.shape, F32)],
        compiler_params=_cp(("parallel",)),
        name="rwkv_step",
    )(r, ld, kp, v, kk, b, state_t)


def _rwkv_post_kernel(y_ref, bonus_ref, g_ref, lnw_ref, lnb_ref, o_ref, *, y_channel_major):
    gmat = _head_indicator()
    for c in range(o_ref.shape[1] // LANES):
        sl = slice(c * LANES, (c + 1) * LANES)
        y = y_ref[sl, :].T if y_channel_major else y_ref[:, sl]
        o_ref[:, sl] = _group_norm_gate(y, bonus_ref[:, sl], g_ref[:, sl], lnw_ref[:, sl],
                                        lnb_ref[:, sl], gmat)


def rwkv_post(y, bonus, g, ln_w, ln_b, *, tm, y_channel_major=False):
    m, c = bonus.shape
    blk = pl.BlockSpec((tm, c), lambda i: (i, 0))
    y_blk = pl.BlockSpec((c, tm), lambda i: (0, i)) if y_channel_major else blk
    vec = pl.BlockSpec((1, c), lambda i: (0, 0))
    return pl.pallas_call(
        functools.partial(_rwkv_post_kernel, y_channel_major=y_channel_major),
        grid=(m // tm,),
        in_specs=[y_blk, blk, blk, vec, vec],
        out_specs=blk,
        out_shape=jax.ShapeDtypeStruct((m, c), F32),
        compiler_params=_cp(("parallel",)),
        name="rwkv_post",
    )(y, bonus, g, ln_w.reshape(1, c), ln_b.reshape(1, c))


ROUTER_LANES = LANES
ROW_TILES = 1
ROW_LANES = D_MODEL // ROW_TILES


def _rows_to_tiles(ref, x):
    rows = x.shape[0]
    if ROW_TILES == 1:
        ref[...] = x
        return
    for j in range(ROW_TILES):
        ref[pl.ds(j, rows, stride=ROW_TILES), :] = x[:, j * ROW_LANES:(j + 1) * ROW_LANES]


def _tiles_to_rows(ref, rows):
    if ROW_TILES == 1:
        return ref[...]
    return jnp.concatenate(
        [ref[pl.ds(j, rows, stride=ROW_TILES), :] for j in range(ROW_TILES)], axis=1)


def _router_kernel(ha_ref, hb_ref, lnw_ref, whi_ref, wlo_ref, bias_ref, u_ref, idx_ref, gate_ref,
                   *, steps_a):
    use_a = pl.program_id(0) < steps_a
    h = jnp.where(use_a, ha_ref[...], hb_ref[...])
    u = _rms_rows(h, lnw_ref[...])
    _rows_to_tiles(u_ref, u)
    u_hi, u_lo = _split2(u)
    w_hi = whi_ref[...]
    logits = _dot(u_hi, w_hi) + _dot(u_lo, w_hi) + _dot(u_hi, wlo_ref[...]) + bias_ref[...]
    lane = lax.broadcasted_iota(jnp.int32, logits.shape, 1)
    neg = -jnp.inf

    def first_max(x):
        m = jnp.max(x, axis=1, keepdims=True)
        return m, jnp.min(jnp.where(x == m, lane, ROUTER_LANES), axis=1, keepdims=True)

    gl = jnp.where(lane < N_EXPERT_GROUPS, logits, neg)
    g_max, g_idx = first_max(gl)
    g_gate = 1.0 / jnp.sum(jnp.exp(gl - g_max), axis=1, keepdims=True)
    lo = N_EXPERT_GROUPS + g_idx * EXPERTS_PER_GROUP
    el = jnp.where((lane >= lo) & (lane < lo + EXPERTS_PER_GROUP), logits, neg)
    v1, i1 = first_max(el)
    v2, i2 = first_max(jnp.where(lane == i1, neg, el))
    e2 = jnp.exp(v2 - v1)
    w1 = g_gate / (1.0 + e2)
    w2 = g_gate * e2 / (1.0 + e2)
    idx_ref[...] = jnp.where(lane == 0, i1 - N_EXPERT_GROUPS,
                             jnp.where(lane == 1, i2 - N_EXPERT_GROUPS, 0))
    gate_ref[...] = jnp.where(lane == 0, w1, jnp.where(lane == 1, w2, 0.0))


def moe_router(h_a, h_b, ln_w, w_hi, w_lo, bias, *, tm):
    (ma, d), mb = h_a.shape, h_b.shape[0]
    assert ma % tm == 0 and mb % tm == 0
    steps_a, steps_b = ma // tm, mb // tm
    m = ma + mb
    const = lambda r, w: pl.BlockSpec((r, w), lambda i: (0, 0))
    row = lambda w: pl.BlockSpec((tm, w), lambda i: (i, 0))
    return pl.pallas_call(
        functools.partial(_router_kernel, steps_a=steps_a),
        grid=(steps_a + steps_b,),
        in_specs=[pl.BlockSpec((tm, d), lambda i: (jnp.minimum(i, steps_a - 1), 0)),
                  pl.BlockSpec((tm, d), lambda i: (jnp.maximum(i - steps_a, 0), 0)),
                  const(1, d), const(d, ROUTER_LANES), const(d, ROUTER_LANES),
                  const(1, ROUTER_LANES)],
        out_specs=[pl.BlockSpec((tm * ROW_TILES, ROW_LANES), lambda i: (i, 0)),
                   row(ROUTER_LANES), row(ROUTER_LANES)],
        out_shape=[jax.ShapeDtypeStruct((m * ROW_TILES, ROW_LANES), F32),
                   jax.ShapeDtypeStruct((m, ROUTER_LANES), jnp.int32),
                   jax.ShapeDtypeStruct((m, ROUTER_LANES), F32)],
        compiler_params=_cp(("arbitrary",)),
        name="moe_router",
    )(h_a, h_b, ln_w.reshape(1, d), w_hi, w_lo, bias)


X_SLOTS = 3
Y_SLOTS = 2


W_SLOTS = 3
DMA_QUEUES = 2


def _moe_expert_kernel(run_ref, rexp_ref, nused_ref, tok0_ref, tok1_ref, tok2_ref, dst_ref,
                       roww_ref, u_hbm, wg_hbm, wu_hbm, wd_hbm, y_hbm, xbuf, ybuf, wg_f, wu_f,
                       wd_f, wg_b, wu_b, wd_b, sem_in, sem_out, sem_w):
    i = pl.program_id(0)
    n_used, n_runs = nused_ref[0], nused_ref[1]
    tile_rows = MOE_BLOCK * ROW_TILES
    pad_base = y_hbm.shape[0] - Y_SLOTS * tile_rows
    run = run_ref[i]

    def weight_copies(k):
        e, s = rexp_ref[jnp.minimum(k, n_runs - 1)], lax.rem(k, W_SLOTS)
        copies = []
        for hbm, buf in ((wg_hbm, wg_f), (wu_hbm, wu_f), (wd_hbm, wd_f)):
            rows = buf.shape[1] // 2
            for part in range(2):
                sl = pl.ds(part * rows, rows)
                copies.append((pltpu.make_async_copy(hbm.at[e, sl], buf.at[s, sl], sem_w.at[s]),
                               part))
        return copies

    def gather_block(idx_ref, x_slot):
        for r in range(MOE_BLOCK):
            pltpu.make_async_copy(u_hbm.at[pl.ds(idx_ref[0, 0, r], ROW_TILES)],
                                  xbuf.at[x_slot, pl.ds(r * ROW_TILES, ROW_TILES)],
                                  sem_in.at[x_slot]).start(priority=r % DMA_QUEUES)

    def scatter_rows(idx_ref, y_slot, rows):
        for r in rows:
            pltpu.make_async_copy(ybuf.at[y_slot, pl.ds(r * ROW_TILES, ROW_TILES)],
                                  y_hbm.at[pl.ds(idx_ref[0, 0, r], ROW_TILES)],
                                  sem_out.at[y_slot]).start(priority=r % DMA_QUEUES)

    def gather_wait(x_slot):
        pltpu.make_async_copy(u_hbm.at[pl.ds(0, tile_rows)], xbuf.at[x_slot],
                              sem_in.at[x_slot]).wait()

    def scatter_wait(y_slot):
        pltpu.make_async_copy(ybuf.at[y_slot], y_hbm.at[pl.ds(0, tile_rows)],
                              sem_out.at[y_slot]).wait()

    @pl.when(i == 0)
    def _():
        ybuf[0] = jnp.zeros(ybuf.shape[1:], F32)
        for s in range(Y_SLOTS):
            pltpu.make_async_copy(ybuf.at[0], y_hbm.at[pl.ds(pad_base + s * tile_rows, tile_rows)],
                                  sem_out.at[s]).start()
        for k in range(W_SLOTS - 1):
            for cp, queue in weight_copies(k):
                cp.start(priority=queue)
        gather_block(tok0_ref, 0)
        gather_block(tok1_ref, 1)
        for s in range(Y_SLOTS):
            scatter_wait(s)

    @pl.when(i < n_used)
    def _():
        x_slot = lax.rem(i, X_SLOTS)
        y_slot = lax.rem(i, Y_SLOTS)

        @pl.when((i == 0) | (run != run_ref[jnp.maximum(i - 1, 0)]))
        def _():
            for cp, _ in weight_copies(run):
                cp.wait()
            w_slot = lax.rem(run, W_SLOTS)
            wg_b[...] = wg_f[w_slot].astype(BF16)
            wu_b[...] = wu_f[w_slot].astype(BF16)
            wd_b[...] = wd_f[w_slot].astype(BF16)
            for cp, queue in weight_copies(run + W_SLOTS - 1):
                cp.start(priority=queue)

        gather_wait(x_slot)

        @pl.when(i >= Y_SLOTS)
        def _():
            scatter_wait(y_slot)

        x = _tiles_to_rows(xbuf.at[x_slot], MOE_BLOCK).astype(BF16)
        hg = _dot(x, wg_b[...])
        hu = _dot(x, wu_b[...])
        act = (hg * _sigmoid(hg) * hu).astype(BF16)
        y = _dot(act, wd_b[...]) * roww_ref[...]
        _rows_to_tiles(ybuf.at[y_slot], y)
        scatter_rows(dst_ref, y_slot, range(MOE_BLOCK))
        gather_block(tok2_ref, lax.rem(i + 2, X_SLOTS))

        @pl.when(i == n_used - 1)
        def _():
            scatter_wait(y_slot)

            @pl.when(i >= 1)
            def _():
                scatter_wait(1 - y_slot)

            gather_wait(lax.rem(i + 1, X_SLOTS))
            gather_wait(lax.rem(i + 2, X_SLOTS))
            for k in range(W_SLOTS - 1):
                for cp, _ in weight_copies(n_runs + k):
                    cp.wait()


def moe_experts(u_all, row_src, row_dst, row_w, block_run, run_exp, n_used_runs, w_gate, w_up,
                w_down, n_assign):
    d, ff = w_gate.shape[1], w_gate.shape[2]
    n_blocks = row_src.shape[0]
    tile_rows = MOE_BLOCK * ROW_TILES
    smem_blk = lambda off: pl.BlockSpec(
        (1, 1, MOE_BLOCK), lambda i, *_: (jnp.clip(i + off, 0, n_blocks - 1), 0, 0),
        memory_space=pltpu.SMEM)
    hbm = pl.BlockSpec(memory_space=pl.ANY)
    grid_spec = pltpu.PrefetchScalarGridSpec(
        num_scalar_prefetch=3,
        grid=(n_blocks,),
        in_specs=[
            smem_blk(0), smem_blk(1), smem_blk(2), smem_blk(0),
            pl.BlockSpec((MOE_BLOCK, 1), lambda i, *_: (i, 0)),
            hbm, hbm, hbm, hbm,
        ],
        out_specs=hbm,
        scratch_shapes=[
            pltpu.VMEM((X_SLOTS, tile_rows, ROW_LANES), F32),
            pltpu.VMEM((Y_SLOTS, tile_rows, ROW_LANES), F32),
            pltpu.VMEM((W_SLOTS, d, ff), F32), pltpu.VMEM((W_SLOTS, d, ff), F32),
            pltpu.VMEM((W_SLOTS, ff, d), F32),
            pltpu.VMEM((d, ff), BF16), pltpu.VMEM((d, ff), BF16), pltpu.VMEM((ff, d), BF16),
            pltpu.SemaphoreType.DMA((X_SLOTS,)), pltpu.SemaphoreType.DMA((Y_SLOTS,)),
            pltpu.SemaphoreType.DMA((W_SLOTS,)),
        ],
    )
    y_rows = (n_assign + Y_SLOTS * MOE_BLOCK) * ROW_TILES
    return pl.pallas_call(
        _moe_expert_kernel,
        grid_spec=grid_spec,
        out_shape=jax.ShapeDtypeStruct((y_rows, ROW_LANES), F32),
        compiler_params=_cp(("arbitrary",), vmem=MOE_VMEM_LIMIT),
        name="moe_experts",
    )(block_run, run_exp, n_used_runs, row_src, row_src, row_src, row_dst, row_w, u_all,
      w_gate, w_up, w_down)


def _moe_combine_kernel(h_ref, y0_ref, y1_ref, o_ref):
    rows = h_ref.shape[0]
    o_ref[...] = h_ref[...] + (_tiles_to_rows(y0_ref, rows) + _tiles_to_rows(y1_ref, rows))


def moe_combine(h, y_slots, row_off, slot_stride, *, tm):
    m, d = h.shape
    assert row_off % tm == 0 and slot_stride % tm == 0
    off0, off1 = row_off // tm, (row_off + slot_stride) // tm
    return pl.pallas_call(
        _moe_combine_kernel,
        grid=(m // tm,),
        in_specs=[pl.BlockSpec((tm, d), lambda i: (i, 0)),
                  pl.BlockSpec((tm * ROW_TILES, ROW_LANES), lambda i: (i + off0, 0)),
                  pl.BlockSpec((tm * ROW_TILES, ROW_LANES), lambda i: (i + off1, 0))],
        out_specs=pl.BlockSpec((tm, d), lambda i: (i, 0)),
        out_shape=jax.ShapeDtypeStruct((m, d), F32),
        compiler_params=_cp(("parallel",)),
        name="moe_combine",
    )(h, y_slots, y_slots)


def moe_dispatch(e_idx, gates, slot_stride):
    m = e_idx.shape[0]
    a = m * TOP_K
    e_flat = e_idx.reshape(a)
    _, order, gate_bits = lax.sort(
        (e_flat, jnp.arange(a, dtype=jnp.int32),
         lax.bitcast_convert_type(gates.reshape(a), jnp.int32)), num_keys=1, is_stable=True)
    gate_sorted = lax.bitcast_convert_type(gate_bits, F32)
    counts =jnp.sum(e_flat[:, None] == jnp.arange(N_EXPERTS, dtype=jnp.int32)[None, :],
                     axis=0, dtype=jnp.int32)
    pad_counts = (counts + MOE_BLOCK - 1) // MOE_BLOCK * MOE_BLOCK
    starts = jnp.cumsum(counts) - counts
    pad_ends = jnp.cumsum(pad_counts)
    pad_starts = pad_ends - pad_counts
    n_blocks = a // MOE_BLOCK + N_EXPERTS
    p = n_blocks * MOE_BLOCK
    n_used = (pad_ends[-1] // MOE_BLOCK).astype(jnp.int32)
    blk = jnp.arange(n_blocks, dtype=jnp.int32)
    blk_start = jnp.minimum(blk, n_used - 1) * MOE_BLOCK
    block_exp = jnp.minimum(jnp.sum(blk_start[:, None] >= pad_ends[None, :], axis=1),
                            N_EXPERTS - 1).astype(jnp.int32)
    in_exp = blk * MOE_BLOCK - pad_starts[block_exp]
    row_cnt = jnp.where(blk < n_used, jnp.clip(counts[block_exp] - in_exp, 0, MOE_BLOCK), 0)
    lane = jnp.arange(MOE_BLOCK, dtype=jnp.int32)[None, :]
    valid = lane < row_cnt[:, None]
    src = jnp.clip((starts[block_exp] + in_exp)[:, None] + lane, 0, a - 1)
    assign = order[src]
    row_tok = jnp.where(valid, assign // TOP_K, 0)
    pad_dst = TOP_K * slot_stride + (blk % Y_SLOTS)[:, None] * MOE_BLOCK + lane
    row_dst = jnp.where(valid, (assign % TOP_K) * slot_stride + assign // TOP_K, pad_dst)
    row_w = jnp.where(valid, gate_sorted[src], 0.0)
    as_blocks = lambda x: (x * ROW_TILES).astype(jnp.int32).reshape(n_blocks, 1, MOE_BLOCK)
    has_rows = counts > 0
    run_exp = jnp.argsort(~has_rows, stable=True).astype(jnp.int32)
    block_run = (jnp.cumsum(has_rows) - 1)[block_exp].astype(jnp.int32)
    n_used_runs = jnp.stack([n_used, jnp.sum(has_rows, dtype=jnp.int32)])
    return (as_blocks(row_tok), as_blocks(row_dst), row_w.reshape(p, 1), block_run, run_exp,
            n_used_runs)


def rwkv_params(rw_mu, rw_w0, rw_w2, rw_a0, rw_a2, rw_g2, rw_k_k, rw_k_a, rw_r_k):
    c = RWKV_WIDTH
    mu = jnp.pad(rw_mu, (0, RWKV_PROJ_PAD - RWKV_PROJ)).reshape(1, RWKV_PROJ_PAD)
    w_lora = jnp.zeros((LANES, 2 * c), F32)
    w_lora = w_lora.at[0:DECAY_LORA, 0:c].set(rw_w2).at[DECAY_LORA:LANES, c:2 * c].set(rw_a2)
    w_gate = jnp.pad(rw_g2, ((0, GATE_PAD - GATE_LORA), (0, 0)))
    vec = lambda x: x.reshape(1, c)
    return (mu, vec(rw_w0), vec(rw_a0), vec(rw_k_k), vec(rw_k_a), vec(rw_r_k),
            w_lora.astype(BF16), w_gate.astype(BF16))


def _token_tiles(m):
    return (1024, 512) if m % 1024 == 0 else (m, m)


def _dense_front(x2d, wts, tm):
    pa = norm_matmul(x2d, wts['ln1_w'], wts['wt_att'], tm=tm, tn=512, w_transposed=True)
    pr = norm_matmul(x2d, wts['ln1_w'], wts['wt_rw'], tm=tm, tn=RWKV_PROJ_PAD // 3,
                     w_transposed=True)
    return pa, pr


def _dense_back(x2d, att2d, rw2d, wts, xattn_fn, tm):
    h1 = matmul_residual([att2d, rw2d], [wts['w_out_a'], wts['w_out_r']], x2d, tm=tm, tn=512)
    qx = norm_matmul(h1, wts['ln2_w'], wts['xq_w'], tm=tm, tn=XATT_WIDTH)
    ox = xattn_fn(qx)
    return matmul_residual([ox], [wts['xo_w']], h1, tm=tm, tn=512)


def kernel(x_prompt, x_sample, cache_win_k, cache_win_v, state_wkv, state_shift, cache_mem_k, cache_mem_v, mem_prompt, ln1_w, w_in, q_norm_w, k_norm_w, attn_sinks, rw_mu, rw_w0, rw_w2, rw_a0, rw_a2, rw_g2, rw_k_k, rw_k_a, rw_r_k, rw_ln_w, rw_ln_b, w_out, ln2_w, mem_norm_w, xq_w, xkv_w, xq_norm_w, xk_norm_w, xo_w, ln3_w, router_group_w, router_group_b, router_expert_w, router_expert_b, exp_w_gate, exp_w_up, exp_w_down):
    assert w_in.shape[0] == 1, "single-layer stack"
    bp, seq, d = x_prompt.shape
    bs = x_sample.shape[0]
    mp = bp * seq
    c = RWKV_WIDTH

    router_w = jnp.concatenate(
        [router_group_w[0], router_expert_w[0],
         jnp.zeros((d, ROUTER_LANES - N_EXPERT_GROUPS - N_EXPERTS), F32)], axis=1)
    router_hi = router_w.astype(BF16)
    wts = {
        'ln1_w': ln1_w[0], 'ln2_w': ln2_w[0], 'ln3_w': ln3_w[0],
        'wt_att': cast_rows_bf16(w_in[0].T, 0, ATT_PROJ, ATT_PROJ),
        'wt_rw': cast_rows_bf16(w_in[0].T, ATT_PROJ, RWKV_PROJ, RWKV_PROJ_PAD),
        'w_out_a': w_out[0][:ATT_WIDTH].astype(BF16),
        'w_out_r': w_out[0][ATT_WIDTH:].astype(BF16),
        'xq_w': xq_w[0].astype(BF16), 'xo_w': xo_w[0].astype(BF16),
        'router_hi': router_hi,
        'router_lo': (router_w - router_hi.astype(F32)).astype(BF16),
        'router_b': jnp.pad(jnp.concatenate([router_group_b[0], router_expert_b[0]]),
                            (0, ROUTER_LANES - N_EXPERT_GROUPS - N_EXPERTS)).reshape(1, -1),
    }
    rw_par = rwkv_params(rw_mu[0], rw_w0[0], rw_w2[0], rw_a0[0], rw_a2[0], rw_g2[0],
                         rw_k_k[0], rw_k_a[0], rw_r_k[0])

    tm_p, te_p = _token_tiles(mp)
    xp = x_prompt.reshape(mp, d)
    pa, pr = _dense_front(xp, wts, tm_p)
    pa3 = pa.reshape(bp, seq, ATT_PROJ)
    pr3 = pr.reshape(bp, seq, RWKV_PROJ_PAD)
    tabs_p = rope_tables(np.arange(seq))
    att_p, kn_p = swa_prompt(pa3, tabs_p, q_norm_w[0], k_norm_w[0], attn_sinks[0])
    rw_p, wkv_p = rwkv_seq(pr3, jnp.zeros((bp, 1, RWKV_PROJ_PAD), F32), rw_par,
                           rw_ln_w[0], rw_ln_b[0])
    rw_p = rw_p.reshape(mp, c)

    n_mem = mem_prompt.shape[1]
    kv_mem = norm_matmul(mem_prompt.reshape(bp * n_mem, d), mem_norm_w[0],
                         xkv_w[0].astype(BF16), tm=bp * n_mem, tn=512)
    mem_k = head_rms(kv_mem[:, :XATT_WIDTH], xk_norm_w[0])
    mem_v = kv_mem[:, XATT_WIDTH:]
    mem_k3 = mem_k.reshape(bp, n_mem, XATT_WIDTH)
    mem_v3 = mem_v.reshape(bp, n_mem, XATT_WIDTH)

    def xattn_p(qx):
        return xattn_prompt(qx.reshape(bp, seq, XATT_WIDTH), mem_k3, mem_v3,
                            xq_norm_w[0]).reshape(mp, XATT_WIDTH)

    h2_p = _dense_back(xp, att_p.reshape(mp, ATT_WIDTH), rw_p, wts, xattn_p, tm_p)

    tm_s, te_s = _token_tiles(bs)
    xs = x_sample.reshape(bs, d)
    sa, sr = _dense_front(xs, wts, tm_s)
    tabs_s = rope_tables(PAST_LEN + np.arange(1))
    qk_w = jnp.concatenate([jnp.tile(q_norm_w[0], ATT_HEADS),
                            jnp.tile(k_norm_w[0], ATT_KV_HEADS)]).reshape(1, -1)
    qk = qk_norm_rope(sa[:, :ATT_WIDTH + KV_WIDTH], qk_w, tabs_s)
    nbuf = cache_win_k.shape[2]

    def feature_major(cache):
        return jnp.transpose(cache, (0, 1, 3, 4, 2)).reshape(bs, KV_WIDTH, nbuf)

    def position_major(win):
        return jnp.transpose(win.reshape(1, bs, ATT_KV_HEADS, HEAD_DIM, nbuf), (0, 1, 4, 2, 3))

    att_s, win_k, win_v = swa_decode(
        qk[:, :ATT_WIDTH].reshape(bs, ATT_HEADS, HEAD_DIM),
        qk[:, ATT_WIDTH:].reshape(bs, 1, KV_WIDTH),
        sa[:, ATT_WIDTH + KV_WIDTH:].reshape(bs, 1, KV_WIDTH),
        feature_major(cache_win_k), feature_major(cache_win_v), attn_sinks[0])
    shift_prev = jnp.pad(state_shift[0], ((0, 0), (0, RWKV_PROJ_PAD - RWKV_PROJ)))
    r, ld, kp, v, kk, b, bonus, g = rwkv_prep_tok(sr, shift_prev, rw_par)
    y_s, wkv_s = rwkv_step(r, ld, kp, v, kk, b, jnp.transpose(state_wkv, (0, 2, 3, 4, 1)))
    wkv_s = jnp.transpose(wkv_s, (0, 4, 1, 2, 3))
    rw_s = rwkv_post(y_s, bonus, g, rw_ln_w[0], rw_ln_b[0], tm=te_s, y_channel_major=True)
    def xattn_s(qx):
        q_pad = jnp.pad(qx.reshape(bs, XATT_HEADS, XATT_HEAD_DIM), ((0, 0), (0, 4), (0, 0)))
        rows_of = lambda c: c.reshape(bs, n_mem * XATT_HEADS, XATT_HEAD_DIM)
        o = xattn_decode(q_pad, rows_of(cache_mem_k), rows_of(cache_mem_v), xq_norm_w[0])
        return o[:, :XATT_HEADS].reshape(bs, XATT_WIDTH)

    h2_s = _dense_back(xs, att_s.reshape(bs, ATT_WIDTH), rw_s, wts, xattn_s, tm_s)

    m_all = mp + bs
    slot_stride = m_all
    tc = math.gcd(mp, bs, 512)
    u_all, idx_all, gate_all = moe_router(h2_p, h2_s, wts['ln3_w'], wts['router_hi'],
                                          wts['router_lo'], wts['router_b'], tm=tc)
    row_src, row_dst, row_w, block_run, run_exp, n_used_runs = moe_dispatch(
        idx_all[:, :TOP_K], gate_all[:, :TOP_K], slot_stride)
    y_slots = moe_experts(u_all, row_src, row_dst, row_w, block_run, run_exp, n_used_runs,
                          exp_w_gate[0], exp_w_up[0], exp_w_down[0], TOP_K * slot_stride)
    out_p = moe_combine(h2_p, y_slots, 0, slot_stride, tm=tc)
    out_s = moe_combine(h2_s, y_slots, mp, slot_stride, tm=tc)

    win = min(WINDOW, seq)
    kv_shape = (1, bp, win, ATT_KV_HEADS, HEAD_DIM)
    return (
        out_p.reshape(bp, seq, d),
        out_s.reshape(bs, 1, d),
        kn_p[:, seq - win:].reshape(kv_shape),
        pa3[:, seq - win:, ATT_WIDTH + KV_WIDTH:].reshape(kv_shape),
        wkv_p[None],
        pr3[:, seq - 1, :RWKV_PROJ][None],
        mem_k3.reshape(1, bp, n_mem, XATT_HEADS, XATT_HEAD_DIM),
        mem_v3.reshape(1, bp, n_mem, XATT_HEADS, XATT_HEAD_DIM),
        position_major(win_k),
        position_major(win_v),
        wkv_s,
        sr[:, :RWKV_PROJ].reshape(1, bs, RWKV_PROJ),
    )
```

```python
import functools
import math

import numpy as np
import jax
import jax.numpy as jnp
from jax import lax
from jax.experimental import pallas as pl
from jax.experimental.pallas import tpu as pltpu

F32 = jnp.float32
BF16 = jnp.bfloat16

D_MODEL = 2048
HEAD_DIM = 64
ATT_HEADS = 16
ATT_KV_HEADS = 4
ATT_GROUP = ATT_HEADS // ATT_KV_HEADS
ATT_WIDTH = ATT_HEADS * HEAD_DIM
KV_WIDTH = ATT_KV_HEADS * HEAD_DIM
ATT_PROJ = ATT_WIDTH + 2 * KV_WIDTH
WINDOW = 128
ATT_SCALE = HEAD_DIM ** -0.5
ROPE_THETA = 500000.0
ROT_DIM = HEAD_DIM // 4
PAST_LEN = 16384

RWKV_WIDTH = 1024
RWKV_HEAD_DIM = 64
RWKV_HEADS = 16
DECAY_LORA = 64
AAA_LORA = 64
GATE_LORA = 160
RWKV_PROJ = 3 * RWKV_WIDTH + DECAY_LORA + AAA_LORA + GATE_LORA
RWKV_PROJ_PAD = 3456

N_MEM = 256
XATT_HEADS = 4
XATT_HEAD_DIM = 128
XATT_WIDTH = XATT_HEADS * XATT_HEAD_DIM

N_EXPERT_GROUPS = 8
EXPERTS_PER_GROUP = 8
N_EXPERTS = 64
TOP_K = 2
EXPERT_FF = D_MODEL // 4
MOE_BLOCK = 128

RMS_EPS = 1e-6
GN_EPS = 64e-5

LANES = 128
CHUNK = 64
VMEM_LIMIT = 56 * 1024 * 1024
MOE_VMEM_LIMIT = 60 * 1024 * 1024


def _cp(sem, vmem=VMEM_LIMIT):
    return pltpu.CompilerParams(dimension_semantics=sem, vmem_limit_bytes=vmem)


def _rms_rows(x, w):
    ms = jnp.mean(x * x, axis=-1, keepdims=True)
    return x * lax.rsqrt(ms + RMS_EPS) * w


def _split2(x):
    hi = x.astype(BF16)
    lo = (x - hi.astype(F32)).astype(BF16)
    return hi, lo


def _split3(x):
    h1 = x.astype(BF16)
    r1 = x - h1.astype(F32)
    h2 = r1.astype(BF16)
    h3 = (r1 - h2.astype(F32)).astype(BF16)
    return h1, h2, h3


def _dot(a, b):
    return jnp.dot(a, b, preferred_element_type=F32)


def _dot_nt(a, b):
    return lax.dot_general(a, b, (((1,), (1,)), ((), ())), preferred_element_type=F32)


def _group_sum(x, gmat):
    hi, lo = _split2(x)
    return _dot(hi, gmat) + _dot(lo, gmat)


def _head_indicator():
    r = lax.broadcasted_iota(jnp.int32, (LANES, LANES), 0) // HEAD_DIM
    c = lax.broadcasted_iota(jnp.int32, (LANES, LANES), 1) // HEAD_DIM
    return jnp.where(r == c, 1.0, 0.0).astype(BF16)


def _norm_mm_kernel(x_ref, lnw_ref, w_ref, o_ref, xn_ref, *, w_transposed):
    @pl.when(pl.program_id(1) == 0)
    def _():
        xn_ref[...] = _rms_rows(x_ref[...], lnw_ref[...]).astype(BF16)

    o_ref[...] = (_dot_nt if w_transposed else _dot)(xn_ref[...], w_ref[...])


def norm_matmul(x, ln_w, w_bf16, *, tm, tn, w_transposed=False):
    m, k = x.shape
    n = w_bf16.shape[0 if w_transposed else 1]
    assert m % tm == 0 and n % tn == 0
    w_spec = (pl.BlockSpec((tn, k), lambda i, j: (j, 0)) if w_transposed
              else pl.BlockSpec((k, tn), lambda i, j: (0, j)))
    return pl.pallas_call(
        functools.partial(_norm_mm_kernel, w_transposed=w_transposed),
        grid=(m // tm, n // tn),
        in_specs=[
            pl.BlockSpec((tm, k), lambda i, j: (i, 0)),
            pl.BlockSpec((1, k), lambda i, j: (0, 0)),
            w_spec,
        ],
        out_specs=pl.BlockSpec((tm, tn), lambda i, j: (i, j)),
        out_shape=jax.ShapeDtypeStruct((m, n), F32),
        scratch_shapes=[pltpu.VMEM((tm, k), BF16)],
        compiler_params=_cp(("parallel", "arbitrary")),
        name="norm_matmul",
    )(x, ln_w.reshape(1, k), w_bf16)


def _cast_rows_kernel(w_ref, o_ref, *, n_in):
    @pl.when(pl.program_id(0) < n_in)
    def _():
        o_ref[...] = w_ref[...].astype(BF16)

    @pl.when(pl.program_id(0) >= n_in)
    def _():
        o_ref[...] = jnp.zeros_like(o_ref)


def cast_rows_bf16(w, first_row, n_rows, n_rows_out, *, rows=96):
    k = w.shape[1]
    assert first_row % rows == 0 and n_rows % rows == 0 and n_rows_out % rows == 0
    first, n_in = first_row // rows, n_rows // rows
    return pl.pallas_call(
        functools.partial(_cast_rows_kernel, n_in=n_in),
        grid=(n_rows_out // rows,),
        in_specs=[pl.BlockSpec((rows, k), lambda i: (first + jnp.minimum(i, n_in - 1), 0))],
        out_specs=pl.BlockSpec((rows, k), lambda i: (i, 0)),
        out_shape=jax.ShapeDtypeStruct((n_rows_out, k), BF16),
        compiler_params=_cp(("parallel",)),
        name="cast_rows_bf16",
    )(w)


def _mm_res_kernel(*refs, n_lhs):
    a_refs = refs[:n_lhs]
    w_refs = refs[n_lhs:2 * n_lhs]
    res_ref = refs[2 * n_lhs]
    o_ref = refs[2 * n_lhs + 1]
    acc = res_ref[...]
    for a_ref, w_ref in zip(a_refs, w_refs):
        acc = acc + _dot(a_ref[...].astype(BF16), w_ref[...])
    o_ref[...] = acc


def matmul_residual(lhs_list, w_list, res, *, tm, tn):
    m, n = res.shape
    n_lhs = len(lhs_list)
    assert m % tm == 0 and n % tn == 0
    in_specs = [pl.BlockSpec((tm, a.shape[1]), lambda i, j: (i, 0)) for a in lhs_list]
    in_specs += [pl.BlockSpec((w.shape[0], tn), lambda i, j: (0, j)) for w in w_list]
    in_specs += [pl.BlockSpec((tm, tn), lambda i, j: (i, j))]
    return pl.pallas_call(
        functools.partial(_mm_res_kernel, n_lhs=n_lhs),
        grid=(m // tm, n // tn),
        in_specs=in_specs,
        out_specs=pl.BlockSpec((tm, tn), lambda i, j: (i, j)),
        out_shape=jax.ShapeDtypeStruct((m, n), F32),
        compiler_params=_cp(("parallel", "arbitrary")),
        name="matmul_residual",
    )(*lhs_list, *w_list, res)


def rope_tables(pos):
    half = ROT_DIM // 2
    f32 = np.float32
    inv = f32(ROPE_THETA) ** (-np.arange(half, dtype=f32) * f32(2.0) / f32(ROT_DIM))
    ang = (np.asarray(pos, f32)[:, None] * inv[None, :].astype(f32)).astype(f32)
    cos, sin = np.cos(ang.astype(np.float64)).astype(f32), np.sin(ang.astype(np.float64)).astype(f32)
    t = ang.shape[0]
    ones = np.ones((t, HEAD_DIM - ROT_DIM), f32)
    zeros = np.zeros((t, HEAD_DIM - ROT_DIM), f32)
    z8 = np.zeros((t, half), f32)
    cos_t = np.concatenate([cos, cos, ones], axis=1)
    sin_a = np.concatenate([z8, sin, zeros], axis=1)
    sin_b = np.concatenate([-sin, z8, zeros], axis=1)
    return tuple(jnp.asarray(np.concatenate([a, a], axis=1)) for a in (cos_t, sin_a, sin_b))


def _norm_rope_chunk(x, w, cos_t, sin_a, sin_b, gmat):
    ms = _group_sum(x * x, gmat) * (1.0 / HEAD_DIM)
    xn = x * lax.rsqrt(ms + RMS_EPS) * w
    half = ROT_DIM // 2
    return (xn * cos_t + pltpu.roll(xn, half, axis=1) * sin_a
            + pltpu.roll(xn, LANES - half, axis=1) * sin_b)


def _norm_rope(x, w, tabs, gmat):
    chunks = [
        _norm_rope_chunk(x[:, c * LANES:(c + 1) * LANES], w, *tabs, gmat)
        for c in range(x.shape[1] // LANES)
    ]
    return chunks[0] if len(chunks) == 1 else jnp.concatenate(chunks, axis=1)


def _sink_softmax(s, sink):
    m = jnp.maximum(jnp.max(s, axis=-1, keepdims=True), sink)
    e = jnp.exp(s - m)
    return e / (jnp.sum(e, axis=-1, keepdims=True) + jnp.exp(sink - m))


def _swa_prompt_kernel(q_ref, kc_ref, vc_ref, kp_ref, vp_ref, cc_ref, sac_ref, sbc_ref,
                       cp_ref, sap_ref, sbp_ref, qw_ref, kw_ref, sink_ref, o_ref, kn_ref):
    n = pl.program_id(1)
    blk = q_ref.shape[0]
    gmat = _head_indicator()
    tabs_c = (cc_ref[...], sac_ref[...], sbc_ref[...])
    tabs_p = (cp_ref[...], sap_ref[...], sbp_ref[...])
    q = _norm_rope(q_ref[...], qw_ref[...], tabs_c, gmat)
    k_cur = _norm_rope(kc_ref[...], kw_ref[...], tabs_c, gmat)
    k_prev = _norm_rope(kp_ref[...], kw_ref[...], tabs_p, gmat)
    kn_ref[...] = k_cur
    k_all = jnp.concatenate([k_prev, k_cur], axis=0).astype(BF16)
    v_all = jnp.concatenate([vp_ref[...], vc_ref[...]], axis=0).astype(BF16)

    qi = lax.broadcasted_iota(jnp.int32, (blk, 2 * blk), 0) + blk
    si = lax.broadcasted_iota(jnp.int32, (blk, 2 * blk), 1)
    rel = qi - si
    valid = (rel >= 0) & (rel <= WINDOW) & ((n > 0) | (si >= blk))

    groups = range(ATT_KV_HEADS)
    lanes = [slice(kv * HEAD_DIM, (kv + 1) * HEAD_DIM) for kv in groups]
    heads = [[kv * ATT_GROUP + g for g in range(ATT_GROUP)] for kv in groups]
    q_g = [jnp.concatenate([q[:, h * HEAD_DIM:(h + 1) * HEAD_DIM] for h in heads[kv]],
                           axis=0).astype(BF16) for kv in groups]
    s = [_dot_nt(q_g[kv], k_all[:, lanes[kv]]) * ATT_SCALE for kv in groups]
    p = [jnp.concatenate(
        [_sink_softmax(jnp.where(valid, s[kv][g * blk:(g + 1) * blk], -jnp.inf), sink_ref[h])
         for g, h in enumerate(heads[kv])], axis=0).astype(BF16) for kv in groups]
    o = [_dot(p[kv], v_all[:, lanes[kv]]) for kv in groups]
    for kv in groups:
        for g, h in enumerate(heads[kv]):
            o_ref[:, h * HEAD_DIM:(h + 1) * HEAD_DIM] = o[kv][g * blk:(g + 1) * blk]


def swa_prompt(pa, tabs, q_norm_w, k_norm_w, sinks):
    b, t, _ = pa.shape
    blk = WINDOW
    nb = t // blk
    qb, kb, vb = 0, ATT_WIDTH // KV_WIDTH, ATT_WIDTH // KV_WIDTH + 1
    cur = lambda i, n, *_: (i, n, 0)
    tab_cur = pl.BlockSpec((blk, LANES), lambda i, n: (n, 0))
    tab_prev = pl.BlockSpec((blk, LANES), lambda i, n: (jnp.maximum(n - 1, 0), 0))
    qw = jnp.tile(q_norm_w.reshape(1, HEAD_DIM), (1, 2))
    kw = jnp.tile(k_norm_w.reshape(1, HEAD_DIM), (1, 2))
    return pl.pallas_call(
        _swa_prompt_kernel,
        grid=(b, nb),
        in_specs=[
            pl.BlockSpec((None, blk, ATT_WIDTH), lambda i, n: (i, n, qb)),
            pl.BlockSpec((None, blk, KV_WIDTH), lambda i, n: (i, n, kb)),
            pl.BlockSpec((None, blk, KV_WIDTH), lambda i, n: (i, n, vb)),
            pl.BlockSpec((None, blk, KV_WIDTH), lambda i, n: (i, jnp.maximum(n - 1, 0), kb)),
            pl.BlockSpec((None, blk, KV_WIDTH), lambda i, n: (i, jnp.maximum(n - 1, 0), vb)),
            tab_cur, tab_cur, tab_cur, tab_prev, tab_prev, tab_prev,
            pl.BlockSpec((1, LANES), lambda i, n: (0, 0)),
            pl.BlockSpec((1, LANES), lambda i, n: (0, 0)),
            pl.BlockSpec(memory_space=pltpu.SMEM),
        ],
        out_specs=[
            pl.BlockSpec((None, blk, ATT_WIDTH), cur),
            pl.BlockSpec((None, blk, KV_WIDTH), cur),
        ],
        out_shape=[
            jax.ShapeDtypeStruct((b, t, ATT_WIDTH), F32),
            jax.ShapeDtypeStruct((b, t, KV_WIDTH), F32),
        ],
        compiler_params=_cp(("parallel", "arbitrary")),
        name="swa_prompt",
    )(pa, pa, pa, pa, pa, *tabs, *tabs, qw, kw, sinks)


def _qk_norm_rope_kernel(x_ref, w_ref, c_ref, sa_ref, sb_ref, o_ref):
    gmat = _head_indicator()
    tabs = (c_ref[...], sa_ref[...], sb_ref[...])
    for c in range(x_ref.shape[1] // LANES):
        sl = slice(c * LANES, (c + 1) * LANES)
        o_ref[:, sl] = _norm_rope_chunk(x_ref[:, sl], w_ref[:, sl], *tabs, gmat)


def qk_norm_rope(x, w_row, tabs):
    m, w = x.shape
    full = lambda *shape: pl.BlockSpec(shape, lambda: (0,) * len(shape))
    return pl.pallas_call(
        _qk_norm_rope_kernel,
        in_specs=[full(m, w), full(1, w), full(1, LANES), full(1, LANES), full(1, LANES)],
        out_specs=full(m, w),
        out_shape=jax.ShapeDtypeStruct((m, w), F32),
        name="qk_norm_rope",
    )(x, w_row, *tabs)


def _swa_decode_kernel(q_ref, kn_ref, vn_ref, knt_ref, vnt_ref, ck_ref, cv_ref, sink_ref, o_ref,
                       kw_ref, vw_ref):
    bb = q_ref.shape[0]
    nbuf = ck_ref.shape[2]
    row_kv = lax.broadcasted_iota(jnp.int32, (ATT_HEADS, KV_WIDTH), 0) // ATT_GROUP
    lane_kv = lax.broadcasted_iota(jnp.int32, (ATT_HEADS, KV_WIDTH), 1) // HEAD_DIM
    own = row_kv == lane_kv
    sink = sink_ref[...]
    seq = lax.broadcasted_iota(jnp.int32, knt_ref.shape, 1)
    pos = lax.broadcasted_iota(jnp.int32, (KV_WIDTH, nbuf), 1)
    b0 = pl.program_id(0) * bb
    for b in range(bb):
        q2 = q_ref[b]
        q_exp = jnp.where(own, jnp.concatenate([q2] * ATT_KV_HEADS, axis=1), 0.0)
        k_new, v_new = kn_ref[b], vn_ref[b]
        k_buf, v_buf = ck_ref[b], cv_ref[b]
        s_buf = _dot(q_exp.astype(BF16), k_buf.astype(BF16)) * ATT_SCALE
        s_new = jnp.sum(q_exp * k_new, axis=-1, keepdims=True) * ATT_SCALE
        m = jnp.maximum(jnp.maximum(jnp.max(s_buf, axis=-1, keepdims=True), s_new), sink)
        e_buf = jnp.exp(s_buf - m)
        e_new = jnp.exp(s_new - m)
        inv = 1.0 / (jnp.sum(e_buf, axis=-1, keepdims=True) + e_new + jnp.exp(sink - m))
        o = _dot_nt((e_buf * inv).astype(BF16), v_buf.astype(BF16)) + (e_new * inv) * v_new
        o = jnp.where(own, o, 0.0)
        o_ref[b] = (o[:, 0:HEAD_DIM] + o[:, HEAD_DIM:2 * HEAD_DIM]
                    + o[:, 2 * HEAD_DIM:3 * HEAD_DIM] + o[:, 3 * HEAD_DIM:4 * HEAD_DIM])
        k_col = jnp.sum(jnp.where(seq == b0 + b, knt_ref[...], 0.0), axis=1, keepdims=True)
        v_col = jnp.sum(jnp.where(seq == b0 + b, vnt_ref[...], 0.0), axis=1, keepdims=True)
        kw_ref[b] = jnp.where(pos == nbuf - 1, k_col, pltpu.roll(k_buf, nbuf - 1, axis=1))
        vw_ref[b] = jnp.where(pos == nbuf - 1, v_col, pltpu.roll(v_buf, nbuf - 1, axis=1))


def swa_decode(q, k_new, v_new, cache_kt, cache_vt, sinks, *, bb=8):
    b, _, nbuf = cache_kt.shape
    blk3 = lambda s1, s2: pl.BlockSpec((bb, s1, s2), lambda i: (i, 0, 0))
    whole = pl.BlockSpec((KV_WIDTH, b), lambda i: (0, 0))
    return pl.pallas_call(
        _swa_decode_kernel,
        grid=(b // bb,),
        in_specs=[
            blk3(ATT_HEADS, HEAD_DIM), blk3(1, KV_WIDTH), blk3(1, KV_WIDTH), whole, whole,
            blk3(KV_WIDTH, nbuf), blk3(KV_WIDTH, nbuf),
            pl.BlockSpec((ATT_HEADS, 1), lambda i: (0, 0)),
        ],
        out_specs=[blk3(ATT_HEADS, HEAD_DIM), blk3(KV_WIDTH, nbuf), blk3(KV_WIDTH, nbuf)],
        out_shape=[
            jax.ShapeDtypeStruct((b, ATT_HEADS, HEAD_DIM), F32),
            jax.ShapeDtypeStruct((b, KV_WIDTH, nbuf), F32),
            jax.ShapeDtypeStruct((b, KV_WIDTH, nbuf), F32),
        ],
        compiler_params=_cp(("parallel",)),
        name="swa_decode",
    )(q, k_new, v_new, k_new.reshape(b, KV_WIDTH).T, v_new.reshape(b, KV_WIDTH).T,
      cache_kt, cache_vt, sinks.reshape(ATT_HEADS, 1))


def _head_rms_kernel(x_ref, w_ref, o_ref):
    for h in range(x_ref.shape[1] // XATT_HEAD_DIM):
        sl = slice(h * XATT_HEAD_DIM, (h + 1) * XATT_HEAD_DIM)
        o_ref[:, sl] = _rms_rows(x_ref[:, sl], w_ref[...])


def head_rms(x, w):
    m, wd = x.shape
    return pl.pallas_call(
        _head_rms_kernel,
        in_specs=[pl.BlockSpec((m, wd), lambda: (0, 0)),
                  pl.BlockSpec((1, XATT_HEAD_DIM), lambda: (0, 0))],
        out_specs=pl.BlockSpec((m, wd), lambda: (0, 0)),
        out_shape=jax.ShapeDtypeStruct((m, wd), F32),
        name="head_rms",
    )(x, w.reshape(1, XATT_HEAD_DIM))


def _xattn_prompt_kernel(q_ref, k_ref, v_ref, w_ref, o_ref):
    scale = 1.0 / math.sqrt(XATT_HEAD_DIM)
    for h in range(XATT_HEADS):
        sl = slice(h * XATT_HEAD_DIM, (h + 1) * XATT_HEAD_DIM)
        qn = _rms_rows(q_ref[:, sl], w_ref[...]).astype(BF16)
        s = _dot_nt(qn, k_ref[:, sl].astype(BF16)) * scale
        e = jnp.exp(s - jnp.max(s, axis=-1, keepdims=True))
        p = e / jnp.sum(e, axis=-1, keepdims=True)
        o_ref[:, sl] = _dot(p.astype(BF16), v_ref[:, sl].astype(BF16))


def xattn_prompt(q, mem_k, mem_v, xq_norm_w, *, tq=512):
    b, t, w = q.shape
    n_mem = mem_k.shape[1]
    return pl.pallas_call(
        _xattn_prompt_kernel,
        grid=(b, t // tq),
        in_specs=[
            pl.BlockSpec((None, tq, w), lambda i, j: (i, j, 0)),
            pl.BlockSpec((None, n_mem, w), lambda i, j: (i, 0, 0)),
            pl.BlockSpec((None, n_mem, w), lambda i, j: (i, 0, 0)),
            pl.BlockSpec((1, XATT_HEAD_DIM), lambda i, j: (0, 0)),
        ],
        out_specs=pl.BlockSpec((None, tq, w), lambda i, j: (i, j, 0)),
        out_shape=jax.ShapeDtypeStruct((b, t, w), F32),
        compiler_params=_cp(("parallel", "arbitrary")),
        name="xattn_prompt",
    )(q, mem_k, mem_v, xq_norm_w.reshape(1, XATT_HEAD_DIM))


def _xattn_decode_kernel(q_ref, k_ref, v_ref, w_ref, o_ref):
    bb, rows, _ = q_ref.shape
    n_keys = k_ref.shape[1]
    scale = 1.0 / math.sqrt(XATT_HEAD_DIM)
    own = (lax.broadcasted_iota(jnp.int32, (rows, n_keys), 1) % XATT_HEADS
           == lax.broadcasted_iota(jnp.int32, (rows, n_keys), 0) % XATT_HEADS)
    seqs = range(bb)
    qn = [_rms_rows(q_ref[b], w_ref[...]).astype(BF16) for b in seqs]
    s = [jnp.where(own, _dot_nt(qn[b], k_ref[b].astype(BF16)) * scale, -jnp.inf) for b in seqs]
    e = [jnp.exp(s[b] - jnp.max(s[b], axis=-1, keepdims=True)) for b in seqs]
    p = [(e[b] / jnp.sum(e[b], axis=-1, keepdims=True)).astype(BF16) for b in seqs]
    for b in seqs:
        o_ref[b] = _dot(p[b], v_ref[b].astype(BF16))


def xattn_decode(q_pad, mem_k, mem_v, xq_norm_w, *, bb=8):
    b, rows, _ = q_pad.shape
    n_keys = mem_k.shape[1]
    kv = pl.BlockSpec((bb, n_keys, XATT_HEAD_DIM), lambda i: (i, 0, 0))
    return pl.pallas_call(
        _xattn_decode_kernel,
        grid=(b // bb,),
        in_specs=[pl.BlockSpec((bb, rows, XATT_HEAD_DIM), lambda i: (i, 0, 0)), kv, kv,
                  pl.BlockSpec((1, XATT_HEAD_DIM), lambda i: (0, 0))],
        out_specs=pl.BlockSpec((bb, rows, XATT_HEAD_DIM), lambda i: (i, 0, 0)),
        out_shape=jax.ShapeDtypeStruct((b, rows, XATT_HEAD_DIM), F32),
        compiler_params=_cp(("parallel",)),
        name="xattn_decode",
    )(q_pad, mem_k, mem_v, xq_norm_w.reshape(1, XATT_HEAD_DIM))


LORA_OFF = 3 * RWKV_WIDTH
GATE_OFF = LORA_OFF + DECAY_LORA + AAA_LORA
GATE_PAD = RWKV_PROJ_PAD - GATE_OFF


def _sigmoid(x):
    return 1.0 / (1.0 + jnp.exp(-x))


def _per_chunk(fn, *arrays):
    w = arrays[0].shape[1]
    outs = [fn(*(a[:, c * LANES:(c + 1) * LANES] for a in arrays)) for c in range(w // LANES)]
    return jnp.concatenate(outs, axis=1)


def _rwkv_prep_core(pr, prev, mu, w0, a0, kk_w, ka_w, rk_w, w_lora, w_gate):
    c = RWKV_WIDTH
    gmat = _head_indicator()
    xm = pr + (prev - pr) * mu
    r, k, v = xm[:, 0:c], xm[:, c:2 * c], xm[:, 2 * c:3 * c]
    lora = xm[:, LORA_OFF:LORA_OFF + LANES]
    lane = lax.broadcasted_iota(jnp.int32, lora.shape, 1)
    lora_in = jnp.where(lane < DECAY_LORA, jnp.tanh(lora), lora)
    wa = _dot(lora_in.astype(BF16), w_lora)
    z = -(w0 + wa[:, 0:c])
    softplus = jnp.maximum(z, 0.0) + jnp.log(1.0 + jnp.exp(-jnp.abs(z)))
    log_decay = -jnp.exp(-softplus - 0.5)
    a = _sigmoid(a0 + wa[:, c:2 * c])
    g = _dot(_sigmoid(xm[:, GATE_OFF:GATE_OFF + GATE_PAD]).astype(BF16), w_gate)
    kk = k * kk_w
    norm = jnp.sqrt(_per_chunk(lambda t: _group_sum(t * t, gmat), kk))
    kk = kk / jnp.maximum(norm, 1e-12)
    kp = k * (1.0 + (a - 1.0) * ka_w)
    bonus = _per_chunk(lambda t: _group_sum(t, gmat), r * kp * rk_w) * v
    return r, log_decay, kp, v, kk, kk * a, bonus, g


def _rwkv_prep_tok_kernel(pr_ref, prev_ref, mu_ref, w0_ref, a0_ref, kkw_ref, kaw_ref, rkw_ref,
                          wl_ref, wg_ref, *out_refs):
    outs = _rwkv_prep_core(pr_ref[...], prev_ref[...], mu_ref[...], w0_ref[...], a0_ref[...],
                           kkw_ref[...], kaw_ref[...], rkw_ref[...], wl_ref[...], wg_ref[...])
    for k, (o_ref, o) in enumerate(zip(out_refs, outs)):
        o_ref[...] = o.T if k < N_STEP_VECS else o


def _rwkv_param_specs(index_map):
    c = RWKV_WIDTH
    shapes = [(1, RWKV_PROJ_PAD)] + [(1, c)] * 5 + [(LANES, 2 * c), (GATE_PAD, c)]
    return [pl.BlockSpec(s, index_map) for s in shapes]


N_STEP_VECS = 6


def rwkv_prep_tok(pr, prev, params):
    m, wd = pr.shape
    c = RWKV_WIDTH
    shapes = [(c, m)] * N_STEP_VECS + [(m, c)] * 2
    return pl.pallas_call(
        _rwkv_prep_tok_kernel,
        grid=(1,),
        in_specs=[pl.BlockSpec((m, wd), lambda i: (0, 0))] * 2
        + _rwkv_param_specs(lambda i: (0, 0)),
        out_specs=[pl.BlockSpec(s, lambda i: (0, 0)) for s in shapes],
        out_shape=[jax.ShapeDtypeStruct(s, F32) for s in shapes],
        compiler_params=_cp(("arbitrary",)),
        name="rwkv_prep_tok",
    )(pr, prev, *params)


def _dot_tn(a, b):
    return lax.dot_general(a, b, (((0,), (0,)), ((), ())), preferred_element_type=F32)


def _rwkv_scan_tile(r_all, ld_all, kp_all, v_all, kk_all, b_all, s_cur, y_ref):
    n = CHUNK
    nc = r_all.shape[0] // n
    ti = lax.broadcasted_iota(jnp.int32, (n, n), 0)
    si = lax.broadcasted_iota(jnp.int32, (n, n), 1)
    tri = jnp.where(si <= ti, 1.0, 0.0).astype(BF16)
    t2 = lax.broadcasted_iota(jnp.int32, (n, LANES), 0)
    lane2 = lax.broadcasted_iota(jnp.int32, (n, LANES), 1)
    s2 = lane2 % n
    low = lane2 < n
    strict2, incl2, eye2 = s2 < t2, s2 <= t2, s2 == t2
    low4 = lax.broadcasted_iota(jnp.int32, (2 * n, LANES), 1) < n
    top4 = lax.broadcasted_iota(jnp.int32, (2 * n, LANES), 0) < n
    diag_blk = top4 == low4
    own4 = {0: low4, 1: ~low4}
    pairs = range(RWKV_HEADS // 2)
    cs = [slice(c * LANES, (c + 1) * LANES) for c in pairs]

    pre = []
    for j in range(nc):
        rows = slice(j * n, (j + 1) * n)
        ld = ld_all[rows, :]
        l1, l2, l3 = _split3(ld)
        lc = _dot(tri, l1) + _dot(tri, l2) + _dot(tri, l3)
        lc_end = lc[n - 1:n, :]
        e_neg = jnp.exp(-lc)
        kk, b, kp = kk_all[rows, :], b_all[rows, :], kp_all[rows, :]
        to_end = jnp.exp(lc_end - lc)
        pre.append(dict(
            a_t=(-kk * jnp.exp(lc - ld)).astype(BF16), b_t=(b * e_neg).astype(BF16),
            k_t=(kp * e_neg).astype(BF16), r_t=(r_all[rows, :] * jnp.exp(lc)).astype(BF16),
            b_e=(b * to_end).astype(BF16), k_e=(kp * to_end).astype(BF16),
            v_b=v_all[rows, :].astype(BF16), g_end=jnp.exp(lc_end)))
    items = [(j, c, par) for j in range(nc) for c in pairs for par in (0, 1)]
    at = {it: i for i, it in enumerate(items)}
    ar = {(j, c): jnp.concatenate([pre[j]['a_t'][:, cs[c]], pre[j]['r_t'][:, cs[c]]], axis=0)
          for j in range(nc) for c in pairs}
    kb = {(j, c): jnp.concatenate([pre[j]['k_t'][:, cs[c]], pre[j]['b_t'][:, cs[c]]], axis=0)
          for j in range(nc) for c in pairs}
    gm = [_dot_nt(jnp.where(own4[par], ar[j, c], 0).astype(BF16), kb[j, c]) for j, c, par in items]
    top = [g[0:n] for g in gm]
    pm = [jnp.where(incl2, g[n:2 * n], 0.0).astype(BF16) for g in gm]
    lak = [jnp.where(strict2 & low, t, 0.0).astype(BF16) for t in top]
    lv = [_dot(lak[i][:, 0:n], pre[j]['v_b'][:, cs[c]]) for i, (j, c, par) in enumerate(items)]
    z = [jnp.where(low, jnp.where(eye2, 1.0, 0.0), jnp.where(strict2, t, 0.0)) for t in top]
    for _ in range(6):
        zb = [zz.astype(BF16) for zz in z]
        res = [_dot(jnp.where(low, 0, zb[i]).astype(BF16), jnp.concatenate([zb[i], zb[i]], axis=0))
               for i in range(len(items))]
        z = [res[i] + jnp.where(low, z[i], 0.0) for i in range(len(items))]
    tmat = [zz[:, 0:n].astype(BF16) for zz in z]

    for j in range(nc):
        p = pre[j]
        sw = [_dot_nt(ar[j, c], s_cur[c].astype(BF16)) for c in pairs]
        w0 = [(sw[c][0:n] + jnp.where(low, lv[at[j, c, 0]], lv[at[j, c, 1]])).astype(BF16)
              for c in pairs]
        u = [jnp.where(low, _dot(tmat[at[j, c, 0]], w0[c]),
                       _dot(tmat[at[j, c, 1]], w0[c])).astype(BF16) for c in pairs]
        vu = [jnp.concatenate([p['v_b'][:, cs[c]], u[c]], axis=0) for c in pairs]
        yb = [jnp.where(low, _dot(pm[at[j, c, 0]], vu[c]), _dot(pm[at[j, c, 1]], vu[c]))
              for c in pairs]
        for c in pairs:
            y_ref[j * n:(j + 1) * n, cs[c]] = sw[c][n:2 * n] + yb[c]
        upd = [_dot_tn(jnp.concatenate([u[c], p['v_b'][:, cs[c]]], axis=0),
                       jnp.concatenate([p['b_e'][:, cs[c]], p['k_e'][:, cs[c]]], axis=0))
               for c in pairs]
        s_cur = [s_cur[c] * p['g_end'][:, cs[c]] + jnp.where(diag_blk, upd[c], 0.0) for c in pairs]
    return s_cur


def _group_norm_gate(y, bonus, g, ln_w, ln_b, gmat):
    inv = 1.0 / RWKV_HEAD_DIM
    d = y - _group_sum(y, gmat) * inv
    var = _group_sum(d * d, gmat) * inv
    return (d * lax.rsqrt(var + GN_EPS) * ln_w + ln_b + bonus) * g


def _rwkv_seq_kernel(pr_ref, prev0_ref, mu_ref, w0_ref, a0_ref, kkw_ref, kaw_ref, rkw_ref,
                     wl_ref, wg_ref, lnw_ref, lnb_ref, o_ref, s_out_ref, s_ref, last_ref, y_ref):
    @pl.when(pl.program_id(1) == 0)
    def _():
        s_ref[...] = jnp.zeros_like(s_ref)
        last_ref[...] = prev0_ref[...]

    pr = pr_ref[...]
    rows = pr.shape[0]
    row = lax.broadcasted_iota(jnp.int32, (rows, 1), 0)
    prev = jnp.where(row == 0, last_ref[...], pltpu.roll(pr, 1, axis=0))
    last_ref[...] = pr[rows - 1:rows, :]
    r, ld, kp, v, kk, b, bonus, g = _rwkv_prep_core(
        pr, prev, mu_ref[...], w0_ref[...], a0_ref[...], kkw_ref[...], kaw_ref[...],
        rkw_ref[...], wl_ref[...], wg_ref[...])
    pairs = range(RWKV_HEADS // 2)
    s_new = _rwkv_scan_tile(r, ld, kp, v, kk, b, [s_ref[c] for c in pairs], y_ref)
    n = RWKV_HEAD_DIM
    for c in pairs:
        s_ref[c] = s_new[c]
        s_out_ref[2 * c] = s_new[c][0:n, 0:n]
        s_out_ref[2 * c + 1] = s_new[c][n:2 * n, n:2 * n]
    gmat = _head_indicator()
    for c in pairs:
        sl = slice(c * LANES, (c + 1) * LANES)
        o_ref[:, sl] = _group_norm_gate(y_ref[:, sl], bonus[:, sl], g[:, sl], lnw_ref[:, sl],
                                        lnb_ref[:, sl], gmat)


SCAN_CHUNKS = 4


def rwkv_seq(pr, prev0, params, ln_w, ln_b):
    bsz, t, wd = pr.shape
    c = RWKV_WIDTH
    rows = SCAN_CHUNKS * CHUNK
    assert t % rows == 0
    vec = pl.BlockSpec((1, c), lambda i, j: (0, 0))
    st = pl.BlockSpec((None, RWKV_HEADS, RWKV_HEAD_DIM, RWKV_HEAD_DIM), lambda i, j: (i, 0, 0, 0))
    return pl.pallas_call(
        _rwkv_seq_kernel,
        grid=(bsz, t // rows),
        in_specs=[pl.BlockSpec((None, rows, wd), lambda i, j: (i, j, 0)),
                  pl.BlockSpec((None, 1, wd), lambda i, j: (i, 0, 0))]
        + _rwkv_param_specs(lambda i, j: (0, 0)) + [vec, vec],
        out_specs=[pl.BlockSpec((None, rows, c), lambda i, j: (i, j, 0)), st],
        out_shape=[jax.ShapeDtypeStruct((bsz, t, c), F32),
                   jax.ShapeDtypeStruct((bsz, RWKV_HEADS, RWKV_HEAD_DIM, RWKV_HEAD_DIM), F32)],
        scratch_shapes=[pltpu.VMEM((RWKV_HEADS // 2, LANES, LANES), F32),
                        pltpu.VMEM((1, wd), F32), pltpu.VMEM((rows, c), F32)],
        compiler_params=_cp(("parallel", "arbitrary")),
        name="rwkv_seq",
    )(pr, prev0, *params, ln_w.reshape(1, c), ln_b.reshape(1, c))


STEP_UNROLL = 8


def _rwkv_step_kernel(r_ref, ld_ref, kp_ref, v_ref, kk_ref, b_ref, s_ref, y_ref, s_out_ref):
    n = RWKV_HEAD_DIM
    neg_kk, decay = -kk_ref[...], jnp.exp(ld_ref[...])
    b_mat, kp_mat, r_mat = b_ref[...], kp_ref[...], r_ref[...]

    def body(i, carry):
        v0 = pl.multiple_of(i * STEP_UNROLL, STEP_UNROLL)
        v_rows = v_ref[pl.ds(v0, STEP_UNROLL), :]
        rows = range(STEP_UNROLL)
        s = [s_ref[v0 + j] for j in rows]
        sa = [jnp.sum(s[j] * neg_kk, axis=0, keepdims=True) for j in rows]
        s_new = [s[j] * decay + sa[j] * b_mat + v_rows[j:j + 1, :] * kp_mat for j in rows]
        y = [jnp.sum(s_new[j] * r_mat, axis=0, keepdims=True) for j in rows]
        for j in rows:
            s_out_ref[v0 + j] = s_new[j]
        y_ref[pl.ds(v0, STEP_UNROLL), :] = jnp.concatenate(y, axis=0)
        return carry

    lax.fori_loop(0, n // STEP_UNROLL, body, 0)


def rwkv_step(r, ld, kp, v, kk, b, state_t):
    _, nh, n, _, bsz = state_t.shape
    vec = pl.BlockSpec((n, bsz), lambda h: (h, 0))
    st = pl.BlockSpec((None, None, n, n, bsz), lambda h: (0, h, 0, 0, 0))
    return pl.pallas_call(
        _rwkv_step_kernel,
        grid=(nh,),
        in_specs=[vec] * 6 + [st],
        out_specs=[vec, st],
        out_shape=[jax.ShapeDtypeStruct((nh * n, bsz), F32),
                   jax.ShapeDtypeStruct(state_t.shape, F32)],
        compiler_params=_cp(("parallel",)),
        name="rwkv_step",
    )(r, ld, kp, v, kk, b, state_t)


def _rwkv_post_kernel(y_ref, bonus_ref, g_ref, lnw_ref, lnb_ref, o_ref, *, y_channel_major):
    gmat = _head_indicator()
    for c in range(o_ref.shape[1] // LANES):
        sl = slice(c * LANES, (c + 1) * LANES)
        y = y_ref[sl, :].T if y_channel_major else y_ref[:, sl]
        o_ref[:, sl] = _group_norm_gate(y, bonus_ref[:, sl], g_ref[:, sl], lnw_ref[:, sl],
                                        lnb_ref[:, sl], gmat)


def rwkv_post(y, bonus, g, ln_w, ln_b, *, tm, y_channel_major=False):
    m, c = bonus.shape
    blk = pl.BlockSpec((tm, c), lambda i: (i, 0))
    y_blk = pl.BlockSpec((c, tm), lambda i: (0, i)) if y_channel_major else blk
    vec = pl.BlockSpec((1, c), lambda i: (0, 0))
    return pl.pallas_call(
        functools.partial(_rwkv_post_kernel, y_channel_major=y_channel_major),
        grid=(m // tm,),
        in_specs=[y_blk, blk, blk, vec, vec],
        out_specs=blk,
        out_shape=jax.ShapeDtypeStruct((m, c), F32),
        compiler_params=_cp(("parallel",)),
        name="rwkv_post",
    )(y, bonus, g, ln_w.reshape(1, c), ln_b.reshape(1, c))


ROUTER_LANES = LANES
ROW_TILES = 1
ROW_LANES = D_MODEL // ROW_TILES


def _rows_to_tiles(ref, x):
    rows = x.shape[0]
    if ROW_TILES == 1:
        ref[...] = x
        return
    for j in range(ROW_TILES):
        ref[pl.ds(j, rows, stride=ROW_TILES), :] = x[:, j * ROW_LANES:(j + 1) * ROW_LANES]


def _tiles_to_rows(ref, rows):
    if ROW_TILES == 1:
        return ref[...]
    return jnp.concatenate(
        [ref[pl.ds(j, rows, stride=ROW_TILES), :] for j in range(ROW_TILES)], axis=1)


def _router_kernel(ha_ref, hb_ref, lnw_ref, whi_ref, wlo_ref, bias_ref, u_ref, idx_ref, gate_ref,
                   *, steps_a):
    use_a = pl.program_id(0) < steps_a
    h = jnp.where(use_a, ha_ref[...], hb_ref[...])
    u = _rms_rows(h, lnw_ref[...])
    _rows_to_tiles(u_ref, u)
    u_hi, u_lo = _split2(u)
    w_hi = whi_ref[...]
    logits = _dot(u_hi, w_hi) + _dot(u_lo, w_hi) + _dot(u_hi, wlo_ref[...]) + bias_ref[...]
    lane = lax.broadcasted_iota(jnp.int32, logits.shape, 1)
    neg = -jnp.inf

    def first_max(x):
        m = jnp.max(x, axis=1, keepdims=True)
        return m, jnp.min(jnp.where(x == m, lane, ROUTER_LANES), axis=1, keepdims=True)

    gl = jnp.where(lane < N_EXPERT_GROUPS, logits, neg)
    g_max, g_idx = first_max(gl)
    g_gate = 1.0 / jnp.sum(jnp.exp(gl - g_max), axis=1, keepdims=True)
    lo = N_EXPERT_GROUPS + g_idx * EXPERTS_PER_GROUP
    el = jnp.where((lane >= lo) & (lane < lo + EXPERTS_PER_GROUP), logits, neg)
    v1, i1 = first_max(el)
    v2, i2 = first_max(jnp.where(lane == i1, neg, el))
    e2 = jnp.exp(v2 - v1)
    w1 = g_gate / (1.0 + e2)
    w2 = g_gate * e2 / (1.0 + e2)
    idx_ref[...] = jnp.where(lane == 0, i1 - N_EXPERT_GROUPS,
                             jnp.where(lane == 1, i2 - N_EXPERT_GROUPS, 0))
    gate_ref[...] = jnp.where(lane == 0, w1, jnp.where(lane == 1, w2, 0.0))


def moe_router(h_a, h_b, ln_w, w_hi, w_lo, bias, *, tm):
    (ma, d), mb = h_a.shape, h_b.shape[0]
    assert ma % tm == 0 and mb % tm == 0
    steps_a, steps_b = ma // tm, mb // tm
    m = ma + mb
    const = lambda r, w: pl.BlockSpec((r, w), lambda i: (0, 0))
    row = lambda w: pl.BlockSpec((tm, w), lambda i: (i, 0))
    return pl.pallas_call(
        functools.partial(_router_kernel, steps_a=steps_a),
        grid=(steps_a + steps_b,),
        in_specs=[pl.BlockSpec((tm, d), lambda i: (jnp.minimum(i, steps_a - 1), 0)),
                  pl.BlockSpec((tm, d), lambda i: (jnp.maximum(i - steps_a, 0), 0)),
                  const(1, d), const(d, ROUTER_LANES), const(d, ROUTER_LANES),
                  const(1, ROUTER_LANES)],
        out_specs=[pl.BlockSpec((tm * ROW_TILES, ROW_LANES), lambda i: (i, 0)),
                   row(ROUTER_LANES), row(ROUTER_LANES)],
        out_shape=[jax.ShapeDtypeStruct((m * ROW_TILES, ROW_LANES), F32),
                   jax.ShapeDtypeStruct((m, ROUTER_LANES), jnp.int32),
                   jax.ShapeDtypeStruct((m, ROUTER_LANES), F32)],
        compiler_params=_cp(("arbitrary",)),
        name="moe_router",
    )(h_a, h_b, ln_w.reshape(1, d), w_hi, w_lo, bias)


X_SLOTS = 3
Y_SLOTS = 2


W_SLOTS = 3
DMA_QUEUES = 2


def _moe_expert_kernel(run_ref, rexp_ref, nused_ref, tok0_ref, tok1_ref, tok2_ref, dst_ref,
                       roww_ref, u_hbm, wg_hbm, wu_hbm, wd_hbm, y_hbm, xbuf, ybuf, wg_f, wu_f,
                       wd_f, wg_b, wu_b, wd_b, sem_in, sem_out, sem_w):
    i = pl.program_id(0)
    n_used, n_runs = nused_ref[0], nused_ref[1]
    tile_rows = MOE_BLOCK * ROW_TILES
    pad_base = y_hbm.shape[0] - Y_SLOTS * tile_rows
    run = run_ref[i]

    def weight_copies(k):
        e, s = rexp_ref[jnp.minimum(k, n_runs - 1)], lax.rem(k, W_SLOTS)
        copies = []
        for hbm, buf in ((wg_hbm, wg_f), (wu_hbm, wu_f), (wd_hbm, wd_f)):
            rows = buf.shape[1] // 2
            for part in range(2):
                sl = pl.ds(part * rows, rows)
                copies.append((pltpu.make_async_copy(hbm.at[e, sl], buf.at[s, sl], sem_w.at[s]),
                               part))
        return copies

    def gather_block(idx_ref, x_slot):
        for r in range(MOE_BLOCK):
            pltpu.make_async_copy(u_hbm.at[pl.ds(idx_ref[0, 0, r], ROW_TILES)],
                                  xbuf.at[x_slot, pl.ds(r * ROW_TILES, ROW_TILES)],
                                  sem_in.at[x_slot]).start(priority=r % DMA_QUEUES)

    def scatter_rows(idx_ref, y_slot, rows):
        for r in rows:
            pltpu.make_async_copy(ybuf.at[y_slot, pl.ds(r * ROW_TILES, ROW_TILES)],
                                  y_hbm.at[pl.ds(idx_ref[0, 0, r], ROW_TILES)],
                                  sem_out.at[y_slot]).start(priority=r % DMA_QUEUES)

    def gather_wait(x_slot):
        pltpu.make_async_copy(u_hbm.at[pl.ds(0, tile_rows)], xbuf.at[x_slot],
                              sem_in.at[x_slot]).wait()

    def scatter_wait(y_slot):
        pltpu.make_async_copy(ybuf.at[y_slot], y_hbm.at[pl.ds(0, tile_rows)],
                              sem_out.at[y_slot]).wait()

    @pl.when(i == 0)
    def _():
        ybuf[0] = jnp.zeros(ybuf.shape[1:], F32)
        for s in range(Y_SLOTS):
            pltpu.make_async_copy(ybuf.at[0], y_hbm.at[pl.ds(pad_base + s * tile_rows, tile_rows)],
                                  sem_out.at[s]).start()
        for k in range(W_SLOTS - 1):
            for cp, queue in weight_copies(k):
                cp.start(priority=queue)
        gather_block(tok0_ref, 0)
        gather_block(tok1_ref, 1)
        for s in range(Y_SLOTS):
            scatter_wait(s)

    @pl.when(i < n_used)
    def _():
        x_slot = lax.rem(i, X_SLOTS)
        y_slot = lax.rem(i, Y_SLOTS)

        @pl.when((i == 0) | (run != run_ref[jnp.maximum(i - 1, 0)]))
        def _():
            for cp, _ in weight_copies(run):
                cp.wait()
            w_slot = lax.rem(run, W_SLOTS)
            wg_b[...] = wg_f[w_slot].astype(BF16)
            wu_b[...] = wu_f[w_slot].astype(BF16)
            wd_b[...] = wd_f[w_slot].astype(BF16)
            for cp, queue in weight_copies(run + W_SLOTS - 1):
                cp.start(priority=queue)

        gather_wait(x_slot)

        @pl.when(i >= Y_SLOTS)
        def _():
            scatter_wait(y_slot)

        x = _tiles_to_rows(xbuf.at[x_slot], MOE_BLOCK).astype(BF16)
        hg = _dot(x, wg_b[...])
        hu = _dot(x, wu_b[...])
        act = (hg * _sigmoid(hg) * hu).astype(BF16)
        y = _dot(act, wd_b[...]) * roww_ref[...]
        _rows_to_tiles(ybuf.at[y_slot], y)
        scatter_rows(dst_ref, y_slot, range(MOE_BLOCK))
        gather_block(tok2_ref, lax.rem(i + 2, X_SLOTS))

        @pl.when(i == n_used - 1)
        def _():
            scatter_wait(y_slot)

            @pl.when(i >= 1)
            def _():
                scatter_wait(1 - y_slot)

            gather_wait(lax.rem(i + 1, X_SLOTS))
            gather_wait(lax.rem(i + 2, X_SLOTS))
            for k in range(W_SLOTS - 1):
                for cp, _ in weight_copies(n_runs + k):
                    cp.wait()


def moe_experts(u_all, row_src, row_dst, row_w, block_run, run_exp, n_used_runs, w_gate, w_up,
                w_down, n_assign):
    d, ff = w_gate.shape[1], w_gate.shape[2]
    n_blocks = row_src.shape[0]
    tile_rows = MOE_BLOCK * ROW_TILES
    smem_blk = lambda off: pl.BlockSpec(
        (1, 1, MOE_BLOCK), lambda i, *_: (jnp.clip(i + off, 0, n_blocks - 1), 0, 0),
        memory_space=pltpu.SMEM)
    hbm = pl.BlockSpec(memory_space=pl.ANY)
    grid_spec = pltpu.PrefetchScalarGridSpec(
        num_scalar_prefetch=3,
        grid=(n_blocks,),
        in_specs=[
            smem_blk(0), smem_blk(1), smem_blk(2), smem_blk(0),
            pl.BlockSpec((MOE_BLOCK, 1), lambda i, *_: (i, 0)),
            hbm, hbm, hbm, hbm,
        ],
        out_specs=hbm,
        scratch_shapes=[
            pltpu.VMEM((X_SLOTS, tile_rows, ROW_LANES), F32),
            pltpu.VMEM((Y_SLOTS, tile_rows, ROW_LANES), F32),
            pltpu.VMEM((W_SLOTS, d, ff), F32), pltpu.VMEM((W_SLOTS, d, ff), F32),
            pltpu.VMEM((W_SLOTS, ff, d), F32),
            pltpu.VMEM((d, ff), BF16), pltpu.VMEM((d, ff), BF16), pltpu.VMEM((ff, d), BF16),
            pltpu.SemaphoreType.DMA((X_SLOTS,)), pltpu.SemaphoreType.DMA((Y_SLOTS,)),
            pltpu.SemaphoreType.DMA((W_SLOTS,)),
        ],
    )
    y_rows = (n_assign + Y_SLOTS * MOE_BLOCK) * ROW_TILES
    return pl.pallas_call(
        _moe_expert_kernel,
        grid_spec=grid_spec,
        out_shape=jax.ShapeDtypeStruct((y_rows, ROW_LANES), F32),
        compiler_params=_cp(("arbitrary",), vmem=MOE_VMEM_LIMIT),
        name="moe_experts",
    )(block_run, run_exp, n_used_runs, row_src, row_src, row_src, row_dst, row_w, u_all,
      w_gate, w_up, w_down)


def _moe_combine_kernel(h_ref, y0_ref, y1_ref, o_ref):
    rows = h_ref.shape[0]
    o_ref[...] = h_ref[...] + (_tiles_to_rows(y0_ref, rows) + _tiles_to_rows(y1_ref, rows))


def moe_combine(h, y_slots, row_off, slot_stride, *, tm):
    m, d = h.shape
    assert row_off % tm == 0 and slot_stride % tm == 0
    off0, off1 = row_off // tm, (row_off + slot_stride) // tm
    return pl.pallas_call(
        _moe_combine_kernel,
        grid=(m // tm,),
        in_specs=[pl.BlockSpec((tm, d), lambda i: (i, 0)),
                  pl.BlockSpec((tm * ROW_TILES, ROW_LANES), lambda i: (i + off0, 0)),
                  pl.BlockSpec((tm * ROW_TILES, ROW_LANES), lambda i: (i + off1, 0))],
        out_specs=pl.BlockSpec((tm, d), lambda i: (i, 0)),
        out_shape=jax.ShapeDtypeStruct((m, d), F32),
        compiler_params=_cp(("parallel",)),
        name="moe_combine",
    )(h, y_slots, y_slots)


def moe_dispatch(e_idx, gates, slot_stride):
    m = e_idx.shape[0]
    a = m * TOP_K
    e_flat = e_idx.reshape(a)
    _, order, gate_bits = lax.sort(
        (e_flat, jnp.arange(a, dtype=jnp.int32),
         lax.bitcast_convert_type(gates.reshape(a), jnp.int32)), num_keys=1, is_stable=True)
    gate_sorted = lax.bitcast_convert_type(gate_bits, F32)
    counts =jnp.sum(e_flat[:, None] == jnp.arange(N_EXPERTS, dtype=jnp.int32)[None, :],
                     axis=0, dtype=jnp.int32)
    pad_counts = (counts + MOE_BLOCK - 1) // MOE_BLOCK * MOE_BLOCK
    starts = jnp.cumsum(counts) - counts
    pad_ends = jnp.cumsum(pad_counts)
    pad_starts = pad_ends - pad_counts
    n_blocks = a // MOE_BLOCK + N_EXPERTS
    p = n_blocks * MOE_BLOCK
    n_used = (pad_ends[-1] // MOE_BLOCK).astype(jnp.int32)
    blk = jnp.arange(n_blocks, dtype=jnp.int32)
    blk_start = jnp.minimum(blk, n_used - 1) * MOE_BLOCK
    block_exp = jnp.minimum(jnp.sum(blk_start[:, None] >= pad_ends[None, :], axis=1),
                            N_EXPERTS - 1).astype(jnp.int32)
    in_exp = blk * MOE_BLOCK - pad_starts[block_exp]
    row_cnt = jnp.where(blk < n_used, jnp.clip(counts[block_exp] - in_exp, 0, MOE_BLOCK), 0)
    lane = jnp.arange(MOE_BLOCK, dtype=jnp.int32)[None, :]
    valid = lane < row_cnt[:, None]
    src = jnp.clip((starts[block_exp] + in_exp)[:, None] + lane, 0, a - 1)
    assign = order[src]
    row_tok = jnp.where(valid, assign // TOP_K, 0)
    pad_dst = TOP_K * slot_stride + (blk % Y_SLOTS)[:, None] * MOE_BLOCK + lane
    row_dst = jnp.where(valid, (assign % TOP_K) * slot_stride + assign // TOP_K, pad_dst)
    row_w = jnp.where(valid, gate_sorted[src], 0.0)
    as_blocks = lambda x: (x * ROW_TILES).astype(jnp.int32).reshape(n_blocks, 1, MOE_BLOCK)
    has_rows = counts > 0
    run_exp = jnp.argsort(~has_rows, stable=True).astype(jnp.int32)
    block_run = (jnp.cumsum(has_rows) - 1)[block_exp].astype(jnp.int32)
    n_used_runs = jnp.stack([n_used, jnp.sum(has_rows, dtype=jnp.int32)])
    return (as_blocks(row_tok), as_blocks(row_dst), row_w.reshape(p, 1), block_run, run_exp,
            n_used_runs)


def rwkv_params(rw_mu, rw_w0, rw_w2, rw_a0, rw_a2, rw_g2, rw_k_k, rw_k_a, rw_r_k):
    c = RWKV_WIDTH
    mu = jnp.pad(rw_mu, (0, RWKV_PROJ_PAD - RWKV_PROJ)).reshape(1, RWKV_PROJ_PAD)
    w_lora = jnp.zeros((LANES, 2 * c), F32)
    w_lora = w_lora.at[0:DECAY_LORA, 0:c].set(rw_w2).at[DECAY_LORA:LANES, c:2 * c].set(rw_a2)
    w_gate = jnp.pad(rw_g2, ((0, GATE_PAD - GATE_LORA), (0, 0)))
    vec = lambda x: x.reshape(1, c)
    return (mu, vec(rw_w0), vec(rw_a0), vec(rw_k_k), vec(rw_k_a), vec(rw_r_k),
            w_lora.astype(BF16), w_gate.astype(BF16))


def _token_tiles(m):
    return (1024, 512) if m % 1024 == 0 else (m, m)


def _dense_front(x2d, wts, tm):
    pa = norm_matmul(x2d, wts['ln1_w'], wts['wt_att'], tm=tm, tn=512, w_transposed=True)
    pr = norm_matmul(x2d, wts['ln1_w'], wts['wt_rw'], tm=tm, tn=RWKV_PROJ_PAD // 3,
                     w_transposed=True)
    return pa, pr


def _dense_back(x2d, att2d, rw2d, wts, xattn_fn, tm):
    h1 = matmul_residual([att2d, rw2d], [wts['w_out_a'], wts['w_out_r']], x2d, tm=tm, tn=512)
    qx = norm_matmul(h1, wts['ln2_w'], wts['xq_w'], tm=tm, tn=XATT_WIDTH)
    ox = xattn_fn(qx)
    return matmul_residual([ox], [wts['xo_w']], h1, tm=tm, tn=512)


def kernel(x_prompt, x_sample, cache_win_k, cache_win_v, state_wkv, state_shift, cache_mem_k, cache_mem_v, mem_prompt, ln1_w, w_in, q_norm_w, k_norm_w, attn_sinks, rw_mu, rw_w0, rw_w2, rw_a0, rw_a2, rw_g2, rw_k_k, rw_k_a, rw_r_k, rw_ln_w, rw_ln_b, w_out, ln2_w, mem_norm_w, xq_w, xkv_w, xq_norm_w, xk_norm_w, xo_w, ln3_w, router_group_w, router_group_b, router_expert_w, router_expert_b, exp_w_gate, exp_w_up, exp_w_down):
    assert w_in.shape[0] == 1, "single-layer stack"
    bp, seq, d = x_prompt.shape
    bs = x_sample.shape[0]
    mp = bp * seq
    c = RWKV_WIDTH

    router_w = jnp.concatenate(
        [router_group_w[0], router_expert_w[0],
         jnp.zeros((d, ROUTER_LANES - N_EXPERT_GROUPS - N_EXPERTS), F32)], axis=1)
    router_hi = router_w.astype(BF16)
    wts = {
        'ln1_w': ln1_w[0], 'ln2_w': ln2_w[0], 'ln3_w': ln3_w[0],
        'wt_att': cast_rows_bf16(w_in[0].T, 0, ATT_PROJ, ATT_PROJ),
        'wt_rw': cast_rows_bf16(w_in[0].T, ATT_PROJ, RWKV_PROJ, RWKV_PROJ_PAD),
        'w_out_a': w_out[0][:ATT_WIDTH].astype(BF16),
        'w_out_r': w_out[0][ATT_WIDTH:].astype(BF16),
        'xq_w': xq_w[0].astype(BF16), 'xo_w': xo_w[0].astype(BF16),
        'router_hi': router_hi,
        'router_lo': (router_w - router_hi.astype(F32)).astype(BF16),
        'router_b': jnp.pad(jnp.concatenate([router_group_b[0], router_expert_b[0]]),
                            (0, ROUTER_LANES - N_EXPERT_GROUPS - N_EXPERTS)).reshape(1, -1),
    }
    rw_par = rwkv_params(rw_mu[0], rw_w0[0], rw_w2[0], rw_a0[0], rw_a2[0], rw_g2[0],
                         rw_k_k[0], rw_k_a[0], rw_r_k[0])

    tm_p, te_p = _token_tiles(mp)
    xp = x_prompt.reshape(mp, d)
    pa, pr = _dense_front(xp, wts, tm_p)
    pa3 = pa.reshape(bp, seq, ATT_PROJ)
    pr3 = pr.reshape(bp, seq, RWKV_PROJ_PAD)
    tabs_p = rope_tables(np.arange(seq))
    att_p, kn_p = swa_prompt(pa3, tabs_p, q_norm_w[0], k_norm_w[0], attn_sinks[0])
    rw_p, wkv_p = rwkv_seq(pr3, jnp.zeros((bp, 1, RWKV_PROJ_PAD), F32), rw_par,
                           rw_ln_w[0], rw_ln_b[0])
    rw_p = rw_p.reshape(mp, c)

    n_mem = mem_prompt.shape[1]
    kv_mem = norm_matmul(mem_prompt.reshape(bp * n_mem, d), mem_norm_w[0],
                         xkv_w[0].astype(BF16), tm=bp * n_mem, tn=512)
    mem_k = head_rms(kv_mem[:, :XATT_WIDTH], xk_norm_w[0])
    mem_v = kv_mem[:, XATT_WIDTH:]
    mem_k3 = mem_k.reshape(bp, n_mem, XATT_WIDTH)
    mem_v3 = mem_v.reshape(bp, n_mem, XATT_WIDTH)

    def xattn_p(qx):
        return xattn_prompt(qx.reshape(bp, seq, XATT_WIDTH), mem_k3, mem_v3,
                            xq_norm_w[0]).reshape(mp, XATT_WIDTH)

    h2_p = _dense_back(xp, att_p.reshape(mp, ATT_WIDTH), rw_p, wts, xattn_p, tm_p)

    tm_s, te_s = _token_tiles(bs)
    xs = x_sample.reshape(bs, d)
    sa, sr = _dense_front(xs, wts, tm_s)
    tabs_s = rope_tables(PAST_LEN + np.arange(1))
    qk_w = jnp.concatenate([jnp.tile(q_norm_w[0], ATT_HEADS),
                            jnp.tile(k_norm_w[0], ATT_KV_HEADS)]).reshape(1, -1)
    qk = qk_norm_rope(sa[:, :ATT_WIDTH + KV_WIDTH], qk_w, tabs_s)
    nbuf = cache_win_k.shape[2]

    def feature_major(cache):
        return jnp.transpose(cache, (0, 1, 3, 4, 2)).reshape(bs, KV_WIDTH, nbuf)

    def position_major(win):
        return jnp.transpose(win.reshape(1, bs, ATT_KV_HEADS, HEAD_DIM, nbuf), (0, 1, 4, 2, 3))

    att_s, win_k, win_v = swa_decode(
        qk[:, :ATT_WIDTH].reshape(bs, ATT_HEADS, HEAD_DIM),
        qk[:, ATT_WIDTH:].reshape(bs, 1, KV_WIDTH),
        sa[:, ATT_WIDTH + KV_WIDTH:].reshape(bs, 1, KV_WIDTH),
        feature_major(cache_win_k), feature_major(cache_win_v), attn_sinks[0])
    shift_prev = jnp.pad(state_shift[0], ((0, 0), (0, RWKV_PROJ_PAD - RWKV_PROJ)))
    r, ld, kp, v, kk, b, bonus, g = rwkv_prep_tok(sr, shift_prev, rw_par)
    y_s, wkv_s = rwkv_step(r, ld, kp, v, kk, b, jnp.transpose(state_wkv, (0, 2, 3, 4, 1)))
    wkv_s = jnp.transpose(wkv_s, (0, 4, 1, 2, 3))
    rw_s = rwkv_post(y_s, bonus, g, rw_ln_w[0], rw_ln_b[0], tm=te_s, y_channel_major=True)
    def xattn_s(qx):
        q_pad = jnp.pad(qx.reshape(bs, XATT_HEADS, XATT_HEAD_DIM), ((0, 0), (0, 4), (0, 0)))
        rows_of = lambda c: c.reshape(bs, n_mem * XATT_HEADS, XATT_HEAD_DIM)
        o = xattn_decode(q_pad, rows_of(cache_mem_k), rows_of(cache_mem_v), xq_norm_w[0])
        return o[:, :XATT_HEADS].reshape(bs, XATT_WIDTH)

    h2_s = _dense_back(xs, att_s.reshape(bs, ATT_WIDTH), rw_s, wts, xattn_s, tm_s)

    m_all = mp + bs
    slot_stride = m_all
    tc = math.gcd(mp, bs, 512)
    tr = te_p
    h2_s_pad = jnp.pad(h2_s, ((0, -bs % tr), (0, 0)))
    u_all, idx_all, gate_all = moe_router(h2_p, h2_s_pad, wts['ln3_w'], wts['router_hi'],
                                          wts['router_lo'], wts['router_b'], tm=tr)
    row_src, row_dst, row_w, block_run, run_exp, n_used_runs = moe_dispatch(
        idx_all[:m_all, :TOP_K], gate_all[:m_all, :TOP_K], slot_stride)
    y_slots = moe_experts(u_all, row_src, row_dst, row_w, block_run, run_exp, n_used_runs,
                          exp_w_gate[0], exp_w_up[0], exp_w_down[0], TOP_K * slot_stride)
    out_p = moe_combine(h2_p, y_slots, 0, slot_stride, tm=tc)
    out_s = moe_combine(h2_s, y_slots, mp, slot_stride, tm=tc)

    win = min(WINDOW, seq)
    kv_shape = (1, bp, win, ATT_KV_HEADS, HEAD_DIM)
    return (
        out_p.reshape(bp, seq, d),
        out_s.reshape(bs, 1, d),
        kn_p[:, seq - win:].reshape(kv_shape),
        pa3[:, seq - win:, ATT_WIDTH + KV_WIDTH:].reshape(kv_shape),
        wkv_p[None],
        pr3[:, seq - 1, :RWKV_PROJ][None],
        mem_k3.reshape(1, bp, n_mem, XATT_HEADS, XATT_HEAD_DIM),
        mem_v3.reshape(1, bp, n_mem, XATT_HEADS, XATT_HEAD_DIM),
        position_major(win_k),
        position_major(win_v),
        wkv_s,
        sr[:, :RWKV_PROJ].reshape(1, bs, RWKV_PROJ),
    )
```

```python
import functools
import math

import numpy as np
import jax
import jax.numpy as jnp
from jax import lax
from jax.experimental import pallas as pl
from jax.experimental.pallas import tpu as pltpu

F32 = jnp.float32
BF16 = jnp.bfloat16

D_MODEL = 2048
HEAD_DIM = 64
ATT_HEADS = 16
ATT_KV_HEADS = 4
ATT_GROUP = ATT_HEADS // ATT_KV_HEADS
ATT_WIDTH = ATT_HEADS * HEAD_DIM
KV_WIDTH = ATT_KV_HEADS * HEAD_DIM
ATT_PROJ = ATT_WIDTH + 2 * KV_WIDTH
WINDOW = 128
ATT_SCALE = HEAD_DIM ** -0.5
ROPE_THETA = 500000.0
ROT_DIM = HEAD_DIM // 4
PAST_LEN = 16384

RWKV_WIDTH = 1024
RWKV_HEAD_DIM = 64
RWKV_HEADS = 16
DECAY_LORA = 64
AAA_LORA = 64
GATE_LORA = 160
RWKV_PROJ = 3 * RWKV_WIDTH + DECAY_LORA + AAA_LORA + GATE_LORA
RWKV_PROJ_PAD = 3456

N_MEM = 256
XATT_HEADS = 4
XATT_HEAD_DIM = 128
XATT_WIDTH = XATT_HEADS * XATT_HEAD_DIM

N_EXPERT_GROUPS = 8
EXPERTS_PER_GROUP = 8
N_EXPERTS = 64
TOP_K = 2
EXPERT_FF = D_MODEL // 4
MOE_BLOCK = 128

RMS_EPS = 1e-6
GN_EPS = 64e-5

LANES = 128
CHUNK = 64
VMEM_LIMIT = 56 * 1024 * 1024
MOE_VMEM_LIMIT = 60 * 1024 * 1024


def _cp(sem, vmem=VMEM_LIMIT):
    return pltpu.CompilerParams(dimension_semantics=sem, vmem_limit_bytes=vmem)


def _rms_rows(x, w):
    ms = jnp.mean(x * x, axis=-1, keepdims=True)
    return x * lax.rsqrt(ms + RMS_EPS) * w


def _split2(x):
    hi = x.astype(BF16)
    lo = (x - hi.astype(F32)).astype(BF16)
    return hi, lo


def _split3(x):
    h1 = x.astype(BF16)
    r1 = x - h1.astype(F32)
    h2 = r1.astype(BF16)
    h3 = (r1 - h2.astype(F32)).astype(BF16)
    return h1, h2, h3


def _dot(a, b):
    return jnp.dot(a, b, preferred_element_type=F32)


def _dot_nt(a, b):
    return lax.dot_general(a, b, (((1,), (1,)), ((), ())), preferred_element_type=F32)


def _group_sum(x, gmat):
    hi, lo = _split2(x)
    return _dot(hi, gmat) + _dot(lo, gmat)


def _head_indicator():
    r = lax.broadcasted_iota(jnp.int32, (LANES, LANES), 0) // HEAD_DIM
    c = lax.broadcasted_iota(jnp.int32, (LANES, LANES), 1) // HEAD_DIM
    return jnp.where(r == c, 1.0, 0.0).astype(BF16)


def _norm_mm_kernel(x_ref, lnw_ref, w_ref, o_ref, xn_ref, *, w_transposed):
    @pl.when(pl.program_id(1) == 0)
    def _():
        xn_ref[...] = _rms_rows(x_ref[...], lnw_ref[...]).astype(BF16)

    o_ref[...] = (_dot_nt if w_transposed else _dot)(xn_ref[...], w_ref[...])


def norm_matmul(x, ln_w, w_bf16, *, tm, tn, w_transposed=False):
    m, k = x.shape
    n = w_bf16.shape[0 if w_transposed else 1]
    assert m % tm == 0 and n % tn == 0
    w_spec = (pl.BlockSpec((tn, k), lambda i, j: (j, 0)) if w_transposed
              else pl.BlockSpec((k, tn), lambda i, j: (0, j)))
    return pl.pallas_call(
        functools.partial(_norm_mm_kernel, w_transposed=w_transposed),
        grid=(m // tm, n // tn),
        in_specs=[
            pl.BlockSpec((tm, k), lambda i, j: (i, 0)),
            pl.BlockSpec((1, k), lambda i, j: (0, 0)),
            w_spec,
        ],
        out_specs=pl.BlockSpec((tm, tn), lambda i, j: (i, j)),
        out_shape=jax.ShapeDtypeStruct((m, n), F32),
        scratch_shapes=[pltpu.VMEM((tm, k), BF16)],
        compiler_params=_cp(("parallel", "arbitrary")),
        name="norm_matmul",
    )(x, ln_w.reshape(1, k), w_bf16)


def _cast_rows_kernel(w_ref, o_ref, *, n_in):
    @pl.when(pl.program_id(0) < n_in)
    def _():
        o_ref[...] = w_ref[...].astype(BF16)

    @pl.when(pl.program_id(0) >= n_in)
    def _():
        o_ref[...] = jnp.zeros_like(o_ref)


def cast_rows_bf16(w, first_row, n_rows, n_rows_out, *, rows=96):
    k = w.shape[1]
    assert first_row % rows == 0 and n_rows % rows == 0 and n_rows_out % rows == 0
    first, n_in = first_row // rows, n_rows // rows
    return pl.pallas_call(
        functools.partial(_cast_rows_kernel, n_in=n_in),
        grid=(n_rows_out // rows,),
        in_specs=[pl.BlockSpec((rows, k), lambda i: (first + jnp.minimum(i, n_in - 1), 0))],
        out_specs=pl.BlockSpec((rows, k), lambda i: (i, 0)),
        out_shape=jax.ShapeDtypeStruct((n_rows_out, k), BF16),
        compiler_params=_cp(("parallel",)),
        name="cast_rows_bf16",
    )(w)


def _mm_res_kernel(*refs, n_lhs):
    a_refs = refs[:n_lhs]
    w_refs = refs[n_lhs:2 * n_lhs]
    res_ref = refs[2 * n_lhs]
    o_ref = refs[2 * n_lhs + 1]
    acc = res_ref[...]
    for a_ref, w_ref in zip(a_refs, w_refs):
        acc = acc + _dot(a_ref[...].astype(BF16), w_ref[...])
    o_ref[...] = acc


def matmul_residual(lhs_list, w_list, res, *, tm, tn):
    m, n = res.shape
    n_lhs = len(lhs_list)
    assert m % tm == 0 and n % tn == 0
    in_specs = [pl.BlockSpec((tm, a.shape[1]), lambda i, j: (i, 0)) for a in lhs_list]
    in_specs += [pl.BlockSpec((w.shape[0], tn), lambda i, j: (0, j)) for w in w_list]
    in_specs += [pl.BlockSpec((tm, tn), lambda i, j: (i, j))]
    return pl.pallas_call(
        functools.partial(_mm_res_kernel, n_lhs=n_lhs),
        grid=(m // tm, n // tn),
        in_specs=in_specs,
        out_specs=pl.BlockSpec((tm, tn), lambda i, j: (i, j)),
        out_shape=jax.ShapeDtypeStruct((m, n), F32),
        compiler_params=_cp(("parallel", "arbitrary")),
        name="matmul_residual",
    )(*lhs_list, *w_list, res)


def rope_tables(pos):
    half = ROT_DIM // 2
    f32 = np.float32
    inv = f32(ROPE_THETA) ** (-np.arange(half, dtype=f32) * f32(2.0) / f32(ROT_DIM))
    ang = (np.asarray(pos, f32)[:, None] * inv[None, :].astype(f32)).astype(f32)
    cos, sin = np.cos(ang.astype(np.float64)).astype(f32), np.sin(ang.astype(np.float64)).astype(f32)
    t = ang.shape[0]
    ones = np.ones((t, HEAD_DIM - ROT_DIM), f32)
    zeros = np.zeros((t, HEAD_DIM - ROT_DIM), f32)
    z8 = np.zeros((t, half), f32)
    cos_t = np.concatenate([cos, cos, ones], axis=1)
    sin_a = np.concatenate([z8, sin, zeros], axis=1)
    sin_b = np.concatenate([-sin, z8, zeros], axis=1)
    return tuple(jnp.asarray(np.concatenate([a, a], axis=1)) for a in (cos_t, sin_a, sin_b))


def _norm_rope_chunk(x, w, cos_t, sin_a, sin_b, gmat):
    ms = _group_sum(x * x, gmat) * (1.0 / HEAD_DIM)
    xn = x * lax.rsqrt(ms + RMS_EPS) * w
    half = ROT_DIM // 2
    return (xn * cos_t + pltpu.roll(xn, half, axis=1) * sin_a
            + pltpu.roll(xn, LANES - half, axis=1) * sin_b)


def _norm_rope(x, w, tabs, gmat):
    chunks = [
        _norm_rope_chunk(x[:, c * LANES:(c + 1) * LANES], w, *tabs, gmat)
        for c in range(x.shape[1] // LANES)
    ]
    return chunks[0] if len(chunks) == 1 else jnp.concatenate(chunks, axis=1)


def _sink_softmax(s, sink):
    m = jnp.maximum(jnp.max(s, axis=-1, keepdims=True), sink)
    e = jnp.exp(s - m)
    return e / (jnp.sum(e, axis=-1, keepdims=True) + jnp.exp(sink - m))


def _swa_prompt_kernel(q_ref, kc_ref, vc_ref, kp_ref, vp_ref, cc_ref, sac_ref, sbc_ref,
                       cp_ref, sap_ref, sbp_ref, qw_ref, kw_ref, sink_ref, o_ref, kn_ref):
    n = pl.program_id(1)
    blk = q_ref.shape[0]
    gmat = _head_indicator()
    tabs_c = (cc_ref[...], sac_ref[...], sbc_ref[...])
    tabs_p = (cp_ref[...], sap_ref[...], sbp_ref[...])
    q = _norm_rope(q_ref[...], qw_ref[...], tabs_c, gmat)
    k_cur = _norm_rope(kc_ref[...], kw_ref[...], tabs_c, gmat)
    k_prev = _norm_rope(kp_ref[...], kw_ref[...], tabs_p, gmat)
    kn_ref[...] = k_cur
    k_all = jnp.concatenate([k_prev, k_cur], axis=0).astype(BF16)
    v_all = jnp.concatenate([vp_ref[...], vc_ref[...]], axis=0).astype(BF16)

    qi = lax.broadcasted_iota(jnp.int32, (blk, 2 * blk), 0) + blk
    si = lax.broadcasted_iota(jnp.int32, (blk, 2 * blk), 1)
    rel = qi - si
    valid = (rel >= 0) & (rel <= WINDOW) & ((n > 0) | (si >= blk))

    groups = range(ATT_KV_HEADS)
    lanes = [slice(kv * HEAD_DIM, (kv + 1) * HEAD_DIM) for kv in groups]
    heads = [[kv * ATT_GROUP + g for g in range(ATT_GROUP)] for kv in groups]
    q_g = [jnp.concatenate([q[:, h * HEAD_DIM:(h + 1) * HEAD_DIM] for h in heads[kv]],
                           axis=0).astype(BF16) for kv in groups]
    s = [_dot_nt(q_g[kv], k_all[:, lanes[kv]]) * ATT_SCALE for kv in groups]
    p = [jnp.concatenate(
        [_sink_softmax(jnp.where(valid, s[kv][g * blk:(g + 1) * blk], -jnp.inf), sink_ref[h])
         for g, h in enumerate(heads[kv])], axis=0).astype(BF16) for kv in groups]
    o = [_dot(p[kv], v_all[:, lanes[kv]]) for kv in groups]
    for kv in groups:
        for g, h in enumerate(heads[kv]):
            o_ref[:, h * HEAD_DIM:(h + 1) * HEAD_DIM] = o[kv][g * blk:(g + 1) * blk]


def swa_prompt(pa, tabs, q_norm_w, k_norm_w, sinks):
    b, t, _ = pa.shape
    blk = WINDOW
    nb = t // blk
    qb, kb, vb = 0, ATT_WIDTH // KV_WIDTH, ATT_WIDTH // KV_WIDTH + 1
    cur = lambda i, n, *_: (i, n, 0)
    tab_cur = pl.BlockSpec((blk, LANES), lambda i, n: (n, 0))
    tab_prev = pl.BlockSpec((blk, LANES), lambda i, n: (jnp.maximum(n - 1, 0), 0))
    qw = jnp.tile(q_norm_w.reshape(1, HEAD_DIM), (1, 2))
    kw = jnp.tile(k_norm_w.reshape(1, HEAD_DIM), (1, 2))
    return pl.pallas_call(
        _swa_prompt_kernel,
        grid=(b, nb),
        in_specs=[
            pl.BlockSpec((None, blk, ATT_WIDTH), lambda i, n: (i, n, qb)),
            pl.BlockSpec((None, blk, KV_WIDTH), lambda i, n: (i, n, kb)),
            pl.BlockSpec((None, blk, KV_WIDTH), lambda i, n: (i, n, vb)),
            pl.BlockSpec((None, blk, KV_WIDTH), lambda i, n: (i, jnp.maximum(n - 1, 0), kb)),
            pl.BlockSpec((None, blk, KV_WIDTH), lambda i, n: (i, jnp.maximum(n - 1, 0), vb)),
            tab_cur, tab_cur, tab_cur, tab_prev, tab_prev, tab_prev,
            pl.BlockSpec((1, LANES), lambda i, n: (0, 0)),
            pl.BlockSpec((1, LANES), lambda i, n: (0, 0)),
            pl.BlockSpec(memory_space=pltpu.SMEM),
        ],
        out_specs=[
            pl.BlockSpec((None, blk, ATT_WIDTH), cur),
            pl.BlockSpec((None, blk, KV_WIDTH), cur),
        ],
        out_shape=[
            jax.ShapeDtypeStruct((b, t, ATT_WIDTH), F32),
            jax.ShapeDtypeStruct((b, t, KV_WIDTH), F32),
        ],
        compiler_params=_cp(("parallel", "arbitrary")),
        name="swa_prompt",
    )(pa, pa, pa, pa, pa, *tabs, *tabs, qw, kw, sinks)


def _qk_norm_rope_kernel(x_ref, w_ref, c_ref, sa_ref, sb_ref, o_ref):
    gmat = _head_indicator()
    tabs = (c_ref[...], sa_ref[...], sb_ref[...])
    for c in range(x_ref.shape[1] // LANES):
        sl = slice(c * LANES, (c + 1) * LANES)
        o_ref[:, sl] = _norm_rope_chunk(x_ref[:, sl], w_ref[:, sl], *tabs, gmat)


def qk_norm_rope(x, w_row, tabs):
    m, w = x.shape
    full = lambda *shape: pl.BlockSpec(shape, lambda: (0,) * len(shape))
    return pl.pallas_call(
        _qk_norm_rope_kernel,
        in_specs=[full(m, w), full(1, w), full(1, LANES), full(1, LANES), full(1, LANES)],
        out_specs=full(m, w),
        out_shape=jax.ShapeDtypeStruct((m, w), F32),
        name="qk_norm_rope",
    )(x, w_row, *tabs)


def _swa_decode_kernel(q_ref, kn_ref, vn_ref, knt_ref, vnt_ref, ck_ref, cv_ref, sink_ref, o_ref,
                       kw_ref, vw_ref):
    bb = q_ref.shape[0]
    nbuf = ck_ref.shape[2]
    row_kv = lax.broadcasted_iota(jnp.int32, (ATT_HEADS, KV_WIDTH), 0) // ATT_GROUP
    lane_kv = lax.broadcasted_iota(jnp.int32, (ATT_HEADS, KV_WIDTH), 1) // HEAD_DIM
    own = row_kv == lane_kv
    sink = sink_ref[...]
    seq = lax.broadcasted_iota(jnp.int32, knt_ref.shape, 1)
    pos = lax.broadcasted_iota(jnp.int32, (KV_WIDTH, nbuf), 1)
    b0 = pl.program_id(0) * bb
    for b in range(bb):
        q2 = q_ref[b]
        q_exp = jnp.where(own, jnp.concatenate([q2] * ATT_KV_HEADS, axis=1), 0.0)
        k_new, v_new = kn_ref[b], vn_ref[b]
        k_buf, v_buf = ck_ref[b], cv_ref[b]
        s_buf = _dot(q_exp.astype(BF16), k_buf.astype(BF16)) * ATT_SCALE
        s_new = jnp.sum(q_exp * k_new, axis=-1, keepdims=True) * ATT_SCALE
        m = jnp.maximum(jnp.maximum(jnp.max(s_buf, axis=-1, keepdims=True), s_new), sink)
        e_buf = jnp.exp(s_buf - m)
        e_new = jnp.exp(s_new - m)
        inv = 1.0 / (jnp.sum(e_buf, axis=-1, keepdims=True) + e_new + jnp.exp(sink - m))
        o = _dot_nt((e_buf * inv).astype(BF16), v_buf.astype(BF16)) + (e_new * inv) * v_new
        o = jnp.where(own, o, 0.0)
        o_ref[b] = (o[:, 0:HEAD_DIM] + o[:, HEAD_DIM:2 * HEAD_DIM]
                    + o[:, 2 * HEAD_DIM:3 * HEAD_DIM] + o[:, 3 * HEAD_DIM:4 * HEAD_DIM])
        k_col = jnp.sum(jnp.where(seq == b0 + b, knt_ref[...], 0.0), axis=1, keepdims=True)
        v_col = jnp.sum(jnp.where(seq == b0 + b, vnt_ref[...], 0.0), axis=1, keepdims=True)
        kw_ref[b] = jnp.where(pos == nbuf - 1, k_col, pltpu.roll(k_buf, nbuf - 1, axis=1))
        vw_ref[b] = jnp.where(pos == nbuf - 1, v_col, pltpu.roll(v_buf, nbuf - 1, axis=1))


def swa_decode(q, k_new, v_new, cache_kt, cache_vt, sinks, *, bb=8):
    b, _, nbuf = cache_kt.shape
    blk3 = lambda s1, s2: pl.BlockSpec((bb, s1, s2), lambda i: (i, 0, 0))
    whole = pl.BlockSpec((KV_WIDTH, b), lambda i: (0, 0))
    return pl.pallas_call(
        _swa_decode_kernel,
        grid=(b // bb,),
        in_specs=[
            blk3(ATT_HEADS, HEAD_DIM), blk3(1, KV_WIDTH), blk3(1, KV_WIDTH), whole, whole,
            blk3(KV_WIDTH, nbuf), blk3(KV_WIDTH, nbuf),
            pl.BlockSpec((ATT_HEADS, 1), lambda i: (0, 0)),
        ],
        out_specs=[blk3(ATT_HEADS, HEAD_DIM), blk3(KV_WIDTH, nbuf), blk3(KV_WIDTH, nbuf)],
        out_shape=[
            jax.ShapeDtypeStruct((b, ATT_HEADS, HEAD_DIM), F32),
            jax.ShapeDtypeStruct((b, KV_WIDTH, nbuf), F32),
            jax.ShapeDtypeStruct((b, KV_WIDTH, nbuf), F32),
        ],
        compiler_params=_cp(("parallel",)),
        name="swa_decode",
    )(q, k_new, v_new, k_new.reshape(b, KV_WIDTH).T, v_new.reshape(b, KV_WIDTH).T,
      cache_kt, cache_vt, sinks.reshape(ATT_HEADS, 1))


def _head_rms_kernel(x_ref, w_ref, o_ref):
    for h in range(x_ref.shape[1] // XATT_HEAD_DIM):
        sl = slice(h * XATT_HEAD_DIM, (h + 1) * XATT_HEAD_DIM)
        o_ref[:, sl] = _rms_rows(x_ref[:, sl], w_ref[...])


def head_rms(x, w):
    m, wd = x.shape
    return pl.pallas_call(
        _head_rms_kernel,
        in_specs=[pl.BlockSpec((m, wd), lambda: (0, 0)),
                  pl.BlockSpec((1, XATT_HEAD_DIM), lambda: (0, 0))],
        out_specs=pl.BlockSpec((m, wd), lambda: (0, 0)),
        out_shape=jax.ShapeDtypeStruct((m, wd), F32),
        name="head_rms",
    )(x, w.reshape(1, XATT_HEAD_DIM))


def _xattn_prompt_kernel(q_ref, k_ref, v_ref, w_ref, o_ref):
    scale = 1.0 / math.sqrt(XATT_HEAD_DIM)
    for h in range(XATT_HEADS):
        sl = slice(h * XATT_HEAD_DIM, (h + 1) * XATT_HEAD_DIM)
        qn = _rms_rows(q_ref[:, sl], w_ref[...]).astype(BF16)
        s = _dot_nt(qn, k_ref[:, sl].astype(BF16)) * scale
        e = jnp.exp(s - jnp.max(s, axis=-1, keepdims=True))
        p = e / jnp.sum(e, axis=-1, keepdims=True)
        o_ref[:, sl] = _dot(p.astype(BF16), v_ref[:, sl].astype(BF16))


def xattn_prompt(q, mem_k, mem_v, xq_norm_w, *, tq=512):
    b, t, w = q.shape
    n_mem = mem_k.shape[1]
    return pl.pallas_call(
        _xattn_prompt_kernel,
        grid=(b, t // tq),
        in_specs=[
            pl.BlockSpec((None, tq, w), lambda i, j: (i, j, 0)),
            pl.BlockSpec((None, n_mem, w), lambda i, j: (i, 0, 0)),
            pl.BlockSpec((None, n_mem, w), lambda i, j: (i, 0, 0)),
            pl.BlockSpec((1, XATT_HEAD_DIM), lambda i, j: (0, 0)),
        ],
        out_specs=pl.BlockSpec((None, tq, w), lambda i, j: (i, j, 0)),
        out_shape=jax.ShapeDtypeStruct((b, t, w), F32),
        compiler_params=_cp(("parallel", "arbitrary")),
        name="xattn_prompt",
    )(q, mem_k, mem_v, xq_norm_w.reshape(1, XATT_HEAD_DIM))


def _xattn_decode_kernel(q_ref, k_ref, v_ref, w_ref, o_ref):
    bb, rows, _ = q_ref.shape
    n_keys = k_ref.shape[1]
    scale = 1.0 / math.sqrt(XATT_HEAD_DIM)
    own = (lax.broadcasted_iota(jnp.int32, (rows, n_keys), 1) % XATT_HEADS
           == lax.broadcasted_iota(jnp.int32, (rows, n_keys), 0) % XATT_HEADS)
    seqs = range(bb)
    qn = [_rms_rows(q_ref[b], w_ref[...]).astype(BF16) for b in seqs]
    s = [jnp.where(own, _dot_nt(qn[b], k_ref[b].astype(BF16)) * scale, -jnp.inf) for b in seqs]
    e = [jnp.exp(s[b] - jnp.max(s[b], axis=-1, keepdims=True)) for b in seqs]
    p = [(e[b] / jnp.sum(e[b], axis=-1, keepdims=True)).astype(BF16) for b in seqs]
    for b in seqs:
        o_ref[b] = _dot(p[b], v_ref[b].astype(BF16))


def xattn_decode(q_pad, mem_k, mem_v, xq_norm_w, *, bb=8):
    b, rows, _ = q_pad.shape
    n_keys = mem_k.shape[1]
    kv = pl.BlockSpec((bb, n_keys, XATT_HEAD_DIM), lambda i: (i, 0, 0))
    return pl.pallas_call(
        _xattn_decode_kernel,
        grid=(b // bb,),
        in_specs=[pl.BlockSpec((bb, rows, XATT_HEAD_DIM), lambda i: (i, 0, 0)), kv, kv,
                  pl.BlockSpec((1, XATT_HEAD_DIM), lambda i: (0, 0))],
        out_specs=pl.BlockSpec((bb, rows, XATT_HEAD_DIM), lambda i: (i, 0, 0)),
        out_shape=jax.ShapeDtypeStruct((b, rows, XATT_HEAD_DIM), F32),
        compiler_params=_cp(("parallel",)),
        name="xattn_decode",
    )(q_pad, mem_k, mem_v, xq_norm_w.reshape(1, XATT_HEAD_DIM))


LORA_OFF = 3 * RWKV_WIDTH
GATE_OFF = LORA_OFF + DECAY_LORA + AAA_LORA
GATE_PAD = RWKV_PROJ_PAD - GATE_OFF


def _sigmoid(x):
    return 1.0 / (1.0 + jnp.exp(-x))


def _per_chunk(fn, *arrays):
    w = arrays[0].shape[1]
    outs = [fn(*(a[:, c * LANES:(c + 1) * LANES] for a in arrays)) for c in range(w // LANES)]
    return jnp.concatenate(outs, axis=1)


def _rwkv_prep_core(pr, prev, mu, w0, a0, kk_w, ka_w, rk_w, w_lora, w_gate):
    c = RWKV_WIDTH
    gmat = _head_indicator()
    xm = pr + (prev - pr) * mu
    r, k, v = xm[:, 0:c], xm[:, c:2 * c], xm[:, 2 * c:3 * c]
    lora = xm[:, LORA_OFF:LORA_OFF + LANES]
    lane = lax.broadcasted_iota(jnp.int32, lora.shape, 1)
    lora_in = jnp.where(lane < DECAY_LORA, jnp.tanh(lora), lora)
    wa = _dot(lora_in.astype(BF16), w_lora)
    z = -(w0 + wa[:, 0:c])
    softplus = jnp.maximum(z, 0.0) + jnp.log(1.0 + jnp.exp(-jnp.abs(z)))
    log_decay = -jnp.exp(-softplus - 0.5)
    a = _sigmoid(a0 + wa[:, c:2 * c])
    g = _dot(_sigmoid(xm[:, GATE_OFF:GATE_OFF + GATE_PAD]).astype(BF16), w_gate)
    kk = k * kk_w
    norm = jnp.sqrt(_per_chunk(lambda t: _group_sum(t * t, gmat), kk))
    kk = kk / jnp.maximum(norm, 1e-12)
    kp = k * (1.0 + (a - 1.0) * ka_w)
    bonus = _per_chunk(lambda t: _group_sum(t, gmat), r * kp * rk_w) * v
    return r, log_decay, kp, v, kk, kk * a, bonus, g


def _rwkv_prep_tok_kernel(pr_ref, prev_ref, mu_ref, w0_ref, a0_ref, kkw_ref, kaw_ref, rkw_ref,
                          wl_ref, wg_ref, *out_refs):
    outs = _rwkv_prep_core(pr_ref[...], prev_ref[...], mu_ref[...], w0_ref[...], a0_ref[...],
                           kkw_ref[...], kaw_ref[...], rkw_ref[...], wl_ref[...], wg_ref[...])
    for k, (o_ref, o) in enumerate(zip(out_refs, outs)):
        o_ref[...] = o.T if k < N_STEP_VECS else o


def _rwkv_param_specs(index_map):
    c = RWKV_WIDTH
    shapes = [(1, RWKV_PROJ_PAD)] + [(1, c)] * 5 + [(LANES, 2 * c), (GATE_PAD, c)]
    return [pl.BlockSpec(s, index_map) for s in shapes]


N_STEP_VECS = 6


def rwkv_prep_tok(pr, prev, params):
    m, wd = pr.shape
    c = RWKV_WIDTH
    shapes = [(c, m)] * N_STEP_VECS + [(m, c)] * 2
    return pl.pallas_call(
        _rwkv_prep_tok_kernel,
        grid=(1,),
        in_specs=[pl.BlockSpec((m, wd), lambda i: (0, 0))] * 2
        + _rwkv_param_specs(lambda i: (0, 0)),
        out_specs=[pl.BlockSpec(s, lambda i: (0, 0)) for s in shapes],
        out_shape=[jax.ShapeDtypeStruct(s, F32) for s in shapes],
        compiler_params=_cp(("arbitrary",)),
        name="rwkv_prep_tok",
    )(pr, prev, *params)


def _dot_tn(a, b):
    return lax.dot_general(a, b, (((0,), (0,)), ((), ())), preferred_element_type=F32)


def _rwkv_scan_tile(r_all, ld_all, kp_all, v_all, kk_all, b_all, s_cur, y_ref):
    n = CHUNK
    nc = r_all.shape[0] // n
    ti = lax.broadcasted_iota(jnp.int32, (n, n), 0)
    si = lax.broadcasted_iota(jnp.int32, (n, n), 1)
    tri = jnp.where(si <= ti, 1.0, 0.0).astype(BF16)
    t2 = lax.broadcasted_iota(jnp.int32, (n, LANES), 0)
    lane2 = lax.broadcasted_iota(jnp.int32, (n, LANES), 1)
    s2 = lane2 % n
    low = lane2 < n
    strict2, incl2, eye2 = s2 < t2, s2 <= t2, s2 == t2
    low4 = lax.broadcasted_iota(jnp.int32, (2 * n, LANES), 1) < n
    top4 = lax.broadcasted_iota(jnp.int32, (2 * n, LANES), 0) < n
    diag_blk = top4 == low4
    own4 = {0: low4, 1: ~low4}
    pairs = range(RWKV_HEADS // 2)
    cs = [slice(c * LANES, (c + 1) * LANES) for c in pairs]

    pre = []
    for j in range(nc):
        rows = slice(j * n, (j + 1) * n)
        ld = ld_all[rows, :]
        l1, l2, l3 = _split3(ld)
        lc = _dot(tri, l1) + _dot(tri, l2) + _dot(tri, l3)
        lc_end = lc[n - 1:n, :]
        e_neg = jnp.exp(-lc)
        kk, b, kp = kk_all[rows, :], b_all[rows, :], kp_all[rows, :]
        to_end = jnp.exp(lc_end - lc)
        pre.append(dict(
            a_t=(-kk * jnp.exp(lc - ld)).astype(BF16), b_t=(b * e_neg).astype(BF16),
            k_t=(kp * e_neg).astype(BF16), r_t=(r_all[rows, :] * jnp.exp(lc)).astype(BF16),
            b_e=(b * to_end).astype(BF16), k_e=(kp * to_end).astype(BF16),
            v_b=v_all[rows, :].astype(BF16), g_end=jnp.exp(lc_end)))
    items = [(j, c, par) for j in range(nc) for c in pairs for par in (0, 1)]
    at = {it: i for i, it in enumerate(items)}
    ar = {(j, c): jnp.concatenate([pre[j]['a_t'][:, cs[c]], pre[j]['r_t'][:, cs[c]]], axis=0)
          for j in range(nc) for c in pairs}
    kb = {(j, c): jnp.concatenate([pre[j]['k_t'][:, cs[c]], pre[j]['b_t'][:, cs[c]]], axis=0)
          for j in range(nc) for c in pairs}
    gm = [_dot_nt(jnp.where(own4[par], ar[j, c], 0).astype(BF16), kb[j, c]) for j, c, par in items]
    top = [g[0:n] for g in gm]
    pm = [jnp.where(incl2, g[n:2 * n], 0.0).astype(BF16) for g in gm]
    lak = [jnp.where(strict2 & low, t, 0.0).astype(BF16) for t in top]
    lv = [_dot(lak[i][:, 0:n], pre[j]['v_b'][:, cs[c]]) for i, (j, c, par) in enumerate(items)]
    z = [jnp.where(low, jnp.where(eye2, 1.0, 0.0), jnp.where(strict2, t, 0.0)) for t in top]
    for _ in range(6):
        zb = [zz.astype(BF16) for zz in z]
        res = [_dot(jnp.where(low, 0, zb[i]).astype(BF16), jnp.concatenate([zb[i], zb[i]], axis=0))
               for i in range(len(items))]
        z = [res[i] + jnp.where(low, z[i], 0.0) for i in range(len(items))]
    tmat = [zz[:, 0:n].astype(BF16) for zz in z]

    for j in range(nc):
        p = pre[j]
        sw = [_dot_nt(ar[j, c], s_cur[c].astype(BF16)) for c in pairs]
        w0 = [(sw[c][0:n] + jnp.where(low, lv[at[j, c, 0]], lv[at[j, c, 1]])).astype(BF16)
              for c in pairs]
        u = [jnp.where(low, _dot(tmat[at[j, c, 0]], w0[c]),
                       _dot(tmat[at[j, c, 1]], w0[c])).astype(BF16) for c in pairs]
        vu = [jnp.concatenate([p['v_b'][:, cs[c]], u[c]], axis=0) for c in pairs]
        yb = [jnp.where(low, _dot(pm[at[j, c, 0]], vu[c]), _dot(pm[at[j, c, 1]], vu[c]))
              for c in pairs]
        for c in pairs:
            y_ref[j * n:(j + 1) * n, cs[c]] = sw[c][n:2 * n] + yb[c]
        upd = [_dot_tn(jnp.concatenate([u[c], p['v_b'][:, cs[c]]], axis=0),
                       jnp.concatenate([p['b_e'][:, cs[c]], p['k_e'][:, cs[c]]], axis=0))
               for c in pairs]
        s_cur = [s_cur[c] * p['g_end'][:, cs[c]] + jnp.where(diag_blk, upd[c], 0.0) for c in pairs]
    return s_cur


def _group_norm_gate(y, bonus, g, ln_w, ln_b, gmat):
    inv = 1.0 / RWKV_HEAD_DIM
    d = y - _group_sum(y, gmat) * inv
    var = _group_sum(d * d, gmat) * inv
    return (d * lax.rsqrt(var + GN_EPS) * ln_w + ln_b + bonus) * g


def _rwkv_seq_kernel(pr_ref, prev0_ref, mu_ref, w0_ref, a0_ref, kkw_ref, kaw_ref, rkw_ref,
                     wl_ref, wg_ref, lnw_ref, lnb_ref, o_ref, s_out_ref, s_ref, last_ref, y_ref):
    @pl.when(pl.program_id(1) == 0)
    def _():
        s_ref[...] = jnp.zeros_like(s_ref)
        last_ref[...] = prev0_ref[...]

    pr = pr_ref[...]
    rows = pr.shape[0]
    row = lax.broadcasted_iota(jnp.int32, (rows, 1), 0)
    prev = jnp.where(row == 0, last_ref[...], pltpu.roll(pr, 1, axis=0))
    last_ref[...] = pr[rows - 1:rows, :]
    r, ld, kp, v, kk, b, bonus, g = _rwkv_prep_core(
        pr, prev, mu_ref[...], w0_ref[...], a0_ref[...], kkw_ref[...], kaw_ref[...],
        rkw_ref[...], wl_ref[...], wg_ref[...])
    pairs = range(RWKV_HEADS // 2)
    s_new = _rwkv_scan_tile(r, ld, kp, v, kk, b, [s_ref[c] for c in pairs], y_ref)
    n = RWKV_HEAD_DIM
    for c in pairs:
        s_ref[c] = s_new[c]
        s_out_ref[2 * c] = s_new[c][0:n, 0:n]
        s_out_ref[2 * c + 1] = s_new[c][n:2 * n, n:2 * n]
    gmat = _head_indicator()
    for c in pairs:
        sl = slice(c * LANES, (c + 1) * LANES)
        o_ref[:, sl] = _group_norm_gate(y_ref[:, sl], bonus[:, sl], g[:, sl], lnw_ref[:, sl],
                                        lnb_ref[:, sl], gmat)


SCAN_CHUNKS = 4


def rwkv_seq(pr, prev0, params, ln_w, ln_b):
    bsz, t, wd = pr.shape
    c = RWKV_WIDTH
    rows = SCAN_CHUNKS * CHUNK
    assert t % rows == 0
    vec = pl.BlockSpec((1, c), lambda i, j: (0, 0))
    st = pl.BlockSpec((None, RWKV_HEADS, RWKV_HEAD_DIM, RWKV_HEAD_DIM), lambda i, j: (i, 0, 0, 0))
    return pl.pallas_call(
        _rwkv_seq_kernel,
        grid=(bsz, t // rows),
        in_specs=[pl.BlockSpec((None, rows, wd), lambda i, j: (i, j, 0)),
                  pl.BlockSpec((None, 1, wd), lambda i, j: (i, 0, 0))]
        + _rwkv_param_specs(lambda i, j: (0, 0)) + [vec, vec],
        out_specs=[pl.BlockSpec((None, rows, c), lambda i, j: (i, j, 0)), st],
        out_shape=[jax.ShapeDtypeStruct((bsz, t, c), F32),
                   jax.ShapeDtypeStruct((bsz, RWKV_HEADS, RWKV_HEAD_DIM, RWKV_HEAD_DIM), F32)],
        scratch_shapes=[pltpu.VMEM((RWKV_HEADS // 2, LANES, LANES), F32),
                        pltpu.VMEM((1, wd), F32), pltpu.VMEM((rows, c), F32)],
        compiler_params=_cp(("parallel", "arbitrary")),
        name="rwkv_seq",
    )(pr, prev0, *params, ln_w.reshape(1, c), ln_b.reshape(1, c))


STEP_UNROLL = 8


def _rwkv_step_kernel(r_ref, ld_ref, kp_ref, v_ref, kk_ref, b_ref, s_ref, y_ref, s_out_ref):
    n = RWKV_HEAD_DIM
    neg_kk, decay = -kk_ref[...], jnp.exp(ld_ref[...])
    b_mat, kp_mat, r_mat = b_ref[...], kp_ref[...], r_ref[...]

    def body(i, carry):
        v0 = pl.multiple_of(i * STEP_UNROLL, STEP_UNROLL)
        v_rows = v_ref[pl.ds(v0, STEP_UNROLL), :]
        rows = range(STEP_UNROLL)
        s = [s_ref[v0 + j] for j in rows]
        sa = [jnp.sum(s[j] * neg_kk, axis=0, keepdims=True) for j in rows]
        s_new = [s[j] * decay + sa[j] * b_mat + v_rows[j:j + 1, :] * kp_mat for j in rows]
        y = [jnp.sum(s_new[j] * r_mat, axis=0, keepdims=True) for j in rows]
        for j in rows:
            s_out_ref[v0 + j] = s_new[j]
        y_ref[pl.ds(v0, STEP_UNROLL), :] = jnp.concatenate(y, axis=0)
        return carry

    lax.fori_loop(0, n // STEP_UNROLL, body, 0)


def rwkv_step(r, ld, kp, v, kk, b, state_t):
    _, nh, n, _, bsz = state_t.shape
    vec = pl.BlockSpec((n, bsz), lambda h: (h, 0))
    st = pl.BlockSpec((None, None, n, n, bsz), lambda h: (0, h, 0, 0, 0))
    return pl.pallas_call(
        _rwkv_step_kernel,
        grid=(nh,),
        in_specs=[vec] * 6 + [st],
        out_specs=[vec, st],
        out_shape=[jax.ShapeDtypeStruct((nh * n, bsz), F32),
                   jax.ShapeDtypeStruct(state_t.shape, F32)],
        compiler_params=_cp(("parallel",)),
        name="rwkv_step",
    )(r, ld, kp, v, kk, b, state_t)


def _rwkv_post_kernel(y_ref, bonus_ref, g_ref, lnw_ref, lnb_ref, o_ref, *, y_channel_major):
    gmat = _head_indicator()
    for c in range(o_ref.shape[1] // LANES):
        sl = slice(c * LANES, (c + 1) * LANES)
        y = y_ref[sl, :].T if y_channel_major else y_ref[:, sl]
        o_ref[:, sl] = _group_norm_gate(y, bonus_ref[:, sl], g_ref[:, sl], lnw_ref[:, sl],
                                        lnb_ref[:, sl], gmat)


def rwkv_post(y, bonus, g, ln_w, ln_b, *, tm, y_channel_major=False):
    m, c = bonus.shape
    blk = pl.BlockSpec((tm, c), lambda i: (i, 0))
    y_blk = pl.BlockSpec((c, tm), lambda i: (0, i)) if y_channel_major else blk
    vec = pl.BlockSpec((1, c), lambda i: (0, 0))
    return pl.pallas_call(
        functools.partial(_rwkv_post_kernel, y_channel_major=y_channel_major),
        grid=(m // tm,),
        in_specs=[y_blk, blk, blk, vec, vec],
        out_specs=blk,
        out_shape=jax.ShapeDtypeStruct((m, c), F32),
        compiler_params=_cp(("parallel",)),
        name="rwkv_post",
    )(y, bonus, g, ln_w.reshape(1, c), ln_b.reshape(1, c))


ROUTER_LANES = LANES
ROW_TILES = 1
ROW_LANES = D_MODEL // ROW_TILES


def _rows_to_tiles(ref, x):
    rows = x.shape[0]
    if ROW_TILES == 1:
        ref[...] = x
        return
    for j in range(ROW_TILES):
        ref[pl.ds(j, rows, stride=ROW_TILES), :] = x[:, j * ROW_LANES:(j + 1) * ROW_LANES]


def _tiles_to_rows(ref, rows):
    if ROW_TILES == 1:
        return ref[...]
    return jnp.concatenate(
        [ref[pl.ds(j, rows, stride=ROW_TILES), :] for j in range(ROW_TILES)], axis=1)


def _router_kernel(ha_ref, hb_ref, lnw_ref, whi_ref, wlo_ref, bias_ref, u_ref, idx_ref, gate_ref,
                   *, steps_a):
    use_a = pl.program_id(0) < steps_a
    h = jnp.where(use_a, ha_ref[...], hb_ref[...])
    u = _rms_rows(h, lnw_ref[...])
    _rows_to_tiles(u_ref, u)
    u_hi, u_lo = _split2(u)
    w_hi = whi_ref[...]
    logits = _dot(u_hi, w_hi) + _dot(u_lo, w_hi) + _dot(u_hi, wlo_ref[...]) + bias_ref[...]
    lane = lax.broadcasted_iota(jnp.int32, logits.shape, 1)
    neg = -jnp.inf

    def first_max(x):
        m = jnp.max(x, axis=1, keepdims=True)
        return m, jnp.min(jnp.where(x == m, lane, ROUTER_LANES), axis=1, keepdims=True)

    gl = jnp.where(lane < N_EXPERT_GROUPS, logits, neg)
    g_max, g_idx = first_max(gl)
    g_gate = 1.0 / jnp.sum(jnp.exp(gl - g_max), axis=1, keepdims=True)
    lo = N_EXPERT_GROUPS + g_idx * EXPERTS_PER_GROUP
    el = jnp.where((lane >= lo) & (lane < lo + EXPERTS_PER_GROUP), logits, neg)
    v1, i1 = first_max(el)
    v2, i2 = first_max(jnp.where(lane == i1, neg, el))
    e2 = jnp.exp(v2 - v1)
    w1 = g_gate / (1.0 + e2)
    w2 = g_gate * e2 / (1.0 + e2)
    idx_ref[...] = jnp.where(lane == 0, i1 - N_EXPERT_GROUPS,
                             jnp.where(lane == 1, i2 - N_EXPERT_GROUPS, 0))
    gate_ref[...] = jnp.where(lane == 0, w1, jnp.where(lane == 1, w2, 0.0))


def moe_router(h_a, h_b, ln_w, w_hi, w_lo, bias, *, tm):
    (ma, d), mb = h_a.shape, h_b.shape[0]
    assert ma % tm == 0 and mb % tm == 0
    steps_a, steps_b = ma // tm, mb // tm
    m = ma + mb
    const = lambda r, w: pl.BlockSpec((r, w), lambda i: (0, 0))
    row = lambda w: pl.BlockSpec((tm, w), lambda i: (i, 0))
    return pl.pallas_call(
        functools.partial(_router_kernel, steps_a=steps_a),
        grid=(steps_a + steps_b,),
        in_specs=[pl.BlockSpec((tm, d), lambda i: (jnp.minimum(i, steps_a - 1), 0)),
                  pl.BlockSpec((tm, d), lambda i: (jnp.maximum(i - steps_a, 0), 0)),
                  const(1, d), const(d, ROUTER_LANES), const(d, ROUTER_LANES),
                  const(1, ROUTER_LANES)],
        out_specs=[pl.BlockSpec((tm * ROW_TILES, ROW_LANES), lambda i: (i, 0)),
                   row(ROUTER_LANES), row(ROUTER_LANES)],
        out_shape=[jax.ShapeDtypeStruct((m * ROW_TILES, ROW_LANES), F32),
                   jax.ShapeDtypeStruct((m, ROUTER_LANES), jnp.int32),
                   jax.ShapeDtypeStruct((m, ROUTER_LANES), F32)],
        compiler_params=_cp(("arbitrary",)),
        name="moe_router",
    )(h_a, h_b, ln_w.reshape(1, d), w_hi, w_lo, bias)


X_SLOTS = 3
Y_SLOTS = 2


W_SLOTS = 3
DMA_QUEUES = 2


def _moe_expert_kernel(run_ref, rexp_ref, nused_ref, tok0_ref, tok1_ref, tok2_ref, dst_ref,
                       u_hbm, wg_hbm, wu_hbm, wd_hbm, y_hbm, xbuf, ybuf, wg_f, wu_f,
                       wd_f, wg_b, wu_b, wd_b, sem_in, sem_out, sem_w):
    i = pl.program_id(0)
    n_used, n_runs = nused_ref[0], nused_ref[1]
    tile_rows = MOE_BLOCK * ROW_TILES
    pad_base = y_hbm.shape[0] - Y_SLOTS * tile_rows
    run = run_ref[i]

    def weight_copies(k):
        e, s = rexp_ref[jnp.minimum(k, n_runs - 1)], lax.rem(k, W_SLOTS)
        copies = []
        for hbm, buf in ((wg_hbm, wg_f), (wu_hbm, wu_f), (wd_hbm, wd_f)):
            rows = buf.shape[1] // 2
            for part in range(2):
                sl = pl.ds(part * rows, rows)
                copies.append((pltpu.make_async_copy(hbm.at[e, sl], buf.at[s, sl], sem_w.at[s]),
                               part))
        return copies

    def gather_block(idx_ref, x_slot):
        for r in range(MOE_BLOCK):
            pltpu.make_async_copy(u_hbm.at[pl.ds(idx_ref[0, 0, r], ROW_TILES)],
                                  xbuf.at[x_slot, pl.ds(r * ROW_TILES, ROW_TILES)],
                                  sem_in.at[x_slot]).start(priority=r % DMA_QUEUES)

    def scatter_rows(idx_ref, y_slot, rows):
        for r in rows:
            pltpu.make_async_copy(ybuf.at[y_slot, pl.ds(r * ROW_TILES, ROW_TILES)],
                                  y_hbm.at[pl.ds(idx_ref[0, 0, r], ROW_TILES)],
                                  sem_out.at[y_slot]).start(priority=r % DMA_QUEUES)

    def gather_wait(x_slot):
        pltpu.make_async_copy(u_hbm.at[pl.ds(0, tile_rows)], xbuf.at[x_slot],
                              sem_in.at[x_slot]).wait()

    def scatter_wait(y_slot):
        pltpu.make_async_copy(ybuf.at[y_slot], y_hbm.at[pl.ds(0, tile_rows)],
                              sem_out.at[y_slot]).wait()

    @pl.when(i == 0)
    def _():
        ybuf[0] = jnp.zeros(ybuf.shape[1:], F32)
        for s in range(Y_SLOTS):
            pltpu.make_async_copy(ybuf.at[0], y_hbm.at[pl.ds(pad_base + s * tile_rows, tile_rows)],
                                  sem_out.at[s]).start()
        for k in range(W_SLOTS - 1):
            for cp, queue in weight_copies(k):
                cp.start(priority=queue)
        gather_block(tok0_ref, 0)
        gather_block(tok1_ref, 1)
        for s in range(Y_SLOTS):
            scatter_wait(s)

    @pl.when(i < n_used)
    def _():
        x_slot = lax.rem(i, X_SLOTS)
        y_slot = lax.rem(i, Y_SLOTS)

        @pl.when((i == 0) | (run != run_ref[jnp.maximum(i - 1, 0)]))
        def _():
            for cp, _ in weight_copies(run):
                cp.wait()
            w_slot = lax.rem(run, W_SLOTS)
            wg_b[...] = wg_f[w_slot].astype(BF16)
            wu_b[...] = wu_f[w_slot].astype(BF16)
            wd_b[...] = wd_f[w_slot].astype(BF16)
            for cp, queue in weight_copies(run + W_SLOTS - 1):
                cp.start(priority=queue)

        gather_wait(x_slot)

        @pl.when(i >= Y_SLOTS)
        def _():
            scatter_wait(y_slot)

        x = _tiles_to_rows(xbuf.at[x_slot], MOE_BLOCK).astype(BF16)
        hg = _dot(x, wg_b[...])
        hu = _dot(x, wu_b[...])
        act = (hg * _sigmoid(hg) * hu).astype(BF16)
        y = _dot(act, wd_b[...])
        _rows_to_tiles(ybuf.at[y_slot], y)
        scatter_rows(dst_ref, y_slot, range(MOE_BLOCK))
        gather_block(tok2_ref, lax.rem(i + 2, X_SLOTS))

        @pl.when(i == n_used - 1)
        def _():
            scatter_wait(y_slot)

            @pl.when(i >= 1)
            def _():
                scatter_wait(1 - y_slot)

            gather_wait(lax.rem(i + 1, X_SLOTS))
            gather_wait(lax.rem(i + 2, X_SLOTS))
            for k in range(W_SLOTS - 1):
                for cp, _ in weight_copies(n_runs + k):
                    cp.wait()


def moe_experts(u_all, row_src, row_dst, block_run, run_exp, n_used_runs, w_gate, w_up, w_down,
                n_assign):
    d, ff = w_gate.shape[1], w_gate.shape[2]
    n_blocks = row_src.shape[0]
    tile_rows = MOE_BLOCK * ROW_TILES
    smem_blk = lambda off: pl.BlockSpec(
        (1, 1, MOE_BLOCK), lambda i, *_: (jnp.clip(i + off, 0, n_blocks - 1), 0, 0),
        memory_space=pltpu.SMEM)
    hbm = pl.BlockSpec(memory_space=pl.ANY)
    grid_spec = pltpu.PrefetchScalarGridSpec(
        num_scalar_prefetch=3,
        grid=(n_blocks,),
        in_specs=[
            smem_blk(0), smem_blk(1), smem_blk(2), smem_blk(0),
            hbm, hbm, hbm, hbm,
        ],
        out_specs=hbm,
        scratch_shapes=[
            pltpu.VMEM((X_SLOTS, tile_rows, ROW_LANES), F32),
            pltpu.VMEM((Y_SLOTS, tile_rows, ROW_LANES), F32),
            pltpu.VMEM((W_SLOTS, d, ff), F32), pltpu.VMEM((W_SLOTS, d, ff), F32),
            pltpu.VMEM((W_SLOTS, ff, d), F32),
            pltpu.VMEM((d, ff), BF16), pltpu.VMEM((d, ff), BF16), pltpu.VMEM((ff, d), BF16),
            pltpu.SemaphoreType.DMA((X_SLOTS,)), pltpu.SemaphoreType.DMA((Y_SLOTS,)),
            pltpu.SemaphoreType.DMA((W_SLOTS,)),
        ],
    )
    y_rows = (n_assign + Y_SLOTS * MOE_BLOCK) * ROW_TILES
    return pl.pallas_call(
        _moe_expert_kernel,
        grid_spec=grid_spec,
        out_shape=jax.ShapeDtypeStruct((y_rows, ROW_LANES), F32),
        compiler_params=_cp(("arbitrary",), vmem=MOE_VMEM_LIMIT),
        name="moe_experts",
    )(block_run, run_exp, n_used_runs, row_src, row_src, row_src, row_dst, u_all,
      w_gate, w_up, w_down)


def _moe_combine_kernel(h_ref, y0_ref, y1_ref, gate_ref, o_ref):
    rows = h_ref.shape[0]
    gate = gate_ref[...]
    o_ref[...] = h_ref[...] + (_tiles_to_rows(y0_ref, rows) * gate[:, 0:1]
                               + _tiles_to_rows(y1_ref, rows) * gate[:, 1:2])


def moe_combine(h, y_slots, gates, row_off, slot_stride, *, tm):
    m, d = h.shape
    assert row_off % tm == 0 and slot_stride % tm == 0
    off0, off1 = row_off // tm, (row_off + slot_stride) // tm
    return pl.pallas_call(
        _moe_combine_kernel,
        grid=(m // tm,),
        in_specs=[pl.BlockSpec((tm, d), lambda i: (i, 0)),
                  pl.BlockSpec((tm * ROW_TILES, ROW_LANES), lambda i: (i + off0, 0)),
                  pl.BlockSpec((tm * ROW_TILES, ROW_LANES), lambda i: (i + off1, 0)),
                  pl.BlockSpec((tm, ROUTER_LANES), lambda i: (i + off0, 0))],
        out_specs=pl.BlockSpec((tm, d), lambda i: (i, 0)),
        out_shape=jax.ShapeDtypeStruct((m, d), F32),
        compiler_params=_cp(("parallel",)),
        name="moe_combine",
    )(h, y_slots, y_slots, gates)


def moe_dispatch(e_idx, slot_stride):
    m = e_idx.shape[0]
    a = m * TOP_K
    e_flat = e_idx.reshape(a)
    order = jnp.argsort(e_flat, stable=True).astype(jnp.int32)
    counts = jnp.sum(e_flat[:, None] == jnp.arange(N_EXPERTS, dtype=jnp.int32)[None, :],
                     axis=0, dtype=jnp.int32)
    pad_counts = (counts + MOE_BLOCK - 1) // MOE_BLOCK * MOE_BLOCK
    starts = jnp.cumsum(counts) - counts
    pad_ends = jnp.cumsum(pad_counts)
    pad_starts = pad_ends - pad_counts
    n_blocks = a // MOE_BLOCK + N_EXPERTS
    p = n_blocks * MOE_BLOCK
    n_used = (pad_ends[-1] // MOE_BLOCK).astype(jnp.int32)
    blk = jnp.arange(n_blocks, dtype=jnp.int32)
    blk_start = jnp.minimum(blk, n_used - 1) * MOE_BLOCK
    block_exp = jnp.minimum(jnp.sum(blk_start[:, None] >= pad_ends[None, :], axis=1),
                            N_EXPERTS - 1).astype(jnp.int32)
    in_exp = blk * MOE_BLOCK - pad_starts[block_exp]
    row_cnt = jnp.where(blk < n_used, jnp.clip(counts[block_exp] - in_exp, 0, MOE_BLOCK), 0)
    lane = jnp.arange(MOE_BLOCK, dtype=jnp.int32)[None, :]
    valid = lane < row_cnt[:, None]
    src = jnp.clip((starts[block_exp] + in_exp)[:, None] + lane, 0, a - 1)
    assign = order[src]
    row_tok = jnp.where(valid, assign // TOP_K, 0)
    pad_dst = TOP_K * slot_stride + (blk % Y_SLOTS)[:, None] * MOE_BLOCK + lane
    row_dst = jnp.where(valid, (assign % TOP_K) * slot_stride + assign // TOP_K, pad_dst)
    as_blocks = lambda x: (x * ROW_TILES).astype(jnp.int32).reshape(n_blocks, 1, MOE_BLOCK)
    has_rows = counts > 0
    run_exp = jnp.argsort(~has_rows, stable=True).astype(jnp.int32)
    block_run = (jnp.cumsum(has_rows) - 1)[block_exp].astype(jnp.int32)
    n_used_runs = jnp.stack([n_used, jnp.sum(has_rows, dtype=jnp.int32)])
    return as_blocks(row_tok), as_blocks(row_dst), block_run, run_exp, n_used_runs


def rwkv_params(rw_mu, rw_w0, rw_w2, rw_a0, rw_a2, rw_g2, rw_k_k, rw_k_a, rw_r_k):
    c = RWKV_WIDTH
    mu = jnp.pad(rw_mu, (0, RWKV_PROJ_PAD - RWKV_PROJ)).reshape(1, RWKV_PROJ_PAD)
    w_lora = jnp.zeros((LANES, 2 * c), F32)
    w_lora = w_lora.at[0:DECAY_LORA, 0:c].set(rw_w2).at[DECAY_LORA:LANES, c:2 * c].set(rw_a2)
    w_gate = jnp.pad(rw_g2, ((0, GATE_PAD - GATE_LORA), (0, 0)))
    vec = lambda x: x.reshape(1, c)
    return (mu, vec(rw_w0), vec(rw_a0), vec(rw_k_k), vec(rw_k_a), vec(rw_r_k),
            w_lora.astype(BF16), w_gate.astype(BF16))


def _token_tiles(m):
    return (1024, 512) if m % 1024 == 0 else (m, m)


def _dense_front(x2d, wts, tm):
    pa = norm_matmul(x2d, wts['ln1_w'], wts['wt_att'], tm=tm, tn=ATT_PROJ // 2, w_transposed=True)
    pr = norm_matmul(x2d, wts['ln1_w'], wts['wt_rw'], tm=tm, tn=RWKV_PROJ_PAD // 3,
                     w_transposed=True)
    return pa, pr


def _dense_back(x2d, att2d, rw2d, wts, xattn_fn, tm):
    h1 = matmul_residual([att2d, rw2d], [wts['w_out_a'], wts['w_out_r']], x2d, tm=tm, tn=512)
    qx = norm_matmul(h1, wts['ln2_w'], wts['xq_w'], tm=tm, tn=XATT_WIDTH)
    ox = xattn_fn(qx)
    return matmul_residual([ox], [wts['xo_w']], h1, tm=tm, tn=512)


def kernel(x_prompt, x_sample, cache_win_k, cache_win_v, state_wkv, state_shift, cache_mem_k, cache_mem_v, mem_prompt, ln1_w, w_in, q_norm_w, k_norm_w, attn_sinks, rw_mu, rw_w0, rw_w2, rw_a0, rw_a2, rw_g2, rw_k_k, rw_k_a, rw_r_k, rw_ln_w, rw_ln_b, w_out, ln2_w, mem_norm_w, xq_w, xkv_w, xq_norm_w, xk_norm_w, xo_w, ln3_w, router_group_w, router_group_b, router_expert_w, router_expert_b, exp_w_gate, exp_w_up, exp_w_down):
    assert w_in.shape[0] == 1, "single-layer stack"
    bp, seq, d = x_prompt.shape
    bs = x_sample.shape[0]
    mp = bp * seq
    c = RWKV_WIDTH

    router_w = jnp.concatenate(
        [router_group_w[0], router_expert_w[0],
         jnp.zeros((d, ROUTER_LANES - N_EXPERT_GROUPS - N_EXPERTS), F32)], axis=1)
    router_hi = router_w.astype(BF16)
    wts = {
        'ln1_w': ln1_w[0], 'ln2_w': ln2_w[0], 'ln3_w': ln3_w[0],
        'wt_att': cast_rows_bf16(w_in[0].T, 0, ATT_PROJ, ATT_PROJ),
        'wt_rw': cast_rows_bf16(w_in[0].T, ATT_PROJ, RWKV_PROJ, RWKV_PROJ_PAD),
        'w_out_a': w_out[0][:ATT_WIDTH].astype(BF16),
        'w_out_r': w_out[0][ATT_WIDTH:].astype(BF16),
        'xq_w': xq_w[0].astype(BF16), 'xo_w': xo_w[0].astype(BF16),
        'router_hi': router_hi,
        'router_lo': (router_w - router_hi.astype(F32)).astype(BF16),
        'router_b': jnp.pad(jnp.concatenate([router_group_b[0], router_expert_b[0]]),
                            (0, ROUTER_LANES - N_EXPERT_GROUPS - N_EXPERTS)).reshape(1, -1),
    }
    rw_par = rwkv_params(rw_mu[0], rw_w0[0], rw_w2[0], rw_a0[0], rw_a2[0], rw_g2[0],
                         rw_k_k[0], rw_k_a[0], rw_r_k[0])

    tm_p, te_p = _token_tiles(mp)
    xp = x_prompt.reshape(mp, d)
    pa, pr = _dense_front(xp, wts, tm_p)
    pa3 = pa.reshape(bp, seq, ATT_PROJ)
    pr3 = pr.reshape(bp, seq, RWKV_PROJ_PAD)
    tabs_p = rope_tables(np.arange(seq))
    att_p, kn_p = swa_prompt(pa3, tabs_p, q_norm_w[0], k_norm_w[0], attn_sinks[0])
    rw_p, wkv_p = rwkv_seq(pr3, jnp.zeros((bp, 1, RWKV_PROJ_PAD), F32), rw_par,
                           rw_ln_w[0], rw_ln_b[0])
    rw_p = rw_p.reshape(mp, c)

    n_mem = mem_prompt.shape[1]
    kv_mem = norm_matmul(mem_prompt.reshape(bp * n_mem, d), mem_norm_w[0],
                         xkv_w[0].astype(BF16), tm=bp * n_mem, tn=512)
    mem_k = head_rms(kv_mem[:, :XATT_WIDTH], xk_norm_w[0])
    mem_v = kv_mem[:, XATT_WIDTH:]
    mem_k3 = mem_k.reshape(bp, n_mem, XATT_WIDTH)
    mem_v3 = mem_v.reshape(bp, n_mem, XATT_WIDTH)

    def xattn_p(qx):
        return xattn_prompt(qx.reshape(bp, seq, XATT_WIDTH), mem_k3, mem_v3,
                            xq_norm_w[0]).reshape(mp, XATT_WIDTH)

    h2_p = _dense_back(xp, att_p.reshape(mp, ATT_WIDTH), rw_p, wts, xattn_p, tm_p)

    tm_s, te_s = _token_tiles(bs)
    xs = x_sample.reshape(bs, d)
    sa, sr = _dense_front(xs, wts, tm_s)
    tabs_s = rope_tables(PAST_LEN + np.arange(1))
    qk_w = jnp.concatenate([jnp.tile(q_norm_w[0], ATT_HEADS),
                            jnp.tile(k_norm_w[0], ATT_KV_HEADS)]).reshape(1, -1)
    qk = qk_norm_rope(sa[:, :ATT_WIDTH + KV_WIDTH], qk_w, tabs_s)
    nbuf = cache_win_k.shape[2]

    def feature_major(cache):
        return jnp.transpose(cache, (0, 1, 3, 4, 2)).reshape(bs, KV_WIDTH, nbuf)

    def position_major(win):
        return jnp.transpose(win.reshape(1, bs, ATT_KV_HEADS, HEAD_DIM, nbuf), (0, 1, 4, 2, 3))

    att_s, win_k, win_v = swa_decode(
        qk[:, :ATT_WIDTH].reshape(bs, ATT_HEADS, HEAD_DIM),
        qk[:, ATT_WIDTH:].reshape(bs, 1, KV_WIDTH),
        sa[:, ATT_WIDTH + KV_WIDTH:].reshape(bs, 1, KV_WIDTH),
        feature_major(cache_win_k), feature_major(cache_win_v), attn_sinks[0])
    shift_prev = jnp.pad(state_shift[0], ((0, 0), (0, RWKV_PROJ_PAD - RWKV_PROJ)))
    r, ld, kp, v, kk, b, bonus, g = rwkv_prep_tok(sr, shift_prev, rw_par)
    y_s, wkv_s = rwkv_step(r, ld, kp, v, kk, b, jnp.transpose(state_wkv, (0, 2, 3, 4, 1)))
    wkv_s = jnp.transpose(wkv_s, (0, 4, 1, 2, 3))
    rw_s = rwkv_post(y_s, bonus, g, rw_ln_w[0], rw_ln_b[0], tm=te_s, y_channel_major=True)
    def xattn_s(qx):
        q_pad = jnp.pad(qx.reshape(bs, XATT_HEADS, XATT_HEAD_DIM), ((0, 0), (0, 4), (0, 0)))
        rows_of = lambda c: c.reshape(bs, n_mem * XATT_HEADS, XATT_HEAD_DIM)
        o = xattn_decode(q_pad, rows_of(cache_mem_k), rows_of(cache_mem_v), xq_norm_w[0])
        return o[:, :XATT_HEADS].reshape(bs, XATT_WIDTH)

    h2_s = _dense_back(xs, att_s.reshape(bs, ATT_WIDTH), rw_s, wts, xattn_s, tm_s)

    m_all = mp + bs
    slot_stride = m_all
    tc = math.gcd(mp, bs, 512)
    tr = te_p
    h2_s_pad = jnp.pad(h2_s, ((0, -bs % tr), (0, 0)))
    u_all, idx_all, gate_all = moe_router(h2_p, h2_s_pad, wts['ln3_w'], wts['router_hi'],
                                          wts['router_lo'], wts['router_b'], tm=tr)
    row_src, row_dst, block_run, run_exp, n_used_runs = moe_dispatch(
        idx_all[:m_all, :TOP_K], slot_stride)
    y_slots = moe_experts(u_all, row_src, row_dst, block_run, run_exp, n_used_runs,
                          exp_w_gate[0], exp_w_up[0], exp_w_down[0], TOP_K * slot_stride)
    out_p = moe_combine(h2_p, y_slots, gate_all, 0, slot_stride, tm=tc)
    out_s = moe_combine(h2_s, y_slots, gate_all, mp, slot_stride, tm=tc)

    win = min(WINDOW, seq)
    kv_shape = (1, bp, win, ATT_KV_HEADS, HEAD_DIM)
    return (
        out_p.reshape(bp, seq, d),
        out_s.reshape(bs, 1, d),
        kn_p[:, seq - win:].reshape(kv_shape),
        pa3[:, seq - win:, ATT_WIDTH + KV_WIDTH:].reshape(kv_shape),
        wkv_p[None],
        pr3[:, seq - 1, :RWKV_PROJ][None],
        mem_k3.reshape(1, bp, n_mem, XATT_HEADS, XATT_HEAD_DIM),
        mem_v3.reshape(1, bp, n_mem, XATT_HEADS, XATT_HEAD_DIM),
        position_major(win_k),
        position_major(win_v),
        wkv_s,
        sr[:, :RWKV_PROJ].reshape(1, bs, RWKV_PROJ),
    )
```

```python
import functools
import math

import numpy as np
import jax
import jax.numpy as jnp
from jax import lax
from jax.experimental import pallas as pl
from jax.experimental.pallas import tpu as pltpu

F32 = jnp.float32
BF16 = jnp.bfloat16

D_MODEL = 2048
HEAD_DIM = 64
ATT_HEADS = 16
ATT_KV_HEADS = 4
ATT_GROUP = ATT_HEADS // ATT_KV_HEADS
ATT_WIDTH = ATT_HEADS * HEAD_DIM
KV_WIDTH = ATT_KV_HEADS * HEAD_DIM
ATT_PROJ = ATT_WIDTH + 2 * KV_WIDTH
WINDOW = 128
ATT_SCALE = HEAD_DIM ** -0.5
ROPE_THETA = 500000.0
ROT_DIM = HEAD_DIM // 4
PAST_LEN = 16384

RWKV_WIDTH = 1024
RWKV_HEAD_DIM = 64
RWKV_HEADS = 16
DECAY_LORA = 64
AAA_LORA = 64
GATE_LORA = 160
RWKV_PROJ = 3 * RWKV_WIDTH + DECAY_LORA + AAA_LORA + GATE_LORA
RWKV_PROJ_PAD = 3456

N_MEM = 256
XATT_HEADS = 4
XATT_HEAD_DIM = 128
XATT_WIDTH = XATT_HEADS * XATT_HEAD_DIM

N_EXPERT_GROUPS = 8
EXPERTS_PER_GROUP = 8
N_EXPERTS = 64
TOP_K = 2
EXPERT_FF = D_MODEL // 4
MOE_BLOCK = 128

RMS_EPS = 1e-6
GN_EPS = 64e-5

LANES = 128
CHUNK = 64
VMEM_LIMIT = 56 * 1024 * 1024
MOE_VMEM_LIMIT = 60 * 1024 * 1024


def _cp(sem, vmem=VMEM_LIMIT):
    return pltpu.CompilerParams(dimension_semantics=sem, vmem_limit_bytes=vmem)


def _rms_rows(x, w):
    ms = jnp.mean(x * x, axis=-1, keepdims=True)
    return x * lax.rsqrt(ms + RMS_EPS) * w


def _split2(x):
    hi = x.astype(BF16)
    lo = (x - hi.astype(F32)).astype(BF16)
    return hi, lo


def _split3(x):
    h1 = x.astype(BF16)
    r1 = x - h1.astype(F32)
    h2 = r1.astype(BF16)
    h3 = (r1 - h2.astype(F32)).astype(BF16)
    return h1, h2, h3


def _dot(a, b):
    return jnp.dot(a, b, preferred_element_type=F32)


def _dot_nt(a, b):
    return lax.dot_general(a, b, (((1,), (1,)), ((), ())), preferred_element_type=F32)


def _group_sum(x, gmat):
    hi, lo = _split2(x)
    return _dot(hi, gmat) + _dot(lo, gmat)


def _head_indicator():
    r = lax.broadcasted_iota(jnp.int32, (LANES, LANES), 0) // HEAD_DIM
    c = lax.broadcasted_iota(jnp.int32, (LANES, LANES), 1) // HEAD_DIM
    return jnp.where(r == c, 1.0, 0.0).astype(BF16)


def _norm_mm_kernel(x_ref, lnw_ref, w_ref, o_ref, xn_ref, *, w_transposed):
    @pl.when(pl.program_id(1) == 0)
    def _():
        xn_ref[...] = _rms_rows(x_ref[...], lnw_ref[...]).astype(BF16)

    o_ref[...] = (_dot_nt if w_transposed else _dot)(xn_ref[...], w_ref[...])


def norm_matmul(x, ln_w, w_bf16, *, tm, tn, w_transposed=False):
    m, k = x.shape
    n = w_bf16.shape[0 if w_transposed else 1]
    assert m % tm == 0 and n % tn == 0
    w_spec = (pl.BlockSpec((tn, k), lambda i, j: (j, 0)) if w_transposed
              else pl.BlockSpec((k, tn), lambda i, j: (0, j)))
    return pl.pallas_call(
        functools.partial(_norm_mm_kernel, w_transposed=w_transposed),
        grid=(m // tm, n // tn),
        in_specs=[
            pl.BlockSpec((tm, k), lambda i, j: (i, 0)),
            pl.BlockSpec((1, k), lambda i, j: (0, 0)),
            w_spec,
        ],
        out_specs=pl.BlockSpec((tm, tn), lambda i, j: (i, j)),
        out_shape=jax.ShapeDtypeStruct((m, n), F32),
        scratch_shapes=[pltpu.VMEM((tm, k), BF16)],
        compiler_params=_cp(("parallel", "arbitrary")),
        name="norm_matmul",
    )(x, ln_w.reshape(1, k), w_bf16)


def _cast_rows_kernel(w_ref, o_ref, *, n_in):
    @pl.when(pl.program_id(0) < n_in)
    def _():
        o_ref[...] = w_ref[...].astype(BF16)

    @pl.when(pl.program_id(0) >= n_in)
    def _():
        o_ref[...] = jnp.zeros_like(o_ref)


def cast_rows_bf16(w, first_row, n_rows, n_rows_out, *, rows=96):
    k = w.shape[1]
    assert first_row % rows == 0 and n_rows % rows == 0 and n_rows_out % rows == 0
    first, n_in = first_row // rows, n_rows // rows
    return pl.pallas_call(
        functools.partial(_cast_rows_kernel, n_in=n_in),
        grid=(n_rows_out // rows,),
        in_specs=[pl.BlockSpec((rows, k), lambda i: (first + jnp.minimum(i, n_in - 1), 0))],
        out_specs=pl.BlockSpec((rows, k), lambda i: (i, 0)),
        out_shape=jax.ShapeDtypeStruct((n_rows_out, k), BF16),
        compiler_params=_cp(("parallel",)),
        name="cast_rows_bf16",
    )(w)


def _mm_res_kernel(*refs, n_lhs):
    a_refs = refs[:n_lhs]
    w_refs = refs[n_lhs:2 * n_lhs]
    res_ref = refs[2 * n_lhs]
    o_ref = refs[2 * n_lhs + 1]
    acc = res_ref[...]
    for a_ref, w_ref in zip(a_refs, w_refs):
        acc = acc + _dot(a_ref[...].astype(BF16), w_ref[...])
    o_ref[...] = acc


def matmul_residual(lhs_list, w_list, res, *, tm, tn):
    m, n = res.shape
    n_lhs = len(lhs_list)
    assert m % tm == 0 and n % tn == 0
    in_specs = [pl.BlockSpec((tm, a.shape[1]), lambda i, j: (i, 0)) for a in lhs_list]
    in_specs += [pl.BlockSpec((w.shape[0], tn), lambda i, j: (0, j)) for w in w_list]
    in_specs += [pl.BlockSpec((tm, tn), lambda i, j: (i, j))]
    return pl.pallas_call(
        functools.partial(_mm_res_kernel, n_lhs=n_lhs),
        grid=(m // tm, n // tn),
        in_specs=in_specs,
        out_specs=pl.BlockSpec((tm, tn), lambda i, j: (i, j)),
        out_shape=jax.ShapeDtypeStruct((m, n), F32),
        compiler_params=_cp(("parallel", "arbitrary")),
        name="matmul_residual",
    )(*lhs_list, *w_list, res)


def rope_tables(pos):
    half = ROT_DIM // 2
    f32 = np.float32
    inv = f32(ROPE_THETA) ** (-np.arange(half, dtype=f32) * f32(2.0) / f32(ROT_DIM))
    ang = (np.asarray(pos, f32)[:, None] * inv[None, :].astype(f32)).astype(f32)
    cos, sin = np.cos(ang.astype(np.float64)).astype(f32), np.sin(ang.astype(np.float64)).astype(f32)
    t = ang.shape[0]
    ones = np.ones((t, HEAD_DIM - ROT_DIM), f32)
    zeros = np.zeros((t, HEAD_DIM - ROT_DIM), f32)
    z8 = np.zeros((t, half), f32)
    cos_t = np.concatenate([cos, cos, ones], axis=1)
    sin_a = np.concatenate([z8, sin, zeros], axis=1)
    sin_b = np.concatenate([-sin, z8, zeros], axis=1)
    return tuple(jnp.asarray(np.concatenate([a, a], axis=1)) for a in (cos_t, sin_a, sin_b))


def _norm_rope_chunk(x, w, cos_t, sin_a, sin_b, gmat):
    ms = _group_sum(x * x, gmat) * (1.0 / HEAD_DIM)
    xn = x * lax.rsqrt(ms + RMS_EPS) * w
    half = ROT_DIM // 2
    return (xn * cos_t + pltpu.roll(xn, half, axis=1) * sin_a
            + pltpu.roll(xn, LANES - half, axis=1) * sin_b)


def _norm_rope(x, w, tabs, gmat):
    chunks = [
        _norm_rope_chunk(x[:, c * LANES:(c + 1) * LANES], w, *tabs, gmat)
        for c in range(x.shape[1] // LANES)
    ]
    return chunks[0] if len(chunks) == 1 else jnp.concatenate(chunks, axis=1)


def _sink_softmax(s, sink):
    m = jnp.maximum(jnp.max(s, axis=-1, keepdims=True), sink)
    e = jnp.exp(s - m)
    return e / (jnp.sum(e, axis=-1, keepdims=True) + jnp.exp(sink - m))


def _swa_prompt_kernel(q_ref, kc_ref, vc_ref, kp_ref, vp_ref, cc_ref, sac_ref, sbc_ref,
                       cp_ref, sap_ref, sbp_ref, qw_ref, kw_ref, sink_ref, o_ref, kn_ref):
    n = pl.program_id(1)
    blk = q_ref.shape[0]
    gmat = _head_indicator()
    tabs_c = (cc_ref[...], sac_ref[...], sbc_ref[...])
    tabs_p = (cp_ref[...], sap_ref[...], sbp_ref[...])
    q = _norm_rope(q_ref[...], qw_ref[...], tabs_c, gmat)
    k_cur = _norm_rope(kc_ref[...], kw_ref[...], tabs_c, gmat)
    k_prev = _norm_rope(kp_ref[...], kw_ref[...], tabs_p, gmat)
    kn_ref[...] = k_cur
    k_all = jnp.concatenate([k_prev, k_cur], axis=0).astype(BF16)
    v_all = jnp.concatenate([vp_ref[...], vc_ref[...]], axis=0).astype(BF16)

    qi = lax.broadcasted_iota(jnp.int32, (blk, 2 * blk), 0) + blk
    si = lax.broadcasted_iota(jnp.int32, (blk, 2 * blk), 1)
    rel = qi - si
    valid = (rel >= 0) & (rel <= WINDOW) & ((n > 0) | (si >= blk))

    groups = range(ATT_KV_HEADS)
    lanes = [slice(kv * HEAD_DIM, (kv + 1) * HEAD_DIM) for kv in groups]
    heads = [[kv * ATT_GROUP + g for g in range(ATT_GROUP)] for kv in groups]
    q_g = [jnp.concatenate([q[:, h * HEAD_DIM:(h + 1) * HEAD_DIM] for h in heads[kv]],
                           axis=0).astype(BF16) for kv in groups]
    s = [_dot_nt(q_g[kv], k_all[:, lanes[kv]]) * ATT_SCALE for kv in groups]
    p = [jnp.concatenate(
        [_sink_softmax(jnp.where(valid, s[kv][g * blk:(g + 1) * blk], -jnp.inf), sink_ref[h])
         for g, h in enumerate(heads[kv])], axis=0).astype(BF16) for kv in groups]
    o = [_dot(p[kv], v_all[:, lanes[kv]]) for kv in groups]
    for kv in groups:
        for g, h in enumerate(heads[kv]):
            o_ref[:, h * HEAD_DIM:(h + 1) * HEAD_DIM] = o[kv][g * blk:(g + 1) * blk]


def swa_prompt(pa, tabs, q_norm_w, k_norm_w, sinks):
    b, t, _ = pa.shape
    blk = WINDOW
    nb = t // blk
    qb, kb, vb = 0, ATT_WIDTH // KV_WIDTH, ATT_WIDTH // KV_WIDTH + 1
    cur = lambda i, n, *_: (i, n, 0)
    tab_cur = pl.BlockSpec((blk, LANES), lambda i, n: (n, 0))
    tab_prev = pl.BlockSpec((blk, LANES), lambda i, n: (jnp.maximum(n - 1, 0), 0))
    qw = jnp.tile(q_norm_w.reshape(1, HEAD_DIM), (1, 2))
    kw = jnp.tile(k_norm_w.reshape(1, HEAD_DIM), (1, 2))
    return pl.pallas_call(
        _swa_prompt_kernel,
        grid=(b, nb),
        in_specs=[
            pl.BlockSpec((None, blk, ATT_WIDTH), lambda i, n: (i, n, qb)),
            pl.BlockSpec((None, blk, KV_WIDTH), lambda i, n: (i, n, kb)),
            pl.BlockSpec((None, blk, KV_WIDTH), lambda i, n: (i, n, vb)),
            pl.BlockSpec((None, blk, KV_WIDTH), lambda i, n: (i, jnp.maximum(n - 1, 0), kb)),
            pl.BlockSpec((None, blk, KV_WIDTH), lambda i, n: (i, jnp.maximum(n - 1, 0), vb)),
            tab_cur, tab_cur, tab_cur, tab_prev, tab_prev, tab_prev,
            pl.BlockSpec((1, LANES), lambda i, n: (0, 0)),
            pl.BlockSpec((1, LANES), lambda i, n: (0, 0)),
            pl.BlockSpec(memory_space=pltpu.SMEM),
        ],
        out_specs=[
            pl.BlockSpec((None, blk, ATT_WIDTH), cur),
            pl.BlockSpec((None, blk, KV_WIDTH), cur),
        ],
        out_shape=[
            jax.ShapeDtypeStruct((b, t, ATT_WIDTH), F32),
            jax.ShapeDtypeStruct((b, t, KV_WIDTH), F32),
        ],
        compiler_params=_cp(("parallel", "arbitrary")),
        name="swa_prompt",
    )(pa, pa, pa, pa, pa, *tabs, *tabs, qw, kw, sinks)


def _qk_norm_rope_kernel(x_ref, w_ref, c_ref, sa_ref, sb_ref, o_ref):
    gmat = _head_indicator()
    tabs = (c_ref[...], sa_ref[...], sb_ref[...])
    for c in range(x_ref.shape[1] // LANES):
        sl = slice(c * LANES, (c + 1) * LANES)
        o_ref[:, sl] = _norm_rope_chunk(x_ref[:, sl], w_ref[:, sl], *tabs, gmat)


def qk_norm_rope(x, w_row, tabs):
    m, w = x.shape
    full = lambda *shape: pl.BlockSpec(shape, lambda: (0,) * len(shape))
    return pl.pallas_call(
        _qk_norm_rope_kernel,
        in_specs=[full(m, w), full(1, w), full(1, LANES), full(1, LANES), full(1, LANES)],
        out_specs=full(m, w),
        out_shape=jax.ShapeDtypeStruct((m, w), F32),
        name="qk_norm_rope",
    )(x, w_row, *tabs)


def _swa_decode_kernel(q_ref, kn_ref, vn_ref, knt_ref, vnt_ref, ck_ref, cv_ref, sink_ref, o_ref,
                       kw_ref, vw_ref):
    bb = q_ref.shape[0]
    nbuf = ck_ref.shape[2]
    row_kv = lax.broadcasted_iota(jnp.int32, (ATT_HEADS, KV_WIDTH), 0) // ATT_GROUP
    lane_kv = lax.broadcasted_iota(jnp.int32, (ATT_HEADS, KV_WIDTH), 1) // HEAD_DIM
    own = row_kv == lane_kv
    sink = sink_ref[...]
    seq = lax.broadcasted_iota(jnp.int32, knt_ref.shape, 1)
    pos = lax.broadcasted_iota(jnp.int32, (KV_WIDTH, nbuf), 1)
    b0 = pl.program_id(0) * bb
    for b in range(bb):
        q2 = q_ref[b]
        q_exp = jnp.where(own, jnp.concatenate([q2] * ATT_KV_HEADS, axis=1), 0.0)
        k_new, v_new = kn_ref[b], vn_ref[b]
        k_buf, v_buf = ck_ref[b], cv_ref[b]
        s_buf = _dot(q_exp.astype(BF16), k_buf.astype(BF16)) * ATT_SCALE
        s_new = jnp.sum(q_exp * k_new, axis=-1, keepdims=True) * ATT_SCALE
        m = jnp.maximum(jnp.maximum(jnp.max(s_buf, axis=-1, keepdims=True), s_new), sink)
        e_buf = jnp.exp(s_buf - m)
        e_new = jnp.exp(s_new - m)
        inv = 1.0 / (jnp.sum(e_buf, axis=-1, keepdims=True) + e_new + jnp.exp(sink - m))
        o = _dot_nt((e_buf * inv).astype(BF16), v_buf.astype(BF16)) + (e_new * inv) * v_new
        o = jnp.where(own, o, 0.0)
        o_ref[b] = (o[:, 0:HEAD_DIM] + o[:, HEAD_DIM:2 * HEAD_DIM]
                    + o[:, 2 * HEAD_DIM:3 * HEAD_DIM] + o[:, 3 * HEAD_DIM:4 * HEAD_DIM])
        k_col = jnp.sum(jnp.where(seq == b0 + b, knt_ref[...], 0.0), axis=1, keepdims=True)
        v_col = jnp.sum(jnp.where(seq == b0 + b, vnt_ref[...], 0.0), axis=1, keepdims=True)
        kw_ref[b] = jnp.where(pos == nbuf - 1, k_col, pltpu.roll(k_buf, nbuf - 1, axis=1))
        vw_ref[b] = jnp.where(pos == nbuf - 1, v_col, pltpu.roll(v_buf, nbuf - 1, axis=1))


def swa_decode(q, k_new, v_new, cache_kt, cache_vt, sinks, *, bb=8):
    b, _, nbuf = cache_kt.shape
    blk3 = lambda s1, s2: pl.BlockSpec((bb, s1, s2), lambda i: (i, 0, 0))
    whole = pl.BlockSpec((KV_WIDTH, b), lambda i: (0, 0))
    return pl.pallas_call(
        _swa_decode_kernel,
        grid=(b // bb,),
        in_specs=[
            blk3(ATT_HEADS, HEAD_DIM), blk3(1, KV_WIDTH), blk3(1, KV_WIDTH), whole, whole,
            blk3(KV_WIDTH, nbuf), blk3(KV_WIDTH, nbuf),
            pl.BlockSpec((ATT_HEADS, 1), lambda i: (0, 0)),
        ],
        out_specs=[blk3(ATT_HEADS, HEAD_DIM), blk3(KV_WIDTH, nbuf), blk3(KV_WIDTH, nbuf)],
        out_shape=[
            jax.ShapeDtypeStruct((b, ATT_HEADS, HEAD_DIM), F32),
            jax.ShapeDtypeStruct((b, KV_WIDTH, nbuf), F32),
            jax.ShapeDtypeStruct((b, KV_WIDTH, nbuf), F32),
        ],
        compiler_params=_cp(("parallel",)),
        name="swa_decode",
    )(q, k_new, v_new, k_new.reshape(b, KV_WIDTH).T, v_new.reshape(b, KV_WIDTH).T,
      cache_kt, cache_vt, sinks.reshape(ATT_HEADS, 1))


def _head_rms_kernel(x_ref, w_ref, o_ref):
    for h in range(x_ref.shape[1] // XATT_HEAD_DIM):
        sl = slice(h * XATT_HEAD_DIM, (h + 1) * XATT_HEAD_DIM)
        o_ref[:, sl] = _rms_rows(x_ref[:, sl], w_ref[...])


def head_rms(x, w):
    m, wd = x.shape
    return pl.pallas_call(
        _head_rms_kernel,
        in_specs=[pl.BlockSpec((m, wd), lambda: (0, 0)),
                  pl.BlockSpec((1, XATT_HEAD_DIM), lambda: (0, 0))],
        out_specs=pl.BlockSpec((m, wd), lambda: (0, 0)),
        out_shape=jax.ShapeDtypeStruct((m, wd), F32),
        name="head_rms",
    )(x, w.reshape(1, XATT_HEAD_DIM))


def _xattn_prompt_kernel(q_ref, k_ref, v_ref, w_ref, o_ref):
    scale = 1.0 / math.sqrt(XATT_HEAD_DIM)
    for h in range(XATT_HEADS):
        sl = slice(h * XATT_HEAD_DIM, (h + 1) * XATT_HEAD_DIM)
        qn = _rms_rows(q_ref[:, sl], w_ref[...]).astype(BF16)
        s = _dot_nt(qn, k_ref[:, sl].astype(BF16)) * scale
        e = jnp.exp(s - jnp.max(s, axis=-1, keepdims=True))
        p = e / jnp.sum(e, axis=-1, keepdims=True)
        o_ref[:, sl] = _dot(p.astype(BF16), v_ref[:, sl].astype(BF16))


def xattn_prompt(q, mem_k, mem_v, xq_norm_w, *, tq=512):
    b, t, w = q.shape
    n_mem = mem_k.shape[1]
    return pl.pallas_call(
        _xattn_prompt_kernel,
        grid=(b, t // tq),
        in_specs=[
            pl.BlockSpec((None, tq, w), lambda i, j: (i, j, 0)),
            pl.BlockSpec((None, n_mem, w), lambda i, j: (i, 0, 0)),
            pl.BlockSpec((None, n_mem, w), lambda i, j: (i, 0, 0)),
            pl.BlockSpec((1, XATT_HEAD_DIM), lambda i, j: (0, 0)),
        ],
        out_specs=pl.BlockSpec((None, tq, w), lambda i, j: (i, j, 0)),
        out_shape=jax.ShapeDtypeStruct((b, t, w), F32),
        compiler_params=_cp(("parallel", "arbitrary")),
        name="xattn_prompt",
    )(q, mem_k, mem_v, xq_norm_w.reshape(1, XATT_HEAD_DIM))


def _xattn_decode_kernel(q_ref, k_ref, v_ref, w_ref, o_ref):
    bb, rows, _ = q_ref.shape
    n_keys = k_ref.shape[1]
    scale = 1.0 / math.sqrt(XATT_HEAD_DIM)
    own = (lax.broadcasted_iota(jnp.int32, (rows, n_keys), 1) % XATT_HEADS
           == lax.broadcasted_iota(jnp.int32, (rows, n_keys), 0) % XATT_HEADS)
    seqs = range(bb)
    qn = [_rms_rows(q_ref[b], w_ref[...]).astype(BF16) for b in seqs]
    s = [jnp.where(own, _dot_nt(qn[b], k_ref[b].astype(BF16)) * scale, -jnp.inf) for b in seqs]
    e = [jnp.exp(s[b] - jnp.max(s[b], axis=-1, keepdims=True)) for b in seqs]
    p = [(e[b] / jnp.sum(e[b], axis=-1, keepdims=True)).astype(BF16) for b in seqs]
    for b in seqs:
        o_ref[b] = _dot(p[b], v_ref[b].astype(BF16))


def xattn_decode(q_pad, mem_k, mem_v, xq_norm_w, *, bb=8):
    b, rows, _ = q_pad.shape
    n_keys = mem_k.shape[1]
    kv = pl.BlockSpec((bb, n_keys, XATT_HEAD_DIM), lambda i: (i, 0, 0))
    return pl.pallas_call(
        _xattn_decode_kernel,
        grid=(b // bb,),
        in_specs=[pl.BlockSpec((bb, rows, XATT_HEAD_DIM), lambda i: (i, 0, 0)), kv, kv,
                  pl.BlockSpec((1, XATT_HEAD_DIM), lambda i: (0, 0))],
        out_specs=pl.BlockSpec((bb, rows, XATT_HEAD_DIM), lambda i: (i, 0, 0)),
        out_shape=jax.ShapeDtypeStruct((b, rows, XATT_HEAD_DIM), F32),
        compiler_params=_cp(("parallel",)),
        name="xattn_decode",
    )(q_pad, mem_k, mem_v, xq_norm_w.reshape(1, XATT_HEAD_DIM))


LORA_OFF = 3 * RWKV_WIDTH
GATE_OFF = LORA_OFF + DECAY_LORA + AAA_LORA
GATE_PAD = RWKV_PROJ_PAD - GATE_OFF


def _sigmoid(x):
    return 1.0 / (1.0 + jnp.exp(-x))


def _per_chunk(fn, *arrays):
    w = arrays[0].shape[1]
    outs = [fn(*(a[:, c * LANES:(c + 1) * LANES] for a in arrays)) for c in range(w // LANES)]
    return jnp.concatenate(outs, axis=1)


def _rwkv_prep_core(pr, prev, mu, w0, a0, kk_w, ka_w, rk_w, w_lora, w_gate):
    c = RWKV_WIDTH
    gmat = _head_indicator()
    xm = pr + (prev - pr) * mu
    r, k, v = xm[:, 0:c], xm[:, c:2 * c], xm[:, 2 * c:3 * c]
    lora = xm[:, LORA_OFF:LORA_OFF + LANES]
    lane = lax.broadcasted_iota(jnp.int32, lora.shape, 1)
    lora_in = jnp.where(lane < DECAY_LORA, jnp.tanh(lora), lora)
    wa = _dot(lora_in.astype(BF16), w_lora)
    z = -(w0 + wa[:, 0:c])
    softplus = jnp.maximum(z, 0.0) + jnp.log(1.0 + jnp.exp(-jnp.abs(z)))
    log_decay = -jnp.exp(-softplus - 0.5)
    a = _sigmoid(a0 + wa[:, c:2 * c])
    g = _dot(_sigmoid(xm[:, GATE_OFF:GATE_OFF + GATE_PAD]).astype(BF16), w_gate)
    kk = k * kk_w
    norm = jnp.sqrt(_per_chunk(lambda t: _group_sum(t * t, gmat), kk))
    kk = kk / jnp.maximum(norm, 1e-12)
    kp = k * (1.0 + (a - 1.0) * ka_w)
    bonus = _per_chunk(lambda t: _group_sum(t, gmat), r * kp * rk_w) * v
    return r, log_decay, kp, v, kk, kk * a, bonus, g


def _rwkv_prep_tok_kernel(pr_ref, prev_ref, mu_ref, w0_ref, a0_ref, kkw_ref, kaw_ref, rkw_ref,
                          wl_ref, wg_ref, *out_refs):
    outs = _rwkv_prep_core(pr_ref[...], prev_ref[...], mu_ref[...], w0_ref[...], a0_ref[...],
                           kkw_ref[...], kaw_ref[...], rkw_ref[...], wl_ref[...], wg_ref[...])
    for k, (o_ref, o) in enumerate(zip(out_refs, outs)):
        o_ref[...] = o.T if k < N_STEP_VECS else o


def _rwkv_param_specs(index_map):
    c = RWKV_WIDTH
    shapes = [(1, RWKV_PROJ_PAD)] + [(1, c)] * 5 + [(LANES, 2 * c), (GATE_PAD, c)]
    return [pl.BlockSpec(s, index_map) for s in shapes]


N_STEP_VECS = 6


def rwkv_prep_tok(pr, prev, params):
    m, wd = pr.shape
    c = RWKV_WIDTH
    shapes = [(c, m)] * N_STEP_VECS + [(m, c)] * 2
    return pl.pallas_call(
        _rwkv_prep_tok_kernel,
        grid=(1,),
        in_specs=[pl.BlockSpec((m, wd), lambda i: (0, 0))] * 2
        + _rwkv_param_specs(lambda i: (0, 0)),
        out_specs=[pl.BlockSpec(s, lambda i: (0, 0)) for s in shapes],
        out_shape=[jax.ShapeDtypeStruct(s, F32) for s in shapes],
        compiler_params=_cp(("arbitrary",)),
        name="rwkv_prep_tok",
    )(pr, prev, *params)


def _dot_tn(a, b):
    return lax.dot_general(a, b, (((0,), (0,)), ((), ())), preferred_element_type=F32)


def _rwkv_scan_tile(r_all, ld_all, kp_all, v_all, kk_all, b_all, s_cur, y_ref):
    n = CHUNK
    nc = r_all.shape[0] // n
    ti = lax.broadcasted_iota(jnp.int32, (n, n), 0)
    si = lax.broadcasted_iota(jnp.int32, (n, n), 1)
    tri = jnp.where(si <= ti, 1.0, 0.0).astype(BF16)
    t2 = lax.broadcasted_iota(jnp.int32, (n, LANES), 0)
    lane2 = lax.broadcasted_iota(jnp.int32, (n, LANES), 1)
    s2 = lane2 % n
    low = lane2 < n
    strict2, incl2, eye2 = s2 < t2, s2 <= t2, s2 == t2
    low4 = lax.broadcasted_iota(jnp.int32, (2 * n, LANES), 1) < n
    top4 = lax.broadcasted_iota(jnp.int32, (2 * n, LANES), 0) < n
    diag_blk = top4 == low4
    own4 = {0: low4, 1: ~low4}
    pairs = range(RWKV_HEADS // 2)
    cs = [slice(c * LANES, (c + 1) * LANES) for c in pairs]

    pre = []
    for j in range(nc):
        rows = slice(j * n, (j + 1) * n)
        ld = ld_all[rows, :]
        l1, l2, l3 = _split3(ld)
        lc = _dot(tri, l1) + _dot(tri, l2) + _dot(tri, l3)
        lc_end = lc[n - 1:n, :]
        e_neg = jnp.exp(-lc)
        kk, b, kp = kk_all[rows, :], b_all[rows, :], kp_all[rows, :]
        to_end = jnp.exp(lc_end - lc)
        pre.append(dict(
            a_t=(-kk * jnp.exp(lc - ld)).astype(BF16), b_t=(b * e_neg).astype(BF16),
            k_t=(kp * e_neg).astype(BF16), r_t=(r_all[rows, :] * jnp.exp(lc)).astype(BF16),
            b_e=(b * to_end).astype(BF16), k_e=(kp * to_end).astype(BF16),
            v_b=v_all[rows, :].astype(BF16), g_end=jnp.exp(lc_end)))
    items = [(j, c, par) for j in range(nc) for c in pairs for par in (0, 1)]
    at = {it: i for i, it in enumerate(items)}
    ar = {(j, c): jnp.concatenate([pre[j]['a_t'][:, cs[c]], pre[j]['r_t'][:, cs[c]]], axis=0)
          for j in range(nc) for c in pairs}
    kb = {(j, c): jnp.concatenate([pre[j]['k_t'][:, cs[c]], pre[j]['b_t'][:, cs[c]]], axis=0)
          for j in range(nc) for c in pairs}
    gm = [_dot_nt(jnp.where(own4[par], ar[j, c], 0).astype(BF16), kb[j, c]) for j, c, par in items]
    top = [g[0:n] for g in gm]
    pm = [jnp.where(incl2, g[n:2 * n], 0.0).astype(BF16) for g in gm]
    lak = [jnp.where(strict2 & low, t, 0.0).astype(BF16) for t in top]
    lv = [_dot(lak[i][:, 0:n], pre[j]['v_b'][:, cs[c]]) for i, (j, c, par) in enumerate(items)]
    z = [jnp.where(low, jnp.where(eye2, 1.0, 0.0), jnp.where(strict2, t, 0.0)) for t in top]
    for _ in range(6):
        zb = [zz.astype(BF16) for zz in z]
        res = [_dot(jnp.where(low, 0, zb[i]).astype(BF16), jnp.concatenate([zb[i], zb[i]], axis=0))
               for i in range(len(items))]
        z = [res[i] + jnp.where(low, z[i], 0.0) for i in range(len(items))]
    tmat = [zz[:, 0:n].astype(BF16) for zz in z]

    for j in range(nc):
        p = pre[j]
        sw = [_dot_nt(ar[j, c], s_cur[c].astype(BF16)) for c in pairs]
        w0 = [(sw[c][0:n] + jnp.where(low, lv[at[j, c, 0]], lv[at[j, c, 1]])).astype(BF16)
              for c in pairs]
        u = [jnp.where(low, _dot(tmat[at[j, c, 0]], w0[c]),
                       _dot(tmat[at[j, c, 1]], w0[c])).astype(BF16) for c in pairs]
        vu = [jnp.concatenate([p['v_b'][:, cs[c]], u[c]], axis=0) for c in pairs]
        yb = [jnp.where(low, _dot(pm[at[j, c, 0]], vu[c]), _dot(pm[at[j, c, 1]], vu[c]))
              for c in pairs]
        for c in pairs:
            y_ref[j * n:(j + 1) * n, cs[c]] = sw[c][n:2 * n] + yb[c]
        upd = [_dot_tn(jnp.concatenate([u[c], p['v_b'][:, cs[c]]], axis=0),
                       jnp.concatenate([p['b_e'][:, cs[c]], p['k_e'][:, cs[c]]], axis=0))
               for c in pairs]
        s_cur = [s_cur[c] * p['g_end'][:, cs[c]] + jnp.where(diag_blk, upd[c], 0.0) for c in pairs]
    return s_cur


def _group_norm_gate(y, bonus, g, ln_w, ln_b, gmat):
    inv = 1.0 / RWKV_HEAD_DIM
    d = y - _group_sum(y, gmat) * inv
    var = _group_sum(d * d, gmat) * inv
    return (d * lax.rsqrt(var + GN_EPS) * ln_w + ln_b + bonus) * g


def _rwkv_seq_kernel(pr_ref, prev0_ref, mu_ref, w0_ref, a0_ref, kkw_ref, kaw_ref, rkw_ref,
                     wl_ref, wg_ref, lnw_ref, lnb_ref, o_ref, s_out_ref, s_ref, last_ref, y_ref):
    @pl.when(pl.program_id(1) == 0)
    def _():
        s_ref[...] = jnp.zeros_like(s_ref)
        last_ref[...] = prev0_ref[...]

    pr = pr_ref[...]
    rows = pr.shape[0]
    row = lax.broadcasted_iota(jnp.int32, (rows, 1), 0)
    prev = jnp.where(row == 0, last_ref[...], pltpu.roll(pr, 1, axis=0))
    last_ref[...] = pr[rows - 1:rows, :]
    r, ld, kp, v, kk, b, bonus, g = _rwkv_prep_core(
        pr, prev, mu_ref[...], w0_ref[...], a0_ref[...], kkw_ref[...], kaw_ref[...],
        rkw_ref[...], wl_ref[...], wg_ref[...])
    pairs = range(RWKV_HEADS // 2)
    s_new = _rwkv_scan_tile(r, ld, kp, v, kk, b, [s_ref[c] for c in pairs], y_ref)
    n = RWKV_HEAD_DIM
    for c in pairs:
        s_ref[c] = s_new[c]
        s_out_ref[2 * c] = s_new[c][0:n, 0:n]
        s_out_ref[2 * c + 1] = s_new[c][n:2 * n, n:2 * n]
    gmat = _head_indicator()
    for c in pairs:
        sl = slice(c * LANES, (c + 1) * LANES)
        o_ref[:, sl] = _group_norm_gate(y_ref[:, sl], bonus[:, sl], g[:, sl], lnw_ref[:, sl],
                                        lnb_ref[:, sl], gmat)


SCAN_CHUNKS = 2


def rwkv_seq(pr, prev0, params, ln_w, ln_b):
    bsz, t, wd = pr.shape
    c = RWKV_WIDTH
    rows = SCAN_CHUNKS * CHUNK
    assert t % rows == 0
    vec = pl.BlockSpec((1, c), lambda i, j: (0, 0))
    st = pl.BlockSpec((None, RWKV_HEADS, RWKV_HEAD_DIM, RWKV_HEAD_DIM), lambda i, j: (i, 0, 0, 0))
    return pl.pallas_call(
        _rwkv_seq_kernel,
        grid=(bsz, t // rows),
        in_specs=[pl.BlockSpec((None, rows, wd), lambda i, j: (i, j, 0)),
                  pl.BlockSpec((None, 1, wd), lambda i, j: (i, 0, 0))]
        + _rwkv_param_specs(lambda i, j: (0, 0)) + [vec, vec],
        out_specs=[pl.BlockSpec((None, rows, c), lambda i, j: (i, j, 0)), st],
        out_shape=[jax.ShapeDtypeStruct((bsz, t, c), F32),
                   jax.ShapeDtypeStruct((bsz, RWKV_HEADS, RWKV_HEAD_DIM, RWKV_HEAD_DIM), F32)],
        scratch_shapes=[pltpu.VMEM((RWKV_HEADS // 2, LANES, LANES), F32),
                        pltpu.VMEM((1, wd), F32), pltpu.VMEM((rows, c), F32)],
        compiler_params=_cp(("parallel", "arbitrary")),
        name="rwkv_seq",
    )(pr, prev0, *params, ln_w.reshape(1, c), ln_b.reshape(1, c))


STEP_UNROLL = 8


def _rwkv_step_kernel(r_ref, ld_ref, kp_ref, v_ref, kk_ref, b_ref, s_ref, y_ref, s_out_ref):
    n = RWKV_HEAD_DIM
    neg_kk, decay = -kk_ref[...], jnp.exp(ld_ref[...])
    b_mat, kp_mat, r_mat = b_ref[...], kp_ref[...], r_ref[...]

    def body(i, carry):
        v0 = pl.multiple_of(i * STEP_UNROLL, STEP_UNROLL)
        v_rows = v_ref[pl.ds(v0, STEP_UNROLL), :]
        rows = range(STEP_UNROLL)
        s = [s_ref[v0 + j] for j in rows]
        sa = [jnp.sum(s[j] * neg_kk, axis=0, keepdims=True) for j in rows]
        s_new = [s[j] * decay + sa[j] * b_mat + v_rows[j:j + 1, :] * kp_mat for j in rows]
        y = [jnp.sum(s_new[j] * r_mat, axis=0, keepdims=True) for j in rows]
        for j in rows:
            s_out_ref[v0 + j] = s_new[j]
        y_ref[pl.ds(v0, STEP_UNROLL), :] = jnp.concatenate(y, axis=0)
        return carry

    lax.fori_loop(0, n // STEP_UNROLL, body, 0)


def rwkv_step(r, ld, kp, v, kk, b, state_t):
    _, nh, n, _, bsz = state_t.shape
    vec = pl.BlockSpec((n, bsz), lambda h: (h, 0))
    st = pl.BlockSpec((None, None, n, n, bsz), lambda h: (0, h, 0, 0, 0))
    return pl.pallas_call(
        _rwkv_step_kernel,
        grid=(nh,),
        in_specs=[vec] * 6 + [st],
        out_specs=[vec, st],
        out_shape=[jax.ShapeDtypeStruct((nh * n, bsz), F32),
                   jax.ShapeDtypeStruct(state_t.shape, F32)],
        compiler_params=_cp(("parallel",)),
        name="rwkv_step",
    )(r, ld, kp, v, kk, b, state_t)


def _rwkv_post_kernel(y_ref, bonus_ref, g_ref, lnw_ref, lnb_ref, o_ref, *, y_channel_major):
    gmat = _head_indicator()
    for c in range(o_ref.shape[1] // LANES):
        sl = slice(c * LANES, (c + 1) * LANES)
        y = y_ref[sl, :].T if y_channel_major else y_ref[:, sl]
        o_ref[:, sl] = _group_norm_gate(y, bonus_ref[:, sl], g_ref[:, sl], lnw_ref[:, sl],
                                        lnb_ref[:, sl], gmat)


def rwkv_post(y, bonus, g, ln_w, ln_b, *, tm, y_channel_major=False):
    m, c = bonus.shape
    blk = pl.BlockSpec((tm, c), lambda i: (i, 0))
    y_blk = pl.BlockSpec((c, tm), lambda i: (0, i)) if y_channel_major else blk
    vec = pl.BlockSpec((1, c), lambda i: (0, 0))
    return pl.pallas_call(
        functools.partial(_rwkv_post_kernel, y_channel_major=y_channel_major),
        grid=(m // tm,),
        in_specs=[y_blk, blk, blk, vec, vec],
        out_specs=blk,
        out_shape=jax.ShapeDtypeStruct((m, c), F32),
        compiler_params=_cp(("parallel",)),
        name="rwkv_post",
    )(y, bonus, g, ln_w.reshape(1, c), ln_b.reshape(1, c))


ROUTER_LANES = LANES
ROW_TILES = 1
ROW_LANES = D_MODEL // ROW_TILES


def _rows_to_tiles(ref, x):
    rows = x.shape[0]
    if ROW_TILES == 1:
        ref[...] = x
        return
    for j in range(ROW_TILES):
        ref[pl.ds(j, rows, stride=ROW_TILES), :] = x[:, j * ROW_LANES:(j + 1) * ROW_LANES]


def _tiles_to_rows(ref, rows):
    if ROW_TILES == 1:
        return ref[...]
    return jnp.concatenate(
        [ref[pl.ds(j, rows, stride=ROW_TILES), :] for j in range(ROW_TILES)], axis=1)


def _router_kernel(ha_ref, hb_ref, lnw_ref, whi_ref, wlo_ref, bias_ref, u_ref, idx_ref, gate_ref,
                   *, steps_a):
    use_a = pl.program_id(0) < steps_a
    h = jnp.where(use_a, ha_ref[...], hb_ref[...])
    u = _rms_rows(h, lnw_ref[...])
    _rows_to_tiles(u_ref, u)
    u_hi, u_lo = _split2(u)
    w_hi = whi_ref[...]
    logits = _dot(u_hi, w_hi) + _dot(u_lo, w_hi) + _dot(u_hi, wlo_ref[...]) + bias_ref[...]
    lane = lax.broadcasted_iota(jnp.int32, logits.shape, 1)
    neg = -jnp.inf

    def first_max(x):
        m = jnp.max(x, axis=1, keepdims=True)
        return m, jnp.min(jnp.where(x == m, lane, ROUTER_LANES), axis=1, keepdims=True)

    gl = jnp.where(lane < N_EXPERT_GROUPS, logits, neg)
    g_max, g_idx = first_max(gl)
    g_gate = 1.0 / jnp.sum(jnp.exp(gl - g_max), axis=1, keepdims=True)
    lo = N_EXPERT_GROUPS + g_idx * EXPERTS_PER_GROUP
    el = jnp.where((lane >= lo) & (lane < lo + EXPERTS_PER_GROUP), logits, neg)
    v1, i1 = first_max(el)
    v2, i2 = first_max(jnp.where(lane == i1, neg, el))
    e2 = jnp.exp(v2 - v1)
    w1 = g_gate / (1.0 + e2)
    w2 = g_gate * e2 / (1.0 + e2)
    idx_ref[...] = jnp.where(lane == 0, i1 - N_EXPERT_GROUPS,
                             jnp.where(lane == 1, i2 - N_EXPERT_GROUPS, 0))
    gate_ref[...] = jnp.where(lane == 0, w1, jnp.where(lane == 1, w2, 0.0))


def moe_router(h_a, h_b, ln_w, w_hi, w_lo, bias, *, tm):
    (ma, d), mb = h_a.shape, h_b.shape[0]
    assert ma % tm == 0 and mb % tm == 0
    steps_a, steps_b = ma // tm, mb // tm
    m = ma + mb
    const = lambda r, w: pl.BlockSpec((r, w), lambda i: (0, 0))
    row = lambda w: pl.BlockSpec((tm, w), lambda i: (i, 0))
    return pl.pallas_call(
        functools.partial(_router_kernel, steps_a=steps_a),
        grid=(steps_a + steps_b,),
        in_specs=[pl.BlockSpec((tm, d), lambda i: (jnp.minimum(i, steps_a - 1), 0)),
                  pl.BlockSpec((tm, d), lambda i: (jnp.maximum(i - steps_a, 0), 0)),
                  const(1, d), const(d, ROUTER_LANES), const(d, ROUTER_LANES),
                  const(1, ROUTER_LANES)],
        out_specs=[pl.BlockSpec((tm * ROW_TILES, ROW_LANES), lambda i: (i, 0)),
                   row(ROUTER_LANES), row(ROUTER_LANES)],
        out_shape=[jax.ShapeDtypeStruct((m * ROW_TILES, ROW_LANES), F32),
                   jax.ShapeDtypeStruct((m, ROUTER_LANES), jnp.int32),
                   jax.ShapeDtypeStruct((m, ROUTER_LANES), F32)],
        compiler_params=_cp(("arbitrary",)),
        name="moe_router",
    )(h_a, h_b, ln_w.reshape(1, d), w_hi, w_lo, bias)


X_SLOTS = 3
Y_SLOTS = 2


W_SLOTS = 3
DMA_QUEUES = 2


def _moe_expert_kernel(run_ref, rexp_ref, nused_ref, tok0_ref, tok1_ref, tok2_ref, dst_ref,
                       u_hbm, wg_hbm, wu_hbm, wd_hbm, y_hbm, xbuf, ybuf, wg_f, wu_f,
                       wd_f, wg_b, wu_b, wd_b, sem_in, sem_out, sem_w):
    i = pl.program_id(0)
    n_used, n_runs = nused_ref[0], nused_ref[1]
    tile_rows = MOE_BLOCK * ROW_TILES
    pad_base = y_hbm.shape[0] - Y_SLOTS * tile_rows
    run = run_ref[i]

    def weight_copies(k):
        e, s = rexp_ref[jnp.minimum(k, n_runs - 1)], lax.rem(k, W_SLOTS)
        copies = []
        for hbm, buf in ((wg_hbm, wg_f), (wu_hbm, wu_f), (wd_hbm, wd_f)):
            rows = buf.shape[1] // 2
            for part in range(2):
                sl = pl.ds(part * rows, rows)
                copies.append((pltpu.make_async_copy(hbm.at[e, sl], buf.at[s, sl], sem_w.at[s]),
                               part))
        return copies

    def gather_block(idx_ref, x_slot):
        for r in range(MOE_BLOCK):
            pltpu.make_async_copy(u_hbm.at[pl.ds(idx_ref[0, 0, r], ROW_TILES)],
                                  xbuf.at[x_slot, pl.ds(r * ROW_TILES, ROW_TILES)],
                                  sem_in.at[x_slot]).start(priority=r % DMA_QUEUES)

    def scatter_rows(idx_ref, y_slot, rows):
        for r in rows:
            pltpu.make_async_copy(ybuf.at[y_slot, pl.ds(r * ROW_TILES, ROW_TILES)],
                                  y_hbm.at[pl.ds(idx_ref[0, 0, r], ROW_TILES)],
                                  sem_out.at[y_slot]).start(priority=r % DMA_QUEUES)

    def gather_wait(x_slot):
        pltpu.make_async_copy(u_hbm.at[pl.ds(0, tile_rows)], xbuf.at[x_slot],
                              sem_in.at[x_slot]).wait()

    def scatter_wait(y_slot):
        pltpu.make_async_copy(ybuf.at[y_slot], y_hbm.at[pl.ds(0, tile_rows)],
                              sem_out.at[y_slot]).wait()

    @pl.when(i == 0)
    def _():
        ybuf[0] = jnp.zeros(ybuf.shape[1:], F32)
        for s in range(Y_SLOTS):
            pltpu.make_async_copy(ybuf.at[0], y_hbm.at[pl.ds(pad_base + s * tile_rows, tile_rows)],
                                  sem_out.at[s]).start()
        for k in range(W_SLOTS - 1):
            for cp, queue in weight_copies(k):
                cp.start(priority=queue)
        gather_block(tok0_ref, 0)
        gather_block(tok1_ref, 1)
        for s in range(Y_SLOTS):
            scatter_wait(s)

    @pl.when(i < n_used)
    def _():
        x_slot = lax.rem(i, X_SLOTS)
        y_slot = lax.rem(i, Y_SLOTS)

        @pl.when((i == 0) | (run != run_ref[jnp.maximum(i - 1, 0)]))
        def _():
            for cp, _ in weight_copies(run):
                cp.wait()
            w_slot = lax.rem(run, W_SLOTS)
            wg_b[...] = wg_f[w_slot].astype(BF16)
            wu_b[...] = wu_f[w_slot].astype(BF16)
            wd_b[...] = wd_f[w_slot].astype(BF16)
            for cp, queue in weight_copies(run + W_SLOTS - 1):
                cp.start(priority=queue)

        gather_wait(x_slot)

        @pl.when(i >= Y_SLOTS)
        def _():
            scatter_wait(y_slot)

        x = _tiles_to_rows(xbuf.at[x_slot], MOE_BLOCK).astype(BF16)
        hg = _dot(x, wg_b[...])
        hu = _dot(x, wu_b[...])
        act = (hg * _sigmoid(hg) * hu).astype(BF16)
        y = _dot(act, wd_b[...])
        _rows_to_tiles(ybuf.at[y_slot], y)
        scatter_rows(dst_ref, y_slot, range(MOE_BLOCK))
        gather_block(tok2_ref, lax.rem(i + 2, X_SLOTS))

        @pl.when(i == n_used - 1)
        def _():
            scatter_wait(y_slot)

            @pl.when(i >= 1)
            def _():
                scatter_wait(1 - y_slot)

            gather_wait(lax.rem(i + 1, X_SLOTS))
            gather_wait(lax.rem(i + 2, X_SLOTS))
            for k in range(W_SLOTS - 1):
                for cp, _ in weight_copies(n_runs + k):
                    cp.wait()


def moe_experts(u_all, row_src, row_dst, block_run, run_exp, n_used_runs, w_gate, w_up, w_down,
                n_assign):
    d, ff = w_gate.shape[1], w_gate.shape[2]
    n_blocks = row_src.shape[0]
    tile_rows = MOE_BLOCK * ROW_TILES
    smem_blk = lambda off: pl.BlockSpec(
        (1, 1, MOE_BLOCK), lambda i, *_: (jnp.clip(i + off, 0, n_blocks - 1), 0, 0),
        memory_space=pltpu.SMEM)
    hbm = pl.BlockSpec(memory_space=pl.ANY)
    grid_spec = pltpu.PrefetchScalarGridSpec(
        num_scalar_prefetch=3,
        grid=(n_blocks,),
        in_specs=[
            smem_blk(0), smem_blk(1), smem_blk(2), smem_blk(0),
            hbm, hbm, hbm, hbm,
        ],
        out_specs=hbm,
        scratch_shapes=[
            pltpu.VMEM((X_SLOTS, tile_rows, ROW_LANES), F32),
            pltpu.VMEM((Y_SLOTS, tile_rows, ROW_LANES), F32),
            pltpu.VMEM((W_SLOTS, d, ff), F32), pltpu.VMEM((W_SLOTS, d, ff), F32),
            pltpu.VMEM((W_SLOTS, ff, d), F32),
            pltpu.VMEM((d, ff), BF16), pltpu.VMEM((d, ff), BF16), pltpu.VMEM((ff, d), BF16),
            pltpu.SemaphoreType.DMA((X_SLOTS,)), pltpu.SemaphoreType.DMA((Y_SLOTS,)),
            pltpu.SemaphoreType.DMA((W_SLOTS,)),
        ],
    )
    y_rows = (n_assign + Y_SLOTS * MOE_BLOCK) * ROW_TILES
    return pl.pallas_call(
        _moe_expert_kernel,
        grid_spec=grid_spec,
        out_shape=jax.ShapeDtypeStruct((y_rows, ROW_LANES), F32),
        compiler_params=_cp(("arbitrary",), vmem=MOE_VMEM_LIMIT),
        name="moe_experts",
    )(block_run, run_exp, n_used_runs, row_src, row_src, row_src, row_dst, u_all,
      w_gate, w_up, w_down)


def _moe_combine_kernel(h_ref, y0_ref, y1_ref, gate_ref, o_ref):
    rows = h_ref.shape[0]
    gate = gate_ref[...]
    o_ref[...] = h_ref[...] + (_tiles_to_rows(y0_ref, rows) * gate[:, 0:1]
                               + _tiles_to_rows(y1_ref, rows) * gate[:, 1:2])


def moe_combine(h, y_slots, gates, row_off, slot_stride, *, tm):
    m, d = h.shape
    assert row_off % tm == 0 and slot_stride % tm == 0
    off0, off1 = row_off // tm, (row_off + slot_stride) // tm
    return pl.pallas_call(
        _moe_combine_kernel,
        grid=(m // tm,),
        in_specs=[pl.BlockSpec((tm, d), lambda i: (i, 0)),
                  pl.BlockSpec((tm * ROW_TILES, ROW_LANES), lambda i: (i + off0, 0)),
                  pl.BlockSpec((tm * ROW_TILES, ROW_LANES), lambda i: (i + off1, 0)),
                  pl.BlockSpec((tm, ROUTER_LANES), lambda i: (i + off0, 0))],
        out_specs=pl.BlockSpec((tm, d), lambda i: (i, 0)),
        out_shape=jax.ShapeDtypeStruct((m, d), F32),
        compiler_params=_cp(("parallel",)),
        name="moe_combine",
    )(h, y_slots, y_slots, gates)


def moe_dispatch(e_idx, slot_stride):
    m = e_idx.shape[0]
    a = m * TOP_K
    e_flat = e_idx.reshape(a)
    order = jnp.argsort(e_flat, stable=True).astype(jnp.int32)
    counts = jnp.sum(e_flat[:, None] == jnp.arange(N_EXPERTS, dtype=jnp.int32)[None, :],
                     axis=0, dtype=jnp.int32)
    pad_counts = (counts + MOE_BLOCK - 1) // MOE_BLOCK * MOE_BLOCK
    starts = jnp.cumsum(counts) - counts
    pad_ends = jnp.cumsum(pad_counts)
    pad_starts = pad_ends - pad_counts
    n_blocks = a // MOE_BLOCK + N_EXPERTS
    p = n_blocks * MOE_BLOCK
    n_used = (pad_ends[-1] // MOE_BLOCK).astype(jnp.int32)
    blk = jnp.arange(n_blocks, dtype=jnp.int32)
    blk_start = jnp.minimum(blk, n_used - 1) * MOE_BLOCK
    block_exp = jnp.minimum(jnp.sum(blk_start[:, None] >= pad_ends[None, :], axis=1),
                            N_EXPERTS - 1).astype(jnp.int32)
    in_exp = blk * MOE_BLOCK - pad_starts[block_exp]
    row_cnt = jnp.where(blk < n_used, jnp.clip(counts[block_exp] - in_exp, 0, MOE_BLOCK), 0)
    lane = jnp.arange(MOE_BLOCK, dtype=jnp.int32)[None, :]
    valid = lane < row_cnt[:, None]
    src = jnp.clip((starts[block_exp] + in_exp)[:, None] + lane, 0, a - 1)
    assign = order[src]
    row_tok = jnp.where(valid, assign // TOP_K, 0)
    pad_dst = TOP_K * slot_stride + (blk % Y_SLOTS)[:, None] * MOE_BLOCK + lane
    row_dst = jnp.where(valid, (assign % TOP_K) * slot_stride + assign // TOP_K, pad_dst)
    as_blocks = lambda x: (x * ROW_TILES).astype(jnp.int32).reshape(n_blocks, 1, MOE_BLOCK)
    has_rows = counts > 0
    run_exp = jnp.argsort(~has_rows, stable=True).astype(jnp.int32)
    block_run = (jnp.cumsum(has_rows) - 1)[block_exp].astype(jnp.int32)
    n_used_runs = jnp.stack([n_used, jnp.sum(has_rows, dtype=jnp.int32)])
    return as_blocks(row_tok), as_blocks(row_dst), block_run, run_exp, n_used_runs


def rwkv_params(rw_mu, rw_w0, rw_w2, rw_a0, rw_a2, rw_g2, rw_k_k, rw_k_a, rw_r_k):
    c = RWKV_WIDTH
    mu = jnp.pad(rw_mu, (0, RWKV_PROJ_PAD - RWKV_PROJ)).reshape(1, RWKV_PROJ_PAD)
    w_lora = jnp.zeros((LANES, 2 * c), F32)
    w_lora = w_lora.at[0:DECAY_LORA, 0:c].set(rw_w2).at[DECAY_LORA:LANES, c:2 * c].set(rw_a2)
    w_gate = jnp.pad(rw_g2, ((0, GATE_PAD - GATE_LORA), (0, 0)))
    vec = lambda x: x.reshape(1, c)
    return (mu, vec(rw_w0), vec(rw_a0), vec(rw_k_k), vec(rw_k_a), vec(rw_r_k),
            w_lora.astype(BF16), w_gate.astype(BF16))


def _token_tiles(m):
    return (1024, 512) if m % 1024 == 0 else (m, m)


def _dense_front(x2d, wts, tm):
    pa = norm_matmul(x2d, wts['ln1_w'], wts['wt_att'], tm=tm, tn=ATT_PROJ // 2, w_transposed=True)
    pr = norm_matmul(x2d, wts['ln1_w'], wts['wt_rw'], tm=tm, tn=RWKV_PROJ_PAD // 3,
                     w_transposed=True)
    return pa, pr


def _dense_back(x2d, att2d, rw2d, wts, xattn_fn, tm):
    h1 = matmul_residual([att2d, rw2d], [wts['w_out_a'], wts['w_out_r']], x2d, tm=tm, tn=1024)
    qx = norm_matmul(h1, wts['ln2_w'], wts['xq_w'], tm=tm, tn=XATT_WIDTH)
    ox = xattn_fn(qx)
    return matmul_residual([ox], [wts['xo_w']], h1, tm=tm, tn=1024)


def kernel(x_prompt, x_sample, cache_win_k, cache_win_v, state_wkv, state_shift, cache_mem_k, cache_mem_v, mem_prompt, ln1_w, w_in, q_norm_w, k_norm_w, attn_sinks, rw_mu, rw_w0, rw_w2, rw_a0, rw_a2, rw_g2, rw_k_k, rw_k_a, rw_r_k, rw_ln_w, rw_ln_b, w_out, ln2_w, mem_norm_w, xq_w, xkv_w, xq_norm_w, xk_norm_w, xo_w, ln3_w, router_group_w, router_group_b, router_expert_w, router_expert_b, exp_w_gate, exp_w_up, exp_w_down):
    assert w_in.shape[0] == 1, "single-layer stack"
    bp, seq, d = x_prompt.shape
    bs = x_sample.shape[0]
    mp = bp * seq
    c = RWKV_WIDTH

    router_w = jnp.concatenate(
        [router_group_w[0], router_expert_w[0],
         jnp.zeros((d, ROUTER_LANES - N_EXPERT_GROUPS - N_EXPERTS), F32)], axis=1)
    router_hi = router_w.astype(BF16)
    wts = {
        'ln1_w': ln1_w[0], 'ln2_w': ln2_w[0], 'ln3_w': ln3_w[0],
        'wt_att': cast_rows_bf16(w_in[0].T, 0, ATT_PROJ, ATT_PROJ),
        'wt_rw': cast_rows_bf16(w_in[0].T, ATT_PROJ, RWKV_PROJ, RWKV_PROJ_PAD),
        'w_out_a': w_out[0][:ATT_WIDTH].astype(BF16),
        'w_out_r': w_out[0][ATT_WIDTH:].astype(BF16),
        'xq_w': xq_w[0].astype(BF16), 'xo_w': xo_w[0].astype(BF16),
        'router_hi': router_hi,
        'router_lo': (router_w - router_hi.astype(F32)).astype(BF16),
        'router_b': jnp.pad(jnp.concatenate([router_group_b[0], router_expert_b[0]]),
                            (0, ROUTER_LANES - N_EXPERT_GROUPS - N_EXPERTS)).reshape(1, -1),
    }
    rw_par = rwkv_params(rw_mu[0], rw_w0[0], rw_w2[0], rw_a0[0], rw_a2[0], rw_g2[0],
                         rw_k_k[0], rw_k_a[0], rw_r_k[0])

    tm_p, te_p = _token_tiles(mp)
    xp = x_prompt.reshape(mp, d)
    pa, pr = _dense_front(xp, wts, tm_p)
    pa3 = pa.reshape(bp, seq, ATT_PROJ)
    pr3 = pr.reshape(bp, seq, RWKV_PROJ_PAD)
    tabs_p = rope_tables(np.arange(seq))
    att_p, kn_p = swa_prompt(pa3, tabs_p, q_norm_w[0], k_norm_w[0], attn_sinks[0])
    rw_p, wkv_p = rwkv_seq(pr3, jnp.zeros((bp, 1, RWKV_PROJ_PAD), F32), rw_par,
                           rw_ln_w[0], rw_ln_b[0])
    rw_p = rw_p.reshape(mp, c)

    n_mem = mem_prompt.shape[1]
    kv_mem = norm_matmul(mem_prompt.reshape(bp * n_mem, d), mem_norm_w[0],
                         xkv_w[0].astype(BF16), tm=bp * n_mem, tn=512)
    mem_k = head_rms(kv_mem[:, :XATT_WIDTH], xk_norm_w[0])
    mem_v = kv_mem[:, XATT_WIDTH:]
    mem_k3 = mem_k.reshape(bp, n_mem, XATT_WIDTH)
    mem_v3 = mem_v.reshape(bp, n_mem, XATT_WIDTH)

    def xattn_p(qx):
        return xattn_prompt(qx.reshape(bp, seq, XATT_WIDTH), mem_k3, mem_v3,
                            xq_norm_w[0]).reshape(mp, XATT_WIDTH)

    h2_p = _dense_back(xp, att_p.reshape(mp, ATT_WIDTH), rw_p, wts, xattn_p, tm_p)

    tm_s, te_s = _token_tiles(bs)
    xs = x_sample.reshape(bs, d)
    sa, sr = _dense_front(xs, wts, tm_s)
    tabs_s = rope_tables(PAST_LEN + np.arange(1))
    qk_w = jnp.concatenate([jnp.tile(q_norm_w[0], ATT_HEADS),
                            jnp.tile(k_norm_w[0], ATT_KV_HEADS)]).reshape(1, -1)
    qk = qk_norm_rope(sa[:, :ATT_WIDTH + KV_WIDTH], qk_w, tabs_s)
    nbuf = cache_win_k.shape[2]

    def feature_major(cache):
        return jnp.transpose(cache, (0, 1, 3, 4, 2)).reshape(bs, KV_WIDTH, nbuf)

    def position_major(win):
        return jnp.transpose(win.reshape(1, bs, ATT_KV_HEADS, HEAD_DIM, nbuf), (0, 1, 4, 2, 3))

    att_s, win_k, win_v = swa_decode(
        qk[:, :ATT_WIDTH].reshape(bs, ATT_HEADS, HEAD_DIM),
        qk[:, ATT_WIDTH:].reshape(bs, 1, KV_WIDTH),
        sa[:, ATT_WIDTH + KV_WIDTH:].reshape(bs, 1, KV_WIDTH),
        feature_major(cache_win_k), feature_major(cache_win_v), attn_sinks[0])
    shift_prev = jnp.pad(state_shift[0], ((0, 0), (0, RWKV_PROJ_PAD - RWKV_PROJ)))
    r, ld, kp, v, kk, b, bonus, g = rwkv_prep_tok(sr, shift_prev, rw_par)
    y_s, wkv_s = rwkv_step(r, ld, kp, v, kk, b, jnp.transpose(state_wkv, (0, 2, 3, 4, 1)))
    wkv_s = jnp.transpose(wkv_s, (0, 4, 1, 2, 3))
    rw_s = rwkv_post(y_s, bonus, g, rw_ln_w[0], rw_ln_b[0], tm=te_s, y_channel_major=True)
    def xattn_s(qx):
        q_pad = jnp.pad(qx.reshape(bs, XATT_HEADS, XATT_HEAD_DIM), ((0, 0), (0, 4), (0, 0)))
        rows_of = lambda c: c.reshape(bs, n_mem * XATT_HEADS, XATT_HEAD_DIM)
        o = xattn_decode(q_pad, rows_of(cache_mem_k), rows_of(cache_mem_v), xq_norm_w[0])
        return o[:, :XATT_HEADS].reshape(bs, XATT_WIDTH)

    h2_s = _dense_back(xs, att_s.reshape(bs, ATT_WIDTH), rw_s, wts, xattn_s, tm_s)

    m_all = mp + bs
    slot_stride = m_all
    tc = math.gcd(mp, bs, 512)
    tr = te_p
    h2_s_pad = jnp.pad(h2_s, ((0, -bs % tr), (0, 0)))
    u_all, idx_all, gate_all = moe_router(h2_p, h2_s_pad, wts['ln3_w'], wts['router_hi'],
                                          wts['router_lo'], wts['router_b'], tm=tr)
    row_src, row_dst, block_run, run_exp, n_used_runs = moe_dispatch(
        idx_all[:m_all, :TOP_K], slot_stride)
    y_slots = moe_experts(u_all, row_src, row_dst, block_run, run_exp, n_used_runs,
                          exp_w_gate[0], exp_w_up[0], exp_w_down[0], TOP_K * slot_stride)
    out_p = moe_combine(h2_p, y_slots, gate_all, 0, slot_stride, tm=tc)
    out_s = moe_combine(h2_s, y_slots, gate_all, mp, slot_stride, tm=tc)

    win = min(WINDOW, seq)
    kv_shape = (1, bp, win, ATT_KV_HEADS, HEAD_DIM)
    return (
        out_p.reshape(bp, seq, d),
        out_s.reshape(bs, 1, d),
        kn_p[:, seq - win:].reshape(kv_shape),
        pa3[:, seq - win:, ATT_WIDTH + KV_WIDTH:].reshape(kv_shape),
        wkv_p[None],
        pr3[:, seq - 1, :RWKV_PROJ][None],
        mem_k3.reshape(1, bp, n_mem, XATT_HEADS, XATT_HEAD_DIM),
        mem_v3.reshape(1, bp, n_mem, XATT_HEADS, XATT_HEAD_DIM),
        position_major(win_k),
        position_major(win_v),
        wkv_s,
        sr[:, :RWKV_PROJ].reshape(1, bs, RWKV_PROJ),
    )
```

```python
import functools
import math

import numpy as np
import jax
import jax.numpy as jnp
from jax import lax
from jax.experimental import pallas as pl
from jax.experimental.pallas import tpu as pltpu

F32 = jnp.float32
BF16 = jnp.bfloat16

D_MODEL = 2048
HEAD_DIM = 64
ATT_HEADS = 16
ATT_KV_HEADS = 4
ATT_GROUP = ATT_HEADS // ATT_KV_HEADS
ATT_WIDTH = ATT_HEADS * HEAD_DIM
KV_WIDTH = ATT_KV_HEADS * HEAD_DIM
ATT_PROJ = ATT_WIDTH + 2 * KV_WIDTH
WINDOW = 128
ATT_SCALE = HEAD_DIM ** -0.5
ROPE_THETA = 500000.0
ROT_DIM = HEAD_DIM // 4
PAST_LEN = 16384

RWKV_WIDTH = 1024
RWKV_HEAD_DIM = 64
RWKV_HEADS = 16
DECAY_LORA = 64
AAA_LORA = 64
GATE_LORA = 160
RWKV_PROJ = 3 * RWKV_WIDTH + DECAY_LORA + AAA_LORA + GATE_LORA
RWKV_PROJ_PAD = 3456

N_MEM = 256
XATT_HEADS = 4
XATT_HEAD_DIM = 128
XATT_WIDTH = XATT_HEADS * XATT_HEAD_DIM

N_EXPERT_GROUPS = 8
EXPERTS_PER_GROUP = 8
N_EXPERTS = 64
TOP_K = 2
EXPERT_FF = D_MODEL // 4
MOE_BLOCK = 128

RMS_EPS = 1e-6
GN_EPS = 64e-5

LANES = 128
CHUNK = 64
VMEM_LIMIT = 56 * 1024 * 1024
MOE_VMEM_LIMIT = 60 * 1024 * 1024


def _cp(sem, vmem=VMEM_LIMIT):
    return pltpu.CompilerParams(dimension_semantics=sem, vmem_limit_bytes=vmem)


def _rms_rows(x, w):
    ms = jnp.mean(x * x, axis=-1, keepdims=True)
    return x * lax.rsqrt(ms + RMS_EPS) * w


def _split2(x):
    hi = x.astype(BF16)
    lo = (x - hi.astype(F32)).astype(BF16)
    return hi, lo


def _split3(x):
    h1 = x.astype(BF16)
    r1 = x - h1.astype(F32)
    h2 = r1.astype(BF16)
    h3 = (r1 - h2.astype(F32)).astype(BF16)
    return h1, h2, h3


def _dot(a, b):
    return jnp.dot(a, b, preferred_element_type=F32)


def _dot_nt(a, b):
    return lax.dot_general(a, b, (((1,), (1,)), ((), ())), preferred_element_type=F32)


def _group_sum(x, gmat):
    hi, lo = _split2(x)
    return _dot(hi, gmat) + _dot(lo, gmat)


def _head_indicator():
    r = lax.broadcasted_iota(jnp.int32, (LANES, LANES), 0) // HEAD_DIM
    c = lax.broadcasted_iota(jnp.int32, (LANES, LANES), 1) // HEAD_DIM
    return jnp.where(r == c, 1.0, 0.0).astype(BF16)


def _norm_mm_kernel(x_ref, lnw_ref, w_ref, o_ref, xn_ref, *, w_transposed):
    @pl.when(pl.program_id(1) == 0)
    def _():
        xn_ref[...] = _rms_rows(x_ref[...], lnw_ref[...]).astype(BF16)

    o_ref[...] = (_dot_nt if w_transposed else _dot)(xn_ref[...], w_ref[...])


def norm_matmul(x, ln_w, w_bf16, *, tm, tn, w_transposed=False):
    m, k = x.shape
    n = w_bf16.shape[0 if w_transposed else 1]
    assert m % tm == 0 and n % tn == 0
    w_spec = (pl.BlockSpec((tn, k), lambda i, j: (j, 0)) if w_transposed
              else pl.BlockSpec((k, tn), lambda i, j: (0, j)))
    return pl.pallas_call(
        functools.partial(_norm_mm_kernel, w_transposed=w_transposed),
        grid=(m // tm, n // tn),
        in_specs=[
            pl.BlockSpec((tm, k), lambda i, j: (i, 0)),
            pl.BlockSpec((1, k), lambda i, j: (0, 0)),
            w_spec,
        ],
        out_specs=pl.BlockSpec((tm, tn), lambda i, j: (i, j)),
        out_shape=jax.ShapeDtypeStruct((m, n), F32),
        scratch_shapes=[pltpu.VMEM((tm, k), BF16)],
        compiler_params=_cp(("parallel", "arbitrary")),
        name="norm_matmul",
    )(x, ln_w.reshape(1, k), w_bf16)


def _cast_rows_kernel(w_ref, o_ref, *, n_in):
    @pl.when(pl.program_id(0) < n_in)
    def _():
        o_ref[...] = w_ref[...].astype(BF16)

    @pl.when(pl.program_id(0) >= n_in)
    def _():
        o_ref[...] = jnp.zeros_like(o_ref)


def cast_rows_bf16(w, first_row, n_rows, n_rows_out, *, rows=96):
    k = w.shape[1]
    assert first_row % rows == 0 and n_rows % rows == 0 and n_rows_out % rows == 0
    first, n_in = first_row // rows, n_rows // rows
    return pl.pallas_call(
        functools.partial(_cast_rows_kernel, n_in=n_in),
        grid=(n_rows_out // rows,),
        in_specs=[pl.BlockSpec((rows, k), lambda i: (first + jnp.minimum(i, n_in - 1), 0))],
        out_specs=pl.BlockSpec((rows, k), lambda i: (i, 0)),
        out_shape=jax.ShapeDtypeStruct((n_rows_out, k), BF16),
        compiler_params=_cp(("parallel",)),
        name="cast_rows_bf16",
    )(w)


def _mm_res_kernel(*refs, n_lhs):
    a_refs = refs[:n_lhs]
    w_refs = refs[n_lhs:2 * n_lhs]
    res_ref = refs[2 * n_lhs]
    o_ref = refs[2 * n_lhs + 1]
    acc = res_ref[...]
    for a_ref, w_ref in zip(a_refs, w_refs):
        acc = acc + _dot(a_ref[...].astype(BF16), w_ref[...])
    o_ref[...] = acc


def matmul_residual(lhs_list, w_list, res, *, tm, tn):
    m, n = res.shape
    n_lhs = len(lhs_list)
    assert m % tm == 0 and n % tn == 0
    in_specs = [pl.BlockSpec((tm, a.shape[1]), lambda i, j: (i, 0)) for a in lhs_list]
    in_specs += [pl.BlockSpec((w.shape[0], tn), lambda i, j: (0, j)) for w in w_list]
    in_specs += [pl.BlockSpec((tm, tn), lambda i, j: (i, j))]
    return pl.pallas_call(
        functools.partial(_mm_res_kernel, n_lhs=n_lhs),
        grid=(m // tm, n // tn),
        in_specs=in_specs,
        out_specs=pl.BlockSpec((tm, tn), lambda i, j: (i, j)),
        out_shape=jax.ShapeDtypeStruct((m, n), F32),
        compiler_params=_cp(("parallel", "arbitrary")),
        name="matmul_residual",
    )(*lhs_list, *w_list, res)


def rope_tables(pos):
    half = ROT_DIM // 2
    f32 = np.float32
    inv = f32(ROPE_THETA) ** (-np.arange(half, dtype=f32) * f32(2.0) / f32(ROT_DIM))
    ang = (np.asarray(pos, f32)[:, None] * inv[None, :].astype(f32)).astype(f32)
    cos, sin = np.cos(ang.astype(np.float64)).astype(f32), np.sin(ang.astype(np.float64)).astype(f32)
    t = ang.shape[0]
    ones = np.ones((t, HEAD_DIM - ROT_DIM), f32)
    zeros = np.zeros((t, HEAD_DIM - ROT_DIM), f32)
    z8 = np.zeros((t, half), f32)
    cos_t = np.concatenate([cos, cos, ones], axis=1)
    sin_a = np.concatenate([z8, sin, zeros], axis=1)
    sin_b = np.concatenate([-sin, z8, zeros], axis=1)
    return tuple(jnp.asarray(np.concatenate([a, a], axis=1)) for a in (cos_t, sin_a, sin_b))


def _norm_rope_chunk(x, w, cos_t, sin_a, sin_b, gmat):
    ms = _group_sum(x * x, gmat) * (1.0 / HEAD_DIM)
    xn = x * lax.rsqrt(ms + RMS_EPS) * w
    half = ROT_DIM // 2
    return (xn * cos_t + pltpu.roll(xn, half, axis=1) * sin_a
            + pltpu.roll(xn, LANES - half, axis=1) * sin_b)


def _norm_rope(x, w, tabs, gmat):
    chunks = [
        _norm_rope_chunk(x[:, c * LANES:(c + 1) * LANES], w, *tabs, gmat)
        for c in range(x.shape[1] // LANES)
    ]
    return chunks[0] if len(chunks) == 1 else jnp.concatenate(chunks, axis=1)


def _sink_softmax(s, sink):
    m = jnp.maximum(jnp.max(s, axis=-1, keepdims=True), sink)
    e = jnp.exp(s - m)
    return e / (jnp.sum(e, axis=-1, keepdims=True) + jnp.exp(sink - m))


def _swa_prompt_kernel(q_ref, kc_ref, vc_ref, kp_ref, vp_ref, cc_ref, sac_ref, sbc_ref,
                       cp_ref, sap_ref, sbp_ref, qw_ref, kw_ref, sink_ref, o_ref, kn_ref):
    n = pl.program_id(1)
    blk = q_ref.shape[0]
    gmat = _head_indicator()
    tabs_c = (cc_ref[...], sac_ref[...], sbc_ref[...])
    tabs_p = (cp_ref[...], sap_ref[...], sbp_ref[...])
    q = _norm_rope(q_ref[...], qw_ref[...], tabs_c, gmat)
    k_cur = _norm_rope(kc_ref[...], kw_ref[...], tabs_c, gmat)
    k_prev = _norm_rope(kp_ref[...], kw_ref[...], tabs_p, gmat)
    kn_ref[...] = k_cur
    k_all = jnp.concatenate([k_prev, k_cur], axis=0).astype(BF16)
    v_all = jnp.concatenate([vp_ref[...], vc_ref[...]], axis=0).astype(BF16)

    qi = lax.broadcasted_iota(jnp.int32, (blk, 2 * blk), 0) + blk
    si = lax.broadcasted_iota(jnp.int32, (blk, 2 * blk), 1)
    rel = qi - si
    valid = (rel >= 0) & (rel <= WINDOW) & ((n > 0) | (si >= blk))

    groups = range(ATT_KV_HEADS)
    lanes = [slice(kv * HEAD_DIM, (kv + 1) * HEAD_DIM) for kv in groups]
    heads = [[kv * ATT_GROUP + g for g in range(ATT_GROUP)] for kv in groups]
    q_g = [jnp.concatenate([q[:, h * HEAD_DIM:(h + 1) * HEAD_DIM] for h in heads[kv]],
                           axis=0).astype(BF16) for kv in groups]
    s = [_dot_nt(q_g[kv], k_all[:, lanes[kv]]) * ATT_SCALE for kv in groups]
    p = [jnp.concatenate(
        [_sink_softmax(jnp.where(valid, s[kv][g * blk:(g + 1) * blk], -jnp.inf), sink_ref[h])
         for g, h in enumerate(heads[kv])], axis=0).astype(BF16) for kv in groups]
    o = [_dot(p[kv], v_all[:, lanes[kv]]) for kv in groups]
    for kv in groups:
        for g, h in enumerate(heads[kv]):
            o_ref[:, h * HEAD_DIM:(h + 1) * HEAD_DIM] = o[kv][g * blk:(g + 1) * blk]


def swa_prompt(pa, tabs, q_norm_w, k_norm_w, sinks):
    b, t, _ = pa.shape
    blk = WINDOW
    nb = t // blk
    qb, kb, vb = 0, ATT_WIDTH // KV_WIDTH, ATT_WIDTH // KV_WIDTH + 1
    cur = lambda i, n, *_: (i, n, 0)
    tab_cur = pl.BlockSpec((blk, LANES), lambda i, n: (n, 0))
    tab_prev = pl.BlockSpec((blk, LANES), lambda i, n: (jnp.maximum(n - 1, 0), 0))
    qw = jnp.tile(q_norm_w.reshape(1, HEAD_DIM), (1, 2))
    kw = jnp.tile(k_norm_w.reshape(1, HEAD_DIM), (1, 2))
    return pl.pallas_call(
        _swa_prompt_kernel,
        grid=(b, nb),
        in_specs=[
            pl.BlockSpec((None, blk, ATT_WIDTH), lambda i, n: (i, n, qb)),
            pl.BlockSpec((None, blk, KV_WIDTH), lambda i, n: (i, n, kb)),
            pl.BlockSpec((None, blk, KV_WIDTH), lambda i, n: (i, n, vb)),
            pl.BlockSpec((None, blk, KV_WIDTH), lambda i, n: (i, jnp.maximum(n - 1, 0), kb)),
            pl.BlockSpec((None, blk, KV_WIDTH), lambda i, n: (i, jnp.maximum(n - 1, 0), vb)),
            tab_cur, tab_cur, tab_cur, tab_prev, tab_prev, tab_prev,
            pl.BlockSpec((1, LANES), lambda i, n: (0, 0)),
            pl.BlockSpec((1, LANES), lambda i, n: (0, 0)),
            pl.BlockSpec(memory_space=pltpu.SMEM),
        ],
        out_specs=[
            pl.BlockSpec((None, blk, ATT_WIDTH), cur),
            pl.BlockSpec((None, blk, KV_WIDTH), cur),
        ],
        out_shape=[
            jax.ShapeDtypeStruct((b, t, ATT_WIDTH), F32),
            jax.ShapeDtypeStruct((b, t, KV_WIDTH), F32),
        ],
        compiler_params=_cp(("parallel", "arbitrary")),
        name="swa_prompt",
    )(pa, pa, pa, pa, pa, *tabs, *tabs, qw, kw, sinks)


def _qk_norm_rope_kernel(x_ref, w_ref, c_ref, sa_ref, sb_ref, o_ref):
    gmat = _head_indicator()
    tabs = (c_ref[...], sa_ref[...], sb_ref[...])
    for c in range(x_ref.shape[1] // LANES):
        sl = slice(c * LANES, (c + 1) * LANES)
        o_ref[:, sl] = _norm_rope_chunk(x_ref[:, sl], w_ref[:, sl], *tabs, gmat)


def qk_norm_rope(x, w_row, tabs):
    m, w = x.shape
    full = lambda *shape: pl.BlockSpec(shape, lambda: (0,) * len(shape))
    return pl.pallas_call(
        _qk_norm_rope_kernel,
        in_specs=[full(m, w), full(1, w), full(1, LANES), full(1, LANES), full(1, LANES)],
        out_specs=full(m, w),
        out_shape=jax.ShapeDtypeStruct((m, w), F32),
        name="qk_norm_rope",
    )(x, w_row, *tabs)


def _swa_decode_kernel(q_ref, kn_ref, vn_ref, knt_ref, vnt_ref, ck_ref, cv_ref, sink_ref, o_ref,
                       kw_ref, vw_ref):
    bb = q_ref.shape[0]
    nbuf = ck_ref.shape[2]
    row_kv = lax.broadcasted_iota(jnp.int32, (ATT_HEADS, KV_WIDTH), 0) // ATT_GROUP
    lane_kv = lax.broadcasted_iota(jnp.int32, (ATT_HEADS, KV_WIDTH), 1) // HEAD_DIM
    own = row_kv == lane_kv
    sink = sink_ref[...]
    seq = lax.broadcasted_iota(jnp.int32, knt_ref.shape, 1)
    pos = lax.broadcasted_iota(jnp.int32, (KV_WIDTH, nbuf), 1)
    b0 = pl.program_id(0) * bb
    seqs = range(bb)
    q_exp = [jnp.where(own, jnp.concatenate([q_ref[b]] * ATT_KV_HEADS, axis=1), 0.0)
             for b in seqs]
    s_buf = [_dot(q_exp[b].astype(BF16), ck_ref[b].astype(BF16)) * ATT_SCALE for b in seqs]
    s_new = [jnp.sum(q_exp[b] * kn_ref[b], axis=-1, keepdims=True) * ATT_SCALE for b in seqs]
    m = [jnp.maximum(jnp.maximum(jnp.max(s_buf[b], axis=-1, keepdims=True), s_new[b]), sink)
         for b in seqs]
    e_buf = [jnp.exp(s_buf[b] - m[b]) for b in seqs]
    e_new = [jnp.exp(s_new[b] - m[b]) for b in seqs]
    inv = [1.0 / (jnp.sum(e_buf[b], axis=-1, keepdims=True) + e_new[b] + jnp.exp(sink - m[b]))
           for b in seqs]
    o = [_dot_nt((e_buf[b] * inv[b]).astype(BF16), cv_ref[b].astype(BF16))
         + (e_new[b] * inv[b]) * vn_ref[b] for b in seqs]
    for b in seqs:
        ob = jnp.where(own, o[b], 0.0)
        o_ref[b] = (ob[:, 0:HEAD_DIM] + ob[:, HEAD_DIM:2 * HEAD_DIM]
                    + ob[:, 2 * HEAD_DIM:3 * HEAD_DIM] + ob[:, 3 * HEAD_DIM:4 * HEAD_DIM])
        k_col = jnp.sum(jnp.where(seq == b0 + b, knt_ref[...], 0.0), axis=1, keepdims=True)
        v_col = jnp.sum(jnp.where(seq == b0 + b, vnt_ref[...], 0.0), axis=1, keepdims=True)
        kw_ref[b] = jnp.where(pos == nbuf - 1, k_col, pltpu.roll(ck_ref[b], nbuf - 1, axis=1))
        vw_ref[b] = jnp.where(pos == nbuf - 1, v_col, pltpu.roll(cv_ref[b], nbuf - 1, axis=1))


def swa_decode(q, k_new, v_new, cache_kt, cache_vt, sinks, *, bb=8):
    b, _, nbuf = cache_kt.shape
    blk3 = lambda s1, s2: pl.BlockSpec((bb, s1, s2), lambda i: (i, 0, 0))
    whole = pl.BlockSpec((KV_WIDTH, b), lambda i: (0, 0))
    return pl.pallas_call(
        _swa_decode_kernel,
        grid=(b // bb,),
        in_specs=[
            blk3(ATT_HEADS, HEAD_DIM), blk3(1, KV_WIDTH), blk3(1, KV_WIDTH), whole, whole,
            blk3(KV_WIDTH, nbuf), blk3(KV_WIDTH, nbuf),
            pl.BlockSpec((ATT_HEADS, 1), lambda i: (0, 0)),
        ],
        out_specs=[blk3(ATT_HEADS, HEAD_DIM), blk3(KV_WIDTH, nbuf), blk3(KV_WIDTH, nbuf)],
        out_shape=[
            jax.ShapeDtypeStruct((b, ATT_HEADS, HEAD_DIM), F32),
            jax.ShapeDtypeStruct((b, KV_WIDTH, nbuf), F32),
            jax.ShapeDtypeStruct((b, KV_WIDTH, nbuf), F32),
        ],
        compiler_params=_cp(("parallel",)),
        name="swa_decode",
    )(q, k_new, v_new, k_new.reshape(b, KV_WIDTH).T, v_new.reshape(b, KV_WIDTH).T,
      cache_kt, cache_vt, sinks.reshape(ATT_HEADS, 1))


def _head_rms_kernel(x_ref, w_ref, o_ref):
    for h in range(x_ref.shape[1] // XATT_HEAD_DIM):
        sl = slice(h * XATT_HEAD_DIM, (h + 1) * XATT_HEAD_DIM)
        o_ref[:, sl] = _rms_rows(x_ref[:, sl], w_ref[...])


def head_rms(x, w):
    m, wd = x.shape
    return pl.pallas_call(
        _head_rms_kernel,
        in_specs=[pl.BlockSpec((m, wd), lambda: (0, 0)),
                  pl.BlockSpec((1, XATT_HEAD_DIM), lambda: (0, 0))],
        out_specs=pl.BlockSpec((m, wd), lambda: (0, 0)),
        out_shape=jax.ShapeDtypeStruct((m, wd), F32),
        name="head_rms",
    )(x, w.reshape(1, XATT_HEAD_DIM))


def _xattn_prompt_kernel(q_ref, k_ref, v_ref, w_ref, o_ref):
    scale = 1.0 / math.sqrt(XATT_HEAD_DIM)
    for h in range(XATT_HEADS):
        sl = slice(h * XATT_HEAD_DIM, (h + 1) * XATT_HEAD_DIM)
        qn = _rms_rows(q_ref[:, sl], w_ref[...]).astype(BF16)
        s = _dot_nt(qn, k_ref[:, sl].astype(BF16)) * scale
        e = jnp.exp(s - jnp.max(s, axis=-1, keepdims=True))
        p = e / jnp.sum(e, axis=-1, keepdims=True)
        o_ref[:, sl] = _dot(p.astype(BF16), v_ref[:, sl].astype(BF16))


def xattn_prompt(q, mem_k, mem_v, xq_norm_w, *, tq=512):
    b, t, w = q.shape
    n_mem = mem_k.shape[1]
    return pl.pallas_call(
        _xattn_prompt_kernel,
        grid=(b, t // tq),
        in_specs=[
            pl.BlockSpec((None, tq, w), lambda i, j: (i, j, 0)),
            pl.BlockSpec((None, n_mem, w), lambda i, j: (i, 0, 0)),
            pl.BlockSpec((None, n_mem, w), lambda i, j: (i, 0, 0)),
            pl.BlockSpec((1, XATT_HEAD_DIM), lambda i, j: (0, 0)),
        ],
        out_specs=pl.BlockSpec((None, tq, w), lambda i, j: (i, j, 0)),
        out_shape=jax.ShapeDtypeStruct((b, t, w), F32),
        compiler_params=_cp(("parallel", "arbitrary")),
        name="xattn_prompt",
    )(q, mem_k, mem_v, xq_norm_w.reshape(1, XATT_HEAD_DIM))


def _xattn_decode_kernel(q_ref, k_ref, v_ref, w_ref, o_ref):
    bb, rows, _ = q_ref.shape
    n_keys = k_ref.shape[1]
    scale = 1.0 / math.sqrt(XATT_HEAD_DIM)
    own = (lax.broadcasted_iota(jnp.int32, (rows, n_keys), 1) % XATT_HEADS
           == lax.broadcasted_iota(jnp.int32, (rows, n_keys), 0) % XATT_HEADS)
    seqs = range(bb)
    qn = [_rms_rows(q_ref[b], w_ref[...]).astype(BF16) for b in seqs]
    s = [jnp.where(own, _dot_nt(qn[b], k_ref[b].astype(BF16)) * scale, -jnp.inf) for b in seqs]
    e = [jnp.exp(s[b] - jnp.max(s[b], axis=-1, keepdims=True)) for b in seqs]
    p = [(e[b] / jnp.sum(e[b], axis=-1, keepdims=True)).astype(BF16) for b in seqs]
    for b in seqs:
        o_ref[b] = _dot(p[b], v_ref[b].astype(BF16))


def xattn_decode(q_pad, mem_k, mem_v, xq_norm_w, *, bb=8):
    b, rows, _ = q_pad.shape
    n_keys = mem_k.shape[1]
    kv = pl.BlockSpec((bb, n_keys, XATT_HEAD_DIM), lambda i: (i, 0, 0))
    return pl.pallas_call(
        _xattn_decode_kernel,
        grid=(b // bb,),
        in_specs=[pl.BlockSpec((bb, rows, XATT_HEAD_DIM), lambda i: (i, 0, 0)), kv, kv,
                  pl.BlockSpec((1, XATT_HEAD_DIM), lambda i: (0, 0))],
        out_specs=pl.BlockSpec((bb, rows, XATT_HEAD_DIM), lambda i: (i, 0, 0)),
        out_shape=jax.ShapeDtypeStruct((b, rows, XATT_HEAD_DIM), F32),
        compiler_params=_cp(("parallel",)),
        name="xattn_decode",
    )(q_pad, mem_k, mem_v, xq_norm_w.reshape(1, XATT_HEAD_DIM))


LORA_OFF = 3 * RWKV_WIDTH
GATE_OFF = LORA_OFF + DECAY_LORA + AAA_LORA
GATE_PAD = RWKV_PROJ_PAD - GATE_OFF


def _sigmoid(x):
    return 1.0 / (1.0 + jnp.exp(-x))


def _per_chunk(fn, *arrays):
    w = arrays[0].shape[1]
    outs = [fn(*(a[:, c * LANES:(c + 1) * LANES] for a in arrays)) for c in range(w // LANES)]
    return jnp.concatenate(outs, axis=1)


def _rwkv_prep_core(pr, prev, mu, w0, a0, kk_w, ka_w, rk_w, w_lora, w_gate):
    c = RWKV_WIDTH
    gmat = _head_indicator()
    xm = pr + (prev - pr) * mu
    r, k, v = xm[:, 0:c], xm[:, c:2 * c], xm[:, 2 * c:3 * c]
    lora = xm[:, LORA_OFF:LORA_OFF + LANES]
    lane = lax.broadcasted_iota(jnp.int32, lora.shape, 1)
    lora_in = jnp.where(lane < DECAY_LORA, jnp.tanh(lora), lora)
    wa = _dot(lora_in.astype(BF16), w_lora)
    z = -(w0 + wa[:, 0:c])
    softplus = jnp.maximum(z, 0.0) + jnp.log(1.0 + jnp.exp(-jnp.abs(z)))
    log_decay = -jnp.exp(-softplus - 0.5)
    a = _sigmoid(a0 + wa[:, c:2 * c])
    g = _dot(_sigmoid(xm[:, GATE_OFF:GATE_OFF + GATE_PAD]).astype(BF16), w_gate)
    kk = k * kk_w
    norm = jnp.sqrt(_per_chunk(lambda t: _group_sum(t * t, gmat), kk))
    kk = kk / jnp.maximum(norm, 1e-12)
    kp = k * (1.0 + (a - 1.0) * ka_w)
    bonus = _per_chunk(lambda t: _group_sum(t, gmat), r * kp * rk_w) * v
    return r, log_decay, kp, v, kk, kk * a, bonus, g


def _rwkv_prep_tok_kernel(pr_ref, prev_ref, mu_ref, w0_ref, a0_ref, kkw_ref, kaw_ref, rkw_ref,
                          wl_ref, wg_ref, *out_refs):
    outs = _rwkv_prep_core(pr_ref[...], prev_ref[...], mu_ref[...], w0_ref[...], a0_ref[...],
                           kkw_ref[...], kaw_ref[...], rkw_ref[...], wl_ref[...], wg_ref[...])
    for k, (o_ref, o) in enumerate(zip(out_refs, outs)):
        o_ref[...] = o.T if k < N_STEP_VECS else o


def _rwkv_param_specs(index_map):
    c = RWKV_WIDTH
    shapes = [(1, RWKV_PROJ_PAD)] + [(1, c)] * 5 + [(LANES, 2 * c), (GATE_PAD, c)]
    return [pl.BlockSpec(s, index_map) for s in shapes]


N_STEP_VECS = 6


def rwkv_prep_tok(pr, prev, params):
    m, wd = pr.shape
    c = RWKV_WIDTH
    shapes = [(c, m)] * N_STEP_VECS + [(m, c)] * 2
    return pl.pallas_call(
        _rwkv_prep_tok_kernel,
        grid=(1,),
        in_specs=[pl.BlockSpec((m, wd), lambda i: (0, 0))] * 2
        + _rwkv_param_specs(lambda i: (0, 0)),
        out_specs=[pl.BlockSpec(s, lambda i: (0, 0)) for s in shapes],
        out_shape=[jax.ShapeDtypeStruct(s, F32) for s in shapes],
        compiler_params=_cp(("arbitrary",)),
        name="rwkv_prep_tok",
    )(pr, prev, *params)


def _dot_tn(a, b):
    return lax.dot_general(a, b, (((0,), (0,)), ((), ())), preferred_element_type=F32)


def _rwkv_scan_tile(r_all, ld_all, kp_all, v_all, kk_all, b_all, s_cur, y_ref):
    n = CHUNK
    nc = r_all.shape[0] // n
    ti = lax.broadcasted_iota(jnp.int32, (n, n), 0)
    si = lax.broadcasted_iota(jnp.int32, (n, n), 1)
    tri = jnp.where(si <= ti, 1.0, 0.0).astype(BF16)
    t2 = lax.broadcasted_iota(jnp.int32, (n, LANES), 0)
    lane2 = lax.broadcasted_iota(jnp.int32, (n, LANES), 1)
    s2 = lane2 % n
    low = lane2 < n
    strict2, incl2, eye2 = s2 < t2, s2 <= t2, s2 == t2
    low4 = lax.broadcasted_iota(jnp.int32, (2 * n, LANES), 1) < n
    top4 = lax.broadcasted_iota(jnp.int32, (2 * n, LANES), 0) < n
    diag_blk = top4 == low4
    own4 = {0: low4, 1: ~low4}
    pairs = range(RWKV_HEADS // 2)
    cs = [slice(c * LANES, (c + 1) * LANES) for c in pairs]

    pre = []
    for j in range(nc):
        rows = slice(j * n, (j + 1) * n)
        ld = ld_all[rows, :]
        l1, l2, l3 = _split3(ld)
        lc = _dot(tri, l1) + _dot(tri, l2) + _dot(tri, l3)
        lc_end = lc[n - 1:n, :]
        e_neg = jnp.exp(-lc)
        kk, b, kp = kk_all[rows, :], b_all[rows, :], kp_all[rows, :]
        to_end = jnp.exp(lc_end - lc)
        pre.append(dict(
            a_t=(-kk * jnp.exp(lc - ld)).astype(BF16), b_t=(b * e_neg).astype(BF16),
            k_t=(kp * e_neg).astype(BF16), r_t=(r_all[rows, :] * jnp.exp(lc)).astype(BF16),
            b_e=(b * to_end).astype(BF16), k_e=(kp * to_end).astype(BF16),
            v_b=v_all[rows, :].astype(BF16), g_end=jnp.exp(lc_end)))
    items = [(j, c, par) for j in range(nc) for c in pairs for par in (0, 1)]
    at = {it: i for i, it in enumerate(items)}
    ar = {(j, c): jnp.concatenate([pre[j]['a_t'][:, cs[c]], pre[j]['r_t'][:, cs[c]]], axis=0)
          for j in range(nc) for c in pairs}
    kb = {(j, c): jnp.concatenate([pre[j]['k_t'][:, cs[c]], pre[j]['b_t'][:, cs[c]]], axis=0)
          for j in range(nc) for c in pairs}
    gm = [_dot_nt(jnp.where(own4[par], ar[j, c], 0).astype(BF16), kb[j, c]) for j, c, par in items]
    top = [g[0:n] for g in gm]
    pm = [jnp.where(incl2, g[n:2 * n], 0.0).astype(BF16) for g in gm]
    lak = [jnp.where(strict2 & low, t, 0.0).astype(BF16) for t in top]
    lv = [_dot(lak[i][:, 0:n], pre[j]['v_b'][:, cs[c]]) for i, (j, c, par) in enumerate(items)]
    z = [jnp.where(low, jnp.where(eye2, 1.0, 0.0), jnp.where(strict2, t, 0.0)) for t in top]
    for _ in range(6):
        zb = [zz.astype(BF16) for zz in z]
        res = [_dot(jnp.where(low, 0, zb[i]).astype(BF16), jnp.concatenate([zb[i], zb[i]], axis=0))
               for i in range(len(items))]
        z = [res[i] + jnp.where(low, z[i], 0.0) for i in range(len(items))]
    tmat = [zz[:, 0:n].astype(BF16) for zz in z]

    for j in range(nc):
        p = pre[j]
        sw = [_dot_nt(ar[j, c], s_cur[c].astype(BF16)) for c in pairs]
        w0 = [(sw[c][0:n] + jnp.where(low, lv[at[j, c, 0]], lv[at[j, c, 1]])).astype(BF16)
              for c in pairs]
        u = [jnp.where(low, _dot(tmat[at[j, c, 0]], w0[c]),
                       _dot(tmat[at[j, c, 1]], w0[c])).astype(BF16) for c in pairs]
        vu = [jnp.concatenate([p['v_b'][:, cs[c]], u[c]], axis=0) for c in pairs]
        yb = [jnp.where(low, _dot(pm[at[j, c, 0]], vu[c]), _dot(pm[at[j, c, 1]], vu[c]))
              for c in pairs]
        for c in pairs:
            y_ref[j * n:(j + 1) * n, cs[c]] = sw[c][n:2 * n] + yb[c]
        upd = [_dot_tn(jnp.concatenate([u[c], p['v_b'][:, cs[c]]], axis=0),
                       jnp.concatenate([p['b_e'][:, cs[c]], p['k_e'][:, cs[c]]], axis=0))
               for c in pairs]
        s_cur = [s_cur[c] * p['g_end'][:, cs[c]] + jnp.where(diag_blk, upd[c], 0.0) for c in pairs]
    return s_cur


def _group_norm_gate(y, bonus, g, ln_w, ln_b, gmat):
    inv = 1.0 / RWKV_HEAD_DIM
    d = y - _group_sum(y, gmat) * inv
    var = _group_sum(d * d, gmat) * inv
    return (d * lax.rsqrt(var + GN_EPS) * ln_w + ln_b + bonus) * g


def _rwkv_seq_kernel(pr_ref, prev0_ref, mu_ref, w0_ref, a0_ref, kkw_ref, kaw_ref, rkw_ref,
                     wl_ref, wg_ref, lnw_ref, lnb_ref, o_ref, s_out_ref, s_ref, last_ref, y_ref):
    @pl.when(pl.program_id(1) == 0)
    def _():
        s_ref[...] = jnp.zeros_like(s_ref)
        last_ref[...] = prev0_ref[...]

    pr = pr_ref[...]
    rows = pr.shape[0]
    row = lax.broadcasted_iota(jnp.int32, (rows, 1), 0)
    prev = jnp.where(row == 0, last_ref[...], pltpu.roll(pr, 1, axis=0))
    last_ref[...] = pr[rows - 1:rows, :]
    r, ld, kp, v, kk, b, bonus, g = _rwkv_prep_core(
        pr, prev, mu_ref[...], w0_ref[...], a0_ref[...], kkw_ref[...], kaw_ref[...],
        rkw_ref[...], wl_ref[...], wg_ref[...])
    pairs = range(RWKV_HEADS // 2)
    s_new = _rwkv_scan_tile(r, ld, kp, v, kk, b, [s_ref[c] for c in pairs], y_ref)
    n = RWKV_HEAD_DIM
    for c in pairs:
        s_ref[c] = s_new[c]
        s_out_ref[2 * c] = s_new[c][0:n, 0:n]
        s_out_ref[2 * c + 1] = s_new[c][n:2 * n, n:2 * n]
    gmat = _head_indicator()
    for c in pairs:
        sl = slice(c * LANES, (c + 1) * LANES)
        o_ref[:, sl] = _group_norm_gate(y_ref[:, sl], bonus[:, sl], g[:, sl], lnw_ref[:, sl],
                                        lnb_ref[:, sl], gmat)


SCAN_CHUNKS = 4


def rwkv_seq(pr, prev0, params, ln_w, ln_b):
    bsz, t, wd = pr.shape
    c = RWKV_WIDTH
    rows = SCAN_CHUNKS * CHUNK
    assert t % rows == 0
    vec = pl.BlockSpec((1, c), lambda i, j: (0, 0))
    st = pl.BlockSpec((None, RWKV_HEADS, RWKV_HEAD_DIM, RWKV_HEAD_DIM), lambda i, j: (i, 0, 0, 0))
    return pl.pallas_call(
        _rwkv_seq_kernel,
        grid=(bsz, t // rows),
        in_specs=[pl.BlockSpec((None, rows, wd), lambda i, j: (i, j, 0)),
                  pl.BlockSpec((None, 1, wd), lambda i, j: (i, 0, 0))]
        + _rwkv_param_specs(lambda i, j: (0, 0)) + [vec, vec],
        out_specs=[pl.BlockSpec((None, rows, c), lambda i, j: (i, j, 0)), st],
        out_shape=[jax.ShapeDtypeStruct((bsz, t, c), F32),
                   jax.ShapeDtypeStruct((bsz, RWKV_HEADS, RWKV_HEAD_DIM, RWKV_HEAD_DIM), F32)],
        scratch_shapes=[pltpu.VMEM((RWKV_HEADS // 2, LANES, LANES), F32),
                        pltpu.VMEM((1, wd), F32), pltpu.VMEM((rows, c), F32)],
        compiler_params=_cp(("parallel", "arbitrary")),
        name="rwkv_seq",
    )(pr, prev0, *params, ln_w.reshape(1, c), ln_b.reshape(1, c))


STEP_UNROLL = 8


def _rwkv_step_kernel(r_ref, ld_ref, kp_ref, v_ref, kk_ref, b_ref, s_ref, y_ref, s_out_ref):
    n = RWKV_HEAD_DIM
    neg_kk, decay = -kk_ref[...], jnp.exp(ld_ref[...])
    b_mat, kp_mat, r_mat = b_ref[...], kp_ref[...], r_ref[...]

    def body(i, carry):
        v0 = pl.multiple_of(i * STEP_UNROLL, STEP_UNROLL)
        v_rows = v_ref[pl.ds(v0, STEP_UNROLL), :]
        rows = range(STEP_UNROLL)
        s = [s_ref[v0 + j] for j in rows]
        sa = [jnp.sum(s[j] * neg_kk, axis=0, keepdims=True) for j in rows]
        s_new = [s[j] * decay + sa[j] * b_mat + v_rows[j:j + 1, :] * kp_mat for j in rows]
        y = [jnp.sum(s_new[j] * r_mat, axis=0, keepdims=True) for j in rows]
        for j in rows:
            s_out_ref[v0 + j] = s_new[j]
        y_ref[pl.ds(v0, STEP_UNROLL), :] = jnp.concatenate(y, axis=0)
        return carry

    lax.fori_loop(0, n // STEP_UNROLL, body, 0)


def rwkv_step(r, ld, kp, v, kk, b, state_t):
    _, nh, n, _, bsz = state_t.shape
    vec = pl.BlockSpec((n, bsz), lambda h: (h, 0))
    st = pl.BlockSpec((None, None, n, n, bsz), lambda h: (0, h, 0, 0, 0))
    return pl.pallas_call(
        _rwkv_step_kernel,
        grid=(nh,),
        in_specs=[vec] * 6 + [st],
        out_specs=[vec, st],
        out_shape=[jax.ShapeDtypeStruct((nh * n, bsz), F32),
                   jax.ShapeDtypeStruct(state_t.shape, F32)],
        compiler_params=_cp(("parallel",)),
        name="rwkv_step",
    )(r, ld, kp, v, kk, b, state_t)


def _rwkv_post_kernel(y_ref, bonus_ref, g_ref, lnw_ref, lnb_ref, o_ref, *, y_channel_major):
    gmat = _head_indicator()
    for c in range(o_ref.shape[1] // LANES):
        sl = slice(c * LANES, (c + 1) * LANES)
        y = y_ref[sl, :].T if y_channel_major else y_ref[:, sl]
        o_ref[:, sl] = _group_norm_gate(y, bonus_ref[:, sl], g_ref[:, sl], lnw_ref[:, sl],
                                        lnb_ref[:, sl], gmat)


def rwkv_post(y, bonus, g, ln_w, ln_b, *, tm, y_channel_major=False):
    m, c = bonus.shape
    blk = pl.BlockSpec((tm, c), lambda i: (i, 0))
    y_blk = pl.BlockSpec((c, tm), lambda i: (0, i)) if y_channel_major else blk
    vec = pl.BlockSpec((1, c), lambda i: (0, 0))
    return pl.pallas_call(
        functools.partial(_rwkv_post_kernel, y_channel_major=y_channel_major),
        grid=(m // tm,),
        in_specs=[y_blk, blk, blk, vec, vec],
        out_specs=blk,
        out_shape=jax.ShapeDtypeStruct((m, c), F32),
        compiler_params=_cp(("parallel",)),
        name="rwkv_post",
    )(y, bonus, g, ln_w.reshape(1, c), ln_b.reshape(1, c))


ROUTER_LANES = LANES
ROW_TILES = 1
ROW_LANES = D_MODEL // ROW_TILES


def _rows_to_tiles(ref, x):
    rows = x.shape[0]
    if ROW_TILES == 1:
        ref[...] = x
        return
    for j in range(ROW_TILES):
        ref[pl.ds(j, rows, stride=ROW_TILES), :] = x[:, j * ROW_LANES:(j + 1) * ROW_LANES]


def _tiles_to_rows(ref, rows):
    if ROW_TILES == 1:
        return ref[...]
    return jnp.concatenate(
        [ref[pl.ds(j, rows, stride=ROW_TILES), :] for j in range(ROW_TILES)], axis=1)


def _router_kernel(ha_ref, hb_ref, lnw_ref, whi_ref, wlo_ref, bias_ref, u_ref, idx_ref, gate_ref,
                   *, steps_a):
    use_a = pl.program_id(0) < steps_a
    h = jnp.where(use_a, ha_ref[...], hb_ref[...])
    u = _rms_rows(h, lnw_ref[...])
    _rows_to_tiles(u_ref, u)
    u_hi, u_lo = _split2(u)
    w_hi = whi_ref[...]
    logits = _dot(u_hi, w_hi) + _dot(u_lo, w_hi) + _dot(u_hi, wlo_ref[...]) + bias_ref[...]
    lane = lax.broadcasted_iota(jnp.int32, logits.shape, 1)
    neg = -jnp.inf

    def first_max(x):
        m = jnp.max(x, axis=1, keepdims=True)
        return m, jnp.min(jnp.where(x == m, lane, ROUTER_LANES), axis=1, keepdims=True)

    gl = jnp.where(lane < N_EXPERT_GROUPS, logits, neg)
    g_max, g_idx = first_max(gl)
    g_gate = 1.0 / jnp.sum(jnp.exp(gl - g_max), axis=1, keepdims=True)
    lo = N_EXPERT_GROUPS + g_idx * EXPERTS_PER_GROUP
    el = jnp.where((lane >= lo) & (lane < lo + EXPERTS_PER_GROUP), logits, neg)
    v1, i1 = first_max(el)
    v2, i2 = first_max(jnp.where(lane == i1, neg, el))
    e2 = jnp.exp(v2 - v1)
    w1 = g_gate / (1.0 + e2)
    w2 = g_gate * e2 / (1.0 + e2)
    idx_ref[...] = jnp.where(lane == 0, i1 - N_EXPERT_GROUPS,
                             jnp.where(lane == 1, i2 - N_EXPERT_GROUPS, 0))
    gate_ref[...] = jnp.where(lane == 0, w1, jnp.where(lane == 1, w2, 0.0))


def moe_router(h_a, h_b, ln_w, w_hi, w_lo, bias, *, tm):
    (ma, d), mb = h_a.shape, h_b.shape[0]
    assert ma % tm == 0 and mb % tm == 0
    steps_a, steps_b = ma // tm, mb // tm
    m = ma + mb
    const = lambda r, w: pl.BlockSpec((r, w), lambda i: (0, 0))
    row = lambda w: pl.BlockSpec((tm, w), lambda i: (i, 0))
    return pl.pallas_call(
        functools.partial(_router_kernel, steps_a=steps_a),
        grid=(steps_a + steps_b,),
        in_specs=[pl.BlockSpec((tm, d), lambda i: (jnp.minimum(i, steps_a - 1), 0)),
                  pl.BlockSpec((tm, d), lambda i: (jnp.maximum(i - steps_a, 0), 0)),
                  const(1, d), const(d, ROUTER_LANES), const(d, ROUTER_LANES),
                  const(1, ROUTER_LANES)],
        out_specs=[pl.BlockSpec((tm * ROW_TILES, ROW_LANES), lambda i: (i, 0)),
                   row(ROUTER_LANES), row(ROUTER_LANES)],
        out_shape=[jax.ShapeDtypeStruct((m * ROW_TILES, ROW_LANES), F32),
                   jax.ShapeDtypeStruct((m, ROUTER_LANES), jnp.int32),
                   jax.ShapeDtypeStruct((m, ROUTER_LANES), F32)],
        compiler_params=_cp(("arbitrary",)),
        name="moe_router",
    )(h_a, h_b, ln_w.reshape(1, d), w_hi, w_lo, bias)


X_SLOTS = 3
Y_SLOTS = 2


W_SLOTS = 3
DMA_QUEUES = 2


def _moe_expert_kernel(run_ref, rexp_ref, nused_ref, tok0_ref, tok1_ref, tok2_ref, dst_ref,
                       u_hbm, wg_hbm, wu_hbm, wd_hbm, y_hbm, xbuf, ybuf, wg_f, wu_f,
                       wd_f, wg_b, wu_b, wd_b, sem_in, sem_out, sem_w):
    i = pl.program_id(0)
    n_used, n_runs = nused_ref[0], nused_ref[1]
    tile_rows = MOE_BLOCK * ROW_TILES
    pad_base = y_hbm.shape[0] - Y_SLOTS * tile_rows
    run = run_ref[i]

    def weight_copies(k):
        e, s = rexp_ref[jnp.minimum(k, n_runs - 1)], lax.rem(k, W_SLOTS)
        copies = []
        for hbm, buf in ((wg_hbm, wg_f), (wu_hbm, wu_f), (wd_hbm, wd_f)):
            rows = buf.shape[1] // 2
            for part in range(2):
                sl = pl.ds(part * rows, rows)
                copies.append((pltpu.make_async_copy(hbm.at[e, sl], buf.at[s, sl], sem_w.at[s]),
                               part))
        return copies

    def gather_block(idx_ref, x_slot):
        for r in range(MOE_BLOCK):
            pltpu.make_async_copy(u_hbm.at[pl.ds(idx_ref[0, 0, r], ROW_TILES)],
                                  xbuf.at[x_slot, pl.ds(r * ROW_TILES, ROW_TILES)],
                                  sem_in.at[x_slot]).start(priority=r % DMA_QUEUES)

    def scatter_rows(idx_ref, y_slot, rows):
        for r in rows:
            pltpu.make_async_copy(ybuf.at[y_slot, pl.ds(r * ROW_TILES, ROW_TILES)],
                                  y_hbm.at[pl.ds(idx_ref[0, 0, r], ROW_TILES)],
                                  sem_out.at[y_slot]).start(priority=r % DMA_QUEUES)

    def gather_wait(x_slot):
        pltpu.make_async_copy(u_hbm.at[pl.ds(0, tile_rows)], xbuf.at[x_slot],
                              sem_in.at[x_slot]).wait()

    def scatter_wait(y_slot):
        pltpu.make_async_copy(ybuf.at[y_slot], y_hbm.at[pl.ds(0, tile_rows)],
                              sem_out.at[y_slot]).wait()

    @pl.when(i == 0)
    def _():
        ybuf[0] = jnp.zeros(ybuf.shape[1:], F32)
        for s in range(Y_SLOTS):
            pltpu.make_async_copy(ybuf.at[0], y_hbm.at[pl.ds(pad_base + s * tile_rows, tile_rows)],
                                  sem_out.at[s]).start()
        for k in range(W_SLOTS - 1):
            for cp, queue in weight_copies(k):
                cp.start(priority=queue)
        gather_block(tok0_ref, 0)
        gather_block(tok1_ref, 1)
        for s in range(Y_SLOTS):
            scatter_wait(s)

    @pl.when(i < n_used)
    def _():
        x_slot = lax.rem(i, X_SLOTS)
        y_slot = lax.rem(i, Y_SLOTS)

        @pl.when((i == 0) | (run != run_ref[jnp.maximum(i - 1, 0)]))
        def _():
            for cp, _ in weight_copies(run):
                cp.wait()
            w_slot = lax.rem(run, W_SLOTS)
            wg_b[...] = wg_f[w_slot].astype(BF16)
            wu_b[...] = wu_f[w_slot].astype(BF16)
            wd_b[...] = wd_f[w_slot].astype(BF16)
            for cp, queue in weight_copies(run + W_SLOTS - 1):
                cp.start(priority=queue)

        gather_wait(x_slot)

        @pl.when(i >= Y_SLOTS)
        def _():
            scatter_wait(y_slot)

        x = _tiles_to_rows(xbuf.at[x_slot], MOE_BLOCK).astype(BF16)
        hg = _dot(x, wg_b[...])
        hu = _dot(x, wu_b[...])
        act = (hg * _sigmoid(hg) * hu).astype(BF16)
        y = _dot(act, wd_b[...])
        _rows_to_tiles(ybuf.at[y_slot], y)
        scatter_rows(dst_ref, y_slot, range(MOE_BLOCK))
        gather_block(tok2_ref, lax.rem(i + 2, X_SLOTS))

        @pl.when(i == n_used - 1)
        def _():
            scatter_wait(y_slot)

            @pl.when(i >= 1)
            def _():
                scatter_wait(1 - y_slot)

            gather_wait(lax.rem(i + 1, X_SLOTS))
            gather_wait(lax.rem(i + 2, X_SLOTS))
            for k in range(W_SLOTS - 1):
                for cp, _ in weight_copies(n_runs + k):
                    cp.wait()


def moe_experts(u_all, row_src, row_dst, block_run, run_exp, n_used_runs, w_gate, w_up, w_down,
                n_assign):
    d, ff = w_gate.shape[1], w_gate.shape[2]
    n_blocks = row_src.shape[0]
    tile_rows = MOE_BLOCK * ROW_TILES
    smem_blk = lambda off: pl.BlockSpec(
        (1, 1, MOE_BLOCK), lambda i, *_: (jnp.clip(i + off, 0, n_blocks - 1), 0, 0),
        memory_space=pltpu.SMEM)
    hbm = pl.BlockSpec(memory_space=pl.ANY)
    grid_spec = pltpu.PrefetchScalarGridSpec(
        num_scalar_prefetch=3,
        grid=(n_blocks,),
        in_specs=[
            smem_blk(0), smem_blk(1), smem_blk(2), smem_blk(0),
            hbm, hbm, hbm, hbm,
        ],
        out_specs=hbm,
        scratch_shapes=[
            pltpu.VMEM((X_SLOTS, tile_rows, ROW_LANES), F32),
            pltpu.VMEM((Y_SLOTS, tile_rows, ROW_LANES), F32),
            pltpu.VMEM((W_SLOTS, d, ff), F32), pltpu.VMEM((W_SLOTS, d, ff), F32),
            pltpu.VMEM((W_SLOTS, ff, d), F32),
            pltpu.VMEM((d, ff), BF16), pltpu.VMEM((d, ff), BF16), pltpu.VMEM((ff, d), BF16),
            pltpu.SemaphoreType.DMA((X_SLOTS,)), pltpu.SemaphoreType.DMA((Y_SLOTS,)),
            pltpu.SemaphoreType.DMA((W_SLOTS,)),
        ],
    )
    y_rows = (n_assign + Y_SLOTS * MOE_BLOCK) * ROW_TILES
    return pl.pallas_call(
        _moe_expert_kernel,
        grid_spec=grid_spec,
        out_shape=jax.ShapeDtypeStruct((y_rows, ROW_LANES), F32),
        compiler_params=_cp(("arbitrary",), vmem=MOE_VMEM_LIMIT),
        name="moe_experts",
    )(block_run, run_exp, n_used_runs, row_src, row_src, row_src, row_dst, u_all,
      w_gate, w_up, w_down)


def _moe_combine_kernel(h_ref, y0_ref, y1_ref, gate_ref, o_ref):
    rows = h_ref.shape[0]
    gate = gate_ref[...]
    o_ref[...] = h_ref[...] + (_tiles_to_rows(y0_ref, rows) * gate[:, 0:1]
                               + _tiles_to_rows(y1_ref, rows) * gate[:, 1:2])


def moe_combine(h, y_slots, gates, row_off, slot_stride, *, tm):
    m, d = h.shape
    assert row_off % tm == 0 and slot_stride % tm == 0
    off0, off1 = row_off // tm, (row_off + slot_stride) // tm
    return pl.pallas_call(
        _moe_combine_kernel,
        grid=(m // tm,),
        in_specs=[pl.BlockSpec((tm, d), lambda i: (i, 0)),
                  pl.BlockSpec((tm * ROW_TILES, ROW_LANES), lambda i: (i + off0, 0)),
                  pl.BlockSpec((tm * ROW_TILES, ROW_LANES), lambda i: (i + off1, 0)),
                  pl.BlockSpec((tm, ROUTER_LANES), lambda i: (i + off0, 0))],
        out_specs=pl.BlockSpec((tm, d), lambda i: (i, 0)),
        out_shape=jax.ShapeDtypeStruct((m, d), F32),
        compiler_params=_cp(("parallel",)),
        name="moe_combine",
    )(h, y_slots, y_slots, gates)


def moe_dispatch(e_idx, slot_stride):
    m = e_idx.shape[0]
    a = m * TOP_K
    e_flat = e_idx.reshape(a)
    order = jnp.argsort(e_flat, stable=True).astype(jnp.int32)
    counts = jnp.sum(e_flat[:, None] == jnp.arange(N_EXPERTS, dtype=jnp.int32)[None, :],
                     axis=0, dtype=jnp.int32)
    pad_counts = (counts + MOE_BLOCK - 1) // MOE_BLOCK * MOE_BLOCK
    starts = jnp.cumsum(counts) - counts
    pad_ends = jnp.cumsum(pad_counts)
    pad_starts = pad_ends - pad_counts
    n_blocks = a // MOE_BLOCK + N_EXPERTS
    n_used = (pad_ends[-1] // MOE_BLOCK).astype(jnp.int32)
    blk = jnp.arange(n_blocks, dtype=jnp.int32)
    blk_start = jnp.minimum(blk, n_used - 1) * MOE_BLOCK
    block_exp = jnp.minimum(jnp.sum(blk_start[:, None] >= pad_ends[None, :], axis=1),
                            N_EXPERTS - 1).astype(jnp.int32)
    in_exp = blk * MOE_BLOCK - pad_starts[block_exp]
    row_cnt = jnp.where(blk < n_used, jnp.clip(counts[block_exp] - in_exp, 0, MOE_BLOCK), 0)
    lane = jnp.arange(MOE_BLOCK, dtype=jnp.int32)[None, :]
    valid = lane < row_cnt[:, None]
    src = jnp.clip((starts[block_exp] + in_exp)[:, None] + lane, 0, a - 1)
    assign = order[src]
    row_tok = jnp.where(valid, assign // TOP_K, 0)
    pad_dst = TOP_K * slot_stride + (blk % Y_SLOTS)[:, None] * MOE_BLOCK + lane
    row_dst = jnp.where(valid, (assign % TOP_K) * slot_stride + assign // TOP_K, pad_dst)
    as_blocks = lambda x: (x * ROW_TILES).astype(jnp.int32).reshape(n_blocks, 1, MOE_BLOCK)
    has_rows = counts > 0
    run_exp = jnp.argsort(~has_rows, stable=True).astype(jnp.int32)
    block_run = (jnp.cumsum(has_rows) - 1)[block_exp].astype(jnp.int32)
    n_used_runs = jnp.stack([n_used, jnp.sum(has_rows, dtype=jnp.int32)])
    return as_blocks(row_tok), as_blocks(row_dst), block_run, run_exp, n_used_runs


def rwkv_params(rw_mu, rw_w0, rw_w2, rw_a0, rw_a2, rw_g2, rw_k_k, rw_k_a, rw_r_k):
    c = RWKV_WIDTH
    mu = jnp.pad(rw_mu, (0, RWKV_PROJ_PAD - RWKV_PROJ)).reshape(1, RWKV_PROJ_PAD)
    w_lora = jnp.zeros((LANES, 2 * c), F32)
    w_lora = w_lora.at[0:DECAY_LORA, 0:c].set(rw_w2).at[DECAY_LORA:LANES, c:2 * c].set(rw_a2)
    w_gate = jnp.pad(rw_g2, ((0, GATE_PAD - GATE_LORA), (0, 0)))
    vec = lambda x: x.reshape(1, c)
    return (mu, vec(rw_w0), vec(rw_a0), vec(rw_k_k), vec(rw_k_a), vec(rw_r_k),
            w_lora.astype(BF16), w_gate.astype(BF16))


def _token_tiles(m):
    return (1024, 512) if m % 1024 == 0 else (m, m)


def _dense_front(x2d, wts, tm):
    pa = norm_matmul(x2d, wts['ln1_w'], wts['wt_att'], tm=tm, tn=ATT_PROJ // 2, w_transposed=True)
    pr = norm_matmul(x2d, wts['ln1_w'], wts['wt_rw'], tm=tm, tn=RWKV_PROJ_PAD // 3,
                     w_transposed=True)
    return pa, pr


def _dense_back(x2d, att2d, rw2d, wts, xattn_fn, tm):
    h1 = matmul_residual([att2d, rw2d], [wts['w_out_a'], wts['w_out_r']], x2d, tm=tm, tn=1024)
    qx = norm_matmul(h1, wts['ln2_w'], wts['xq_w'], tm=tm, tn=XATT_WIDTH)
    ox = xattn_fn(qx)
    return matmul_residual([ox], [wts['xo_w']], h1, tm=tm, tn=1024)


def kernel(x_prompt, x_sample, cache_win_k, cache_win_v, state_wkv, state_shift, cache_mem_k, cache_mem_v, mem_prompt, ln1_w, w_in, q_norm_w, k_norm_w, attn_sinks, rw_mu, rw_w0, rw_w2, rw_a0, rw_a2, rw_g2, rw_k_k, rw_k_a, rw_r_k, rw_ln_w, rw_ln_b, w_out, ln2_w, mem_norm_w, xq_w, xkv_w, xq_norm_w, xk_norm_w, xo_w, ln3_w, router_group_w, router_group_b, router_expert_w, router_expert_b, exp_w_gate, exp_w_up, exp_w_down):
    assert w_in.shape[0] == 1, "single-layer stack"
    bp, seq, d = x_prompt.shape
    bs = x_sample.shape[0]
    mp = bp * seq
    c = RWKV_WIDTH

    router_w = jnp.concatenate(
        [router_group_w[0], router_expert_w[0],
         jnp.zeros((d, ROUTER_LANES - N_EXPERT_GROUPS - N_EXPERTS), F32)], axis=1)
    router_hi = router_w.astype(BF16)
    wts = {
        'ln1_w': ln1_w[0], 'ln2_w': ln2_w[0], 'ln3_w': ln3_w[0],
        'wt_att': cast_rows_bf16(w_in[0].T, 0, ATT_PROJ, ATT_PROJ),
        'wt_rw': cast_rows_bf16(w_in[0].T, ATT_PROJ, RWKV_PROJ, RWKV_PROJ_PAD),
        'w_out_a': w_out[0][:ATT_WIDTH].astype(BF16),
        'w_out_r': w_out[0][ATT_WIDTH:].astype(BF16),
        'xq_w': xq_w[0].astype(BF16), 'xo_w': xo_w[0].astype(BF16),
        'router_hi': router_hi,
        'router_lo': (router_w - router_hi.astype(F32)).astype(BF16),
        'router_b': jnp.pad(jnp.concatenate([router_group_b[0], router_expert_b[0]]),
                            (0, ROUTER_LANES - N_EXPERT_GROUPS - N_EXPERTS)).reshape(1, -1),
    }
    rw_par = rwkv_params(rw_mu[0], rw_w0[0], rw_w2[0], rw_a0[0], rw_a2[0], rw_g2[0],
                         rw_k_k[0], rw_k_a[0], rw_r_k[0])

    tm_p, te_p = _token_tiles(mp)
    xp = x_prompt.reshape(mp, d)
    pa, pr = _dense_front(xp, wts, tm_p)
    pa3 = pa.reshape(bp, seq, ATT_PROJ)
    pr3 = pr.reshape(bp, seq, RWKV_PROJ_PAD)
    tabs_p = rope_tables(np.arange(seq))
    att_p, kn_p = swa_prompt(pa3, tabs_p, q_norm_w[0], k_norm_w[0], attn_sinks[0])
    rw_p, wkv_p = rwkv_seq(pr3, jnp.zeros((bp, 1, RWKV_PROJ_PAD), F32), rw_par,
                           rw_ln_w[0], rw_ln_b[0])
    rw_p = rw_p.reshape(mp, c)

    n_mem = mem_prompt.shape[1]
    kv_mem = norm_matmul(mem_prompt.reshape(bp * n_mem, d), mem_norm_w[0],
                         xkv_w[0].astype(BF16), tm=bp * n_mem, tn=512)
    mem_k = head_rms(kv_mem[:, :XATT_WIDTH], xk_norm_w[0])
    mem_v = kv_mem[:, XATT_WIDTH:]
    mem_k3 = mem_k.reshape(bp, n_mem, XATT_WIDTH)
    mem_v3 = mem_v.reshape(bp, n_mem, XATT_WIDTH)

    def xattn_p(qx):
        return xattn_prompt(qx.reshape(bp, seq, XATT_WIDTH), mem_k3, mem_v3,
                            xq_norm_w[0]).reshape(mp, XATT_WIDTH)

    h2_p = _dense_back(xp, att_p.reshape(mp, ATT_WIDTH), rw_p, wts, xattn_p, tm_p)

    tm_s, te_s = _token_tiles(bs)
    xs = x_sample.reshape(bs, d)
    sa, sr = _dense_front(xs, wts, tm_s)
    tabs_s = rope_tables(PAST_LEN + np.arange(1))
    qk_w = jnp.concatenate([jnp.tile(q_norm_w[0], ATT_HEADS),
                            jnp.tile(k_norm_w[0], ATT_KV_HEADS)]).reshape(1, -1)
    qk = qk_norm_rope(sa[:, :ATT_WIDTH + KV_WIDTH], qk_w, tabs_s)
    nbuf = cache_win_k.shape[2]

    def feature_major(cache):
        return jnp.transpose(cache, (0, 1, 3, 4, 2)).reshape(bs, KV_WIDTH, nbuf)

    def position_major(win):
        return jnp.transpose(win.reshape(1, bs, ATT_KV_HEADS, HEAD_DIM, nbuf), (0, 1, 4, 2, 3))

    att_s, win_k, win_v = swa_decode(
        qk[:, :ATT_WIDTH].reshape(bs, ATT_HEADS, HEAD_DIM),
        qk[:, ATT_WIDTH:].reshape(bs, 1, KV_WIDTH),
        sa[:, ATT_WIDTH + KV_WIDTH:].reshape(bs, 1, KV_WIDTH),
        feature_major(cache_win_k), feature_major(cache_win_v), attn_sinks[0])
    shift_prev = jnp.pad(state_shift[0], ((0, 0), (0, RWKV_PROJ_PAD - RWKV_PROJ)))
    r, ld, kp, v, kk, b, bonus, g = rwkv_prep_tok(sr, shift_prev, rw_par)
    y_s, wkv_s = rwkv_step(r, ld, kp, v, kk, b, jnp.transpose(state_wkv, (0, 2, 3, 4, 1)))
    wkv_s = jnp.transpose(wkv_s, (0, 4, 1, 2, 3))
    rw_s = rwkv_post(y_s, bonus, g, rw_ln_w[0], rw_ln_b[0], tm=te_s, y_channel_major=True)
    def xattn_s(qx):
        q_pad = jnp.pad(qx.reshape(bs, XATT_HEADS, XATT_HEAD_DIM), ((0, 0), (0, 4), (0, 0)))
        rows_of = lambda c: c.reshape(bs, n_mem * XATT_HEADS, XATT_HEAD_DIM)
        o = xattn_decode(q_pad, rows_of(cache_mem_k), rows_of(cache_mem_v), xq_norm_w[0])
        return o[:, :XATT_HEADS].reshape(bs, XATT_WIDTH)

    h2_s = _dense_back(xs, att_s.reshape(bs, ATT_WIDTH), rw_s, wts, xattn_s, tm_s)

    m_all = mp + bs
    slot_stride = m_all
    tc = math.gcd(mp, bs, 512)
    tr = te_p
    h2_s_pad = jnp.pad(h2_s, ((0, -bs % tr), (0, 0)))
    u_all, idx_all, gate_all = moe_router(h2_p, h2_s_pad, wts['ln3_w'], wts['router_hi'],
                                          wts['router_lo'], wts['router_b'], tm=tr)
    row_src, row_dst, block_run, run_exp, n_used_runs = moe_dispatch(
        idx_all[:m_all, :TOP_K], slot_stride)
    y_slots = moe_experts(u_all, row_src, row_dst, block_run, run_exp, n_used_runs,
                          exp_w_gate[0], exp_w_up[0], exp_w_down[0], TOP_K * slot_stride)
    out_p = moe_combine(h2_p, y_slots, gate_all, 0, slot_stride, tm=tc)
    out_s = moe_combine(h2_s, y_slots, gate_all, mp, slot_stride, tm=tc)

    win = min(WINDOW, seq)
    kv_shape = (1, bp, win, ATT_KV_HEADS, HEAD_DIM)
    return (
        out_p.reshape(bp, seq, d),
        out_s.reshape(bs, 1, d),
        kn_p[:, seq - win:].reshape(kv_shape),
        pa3[:, seq - win:, ATT_WIDTH + KV_WIDTH:].reshape(kv_shape),
        wkv_p[None],
        pr3[:, seq - 1, :RWKV_PROJ][None],
        mem_k3.reshape(1, bp, n_mem, XATT_HEADS, XATT_HEAD_DIM),
        mem_v3.reshape(1, bp, n_mem, XATT_HEADS, XATT_HEAD_DIM),
        position_major(win_k),
        position_major(win_v),
        wkv_s,
        sr[:, :RWKV_PROJ].reshape(1, bs, RWKV_PROJ),
    )
```
